```python
import math
import jax, jax.numpy as jnp
from jax import lax
import numpy as np

D_MODEL = 1024
BATCH = 8
SEQ = 8192
DEPTH = 1

N_META = 16
GRID_W = 64
SSM_GROUP = 16
SSM_STATE = 64
SSM_WIDTH = D_MODEL // 2
SSM_GROUPS = SSM_WIDTH // SSM_GROUP
HEAD_DIM = 64
N_HEADS = D_MODEL // HEAD_DIM
N_KV_HEADS = N_HEADS // 4
Q_WIDTH = N_HEADS * HEAD_DIM
KV_WIDTH = N_KV_HEADS * HEAD_DIM
IN_WIDTH = SSM_WIDTH + Q_WIDTH + 2 * KV_WIDTH + 2 * D_MODEL
D_FF = 4 * D_MODEL
Q_BLOCK = 128
ROPE_THETA = 10000.0
NORM_EPS = 1e-6
DT_MIN = 1e-3
DT_MAX = 1e-1
EIG_RE_MAX = -1e-4

kernel_name = "hybrid_s5_gqa_axial_gated_encoder"


def rms_norm(x, g):
    xf = x.astype(jnp.float32)
    y = xf * lax.rsqrt(jnp.mean(xf * xf, axis=-1, keepdims=True) + NORM_EPS)
    return (y * g.astype(jnp.float32)).astype(x.dtype)


def axial_rope_tables(n_total):
    n_real = n_total - N_META
    rows = n_real // GRID_W
    row_id = jnp.repeat(jnp.arange(rows, dtype=jnp.float32), GRID_W)
    col_id = jnp.tile(jnp.arange(GRID_W, dtype=jnp.float32), rows)
    pairs_per_axis = HEAD_DIM // 4
    inv_freq = ROPE_THETA ** (-jnp.arange(pairs_per_axis, dtype=jnp.float32) / pairs_per_axis)
    ang = jnp.concatenate([row_id[:, None] * inv_freq, col_id[:, None] * inv_freq], axis=-1)
    ang = jnp.concatenate([jnp.zeros((N_META, HEAD_DIM // 2), jnp.float32), ang], axis=0)
    return jnp.cos(ang), jnp.sin(ang)


def apply_rope(t, cos, sin):
    tf = t.astype(jnp.float32).reshape(t.shape[:-1] + (HEAD_DIM // 2, 2))
    t0, t1 = tf[..., 0], tf[..., 1]
    c = cos[:, None, :]
    s = sin[:, None, :]
    out = jnp.stack([t0 * c - t1 * s, t0 * s + t1 * c], axis=-1)
    return out.reshape(t.shape)


def gqa_attention(q, k, v):
    b, l = q.shape[0], q.shape[1]
    rep = N_HEADS // N_KV_HEADS
    scale = HEAD_DIM ** -0.5
    qg = q.reshape(b, l, N_KV_HEADS, rep, HEAD_DIM)

    def attend(qb):
        s = jnp.einsum('bqgrd,bkgd->bgrqk', qb, k) * scale
        p = jax.nn.softmax(s, axis=-1)
        return jnp.einsum('bgrqk,bkgd->bqgrd', p, v)

    out_meta = attend(qg[:, :N_META]).reshape(b, N_META, Q_WIDTH)
    n_real = l - N_META
    n_blk = n_real // Q_BLOCK
    q_blocks = qg[:, N_META:].reshape(b, n_blk, Q_BLOCK, N_KV_HEADS, rep, HEAD_DIM)
    q_blocks = jnp.transpose(q_blocks, (1, 0, 2, 3, 4, 5))
    out_real = lax.map(attend, q_blocks)
    out_real = jnp.transpose(out_real, (1, 0, 2, 3, 4, 5)).reshape(b, n_real, Q_WIDTH)
    return jnp.concatenate([out_meta, out_real], axis=1)


def _complex_scan_combine(e1, e2):
    a1r, a1i, b1r, b1i = e1
    a2r, a2i, b2r, b2i = e2
    return (a1r * a2r - a1i * a2i,
            a1r * a2i + a1i * a2r,
            a2r * b1r - a2i * b1i + b2r,
            a2r * b1i + a2i * b1r + b2i)


def s5_direction(uf, a_re, a_im, log_dt, b_re, b_im, c_re, c_im, reverse):
    l = uf.shape[1]
    f32 = jnp.float32
    dt = jnp.exp(log_dt.astype(f32))[:, None]
    lam_re = jnp.minimum(a_re.astype(f32), EIG_RE_MAX)
    lam_im = a_im.astype(f32)
    mag = jnp.exp(lam_re * dt)
    ang = lam_im * dt
    lb_re = mag * jnp.cos(ang)
    lb_im = mag * jnp.sin(ang)
    num_re = lb_re - 1.0
    num_im = lb_im
    den = lam_re * lam_re + lam_im * lam_im
    f_re = (num_re * lam_re + num_im * lam_im) / den
    f_im = (num_im * lam_re - num_re * lam_im) / den
    br = b_re.astype(f32)
    bi = b_im.astype(f32)
    bb_re = f_re[..., None] * br - f_im[..., None] * bi
    bb_im = f_re[..., None] * bi + f_im[..., None] * br
    bu_re = jnp.einsum('blgp,gnp->blgn', uf, bb_re)
    bu_im = jnp.einsum('blgp,gnp->blgn', uf, bb_im)
    shape_a = (1, l) + lb_re.shape
    a_seq_re = jnp.broadcast_to(lb_re, shape_a)
    a_seq_im = jnp.broadcast_to(lb_im, shape_a)
    _, _, x_re, x_im = lax.associative_scan(
        _complex_scan_combine, (a_seq_re, a_seq_im, bu_re, bu_im), reverse=reverse, axis=1)
    return (jnp.einsum('blgn,gpn->blgp', x_re, c_re.astype(f32))
            - jnp.einsum('blgn,gpn->blgp', x_im, c_im.astype(f32)))


def s5_bidirectional(u, a_re, a_im, log_dt, b_re, b_im, c_re, c_im, d):
    bsz, l = u.shape[0], u.shape[1]
    uf = u.astype(jnp.float32).reshape(bsz, l, SSM_GROUPS, SSM_GROUP)
    y = uf * d.astype(jnp.float32).reshape(SSM_GROUPS, SSM_GROUP)
    for direction in range(2):
        y = y + s5_direction(uf, a_re[direction], a_im[direction], log_dt[direction],
                             b_re[direction], b_im[direction], c_re[direction], c_im[direction],
                             reverse=(direction == 1))
    return y.reshape(bsz, l, SSM_WIDTH)


def _fwd_setup_inputs(seed: int = 0) -> dict:
    key = jax.random.key(seed)
    ks = jax.random.split(key, 24)
    f32 = jnp.float32

    def nrm(k, shape, scale):
        return jax.random.normal(k, shape, f32) * scale

    def gain(k, shape):
        return 1.0 + 0.02 * jax.random.normal(k, shape, f32)

    n_idx = jnp.arange(SSM_STATE, dtype=f32)
    ssm_shape = (DEPTH, 2, SSM_GROUPS, SSM_STATE)
    return {
        "x": jax.random.normal(ks[0], (BATCH, SEQ, D_MODEL), f32),
        "meta_tokens": nrm(ks[1], (N_META, D_MODEL), 1.0),
        "norm_mix_g": gain(ks[2], (DEPTH, D_MODEL)),
        "w_in": nrm(ks[3], (DEPTH, D_MODEL, IN_WIDTH), D_MODEL ** -0.5),
        "ssm_a_re": -0.5 + 0.01 * jax.random.normal(ks[4], ssm_shape, f32),
        "ssm_a_im": jnp.pi * n_idx + 0.01 * jax.random.normal(ks[5], ssm_shape, f32),
        "ssm_log_dt": jax.random.uniform(ks[6], (DEPTH, 2, SSM_GROUPS), f32,
                                         minval=math.log(DT_MIN), maxval=math.log(DT_MAX)),
        "ssm_b_re": nrm(ks[7], (DEPTH, 2, SSM_GROUPS, SSM_STATE, SSM_GROUP), (2 * SSM_GROUP) ** -0.5),
        "ssm_b_im": nrm(ks[8], (DEPTH, 2, SSM_GROUPS, SSM_STATE, SSM_GROUP), (2 * SSM_GROUP) ** -0.5),
        "ssm_c_re": nrm(ks[9], (DEPTH, 2, SSM_GROUPS, SSM_GROUP, SSM_STATE), SSM_STATE ** -0.5),
        "ssm_c_im": nrm(ks[10], (DEPTH, 2, SSM_GROUPS, SSM_GROUP, SSM_STATE), SSM_STATE ** -0.5),
        "ssm_d": nrm(ks[11], (DEPTH, SSM_WIDTH), 1.0),
        "w_glu": nrm(ks[12], (DEPTH, SSM_WIDTH, SSM_WIDTH), SSM_WIDTH ** -0.5),
        "b_glu": nrm(ks[13], (DEPTH, SSM_WIDTH), 0.02),
        "q_norm_g": gain(ks[14], (DEPTH, HEAD_DIM)),
        "k_norm_g": gain(ks[15], (DEPTH, HEAD_DIM)),
        "w_ssm_proj": nrm(ks[16], (DEPTH, SSM_WIDTH, D_MODEL), SSM_WIDTH ** -0.5),
        "w_attn_proj": nrm(ks[17], (DEPTH, Q_WIDTH, D_MODEL), Q_WIDTH ** -0.5),
        "w_out": nrm(ks[18], (DEPTH, D_MODEL, D_MODEL), D_MODEL ** -0.5),
        "norm_mlp_g": gain(ks[19], (DEPTH, D_MODEL)),
        "w_mlp_in": nrm(ks[20], (DEPTH, D_MODEL, D_FF), D_MODEL ** -0.5),
        "w_mlp_out": nrm(ks[21], (DEPTH, D_FF, D_MODEL), D_FF ** -0.5),
        "norm_final_g": gain(ks[22], (D_MODEL,)),
    }


def _fwd_reference(x, meta_tokens, norm_mix_g, w_in, ssm_a_re, ssm_a_im, ssm_log_dt,
              ssm_b_re, ssm_b_im, ssm_c_re, ssm_c_im, ssm_d, w_glu, b_glu,
              q_norm_g, k_norm_g, w_ssm_proj, w_attn_proj, w_out,
              norm_mlp_g, w_mlp_in, w_mlp_out, norm_final_g):
    dtype = x.dtype
    bsz = x.shape[0]
    meta = jnp.broadcast_to(meta_tokens.astype(dtype)[None], (bsz, N_META, D_MODEL))
    h_res = jnp.concatenate([meta, x], axis=1)
    l = h_res.shape[1]
    cos, sin = axial_rope_tables(l)
    split_at = [SSM_WIDTH, SSM_WIDTH + Q_WIDTH, SSM_WIDTH + Q_WIDTH + KV_WIDTH,
                SSM_WIDTH + Q_WIDTH + 2 * KV_WIDTH, SSM_WIDTH + Q_WIDTH + 2 * KV_WIDTH + D_MODEL]

    for i in range(DEPTH):
        h = rms_norm(h_res, norm_mix_g[i])
        proj = h @ w_in[i]
        u, q, k, v, g_ssm, g_attn = jnp.split(proj, split_at, axis=-1)

        y = s5_bidirectional(u, ssm_a_re[i], ssm_a_im[i], ssm_log_dt[i], ssm_b_re[i], ssm_b_im[i],
                             ssm_c_re[i], ssm_c_im[i], ssm_d[i])
        z = jax.nn.gelu(y, approximate=False)
        y_ssm = z * jax.nn.sigmoid(z @ w_glu[i].astype(jnp.float32) + b_glu[i].astype(jnp.float32))

        q = rms_norm(q.reshape(bsz, l, N_HEADS, HEAD_DIM), q_norm_g[i])
        k = rms_norm(k.reshape(bsz, l, N_KV_HEADS, HEAD_DIM), k_norm_g[i])
        q = apply_rope(q, cos, sin)
        k = apply_rope(k, cos, sin)
        v = v.reshape(bsz, l, N_KV_HEADS, HEAD_DIM).astype(jnp.float32)
        y_attn = gqa_attention(q, k, v)

        merged = (jax.nn.sigmoid(g_ssm.astype(jnp.float32)) * (y_ssm.astype(dtype) @ w_ssm_proj[i])
                  + jax.nn.sigmoid(g_attn.astype(jnp.float32)) * (y_attn.astype(dtype) @ w_attn_proj[i]))
        h_res = h_res + (merged.astype(dtype) @ w_out[i]).astype(dtype)

        h2 = rms_norm(h_res, norm_mlp_g[i])
        h_res = h_res + (jnp.square(jax.nn.relu(h2 @ w_mlp_in[i])) @ w_mlp_out[i]).astype(dtype)

    out = rms_norm(h_res, norm_final_g)
    return out[:, N_META:]


import jax as _jax
import jax.numpy as _jnp

TWIN_FORMAT = 'train_step'
FWD_PARAMS = ['x', 'meta_tokens', 'norm_mix_g', 'w_in', 'ssm_a_re', 'ssm_a_im', 'ssm_log_dt', 'ssm_b_re', 'ssm_b_im', 'ssm_c_re', 'ssm_c_im', 'ssm_d', 'w_glu', 'b_glu', 'q_norm_g', 'k_norm_g', 'w_ssm_proj', 'w_attn_proj', 'w_out', 'norm_mlp_g', 'w_mlp_in', 'w_mlp_out', 'norm_final_g']
TWIN_WEIGHTS = ['meta_tokens', 'norm_mix_g', 'w_in', 'ssm_a_re', 'ssm_a_im', 'ssm_log_dt', 'ssm_b_re', 'ssm_b_im', 'ssm_c_re', 'ssm_c_im', 'ssm_d', 'w_glu', 'b_glu', 'q_norm_g', 'k_norm_g', 'w_ssm_proj', 'w_attn_proj', 'w_out', 'norm_mlp_g', 'w_mlp_in', 'w_mlp_out', 'norm_final_g']
TWIN_DIFF_INPUT = 'x'
TWIN_INPUTS = ['x', 'meta_tokens', 'norm_mix_g', 'w_in', 'ssm_a_re', 'ssm_a_im', 'ssm_log_dt', 'ssm_b_re', 'ssm_b_im', 'ssm_c_re', 'ssm_c_im', 'ssm_d', 'w_glu', 'b_glu', 'q_norm_g', 'k_norm_g', 'w_ssm_proj', 'w_attn_proj', 'w_out', 'norm_mlp_g', 'w_mlp_in', 'w_mlp_out', 'norm_final_g', 'loss_target', 'm_meta_tokens', 'm_norm_mix_g', 'm_w_in', 'm_ssm_a_re', 'm_ssm_a_im', 'm_ssm_log_dt', 'm_ssm_b_re', 'm_ssm_b_im', 'm_ssm_c_re', 'm_ssm_c_im', 'm_ssm_d', 'm_w_glu', 'm_b_glu', 'm_q_norm_g', 'm_k_norm_g', 'm_w_ssm_proj', 'm_w_attn_proj', 'm_w_out', 'm_norm_mlp_g', 'm_w_mlp_in', 'm_w_mlp_out', 'm_norm_final_g', 'v_meta_tokens', 'v_norm_mix_g', 'v_w_in', 'v_ssm_a_re', 'v_ssm_a_im', 'v_ssm_log_dt', 'v_ssm_b_re', 'v_ssm_b_im', 'v_ssm_c_re', 'v_ssm_c_im', 'v_ssm_d', 'v_w_glu', 'v_b_glu', 'v_q_norm_g', 'v_k_norm_g', 'v_w_ssm_proj', 'v_w_attn_proj', 'v_w_out', 'v_norm_mlp_g', 'v_w_mlp_in', 'v_w_mlp_out', 'v_norm_final_g']
TWIN_OUTPUTS = ['loss', 'grad_x', 'grad_meta_tokens', 'grad_norm_mix_g', 'grad_w_in', 'grad_ssm_a_re', 'grad_ssm_a_im', 'grad_ssm_log_dt', 'grad_ssm_b_re', 'grad_ssm_b_im', 'grad_ssm_c_re', 'grad_ssm_c_im', 'grad_ssm_d', 'grad_w_glu', 'grad_b_glu', 'grad_q_norm_g', 'grad_k_norm_g', 'grad_w_ssm_proj', 'grad_w_attn_proj', 'grad_w_out', 'grad_norm_mlp_g', 'grad_w_mlp_in', 'grad_w_mlp_out', 'grad_norm_final_g', 'delta_meta_tokens', 'delta_norm_mix_g', 'delta_w_in', 'delta_ssm_a_re', 'delta_ssm_a_im', 'delta_ssm_log_dt', 'delta_ssm_b_re', 'delta_ssm_b_im', 'delta_ssm_c_re', 'delta_ssm_c_im', 'delta_ssm_d', 'delta_w_glu', 'delta_b_glu', 'delta_q_norm_g', 'delta_k_norm_g', 'delta_w_ssm_proj', 'delta_w_attn_proj', 'delta_w_out', 'delta_norm_mlp_g', 'delta_w_mlp_in', 'delta_w_mlp_out', 'delta_norm_final_g', 'new_m_meta_tokens', 'new_m_norm_mix_g', 'new_m_w_in', 'new_m_ssm_a_re', 'new_m_ssm_a_im', 'new_m_ssm_log_dt', 'new_m_ssm_b_re', 'new_m_ssm_b_im', 'new_m_ssm_c_re', 'new_m_ssm_c_im', 'new_m_ssm_d', 'new_m_w_glu', 'new_m_b_glu', 'new_m_q_norm_g', 'new_m_k_norm_g', 'new_m_w_ssm_proj', 'new_m_w_attn_proj', 'new_m_w_out', 'new_m_norm_mlp_g', 'new_m_w_mlp_in', 'new_m_w_mlp_out', 'new_m_norm_final_g', 'new_v_meta_tokens', 'new_v_norm_mix_g', 'new_v_w_in', 'new_v_ssm_a_re', 'new_v_ssm_a_im', 'new_v_ssm_log_dt', 'new_v_ssm_b_re', 'new_v_ssm_b_im', 'new_v_ssm_c_re', 'new_v_ssm_c_im', 'new_v_ssm_d', 'new_v_w_glu', 'new_v_b_glu', 'new_v_q_norm_g', 'new_v_k_norm_g', 'new_v_w_ssm_proj', 'new_v_w_attn_proj', 'new_v_w_out', 'new_v_norm_mlp_g', 'new_v_w_mlp_in', 'new_v_w_mlp_out', 'new_v_norm_final_g']
TWIN_LEAF_KINDS = {'loss': 'loss', 'grad_x': 'grad_x', 'grad_meta_tokens': 'grad_w', 'grad_norm_mix_g': 'grad_w', 'grad_w_in': 'grad_w', 'grad_ssm_a_re': 'grad_w', 'grad_ssm_a_im': 'grad_w', 'grad_ssm_log_dt': 'grad_w', 'grad_ssm_b_re': 'grad_w', 'grad_ssm_b_im': 'grad_w', 'grad_ssm_c_re': 'grad_w', 'grad_ssm_c_im': 'grad_w', 'grad_ssm_d': 'grad_w', 'grad_w_glu': 'grad_w', 'grad_b_glu': 'grad_w', 'grad_q_norm_g': 'grad_w', 'grad_k_norm_g': 'grad_w', 'grad_w_ssm_proj': 'grad_w', 'grad_w_attn_proj': 'grad_w', 'grad_w_out': 'grad_w', 'grad_norm_mlp_g': 'grad_w', 'grad_w_mlp_in': 'grad_w', 'grad_w_mlp_out': 'grad_w', 'grad_norm_final_g': 'grad_w', 'delta_meta_tokens': 'delta_w', 'delta_norm_mix_g': 'delta_w', 'delta_w_in': 'delta_w', 'delta_ssm_a_re': 'delta_w', 'delta_ssm_a_im': 'delta_w', 'delta_ssm_log_dt': 'delta_w', 'delta_ssm_b_re': 'delta_w', 'delta_ssm_b_im': 'delta_w', 'delta_ssm_c_re': 'delta_w', 'delta_ssm_c_im': 'delta_w', 'delta_ssm_d': 'delta_w', 'delta_w_glu': 'delta_w', 'delta_b_glu': 'delta_w', 'delta_q_norm_g': 'delta_w', 'delta_k_norm_g': 'delta_w', 'delta_w_ssm_proj': 'delta_w', 'delta_w_attn_proj': 'delta_w', 'delta_w_out': 'delta_w', 'delta_norm_mlp_g': 'delta_w', 'delta_w_mlp_in': 'delta_w', 'delta_w_mlp_out': 'delta_w', 'delta_norm_final_g': 'delta_w', 'new_m_meta_tokens': 'new_m', 'new_m_norm_mix_g': 'new_m', 'new_m_w_in': 'new_m', 'new_m_ssm_a_re': 'new_m', 'new_m_ssm_a_im': 'new_m', 'new_m_ssm_log_dt': 'new_m', 'new_m_ssm_b_re': 'new_m', 'new_m_ssm_b_im': 'new_m', 'new_m_ssm_c_re': 'new_m', 'new_m_ssm_c_im': 'new_m', 'new_m_ssm_d': 'new_m', 'new_m_w_glu': 'new_m', 'new_m_b_glu': 'new_m', 'new_m_q_norm_g': 'new_m', 'new_m_k_norm_g': 'new_m', 'new_m_w_ssm_proj': 'new_m', 'new_m_w_attn_proj': 'new_m', 'new_m_w_out': 'new_m', 'new_m_norm_mlp_g': 'new_m', 'new_m_w_mlp_in': 'new_m', 'new_m_w_mlp_out': 'new_m', 'new_m_norm_final_g': 'new_m', 'new_v_meta_tokens': 'new_v', 'new_v_norm_mix_g': 'new_v', 'new_v_w_in': 'new_v', 'new_v_ssm_a_re': 'new_v', 'new_v_ssm_a_im': 'new_v', 'new_v_ssm_log_dt': 'new_v', 'new_v_ssm_b_re': 'new_v', 'new_v_ssm_b_im': 'new_v', 'new_v_ssm_c_re': 'new_v', 'new_v_ssm_c_im': 'new_v', 'new_v_ssm_d': 'new_v', 'new_v_w_glu': 'new_v', 'new_v_b_glu': 'new_v', 'new_v_q_norm_g': 'new_v', 'new_v_k_norm_g': 'new_v', 'new_v_w_ssm_proj': 'new_v', 'new_v_w_attn_proj': 'new_v', 'new_v_w_out': 'new_v', 'new_v_norm_mlp_g': 'new_v', 'new_v_w_mlp_in': 'new_v', 'new_v_w_mlp_out': 'new_v', 'new_v_norm_final_g': 'new_v'}


def _forward(args):
    return _fwd_reference(*[args[k] for k in FWD_PARAMS])


def _output_shape():
    out = _jax.eval_shape(lambda: _forward(_fwd_setup_inputs(0)))
    return out.shape, out.dtype

N_MICROBATCH = 1
ADAM_LR = 0.001
ADAM_B1 = 0.9
ADAM_B2 = 0.999
ADAM_EPS = 1e-08
ADAM_WD = 0.01
ADAM_STEP = 10
PER_EXAMPLE_BATCH_AXIS = {'x': 0, 'loss_target': 0}
SHARED_INPUTS = []
_WEIGHT_DTYPES = {'meta_tokens': _jnp.float32, 'norm_mix_g': _jnp.float32, 'w_in': _jnp.float32, 'ssm_a_re': _jnp.float32, 'ssm_a_im': _jnp.float32, 'ssm_log_dt': _jnp.float32, 'ssm_b_re': _jnp.float32, 'ssm_b_im': _jnp.float32, 'ssm_c_re': _jnp.float32, 'ssm_c_im': _jnp.float32, 'ssm_d': _jnp.float32, 'w_glu': _jnp.float32, 'b_glu': _jnp.float32, 'q_norm_g': _jnp.float32, 'k_norm_g': _jnp.float32, 'w_ssm_proj': _jnp.float32, 'w_attn_proj': _jnp.float32, 'w_out': _jnp.float32, 'norm_mlp_g': _jnp.float32, 'w_mlp_in': _jnp.float32, 'w_mlp_out': _jnp.float32, 'norm_final_g': _jnp.float32}
MOMENT_SCALE = {'meta_tokens': 1.206159e-03, 'norm_mix_g': 6.726748e-02, 'w_in': 3.233414e-02, 'ssm_a_re': 6.608966e-03, 'ssm_a_im': 6.769656e-03, 'ssm_log_dt': 4.634228e+00, 'ssm_b_re': 3.717692e-03, 'ssm_b_im': 3.698300e-03, 'ssm_c_re': 5.242598e-03, 'ssm_c_im': 5.354800e-03, 'ssm_d': 9.618468e-02, 'w_glu': 2.426309e-02, 'b_glu': 3.934123e-02, 'q_norm_g': 5.379917e-02, 'k_norm_g': 5.187453e-02, 'w_ssm_proj': 5.846999e-02, 'w_attn_proj': 1.480516e-02, 'w_out': 5.428878e-02, 'norm_mlp_g': 2.312170e-01, 'w_mlp_in': 1.112597e-01, 'w_mlp_out': 2.193305e-01, 'norm_final_g': 6.465106e+01}


def _to_microbatches(a, axis):
    t = _jnp.moveaxis(a, axis, 0)
    t = t.reshape((N_MICROBATCH, t.shape[0] // N_MICROBATCH) + t.shape[1:])
    return _jnp.moveaxis(t, 1, axis + 1)


def setup_inputs(seed: int = 0) -> dict:
    inp = _fwd_setup_inputs(seed)
    key = _jax.random.fold_in(_jax.random.key(seed), 7919)
    shape, _ = _output_shape()
    out = dict(inp)
    out["loss_target"] = _jax.random.normal(_jax.random.fold_in(key, 0), shape, _jnp.float32)
    for i, name in enumerate(TWIN_WEIGHTS):
        w = inp[name].astype(_jnp.float32)
        if MOMENT_SCALE is None:
            s = _jnp.sqrt(_jnp.mean(_jnp.square(w)) + 1e-30)
        else:
            s = MOMENT_SCALE[name]
        km, kv = _jax.random.split(_jax.random.fold_in(key, i + 1))
        out[name] = w
        out["m_" + name] = s * _jax.random.normal(km, w.shape, _jnp.float32)
        out["v_" + name] = (s * s) * _jax.random.uniform(kv, w.shape, _jnp.float32, 0.5, 1.5)
    if N_MICROBATCH > 1:
        for name, axis in PER_EXAMPLE_BATCH_AXIS.items():
            out[name] = _to_microbatches(out[name], axis)
    return {'x': out['x'], 'meta_tokens': out['meta_tokens'], 'norm_mix_g': out['norm_mix_g'], 'w_in': out['w_in'], 'ssm_a_re': out['ssm_a_re'], 'ssm_a_im': out['ssm_a_im'], 'ssm_log_dt': out['ssm_log_dt'], 'ssm_b_re': out['ssm_b_re'], 'ssm_b_im': out['ssm_b_im'], 'ssm_c_re': out['ssm_c_re'], 'ssm_c_im': out['ssm_c_im'], 'ssm_d': out['ssm_d'], 'w_glu': out['w_glu'], 'b_glu': out['b_glu'], 'q_norm_g': out['q_norm_g'], 'k_norm_g': out['k_norm_g'], 'w_ssm_proj': out['w_ssm_proj'], 'w_attn_proj': out['w_attn_proj'], 'w_out': out['w_out'], 'norm_mlp_g': out['norm_mlp_g'], 'w_mlp_in': out['w_mlp_in'], 'w_mlp_out': out['w_mlp_out'], 'norm_final_g': out['norm_final_g'], 'loss_target': out['loss_target'], 'm_meta_tokens': out['m_meta_tokens'], 'm_norm_mix_g': out['m_norm_mix_g'], 'm_w_in': out['m_w_in'], 'm_ssm_a_re': out['m_ssm_a_re'], 'm_ssm_a_im': out['m_ssm_a_im'], 'm_ssm_log_dt': out['m_ssm_log_dt'], 'm_ssm_b_re': out['m_ssm_b_re'], 'm_ssm_b_im': out['m_ssm_b_im'], 'm_ssm_c_re': out['m_ssm_c_re'], 'm_ssm_c_im': out['m_ssm_c_im'], 'm_ssm_d': out['m_ssm_d'], 'm_w_glu': out['m_w_glu'], 'm_b_glu': out['m_b_glu'], 'm_q_norm_g': out['m_q_norm_g'], 'm_k_norm_g': out['m_k_norm_g'], 'm_w_ssm_proj': out['m_w_ssm_proj'], 'm_w_attn_proj': out['m_w_attn_proj'], 'm_w_out': out['m_w_out'], 'm_norm_mlp_g': out['m_norm_mlp_g'], 'm_w_mlp_in': out['m_w_mlp_in'], 'm_w_mlp_out': out['m_w_mlp_out'], 'm_norm_final_g': out['m_norm_final_g'], 'v_meta_tokens': out['v_meta_tokens'], 'v_norm_mix_g': out['v_norm_mix_g'], 'v_w_in': out['v_w_in'], 'v_ssm_a_re': out['v_ssm_a_re'], 'v_ssm_a_im': out['v_ssm_a_im'], 'v_ssm_log_dt': out['v_ssm_log_dt'], 'v_ssm_b_re': out['v_ssm_b_re'], 'v_ssm_b_im': out['v_ssm_b_im'], 'v_ssm_c_re': out['v_ssm_c_re'], 'v_ssm_c_im': out['v_ssm_c_im'], 'v_ssm_d': out['v_ssm_d'], 'v_w_glu': out['v_w_glu'], 'v_b_glu': out['v_b_glu'], 'v_q_norm_g': out['v_q_norm_g'], 'v_k_norm_g': out['v_k_norm_g'], 'v_w_ssm_proj': out['v_w_ssm_proj'], 'v_w_attn_proj': out['v_w_attn_proj'], 'v_w_out': out['v_w_out'], 'v_norm_mlp_g': out['v_norm_mlp_g'], 'v_w_mlp_in': out['v_w_mlp_in'], 'v_w_mlp_out': out['v_w_mlp_out'], 'v_norm_final_g': out['v_norm_final_g']}


def _loss(weights, diff, rest, loss_target):
    with _jax.named_scope("forward"):
        args = {**rest, TWIN_DIFF_INPUT: diff, **{k: w.astype(_WEIGHT_DTYPES[k]) for k, w in weights.items()}}
        y = _forward(args)
    with _jax.named_scope("loss_head"):
        err = _jnp.square(y.astype(_jnp.float32) - loss_target)
        return 0.5 * _jnp.sum(_jnp.mean(err, axis=-1)) if err.ndim else 0.5 * err


def _adamw(w, g, m, v):
    m = ADAM_B1 * m + (1.0 - ADAM_B1) * g
    v = ADAM_B2 * v + (1.0 - ADAM_B2) * _jnp.square(g)
    m_hat = m / (1.0 - ADAM_B1 ** ADAM_STEP)
    v_hat = v / (1.0 - ADAM_B2 ** ADAM_STEP)
    delta = -ADAM_LR * (m_hat / (_jnp.sqrt(v_hat) + ADAM_EPS) + ADAM_WD * w)
    return delta, m, v


def reference(x, meta_tokens, norm_mix_g, w_in, ssm_a_re, ssm_a_im, ssm_log_dt, ssm_b_re, ssm_b_im, ssm_c_re, ssm_c_im, ssm_d, w_glu, b_glu, q_norm_g, k_norm_g, w_ssm_proj, w_attn_proj, w_out, norm_mlp_g, w_mlp_in, w_mlp_out, norm_final_g, loss_target, m_meta_tokens, m_norm_mix_g, m_w_in, m_ssm_a_re, m_ssm_a_im, m_ssm_log_dt, m_ssm_b_re, m_ssm_b_im, m_ssm_c_re, m_ssm_c_im, m_ssm_d, m_w_glu, m_b_glu, m_q_norm_g, m_k_norm_g, m_w_ssm_proj, m_w_attn_proj, m_w_out, m_norm_mlp_g, m_w_mlp_in, m_w_mlp_out, m_norm_final_g, v_meta_tokens, v_norm_mix_g, v_w_in, v_ssm_a_re, v_ssm_a_im, v_ssm_log_dt, v_ssm_b_re, v_ssm_b_im, v_ssm_c_re, v_ssm_c_im, v_ssm_d, v_w_glu, v_b_glu, v_q_norm_g, v_k_norm_g, v_w_ssm_proj, v_w_attn_proj, v_w_out, v_norm_mlp_g, v_w_mlp_in, v_w_mlp_out, v_norm_final_g):
    given = dict(x=x, meta_tokens=meta_tokens, norm_mix_g=norm_mix_g, w_in=w_in, ssm_a_re=ssm_a_re, ssm_a_im=ssm_a_im, ssm_log_dt=ssm_log_dt, ssm_b_re=ssm_b_re, ssm_b_im=ssm_b_im, ssm_c_re=ssm_c_re, ssm_c_im=ssm_c_im, ssm_d=ssm_d, w_glu=w_glu, b_glu=b_glu, q_norm_g=q_norm_g, k_norm_g=k_norm_g, w_ssm_proj=w_ssm_proj, w_attn_proj=w_attn_proj, w_out=w_out, norm_mlp_g=norm_mlp_g, w_mlp_in=w_mlp_in, w_mlp_out=w_mlp_out, norm_final_g=norm_final_g, loss_target=loss_target, m_meta_tokens=m_meta_tokens, m_norm_mix_g=m_norm_mix_g, m_w_in=m_w_in, m_ssm_a_re=m_ssm_a_re, m_ssm_a_im=m_ssm_a_im, m_ssm_log_dt=m_ssm_log_dt, m_ssm_b_re=m_ssm_b_re, m_ssm_b_im=m_ssm_b_im, m_ssm_c_re=m_ssm_c_re, m_ssm_c_im=m_ssm_c_im, m_ssm_d=m_ssm_d, m_w_glu=m_w_glu, m_b_glu=m_b_glu, m_q_norm_g=m_q_norm_g, m_k_norm_g=m_k_norm_g, m_w_ssm_proj=m_w_ssm_proj, m_w_attn_proj=m_w_attn_proj, m_w_out=m_w_out, m_norm_mlp_g=m_norm_mlp_g, m_w_mlp_in=m_w_mlp_in, m_w_mlp_out=m_w_mlp_out, m_norm_final_g=m_norm_final_g, v_meta_tokens=v_meta_tokens, v_norm_mix_g=v_norm_mix_g, v_w_in=v_w_in, v_ssm_a_re=v_ssm_a_re, v_ssm_a_im=v_ssm_a_im, v_ssm_log_dt=v_ssm_log_dt, v_ssm_b_re=v_ssm_b_re, v_ssm_b_im=v_ssm_b_im, v_ssm_c_re=v_ssm_c_re, v_ssm_c_im=v_ssm_c_im, v_ssm_d=v_ssm_d, v_w_glu=v_w_glu, v_b_glu=v_b_glu, v_q_norm_g=v_q_norm_g, v_k_norm_g=v_k_norm_g, v_w_ssm_proj=v_w_ssm_proj, v_w_attn_proj=v_w_attn_proj, v_w_out=v_w_out, v_norm_mlp_g=v_norm_mlp_g, v_w_mlp_in=v_w_mlp_in, v_w_mlp_out=v_w_mlp_out, v_norm_final_g=v_norm_final_g)
    weights = {n: given[n] for n in TWIN_WEIGHTS}
    shared = {n: given[n] for n in SHARED_INPUTS}
    per_example = {n: given[n] for n in ['x']}
    grad_fn = _jax.value_and_grad(_loss, argnums=(0, 1))

    def one_microbatch(ex, loss_target):
        ex = dict(ex)
        diff = ex.pop(TWIN_DIFF_INPUT)
        return grad_fn(weights, diff, {**shared, **ex}, loss_target)

    if N_MICROBATCH == 1:
        loss, (grad_w, grad_x) = one_microbatch(per_example, given["loss_target"])
    else:
        def body(carry, xs):
            loss_sum, grad_sum = carry
            l_k, (gw_k, gx_k) = one_microbatch(xs[0], xs[1])
            with _jax.named_scope("update"):
                return (loss_sum + l_k, _jax.tree.map(_jnp.add, grad_sum, gw_k)), gx_k

        init = (_jnp.zeros((), _jnp.float32), _jax.tree.map(_jnp.zeros_like, weights))
        (loss, grad_w), grad_x = _jax.lax.scan(body, init, (per_example, given["loss_target"]))
    with _jax.named_scope("update"):
        delta_w, new_m, new_v = {}, {}, {}
        for n in TWIN_WEIGHTS:
            delta_w[n], new_m[n], new_v[n] = _adamw(weights[n], grad_w[n], given["m_" + n], given["v_" + n])
    return (loss, grad_x, *[grad_w[n] for n in TWIN_WEIGHTS], *[delta_w[n] for n in TWIN_WEIGHTS],
            *[new_m[n] for n in TWIN_WEIGHTS], *[new_v[n] for n in TWIN_WEIGHTS])
```

```python
import functools
import math

import numpy as np
import jax
import jax.numpy as jnp
from jax import lax
from jax.experimental import pallas as pl
from jax.experimental.pallas import tpu as pltpu

F32 = jnp.float32
BF16 = jnp.bfloat16

N_META = 16
GRID_W = 64
HEAD_DIM = 64
GQA_REP = 4
SSM_GROUP = 16
SSM_STATE = 64
ROPE_THETA = 10000.0
NORM_EPS = 1e-6
EIG_RE_MAX = -1e-4
ADAM_LR, ADAM_B1, ADAM_B2, ADAM_EPS, ADAM_WD, ADAM_STEP = 0.001, 0.9, 0.999, 1e-08, 0.01, 10

SUBLANES = 8
LANES = 128
CHUNK = 128
KSTEPS = CHUNK // SUBLANES
SCAN_LANES = 512
VMEM_LIMIT = 56 << 20
MASK_VALUE = -1e30
MESH_ID = pl.DeviceIdType.MESH


def _dot(a, b):
    return jnp.dot(a, b, preferred_element_type=F32)


def _dot_nt(a, b):
    return lax.dot_general(a, b, (((1,), (1,)), ((), ())), preferred_element_type=F32)


def _dot_tn(a, b):
    return lax.dot_general(a, b, (((0,), (0,)), ((), ())), preferred_element_type=F32)


def _row(tm, width):
    return pl.BlockSpec((tm, width), lambda i: (i, 0))


def _full(shape):
    nd = len(shape)
    return pl.BlockSpec(shape, lambda i: (0,) * nd)


def _params(sem):
    return pltpu.CompilerParams(dimension_semantics=sem, vmem_limit_bytes=VMEM_LIMIT)


def _pick_tile(n, cap, mult=16):
    best = None
    for t in range(mult, min(n, cap) + 1, mult):
        if n % t == 0:
            best = t
    assert best is not None, (n, cap)
    return best


def _rstd(x):
    return lax.rsqrt(jnp.mean(x * x, axis=-1, keepdims=True) + NORM_EPS)


def _rms(x, g):
    return x * _rstd(x) * g


def _rms_bwd(dy, x, g):
    r = _rstd(x)
    xh = x * r
    gdy = dy * g
    dx = r * (gdy - xh * jnp.mean(gdy * xh, axis=-1, keepdims=True))
    return dx, dy * xh


def _split_dot(x, m):
    hi = x.astype(BF16)
    lo = (x - hi.astype(F32)).astype(BF16)
    return _dot(hi, m) + _dot(lo, m)


def _sigmoid(x):
    return 1.0 / (1.0 + jnp.exp(-x))


def _acc_rows(ref, val, first):
    s = jnp.sum(val, axis=0, keepdims=True)

    @pl.when(first)
    def _():
        ref[...] = s

    @pl.when(jnp.logical_not(first))
    def _():
        ref[...] += s


def _in_proj(xin, g, w4, tm):
    lp, d = xin.shape
    hd = d // 2

    def body(x_ref, g_ref, w_ref, u_ref, qkv_ref, gt_ref):
        h = _rms(x_ref[...], g_ref[...]).astype(BF16)
        p0 = _dot(h, w_ref[0])
        u_ref[...] = p0[:, :hd]
        qkv_ref[:, :hd] = p0[:, hd:]
        qkv_ref[:, hd:] = _dot(h, w_ref[1])
        gt_ref[:, :d] = _dot(h, w_ref[2])
        gt_ref[:, d:] = _dot(h, w_ref[3])

    return pl.pallas_call(
        body, name="in_proj", grid=(lp // tm,),
        in_specs=[_row(tm, d), _full((1, d)), _full((4, d, d))],
        out_specs=[_row(tm, hd), _row(tm, 3 * hd), _row(tm, 2 * d)],
        out_shape=[jax.ShapeDtypeStruct((lp, hd), F32), jax.ShapeDtypeStruct((lp, 3 * hd), F32),
                   jax.ShapeDtypeStruct((lp, 2 * d), F32)],
        compiler_params=_params(("parallel",)),
    )(xin, g, w4)


def _gelu(y):
    return 0.5 * y * (1.0 + lax.erf(y * (1.0 / math.sqrt(2.0))))


def _gelu_grad(y):
    return 0.5 * (1.0 + lax.erf(y * (1.0 / math.sqrt(2.0)))) + y * jnp.exp(-0.5 * y * y) * (1.0 / math.sqrt(2.0 * math.pi))


def _glu_fwd(u, y0, y1, dskip, w_glu, b_glu, tm):
    lp, w = u.shape

    def body(u_ref, y0_ref, y1_ref, d_ref, w_ref, b_ref, o_ref):
        y = u_ref[...] * d_ref[...] + y0_ref[...] + y1_ref[...]
        z = _gelu(y)
        t = _dot(z.astype(BF16), w_ref[...]) + b_ref[...]
        o_ref[...] = (z * _sigmoid(t)).astype(BF16)

    return pl.pallas_call(
        body, name="glu_fwd", grid=(lp // tm,),
        in_specs=[_row(tm, w), _row(tm, w), _row(tm, w), _full((1, w)), _full((w, w)), _full((1, w))],
        out_specs=_row(tm, w), out_shape=jax.ShapeDtypeStruct((lp, w), BF16),
        compiler_params=_params(("parallel",)),
    )(u, y0, y1, dskip, w_glu, b_glu)


def _glu_bwd(dyssm, u, y0, y1, dskip, w_glu, b_glu, tm):
    lp, w = u.shape

    def body(g_ref, u_ref, y0_ref, y1_ref, d_ref, w_ref, b_ref, dy_ref, dw_ref, db_ref, dd_ref):
        first = pl.program_id(0) == 0
        uu = u_ref[...]
        y = uu * d_ref[...] + y0_ref[...] + y1_ref[...]
        z = _gelu(y)
        zb = z.astype(BF16)
        sg = _sigmoid(_dot(zb, w_ref[...]) + b_ref[...])
        g = g_ref[...]
        dt = g * z * sg * (1.0 - sg)
        dtb = dt.astype(BF16)
        dz = g * sg + _dot_nt(dtb, w_ref[...])
        dy = dz * _gelu_grad(y)
        dy_ref[...] = dy
        dw = _dot_tn(zb, dtb)

        @pl.when(first)
        def _():
            dw_ref[...] = dw

        @pl.when(jnp.logical_not(first))
        def _():
            dw_ref[...] += dw

        _acc_rows(db_ref, dt, first)
        _acc_rows(dd_ref, dy * uu, first)

    return pl.pallas_call(
        body, name="glu_bwd", grid=(lp // tm,),
        in_specs=[_row(tm, w), _row(tm, w), _row(tm, w), _row(tm, w), _full((1, w)), _full((w, w)), _full((1, w))],
        out_specs=[_row(tm, w), _full((w, w)), _full((1, w)), _full((1, w))],
        out_shape=[jax.ShapeDtypeStruct((lp, w), F32), jax.ShapeDtypeStruct((w, w), F32),
                   jax.ShapeDtypeStruct((1, w), F32), jax.ShapeDtypeStruct((1, w), F32)],
        compiler_params=_params(("arbitrary",)),
    )(dyssm, u, y0, y1, dskip, w_glu, b_glu)


def _merge_fwd(yssm, o, gates, xin, wsp4, wap, wo, tm):
    lp, d = xin.shape
    w = yssm.shape[1]
    ns = d // 4

    def body(y_ref, o_ref, g_ref, x_ref, wsp_ref, wap_ref, wo_ref, h_ref, m_ref):
        yb = y_ref[...]
        ms = jnp.concatenate([_dot(yb, wsp_ref[s]) for s in range(4)], axis=1)
        ma = _dot(o_ref[...], wap_ref[...])
        merged = (_sigmoid(g_ref[:, :d]) * ms + _sigmoid(g_ref[:, d:]) * ma).astype(BF16)
        m_ref[...] = merged
        h_ref[...] = x_ref[...] + _dot(merged, wo_ref[...])

    return pl.pallas_call(
        body, name="merge_fwd", grid=(lp // tm,),
        in_specs=[_row(tm, w), _row(tm, d), _row(tm, 2 * d), _row(tm, d),
                  _full((4, w, ns)), _full((d, d)), _full((d, d))],
        out_specs=[_row(tm, d), _row(tm, d)],
        out_shape=[jax.ShapeDtypeStruct((lp, d), F32), jax.ShapeDtypeStruct((lp, d), BF16)],
        compiler_params=_params(("parallel",)),
    )(yssm, o, gates, xin, wsp4, wap, wo)


def _merge_bwd(dh1, yssm, o, gates, wsp4, wap, wo, sel, tm):
    lp, d = dh1.shape
    w = yssm.shape[1]
    ns = d // 4
    nsel = sel.shape[1]

    def body(dh_ref, y_ref, o_ref, g_ref, wsp_ref, wap_ref, wo_ref, sel_ref,
             dg_ref, dms_ref, dma_ref, dy_ref, do_ref, dl_ref, dhb_ref):
        dhb = dh_ref[...].astype(BF16)
        dhb_ref[...] = dhb
        dm = _dot_nt(dhb, wo_ref[...])
        yb = y_ref[...]
        ob = o_ref[...]
        ms = jnp.concatenate([_dot(yb, wsp_ref[s]) for s in range(4)], axis=1)
        ma = _dot(ob, wap_ref[...])
        ss = _sigmoid(g_ref[:, :d])
        sa = _sigmoid(g_ref[:, d:])
        dg_ref[:, :d] = dm * ms * ss * (1.0 - ss)
        dg_ref[:, d:] = dm * ma * sa * (1.0 - sa)
        dms = (dm * ss).astype(BF16)
        dma = (dm * sa).astype(BF16)
        dms_ref[...] = dms
        dma_ref[...] = dma
        dy = _dot_nt(dms[:, :ns], wsp_ref[0])
        for s in range(1, 4):
            dy += _dot_nt(dms[:, s * ns:(s + 1) * ns], wsp_ref[s])
        dy_ref[...] = dy
        do = _dot_nt(dma, wap_ref[...])
        do_ref[...] = do.astype(BF16)
        dl_ref[...] = _split_dot(do * ob.astype(F32), sel_ref[...])

    return pl.pallas_call(
        body, name="merge_bwd", grid=(lp // tm,),
        in_specs=[_row(tm, d), _row(tm, w), _row(tm, d), _row(tm, 2 * d),
                  _full((4, w, ns)), _full((d, d)), _full((d, d)), _full((d, nsel))],
        out_specs=[_row(tm, 2 * d), _row(tm, d), _row(tm, d), _row(tm, w), _row(tm, d), _row(tm, nsel), _row(tm, d)],
        out_shape=[jax.ShapeDtypeStruct((lp, 2 * d), F32), jax.ShapeDtypeStruct((lp, d), BF16),
                   jax.ShapeDtypeStruct((lp, d), BF16), jax.ShapeDtypeStruct((lp, w), F32),
                   jax.ShapeDtypeStruct((lp, d), BF16), jax.ShapeDtypeStruct((lp, nsel), F32),
                   jax.ShapeDtypeStruct((lp, d), BF16)],
        compiler_params=_params(("parallel",)),
    )(dh1, yssm, o, gates, wsp4, wap, wo, sel)


def _mlp_in(h1, g, w4, tm):
    lp, d = h1.shape

    def body(x_ref, g_ref, w_ref, r_ref):
        h = _rms(x_ref[...], g_ref[...]).astype(BF16)
        for s in range(4):
            r_ref[:, s * d:(s + 1) * d] = jnp.maximum(_dot(h, w_ref[s]), 0.0).astype(BF16)

    return pl.pallas_call(
        body, name="mlp_in", grid=(lp // tm,),
        in_specs=[_row(tm, d), _full((1, d)), _full((4, d, d))],
        out_specs=_row(tm, 4 * d), out_shape=jax.ShapeDtypeStruct((lp, 4 * d), BF16),
        compiler_params=_params(("parallel",)),
    )(h1, g, w4)


def _square_bf16(r):
    rf = r.astype(F32)
    return (rf * rf).astype(BF16)


def _mlp_out(h1, r, w2, tm):
    lp, d = h1.shape
    ff = r.shape[1]

    def body(x_ref, r_ref, w_ref, o_ref):
        o_ref[...] = x_ref[...] + _dot(_square_bf16(r_ref[...]), w_ref[...])

    return pl.pallas_call(
        body, name="mlp_out", grid=(lp // tm,),
        in_specs=[_row(tm, d), _row(tm, ff), _full((ff, d))],
        out_specs=_row(tm, d), out_shape=jax.ShapeDtypeStruct((lp, d), F32),
        compiler_params=_params(("parallel",)),
    )(h1, r, w2)


def _final_loss(h3, g, tgt, rowmask, tm):
    lp, d = h3.shape

    def body(x_ref, g_ref, t_ref, m_ref, loss_ref, dx_ref, dg_ref):
        first = pl.program_id(0) == 0
        x = x_ref[...]
        gg = g_ref[...]
        err = (_rms(x, gg) - t_ref[...]) * m_ref[...]
        part = 0.5 * jnp.sum(jnp.sum(err * err, axis=1, keepdims=True), axis=0, keepdims=True) * (1.0 / d)
        part = jnp.broadcast_to(part, (SUBLANES, LANES))

        @pl.when(first)
        def _():
            loss_ref[...] = part

        @pl.when(jnp.logical_not(first))
        def _():
            loss_ref[...] += part

        dx, dgr = _rms_bwd(err * (1.0 / d), x, gg)
        dx_ref[...] = dx
        _acc_rows(dg_ref, dgr, first)

    return pl.pallas_call(
        body, name="final_loss", grid=(lp // tm,),
        in_specs=[_row(tm, d), _full((1, d)), _row(tm, d), _row(tm, 1)],
        out_specs=[_full((SUBLANES, LANES)), _row(tm, d), _full((1, d))],
        out_shape=[jax.ShapeDtypeStruct((SUBLANES, LANES), F32), jax.ShapeDtypeStruct((lp, d), F32),
                   jax.ShapeDtypeStruct((1, d), F32)],
        compiler_params=_params(("arbitrary",)),
    )(h3, g, tgt, rowmask)


def _mlp_bwd_a(dh3, r, w2, tm):
    lp, d = dh3.shape
    ff = r.shape[1]

    def body(dh_ref, r_ref, w_ref, dz_ref, dhb_ref):
        dhb = dh_ref[...].astype(BF16)
        dhb_ref[...] = dhb
        da = _dot_nt(dhb, w_ref[...])
        dz_ref[...] = (da * (2.0 * r_ref[...].astype(F32))).astype(BF16)

    return pl.pallas_call(
        body, name="mlp_bwd_a", grid=(lp // tm,),
        in_specs=[_row(tm, d), _row(tm, ff), _full((ff, d))],
        out_specs=[_row(tm, ff), _row(tm, d)],
        out_shape=[jax.ShapeDtypeStruct((lp, ff), BF16), jax.ShapeDtypeStruct((lp, d), BF16)],
        compiler_params=_params(("parallel",)),
    )(dh3, r, w2)


def _mlp_bwd_b(dz, dh3, h1, g, w4, tm):
    lp, d = h1.shape

    def body(dz_ref, dh_ref, x_ref, g_ref, w_ref, dx_ref, dg_ref):
        first = pl.program_id(0) == 0
        dh2 = _dot_nt(dz_ref[:, :d], w_ref[0])
        for s in range(1, 4):
            dh2 += _dot_nt(dz_ref[:, s * d:(s + 1) * d], w_ref[s])
        dx, dgr = _rms_bwd(dh2, x_ref[...], g_ref[...])
        dx_ref[...] = dh_ref[...] + dx
        _acc_rows(dg_ref, dgr, first)

    return pl.pallas_call(
        body, name="mlp_bwd_b", grid=(lp // tm,),
        in_specs=[_row(tm, 4 * d), _row(tm, d), _row(tm, d), _full((1, d)), _full((4, d, d))],
        out_specs=[_row(tm, d), _full((1, d))],
        out_shape=[jax.ShapeDtypeStruct((lp, d), F32), jax.ShapeDtypeStruct((1, d), F32)],
        compiler_params=_params(("arbitrary",)),
    )(dz, dh3, h1, g, w4)


def _in_proj_bwd(dyv, du0, du1, dskip, dqkv, dgates, dres, xin, g, w4, tm):
    lp, d = xin.shape
    hd = d // 2

    def body(dy_ref, a_ref, b_ref, ds_ref, dq_ref, dgt_ref, dr_ref, x_ref, g_ref, w_ref, dx_ref, dg_ref, dp_ref):
        first = pl.program_id(0) == 0
        du = (dy_ref[...] * ds_ref[...] + a_ref[...] + b_ref[...]).astype(BF16)
        dq = dq_ref[...].astype(BF16)
        dgt = dgt_ref[...].astype(BF16)
        dp_ref[:, :hd] = du
        dp_ref[:, hd:2 * d] = dq
        dp_ref[:, 2 * d:] = dgt
        dh = _dot_nt(du, w_ref[0, :, :hd]) + _dot_nt(dq[:, :hd], w_ref[0, :, hd:])
        dh += _dot_nt(dq[:, hd:], w_ref[1])
        dh += _dot_nt(dgt[:, :d], w_ref[2]) + _dot_nt(dgt[:, d:], w_ref[3])
        dx, dgr = _rms_bwd(dh, x_ref[...], g_ref[...])
        dx_ref[...] = dr_ref[...] + dx
        _acc_rows(dg_ref, dgr, first)

    return pl.pallas_call(
        body, name="in_proj_bwd", grid=(lp // tm,),
        in_specs=[_row(tm, hd), _row(tm, hd), _row(tm, hd), _full((1, hd)), _row(tm, 3 * hd), _row(tm, 2 * d),
                  _row(tm, d), _row(tm, d), _full((1, d)), _full((4, d, d))],
        out_specs=[_row(tm, d), _full((1, d)), _row(tm, 4 * d)],
        out_shape=[jax.ShapeDtypeStruct((lp, d), F32), jax.ShapeDtypeStruct((1, d), F32),
                   jax.ShapeDtypeStruct((lp, 4 * d), BF16)],
        compiler_params=_params(("arbitrary",)),
    )(dyv, du0, du1, dskip, dqkv, dgates, dres, xin, g, w4)


def _wgrad(a, dy, nshard, tm, tn, name, gain=None, square=False):
    lp, k = a.shape
    n = dy.shape[1]
    ns = n // nshard
    assert ns % tn == 0
    per = ns // tn

    def body(*refs):
        if gain is not None:
            a_ref, g_ref, dy_ref, o_ref = refs
            at = _rms(a_ref[...], g_ref[...]).astype(BF16)
        else:
            a_ref, dy_ref, o_ref = refs
            at = _square_bf16(a_ref[...]) if square else a_ref[...]
        i = pl.program_id(1)
        acc = _dot_tn(at, dy_ref[...])

        @pl.when(i == 0)
        def _():
            o_ref[0] = acc

        @pl.when(i != 0)
        def _():
            o_ref[0] += acc

    in_specs = [pl.BlockSpec((tm, k), lambda j, i: (i, 0))]
    args = [a]
    if gain is not None:
        in_specs.append(pl.BlockSpec((1, k), lambda j, i: (0, 0)))
        args.append(gain)
    in_specs.append(pl.BlockSpec((tm, tn), lambda j, i: (i, j)))
    args.append(dy)
    return pl.pallas_call(
        body, name=name, grid=(n // tn, lp // tm), in_specs=in_specs,
        out_specs=pl.BlockSpec((1, k, tn), lambda j, i: (j // per, 0, j % per)),
        out_shape=jax.ShapeDtypeStruct((nshard, k, ns), F32),
        compiler_params=_params(("parallel", "arbitrary")),
    )(*args)


def _head_tables(d):
    idx = np.arange(LANES)
    mean = (idx[:, None] // HEAD_DIM == idx[None, :] // HEAD_DIM).astype(np.float32) / HEAD_DIM
    n_heads = d // HEAD_DIM
    kvh = n_heads // GQA_REP
    c = np.arange(d)
    col = np.arange(kvh * LANES)
    head_of_col = (col // LANES) * GQA_REP + (col % LANES)
    sel = ((c[:, None] // HEAD_DIM == head_of_col[None, :]) & ((col % LANES) < GQA_REP)[None, :]).astype(np.float32)
    return jnp.asarray(mean, BF16), jnp.asarray(sel, BF16)


def _swap_pairs(y):
    lane = lax.broadcasted_iota(jnp.int32, y.shape, 1)
    return jnp.where(lane % 2 == 0, pltpu.roll(y, LANES - 1, 1), pltpu.roll(y, 1, 1))


def _qk_prep(qkv, cos_t, sin_t, qg, kg, mean_m, tm):
    lp, wq = qkv.shape
    d = wq * 2 // 3
    kvw = d // 4
    kvh = kvw // HEAD_DIM
    scale = HEAD_DIM ** -0.5

    def body(x_ref, c_ref, s_ref, qg_ref, kg_ref, m_ref, q_ref, k_ref, v_ref):
        cs, sn, mm = c_ref[...], s_ref[...], m_ref[...]
        for b in range((d + kvw) // LANES):
            x = x_ref[:, b * LANES:(b + 1) * LANES]
            gg = qg_ref[...] if b < d // LANES else kg_ref[...]
            y = x * lax.rsqrt(_split_dot(x * x, mm) + NORM_EPS) * gg
            out = y * cs + _swap_pairs(y) * sn
            if b < d // LANES:
                q_ref[:, b * LANES:(b + 1) * LANES] = (out * scale).astype(BF16)
            else:
                kb = b - d // LANES
                k_ref[2 * kb] = out[:, :HEAD_DIM].astype(BF16)
                k_ref[2 * kb + 1] = out[:, HEAD_DIM:].astype(BF16)
        for h in range(kvh):
            v_ref[h] = x_ref[:, d + kvw + h * HEAD_DIM:d + kvw + (h + 1) * HEAD_DIM].astype(BF16)

    kv_spec = pl.BlockSpec((kvh, tm, HEAD_DIM), lambda i: (0, i, 0))
    return pl.pallas_call(
        body, name="qk_prep", grid=(lp // tm,),
        in_specs=[_row(tm, wq), _row(tm, LANES), _row(tm, LANES), _full((1, LANES)), _full((1, LANES)),
                  _full((LANES, LANES))],
        out_specs=[_row(tm, d), kv_spec, kv_spec],
        out_shape=[jax.ShapeDtypeStruct((lp, d), BF16), jax.ShapeDtypeStruct((kvh, lp, HEAD_DIM), BF16),
                   jax.ShapeDtypeStruct((kvh, lp, HEAD_DIM), BF16)],
        compiler_params=_params(("parallel",)),
    )(qkv, cos_t, sin_t, qg, kg, mean_m)


def _qk_bwd(qkv, dq, dk, dv, cos_t, sin_t, qg, kg, mean_m, tm):
    lp, wq = qkv.shape
    d = wq * 2 // 3
    kvw = d // 4
    kvh = kvw // HEAD_DIM
    scale = HEAD_DIM ** -0.5

    def body(x_ref, dq_ref, dk_ref, dv_ref, c_ref, s_ref, qg_ref, kg_ref, m_ref, o_ref, dqg_ref, dkg_ref):
        first = pl.program_id(0) == 0
        cs, sn, mm = c_ref[...], s_ref[...], m_ref[...]
        sums = [None, None]
        for b in range((d + kvw) // LANES):
            is_q = b < d // LANES
            x = x_ref[:, b * LANES:(b + 1) * LANES]
            gg = qg_ref[...] if is_q else kg_ref[...]
            r = lax.rsqrt(_split_dot(x * x, mm) + NORM_EPS)
            nrm = x * r
            if is_q:
                dout = dq_ref[:, b * LANES:(b + 1) * LANES] * scale
            else:
                kb = b - d // LANES
                dout = jnp.concatenate([dk_ref[2 * kb], dk_ref[2 * kb + 1]], axis=1)
            dy = dout * cs + _swap_pairs(dout * sn)
            part = jnp.sum(dy * nrm, axis=0, keepdims=True)
            sums[0 if is_q else 1] = part if sums[0 if is_q else 1] is None else sums[0 if is_q else 1] + part
            dn = dy * gg
            o_ref[:, b * LANES:(b + 1) * LANES] = r * (dn - nrm * _split_dot(dn * nrm, mm))
        for h in range(kvh):
            o_ref[:, d + kvw + h * HEAD_DIM:d + kvw + (h + 1) * HEAD_DIM] = dv_ref[h]
        for ref, s in ((dqg_ref, sums[0]), (dkg_ref, sums[1])):
            s = s + pltpu.roll(s, HEAD_DIM, 1)

            @pl.when(first)
            def _(ref=ref, s=s):
                ref[...] = s

            @pl.when(jnp.logical_not(first))
            def _(ref=ref, s=s):
                ref[...] += s

    kv_spec = pl.BlockSpec((kvh, tm, HEAD_DIM), lambda i: (0, i, 0))
    return pl.pallas_call(
        body, name="qk_bwd", grid=(lp // tm,),
        in_specs=[_row(tm, wq), _row(tm, d), kv_spec, kv_spec, _row(tm, LANES), _row(tm, LANES),
                  _full((1, LANES)), _full((1, LANES)), _full((LANES, LANES))],
        out_specs=[_row(tm, wq), _full((1, LANES)), _full((1, LANES))],
        out_shape=[jax.ShapeDtypeStruct((lp, wq), F32), jax.ShapeDtypeStruct((1, LANES), F32),
                   jax.ShapeDtypeStruct((1, LANES), F32)],
        compiler_params=_params(("arbitrary",)),
    )(qkv, dq, dk, dv, cos_t, sin_t, qg, kg, mean_m)


def _attn_fwd(q, k, v, kbias, tq, tk):
    lp, d = q.shape
    kvh = k.shape[0]
    rw = GQA_REP * HEAD_DIM
    nk = lp // tk

    def body(q_ref, k_ref, v_ref, kb_ref, o_ref, lse_ref, m_s, l_s, acc_s):
        j = pl.program_id(2)

        @pl.when(j == 0)
        def _():
            m_s[...] = jnp.full(m_s.shape, MASK_VALUE, F32)
            l_s[...] = jnp.zeros(l_s.shape, F32)
            acc_s[...] = jnp.zeros(acc_s.shape, F32)

        kk, vv, kb = k_ref[0], v_ref[0], kb_ref[...]
        for h in range(GQA_REP):
            qh = q_ref[:, h * HEAD_DIM:(h + 1) * HEAD_DIM]
            s = _dot_nt(qh, kk) + kb
            m_prev = m_s[h]
            m_new = jnp.maximum(m_prev, jnp.max(s, axis=1, keepdims=True))
            p = jnp.exp(s - m_new[:, :1])
            alpha = jnp.exp(m_prev - m_new)
            l_s[h] = alpha * l_s[h] + jnp.sum(p, axis=1, keepdims=True)
            acc_s[h] = acc_s[h] * alpha[:, :HEAD_DIM] + _dot(p.astype(BF16), vv)
            m_s[h] = m_new

        @pl.when(j == nk - 1)
        def _():
            outs = [acc_s[h] / l_s[h][:, :HEAD_DIM] for h in range(GQA_REP)]
            o_ref[...] = jnp.concatenate(outs, axis=1).astype(BF16)
            lane = lax.broadcasted_iota(jnp.int32, (tq, LANES), 1)
            lse = jnp.zeros((tq, LANES), F32)
            for h in range(GQA_REP):
                lse = jnp.where(lane == h, m_s[h] + jnp.log(l_s[h]), lse)
            lse_ref[...] = lse

    return pl.pallas_call(
        body, name="attn_fwd", grid=(kvh, lp // tq, nk),
        in_specs=[pl.BlockSpec((tq, rw), lambda g, i, j: (i, g)),
                  pl.BlockSpec((1, tk, HEAD_DIM), lambda g, i, j: (g, j, 0)),
                  pl.BlockSpec((1, tk, HEAD_DIM), lambda g, i, j: (g, j, 0)),
                  pl.BlockSpec((1, tk), lambda g, i, j: (0, j))],
        out_specs=[pl.BlockSpec((tq, rw), lambda g, i, j: (i, g)),
                   pl.BlockSpec((tq, LANES), lambda g, i, j: (i, g))],
        out_shape=[jax.ShapeDtypeStruct((lp, d), BF16), jax.ShapeDtypeStruct((lp, kvh * LANES), F32)],
        scratch_shapes=[pltpu.VMEM((GQA_REP, tq, LANES), F32), pltpu.VMEM((GQA_REP, tq, LANES), F32),
                        pltpu.VMEM((GQA_REP, tq, HEAD_DIM), F32)],
        compiler_params=_params(("parallel", "parallel", "arbitrary")),
    )(q, k, v, kbias)


def _attn_bwd(q, k, v, kbias, do, lse, delta, tq, tk):
    lp, d = q.shape
    kvh = k.shape[0]
    rw = GQA_REP * HEAD_DIM
    nq = lp // tq

    def body(q_ref, k_ref, v_ref, kb_ref, do_ref, lse_ref, dl_ref, dq_ref, dk_ref, dv_ref, dk_s, dv_s):
        j = pl.program_id(1)
        i = pl.program_id(2)

        @pl.when(jnp.logical_and(i == 0, j == 0))
        def _():
            dq_ref[...] = jnp.zeros(dq_ref.shape, F32)

        @pl.when(i == 0)
        def _():
            dk_s[...] = jnp.zeros(dk_s.shape, F32)
            dv_s[...] = jnp.zeros(dv_s.shape, F32)

        kk, vv, kb = k_ref[0], v_ref[0], kb_ref[...]
        lse, dl = lse_ref[...], dl_ref[...]
        dqs = []
        for h in range(GQA_REP):
            qh = q_ref[:, h * HEAD_DIM:(h + 1) * HEAD_DIM]
            doh = do_ref[:, h * HEAD_DIM:(h + 1) * HEAD_DIM]
            p = jnp.exp(_dot_nt(qh, kk) + kb - lse[:, h:h + 1])
            ds = (p * (_dot_nt(doh, vv) - dl[:, h:h + 1])).astype(BF16)
            dv_s[...] += _dot_tn(p.astype(BF16), doh)
            dk_s[...] += _dot_tn(ds, qh)
            dqs.append(_dot(ds, kk))
        rows = pl.ds(pl.multiple_of(i * tq, tq), tq)
        dq_ref[rows, :] += jnp.concatenate(dqs, axis=1)

        @pl.when(i == nq - 1)
        def _():
            dk_ref[0] = dk_s[...]
            dv_ref[0] = dv_s[...]

    return pl.pallas_call(
        body, name="attn_bwd", grid=(kvh, lp // tk, nq),
        in_specs=[pl.BlockSpec((tq, rw), lambda g, j, i: (i, g)),
                  pl.BlockSpec((1, tk, HEAD_DIM), lambda g, j, i: (g, j, 0)),
                  pl.BlockSpec((1, tk, HEAD_DIM), lambda g, j, i: (g, j, 0)),
                  pl.BlockSpec((1, tk), lambda g, j, i: (0, j)),
                  pl.BlockSpec((tq, rw), lambda g, j, i: (i, g)),
                  pl.BlockSpec((tq, LANES), lambda g, j, i: (i, g)),
                  pl.BlockSpec((tq, LANES), lambda g, j, i: (i, g))],
        out_specs=[pl.BlockSpec((lp, rw), lambda g, j, i: (0, g)),
                   pl.BlockSpec((1, tk, HEAD_DIM), lambda g, j, i: (g, j, 0)),
                   pl.BlockSpec((1, tk, HEAD_DIM), lambda g, j, i: (g, j, 0))],
        out_shape=[jax.ShapeDtypeStruct((lp, d), F32), jax.ShapeDtypeStruct((kvh, lp, HEAD_DIM), F32),
                   jax.ShapeDtypeStruct((kvh, lp, HEAD_DIM), F32)],
        scratch_shapes=[pltpu.VMEM((tk, HEAD_DIM), F32), pltpu.VMEM((tk, HEAD_DIM), F32)],
        compiler_params=_params(("parallel", "arbitrary", "arbitrary")),
    )(q, k, v, kbias, do, lse, delta)


def _ssm_math(a_re, a_im, log_dt, bt_re, bt_im):
    dt = jnp.exp(log_dt)
    lam_re = jnp.minimum(a_re, EIG_RE_MAX)
    lam_im = a_im
    mag = jnp.exp(lam_re * dt)
    ang = lam_im * dt
    lb_re = mag * jnp.cos(ang)
    lb_im = mag * jnp.sin(ang)
    num_re = lb_re - 1.0
    num_im = lb_im
    den = lam_re * lam_re + lam_im * lam_im
    f_re = (num_re * lam_re + num_im * lam_im) / den
    f_im = (num_im * lam_re - num_re * lam_im) / den
    bb_re = f_re[:, None, :] * bt_re - f_im[:, None, :] * bt_im
    bb_im = f_re[:, None, :] * bt_im + f_im[:, None, :] * bt_re
    return lb_re, lb_im, bb_re, bb_im


def _ssm_discretize(a_re, a_im, log_dt, bt_re, bt_im):
    nd, g, n = a_re.shape
    p = bt_re.shape[2]

    def body(ar_ref, ai_ref, ld_ref, br_ref, bi_ref, bbr_ref, bbi_ref, pr_ref, pi_ref, hr_ref, hi_ref):
        lb_re, lb_im, bb_re, bb_im = _ssm_math(ar_ref[0], ai_ref[0], ld_ref[0], br_ref[0], bi_ref[0])
        bbr_ref[0] = bb_re
        bbi_ref[0] = bb_im
        cr, ci = lb_re, lb_im
        for k in range(KSTEPS):
            pr_ref[0, k] = cr
            pi_ref[0, k] = ci
            if k < KSTEPS - 1:
                cr, ci = cr * lb_re - ci * lb_im, cr * lb_im + ci * lb_re
        for t in range(2):
            cr, ci = cr * cr - ci * ci, 2.0 * cr * ci
            hr_ref[0, t] = cr
            hi_ref[0, t] = ci

    s3 = pl.BlockSpec((1, g, n), lambda i: (i, 0, 0))
    s4 = pl.BlockSpec((1, g, p, n), lambda i: (i, 0, 0, 0))
    sp = pl.BlockSpec((1, KSTEPS, g, n), lambda i: (i, 0, 0, 0))
    sh = pl.BlockSpec((1, 2, g, n), lambda i: (i, 0, 0, 0))
    return pl.pallas_call(
        body, name="ssm_discretize", grid=(nd,),
        in_specs=[s3, s3, pl.BlockSpec((1, g, 1), lambda i: (i, 0, 0)), s4, s4],
        out_specs=[s4, s4, sp, sp, sh, sh],
        out_shape=[jax.ShapeDtypeStruct((nd, g, p, n), F32)] * 2 + [jax.ShapeDtypeStruct((nd, KSTEPS, g, n), F32)] * 2
        + [jax.ShapeDtypeStruct((nd, 2, g, n), F32)] * 2,
        compiler_params=_params(("parallel",)),
    )(a_re, a_im, log_dt, bt_re, bt_im)


def _ssm_param_bwd(a_re, a_im, log_dt, bt_re, bt_im, dlb_re, dlb_im, dbb_re, dbb_im):
    nd, g, n = a_re.shape
    p = bt_re.shape[2]

    def body(ar_ref, ai_ref, ld_ref, br_ref, bi_ref, c0_ref, c1_ref, c2_ref, c3_ref,
             o0_ref, o1_ref, o2_ref, o3_ref, o4_ref):
        _, vjp = jax.vjp(_ssm_math, ar_ref[0], ai_ref[0], ld_ref[0], br_ref[0], bi_ref[0])
        outs = vjp((c0_ref[0], c1_ref[0], c2_ref[0], c3_ref[0]))
        for ref, val in zip((o0_ref, o1_ref, o2_ref, o3_ref, o4_ref), outs):
            ref[0] = val

    s3 = pl.BlockSpec((1, g, n), lambda i: (i, 0, 0))
    s1 = pl.BlockSpec((1, g, 1), lambda i: (i, 0, 0))
    s4 = pl.BlockSpec((1, g, p, n), lambda i: (i, 0, 0, 0))
    return pl.pallas_call(
        body, name="ssm_param_bwd", grid=(nd,),
        in_specs=[s3, s3, s1, s4, s4, s3, s3, s4, s4],
        out_specs=[s3, s3, s1, s4, s4],
        out_shape=[jax.ShapeDtypeStruct((nd, g, n), F32)] * 2 + [jax.ShapeDtypeStruct((nd, g, 1), F32)]
        + [jax.ShapeDtypeStruct((nd, g, p, n), F32)] * 2,
        compiler_params=_params(("parallel",)),
    )(a_re, a_im, log_dt, bt_re, bt_im, dlb_re, dlb_im, dbb_re, dbb_im)


def _cmul(ar, ai, xr, xi, conj):
    if conj:
        return ar * xr + ai * xi, ar * xi - ai * xr
    return ar * xr - ai * xi, ar * xi + ai * xr


def _scan_chunk(buf, tab, carry, ein, nj, rev, conj, base=0):
    ks = list(range(KSTEPS))
    if rev:
        ks = ks[::-1]
    sub = lax.broadcasted_iota(jnp.int32, (SUBLANES, SCAN_LANES), 0)
    edge = sub == (SUBLANES - 1 if rev else 0)

    def step(j, _):
        jr, ji = j, nj + j
        ar, ai = tab[base, jr], tab[base, ji]
        hr = jnp.zeros((SUBLANES, SCAN_LANES), F32)
        hi = jnp.zeros((SUBLANES, SCAN_LANES), F32)
        for k in ks:
            rows = pl.ds(k * SUBLANES, SUBLANES)
            pr, pi_ = _cmul(ar, ai, hr, hi, conj)
            hr = pr + buf[jr, rows, :]
            hi = pi_ + buf[ji, rows, :]
            buf[jr, rows, :] = hr
            buf[ji, rows, :] = hi
        shift = SUBLANES - 1 if rev else 1
        er = jnp.where(edge, carry[jr], pltpu.roll(hr, shift, 0))
        ei = jnp.where(edge, carry[ji], pltpu.roll(hi, shift, 0))
        for t, dist in enumerate((1, 2, 4)):
            sh = SUBLANES - dist if rev else dist
            pr, pi_ = _cmul(tab[base + 1 + t, jr], tab[base + 1 + t, ji], pltpu.roll(er, sh, 0), pltpu.roll(ei, sh, 0), conj)
            er, ei = er + pr, ei + pi_
        ein[jr] = er
        ein[ji] = ei
        pr, pi_ = _cmul(tab[base + 4 + KSTEPS - 1, jr], tab[base + 4 + KSTEPS - 1, ji], er, ei, conj)
        last = 0 if rev else SUBLANES - 1
        carry[jr] = jnp.broadcast_to((hr + pr)[last:last + 1, :], (SUBLANES, SCAN_LANES))
        carry[ji] = jnp.broadcast_to((hi + pi_)[last:last + 1, :], (SUBLANES, SCAN_LANES))
        for n, k in enumerate(ks):
            rows = pl.ds(k * SUBLANES, SUBLANES)
            pr, pi_ = _cmul(tab[base + 4 + n, jr], tab[base + 4 + n, ji], er, ei, conj)
            buf[jr, rows, :] += pr
            buf[ji, rows, :] += pi_
        return 0

    lax.fori_loop(0, nj, step, 0)


def _to_blocks(buf, val, nblk):
    for b in range(nblk):
        buf[b] = val[:, b * SCAN_LANES:(b + 1) * SCAN_LANES]


def _from_blocks(buf, nblk):
    return jnp.concatenate([buf[b] for b in range(nblk)], axis=1)


def _ssm_fwd(u, wb, wct, tab, rev, name):
    lp, w = u.shape
    s2 = wb.shape[1]
    nj = s2 // (2 * SCAN_LANES)
    nc = lp // CHUNK
    ntab = tab.shape[0]
    cidx = (lambda c: nc - 1 - c) if rev else (lambda c: c)

    def body(u_ref, wb_ref, wct_ref, tab_ref, y_ref, ck_ref, buf, carry, ein):
        @pl.when(pl.program_id(0) == 0)
        def _():
            carry[...] = jnp.zeros(carry.shape, F32)

        _to_blocks(buf, _dot(u_ref[...].astype(BF16), wb_ref[...]), 2 * nj)
        ck_ref[0] = carry[...]
        _scan_chunk(buf, tab_ref, carry, ein, nj, rev, False)
        y_ref[...] = _dot_nt(_from_blocks(buf, 2 * nj).astype(BF16), wct_ref[...])

    return pl.pallas_call(
        body, name=name, grid=(nc,),
        in_specs=[pl.BlockSpec((CHUNK, w), lambda c: (cidx(c), 0)), _full((w, s2)), _full((w, s2)),
                  _full((ntab, 2 * nj, SUBLANES, SCAN_LANES))],
        out_specs=[pl.BlockSpec((CHUNK, w), lambda c: (cidx(c), 0)),
                   pl.BlockSpec((1, 2 * nj, SUBLANES, SCAN_LANES), lambda c: (cidx(c), 0, 0, 0))],
        out_shape=[jax.ShapeDtypeStruct((lp, w), F32), jax.ShapeDtypeStruct((nc, 2 * nj, SUBLANES, SCAN_LANES), F32)],
        scratch_shapes=[pltpu.VMEM((2 * nj, CHUNK, SCAN_LANES), F32), pltpu.VMEM((2 * nj, SUBLANES, SCAN_LANES), F32),
                        pltpu.VMEM((2 * nj, SUBLANES, SCAN_LANES), F32)],
        compiler_params=_params(("arbitrary",)),
    )(u, wb, wct, tab)


def _ssm_bwd(u, dy, ckpt, wb, wct, tab, rev, name):
    lp, w = u.shape
    s2 = wb.shape[1]
    nj = s2 // (2 * SCAN_LANES)
    nc = lp // CHUNK
    ntab = tab.shape[0]
    cidx = (lambda c: c) if rev else (lambda c: nc - 1 - c)

    def body(u_ref, dy_ref, ck_ref, wb_hbm, wct_hbm, tab_hbm, du_ref, dwb_hbm, dwc_hbm, dlb_ref,
             wb_ref, wct_ref, tab_ref, dwb_ref, dwc_ref, xs, ls, xcar, lcar, xin, lin):
        c = pl.program_id(0)

        @pl.when(c == 0)
        def _():
            pltpu.sync_copy(wb_hbm, wb_ref)
            pltpu.sync_copy(wct_hbm, wct_ref)
            pltpu.sync_copy(tab_hbm, tab_ref)
            lcar[...] = jnp.zeros(lcar.shape, F32)
            dwb_ref[...] = jnp.zeros(dwb_ref.shape, F32)
            dwc_ref[...] = jnp.zeros(dwc_ref.shape, F32)
            dlb_ref[...] = jnp.zeros(dlb_ref.shape, F32)

        ub = u_ref[...].astype(BF16)
        dyb = dy_ref[...].astype(BF16)
        _to_blocks(xs, _dot(ub, wb_ref[...]), 2 * nj)
        xcar[...] = ck_ref[0]
        _scan_chunk(xs, tab_ref, xcar, xin, nj, rev, False)
        _to_blocks(ls, _dot(dyb, wct_ref[...]), 2 * nj)
        _scan_chunk(ls, tab_ref, lcar, lin, nj, not rev, True, base=ntab // 2)
        xb = _from_blocks(xs, 2 * nj).astype(BF16)
        lb = _from_blocks(ls, 2 * nj).astype(BF16)
        dwc_ref[...] += _dot_tn(dyb, xb)
        dwb_ref[...] += _dot_tn(ub, lb)
        du_ref[...] = _dot_nt(lb, wb_ref[...])

        def step(j, _):
            jr, ji = j, nj + j
            ar = jnp.zeros((SUBLANES, SCAN_LANES), F32)
            ai = jnp.zeros((SUBLANES, SCAN_LANES), F32)
            for k in range(KSTEPS):
                kp = k + 1 if rev else k - 1
                rows = pl.ds(k * SUBLANES, SUBLANES)
                if 0 <= kp < KSTEPS:
                    prow = pl.ds(kp * SUBLANES, SUBLANES)
                    xr, xi = xs[jr, prow, :], xs[ji, prow, :]
                else:
                    xr, xi = xin[jr], xin[ji]
                lr, li = ls[jr, rows, :], ls[ji, rows, :]
                ar += lr * xr + li * xi
                ai += li * xr - lr * xi
            dlb_ref[jr] += ar
            dlb_ref[ji] += ai
            return 0

        lax.fori_loop(0, nj, step, 0)

        @pl.when(c == nc - 1)
        def _():
            for b in range(2 * nj):
                dlb_ref[b] = jnp.broadcast_to(jnp.sum(dlb_ref[b], axis=0, keepdims=True), (SUBLANES, SCAN_LANES))
            pltpu.sync_copy(dwb_ref, dwb_hbm)
            pltpu.sync_copy(dwc_ref, dwc_hbm)

    st = (2 * nj, SUBLANES, SCAN_LANES)
    return pl.pallas_call(
        body, name=name, grid=(nc,),
        in_specs=[pl.BlockSpec((CHUNK, w), lambda c: (cidx(c), 0)), pl.BlockSpec((CHUNK, w), lambda c: (cidx(c), 0)),
                  pl.BlockSpec((1,) + st, lambda c: (cidx(c), 0, 0, 0)), _ANY, _ANY, _ANY],
        out_specs=[pl.BlockSpec((CHUNK, w), lambda c: (cidx(c), 0)), _ANY, _ANY, _full(st)],
        out_shape=[jax.ShapeDtypeStruct((lp, w), F32), jax.ShapeDtypeStruct((w, s2), F32),
                   jax.ShapeDtypeStruct((w, s2), F32), jax.ShapeDtypeStruct(st, F32)],
        scratch_shapes=[pltpu.VMEM((w, s2), BF16), pltpu.VMEM((w, s2), BF16), pltpu.VMEM((ntab,) + st, F32),
                        pltpu.VMEM((w, s2), F32), pltpu.VMEM((w, s2), F32),
                        pltpu.VMEM((2 * nj, CHUNK, SCAN_LANES), F32), pltpu.VMEM((2 * nj, CHUNK, SCAN_LANES), F32),
                        pltpu.VMEM(st, F32), pltpu.VMEM(st, F32), pltpu.VMEM(st, F32), pltpu.VMEM(st, F32)],
        compiler_params=_params(("arbitrary",)),
    )(u, dy, ckpt, wb, wct, tab)


def _embed_block_diag(t_re, t_im):
    g, p, n = t_re.shape
    eye = jnp.eye(g, dtype=t_re.dtype)
    parts = [jnp.einsum('gpn,gh->gphn', t, eye).reshape(g * p, g * n) for t in (t_re, t_im)]
    return jnp.concatenate(parts, axis=1)


def _extract_block_diag(m, g, p, n):
    m5 = m.reshape(g, p, 2, g, n)
    diag = jnp.einsum('gpcgn->cgpn', m5)
    return diag[0], diag[1]


def _scan_layout(x, nj):
    lead = x.shape[:-1]
    x = x.reshape(lead + (nj, 1, SCAN_LANES))
    return jnp.broadcast_to(x, lead + (nj, SUBLANES, SCAN_LANES))


def _scan_tables(pw_re, pw_im, hi_re, hi_im, rev):
    s = pw_re.shape[1] * pw_re.shape[2]
    nj = s // SCAN_LANES
    sub = jnp.arange(SUBLANES).reshape(1, SUBLANES, 1)

    def pair(re, im, mask=None):
        re, im = _scan_layout(re.reshape(s), nj), _scan_layout(im.reshape(s), nj)
        if mask is not None:
            re, im = jnp.where(mask, re, 0.0), jnp.where(mask, im, 0.0)
        return jnp.concatenate([re, im], axis=0)

    def live(dist):
        return (sub < SUBLANES - dist) if rev else (sub >= dist)

    rows = [pair(pw_re[0], pw_im[0]),
            pair(pw_re[KSTEPS - 1], pw_im[KSTEPS - 1], live(1)),
            pair(hi_re[0], hi_im[0], live(2)),
            pair(hi_re[1], hi_im[1], live(4))]
    rows += [pair(pw_re[k], pw_im[k]) for k in range(KSTEPS)]
    return jnp.stack(rows, axis=0)


def _adamw(w, g, m, v, tm):
    r, c = w.shape
    c1 = 1.0 - ADAM_B1 ** ADAM_STEP
    c2 = 1.0 - ADAM_B2 ** ADAM_STEP

    def body(w_ref, g_ref, m_ref, v_ref, d_ref, nm_ref, nv_ref):
        gg = g_ref[...]
        nm = ADAM_B1 * m_ref[...] + (1.0 - ADAM_B1) * gg
        nv = ADAM_B2 * v_ref[...] + (1.0 - ADAM_B2) * (gg * gg)
        nm_ref[...] = nm
        nv_ref[...] = nv
        d_ref[...] = -ADAM_LR * ((nm / c1) / (jnp.sqrt(nv / c2) + ADAM_EPS) + ADAM_WD * w_ref[...])

    spec = _row(tm, c)
    return pl.pallas_call(
        body, name="adamw", grid=(r // tm,), in_specs=[spec] * 4, out_specs=[spec] * 3,
        out_shape=[jax.ShapeDtypeStruct((r, c), F32)] * 3, compiler_params=_params(("parallel",)),
    )(w, g, m, v)


def _add2(a, b, tm):
    n, r, c = a.shape

    def body(a_ref, b_ref, o_ref):
        o_ref[...] = a_ref[...] + b_ref[...]

    spec = pl.BlockSpec((1, tm, c), lambda s, i: (s, i, 0))
    return pl.pallas_call(
        body, name="pair_sum", grid=(n, r // tm), in_specs=[spec, spec], out_specs=spec,
        out_shape=jax.ShapeDtypeStruct((n, r, c), F32), compiler_params=_params(("parallel", "parallel")),
    )(a, b)


def _sum4(a, tm):
    _, r, c = a.shape

    def body(a_ref, o_ref):
        o_ref[...] = ((a_ref[0] + a_ref[1]) + a_ref[2]) + a_ref[3]

    return pl.pallas_call(
        body, name="chip_sum", grid=(r // tm,), in_specs=[pl.BlockSpec((4, tm, c), lambda i: (0, i, 0))],
        out_specs=_row(tm, c), out_shape=jax.ShapeDtypeStruct((r, c), F32), compiler_params=_params(("parallel",)),
    )(a)


_ANY = pl.BlockSpec(memory_space=pl.ANY)


def _all_gather8(block, name):
    shape = block.shape

    def body(x_ref, out_ref, send_sems, recv_sems, local_sem):
        x, y, c = lax.axis_index("x"), lax.axis_index("y"), lax.axis_index("c")
        me, sibling = (x, y, c), (x, y, 1 - c)
        chips = [(1 - x, y), (x, 1 - y), (1 - x, 1 - y)]

        def slot(px, py, pc):
            return out_ref.at[4 * px + 2 * py + pc]

        def copy(k, blk, to, src=None):
            return pltpu.make_async_remote_copy(
                src_ref=slot(*blk) if src is None else src, dst_ref=slot(*blk),
                send_sem=send_sems.at[k], recv_sem=recv_sems.at[k], device_id=to, device_id_type=MESH_ID)

        mine = pltpu.make_async_copy(x_ref, slot(*me), local_sem)
        mine.start()
        first = [copy(0, me, sibling, src=x_ref)]
        first += [copy(1 + j, me, (*chip, c), src=x_ref) for j, chip in enumerate(chips)]
        for cp in first:
            cp.start()
        passed = [copy(4 + j, (*chip, c), sibling) for j, chip in enumerate(chips)]
        for j, chip in enumerate(chips):
            copy(1 + j, (*chip, c), me).wait_recv()
            passed[j].start()
        copy(0, sibling, me).wait_recv()
        for j, chip in enumerate(chips):
            copy(4 + j, (*chip, 1 - c), me).wait_recv()
        for cp in first + passed:
            cp.wait_send()
        mine.wait()

    return pl.pallas_call(
        body, name=name, out_shape=jax.ShapeDtypeStruct((8,) + shape, block.dtype),
        in_specs=[_ANY], out_specs=_ANY,
        scratch_shapes=[pltpu.SemaphoreType.DMA((7,)), pltpu.SemaphoreType.DMA((7,)), pltpu.SemaphoreType.DMA],
    )(block)


def _pair_exchange(g2, name):
    shape = g2.shape[1:]

    def body(g_ref, out_ref, send_sem, recv_sem):
        x, y, c = lax.axis_index("x"), lax.axis_index("y"), lax.axis_index("c")
        cp = pltpu.make_async_remote_copy(
            src_ref=g_ref.at[1 - c], dst_ref=out_ref, send_sem=send_sem, recv_sem=recv_sem,
            device_id=(x, y, 1 - c), device_id_type=MESH_ID)
        cp.start()
        cp.wait()

    return pl.pallas_call(
        body, name=name, out_shape=jax.ShapeDtypeStruct(shape, g2.dtype), in_specs=[_ANY], out_specs=_ANY,
        scratch_shapes=[pltpu.SemaphoreType.DMA, pltpu.SemaphoreType.DMA],
    )(g2)


def _chip_scatter(p4, name):
    def body(p_ref, out_ref, send_sems, recv_sems, local_sem):
        x, y, c = lax.axis_index("x"), lax.axis_index("y"), lax.axis_index("c")
        mine = 2 * x + y
        chips = [(1 - x, y), (x, 1 - y), (1 - x, 1 - y)]
        own = pltpu.make_async_copy(p_ref.at[mine], out_ref.at[mine], local_sem)
        own.start()
        sends = [pltpu.make_async_remote_copy(
            src_ref=p_ref.at[2 * cx + cy], dst_ref=out_ref.at[mine], send_sem=send_sems.at[k],
            recv_sem=recv_sems.at[k], device_id=(cx, cy, c), device_id_type=MESH_ID)
            for k, (cx, cy) in enumerate(chips)]
        for cp in sends:
            cp.start()
        for k, (cx, cy) in enumerate(chips):
            pltpu.make_async_remote_copy(
                src_ref=p_ref.at[mine], dst_ref=out_ref.at[2 * cx + cy], send_sem=send_sems.at[k],
                recv_sem=recv_sems.at[k], device_id=(cx, cy, c), device_id_type=MESH_ID).wait_recv()
        for cp in sends:
            cp.wait_send()
        own.wait()

    return pl.pallas_call(
        body, name=name, out_shape=jax.ShapeDtypeStruct(p4.shape, p4.dtype), in_specs=[_ANY], out_specs=_ANY,
        scratch_shapes=[pltpu.SemaphoreType.DMA((3,)), pltpu.SemaphoreType.DMA((3,)), pltpu.SemaphoreType.DMA],
    )(p4)


def _pair_gather(r, name):
    def body(r_ref, out_ref, send_sem, recv_sem, local_sem):
        x, y, c = lax.axis_index("x"), lax.axis_index("y"), lax.axis_index("c")
        own = pltpu.make_async_copy(r_ref, out_ref.at[c], local_sem)
        own.start()
        cp = pltpu.make_async_remote_copy(
            src_ref=r_ref, dst_ref=out_ref.at[c], send_sem=send_sem, recv_sem=recv_sem,
            device_id=(x, y, 1 - c), device_id_type=MESH_ID)
        cp.start()
        pltpu.make_async_remote_copy(
            src_ref=r_ref, dst_ref=out_ref.at[1 - c], send_sem=send_sem, recv_sem=recv_sem,
            device_id=(x, y, 1 - c), device_id_type=MESH_ID).wait_recv()
        cp.wait_send()
        own.wait()

    return pl.pallas_call(
        body, name=name, out_shape=jax.ShapeDtypeStruct((2,) + r.shape, r.dtype), in_specs=[_ANY], out_specs=_ANY,
        scratch_shapes=[pltpu.SemaphoreType.DMA, pltpu.SemaphoreType.DMA, pltpu.SemaphoreType.DMA],
    )(r)


PACK_COLS = 1024
PACK_ROW_MULT = 16
BIG = (("meta_tokens", 1), ("w_in", 1), ("w_glu", 0), ("w_ssm_proj", 1), ("w_attn_proj", 0), ("w_out", 0),
       ("w_mlp_in", 1), ("w_mlp_out", 0))
SMALL = ("norm_mix_g", "ssm_a_re", "ssm_a_im", "ssm_log_dt", "ssm_b_re", "ssm_b_im", "ssm_c_re", "ssm_c_im",
         "ssm_d", "b_glu", "q_norm_g", "k_norm_g", "norm_mlp_g", "norm_final_g")


def _pad_rows(flat, mult_rows):
    n = flat.shape[0]
    unit = PACK_COLS * mult_rows
    total = -(-n // unit) * unit
    return jnp.pad(flat, (0, total - n)).reshape(total // PACK_COLS, PACK_COLS)


def _half(t, c):
    return lax.dynamic_slice_in_dim(t, c * (t.shape[0] // 2), t.shape[0] // 2, 0)


def _gather_weights(shards, c):
    pieces = []
    for name, _ in BIG:
        h = _half(shards[name], c)
        if name == "meta_tokens":
            h = lax.bitcast_convert_type(h, BF16)
        else:
            h = h.astype(BF16)
        pieces.append(h.reshape(-1))
    block = _pad_rows(jnp.concatenate(pieces), PACK_ROW_MULT)
    got = _all_gather8(block, "weight_all_gather").reshape(4, 2, -1)
    out, off = {}, 0
    for name, _ in BIG:
        r, cdim = shards[name].shape
        n = (r // 2) * cdim * (2 if name == "meta_tokens" else 1)
        seg = got[:, :, off:off + n]
        off += n
        if name == "meta_tokens":
            seg = lax.bitcast_convert_type(seg.reshape(4, 2, r // 2, cdim, 2), F32)
        out[name] = seg.reshape(4, r, cdim)
    return out


def _reduce_gradients(big4, small_flat, c):
    n_small = small_flat.shape[0]
    unit = 8 * PACK_COLS
    small_pad = jnp.pad(small_flat, (0, -(-n_small // unit) * unit - n_small)).reshape(4, 2, -1)
    halves = []
    for name, _ in BIG:
        g = big4[name]
        halves.append(g.reshape(4, 2, -1))
    halves.append(small_pad)
    flat = jnp.concatenate(halves, axis=2)
    n = flat.shape[2]
    unit = PACK_COLS * PACK_ROW_MULT
    npad = -(-n // unit) * unit
    flat = jnp.pad(flat, ((0, 0), (0, 0), (0, npad - n)))
    g2 = jnp.transpose(flat, (1, 0, 2)).reshape(2, 4, npad // PACK_COLS, PACK_COLS)
    rows = npad // PACK_COLS
    tm = _pick_tile(rows, 512, 8)
    got = _pair_exchange(g2, "grad_pair_exchange")
    mine = lax.dynamic_index_in_dim(g2, c, 0, keepdims=False)
    pair = _add2(mine, got, tm)
    by_src = _chip_scatter(pair, "grad_chip_scatter")
    red = _sum4(by_src, tm)
    both = _pair_gather(red, "grad_pair_gather").reshape(2, npad)
    out, off = {}, 0
    for name, _ in BIG:
        _, r, cdim = big4[name].shape
        k = (r // 2) * cdim
        out[name] = both[:, off:off + k].reshape(r, cdim)
        off += k
    k = small_pad.shape[2]
    piece = red.reshape(npad)[off:off + k]
    small = _all_gather8(_pad_rows(piece, SUBLANES), "small_grad_all_gather")
    small = small.reshape(8, -1)[:, :k].reshape(-1)[:n_small]
    return out, small


def _to_chunk_order(a):
    lp = a.shape[0]
    rest = a.shape[1:]
    a = a.reshape((lp // CHUNK, SUBLANES, KSTEPS) + rest)
    return jnp.swapaxes(a, 1, 2).reshape((lp,) + rest)


def _from_chunk_order(a):
    lp = a.shape[0]
    rest = a.shape[1:]
    a = a.reshape((lp // CHUNK, KSTEPS, SUBLANES) + rest)
    return jnp.swapaxes(a, 1, 2).reshape((lp,) + rest)


def _rope_tables(l_total, lp):
    n_real = l_total - N_META
    pos = jnp.arange(n_real)
    row_id = (pos // GRID_W).astype(F32)
    col_id = (pos % GRID_W).astype(F32)
    ppa = HEAD_DIM // 4
    inv_freq = ROPE_THETA ** (-jnp.arange(ppa, dtype=F32) / ppa)
    ang = jnp.concatenate([row_id[:, None] * inv_freq, col_id[:, None] * inv_freq], axis=-1)
    ang = jnp.concatenate([jnp.zeros((N_META, HEAD_DIM // 2), F32), ang,
                           jnp.zeros((lp - l_total, HEAD_DIM // 2), F32)], axis=0)
    cos = jnp.repeat(jnp.cos(ang), 2, axis=1)
    sin = jnp.repeat(jnp.sin(ang), 2, axis=1) * jnp.tile(jnp.asarray([-1.0, 1.0], F32), HEAD_DIM // 2)
    return jnp.tile(cos, (1, LANES // HEAD_DIM)), jnp.tile(sin, (1, LANES // HEAD_DIM))


def kernel(x, meta_tokens, norm_mix_g, w_in, ssm_a_re, ssm_a_im, ssm_log_dt, ssm_b_re, ssm_b_im, ssm_c_re, ssm_c_im, ssm_d, w_glu, b_glu, q_norm_g, k_norm_g, w_ssm_proj, w_attn_proj, w_out, norm_mlp_g, w_mlp_in, w_mlp_out, norm_final_g, loss_target, m_meta_tokens, m_norm_mix_g, m_w_in, m_ssm_a_re, m_ssm_a_im, m_ssm_log_dt, m_ssm_b_re, m_ssm_b_im, m_ssm_c_re, m_ssm_c_im, m_ssm_d, m_w_glu, m_b_glu, m_q_norm_g, m_k_norm_g, m_w_ssm_proj, m_w_attn_proj, m_w_out, m_norm_mlp_g, m_w_mlp_in, m_w_mlp_out, m_norm_final_g, v_meta_tokens, v_norm_mix_g, v_w_in, v_ssm_a_re, v_ssm_a_im, v_ssm_log_dt, v_ssm_b_re, v_ssm_b_im, v_ssm_c_re, v_ssm_c_im, v_ssm_d, v_w_glu, v_b_glu, v_q_norm_g, v_k_norm_g, v_w_ssm_proj, v_w_attn_proj, v_w_out, v_norm_mlp_g, v_w_mlp_in, v_w_mlp_out, v_norm_final_g):
    args = dict(locals())
    names = list(dict.fromkeys([n for n, _ in BIG] + list(SMALL)))
    order = ['meta_tokens', 'norm_mix_g', 'w_in', 'ssm_a_re', 'ssm_a_im', 'ssm_log_dt', 'ssm_b_re', 'ssm_b_im',
             'ssm_c_re', 'ssm_c_im', 'ssm_d', 'w_glu', 'b_glu', 'q_norm_g', 'k_norm_g', 'w_ssm_proj', 'w_attn_proj',
             'w_out', 'norm_mlp_g', 'w_mlp_in', 'w_mlp_out', 'norm_final_g']
    assert sorted(names) == sorted(order)
    c_idx = lax.axis_index("c")

    seq, d = x.shape[1], x.shape[2]
    l_total = seq + N_META
    lp = -(-l_total // CHUNK) * CHUNK
    hd = d // 2
    n_groups = hd // SSM_GROUP
    n_state = n_groups * SSM_STATE
    nj = n_state // SCAN_LANES
    kvh = d // HEAD_DIM // GQA_REP

    shard2d = {}
    for name, _ in BIG:
        t = args[name]
        shard2d[name] = t.reshape(t.shape[-2], t.shape[-1])
    full = _gather_weights(shard2d, c_idx)
    meta_full = jnp.transpose(full["meta_tokens"], (1, 0, 2)).reshape(N_META, d)
    w_in4 = full["w_in"]
    w_mlp_in4 = full["w_mlp_in"]
    w_ssm_proj4 = full["w_ssm_proj"]
    w_glu_f = full["w_glu"].reshape(hd, hd)
    w_attn_proj_f = full["w_attn_proj"].reshape(d, d)
    w_out_f = full["w_out"].reshape(d, d)
    w_mlp_out_f = full["w_mlp_out"].reshape(4 * d, d)

    xin = jnp.concatenate([meta_full, x[0], jnp.zeros((lp - l_total, d), F32)], axis=0)
    xin = _to_chunk_order(xin)
    tgt = _to_chunk_order(jnp.pad(loss_target[0], ((N_META, lp - l_total), (0, 0))))
    pos = jnp.arange(lp)
    rowmask = _to_chunk_order(((pos >= N_META) & (pos < l_total)).astype(F32)[:, None])
    kbias = _to_chunk_order(jnp.where(pos < l_total, 0.0, MASK_VALUE).astype(F32)[:, None]).reshape(1, lp)
    cos_t, sin_t = _rope_tables(l_total, lp)
    cos_t, sin_t = _to_chunk_order(cos_t), _to_chunk_order(sin_t)
    mean_m, sel = _head_tables(d)

    tm = _pick_tile(lp, 320)
    tm_big = _pick_tile(lp, 640)
    tq = _pick_tile(lp, 640, LANES)
    g_mix = norm_mix_g.reshape(1, d)
    g_mlp = norm_mlp_g.reshape(1, d)
    g_fin = norm_final_g.reshape(1, d)
    qg = jnp.tile(q_norm_g.reshape(1, HEAD_DIM), (1, LANES // HEAD_DIM))
    kg = jnp.tile(k_norm_g.reshape(1, HEAD_DIM), (1, LANES // HEAD_DIM))
    dskip = ssm_d.reshape(1, hd)
    bglu = b_glu.reshape(1, hd)

    a_re, a_im = ssm_a_re[0], ssm_a_im[0]
    log_dt = ssm_log_dt[0][..., None]
    bt_re = jnp.swapaxes(ssm_b_re[0], 2, 3)
    bt_im = jnp.swapaxes(ssm_b_im[0], 2, 3)
    bb_re, bb_im, pw_re, pw_im, hi_re, hi_im = _ssm_discretize(a_re, a_im, log_dt, bt_re, bt_im)
    wb = [_embed_block_diag(bb_re[i], bb_im[i]).astype(BF16) for i in range(2)]
    wct = [_embed_block_diag(ssm_c_re[0, i], -ssm_c_im[0, i]).astype(BF16) for i in range(2)]
    tabs = [_scan_tables(pw_re[i], pw_im[i], hi_re[i], hi_im[i], rev=(i == 1)) for i in range(2)]
    tabs_adj = [_scan_tables(pw_re[i], pw_im[i], hi_re[i], hi_im[i], rev=(i == 0)) for i in range(2)]

    u, qkv, gates = _in_proj(xin, g_mix, w_in4, tm)
    y0, ck0 = _ssm_fwd(u, wb[0], wct[0], tabs[0], False, "ssm_fwd_0")
    y1, ck1 = _ssm_fwd(u, wb[1], wct[1], tabs[1], True, "ssm_fwd_1")
    yssm = _glu_fwd(u, y0, y1, dskip, w_glu_f, bglu, tm_big)
    q, k, v = _qk_prep(qkv, cos_t, sin_t, qg, kg, mean_m, tm)
    o, lse = _attn_fwd(q, k, v, kbias, tq, tq)
    h1, merged = _merge_fwd(yssm, o, gates, xin, w_ssm_proj4, w_attn_proj_f, w_out_f, tm)
    r = _mlp_in(h1, g_mlp, w_mlp_in4, tm)
    h3 = _mlp_out(h1, r, w_mlp_out_f, tm)
    loss_tile, dh3, d_gfin = _final_loss(h3, g_fin, tgt, rowmask, tm_big)
    loss = lax.psum(loss_tile[0, 0], ("x", "y", "c"))

    dz, dh3b = _mlp_bwd_a(dh3, r, w_mlp_out_f, tm)
    dh1, d_gmlp = _mlp_bwd_b(dz, dh3, h1, g_mlp, w_mlp_in4, tm)
    dgates, dms, dma, dyssm, do, delta, dh1b = _merge_bwd(dh1, yssm, o, gates, w_ssm_proj4, w_attn_proj_f, w_out_f,
                                                          sel, tm)
    dyv, d_wglu, d_bglu, d_dskip = _glu_bwd(dyssm, u, y0, y1, dskip, w_glu_f, bglu, tm_big)
    du0, dwb0, dwc0, dlb0 = _ssm_bwd(u, dyv, ck0, wb[0], wct[0], _both(tabs[0], tabs_adj[0]), False, "ssm_bwd_0")
    du1, dwb1, dwc1, dlb1 = _ssm_bwd(u, dyv, ck1, wb[1], wct[1], _both(tabs[1], tabs_adj[1]), True, "ssm_bwd_1")
    dq, dk, dv = _attn_bwd(q, k, v, kbias, do, lse, delta, tq, tq)
    dqkv, d_qg, d_kg = _qk_bwd(qkv, dq, dk, dv, cos_t, sin_t, qg, kg, mean_m, tm)
    dxin, d_gmix, dproj = _in_proj_bwd(dyv, du0, du1, dskip, dqkv, dgates, dh1, xin, g_mix, w_in4, tm)

    tn = min(d, 1024)
    grads4 = {
        "w_in": _wgrad(xin, dproj, 4, tm_big, tn, "wgrad_in", gain=g_mix),
        "w_mlp_in": _wgrad(h1, dz, 4, tm_big, tn, "wgrad_mlp_in", gain=g_mlp),
        "w_mlp_out": _wgrad(r, dh3b, 1, tm_big, min(d, 256), "wgrad_mlp_out", square=True).reshape(4, d, d),
        "w_out": _wgrad(merged, dh1b, 1, tm_big, tn, "wgrad_out").reshape(4, d // 4, d),
        "w_attn_proj": _wgrad(o, dma, 1, tm_big, tn, "wgrad_attn_proj").reshape(4, d // 4, d),
        "w_ssm_proj": _wgrad(yssm, dms, 4, tm_big, d // 4, "wgrad_ssm_proj"),
        "w_glu": d_wglu.reshape(4, hd // 4, hd),
    }
    dx_nat = _from_chunk_order(dxin)
    grads4["meta_tokens"] = jnp.swapaxes(dx_nat[:N_META].reshape(N_META, 4, d // 4), 0, 1)
    grad_x = dx_nat[N_META:l_total][None]

    dlb = jnp.stack([dlb0, dlb1])[:, :, 0, :]
    dlb_re = dlb[:, :nj].reshape(2, n_groups, SSM_STATE)
    dlb_im = dlb[:, nj:].reshape(2, n_groups, SSM_STATE)
    ex = [_extract_block_diag(m, n_groups, SSM_GROUP, SSM_STATE) for m in (dwb0, dwb1, dwc0, dwc1)]
    dbb_re = jnp.stack([ex[0][0], ex[1][0]])
    dbb_im = jnp.stack([ex[0][1], ex[1][1]])
    d_are, d_aim, d_logdt, d_btre, d_btim = _ssm_param_bwd(a_re, a_im, log_dt, bt_re, bt_im, dlb_re, dlb_im, dbb_re, dbb_im)
    small_grads = {
        "norm_mix_g": d_gmix, "ssm_a_re": d_are, "ssm_a_im": d_aim, "ssm_log_dt": d_logdt,
        "ssm_b_re": jnp.swapaxes(d_btre, 2, 3), "ssm_b_im": jnp.swapaxes(d_btim, 2, 3),
        "ssm_c_re": jnp.stack([ex[2][0], ex[3][0]]), "ssm_c_im": -jnp.stack([ex[2][1], ex[3][1]]),
        "ssm_d": d_dskip, "b_glu": d_bglu, "q_norm_g": d_qg[:, :HEAD_DIM], "k_norm_g": d_kg[:, :HEAD_DIM],
        "norm_mlp_g": d_gmlp, "norm_final_g": d_gfin,
    }
    small_flat = jnp.concatenate([small_grads[n].reshape(-1) for n in SMALL])

    red_big, red_small = _reduce_gradients(grads4, small_flat, c_idx)
    grad, delta_w, new_m, new_v = {}, {}, {}, {}
    for name, _ in BIG:
        w2 = shard2d[name]
        shp = args[name].shape
        g2 = red_big[name]
        t = _pick_tile(w2.shape[0], 256, 8)
        dl, nm, nv = _adamw(w2, g2, args["m_" + name].reshape(w2.shape), args["v_" + name].reshape(w2.shape), t)
        grad[name], delta_w[name], new_m[name], new_v[name] = (a.reshape(shp) for a in (g2, dl, nm, nv))

    def pack_small(prefix):
        flat = jnp.concatenate([args[prefix + n].reshape(-1) for n in SMALL])
        return _pad_rows(flat, SUBLANES)

    n_small = red_small.shape[0]
    gs = _pad_rows(red_small, SUBLANES)
    dl, nm, nv = _adamw(pack_small(""), gs, pack_small("m_"), pack_small("v_"), _pick_tile(gs.shape[0], 256, 8))
    off = 0
    for name in SMALL:
        shp = args[name].shape
        k = int(np.prod(shp))
        for dst, src in ((grad, gs), (delta_w, dl), (new_m, nm), (new_v, nv)):
            dst[name] = src.reshape(-1)[off:off + k].reshape(shp)
        off += k
    assert off == n_small

    return (loss, grad_x, *[grad[n] for n in order], *[delta_w[n] for n in order],
            *[new_m[n] for n in order], *[new_v[n] for n in order])


def _both(tab, tab_adj):
    return jnp.concatenate([tab, tab_adj], axis=0)
```

```python
import functools
import math

import numpy as np
import jax
import jax.numpy as jnp
from jax import lax
from jax.experimental import pallas as pl
from jax.experimental.pallas import tpu as pltpu

F32 = jnp.float32
BF16 = jnp.bfloat16

N_META = 16
GRID_W = 64
HEAD_DIM = 64
GQA_REP = 4
SSM_GROUP = 16
SSM_STATE = 64
ROPE_THETA = 10000.0
NORM_EPS = 1e-6
EIG_RE_MAX = -1e-4
ADAM_LR, ADAM_B1, ADAM_B2, ADAM_EPS, ADAM_WD, ADAM_STEP = 0.001, 0.9, 0.999, 1e-08, 0.01, 10

SUBLANES = 8
LANES = 128
CHUNK = 128
KSTEPS = CHUNK // SUBLANES
SCAN_LANES = 512
MXU_DIM = 256
SEQ_ALIGN = MXU_DIM
ATTN_TILE = 3 * MXU_DIM
VMEM_LIMIT = 56 << 20
MASK_VALUE = -1e30
MESH_ID = pl.DeviceIdType.MESH


def _dot(a, b):
    return jnp.dot(a, b, preferred_element_type=F32)


def _dot_nt(a, b):
    return lax.dot_general(a, b, (((1,), (1,)), ((), ())), preferred_element_type=F32)


def _dot_tn(a, b):
    return lax.dot_general(a, b, (((0,), (0,)), ((), ())), preferred_element_type=F32)


def _row(tm, width):
    return pl.BlockSpec((tm, width), lambda i: (i, 0))


def _full(shape):
    nd = len(shape)
    return pl.BlockSpec(shape, lambda i: (0,) * nd)


def _params(sem):
    return pltpu.CompilerParams(dimension_semantics=sem, vmem_limit_bytes=VMEM_LIMIT)


def _pick_tile(n, cap, mult=16):
    best = None
    for t in range(mult, min(n, cap) + 1, mult):
        if n % t == 0:
            best = t
    assert best is not None, (n, cap)
    return best


def _rstd(x):
    return lax.rsqrt(jnp.mean(x * x, axis=-1, keepdims=True) + NORM_EPS)


def _rms(x, g):
    return x * _rstd(x) * g


def _rms_bwd(dy, x, g):
    r = _rstd(x)
    xh = x * r
    gdy = dy * g
    dx = r * (gdy - xh * jnp.mean(gdy * xh, axis=-1, keepdims=True))
    return dx, dy * xh


def _split_dot(x, m):
    hi = x.astype(BF16)
    lo = (x - hi.astype(F32)).astype(BF16)
    return _dot(hi, m) + _dot(lo, m)


def _sigmoid(x):
    return 1.0 / (1.0 + jnp.exp(-x))


def _acc_rows(ref, val, first):
    s = jnp.sum(val, axis=0, keepdims=True)

    @pl.when(first)
    def _():
        ref[...] = s

    @pl.when(jnp.logical_not(first))
    def _():
        ref[...] += s


def _in_proj(xin, g, w4, tm):
    lp, d = xin.shape
    hd = d // 2

    def body(x_ref, g_ref, w_ref, u_ref, qkv_ref, gt_ref):
        h = _rms(x_ref[...], g_ref[...]).astype(BF16)
        p0 = _dot(h, w_ref[0])
        u_ref[...] = p0[:, :hd]
        qkv_ref[:, :hd] = p0[:, hd:]
        qkv_ref[:, hd:] = _dot(h, w_ref[1])
        gt_ref[:, :d] = _dot(h, w_ref[2])
        gt_ref[:, d:] = _dot(h, w_ref[3])

    return pl.pallas_call(
        body, name="in_proj", grid=(lp // tm,),
        in_specs=[_row(tm, d), _full((1, d)), _full((4, d, d))],
        out_specs=[_row(tm, hd), _row(tm, 3 * hd), _row(tm, 2 * d)],
        out_shape=[jax.ShapeDtypeStruct((lp, hd), F32), jax.ShapeDtypeStruct((lp, 3 * hd), F32),
                   jax.ShapeDtypeStruct((lp, 2 * d), F32)],
        compiler_params=_params(("parallel",)),
    )(xin, g, w4)


def _gelu(y):
    return 0.5 * y * (1.0 + lax.erf(y * (1.0 / math.sqrt(2.0))))


def _gelu_grad(y):
    return 0.5 * (1.0 + lax.erf(y * (1.0 / math.sqrt(2.0)))) + y * jnp.exp(-0.5 * y * y) * (1.0 / math.sqrt(2.0 * math.pi))


def _glu_fwd(u, y0, y1, dskip, w_glu, b_glu, tm):
    lp, w = u.shape

    def body(u_ref, y0_ref, y1_ref, d_ref, w_ref, b_ref, o_ref):
        y = u_ref[...] * d_ref[...] + y0_ref[...] + y1_ref[...]
        z = _gelu(y)
        t = _dot(z.astype(BF16), w_ref[...]) + b_ref[...]
        o_ref[...] = (z * _sigmoid(t)).astype(BF16)

    return pl.pallas_call(
        body, name="glu_fwd", grid=(lp // tm,),
        in_specs=[_row(tm, w), _row(tm, w), _row(tm, w), _full((1, w)), _full((w, w)), _full((1, w))],
        out_specs=_row(tm, w), out_shape=jax.ShapeDtypeStruct((lp, w), BF16),
        compiler_params=_params(("parallel",)),
    )(u, y0, y1, dskip, w_glu, b_glu)


def _glu_bwd(dyssm, u, y0, y1, dskip, w_glu, b_glu, tm):
    lp, w = u.shape

    def body(g_ref, u_ref, y0_ref, y1_ref, d_ref, w_ref, b_ref, dy_ref, dw_ref, db_ref, dd_ref):
        first = pl.program_id(0) == 0
        uu = u_ref[...]
        y = uu * d_ref[...] + y0_ref[...] + y1_ref[...]
        z = _gelu(y)
        zb = z.astype(BF16)
        sg = _sigmoid(_dot(zb, w_ref[...]) + b_ref[...])
        g = g_ref[...]
        dt = g * z * sg * (1.0 - sg)
        dtb = dt.astype(BF16)
        dz = g * sg + _dot_nt(dtb, w_ref[...])
        dy = dz * _gelu_grad(y)
        dy_ref[...] = dy
        dw = _dot_tn(zb, dtb)

        @pl.when(first)
        def _():
            dw_ref[...] = dw

        @pl.when(jnp.logical_not(first))
        def _():
            dw_ref[...] += dw

        _acc_rows(db_ref, dt, first)
        _acc_rows(dd_ref, dy * uu, first)

    return pl.pallas_call(
        body, name="glu_bwd", grid=(lp // tm,),
        in_specs=[_row(tm, w), _row(tm, w), _row(tm, w), _row(tm, w), _full((1, w)), _full((w, w)), _full((1, w))],
        out_specs=[_row(tm, w), _full((w, w)), _full((1, w)), _full((1, w))],
        out_shape=[jax.ShapeDtypeStruct((lp, w), F32), jax.ShapeDtypeStruct((w, w), F32),
                   jax.ShapeDtypeStruct((1, w), F32), jax.ShapeDtypeStruct((1, w), F32)],
        compiler_params=_params(("arbitrary",)),
    )(dyssm, u, y0, y1, dskip, w_glu, b_glu)


def _merge_fwd(yssm, o, gates, xin, wsp4, wap, wo, tm):
    lp, d = xin.shape
    w = yssm.shape[1]
    ns = d // 4

    def body(y_ref, o_ref, g_ref, x_ref, wsp_ref, wap_ref, wo_ref, h_ref, m_ref):
        yb = y_ref[...]
        ms = jnp.concatenate([_dot(yb, wsp_ref[s]) for s in range(4)], axis=1)
        ma = _dot(o_ref[...], wap_ref[...])
        merged = (_sigmoid(g_ref[:, :d]) * ms + _sigmoid(g_ref[:, d:]) * ma).astype(BF16)
        m_ref[...] = merged
        h_ref[...] = x_ref[...] + _dot(merged, wo_ref[...])

    return pl.pallas_call(
        body, name="merge_fwd", grid=(lp // tm,),
        in_specs=[_row(tm, w), _row(tm, d), _row(tm, 2 * d), _row(tm, d),
                  _full((4, w, ns)), _full((d, d)), _full((d, d))],
        out_specs=[_row(tm, d), _row(tm, d)],
        out_shape=[jax.ShapeDtypeStruct((lp, d), F32), jax.ShapeDtypeStruct((lp, d), BF16)],
        compiler_params=_params(("parallel",)),
    )(yssm, o, gates, xin, wsp4, wap, wo)


def _merge_bwd(dh1, yssm, o, gates, wsp4, wap, wo, sel, tm):
    lp, d = dh1.shape
    w = yssm.shape[1]
    ns = d // 4
    nsel = sel.shape[1]

    def body(dh_ref, y_ref, o_ref, g_ref, wsp_ref, wap_ref, wo_ref, sel_ref,
             dg_ref, dms_ref, dma_ref, dy_ref, do_ref, dl_ref, dhb_ref):
        dhb = dh_ref[...].astype(BF16)
        dhb_ref[...] = dhb
        dm = _dot_nt(dhb, wo_ref[...])
        yb = y_ref[...]
        ob = o_ref[...]
        ms = jnp.concatenate([_dot(yb, wsp_ref[s]) for s in range(4)], axis=1)
        ma = _dot(ob, wap_ref[...])
        ss = _sigmoid(g_ref[:, :d])
        sa = _sigmoid(g_ref[:, d:])
        dg_ref[:, :d] = dm * ms * ss * (1.0 - ss)
        dg_ref[:, d:] = dm * ma * sa * (1.0 - sa)
        dms = (dm * ss).astype(BF16)
        dma = (dm * sa).astype(BF16)
        dms_ref[...] = dms
        dma_ref[...] = dma
        dy = _dot_nt(dms[:, :ns], wsp_ref[0])
        for s in range(1, 4):
            dy += _dot_nt(dms[:, s * ns:(s + 1) * ns], wsp_ref[s])
        dy_ref[...] = dy
        do = _dot_nt(dma, wap_ref[...])
        do_ref[...] = do.astype(BF16)
        dl_ref[...] = _split_dot(do * ob.astype(F32), sel_ref[...])

    return pl.pallas_call(
        body, name="merge_bwd", grid=(lp // tm,),
        in_specs=[_row(tm, d), _row(tm, w), _row(tm, d), _row(tm, 2 * d),
                  _full((4, w, ns)), _full((d, d)), _full((d, d)), _full((d, nsel))],
        out_specs=[_row(tm, 2 * d), _row(tm, d), _row(tm, d), _row(tm, w), _row(tm, d), _row(tm, nsel), _row(tm, d)],
        out_shape=[jax.ShapeDtypeStruct((lp, 2 * d), F32), jax.ShapeDtypeStruct((lp, d), BF16),
                   jax.ShapeDtypeStruct((lp, d), BF16), jax.ShapeDtypeStruct((lp, w), F32),
                   jax.ShapeDtypeStruct((lp, d), BF16), jax.ShapeDtypeStruct((lp, nsel), F32),
                   jax.ShapeDtypeStruct((lp, d), BF16)],
        compiler_params=_params(("parallel",)),
    )(dh1, yssm, o, gates, wsp4, wap, wo, sel)


def _mlp_in(h1, g, w4, tm):
    lp, d = h1.shape

    def body(x_ref, g_ref, w_ref, r_ref):
        h = _rms(x_ref[...], g_ref[...]).astype(BF16)
        for s in range(4):
            r_ref[:, s * d:(s + 1) * d] = jnp.maximum(_dot(h, w_ref[s]), 0.0).astype(BF16)

    return pl.pallas_call(
        body, name="mlp_in", grid=(lp // tm,),
        in_specs=[_row(tm, d), _full((1, d)), _full((4, d, d))],
        out_specs=_row(tm, 4 * d), out_shape=jax.ShapeDtypeStruct((lp, 4 * d), BF16),
        compiler_params=_params(("parallel",)),
    )(h1, g, w4)


def _square_bf16(r):
    rf = r.astype(F32)
    return (rf * rf).astype(BF16)


def _mlp_out(h1, r, w2, tm):
    lp, d = h1.shape
    ff = r.shape[1]

    def body(x_ref, r_ref, w_ref, o_ref):
        o_ref[...] = x_ref[...] + _dot(_square_bf16(r_ref[...]), w_ref[...])

    return pl.pallas_call(
        body, name="mlp_out", grid=(lp // tm,),
        in_specs=[_row(tm, d), _row(tm, ff), _full((ff, d))],
        out_specs=_row(tm, d), out_shape=jax.ShapeDtypeStruct((lp, d), F32),
        compiler_params=_params(("parallel",)),
    )(h1, r, w2)


def _final_loss(h3, g, tgt, rowmask, tm):
    lp, d = h3.shape

    def body(x_ref, g_ref, t_ref, m_ref, loss_ref, dx_ref, dg_ref):
        first = pl.program_id(0) == 0
        x = x_ref[...]
        gg = g_ref[...]
        err = (_rms(x, gg) - t_ref[...]) * m_ref[...]
        part = 0.5 * jnp.sum(jnp.sum(err * err, axis=1, keepdims=True), axis=0, keepdims=True) * (1.0 / d)
        part = jnp.broadcast_to(part, (SUBLANES, LANES))

        @pl.when(first)
        def _():
            loss_ref[...] = part

        @pl.when(jnp.logical_not(first))
        def _():
            loss_ref[...] += part

        dx, dgr = _rms_bwd(err * (1.0 / d), x, gg)
        dx_ref[...] = dx
        _acc_rows(dg_ref, dgr, first)

    return pl.pallas_call(
        body, name="final_loss", grid=(lp // tm,),
        in_specs=[_row(tm, d), _full((1, d)), _row(tm, d), _row(tm, 1)],
        out_specs=[_full((SUBLANES, LANES)), _row(tm, d), _full((1, d))],
        out_shape=[jax.ShapeDtypeStruct((SUBLANES, LANES), F32), jax.ShapeDtypeStruct((lp, d), F32),
                   jax.ShapeDtypeStruct((1, d), F32)],
        compiler_params=_params(("arbitrary",)),
    )(h3, g, tgt, rowmask)


def _mlp_bwd_a(dh3, r, w2, tm):
    lp, d = dh3.shape
    ff = r.shape[1]

    def body(dh_ref, r_ref, w_ref, dz_ref, dhb_ref):
        dhb = dh_ref[...].astype(BF16)
        dhb_ref[...] = dhb
        da = _dot_nt(dhb, w_ref[...])
        dz_ref[...] = (da * (2.0 * r_ref[...].astype(F32))).astype(BF16)

    return pl.pallas_call(
        body, name="mlp_bwd_a", grid=(lp // tm,),
        in_specs=[_row(tm, d), _row(tm, ff), _full((ff, d))],
        out_specs=[_row(tm, ff), _row(tm, d)],
        out_shape=[jax.ShapeDtypeStruct((lp, ff), BF16), jax.ShapeDtypeStruct((lp, d), BF16)],
        compiler_params=_params(("parallel",)),
    )(dh3, r, w2)


def _mlp_bwd_b(dz, dh3, h1, g, w4, tm):
    lp, d = h1.shape

    def body(dz_ref, dh_ref, x_ref, g_ref, w_ref, dx_ref, dg_ref):
        first = pl.program_id(0) == 0
        dh2 = _dot_nt(dz_ref[:, :d], w_ref[0])
        for s in range(1, 4):
            dh2 += _dot_nt(dz_ref[:, s * d:(s + 1) * d], w_ref[s])
        dx, dgr = _rms_bwd(dh2, x_ref[...], g_ref[...])
        dx_ref[...] = dh_ref[...] + dx
        _acc_rows(dg_ref, dgr, first)

    return pl.pallas_call(
        body, name="mlp_bwd_b", grid=(lp // tm,),
        in_specs=[_row(tm, 4 * d), _row(tm, d), _row(tm, d), _full((1, d)), _full((4, d, d))],
        out_specs=[_row(tm, d), _full((1, d))],
        out_shape=[jax.ShapeDtypeStruct((lp, d), F32), jax.ShapeDtypeStruct((1, d), F32)],
        compiler_params=_params(("arbitrary",)),
    )(dz, dh3, h1, g, w4)


def _in_proj_bwd(dyv, du0, du1, dskip, dqkv, dgates, dres, xin, g, w4, tm):
    lp, d = xin.shape
    hd = d // 2

    def body(dy_ref, a_ref, b_ref, ds_ref, dq_ref, dgt_ref, dr_ref, x_ref, g_ref, w_ref, dx_ref, dg_ref, dp_ref):
        first = pl.program_id(0) == 0
        du = (dy_ref[...] * ds_ref[...] + a_ref[...] + b_ref[...]).astype(BF16)
        dq = dq_ref[...].astype(BF16)
        dgt = dgt_ref[...].astype(BF16)
        dp_ref[:, :hd] = du
        dp_ref[:, hd:2 * d] = dq
        dp_ref[:, 2 * d:] = dgt
        dh = _dot_nt(du, w_ref[0, :, :hd]) + _dot_nt(dq[:, :hd], w_ref[0, :, hd:])
        dh += _dot_nt(dq[:, hd:], w_ref[1])
        dh += _dot_nt(dgt[:, :d], w_ref[2]) + _dot_nt(dgt[:, d:], w_ref[3])
        dx, dgr = _rms_bwd(dh, x_ref[...], g_ref[...])
        dx_ref[...] = dr_ref[...] + dx
        _acc_rows(dg_ref, dgr, first)

    return pl.pallas_call(
        body, name="in_proj_bwd", grid=(lp // tm,),
        in_specs=[_row(tm, hd), _row(tm, hd), _row(tm, hd), _full((1, hd)), _row(tm, 3 * hd), _row(tm, 2 * d),
                  _row(tm, d), _row(tm, d), _full((1, d)), _full((4, d, d))],
        out_specs=[_row(tm, d), _full((1, d)), _row(tm, 4 * d)],
        out_shape=[jax.ShapeDtypeStruct((lp, d), F32), jax.ShapeDtypeStruct((1, d), F32),
                   jax.ShapeDtypeStruct((lp, 4 * d), BF16)],
        compiler_params=_params(("arbitrary",)),
    )(dyv, du0, du1, dskip, dqkv, dgates, dres, xin, g, w4)


def _wgrad(a, dy, nshard, tm, tn, name, gain=None, square=False):
    lp, k = a.shape
    n = dy.shape[1]
    ns = n // nshard
    assert ns % tn == 0
    per = ns // tn

    def body(*refs):
        if gain is not None:
            a_ref, g_ref, dy_ref, o_ref = refs
            at = _rms(a_ref[...], g_ref[...]).astype(BF16)
        else:
            a_ref, dy_ref, o_ref = refs
            at = _square_bf16(a_ref[...]) if square else a_ref[...]
        i = pl.program_id(1)
        acc = _dot_tn(at, dy_ref[...])

        @pl.when(i == 0)
        def _():
            o_ref[0] = acc

        @pl.when(i != 0)
        def _():
            o_ref[0] += acc

    in_specs = [pl.BlockSpec((tm, k), lambda j, i: (i, 0))]
    args = [a]
    if gain is not None:
        in_specs.append(pl.BlockSpec((1, k), lambda j, i: (0, 0)))
        args.append(gain)
    in_specs.append(pl.BlockSpec((tm, tn), lambda j, i: (i, j)))
    args.append(dy)
    return pl.pallas_call(
        body, name=name, grid=(n // tn, lp // tm), in_specs=in_specs,
        out_specs=pl.BlockSpec((1, k, tn), lambda j, i: (j // per, 0, j % per)),
        out_shape=jax.ShapeDtypeStruct((nshard, k, ns), F32),
        compiler_params=_params(("parallel", "arbitrary")),
    )(*args)


def _head_tables(d):
    idx = np.arange(LANES)
    mean = (idx[:, None] // HEAD_DIM == idx[None, :] // HEAD_DIM).astype(np.float32) / HEAD_DIM
    n_heads = d // HEAD_DIM
    kvh = n_heads // GQA_REP
    c = np.arange(d)
    col = np.arange(kvh * LANES)
    head_of_col = (col // LANES) * GQA_REP + (col % LANES)
    sel = ((c[:, None] // HEAD_DIM == head_of_col[None, :]) & ((col % LANES) < GQA_REP)[None, :]).astype(np.float32)
    return jnp.asarray(mean, BF16), jnp.asarray(sel, BF16)


def _swap_pairs(y):
    lane = lax.broadcasted_iota(jnp.int32, y.shape, 1)
    return jnp.where(lane % 2 == 0, pltpu.roll(y, LANES - 1, 1), pltpu.roll(y, 1, 1))


def _qk_prep(qkv, cos_t, sin_t, qg, kg, mean_m, tm):
    lp, wq = qkv.shape
    d = wq * 2 // 3
    kvw = d // 4
    kvh = kvw // HEAD_DIM
    scale = HEAD_DIM ** -0.5

    def body(x_ref, c_ref, s_ref, qg_ref, kg_ref, m_ref, q_ref, k_ref, v_ref):
        cs, sn, mm = c_ref[...], s_ref[...], m_ref[...]
        for b in range((d + kvw) // LANES):
            x = x_ref[:, b * LANES:(b + 1) * LANES]
            gg = qg_ref[...] if b < d // LANES else kg_ref[...]
            y = x * lax.rsqrt(_split_dot(x * x, mm) + NORM_EPS) * gg
            out = y * cs + _swap_pairs(y) * sn
            if b < d // LANES:
                q_ref[:, b * LANES:(b + 1) * LANES] = (out * scale).astype(BF16)
            else:
                kb = b - d // LANES
                k_ref[2 * kb] = out[:, :HEAD_DIM].astype(BF16)
                k_ref[2 * kb + 1] = out[:, HEAD_DIM:].astype(BF16)
        lane = lax.broadcasted_iota(jnp.int32, (tm, LANES - HEAD_DIM), 1)
        ones_col = (lane == 0).astype(BF16)
        for h in range(kvh):
            vh = x_ref[:, d + kvw + h * HEAD_DIM:d + kvw + (h + 1) * HEAD_DIM].astype(BF16)
            v_ref[h] = jnp.concatenate([vh, ones_col], axis=1)

    k_spec = pl.BlockSpec((kvh, tm, HEAD_DIM), lambda i: (0, i, 0))
    v_spec = pl.BlockSpec((kvh, tm, LANES), lambda i: (0, i, 0))
    return pl.pallas_call(
        body, name="qk_prep", grid=(lp // tm,),
        in_specs=[_row(tm, wq), _row(tm, LANES), _row(tm, LANES), _full((1, LANES)), _full((1, LANES)),
                  _full((LANES, LANES))],
        out_specs=[_row(tm, d), k_spec, v_spec],
        out_shape=[jax.ShapeDtypeStruct((lp, d), BF16), jax.ShapeDtypeStruct((kvh, lp, HEAD_DIM), BF16),
                   jax.ShapeDtypeStruct((kvh, lp, LANES), BF16)],
        compiler_params=_params(("parallel",)),
    )(qkv, cos_t, sin_t, qg, kg, mean_m)


def _qk_bwd(qkv, dq, dk, dv, cos_t, sin_t, qg, kg, mean_m, tm):
    lp, wq = qkv.shape
    d = wq * 2 // 3
    kvw = d // 4
    kvh = kvw // HEAD_DIM
    scale = HEAD_DIM ** -0.5

    def body(x_ref, dq_ref, dk_ref, dv_ref, c_ref, s_ref, qg_ref, kg_ref, m_ref, o_ref, dqg_ref, dkg_ref):
        first = pl.program_id(0) == 0
        cs, sn, mm = c_ref[...], s_ref[...], m_ref[...]
        sums = [None, None]
        for b in range((d + kvw) // LANES):
            is_q = b < d // LANES
            x = x_ref[:, b * LANES:(b + 1) * LANES]
            gg = qg_ref[...] if is_q else kg_ref[...]
            r = lax.rsqrt(_split_dot(x * x, mm) + NORM_EPS)
            nrm = x * r
            if is_q:
                dout = dq_ref[:, b * LANES:(b + 1) * LANES] * scale
            else:
                kb = b - d // LANES
                dout = jnp.concatenate([dk_ref[2 * kb], dk_ref[2 * kb + 1]], axis=1)
            dy = dout * cs + _swap_pairs(dout * sn)
            part = jnp.sum(dy * nrm, axis=0, keepdims=True)
            sums[0 if is_q else 1] = part if sums[0 if is_q else 1] is None else sums[0 if is_q else 1] + part
            dn = dy * gg
            o_ref[:, b * LANES:(b + 1) * LANES] = r * (dn - nrm * _split_dot(dn * nrm, mm))
        for h in range(kvh):
            o_ref[:, d + kvw + h * HEAD_DIM:d + kvw + (h + 1) * HEAD_DIM] = dv_ref[h]
        for ref, s in ((dqg_ref, sums[0]), (dkg_ref, sums[1])):
            s = s + pltpu.roll(s, HEAD_DIM, 1)

            @pl.when(first)
            def _(ref=ref, s=s):
                ref[...] = s

            @pl.when(jnp.logical_not(first))
            def _(ref=ref, s=s):
                ref[...] += s

    kv_spec = pl.BlockSpec((kvh, tm, HEAD_DIM), lambda i: (0, i, 0))
    return pl.pallas_call(
        body, name="qk_bwd", grid=(lp // tm,),
        in_specs=[_row(tm, wq), _row(tm, d), kv_spec, kv_spec, _row(tm, LANES), _row(tm, LANES),
                  _full((1, LANES)), _full((1, LANES)), _full((LANES, LANES))],
        out_specs=[_row(tm, wq), _full((1, LANES)), _full((1, LANES))],
        out_shape=[jax.ShapeDtypeStruct((lp, wq), F32), jax.ShapeDtypeStruct((1, LANES), F32),
                   jax.ShapeDtypeStruct((1, LANES), F32)],
        compiler_params=_params(("arbitrary",)),
    )(qkv, dq, dk, dv, cos_t, sin_t, qg, kg, mean_m)


def _attn_fwd(q, k, v, kbias, tq, tk):
    lp, d = q.shape
    kvh = k.shape[0]
    rw = GQA_REP * HEAD_DIM
    nk = lp // tk

    def body(q_ref, k_ref, v_ref, kb_ref, o_ref, lse_ref, m_s, acc_s):
        j = pl.program_id(2)

        @pl.when(j == 0)
        def _():
            m_s[...] = jnp.full(m_s.shape, MASK_VALUE, F32)
            acc_s[...] = jnp.zeros(acc_s.shape, F32)

        def heads(masked):
            kk, vv = k_ref[0], v_ref[0]
            ss = [_dot_nt(q_ref[:, h * HEAD_DIM:(h + 1) * HEAD_DIM], kk) for h in range(GQA_REP)]
            ps, alphas = [], []
            for h in range(GQA_REP):
                s = ss[h] + kb_ref[...] if masked else ss[h]
                m_prev = m_s[h]
                m_new = jnp.maximum(m_prev, jnp.max(s, axis=1, keepdims=True))
                ps.append(jnp.exp(s - m_new[:, :1]).astype(BF16))
                alphas.append(jnp.exp(m_prev - m_new))
                m_s[h] = m_new
            for h in range(GQA_REP):
                acc_s[h] = acc_s[h] * alphas[h] + _dot(ps[h], vv)

        pl.when(j != nk - 1)(functools.partial(heads, False))
        pl.when(j == nk - 1)(functools.partial(heads, True))

        @pl.when(j == nk - 1)
        def _():
            lane = lax.broadcasted_iota(jnp.int32, (tq, LANES), 1)
            lse = jnp.zeros((tq, LANES), F32)
            outs = []
            for h in range(GQA_REP):
                acc = acc_s[h]
                l = acc[:, HEAD_DIM:HEAD_DIM + 1]
                outs.append(acc[:, :HEAD_DIM] / l)
                lse = jnp.where(lane == h, m_s[h][:, :1] + jnp.log(l), lse)
            o_ref[...] = jnp.concatenate(outs, axis=1).astype(BF16)
            lse_ref[...] = lse

    return pl.pallas_call(
        body, name="attn_fwd", grid=(kvh, lp // tq, nk),
        in_specs=[pl.BlockSpec((tq, rw), lambda g, i, j: (i, g)),
                  pl.BlockSpec((1, tk, HEAD_DIM), lambda g, i, j: (g, j, 0)),
                  pl.BlockSpec((1, tk, LANES), lambda g, i, j: (g, j, 0)),
                  pl.BlockSpec((1, tk), lambda g, i, j: (0, j))],
        out_specs=[pl.BlockSpec((tq, rw), lambda g, i, j: (i, g)),
                   pl.BlockSpec((tq, LANES), lambda g, i, j: (i, g))],
        out_shape=[jax.ShapeDtypeStruct((lp, d), BF16), jax.ShapeDtypeStruct((lp, kvh * LANES), F32)],
        scratch_shapes=[pltpu.VMEM((GQA_REP, tq, LANES), F32), pltpu.VMEM((GQA_REP, tq, LANES), F32)],
        compiler_params=_params(("parallel", "parallel", "arbitrary")),
    )(q, k, v, kbias)


def _attn_bwd(q, k, v, kbias, do, lse, delta, tq, tk):
    lp, d = q.shape
    kvh = k.shape[0]
    rw = GQA_REP * HEAD_DIM
    nq = lp // tq

    def body(q_ref, k_ref, v_ref, kb_ref, do_ref, lse_ref, dl_ref, dq_ref, dk_ref, dv_ref, dk_s, dv_s):
        j = pl.program_id(1)
        i = pl.program_id(2)

        @pl.when(jnp.logical_and(i == 0, j == 0))
        def _():
            dq_ref[...] = jnp.zeros(dq_ref.shape, F32)

        @pl.when(i == 0)
        def _():
            dk_s[...] = jnp.zeros(dk_s.shape, F32)
            dv_s[...] = jnp.zeros(dv_s.shape, F32)

        def heads(masked):
            kk, vv = k_ref[0], v_ref[0][:, :HEAD_DIM]
            lse, dl = lse_ref[...], dl_ref[...]
            dqs = []
            for h in range(GQA_REP):
                qh = q_ref[:, h * HEAD_DIM:(h + 1) * HEAD_DIM]
                doh = do_ref[:, h * HEAD_DIM:(h + 1) * HEAD_DIM]
                s = _dot_nt(qh, kk)
                if masked:
                    s = s + kb_ref[...]
                p = jnp.exp(s - lse[:, h:h + 1])
                ds = (p * (_dot_nt(doh, vv) - dl[:, h:h + 1])).astype(BF16)
                dv_s[...] += _dot_tn(p.astype(BF16), doh)
                dk_s[...] += _dot_tn(ds, qh)
                dqs.append(_dot(ds, kk))
            rows = pl.ds(pl.multiple_of(i * tq, tq), tq)
            dq_ref[rows, :] += jnp.concatenate(dqs, axis=1)

        nk = lp // tk
        pl.when(j != nk - 1)(functools.partial(heads, False))
        pl.when(j == nk - 1)(functools.partial(heads, True))

        @pl.when(i == nq - 1)
        def _():
            dk_ref[0] = dk_s[...]
            dv_ref[0] = dv_s[...]

    return pl.pallas_call(
        body, name="attn_bwd", grid=(kvh, lp // tk, nq),
        in_specs=[pl.BlockSpec((tq, rw), lambda g, j, i: (i, g)),
                  pl.BlockSpec((1, tk, HEAD_DIM), lambda g, j, i: (g, j, 0)),
                  pl.BlockSpec((1, tk, LANES), lambda g, j, i: (g, j, 0)),
                  pl.BlockSpec((1, tk), lambda g, j, i: (0, j)),
                  pl.BlockSpec((tq, rw), lambda g, j, i: (i, g)),
                  pl.BlockSpec((tq, LANES), lambda g, j, i: (i, g)),
                  pl.BlockSpec((tq, LANES), lambda g, j, i: (i, g))],
        out_specs=[pl.BlockSpec((lp, rw), lambda g, j, i: (0, g)),
                   pl.BlockSpec((1, tk, HEAD_DIM), lambda g, j, i: (g, j, 0)),
                   pl.BlockSpec((1, tk, HEAD_DIM), lambda g, j, i: (g, j, 0))],
        out_shape=[jax.ShapeDtypeStruct((lp, d), F32), jax.ShapeDtypeStruct((kvh, lp, HEAD_DIM), F32),
                   jax.ShapeDtypeStruct((kvh, lp, HEAD_DIM), F32)],
        scratch_shapes=[pltpu.VMEM((tk, HEAD_DIM), F32), pltpu.VMEM((tk, HEAD_DIM), F32)],
        compiler_params=_params(("parallel", "arbitrary", "arbitrary")),
    )(q, k, v, kbias, do, lse, delta)


def _ssm_math(a_re, a_im, log_dt, bt_re, bt_im):
    dt = jnp.exp(log_dt)
    lam_re = jnp.minimum(a_re, EIG_RE_MAX)
    lam_im = a_im
    mag = jnp.exp(lam_re * dt)
    ang = lam_im * dt
    lb_re = mag * jnp.cos(ang)
    lb_im = mag * jnp.sin(ang)
    num_re = lb_re - 1.0
    num_im = lb_im
    den = lam_re * lam_re + lam_im * lam_im
    f_re = (num_re * lam_re + num_im * lam_im) / den
    f_im = (num_im * lam_re - num_re * lam_im) / den
    bb_re = f_re[:, None, :] * bt_re - f_im[:, None, :] * bt_im
    bb_im = f_re[:, None, :] * bt_im + f_im[:, None, :] * bt_re
    return lb_re, lb_im, bb_re, bb_im


def _ssm_discretize(a_re, a_im, log_dt, bt_re, bt_im):
    nd, g, n = a_re.shape
    p = bt_re.shape[2]

    def body(ar_ref, ai_ref, ld_ref, br_ref, bi_ref, bbr_ref, bbi_ref, pr_ref, pi_ref, hr_ref, hi_ref):
        lb_re, lb_im, bb_re, bb_im = _ssm_math(ar_ref[0], ai_ref[0], ld_ref[0], br_ref[0], bi_ref[0])
        bbr_ref[0] = bb_re
        bbi_ref[0] = bb_im
        cr, ci = lb_re, lb_im
        for k in range(KSTEPS):
            pr_ref[0, k] = cr
            pi_ref[0, k] = ci
            if k < KSTEPS - 1:
                cr, ci = cr * lb_re - ci * lb_im, cr * lb_im + ci * lb_re
        for t in range(2):
            cr, ci = cr * cr - ci * ci, 2.0 * cr * ci
            hr_ref[0, t] = cr
            hi_ref[0, t] = ci

    s3 = pl.BlockSpec((1, g, n), lambda i: (i, 0, 0))
    s4 = pl.BlockSpec((1, g, p, n), lambda i: (i, 0, 0, 0))
    sp = pl.BlockSpec((1, KSTEPS, g, n), lambda i: (i, 0, 0, 0))
    sh = pl.BlockSpec((1, 2, g, n), lambda i: (i, 0, 0, 0))
    return pl.pallas_call(
        body, name="ssm_discretize", grid=(nd,),
        in_specs=[s3, s3, pl.BlockSpec((1, g, 1), lambda i: (i, 0, 0)), s4, s4],
        out_specs=[s4, s4, sp, sp, sh, sh],
        out_shape=[jax.ShapeDtypeStruct((nd, g, p, n), F32)] * 2 + [jax.ShapeDtypeStruct((nd, KSTEPS, g, n), F32)] * 2
        + [jax.ShapeDtypeStruct((nd, 2, g, n), F32)] * 2,
        compiler_params=_params(("parallel",)),
    )(a_re, a_im, log_dt, bt_re, bt_im)


def _ssm_param_bwd(a_re, a_im, log_dt, bt_re, bt_im, dlb_re, dlb_im, dbb_re, dbb_im):
    nd, g, n = a_re.shape
    p = bt_re.shape[2]

    def body(ar_ref, ai_ref, ld_ref, br_ref, bi_ref, c0_ref, c1_ref, c2_ref, c3_ref,
             o0_ref, o1_ref, o2_ref, o3_ref, o4_ref):
        _, vjp = jax.vjp(_ssm_math, ar_ref[0], ai_ref[0], ld_ref[0], br_ref[0], bi_ref[0])
        outs = vjp((c0_ref[0], c1_ref[0], c2_ref[0], c3_ref[0]))
        for ref, val in zip((o0_ref, o1_ref, o2_ref, o3_ref, o4_ref), outs):
            ref[0] = val

    s3 = pl.BlockSpec((1, g, n), lambda i: (i, 0, 0))
    s1 = pl.BlockSpec((1, g, 1), lambda i: (i, 0, 0))
    s4 = pl.BlockSpec((1, g, p, n), lambda i: (i, 0, 0, 0))
    return pl.pallas_call(
        body, name="ssm_param_bwd", grid=(nd,),
        in_specs=[s3, s3, s1, s4, s4, s3, s3, s4, s4],
        out_specs=[s3, s3, s1, s4, s4],
        out_shape=[jax.ShapeDtypeStruct((nd, g, n), F32)] * 2 + [jax.ShapeDtypeStruct((nd, g, 1), F32)]
        + [jax.ShapeDtypeStruct((nd, g, p, n), F32)] * 2,
        compiler_params=_params(("parallel",)),
    )(a_re, a_im, log_dt, bt_re, bt_im, dlb_re, dlb_im, dbb_re, dbb_im)


def _cmul(ar, ai, xr, xi, conj):
    if conj:
        return ar * xr + ai * xi, ar * xi - ai * xr
    return ar * xr - ai * xi, ar * xi + ai * xr


def _scan_chunk(buf, tab, carry, ein, nj, rev, conj, base=0):
    ks = list(range(KSTEPS))
    if rev:
        ks = ks[::-1]
    sub = lax.broadcasted_iota(jnp.int32, (SUBLANES, SCAN_LANES), 0)
    edge = sub == (SUBLANES - 1 if rev else 0)

    def step(j, _):
        jr, ji = j, nj + j
        ar, ai = tab[base, jr], tab[base, ji]
        hr = jnp.zeros((SUBLANES, SCAN_LANES), F32)
        hi = jnp.zeros((SUBLANES, SCAN_LANES), F32)
        for k in ks:
            rows = pl.ds(k * SUBLANES, SUBLANES)
            pr, pi_ = _cmul(ar, ai, hr, hi, conj)
            hr = pr + buf[jr, rows, :]
            hi = pi_ + buf[ji, rows, :]
            buf[jr, rows, :] = hr
            buf[ji, rows, :] = hi
        shift = SUBLANES - 1 if rev else 1
        er = jnp.where(edge, carry[jr], pltpu.roll(hr, shift, 0))
        ei = jnp.where(edge, carry[ji], pltpu.roll(hi, shift, 0))
        for t, dist in enumerate((1, 2, 4)):
            sh = SUBLANES - dist if rev else dist
            pr, pi_ = _cmul(tab[base + 1 + t, jr], tab[base + 1 + t, ji], pltpu.roll(er, sh, 0), pltpu.roll(ei, sh, 0), conj)
            er, ei = er + pr, ei + pi_
        ein[jr] = er
        ein[ji] = ei
        pr, pi_ = _cmul(tab[base + 4 + KSTEPS - 1, jr], tab[base + 4 + KSTEPS - 1, ji], er, ei, conj)
        last = 0 if rev else SUBLANES - 1
        carry[jr] = jnp.broadcast_to((hr + pr)[last:last + 1, :], (SUBLANES, SCAN_LANES))
        carry[ji] = jnp.broadcast_to((hi + pi_)[last:last + 1, :], (SUBLANES, SCAN_LANES))
        for n, k in enumerate(ks):
            rows = pl.ds(k * SUBLANES, SUBLANES)
            pr, pi_ = _cmul(tab[base + 4 + n, jr], tab[base + 4 + n, ji], er, ei, conj)
            buf[jr, rows, :] += pr
            buf[ji, rows, :] += pi_
        return 0

    lax.fori_loop(0, nj, step, 0)


def _to_blocks(buf, val, nblk):
    for b in range(nblk):
        buf[b] = val[:, b * SCAN_LANES:(b + 1) * SCAN_LANES]


def _from_blocks(buf, nblk):
    return jnp.concatenate([buf[b] for b in range(nblk)], axis=1)


def _ssm_fwd(u, wb, wct, tab, rev, name):
    lp, w = u.shape
    s2 = wb.shape[1]
    nj = s2 // (2 * SCAN_LANES)
    nc = lp // CHUNK
    ntab = tab.shape[0]
    cidx = (lambda c: nc - 1 - c) if rev else (lambda c: c)

    def body(u_ref, wb_ref, wct_ref, tab_ref, y_ref, ck_ref, buf, carry, ein):
        @pl.when(pl.program_id(0) == 0)
        def _():
            carry[...] = jnp.zeros(carry.shape, F32)

        _to_blocks(buf, _dot(u_ref[...].astype(BF16), wb_ref[...]), 2 * nj)
        ck_ref[0] = carry[...]
        _scan_chunk(buf, tab_ref, carry, ein, nj, rev, False)
        y_ref[...] = _dot_nt(_from_blocks(buf, 2 * nj).astype(BF16), wct_ref[...])

    return pl.pallas_call(
        body, name=name, grid=(nc,),
        in_specs=[pl.BlockSpec((CHUNK, w), lambda c: (cidx(c), 0)), _full((w, s2)), _full((w, s2)),
                  _full((ntab, 2 * nj, SUBLANES, SCAN_LANES))],
        out_specs=[pl.BlockSpec((CHUNK, w), lambda c: (cidx(c), 0)),
                   pl.BlockSpec((1, 2 * nj, SUBLANES, SCAN_LANES), lambda c: (cidx(c), 0, 0, 0))],
        out_shape=[jax.ShapeDtypeStruct((lp, w), F32), jax.ShapeDtypeStruct((nc, 2 * nj, SUBLANES, SCAN_LANES), F32)],
        scratch_shapes=[pltpu.VMEM((2 * nj, CHUNK, SCAN_LANES), F32), pltpu.VMEM((2 * nj, SUBLANES, SCAN_LANES), F32),
                        pltpu.VMEM((2 * nj, SUBLANES, SCAN_LANES), F32)],
        compiler_params=_params(("arbitrary",)),
    )(u, wb, wct, tab)


def _ssm_bwd(u, dy, ckpt, wb, wct, tab, rev, name):
    lp, w = u.shape
    s2 = wb.shape[1]
    nj = s2 // (2 * SCAN_LANES)
    nc = lp // CHUNK
    ntab = tab.shape[0]
    cidx = (lambda c: c) if rev else (lambda c: nc - 1 - c)

    def body(u_ref, dy_ref, ck_ref, wb_hbm, wct_hbm, tab_hbm, du_ref, dbb_ref, dcc_ref, dlb_ref,
             wb_ref, wct_ref, tab_ref, dwb_ref, dwc_ref, xs, ls, xcar, lcar, xin, lin):
        c = pl.program_id(0)

        @pl.when(c == 0)
        def _():
            pltpu.sync_copy(wb_hbm, wb_ref)
            pltpu.sync_copy(wct_hbm, wct_ref)
            pltpu.sync_copy(tab_hbm, tab_ref)
            lcar[...] = jnp.zeros(lcar.shape, F32)
            dwb_ref[...] = jnp.zeros(dwb_ref.shape, F32)
            dwc_ref[...] = jnp.zeros(dwc_ref.shape, F32)
            dlb_ref[...] = jnp.zeros(dlb_ref.shape, F32)

        ub = u_ref[...].astype(BF16)
        dyb = dy_ref[...].astype(BF16)
        _to_blocks(xs, _dot(ub, wb_ref[...]), 2 * nj)
        xcar[...] = ck_ref[0]
        _scan_chunk(xs, tab_ref, xcar, xin, nj, rev, False)
        _to_blocks(ls, _dot(dyb, wct_ref[...]), 2 * nj)
        _scan_chunk(ls, tab_ref, lcar, lin, nj, not rev, True, base=ntab // 2)
        xb = _from_blocks(xs, 2 * nj).astype(BF16)
        lb = _from_blocks(ls, 2 * nj).astype(BF16)
        dwc_ref[...] += _dot_tn(dyb, xb)
        dwb_ref[...] += _dot_tn(ub, lb)
        du_ref[...] = _dot_nt(lb, wb_ref[...])

        def step(j, _):
            jr, ji = j, nj + j
            ar = jnp.zeros((SUBLANES, SCAN_LANES), F32)
            ai = jnp.zeros((SUBLANES, SCAN_LANES), F32)
            for k in range(KSTEPS):
                kp = k + 1 if rev else k - 1
                rows = pl.ds(k * SUBLANES, SUBLANES)
                if 0 <= kp < KSTEPS:
                    prow = pl.ds(kp * SUBLANES, SUBLANES)
                    xr, xi = xs[jr, prow, :], xs[ji, prow, :]
                else:
                    xr, xi = xin[jr], xin[ji]
                lr, li = ls[jr, rows, :], ls[ji, rows, :]
                ar += lr * xr + li * xi
                ai += li * xr - lr * xi
            dlb_ref[jr] += ar
            dlb_ref[ji] += ai
            return 0

        lax.fori_loop(0, nj, step, 0)

        @pl.when(c == nc - 1)
        def _():
            for b in range(2 * nj):
                dlb_ref[b] = jnp.broadcast_to(jnp.sum(dlb_ref[b], axis=0, keepdims=True), (SUBLANES, SCAN_LANES))
            for g in range(w // SSM_GROUP):
                rows = slice(g * SSM_GROUP, (g + 1) * SSM_GROUP)
                for part in range(2):
                    cols = slice(part * (s2 // 2) + g * SSM_STATE, part * (s2 // 2) + (g + 1) * SSM_STATE)
                    dbb_ref[part, rows, :] = dwb_ref[rows, cols]
                    dcc_ref[part, rows, :] = dwc_ref[rows, cols]

    st = (2 * nj, SUBLANES, SCAN_LANES)
    return pl.pallas_call(
        body, name=name, grid=(nc,),
        in_specs=[pl.BlockSpec((CHUNK, w), lambda c: (cidx(c), 0)), pl.BlockSpec((CHUNK, w), lambda c: (cidx(c), 0)),
                  pl.BlockSpec((1,) + st, lambda c: (cidx(c), 0, 0, 0)), _ANY, _ANY, _ANY],
        out_specs=[pl.BlockSpec((CHUNK, w), lambda c: (cidx(c), 0)), _full((2, w, SSM_STATE)),
                   _full((2, w, SSM_STATE)), _full(st)],
        out_shape=[jax.ShapeDtypeStruct((lp, w), F32), jax.ShapeDtypeStruct((2, w, SSM_STATE), F32),
                   jax.ShapeDtypeStruct((2, w, SSM_STATE), F32), jax.ShapeDtypeStruct(st, F32)],
        scratch_shapes=[pltpu.VMEM((w, s2), BF16), pltpu.VMEM((w, s2), BF16), pltpu.VMEM((ntab,) + st, F32),
                        pltpu.VMEM((w, s2), F32), pltpu.VMEM((w, s2), F32),
                        pltpu.VMEM((2 * nj, CHUNK, SCAN_LANES), F32), pltpu.VMEM((2 * nj, CHUNK, SCAN_LANES), F32),
                        pltpu.VMEM(st, F32), pltpu.VMEM(st, F32), pltpu.VMEM(st, F32), pltpu.VMEM(st, F32)],
        compiler_params=_params(("arbitrary",)),
    )(u, dy, ckpt, wb, wct, tab)


def _embed_block_diag(t_re, t_im):
    g, p, n = t_re.shape
    eye = jnp.eye(g, dtype=t_re.dtype)
    parts = [jnp.einsum('gpn,gh->gphn', t, eye).reshape(g * p, g * n) for t in (t_re, t_im)]
    return jnp.concatenate(parts, axis=1)


def _scan_layout(x, nj):
    lead = x.shape[:-1]
    x = x.reshape(lead + (nj, 1, SCAN_LANES))
    return jnp.broadcast_to(x, lead + (nj, SUBLANES, SCAN_LANES))


def _scan_tables(pw_re, pw_im, hi_re, hi_im, rev):
    s = pw_re.shape[1] * pw_re.shape[2]
    nj = s // SCAN_LANES
    sub = jnp.arange(SUBLANES).reshape(1, SUBLANES, 1)

    def pair(re, im, mask=None):
        re, im = _scan_layout(re.reshape(s), nj), _scan_layout(im.reshape(s), nj)
        if mask is not None:
            re, im = jnp.where(mask, re, 0.0), jnp.where(mask, im, 0.0)
        return jnp.concatenate([re, im], axis=0)

    def live(dist):
        return (sub < SUBLANES - dist) if rev else (sub >= dist)

    rows = [pair(pw_re[0], pw_im[0]),
            pair(pw_re[KSTEPS - 1], pw_im[KSTEPS - 1], live(1)),
            pair(hi_re[0], hi_im[0], live(2)),
            pair(hi_re[1], hi_im[1], live(4))]
    rows += [pair(pw_re[k], pw_im[k]) for k in range(KSTEPS)]
    return jnp.stack(rows, axis=0)


def _adamw(w, g, m, v, tm):
    r, c = w.shape
    c1 = 1.0 - ADAM_B1 ** ADAM_STEP
    c2 = 1.0 - ADAM_B2 ** ADAM_STEP

    def body(w_ref, g_ref, m_ref, v_ref, d_ref, nm_ref, nv_ref):
        gg = g_ref[...]
        nm = ADAM_B1 * m_ref[...] + (1.0 - ADAM_B1) * gg
        nv = ADAM_B2 * v_ref[...] + (1.0 - ADAM_B2) * (gg * gg)
        nm_ref[...] = nm
        nv_ref[...] = nv
        d_ref[...] = -ADAM_LR * ((nm / c1) / (jnp.sqrt(nv / c2) + ADAM_EPS) + ADAM_WD * w_ref[...])

    spec = _row(tm, c)
    return pl.pallas_call(
        body, name="adamw", grid=(r // tm,), in_specs=[spec] * 4, out_specs=[spec] * 3,
        out_shape=[jax.ShapeDtypeStruct((r, c), F32)] * 3, compiler_params=_params(("parallel",)),
    )(w, g, m, v)


def _pair_sum(g42, got, core, out_dtype, tm, name):
    _, _, r, c = g42.shape

    def body(core_ref, a_ref, b_ref, o_ref):
        o_ref[...] = (a_ref[...] + b_ref[...]).astype(out_dtype)

    grid_spec = pltpu.PrefetchScalarGridSpec(
        num_scalar_prefetch=1, grid=(4, r // tm),
        in_specs=[pl.BlockSpec((1, None, tm, c), lambda s, i, core_ref: (s, core_ref[0], i, 0)),
                  pl.BlockSpec((1, tm, c), lambda s, i, core_ref: (s, i, 0))],
        out_specs=pl.BlockSpec((1, tm, c), lambda s, i, core_ref: (s, i, 0)))
    return pl.pallas_call(
        body, name=name, grid_spec=grid_spec, out_shape=jax.ShapeDtypeStruct((4, r, c), out_dtype),
        compiler_params=_params(("parallel", "parallel")),
    )(core, g42, got)


def _sum4(a, tm, name):
    _, r, c = a.shape

    def body(a_ref, o_ref):
        o_ref[...] = ((a_ref[0].astype(F32) + a_ref[1].astype(F32)) + a_ref[2].astype(F32)) + a_ref[3].astype(F32)

    return pl.pallas_call(
        body, name=name, grid=(r // tm,), in_specs=[pl.BlockSpec((4, tm, c), lambda i: (0, i, 0))],
        out_specs=_row(tm, c), out_shape=jax.ShapeDtypeStruct((r, c), F32), compiler_params=_params(("parallel",)),
    )(a)


_ANY = pl.BlockSpec(memory_space=pl.ANY)


def _all_gather8(blocks, name):
    n = len(blocks)

    def body(*refs):
        xs, outs = refs[:n], refs[n:2 * n]
        send_sems, recv_sems, local_sems = refs[2 * n:]
        x, y, c = lax.axis_index("x"), lax.axis_index("y"), lax.axis_index("c")
        me, sibling = (x, y, c), (x, y, 1 - c)
        chips = [(1 - x, y), (x, 1 - y), (1 - x, 1 - y)]

        def slot(t, px, py, pc):
            return outs[t].at[4 * px + 2 * py + pc]

        def copy(t, k, blk, to, src=None):
            return pltpu.make_async_remote_copy(
                src_ref=slot(t, *blk) if src is None else src, dst_ref=slot(t, *blk),
                send_sem=send_sems.at[t, k], recv_sem=recv_sems.at[t, k], device_id=to, device_id_type=MESH_ID)

        mine = [pltpu.make_async_copy(xs[t], slot(t, *me), local_sems.at[t]) for t in range(n)]
        for cp in mine:
            cp.start()
        first = [[copy(t, 0, me, sibling, src=xs[t])]
                 + [copy(t, 1 + j, me, (*chip, c), src=xs[t]) for j, chip in enumerate(chips)] for t in range(n)]
        for t in range(n):
            for cp in first[t]:
                cp.start()
        passed = [[copy(t, 4 + j, (*chip, c), sibling) for j, chip in enumerate(chips)] for t in range(n)]
        for j, chip in enumerate(chips):
            for t in range(n):
                copy(t, 1 + j, (*chip, c), me).wait_recv()
                passed[t][j].start()
        for t in range(n):
            copy(t, 0, sibling, me).wait_recv()
        for j, chip in enumerate(chips):
            for t in range(n):
                copy(t, 4 + j, (*chip, 1 - c), me).wait_recv()
        for t in range(n):
            for cp in first[t] + passed[t]:
                cp.wait_send()
        for cp in mine:
            cp.wait()

    return pl.pallas_call(
        body, name=name, out_shape=[jax.ShapeDtypeStruct((8,) + b.shape, b.dtype) for b in blocks],
        in_specs=[_ANY] * n, out_specs=[_ANY] * n,
        scratch_shapes=[pltpu.SemaphoreType.DMA((n, 7)), pltpu.SemaphoreType.DMA((n, 7)),
                        pltpu.SemaphoreType.DMA((n,))],
    )(*blocks)


def _pair_exchange(gs, name):
    n = len(gs)

    def body(*refs):
        g_refs, outs = refs[:n], refs[n:2 * n]
        send_sems, recv_sems = refs[2 * n:]
        x, y, c = lax.axis_index("x"), lax.axis_index("y"), lax.axis_index("c")
        cps = [pltpu.make_async_remote_copy(
            src_ref=g_refs[t].at[:, 1 - c], dst_ref=outs[t], send_sem=send_sems.at[t], recv_sem=recv_sems.at[t],
            device_id=(x, y, 1 - c), device_id_type=MESH_ID) for t in range(n)]
        for cp in cps:
            cp.start()
        for cp in cps:
            cp.wait()

    return pl.pallas_call(
        body, name=name,
        out_shape=[jax.ShapeDtypeStruct((g.shape[0],) + g.shape[2:], g.dtype) for g in gs],
        in_specs=[_ANY] * n, out_specs=[_ANY] * n,
        scratch_shapes=[pltpu.SemaphoreType.DMA((n,)), pltpu.SemaphoreType.DMA((n,))],
    )(*gs)


def _chip_scatter(ps, name):
    n = len(ps)

    def body(*refs):
        p_refs, outs = refs[:n], refs[n:2 * n]
        send_sems, recv_sems, local_sems = refs[2 * n:]
        x, y, c = lax.axis_index("x"), lax.axis_index("y"), lax.axis_index("c")
        mine = 2 * x + y
        chips = [(1 - x, y), (x, 1 - y), (1 - x, 1 - y)]
        own = [pltpu.make_async_copy(p_refs[t].at[mine], outs[t].at[mine], local_sems.at[t]) for t in range(n)]
        for cp in own:
            cp.start()

        def copy(t, k, src_slab, dst_slab, chip):
            return pltpu.make_async_remote_copy(
                src_ref=p_refs[t].at[src_slab], dst_ref=outs[t].at[dst_slab], send_sem=send_sems.at[t, k],
                recv_sem=recv_sems.at[t, k], device_id=(*chip, c), device_id_type=MESH_ID)

        sends = [[copy(t, k, 2 * cx + cy, mine, (cx, cy)) for k, (cx, cy) in enumerate(chips)] for t in range(n)]
        for k in range(3):
            for t in range(n):
                sends[t][k].start()
        for k, (cx, cy) in enumerate(chips):
            for t in range(n):
                copy(t, k, mine, 2 * cx + cy, (cx, cy)).wait_recv()
        for t in range(n):
            for cp in sends[t]:
                cp.wait_send()
        for cp in own:
            cp.wait()

    return pl.pallas_call(
        body, name=name, out_shape=[jax.ShapeDtypeStruct(p.shape, p.dtype) for p in ps],
        in_specs=[_ANY] * n, out_specs=[_ANY] * n,
        scratch_shapes=[pltpu.SemaphoreType.DMA((n, 3)), pltpu.SemaphoreType.DMA((n, 3)),
                        pltpu.SemaphoreType.DMA((n,))],
    )(*ps)


def _pair_gather(rs, name):
    n = len(rs)

    def body(*refs):
        r_refs, outs = refs[:n], refs[n:2 * n]
        send_sems, recv_sems, local_sems = refs[2 * n:]
        x, y, c = lax.axis_index("x"), lax.axis_index("y"), lax.axis_index("c")
        own = [pltpu.make_async_copy(r_refs[t], outs[t].at[c], local_sems.at[t]) for t in range(n)]
        for cp in own:
            cp.start()

        def copy(t, slab):
            return pltpu.make_async_remote_copy(
                src_ref=r_refs[t], dst_ref=outs[t].at[slab], send_sem=send_sems.at[t], recv_sem=recv_sems.at[t],
                device_id=(x, y, 1 - c), device_id_type=MESH_ID)

        sends = [copy(t, c) for t in range(n)]
        for cp in sends:
            cp.start()
        for t in range(n):
            copy(t, 1 - c).wait_recv()
        for cp in sends:
            cp.wait_send()
        for cp in own:
            cp.wait()

    return pl.pallas_call(
        body, name=name, out_shape=[jax.ShapeDtypeStruct((2,) + r.shape, r.dtype) for r in rs],
        in_specs=[_ANY] * n, out_specs=[_ANY] * n,
        scratch_shapes=[pltpu.SemaphoreType.DMA((n,)), pltpu.SemaphoreType.DMA((n,)), pltpu.SemaphoreType.DMA((n,))],
    )(*rs)


PACK_COLS = 1024
BIG = (("meta_tokens", 1), ("w_in", 1), ("w_glu", 0), ("w_ssm_proj", 1), ("w_attn_proj", 0), ("w_out", 0),
       ("w_mlp_in", 1), ("w_mlp_out", 0))
SMALL = ("norm_mix_g", "ssm_a_re", "ssm_a_im", "ssm_log_dt", "ssm_b_re", "ssm_b_im", "ssm_c_re", "ssm_c_im",
         "ssm_d", "b_glu", "q_norm_g", "k_norm_g", "norm_mlp_g", "norm_final_g")


def _pad_rows(flat, mult_rows):
    n = flat.shape[0]
    unit = PACK_COLS * mult_rows
    total = -(-n // unit) * unit
    return jnp.pad(flat, (0, total - n)).reshape(total // PACK_COLS, PACK_COLS)


def _half(t, c):
    return lax.dynamic_slice_in_dim(t, c * (t.shape[0] // 2), t.shape[0] // 2, 0)


def _gather_weights(shards, c):
    names = [name for name, _ in BIG]
    blocks = [_half(shards[name], c) if name == "meta_tokens" else _half(shards[name], c).astype(BF16)
              for name in names]
    got = _all_gather8(blocks, "weight_all_gather")
    return {name: g.reshape((4, 2 * g.shape[1]) + g.shape[2:]) for name, g in zip(names, got)}


def _reduce_gradients(big4, small_flat, c):
    names = [name for name, _ in BIG]
    n_small = small_flat.shape[0]
    unit = 8 * SUBLANES * PACK_COLS
    k = -(-n_small // unit) * unit
    small42 = jnp.pad(small_flat, (0, k - n_small)).reshape(4, 2, k // (8 * PACK_COLS), PACK_COLS)
    g42 = [big4[name].reshape(4, 2, big4[name].shape[1] // 2, big4[name].shape[2]) for name in names] + [small42]
    labels = names + ["small"]
    wire = [F32 if name == "meta_tokens" else BF16 for name in names] + [F32]
    tiles = [_pick_tile(g.shape[2], 256, SUBLANES if dt == F32 else 2 * SUBLANES) for g, dt in zip(g42, wire)]
    core = c.astype(jnp.int32).reshape(1)
    got = _pair_exchange(g42, "grad_pair_exchange")
    pair = [_pair_sum(g, o, core, dt, tm, "pair_sum_" + lb) for g, o, dt, tm, lb in zip(g42, got, wire, tiles, labels)]
    by_src = _chip_scatter(pair, "grad_chip_scatter")
    red = [_sum4(b, tm, "chip_sum_" + lb) for b, tm, lb in zip(by_src, tiles, labels)]
    both = _pair_gather(red[:-1], "grad_pair_gather")
    out = {name: b.reshape(2 * b.shape[1], b.shape[2]) for name, b in zip(names, both)}
    small = _all_gather8([red[-1]], "small_grad_all_gather")[0].reshape(-1)[:n_small]
    return out, small


def _to_chunk_order(a):
    lp = a.shape[0]
    rest = a.shape[1:]
    a = a.reshape((lp // CHUNK, SUBLANES, KSTEPS) + rest)
    return a.swapaxes(1, 2).reshape((lp,) + rest)


def _from_chunk_order(a):
    lp = a.shape[0]
    rest = a.shape[1:]
    a = a.reshape((lp // CHUNK, KSTEPS, SUBLANES) + rest)
    return a.swapaxes(1, 2).reshape((lp,) + rest)


def _rope_tables(l_total, lp):
    n_real = l_total - N_META
    pos = np.arange(n_real)
    row_id = (pos // GRID_W).astype(np.float32)
    col_id = (pos % GRID_W).astype(np.float32)
    ppa = HEAD_DIM // 4
    inv_freq = (ROPE_THETA ** (-np.arange(ppa, dtype=np.float64) / ppa)).astype(np.float32)
    ang = np.concatenate([row_id[:, None] * inv_freq, col_id[:, None] * inv_freq], axis=-1)
    ang = np.concatenate([np.zeros((N_META, HEAD_DIM // 2), np.float32), ang,
                          np.zeros((lp - l_total, HEAD_DIM // 2), np.float32)], axis=0).astype(np.float64)
    cos = np.repeat(np.cos(ang), 2, axis=1)
    sin = np.repeat(np.sin(ang), 2, axis=1) * np.tile(np.asarray([-1.0, 1.0]), HEAD_DIM // 2)
    reps = (1, LANES // HEAD_DIM)
    return np.tile(cos, reps).astype(np.float32), np.tile(sin, reps).astype(np.float32)


def kernel(x, meta_tokens, norm_mix_g, w_in, ssm_a_re, ssm_a_im, ssm_log_dt, ssm_b_re, ssm_b_im, ssm_c_re, ssm_c_im, ssm_d, w_glu, b_glu, q_norm_g, k_norm_g, w_ssm_proj, w_attn_proj, w_out, norm_mlp_g, w_mlp_in, w_mlp_out, norm_final_g, loss_target, m_meta_tokens, m_norm_mix_g, m_w_in, m_ssm_a_re, m_ssm_a_im, m_ssm_log_dt, m_ssm_b_re, m_ssm_b_im, m_ssm_c_re, m_ssm_c_im, m_ssm_d, m_w_glu, m_b_glu, m_q_norm_g, m_k_norm_g, m_w_ssm_proj, m_w_attn_proj, m_w_out, m_norm_mlp_g, m_w_mlp_in, m_w_mlp_out, m_norm_final_g, v_meta_tokens, v_norm_mix_g, v_w_in, v_ssm_a_re, v_ssm_a_im, v_ssm_log_dt, v_ssm_b_re, v_ssm_b_im, v_ssm_c_re, v_ssm_c_im, v_ssm_d, v_w_glu, v_b_glu, v_q_norm_g, v_k_norm_g, v_w_ssm_proj, v_w_attn_proj, v_w_out, v_norm_mlp_g, v_w_mlp_in, v_w_mlp_out, v_norm_final_g):
    args = dict(locals())
    names = list(dict.fromkeys([n for n, _ in BIG] + list(SMALL)))
    order = ['meta_tokens', 'norm_mix_g', 'w_in', 'ssm_a_re', 'ssm_a_im', 'ssm_log_dt', 'ssm_b_re', 'ssm_b_im',
             'ssm_c_re', 'ssm_c_im', 'ssm_d', 'w_glu', 'b_glu', 'q_norm_g', 'k_norm_g', 'w_ssm_proj', 'w_attn_proj',
             'w_out', 'norm_mlp_g', 'w_mlp_in', 'w_mlp_out', 'norm_final_g']
    assert sorted(names) == sorted(order)
    c_idx = lax.axis_index("c")

    seq, d = x.shape[1], x.shape[2]
    l_total = seq + N_META
    lp = -(-l_total // SEQ_ALIGN) * SEQ_ALIGN
    hd = d // 2
    n_groups = hd // SSM_GROUP
    n_state = n_groups * SSM_STATE
    nj = n_state // SCAN_LANES
    kvh = d // HEAD_DIM // GQA_REP

    shard2d = {}
    for name, _ in BIG:
        t = args[name]
        shard2d[name] = t.reshape(t.shape[-2], t.shape[-1])
    full = _gather_weights(shard2d, c_idx)
    meta_full = jnp.transpose(full["meta_tokens"], (1, 0, 2)).reshape(N_META, d)
    w_in4 = full["w_in"]
    w_mlp_in4 = full["w_mlp_in"]
    w_ssm_proj4 = full["w_ssm_proj"]
    w_glu_f = full["w_glu"].reshape(hd, hd)
    w_attn_proj_f = full["w_attn_proj"].reshape(d, d)
    w_out_f = full["w_out"].reshape(d, d)
    w_mlp_out_f = full["w_mlp_out"].reshape(4 * d, d)

    xin = jnp.concatenate([meta_full, x[0], jnp.zeros((lp - l_total, d), F32)], axis=0)
    xin = _to_chunk_order(xin)
    tgt = _to_chunk_order(jnp.pad(loss_target[0], ((N_META, lp - l_total), (0, 0))))
    pos = np.arange(lp)
    rowmask = jnp.asarray(_to_chunk_order(((pos >= N_META) & (pos < l_total)).astype(np.float32)[:, None]))
    kbias = jnp.asarray(_to_chunk_order(np.where(pos < l_total, 0.0, MASK_VALUE).astype(np.float32)[:, None])
                        .reshape(1, lp))
    cos_t, sin_t = (jnp.asarray(_to_chunk_order(t)) for t in _rope_tables(l_total, lp))
    mean_m, sel = _head_tables(d)

    tm = _pick_tile(lp, 320)
    tm_big = _pick_tile(lp, 640)
    tq = _pick_tile(lp, ATTN_TILE, LANES)
    assert lp - tq <= (l_total // CHUNK) * CHUNK
    g_mix = norm_mix_g.reshape(1, d)
    g_mlp = norm_mlp_g.reshape(1, d)
    g_fin = norm_final_g.reshape(1, d)
    qg = jnp.tile(q_norm_g.reshape(1, HEAD_DIM), (1, LANES // HEAD_DIM))
    kg = jnp.tile(k_norm_g.reshape(1, HEAD_DIM), (1, LANES // HEAD_DIM))
    dskip = ssm_d.reshape(1, hd)
    bglu = b_glu.reshape(1, hd)

    a_re, a_im = ssm_a_re[0], ssm_a_im[0]
    log_dt = ssm_log_dt[0][..., None]
    bt_re = jnp.swapaxes(ssm_b_re[0], 2, 3)
    bt_im = jnp.swapaxes(ssm_b_im[0], 2, 3)
    bb_re, bb_im, pw_re, pw_im, hi_re, hi_im = _ssm_discretize(a_re, a_im, log_dt, bt_re, bt_im)
    wb = [_embed_block_diag(bb_re[i], bb_im[i]).astype(BF16) for i in range(2)]
    wct = [_embed_block_diag(ssm_c_re[0, i], -ssm_c_im[0, i]).astype(BF16) for i in range(2)]
    tabs = [_scan_tables(pw_re[i], pw_im[i], hi_re[i], hi_im[i], rev=(i == 1)) for i in range(2)]
    tabs_adj = [_scan_tables(pw_re[i], pw_im[i], hi_re[i], hi_im[i], rev=(i == 0)) for i in range(2)]

    u, qkv, gates = _in_proj(xin, g_mix, w_in4, tm)
    y0, ck0 = _ssm_fwd(u, wb[0], wct[0], tabs[0], False, "ssm_fwd_0")
    y1, ck1 = _ssm_fwd(u, wb[1], wct[1], tabs[1], True, "ssm_fwd_1")
    yssm = _glu_fwd(u, y0, y1, dskip, w_glu_f, bglu, tm_big)
    q, k, v = _qk_prep(qkv, cos_t, sin_t, qg, kg, mean_m, tm)
    o, lse = _attn_fwd(q, k, v, kbias, tq, tq)
    h1, merged = _merge_fwd(yssm, o, gates, xin, w_ssm_proj4, w_attn_proj_f, w_out_f, tm)
    r = _mlp_in(h1, g_mlp, w_mlp_in4, tm)
    h3 = _mlp_out(h1, r, w_mlp_out_f, tm)
    loss_tile, dh3, d_gfin = _final_loss(h3, g_fin, tgt, rowmask, tm_big)
    loss = lax.psum(loss_tile[0, 0], ("x", "y", "c"))

    dz, dh3b = _mlp_bwd_a(dh3, r, w_mlp_out_f, tm)
    dh1, d_gmlp = _mlp_bwd_b(dz, dh3, h1, g_mlp, w_mlp_in4, tm)
    dgates, dms, dma, dyssm, do, delta, dh1b = _merge_bwd(dh1, yssm, o, gates, w_ssm_proj4, w_attn_proj_f, w_out_f,
                                                          sel, tm)
    dyv, d_wglu, d_bglu, d_dskip = _glu_bwd(dyssm, u, y0, y1, dskip, w_glu_f, bglu, tm_big)
    du0, dbb0, dcc0, dlb0 = _ssm_bwd(u, dyv, ck0, wb[0], wct[0], _both(tabs[0], tabs_adj[0]), False, "ssm_bwd_0")
    du1, dbb1, dcc1, dlb1 = _ssm_bwd(u, dyv, ck1, wb[1], wct[1], _both(tabs[1], tabs_adj[1]), True, "ssm_bwd_1")
    dq, dk, dv = _attn_bwd(q, k, v, kbias, do, lse, delta, tq, tq)
    dqkv, d_qg, d_kg = _qk_bwd(qkv, dq, dk, dv, cos_t, sin_t, qg, kg, mean_m, tm)
    dxin, d_gmix, dproj = _in_proj_bwd(dyv, du0, du1, dskip, dqkv, dgates, dh1, xin, g_mix, w_in4, tm)

    tn = min(d, 1024)
    grads4 = {
        "w_in": _wgrad(xin, dproj, 4, tm_big, tn, "wgrad_in", gain=g_mix),
        "w_mlp_in": _wgrad(h1, dz, 4, tm_big, tn, "wgrad_mlp_in", gain=g_mlp),
        "w_mlp_out": _wgrad(r, dh3b, 1, tm_big, min(d, 256), "wgrad_mlp_out", square=True).reshape(4, d, d),
        "w_out": _wgrad(merged, dh1b, 1, tm_big, tn, "wgrad_out").reshape(4, d // 4, d),
        "w_attn_proj": _wgrad(o, dma, 1, tm_big, tn, "wgrad_attn_proj").reshape(4, d // 4, d),
        "w_ssm_proj": _wgrad(yssm, dms, 4, tm_big, d // 4, "wgrad_ssm_proj"),
        "w_glu": d_wglu.reshape(4, hd // 4, hd),
    }
    dx_nat = _from_chunk_order(dxin)
    grads4["meta_tokens"] = jnp.swapaxes(dx_nat[:N_META].reshape(N_META, 4, d // 4), 0, 1)
    grad_x = dx_nat[N_META:l_total][None]

    dlb = jnp.stack([dlb0, dlb1])[:, :, 0, :]
    dlb_re = dlb[:, :nj].reshape(2, n_groups, SSM_STATE)
    dlb_im = dlb[:, nj:].reshape(2, n_groups, SSM_STATE)
    gpn = (2, 2, n_groups, SSM_GROUP, SSM_STATE)
    dbb = jnp.stack([dbb0, dbb1]).reshape(gpn)
    dcc = jnp.stack([dcc0, dcc1]).reshape(gpn)
    d_are, d_aim, d_logdt, d_btre, d_btim = _ssm_param_bwd(a_re, a_im, log_dt, bt_re, bt_im, dlb_re, dlb_im,
                                                           dbb[:, 0], dbb[:, 1])
    small_grads = {
        "norm_mix_g": d_gmix, "ssm_a_re": d_are, "ssm_a_im": d_aim, "ssm_log_dt": d_logdt,
        "ssm_b_re": jnp.swapaxes(d_btre, 2, 3), "ssm_b_im": jnp.swapaxes(d_btim, 2, 3),
        "ssm_c_re": dcc[:, 0], "ssm_c_im": -dcc[:, 1],
        "ssm_d": d_dskip, "b_glu": d_bglu, "q_norm_g": d_qg[:, :HEAD_DIM], "k_norm_g": d_kg[:, :HEAD_DIM],
        "norm_mlp_g": d_gmlp, "norm_final_g": d_gfin,
    }
    small_flat = jnp.concatenate([small_grads[n].reshape(-1) for n in SMALL])

    red_big, red_small = _reduce_gradients(grads4, small_flat, c_idx)
    grad, delta_w, new_m, new_v = {}, {}, {}, {}
    for name, _ in BIG:
        w2 = shard2d[name]
        shp = args[name].shape
        g2 = red_big[name]
        t = _pick_tile(w2.shape[0], 256, 8)
        dl, nm, nv = _adamw(w2, g2, args["m_" + name].reshape(w2.shape), args["v_" + name].reshape(w2.shape), t)
        grad[name], delta_w[name], new_m[name], new_v[name] = (a.reshape(shp) for a in (g2, dl, nm, nv))

    def pack_small(prefix):
        flat = jnp.concatenate([args[prefix + n].reshape(-1) for n in SMALL])
        return _pad_rows(flat, SUBLANES)

    n_small = red_small.shape[0]
    gs = _pad_rows(red_small, SUBLANES)
    dl, nm, nv = _adamw(pack_small(""), gs, pack_small("m_"), pack_small("v_"), _pick_tile(gs.shape[0], 256, 8))
    off = 0
    for name in SMALL:
        shp = args[name].shape
        k = int(np.prod(shp))
        for dst, src in ((grad, gs), (delta_w, dl), (new_m, nm), (new_v, nv)):
            dst[name] = src.reshape(-1)[off:off + k].reshape(shp)
        off += k
    assert off == n_small

    return (loss, grad_x, *[grad[n] for n in order], *[delta_w[n] for n in order],
            *[new_m[n] for n in order], *[new_v[n] for n in order])


def _both(tab, tab_adj):
    return jnp.concatenate([tab, tab_adj], axis=0)
```

```python
import functools
import math

import numpy as np
import jax
import jax.numpy as jnp
from jax import lax
from jax.experimental import pallas as pl
from jax.experimental.pallas import tpu as pltpu

F32 = jnp.float32
BF16 = jnp.bfloat16

N_META = 16
GRID_W = 64
HEAD_DIM = 64
GQA_REP = 4
SSM_GROUP = 16
SSM_STATE = 64
ROPE_THETA = 10000.0
NORM_EPS = 1e-6
EIG_RE_MAX = -1e-4
ADAM_LR, ADAM_B1, ADAM_B2, ADAM_EPS, ADAM_WD, ADAM_STEP = 0.001, 0.9, 0.999, 1e-08, 0.01, 10

SUBLANES = 8
LANES = 128
CHUNK = 256
KSTEPS = CHUNK // SUBLANES
SCAN_LANES = 512
MXU_DIM = 256
SSM_BLOCK = MXU_DIM
SEQ_ALIGN = MXU_DIM
ATTN_TILE = 3 * MXU_DIM
VMEM_LIMIT = 56 << 20
MASK_VALUE = -1e30
MESH_ID = pl.DeviceIdType.MESH


def _dot(a, b):
    return jnp.dot(a, b, preferred_element_type=F32)


def _dot_nt(a, b):
    return lax.dot_general(a, b, (((1,), (1,)), ((), ())), preferred_element_type=F32)


def _dot_tn(a, b):
    return lax.dot_general(a, b, (((0,), (0,)), ((), ())), preferred_element_type=F32)


def _row(tm, width):
    return pl.BlockSpec((tm, width), lambda i: (i, 0))


def _full(shape):
    nd = len(shape)
    return pl.BlockSpec(shape, lambda i: (0,) * nd)


def _params(sem):
    return pltpu.CompilerParams(dimension_semantics=sem, vmem_limit_bytes=VMEM_LIMIT)


def _pick_tile(n, cap, mult=16):
    best = None
    for t in range(mult, min(n, cap) + 1, mult):
        if n % t == 0:
            best = t
    assert best is not None, (n, cap)
    return best


def _rstd(x):
    return lax.rsqrt(jnp.mean(x * x, axis=-1, keepdims=True) + NORM_EPS)


def _rms(x, g):
    return x * _rstd(x) * g


def _rms_bwd(dy, x, g):
    r = _rstd(x)
    xh = x * r
    gdy = dy * g
    dx = r * (gdy - xh * jnp.mean(gdy * xh, axis=-1, keepdims=True))
    return dx, dy * xh


def _split_dot(x, m):
    hi = x.astype(BF16)
    lo = (x - hi.astype(F32)).astype(BF16)
    return _dot(hi, m) + _dot(lo, m)


def _sigmoid(x):
    return 1.0 / (1.0 + jnp.exp(-x))


def _acc_rows(ref, val, first):
    s = jnp.sum(val, axis=0, keepdims=True)

    @pl.when(first)
    def _():
        ref[...] = s

    @pl.when(jnp.logical_not(first))
    def _():
        ref[...] += s


def _in_proj(xin, g, w4, tm):
    lp, d = xin.shape
    hd = d // 2

    def body(x_ref, g_ref, w_ref, u_ref, qkv_ref, gt_ref):
        h = _rms(x_ref[...], g_ref[...]).astype(BF16)
        p0 = _dot(h, w_ref[0])
        u_ref[...] = p0[:, :hd]
        qkv_ref[:, :hd] = p0[:, hd:]
        qkv_ref[:, hd:] = _dot(h, w_ref[1])
        gt_ref[:, :d] = _dot(h, w_ref[2])
        gt_ref[:, d:] = _dot(h, w_ref[3])

    return pl.pallas_call(
        body, name="in_proj", grid=(lp // tm,),
        in_specs=[_row(tm, d), _full((1, d)), _full((4, d, d))],
        out_specs=[_row(tm, hd), _row(tm, 3 * hd), _row(tm, 2 * d)],
        out_shape=[jax.ShapeDtypeStruct((lp, hd), F32), jax.ShapeDtypeStruct((lp, 3 * hd), F32),
                   jax.ShapeDtypeStruct((lp, 2 * d), F32)],
        compiler_params=_params(("parallel",)),
    )(xin, g, w4)


def _gelu(y):
    return 0.5 * y * (1.0 + lax.erf(y * (1.0 / math.sqrt(2.0))))


def _gelu_grad(y):
    return 0.5 * (1.0 + lax.erf(y * (1.0 / math.sqrt(2.0)))) + y * jnp.exp(-0.5 * y * y) * (1.0 / math.sqrt(2.0 * math.pi))


def _glu_fwd(u, y0, y1, dskip, w_glu, b_glu, tm):
    lp, w = u.shape

    def body(u_ref, y0_ref, y1_ref, d_ref, w_ref, b_ref, o_ref):
        y = u_ref[...] * d_ref[...] + y0_ref[...] + y1_ref[...]
        z = _gelu(y)
        t = _dot(z.astype(BF16), w_ref[...]) + b_ref[...]
        o_ref[...] = (z * _sigmoid(t)).astype(BF16)

    return pl.pallas_call(
        body, name="glu_fwd", grid=(lp // tm,),
        in_specs=[_row(tm, w), _row(tm, w), _row(tm, w), _full((1, w)), _full((w, w)), _full((1, w))],
        out_specs=_row(tm, w), out_shape=jax.ShapeDtypeStruct((lp, w), BF16),
        compiler_params=_params(("parallel",)),
    )(u, y0, y1, dskip, w_glu, b_glu)


def _glu_bwd(dyssm, u, y0, y1, dskip, w_glu, b_glu, tm):
    lp, w = u.shape

    def body(g_ref, u_ref, y0_ref, y1_ref, d_ref, w_ref, b_ref, dy_ref, dw_ref, db_ref, dd_ref):
        first = pl.program_id(0) == 0
        uu = u_ref[...]
        y = uu * d_ref[...] + y0_ref[...] + y1_ref[...]
        z = _gelu(y)
        zb = z.astype(BF16)
        sg = _sigmoid(_dot(zb, w_ref[...]) + b_ref[...])
        g = g_ref[...]
        dt = g * z * sg * (1.0 - sg)
        dtb = dt.astype(BF16)
        dz = g * sg + _dot_nt(dtb, w_ref[...])
        dy = dz * _gelu_grad(y)
        dy_ref[...] = dy
        dw = _dot_tn(zb, dtb)

        @pl.when(first)
        def _():
            dw_ref[...] = dw

        @pl.when(jnp.logical_not(first))
        def _():
            dw_ref[...] += dw

        _acc_rows(db_ref, dt, first)
        _acc_rows(dd_ref, dy * uu, first)

    return pl.pallas_call(
        body, name="glu_bwd", grid=(lp // tm,),
        in_specs=[_row(tm, w), _row(tm, w), _row(tm, w), _row(tm, w), _full((1, w)), _full((w, w)), _full((1, w))],
        out_specs=[_row(tm, w), _full((w, w)), _full((1, w)), _full((1, w))],
        out_shape=[jax.ShapeDtypeStruct((lp, w), F32), jax.ShapeDtypeStruct((w, w), F32),
                   jax.ShapeDtypeStruct((1, w), F32), jax.ShapeDtypeStruct((1, w), F32)],
        compiler_params=_params(("arbitrary",)),
    )(dyssm, u, y0, y1, dskip, w_glu, b_glu)


def _merge_fwd(yssm, o, gates, xin, wsp4, wap, wo, tm):
    lp, d = xin.shape
    w = yssm.shape[1]
    ns = d // 4

    def body(y_ref, o_ref, g_ref, x_ref, wsp_ref, wap_ref, wo_ref, h_ref, m_ref):
        yb = y_ref[...]
        ms = jnp.concatenate([_dot(yb, wsp_ref[s]) for s in range(4)], axis=1)
        ma = _dot(o_ref[...], wap_ref[...])
        merged = (_sigmoid(g_ref[:, :d]) * ms + _sigmoid(g_ref[:, d:]) * ma).astype(BF16)
        m_ref[...] = merged
        h_ref[...] = x_ref[...] + _dot(merged, wo_ref[...])

    return pl.pallas_call(
        body, name="merge_fwd", grid=(lp // tm,),
        in_specs=[_row(tm, w), _row(tm, d), _row(tm, 2 * d), _row(tm, d),
                  _full((4, w, ns)), _full((d, d)), _full((d, d))],
        out_specs=[_row(tm, d), _row(tm, d)],
        out_shape=[jax.ShapeDtypeStruct((lp, d), F32), jax.ShapeDtypeStruct((lp, d), BF16)],
        compiler_params=_params(("parallel",)),
    )(yssm, o, gates, xin, wsp4, wap, wo)


def _merge_bwd(dh1, yssm, o, gates, wsp4, wap, wo, sel, tm):
    lp, d = dh1.shape
    w = yssm.shape[1]
    ns = d // 4
    nsel = sel.shape[1]

    def body(dh_ref, y_ref, o_ref, g_ref, wsp_ref, wap_ref, wo_ref, sel_ref,
             dg_ref, dms_ref, dma_ref, dy_ref, do_ref, dl_ref, dhb_ref):
        dhb = dh_ref[...].astype(BF16)
        dhb_ref[...] = dhb
        dm = _dot_nt(dhb, wo_ref[...])
        yb = y_ref[...]
        ob = o_ref[...]
        ms = jnp.concatenate([_dot(yb, wsp_ref[s]) for s in range(4)], axis=1)
        ma = _dot(ob, wap_ref[...])
        ss = _sigmoid(g_ref[:, :d])
        sa = _sigmoid(g_ref[:, d:])
        dg_ref[:, :d] = dm * ms * ss * (1.0 - ss)
        dg_ref[:, d:] = dm * ma * sa * (1.0 - sa)
        dms = (dm * ss).astype(BF16)
        dma = (dm * sa).astype(BF16)
        dms_ref[...] = dms
        dma_ref[...] = dma
        dy = _dot_nt(dms[:, :ns], wsp_ref[0])
        for s in range(1, 4):
            dy += _dot_nt(dms[:, s * ns:(s + 1) * ns], wsp_ref[s])
        dy_ref[...] = dy
        do = _dot_nt(dma, wap_ref[...])
        do_ref[...] = do.astype(BF16)
        dl_ref[...] = _split_dot(do * ob.astype(F32), sel_ref[...])

    return pl.pallas_call(
        body, name="merge_bwd", grid=(lp // tm,),
        in_specs=[_row(tm, d), _row(tm, w), _row(tm, d), _row(tm, 2 * d),
                  _full((4, w, ns)), _full((d, d)), _full((d, d)), _full((d, nsel))],
        out_specs=[_row(tm, 2 * d), _row(tm, d), _row(tm, d), _row(tm, w), _row(tm, d), _row(tm, nsel), _row(tm, d)],
        out_shape=[jax.ShapeDtypeStruct((lp, 2 * d), F32), jax.ShapeDtypeStruct((lp, d), BF16),
                   jax.ShapeDtypeStruct((lp, d), BF16), jax.ShapeDtypeStruct((lp, w), F32),
                   jax.ShapeDtypeStruct((lp, d), BF16), jax.ShapeDtypeStruct((lp, nsel), F32),
                   jax.ShapeDtypeStruct((lp, d), BF16)],
        compiler_params=_params(("parallel",)),
    )(dh1, yssm, o, gates, wsp4, wap, wo, sel)


def _mlp_in(h1, g, w4, tm):
    lp, d = h1.shape

    def body(x_ref, g_ref, w_ref, r_ref):
        h = _rms(x_ref[...], g_ref[...]).astype(BF16)
        for s in range(4):
            r_ref[:, s * d:(s + 1) * d] = jnp.maximum(_dot(h, w_ref[s]), 0.0).astype(BF16)

    return pl.pallas_call(
        body, name="mlp_in", grid=(lp // tm,),
        in_specs=[_row(tm, d), _full((1, d)), _full((4, d, d))],
        out_specs=_row(tm, 4 * d), out_shape=jax.ShapeDtypeStruct((lp, 4 * d), BF16),
        compiler_params=_params(("parallel",)),
    )(h1, g, w4)


def _square_bf16(r):
    rf = r.astype(F32)
    return (rf * rf).astype(BF16)


def _mlp_out(h1, r, w2, tm):
    lp, d = h1.shape
    ff = r.shape[1]

    def body(x_ref, r_ref, w_ref, o_ref):
        o_ref[...] = x_ref[...] + _dot(_square_bf16(r_ref[...]), w_ref[...])

    return pl.pallas_call(
        body, name="mlp_out", grid=(lp // tm,),
        in_specs=[_row(tm, d), _row(tm, ff), _full((ff, d))],
        out_specs=_row(tm, d), out_shape=jax.ShapeDtypeStruct((lp, d), F32),
        compiler_params=_params(("parallel",)),
    )(h1, r, w2)


def _final_loss(h3, g, tgt, rowmask, tm):
    lp, d = h3.shape

    def body(x_ref, g_ref, t_ref, m_ref, loss_ref, dx_ref, dg_ref):
        first = pl.program_id(0) == 0
        x = x_ref[...]
        gg = g_ref[...]
        err = (_rms(x, gg) - t_ref[...]) * m_ref[...]
        part = 0.5 * jnp.sum(jnp.sum(err * err, axis=1, keepdims=True), axis=0, keepdims=True) * (1.0 / d)
        part = jnp.broadcast_to(part, (SUBLANES, LANES))

        @pl.when(first)
        def _():
            loss_ref[...] = part

        @pl.when(jnp.logical_not(first))
        def _():
            loss_ref[...] += part

        dx, dgr = _rms_bwd(err * (1.0 / d), x, gg)
        dx_ref[...] = dx
        _acc_rows(dg_ref, dgr, first)

    return pl.pallas_call(
        body, name="final_loss", grid=(lp // tm,),
        in_specs=[_row(tm, d), _full((1, d)), _row(tm, d), _row(tm, 1)],
        out_specs=[_full((SUBLANES, LANES)), _row(tm, d), _full((1, d))],
        out_shape=[jax.ShapeDtypeStruct((SUBLANES, LANES), F32), jax.ShapeDtypeStruct((lp, d), F32),
                   jax.ShapeDtypeStruct((1, d), F32)],
        compiler_params=_params(("arbitrary",)),
    )(h3, g, tgt, rowmask)


def _mlp_bwd_a(dh3, r, w2, tm):
    lp, d = dh3.shape
    ff = r.shape[1]

    def body(dh_ref, r_ref, w_ref, dz_ref, dhb_ref):
        dhb = dh_ref[...].astype(BF16)
        dhb_ref[...] = dhb
        da = _dot_nt(dhb, w_ref[...])
        dz_ref[...] = (da * (2.0 * r_ref[...].astype(F32))).astype(BF16)

    return pl.pallas_call(
        body, name="mlp_bwd_a", grid=(lp // tm,),
        in_specs=[_row(tm, d), _row(tm, ff), _full((ff, d))],
        out_specs=[_row(tm, ff), _row(tm, d)],
        out_shape=[jax.ShapeDtypeStruct((lp, ff), BF16), jax.ShapeDtypeStruct((lp, d), BF16)],
        compiler_params=_params(("parallel",)),
    )(dh3, r, w2)


def _mlp_bwd_b(dz, dh3, h1, g, w4, tm):
    lp, d = h1.shape

    def body(dz_ref, dh_ref, x_ref, g_ref, w_ref, dx_ref, dg_ref):
        first = pl.program_id(0) == 0
        dh2 = _dot_nt(dz_ref[:, :d], w_ref[0])
        for s in range(1, 4):
            dh2 += _dot_nt(dz_ref[:, s * d:(s + 1) * d], w_ref[s])
        dx, dgr = _rms_bwd(dh2, x_ref[...], g_ref[...])
        dx_ref[...] = dh_ref[...] + dx
        _acc_rows(dg_ref, dgr, first)

    return pl.pallas_call(
        body, name="mlp_bwd_b", grid=(lp // tm,),
        in_specs=[_row(tm, 4 * d), _row(tm, d), _row(tm, d), _full((1, d)), _full((4, d, d))],
        out_specs=[_row(tm, d), _full((1, d))],
        out_shape=[jax.ShapeDtypeStruct((lp, d), F32), jax.ShapeDtypeStruct((1, d), F32)],
        compiler_params=_params(("arbitrary",)),
    )(dz, dh3, h1, g, w4)


def _in_proj_bwd(dyv, du0, du1, dskip, dqkv, dgates, dres, xin, g, w4, tm):
    lp, d = xin.shape
    hd = d // 2

    def body(dy_ref, a_ref, b_ref, ds_ref, dq_ref, dgt_ref, dr_ref, x_ref, g_ref, w_ref, dx_ref, dg_ref, dp_ref):
        first = pl.program_id(0) == 0
        du = (dy_ref[...] * ds_ref[...] + a_ref[...] + b_ref[...]).astype(BF16)
        dq = dq_ref[...].astype(BF16)
        dgt = dgt_ref[...].astype(BF16)
        dp_ref[:, :hd] = du
        dp_ref[:, hd:2 * d] = dq
        dp_ref[:, 2 * d:] = dgt
        dh = _dot_nt(du, w_ref[0, :, :hd]) + _dot_nt(dq[:, :hd], w_ref[0, :, hd:])
        dh += _dot_nt(dq[:, hd:], w_ref[1])
        dh += _dot_nt(dgt[:, :d], w_ref[2]) + _dot_nt(dgt[:, d:], w_ref[3])
        dx, dgr = _rms_bwd(dh, x_ref[...], g_ref[...])
        dx_ref[...] = dr_ref[...] + dx
        _acc_rows(dg_ref, dgr, first)

    return pl.pallas_call(
        body, name="in_proj_bwd", grid=(lp // tm,),
        in_specs=[_row(tm, hd), _row(tm, hd), _row(tm, hd), _full((1, hd)), _row(tm, 3 * hd), _row(tm, 2 * d),
                  _row(tm, d), _row(tm, d), _full((1, d)), _full((4, d, d))],
        out_specs=[_row(tm, d), _full((1, d)), _row(tm, 4 * d)],
        out_shape=[jax.ShapeDtypeStruct((lp, d), F32), jax.ShapeDtypeStruct((1, d), F32),
                   jax.ShapeDtypeStruct((lp, 4 * d), BF16)],
        compiler_params=_params(("arbitrary",)),
    )(dyv, du0, du1, dskip, dqkv, dgates, dres, xin, g, w4)


def _wgrad(a, dy, nshard, tm, tn, name, gain=None, square=False):
    lp, k = a.shape
    n = dy.shape[1]
    ns = n // nshard
    assert ns % tn == 0
    per = ns // tn

    def body(*refs):
        if gain is not None:
            a_ref, g_ref, dy_ref, o_ref = refs
            at = _rms(a_ref[...], g_ref[...]).astype(BF16)
        else:
            a_ref, dy_ref, o_ref = refs
            at = _square_bf16(a_ref[...]) if square else a_ref[...]
        i = pl.program_id(1)
        acc = _dot_tn(at, dy_ref[...])

        @pl.when(i == 0)
        def _():
            o_ref[0] = acc

        @pl.when(i != 0)
        def _():
            o_ref[0] += acc

    in_specs = [pl.BlockSpec((tm, k), lambda j, i: (i, 0))]
    args = [a]
    if gain is not None:
        in_specs.append(pl.BlockSpec((1, k), lambda j, i: (0, 0)))
        args.append(gain)
    in_specs.append(pl.BlockSpec((tm, tn), lambda j, i: (i, j)))
    args.append(dy)
    return pl.pallas_call(
        body, name=name, grid=(n // tn, lp // tm), in_specs=in_specs,
        out_specs=pl.BlockSpec((1, k, tn), lambda j, i: (j // per, 0, j % per)),
        out_shape=jax.ShapeDtypeStruct((nshard, k, ns), F32),
        compiler_params=_params(("parallel", "arbitrary")),
    )(*args)


def _head_tables(d):
    idx = np.arange(LANES)
    mean = (idx[:, None] // HEAD_DIM == idx[None, :] // HEAD_DIM).astype(np.float32) / HEAD_DIM
    n_heads = d // HEAD_DIM
    kvh = n_heads // GQA_REP
    c = np.arange(d)
    col = np.arange(kvh * LANES)
    head_of_col = (col // LANES) * GQA_REP + (col % LANES)
    sel = ((c[:, None] // HEAD_DIM == head_of_col[None, :]) & ((col % LANES) < GQA_REP)[None, :]).astype(np.float32)
    return jnp.asarray(mean, BF16), jnp.asarray(sel, BF16)


def _swap_pairs(y):
    lane = lax.broadcasted_iota(jnp.int32, y.shape, 1)
    return jnp.where(lane % 2 == 0, pltpu.roll(y, LANES - 1, 1), pltpu.roll(y, 1, 1))


def _qk_prep(qkv, cos_t, sin_t, qg, kg, mean_m, tm):
    lp, wq = qkv.shape
    d = wq * 2 // 3
    kvw = d // 4
    kvh = kvw // HEAD_DIM
    scale = HEAD_DIM ** -0.5

    def body(x_ref, c_ref, s_ref, qg_ref, kg_ref, m_ref, q_ref, k_ref, v_ref):
        cs, sn, mm = c_ref[...], s_ref[...], m_ref[...]
        for b in range((d + kvw) // LANES):
            x = x_ref[:, b * LANES:(b + 1) * LANES]
            gg = qg_ref[...] if b < d // LANES else kg_ref[...]
            y = x * lax.rsqrt(_split_dot(x * x, mm) + NORM_EPS) * gg
            out = y * cs + _swap_pairs(y) * sn
            if b < d // LANES:
                q_ref[:, b * LANES:(b + 1) * LANES] = (out * scale).astype(BF16)
            else:
                kb = b - d // LANES
                k_ref[2 * kb] = out[:, :HEAD_DIM].astype(BF16)
                k_ref[2 * kb + 1] = out[:, HEAD_DIM:].astype(BF16)
        lane = lax.broadcasted_iota(jnp.int32, (tm, LANES - HEAD_DIM), 1)
        ones_col = (lane == 0).astype(BF16)
        for h in range(kvh):
            vh = x_ref[:, d + kvw + h * HEAD_DIM:d + kvw + (h + 1) * HEAD_DIM].astype(BF16)
            v_ref[h] = jnp.concatenate([vh, ones_col], axis=1)

    k_spec = pl.BlockSpec((kvh, tm, HEAD_DIM), lambda i: (0, i, 0))
    v_spec = pl.BlockSpec((kvh, tm, LANES), lambda i: (0, i, 0))
    return pl.pallas_call(
        body, name="qk_prep", grid=(lp // tm,),
        in_specs=[_row(tm, wq), _row(tm, LANES), _row(tm, LANES), _full((1, LANES)), _full((1, LANES)),
                  _full((LANES, LANES))],
        out_specs=[_row(tm, d), k_spec, v_spec],
        out_shape=[jax.ShapeDtypeStruct((lp, d), BF16), jax.ShapeDtypeStruct((kvh, lp, HEAD_DIM), BF16),
                   jax.ShapeDtypeStruct((kvh, lp, LANES), BF16)],
        compiler_params=_params(("parallel",)),
    )(qkv, cos_t, sin_t, qg, kg, mean_m)


def _qk_bwd(qkv, dq, dk, dv, cos_t, sin_t, qg, kg, mean_m, tm):
    lp, wq = qkv.shape
    d = wq * 2 // 3
    kvw = d // 4
    kvh = kvw // HEAD_DIM
    scale = HEAD_DIM ** -0.5

    def body(x_ref, dq_ref, dk_ref, dv_ref, c_ref, s_ref, qg_ref, kg_ref, m_ref, o_ref, dqg_ref, dkg_ref):
        first = pl.program_id(0) == 0
        cs, sn, mm = c_ref[...], s_ref[...], m_ref[...]
        sums = [None, None]
        for b in range((d + kvw) // LANES):
            is_q = b < d // LANES
            x = x_ref[:, b * LANES:(b + 1) * LANES]
            gg = qg_ref[...] if is_q else kg_ref[...]
            r = lax.rsqrt(_split_dot(x * x, mm) + NORM_EPS)
            nrm = x * r
            if is_q:
                dout = dq_ref[:, b * LANES:(b + 1) * LANES] * scale
            else:
                kb = b - d // LANES
                dout = jnp.concatenate([dk_ref[2 * kb], dk_ref[2 * kb + 1]], axis=1)
            dy = dout * cs + _swap_pairs(dout * sn)
            part = jnp.sum(dy * nrm, axis=0, keepdims=True)
            sums[0 if is_q else 1] = part if sums[0 if is_q else 1] is None else sums[0 if is_q else 1] + part
            dn = dy * gg
            o_ref[:, b * LANES:(b + 1) * LANES] = r * (dn - nrm * _split_dot(dn * nrm, mm))
        for h in range(kvh):
            o_ref[:, d + kvw + h * HEAD_DIM:d + kvw + (h + 1) * HEAD_DIM] = dv_ref[h]
        for ref, s in ((dqg_ref, sums[0]), (dkg_ref, sums[1])):
            s = s + pltpu.roll(s, HEAD_DIM, 1)

            @pl.when(first)
            def _(ref=ref, s=s):
                ref[...] = s

            @pl.when(jnp.logical_not(first))
            def _(ref=ref, s=s):
                ref[...] += s

    kv_spec = pl.BlockSpec((kvh, tm, HEAD_DIM), lambda i: (0, i, 0))
    return pl.pallas_call(
        body, name="qk_bwd", grid=(lp // tm,),
        in_specs=[_row(tm, wq), _row(tm, d), kv_spec, kv_spec, _row(tm, LANES), _row(tm, LANES),
                  _full((1, LANES)), _full((1, LANES)), _full((LANES, LANES))],
        out_specs=[_row(tm, wq), _full((1, LANES)), _full((1, LANES))],
        out_shape=[jax.ShapeDtypeStruct((lp, wq), F32), jax.ShapeDtypeStruct((1, LANES), F32),
                   jax.ShapeDtypeStruct((1, LANES), F32)],
        compiler_params=_params(("arbitrary",)),
    )(qkv, dq, dk, dv, cos_t, sin_t, qg, kg, mean_m)


def _attn_fwd(q, k, v, kbias, tq, tk):
    lp, d = q.shape
    kvh = k.shape[0]
    rw = GQA_REP * HEAD_DIM
    nk = lp // tk

    def body(q_ref, k_ref, v_ref, kb_ref, o_ref, lse_ref, m_s, acc_s):
        j = pl.program_id(2)

        @pl.when(j == 0)
        def _():
            m_s[...] = jnp.full(m_s.shape, MASK_VALUE, F32)
            acc_s[...] = jnp.zeros(acc_s.shape, F32)

        def heads(masked):
            kk, vv = k_ref[0], v_ref[0]
            ss = [_dot_nt(q_ref[:, h * HEAD_DIM:(h + 1) * HEAD_DIM], kk) for h in range(GQA_REP)]
            ps, alphas = [], []
            for h in range(GQA_REP):
                s = ss[h] + kb_ref[...] if masked else ss[h]
                m_prev = m_s[h]
                m_new = jnp.maximum(m_prev, jnp.max(s, axis=1, keepdims=True))
                ps.append(jnp.exp(s - m_new[:, :1]).astype(BF16))
                alphas.append(jnp.exp(m_prev - m_new))
                m_s[h] = m_new
            for h in range(GQA_REP):
                acc_s[h] = acc_s[h] * alphas[h] + _dot(ps[h], vv)

        pl.when(j != nk - 1)(functools.partial(heads, False))
        pl.when(j == nk - 1)(functools.partial(heads, True))

        @pl.when(j == nk - 1)
        def _():
            lane = lax.broadcasted_iota(jnp.int32, (tq, LANES), 1)
            lse = jnp.zeros((tq, LANES), F32)
            outs = []
            for h in range(GQA_REP):
                acc = acc_s[h]
                l = acc[:, HEAD_DIM:HEAD_DIM + 1]
                outs.append(acc[:, :HEAD_DIM] / l)
                lse = jnp.where(lane == h, m_s[h][:, :1] + jnp.log(l), lse)
            o_ref[...] = jnp.concatenate(outs, axis=1).astype(BF16)
            lse_ref[...] = lse

    return pl.pallas_call(
        body, name="attn_fwd", grid=(kvh, lp // tq, nk),
        in_specs=[pl.BlockSpec((tq, rw), lambda g, i, j: (i, g)),
                  pl.BlockSpec((1, tk, HEAD_DIM), lambda g, i, j: (g, j, 0)),
                  pl.BlockSpec((1, tk, LANES), lambda g, i, j: (g, j, 0)),
                  pl.BlockSpec((1, tk), lambda g, i, j: (0, j))],
        out_specs=[pl.BlockSpec((tq, rw), lambda g, i, j: (i, g)),
                   pl.BlockSpec((tq, LANES), lambda g, i, j: (i, g))],
        out_shape=[jax.ShapeDtypeStruct((lp, d), BF16), jax.ShapeDtypeStruct((lp, kvh * LANES), F32)],
        scratch_shapes=[pltpu.VMEM((GQA_REP, tq, LANES), F32), pltpu.VMEM((GQA_REP, tq, LANES), F32)],
        compiler_params=_params(("parallel", "parallel", "arbitrary")),
    )(q, k, v, kbias)


def _attn_bwd(q, k, v, kbias, do, lse, delta, tq, tk):
    lp, d = q.shape
    kvh = k.shape[0]
    rw = GQA_REP * HEAD_DIM
    nq = lp // tq

    def body(q_ref, k_ref, v_ref, kb_ref, do_ref, lse_ref, dl_ref, dq_ref, dk_ref, dv_ref, dk_s, dv_s):
        j = pl.program_id(1)
        i = pl.program_id(2)

        @pl.when(jnp.logical_and(i == 0, j == 0))
        def _():
            dq_ref[...] = jnp.zeros(dq_ref.shape, F32)

        @pl.when(i == 0)
        def _():
            dk_s[...] = jnp.zeros(dk_s.shape, F32)
            dv_s[...] = jnp.zeros(dv_s.shape, F32)

        def heads(masked):
            kk, vv = k_ref[0], v_ref[0][:, :HEAD_DIM]
            lse, dl = lse_ref[...], dl_ref[...]
            dqs = []
            for h in range(GQA_REP):
                qh = q_ref[:, h * HEAD_DIM:(h + 1) * HEAD_DIM]
                doh = do_ref[:, h * HEAD_DIM:(h + 1) * HEAD_DIM]
                s = _dot_nt(qh, kk)
                if masked:
                    s = s + kb_ref[...]
                p = jnp.exp(s - lse[:, h:h + 1])
                ds = (p * (_dot_nt(doh, vv) - dl[:, h:h + 1])).astype(BF16)
                dv_s[...] += _dot_tn(p.astype(BF16), doh)
                dk_s[...] += _dot_tn(ds, qh)
                dqs.append(_dot(ds, kk))
            rows = pl.ds(pl.multiple_of(i * tq, tq), tq)
            dq_ref[rows, :] += jnp.concatenate(dqs, axis=1)

        nk = lp // tk
        pl.when(j != nk - 1)(functools.partial(heads, False))
        pl.when(j == nk - 1)(functools.partial(heads, True))

        @pl.when(i == nq - 1)
        def _():
            dk_ref[0] = dk_s[...]
            dv_ref[0] = dv_s[...]

    return pl.pallas_call(
        body, name="attn_bwd", grid=(kvh, lp // tk, nq),
        in_specs=[pl.BlockSpec((tq, rw), lambda g, j, i: (i, g)),
                  pl.BlockSpec((1, tk, HEAD_DIM), lambda g, j, i: (g, j, 0)),
                  pl.BlockSpec((1, tk, LANES), lambda g, j, i: (g, j, 0)),
                  pl.BlockSpec((1, tk), lambda g, j, i: (0, j)),
                  pl.BlockSpec((tq, rw), lambda g, j, i: (i, g)),
                  pl.BlockSpec((tq, LANES), lambda g, j, i: (i, g)),
                  pl.BlockSpec((tq, LANES), lambda g, j, i: (i, g))],
        out_specs=[pl.BlockSpec((lp, rw), lambda g, j, i: (0, g)),
                   pl.BlockSpec((1, tk, HEAD_DIM), lambda g, j, i: (g, j, 0)),
                   pl.BlockSpec((1, tk, HEAD_DIM), lambda g, j, i: (g, j, 0))],
        out_shape=[jax.ShapeDtypeStruct((lp, d), F32), jax.ShapeDtypeStruct((kvh, lp, HEAD_DIM), F32),
                   jax.ShapeDtypeStruct((kvh, lp, HEAD_DIM), F32)],
        scratch_shapes=[pltpu.VMEM((tk, HEAD_DIM), F32), pltpu.VMEM((tk, HEAD_DIM), F32)],
        compiler_params=_params(("parallel", "arbitrary", "arbitrary")),
    )(q, k, v, kbias, do, lse, delta)


def _ssm_math(a_re, a_im, log_dt, bt_re, bt_im):
    dt = jnp.exp(log_dt)
    lam_re = jnp.minimum(a_re, EIG_RE_MAX)
    lam_im = a_im
    mag = jnp.exp(lam_re * dt)
    ang = lam_im * dt
    lb_re = mag * jnp.cos(ang)
    lb_im = mag * jnp.sin(ang)
    num_re = lb_re - 1.0
    num_im = lb_im
    den = lam_re * lam_re + lam_im * lam_im
    f_re = (num_re * lam_re + num_im * lam_im) / den
    f_im = (num_im * lam_re - num_re * lam_im) / den
    bb_re = f_re[:, None, :] * bt_re - f_im[:, None, :] * bt_im
    bb_im = f_re[:, None, :] * bt_im + f_im[:, None, :] * bt_re
    return lb_re, lb_im, bb_re, bb_im


def _ssm_discretize(a_re, a_im, log_dt, bt_re, bt_im):
    nd, g, n = a_re.shape
    p = bt_re.shape[2]

    def body(ar_ref, ai_ref, ld_ref, br_ref, bi_ref, bbr_ref, bbi_ref, pr_ref, pi_ref, hr_ref, hi_ref):
        lb_re, lb_im, bb_re, bb_im = _ssm_math(ar_ref[0], ai_ref[0], ld_ref[0], br_ref[0], bi_ref[0])
        bbr_ref[0] = bb_re
        bbi_ref[0] = bb_im
        cr, ci = lb_re, lb_im
        for k in range(KSTEPS):
            pr_ref[0, k] = cr
            pi_ref[0, k] = ci
            if k < KSTEPS - 1:
                cr, ci = cr * lb_re - ci * lb_im, cr * lb_im + ci * lb_re
        for t in range(2):
            cr, ci = cr * cr - ci * ci, 2.0 * cr * ci
            hr_ref[0, t] = cr
            hi_ref[0, t] = ci

    s3 = pl.BlockSpec((1, g, n), lambda i: (i, 0, 0))
    s4 = pl.BlockSpec((1, g, p, n), lambda i: (i, 0, 0, 0))
    sp = pl.BlockSpec((1, KSTEPS, g, n), lambda i: (i, 0, 0, 0))
    sh = pl.BlockSpec((1, 2, g, n), lambda i: (i, 0, 0, 0))
    return pl.pallas_call(
        body, name="ssm_discretize", grid=(nd,),
        in_specs=[s3, s3, pl.BlockSpec((1, g, 1), lambda i: (i, 0, 0)), s4, s4],
        out_specs=[s4, s4, sp, sp, sh, sh],
        out_shape=[jax.ShapeDtypeStruct((nd, g, p, n), F32)] * 2 + [jax.ShapeDtypeStruct((nd, KSTEPS, g, n), F32)] * 2
        + [jax.ShapeDtypeStruct((nd, 2, g, n), F32)] * 2,
        compiler_params=_params(("parallel",)),
    )(a_re, a_im, log_dt, bt_re, bt_im)


def _ssm_param_bwd(a_re, a_im, log_dt, bt_re, bt_im, dlb_re, dlb_im, dbb_re, dbb_im):
    nd, g, n = a_re.shape
    p = bt_re.shape[2]

    def body(ar_ref, ai_ref, ld_ref, br_ref, bi_ref, c0_ref, c1_ref, c2_ref, c3_ref,
             o0_ref, o1_ref, o2_ref, o3_ref, o4_ref):
        _, vjp = jax.vjp(_ssm_math, ar_ref[0], ai_ref[0], ld_ref[0], br_ref[0], bi_ref[0])
        outs = vjp((c0_ref[0], c1_ref[0], c2_ref[0], c3_ref[0]))
        for ref, val in zip((o0_ref, o1_ref, o2_ref, o3_ref, o4_ref), outs):
            ref[0] = val

    s3 = pl.BlockSpec((1, g, n), lambda i: (i, 0, 0))
    s1 = pl.BlockSpec((1, g, 1), lambda i: (i, 0, 0))
    s4 = pl.BlockSpec((1, g, p, n), lambda i: (i, 0, 0, 0))
    return pl.pallas_call(
        body, name="ssm_param_bwd", grid=(nd,),
        in_specs=[s3, s3, s1, s4, s4, s3, s3, s4, s4],
        out_specs=[s3, s3, s1, s4, s4],
        out_shape=[jax.ShapeDtypeStruct((nd, g, n), F32)] * 2 + [jax.ShapeDtypeStruct((nd, g, 1), F32)]
        + [jax.ShapeDtypeStruct((nd, g, p, n), F32)] * 2,
        compiler_params=_params(("parallel",)),
    )(a_re, a_im, log_dt, bt_re, bt_im, dlb_re, dlb_im, dbb_re, dbb_im)


def _cmul(ar, ai, xr, xi, conj):
    if conj:
        return ar * xr + ai * xi, ar * xi - ai * xr
    return ar * xr - ai * xi, ar * xi + ai * xr


def _scan_chunk(buf, tab, carry, ein, nj, rev, conj, base=0):
    ks = list(range(KSTEPS))
    if rev:
        ks = ks[::-1]
    sub = lax.broadcasted_iota(jnp.int32, (SUBLANES, SCAN_LANES), 0)
    edge = sub == (SUBLANES - 1 if rev else 0)

    def step(j, _):
        jr, ji = j, nj + j
        ar, ai = tab[base, jr], tab[base, ji]
        hr = jnp.zeros((SUBLANES, SCAN_LANES), F32)
        hi = jnp.zeros((SUBLANES, SCAN_LANES), F32)
        for k in ks:
            rows = pl.ds(k * SUBLANES, SUBLANES)
            pr, pi_ = _cmul(ar, ai, hr, hi, conj)
            hr = pr + buf[jr, rows, :]
            hi = pi_ + buf[ji, rows, :]
            buf[jr, rows, :] = hr
            buf[ji, rows, :] = hi
        shift = SUBLANES - 1 if rev else 1
        er = jnp.where(edge, carry[jr], pltpu.roll(hr, shift, 0))
        ei = jnp.where(edge, carry[ji], pltpu.roll(hi, shift, 0))
        for t, dist in enumerate((1, 2, 4)):
            sh = SUBLANES - dist if rev else dist
            pr, pi_ = _cmul(tab[base + 1 + t, jr], tab[base + 1 + t, ji], pltpu.roll(er, sh, 0), pltpu.roll(ei, sh, 0), conj)
            er, ei = er + pr, ei + pi_
        ein[jr] = er
        ein[ji] = ei
        pr, pi_ = _cmul(tab[base + 4 + KSTEPS - 1, jr], tab[base + 4 + KSTEPS - 1, ji], er, ei, conj)
        last = 0 if rev else SUBLANES - 1
        carry[jr] = jnp.broadcast_to((hr + pr)[last:last + 1, :], (SUBLANES, SCAN_LANES))
        carry[ji] = jnp.broadcast_to((hi + pi_)[last:last + 1, :], (SUBLANES, SCAN_LANES))
        for n, k in enumerate(ks):
            rows = pl.ds(k * SUBLANES, SUBLANES)
            pr, pi_ = _cmul(tab[base + 4 + n, jr], tab[base + 4 + n, ji], er, ei, conj)
            buf[jr, rows, :] += pr
            buf[ji, rows, :] += pi_
        return 0

    lax.fori_loop(0, nj, step, 0)


def _state_lanes(b):
    per = SCAN_LANES // SSM_BLOCK
    return b // per, slice((b % per) * SSM_BLOCK, (b % per + 1) * SSM_BLOCK)


def _project_in(src, w_ref, buf, nj):
    nb, cb, _ = w_ref.shape
    for b in range(nb):
        res = _dot(src[:, b * cb:(b + 1) * cb], w_ref[b])
        j, lanes = _state_lanes(b)
        buf[j, :, lanes] = res[:, :SSM_BLOCK]
        buf[nj + j, :, lanes] = res[:, SSM_BLOCK:]


def _state_block(buf, b, nj):
    j, lanes = _state_lanes(b)
    return jnp.concatenate([buf[j, :, lanes], buf[nj + j, :, lanes]], axis=1).astype(BF16)


def _project_out(buf, w_ref, nj):
    return jnp.concatenate([_dot_nt(_state_block(buf, b, nj), w_ref[b]) for b in range(w_ref.shape[0])], axis=1)


def _ssm_fwd(u, wb, wct, tab, rev, name):
    lp, w = u.shape
    nb, cb, _ = wb.shape
    nj = nb * SSM_BLOCK // SCAN_LANES
    nc = lp // CHUNK
    ntab = tab.shape[0]
    cidx = (lambda c: nc - 1 - c) if rev else (lambda c: c)

    def body(u_ref, wb_ref, wct_ref, tab_ref, y_ref, ck_ref, buf, carry, ein):
        @pl.when(pl.program_id(0) == 0)
        def _():
            carry[...] = jnp.zeros(carry.shape, F32)

        _project_in(u_ref[...].astype(BF16), wb_ref, buf, nj)
        ck_ref[0] = carry[...]
        _scan_chunk(buf, tab_ref, carry, ein, nj, rev, False)
        y_ref[...] = _project_out(buf, wct_ref, nj)

    wshape = (nb, cb, 2 * SSM_BLOCK)
    return pl.pallas_call(
        body, name=name, grid=(nc,),
        in_specs=[pl.BlockSpec((CHUNK, w), lambda c: (cidx(c), 0)), _full(wshape), _full(wshape),
                  _full((ntab, 2 * nj, SUBLANES, SCAN_LANES))],
        out_specs=[pl.BlockSpec((CHUNK, w), lambda c: (cidx(c), 0)),
                   pl.BlockSpec((1, 2 * nj, SUBLANES, SCAN_LANES), lambda c: (cidx(c), 0, 0, 0))],
        out_shape=[jax.ShapeDtypeStruct((lp, w), F32), jax.ShapeDtypeStruct((nc, 2 * nj, SUBLANES, SCAN_LANES), F32)],
        scratch_shapes=[pltpu.VMEM((2 * nj, CHUNK, SCAN_LANES), F32), pltpu.VMEM((2 * nj, SUBLANES, SCAN_LANES), F32),
                        pltpu.VMEM((2 * nj, SUBLANES, SCAN_LANES), F32)],
        compiler_params=_params(("arbitrary",)),
    )(u, wb, wct, tab)


def _ssm_bwd(u, dy, ckpt, wb, wct, tab, rev, name):
    lp, w = u.shape
    nb, cb, _ = wb.shape
    nj = nb * SSM_BLOCK // SCAN_LANES
    nc = lp // CHUNK
    ntab = tab.shape[0]
    cidx = (lambda c: c) if rev else (lambda c: nc - 1 - c)

    def body(u_ref, dy_ref, ck_ref, wb_ref, wct_ref, tab_hbm, du_ref, dbb_ref, dcc_ref, dlb_ref,
             tab_ref, dwb_ref, dwc_ref, xs, ls, xcar, lcar, xin, lin):
        c = pl.program_id(0)

        @pl.when(c == 0)
        def _():
            pltpu.sync_copy(tab_hbm, tab_ref)
            lcar[...] = jnp.zeros(lcar.shape, F32)
            dwb_ref[...] = jnp.zeros(dwb_ref.shape, F32)
            dwc_ref[...] = jnp.zeros(dwc_ref.shape, F32)
            dlb_ref[...] = jnp.zeros(dlb_ref.shape, F32)

        ub = u_ref[...].astype(BF16)
        dyb = dy_ref[...].astype(BF16)
        _project_in(ub, wb_ref, xs, nj)
        xcar[...] = ck_ref[0]
        _scan_chunk(xs, tab_ref, xcar, xin, nj, rev, False)
        _project_in(dyb, wct_ref, ls, nj)
        _scan_chunk(ls, tab_ref, lcar, lin, nj, not rev, True, base=ntab // 2)
        dus = []
        for b in range(nb):
            chans = slice(b * cb, (b + 1) * cb)
            xb = _state_block(xs, b, nj)
            lb = _state_block(ls, b, nj)
            dwc_ref[b] += _dot_tn(dyb[:, chans], xb)
            dwb_ref[b] += _dot_tn(ub[:, chans], lb)
            dus.append(_dot_nt(lb, wb_ref[b]))
        du_ref[...] = jnp.concatenate(dus, axis=1)

        def step(j, _):
            jr, ji = j, nj + j
            ar = jnp.zeros((SUBLANES, SCAN_LANES), F32)
            ai = jnp.zeros((SUBLANES, SCAN_LANES), F32)
            for k in range(KSTEPS):
                kp = k + 1 if rev else k - 1
                rows = pl.ds(k * SUBLANES, SUBLANES)
                if 0 <= kp < KSTEPS:
                    prow = pl.ds(kp * SUBLANES, SUBLANES)
                    xr, xi = xs[jr, prow, :], xs[ji, prow, :]
                else:
                    xr, xi = xin[jr], xin[ji]
                lr, li = ls[jr, rows, :], ls[ji, rows, :]
                ar += lr * xr + li * xi
                ai += li * xr - lr * xi
            dlb_ref[jr] += ar
            dlb_ref[ji] += ai
            return 0

        lax.fori_loop(0, nj, step, 0)

        @pl.when(c == nc - 1)
        def _():
            for b in range(2 * nj):
                dlb_ref[b] = jnp.broadcast_to(jnp.sum(dlb_ref[b], axis=0, keepdims=True), (SUBLANES, SCAN_LANES))
            for g in range(w // SSM_GROUP):
                b, gl = divmod(g, cb // SSM_GROUP)
                rows = slice(gl * SSM_GROUP, (gl + 1) * SSM_GROUP)
                for part in range(2):
                    cols = slice(part * SSM_BLOCK + gl * SSM_STATE, part * SSM_BLOCK + (gl + 1) * SSM_STATE)
                    dbb_ref[part, g * SSM_GROUP:(g + 1) * SSM_GROUP, :] = dwb_ref[b, rows, cols]
                    dcc_ref[part, g * SSM_GROUP:(g + 1) * SSM_GROUP, :] = dwc_ref[b, rows, cols]

    st = (2 * nj, SUBLANES, SCAN_LANES)
    wshape = (nb, cb, 2 * SSM_BLOCK)
    return pl.pallas_call(
        body, name=name, grid=(nc,),
        in_specs=[pl.BlockSpec((CHUNK, w), lambda c: (cidx(c), 0)), pl.BlockSpec((CHUNK, w), lambda c: (cidx(c), 0)),
                  pl.BlockSpec((1,) + st, lambda c: (cidx(c), 0, 0, 0)), _full(wshape), _full(wshape), _ANY],
        out_specs=[pl.BlockSpec((CHUNK, w), lambda c: (cidx(c), 0)), _full((2, w, SSM_STATE)),
                   _full((2, w, SSM_STATE)), _full(st)],
        out_shape=[jax.ShapeDtypeStruct((lp, w), F32), jax.ShapeDtypeStruct((2, w, SSM_STATE), F32),
                   jax.ShapeDtypeStruct((2, w, SSM_STATE), F32), jax.ShapeDtypeStruct(st, F32)],
        scratch_shapes=[pltpu.VMEM((ntab,) + st, F32), pltpu.VMEM(wshape, F32), pltpu.VMEM(wshape, F32),
                        pltpu.VMEM((2 * nj, CHUNK, SCAN_LANES), F32), pltpu.VMEM((2 * nj, CHUNK, SCAN_LANES), F32),
                        pltpu.VMEM(st, F32), pltpu.VMEM(st, F32), pltpu.VMEM(st, F32), pltpu.VMEM(st, F32)],
        compiler_params=_params(("arbitrary",)),
    )(u, dy, ckpt, wb, wct, tab)


def _embed_blocks(t_re, t_im):
    g, p, n = t_re.shape
    gb = SSM_BLOCK // n
    eye = jnp.eye(gb, dtype=t_re.dtype)
    parts = [jnp.einsum('bgpn,gh->bgphn', t.reshape(g // gb, gb, p, n), eye).reshape(g // gb, gb * p, gb * n)
             for t in (t_re, t_im)]
    return jnp.concatenate(parts, axis=2)


def _scan_layout(x, nj):
    lead = x.shape[:-1]
    x = x.reshape(lead + (nj, 1, SCAN_LANES))
    return jnp.broadcast_to(x, lead + (nj, SUBLANES, SCAN_LANES))


def _scan_tables(pw_re, pw_im, hi_re, hi_im, rev):
    s = pw_re.shape[1] * pw_re.shape[2]
    nj = s // SCAN_LANES
    sub = jnp.arange(SUBLANES).reshape(1, SUBLANES, 1)

    def pair(re, im, mask=None):
        re, im = _scan_layout(re.reshape(s), nj), _scan_layout(im.reshape(s), nj)
        if mask is not None:
            re, im = jnp.where(mask, re, 0.0), jnp.where(mask, im, 0.0)
        return jnp.concatenate([re, im], axis=0)

    def live(dist):
        return (sub < SUBLANES - dist) if rev else (sub >= dist)

    rows = [pair(pw_re[0], pw_im[0]),
            pair(pw_re[KSTEPS - 1], pw_im[KSTEPS - 1], live(1)),
            pair(hi_re[0], hi_im[0], live(2)),
            pair(hi_re[1], hi_im[1], live(4))]
    rows += [pair(pw_re[k], pw_im[k]) for k in range(KSTEPS)]
    return jnp.stack(rows, axis=0)


def _adamw(w, g, m, v, tm):
    r, c = w.shape
    c1 = 1.0 - ADAM_B1 ** ADAM_STEP
    c2 = 1.0 - ADAM_B2 ** ADAM_STEP

    def body(w_ref, g_ref, m_ref, v_ref, d_ref, nm_ref, nv_ref):
        gg = g_ref[...]
        nm = ADAM_B1 * m_ref[...] + (1.0 - ADAM_B1) * gg
        nv = ADAM_B2 * v_ref[...] + (1.0 - ADAM_B2) * (gg * gg)
        nm_ref[...] = nm
        nv_ref[...] = nv
        d_ref[...] = -ADAM_LR * ((nm / c1) / (jnp.sqrt(nv / c2) + ADAM_EPS) + ADAM_WD * w_ref[...])

    spec = _row(tm, c)
    return pl.pallas_call(
        body, name="adamw", grid=(r // tm,), in_specs=[spec] * 4, out_specs=[spec] * 3,
        out_shape=[jax.ShapeDtypeStruct((r, c), F32)] * 3, compiler_params=_params(("parallel",)),
    )(w, g, m, v)


def _pair_sum(g42, got, core, out_dtype, tm, name):
    _, _, r, c = g42.shape

    def body(core_ref, a_ref, b_ref, o_ref):
        o_ref[...] = (a_ref[...] + b_ref[...]).astype(out_dtype)

    grid_spec = pltpu.PrefetchScalarGridSpec(
        num_scalar_prefetch=1, grid=(4, r // tm),
        in_specs=[pl.BlockSpec((1, None, tm, c), lambda s, i, core_ref: (s, core_ref[0], i, 0)),
                  pl.BlockSpec((1, tm, c), lambda s, i, core_ref: (s, i, 0))],
        out_specs=pl.BlockSpec((1, tm, c), lambda s, i, core_ref: (s, i, 0)))
    return pl.pallas_call(
        body, name=name, grid_spec=grid_spec, out_shape=jax.ShapeDtypeStruct((4, r, c), out_dtype),
        compiler_params=_params(("parallel", "parallel")),
    )(core, g42, got)


def _sum4(a, core, tm, name):
    _, r, c = a.shape

    def body(core_ref, a_ref, o_ref):
        o_ref[...] = ((a_ref[0].astype(F32) + a_ref[1].astype(F32)) + a_ref[2].astype(F32)) + a_ref[3].astype(F32)

    grid_spec = pltpu.PrefetchScalarGridSpec(
        num_scalar_prefetch=1, grid=(r // tm,),
        in_specs=[pl.BlockSpec((4, tm, c), lambda i, core_ref: (0, i, 0))],
        out_specs=pl.BlockSpec((None, tm, c), lambda i, core_ref: (core_ref[0], i, 0)))
    return pl.pallas_call(
        body, name=name, grid_spec=grid_spec, out_shape=jax.ShapeDtypeStruct((2, r, c), F32),
        compiler_params=_params(("parallel",)),
    )(core, a)


_ANY = pl.BlockSpec(memory_space=pl.ANY)


def _all_gather8(blocks, name):
    n = len(blocks)

    def body(*refs):
        xs, outs = refs[:n], refs[n:2 * n]
        send_sems, recv_sems, local_sems = refs[2 * n:]
        x, y, c = lax.axis_index("x"), lax.axis_index("y"), lax.axis_index("c")
        me, sibling = (x, y, c), (x, y, 1 - c)
        chips = [(1 - x, y), (x, 1 - y), (1 - x, 1 - y)]

        def slot(t, px, py, pc):
            return outs[t].at[4 * px + 2 * py + pc]

        def copy(t, k, blk, to, src=None):
            return pltpu.make_async_remote_copy(
                src_ref=slot(t, *blk) if src is None else src, dst_ref=slot(t, *blk),
                send_sem=send_sems.at[t, k], recv_sem=recv_sems.at[t, k], device_id=to, device_id_type=MESH_ID)

        mine = [pltpu.make_async_copy(xs[t], slot(t, *me), local_sems.at[t]) for t in range(n)]
        for cp in mine:
            cp.start()
        first = [[copy(t, 0, me, sibling, src=xs[t])]
                 + [copy(t, 1 + j, me, (*chip, c), src=xs[t]) for j, chip in enumerate(chips)] for t in range(n)]
        for t in range(n):
            for cp in first[t]:
                cp.start()
        passed = [[copy(t, 4 + j, (*chip, c), sibling) for j, chip in enumerate(chips)] for t in range(n)]
        for j, chip in enumerate(chips):
            for t in range(n):
                copy(t, 1 + j, (*chip, c), me).wait_recv()
                passed[t][j].start()
        for t in range(n):
            copy(t, 0, sibling, me).wait_recv()
        for j, chip in enumerate(chips):
            for t in range(n):
                copy(t, 4 + j, (*chip, 1 - c), me).wait_recv()
        for t in range(n):
            for cp in first[t] + passed[t]:
                cp.wait_send()
        for cp in mine:
            cp.wait()

    return pl.pallas_call(
        body, name=name, out_shape=[jax.ShapeDtypeStruct((8,) + b.shape, b.dtype) for b in blocks],
        in_specs=[_ANY] * n, out_specs=[_ANY] * n,
        scratch_shapes=[pltpu.SemaphoreType.DMA((n, 7)), pltpu.SemaphoreType.DMA((n, 7)),
                        pltpu.SemaphoreType.DMA((n,))],
    )(*blocks)


def _pair_exchange(gs, name):
    n = len(gs)

    def body(*refs):
        g_refs, outs = refs[:n], refs[n:2 * n]
        send_sems, recv_sems = refs[2 * n:]
        x, y, c = lax.axis_index("x"), lax.axis_index("y"), lax.axis_index("c")
        cps = [pltpu.make_async_remote_copy(
            src_ref=g_refs[t].at[:, 1 - c], dst_ref=outs[t], send_sem=send_sems.at[t], recv_sem=recv_sems.at[t],
            device_id=(x, y, 1 - c), device_id_type=MESH_ID) for t in range(n)]
        for cp in cps:
            cp.start()
        for cp in cps:
            cp.wait()

    return pl.pallas_call(
        body, name=name,
        out_shape=[jax.ShapeDtypeStruct((g.shape[0],) + g.shape[2:], g.dtype) for g in gs],
        in_specs=[_ANY] * n, out_specs=[_ANY] * n,
        scratch_shapes=[pltpu.SemaphoreType.DMA((n,)), pltpu.SemaphoreType.DMA((n,))],
    )(*gs)


def _chip_scatter(ps, name):
    n = len(ps)

    def body(*refs):
        p_refs, outs = refs[:n], refs[n:2 * n]
        send_sems, recv_sems, local_sems = refs[2 * n:]
        x, y, c = lax.axis_index("x"), lax.axis_index("y"), lax.axis_index("c")
        mine = 2 * x + y
        chips = [(1 - x, y), (x, 1 - y), (1 - x, 1 - y)]
        own = [pltpu.make_async_copy(p_refs[t].at[mine], outs[t].at[mine], local_sems.at[t]) for t in range(n)]
        for cp in own:
            cp.start()

        def copy(t, k, src_slab, dst_slab, chip):
            return pltpu.make_async_remote_copy(
                src_ref=p_refs[t].at[src_slab], dst_ref=outs[t].at[dst_slab], send_sem=send_sems.at[t, k],
                recv_sem=recv_sems.at[t, k], device_id=(*chip, c), device_id_type=MESH_ID)

        sends = [[copy(t, k, 2 * cx + cy, mine, (cx, cy)) for k, (cx, cy) in enumerate(chips)] for t in range(n)]
        for k in range(3):
            for t in range(n):
                sends[t][k].start()
        for k, (cx, cy) in enumerate(chips):
            for t in range(n):
                copy(t, k, mine, 2 * cx + cy, (cx, cy)).wait_recv()
        for t in range(n):
            for cp in sends[t]:
                cp.wait_send()
        for cp in own:
            cp.wait()

    return pl.pallas_call(
        body, name=name, out_shape=[jax.ShapeDtypeStruct(p.shape, p.dtype) for p in ps],
        in_specs=[_ANY] * n, out_specs=[_ANY] * n,
        scratch_shapes=[pltpu.SemaphoreType.DMA((n, 3)), pltpu.SemaphoreType.DMA((n, 3)),
                        pltpu.SemaphoreType.DMA((n,))],
    )(*ps)


def _pair_gather(rs, name):
    n = len(rs)

    def body(*refs):
        ins, outs = refs[:n], refs[n:2 * n]
        send_sems, recv_sems = refs[2 * n:]
        x, y, c = lax.axis_index("x"), lax.axis_index("y"), lax.axis_index("c")

        def copy(t, slab):
            return pltpu.make_async_remote_copy(
                src_ref=ins[t].at[slab], dst_ref=outs[t].at[slab], send_sem=send_sems.at[t],
                recv_sem=recv_sems.at[t], device_id=(x, y, 1 - c), device_id_type=MESH_ID)

        sends = [copy(t, c) for t in range(n)]
        for cp in sends:
            cp.start()
        for t in range(n):
            copy(t, 1 - c).wait_recv()
        for cp in sends:
            cp.wait_send()

    return pl.pallas_call(
        body, name=name, out_shape=[jax.ShapeDtypeStruct(r.shape, r.dtype) for r in rs],
        in_specs=[_ANY] * n, out_specs=[_ANY] * n, input_output_aliases={t: t for t in range(n)},
        scratch_shapes=[pltpu.SemaphoreType.DMA((n,)), pltpu.SemaphoreType.DMA((n,))],
    )(*rs)


PACK_COLS = 1024
BIG = (("meta_tokens", 1), ("w_in", 1), ("w_glu", 0), ("w_ssm_proj", 1), ("w_attn_proj", 0), ("w_out", 0),
       ("w_mlp_in", 1), ("w_mlp_out", 0))
SMALL = ("norm_mix_g", "ssm_a_re", "ssm_a_im", "ssm_log_dt", "ssm_b_re", "ssm_b_im", "ssm_c_re", "ssm_c_im",
         "ssm_d", "b_glu", "q_norm_g", "k_norm_g", "norm_mlp_g", "norm_final_g")


def _pad_rows(flat, mult_rows):
    n = flat.shape[0]
    unit = PACK_COLS * mult_rows
    total = -(-n // unit) * unit
    return jnp.pad(flat, (0, total - n)).reshape(total // PACK_COLS, PACK_COLS)


def _half(t, c):
    return lax.dynamic_slice_in_dim(t, c * (t.shape[0] // 2), t.shape[0] // 2, 0)


def _gather_weights(shards, c):
    names = [name for name, _ in BIG]
    blocks = [_half(shards[name], c) if name == "meta_tokens" else _half(shards[name], c).astype(BF16)
              for name in names]
    got = _all_gather8(blocks, "weight_all_gather")
    return {name: g.reshape((4, 2 * g.shape[1]) + g.shape[2:]) for name, g in zip(names, got)}


def _reduce_gradients(big4, small_flat, c):
    names = [name for name, _ in BIG]
    n_small = small_flat.shape[0]
    unit = 8 * SUBLANES * PACK_COLS
    k = -(-n_small // unit) * unit
    small42 = jnp.pad(small_flat, (0, k - n_small)).reshape(4, 2, k // (8 * PACK_COLS), PACK_COLS)
    g42 = [big4[name].reshape(4, 2, big4[name].shape[1] // 2, big4[name].shape[2]) for name in names] + [small42]
    labels = names + ["small"]
    wire = [F32 if name == "meta_tokens" else BF16 for name in names] + [F32]
    tiles = [_pick_tile(g.shape[2], 256, SUBLANES if dt == F32 else 2 * SUBLANES) for g, dt in zip(g42, wire)]
    core = c.astype(jnp.int32).reshape(1)
    got = _pair_exchange(g42, "grad_pair_exchange")
    pair = [_pair_sum(g, o, core, dt, tm, "pair_sum_" + lb) for g, o, dt, tm, lb in zip(g42, got, wire, tiles, labels)]
    by_src = _chip_scatter(pair, "grad_chip_scatter")
    red = [_sum4(b, core, tm, "chip_sum_" + lb) for b, tm, lb in zip(by_src, tiles, labels)]
    both = _pair_gather(red[:-1], "grad_pair_gather")
    out = {name: b.reshape(2 * b.shape[1], b.shape[2]) for name, b in zip(names, both)}
    small_piece = lax.dynamic_index_in_dim(red[-1], c, 0, keepdims=False)
    small = _all_gather8([small_piece], "small_grad_all_gather")[0].reshape(-1)[:n_small]
    return out, small


def _to_chunk_order(a):
    lp = a.shape[0]
    rest = a.shape[1:]
    a = a.reshape((lp // CHUNK, SUBLANES, KSTEPS) + rest)
    return a.swapaxes(1, 2).reshape((lp,) + rest)


def _from_chunk_order(a):
    lp = a.shape[0]
    rest = a.shape[1:]
    a = a.reshape((lp // CHUNK, KSTEPS, SUBLANES) + rest)
    return a.swapaxes(1, 2).reshape((lp,) + rest)


def _rope_tables(l_total, lp):
    n_real = l_total - N_META
    pos = np.arange(n_real)
    row_id = (pos // GRID_W).astype(np.float32)
    col_id = (pos % GRID_W).astype(np.float32)
    ppa = HEAD_DIM // 4
    inv_freq = (ROPE_THETA ** (-np.arange(ppa, dtype=np.float64) / ppa)).astype(np.float32)
    ang = np.concatenate([row_id[:, None] * inv_freq, col_id[:, None] * inv_freq], axis=-1)
    ang = np.concatenate([np.zeros((N_META, HEAD_DIM // 2), np.float32), ang,
                          np.zeros((lp - l_total, HEAD_DIM // 2), np.float32)], axis=0).astype(np.float64)
    cos = np.repeat(np.cos(ang), 2, axis=1)
    sin = np.repeat(np.sin(ang), 2, axis=1) * np.tile(np.asarray([-1.0, 1.0]), HEAD_DIM // 2)
    reps = (1, LANES // HEAD_DIM)
    return np.tile(cos, reps).astype(np.float32), np.tile(sin, reps).astype(np.float32)


def kernel(x, meta_tokens, norm_mix_g, w_in, ssm_a_re, ssm_a_im, ssm_log_dt, ssm_b_re, ssm_b_im, ssm_c_re, ssm_c_im, ssm_d, w_glu, b_glu, q_norm_g, k_norm_g, w_ssm_proj, w_attn_proj, w_out, norm_mlp_g, w_mlp_in, w_mlp_out, norm_final_g, loss_target, m_meta_tokens, m_norm_mix_g, m_w_in, m_ssm_a_re, m_ssm_a_im, m_ssm_log_dt, m_ssm_b_re, m_ssm_b_im, m_ssm_c_re, m_ssm_c_im, m_ssm_d, m_w_glu, m_b_glu, m_q_norm_g, m_k_norm_g, m_w_ssm_proj, m_w_attn_proj, m_w_out, m_norm_mlp_g, m_w_mlp_in, m_w_mlp_out, m_norm_final_g, v_meta_tokens, v_norm_mix_g, v_w_in, v_ssm_a_re, v_ssm_a_im, v_ssm_log_dt, v_ssm_b_re, v_ssm_b_im, v_ssm_c_re, v_ssm_c_im, v_ssm_d, v_w_glu, v_b_glu, v_q_norm_g, v_k_norm_g, v_w_ssm_proj, v_w_attn_proj, v_w_out, v_norm_mlp_g, v_w_mlp_in, v_w_mlp_out, v_norm_final_g):
    args = dict(locals())
    names = list(dict.fromkeys([n for n, _ in BIG] + list(SMALL)))
    order = ['meta_tokens', 'norm_mix_g', 'w_in', 'ssm_a_re', 'ssm_a_im', 'ssm_log_dt', 'ssm_b_re', 'ssm_b_im',
             'ssm_c_re', 'ssm_c_im', 'ssm_d', 'w_glu', 'b_glu', 'q_norm_g', 'k_norm_g', 'w_ssm_proj', 'w_attn_proj',
             'w_out', 'norm_mlp_g', 'w_mlp_in', 'w_mlp_out', 'norm_final_g']
    assert sorted(names) == sorted(order)
    c_idx = lax.axis_index("c")

    seq, d = x.shape[1], x.shape[2]
    l_total = seq + N_META
    lp = -(-l_total // SEQ_ALIGN) * SEQ_ALIGN
    hd = d // 2
    n_groups = hd // SSM_GROUP
    n_state = n_groups * SSM_STATE
    nj = n_state // SCAN_LANES
    kvh = d // HEAD_DIM // GQA_REP

    shard2d = {}
    for name, _ in BIG:
        t = args[name]
        shard2d[name] = t.reshape(t.shape[-2], t.shape[-1])
    full = _gather_weights(shard2d, c_idx)
    meta_full = jnp.transpose(full["meta_tokens"], (1, 0, 2)).reshape(N_META, d)
    w_in4 = full["w_in"]
    w_mlp_in4 = full["w_mlp_in"]
    w_ssm_proj4 = full["w_ssm_proj"]
    w_glu_f = full["w_glu"].reshape(hd, hd)
    w_attn_proj_f = full["w_attn_proj"].reshape(d, d)
    w_out_f = full["w_out"].reshape(d, d)
    w_mlp_out_f = full["w_mlp_out"].reshape(4 * d, d)

    xin = jnp.concatenate([meta_full, x[0], jnp.zeros((lp - l_total, d), F32)], axis=0)
    xin = _to_chunk_order(xin)
    tgt = _to_chunk_order(jnp.pad(loss_target[0], ((N_META, lp - l_total), (0, 0))))
    pos = np.arange(lp)
    rowmask = jnp.asarray(_to_chunk_order(((pos >= N_META) & (pos < l_total)).astype(np.float32)[:, None]))
    kbias = jnp.asarray(_to_chunk_order(np.where(pos < l_total, 0.0, MASK_VALUE).astype(np.float32)[:, None])
                        .reshape(1, lp))
    cos_t, sin_t = (jnp.asarray(_to_chunk_order(t)) for t in _rope_tables(l_total, lp))
    mean_m, sel = _head_tables(d)

    tm = _pick_tile(lp, 320)
    tm_big = _pick_tile(lp, 640)
    tq = _pick_tile(lp, ATTN_TILE, LANES)
    assert lp - tq <= (l_total // CHUNK) * CHUNK
    g_mix = norm_mix_g.reshape(1, d)
    g_mlp = norm_mlp_g.reshape(1, d)
    g_fin = norm_final_g.reshape(1, d)
    qg = jnp.tile(q_norm_g.reshape(1, HEAD_DIM), (1, LANES // HEAD_DIM))
    kg = jnp.tile(k_norm_g.reshape(1, HEAD_DIM), (1, LANES // HEAD_DIM))
    dskip = ssm_d.reshape(1, hd)
    bglu = b_glu.reshape(1, hd)

    a_re, a_im = ssm_a_re[0], ssm_a_im[0]
    log_dt = ssm_log_dt[0][..., None]
    bt_re = jnp.swapaxes(ssm_b_re[0], 2, 3)
    bt_im = jnp.swapaxes(ssm_b_im[0], 2, 3)
    bb_re, bb_im, pw_re, pw_im, hi_re, hi_im = _ssm_discretize(a_re, a_im, log_dt, bt_re, bt_im)
    wb = [_embed_blocks(bb_re[i], bb_im[i]).astype(BF16) for i in range(2)]
    wct = [_embed_blocks(ssm_c_re[0, i], -ssm_c_im[0, i]).astype(BF16) for i in range(2)]
    tabs = [_scan_tables(pw_re[i], pw_im[i], hi_re[i], hi_im[i], rev=(i == 1)) for i in range(2)]
    tabs_adj = [_scan_tables(pw_re[i], pw_im[i], hi_re[i], hi_im[i], rev=(i == 0)) for i in range(2)]

    u, qkv, gates = _in_proj(xin, g_mix, w_in4, tm)
    y0, ck0 = _ssm_fwd(u, wb[0], wct[0], tabs[0], False, "ssm_fwd_0")
    y1, ck1 = _ssm_fwd(u, wb[1], wct[1], tabs[1], True, "ssm_fwd_1")
    yssm = _glu_fwd(u, y0, y1, dskip, w_glu_f, bglu, tm_big)
    q, k, v = _qk_prep(qkv, cos_t, sin_t, qg, kg, mean_m, tm)
    o, lse = _attn_fwd(q, k, v, kbias, tq, tq)
    h1, merged = _merge_fwd(yssm, o, gates, xin, w_ssm_proj4, w_attn_proj_f, w_out_f, tm)
    r = _mlp_in(h1, g_mlp, w_mlp_in4, tm)
    h3 = _mlp_out(h1, r, w_mlp_out_f, tm)
    loss_tile, dh3, d_gfin = _final_loss(h3, g_fin, tgt, rowmask, tm_big)
    loss = lax.psum(loss_tile[0, 0], ("x", "y", "c"))

    dz, dh3b = _mlp_bwd_a(dh3, r, w_mlp_out_f, tm)
    dh1, d_gmlp = _mlp_bwd_b(dz, dh3, h1, g_mlp, w_mlp_in4, tm)
    dgates, dms, dma, dyssm, do, delta, dh1b = _merge_bwd(dh1, yssm, o, gates, w_ssm_proj4, w_attn_proj_f, w_out_f,
                                                          sel, tm)
    dyv, d_wglu, d_bglu, d_dskip = _glu_bwd(dyssm, u, y0, y1, dskip, w_glu_f, bglu, tm_big)
    du0, dbb0, dcc0, dlb0 = _ssm_bwd(u, dyv, ck0, wb[0], wct[0], _both(tabs[0], tabs_adj[0]), False, "ssm_bwd_0")
    du1, dbb1, dcc1, dlb1 = _ssm_bwd(u, dyv, ck1, wb[1], wct[1], _both(tabs[1], tabs_adj[1]), True, "ssm_bwd_1")
    dq, dk, dv = _attn_bwd(q, k, v, kbias, do, lse, delta, tq, tq)
    dqkv, d_qg, d_kg = _qk_bwd(qkv, dq, dk, dv, cos_t, sin_t, qg, kg, mean_m, tm)
    dxin, d_gmix, dproj = _in_proj_bwd(dyv, du0, du1, dskip, dqkv, dgates, dh1, xin, g_mix, w_in4, tm)

    tn = min(d, 1024)
    grads4 = {
        "w_in": _wgrad(xin, dproj, 4, tm_big, tn, "wgrad_in", gain=g_mix),
        "w_mlp_in": _wgrad(h1, dz, 4, tm_big, tn, "wgrad_mlp_in", gain=g_mlp),
        "w_mlp_out": _wgrad(r, dh3b, 1, tm_big, min(d, 256), "wgrad_mlp_out", square=True).reshape(4, d, d),
        "w_out": _wgrad(merged, dh1b, 1, tm_big, tn, "wgrad_out").reshape(4, d // 4, d),
        "w_attn_proj": _wgrad(o, dma, 1, tm_big, tn, "wgrad_attn_proj").reshape(4, d // 4, d),
        "w_ssm_proj": _wgrad(yssm, dms, 4, tm_big, d // 4, "wgrad_ssm_proj"),
        "w_glu": d_wglu.reshape(4, hd // 4, hd),
    }
    dx_nat = _from_chunk_order(dxin)
    grads4["meta_tokens"] = jnp.swapaxes(dx_nat[:N_META].reshape(N_META, 4, d // 4), 0, 1)
    grad_x = dx_nat[N_META:l_total][None]

    dlb = jnp.stack([dlb0, dlb1])[:, :, 0, :]
    dlb_re = dlb[:, :nj].reshape(2, n_groups, SSM_STATE)
    dlb_im = dlb[:, nj:].reshape(2, n_groups, SSM_STATE)
    gpn = (2, 2, n_groups, SSM_GROUP, SSM_STATE)
    dbb = jnp.stack([dbb0, dbb1]).reshape(gpn)
    dcc = jnp.stack([dcc0, dcc1]).reshape(gpn)
    d_are, d_aim, d_logdt, d_btre, d_btim = _ssm_param_bwd(a_re, a_im, log_dt, bt_re, bt_im, dlb_re, dlb_im,
                                                           dbb[:, 0], dbb[:, 1])
    small_grads = {
        "norm_mix_g": d_gmix, "ssm_a_re": d_are, "ssm_a_im": d_aim, "ssm_log_dt": d_logdt,
        "ssm_b_re": jnp.swapaxes(d_btre, 2, 3), "ssm_b_im": jnp.swapaxes(d_btim, 2, 3),
        "ssm_c_re": dcc[:, 0], "ssm_c_im": -dcc[:, 1],
        "ssm_d": d_dskip, "b_glu": d_bglu, "q_norm_g": d_qg[:, :HEAD_DIM], "k_norm_g": d_kg[:, :HEAD_DIM],
        "norm_mlp_g": d_gmlp, "norm_final_g": d_gfin,
    }
    small_flat = jnp.concatenate([small_grads[n].reshape(-1) for n in SMALL])

    red_big, red_small = _reduce_gradients(grads4, small_flat, c_idx)
    grad, delta_w, new_m, new_v = {}, {}, {}, {}
    for name, _ in BIG:
        w2 = shard2d[name]
        shp = args[name].shape
        g2 = red_big[name]
        t = _pick_tile(w2.shape[0], 256, 8)
        dl, nm, nv = _adamw(w2, g2, args["m_" + name].reshape(w2.shape), args["v_" + name].reshape(w2.shape), t)
        grad[name], delta_w[name], new_m[name], new_v[name] = (a.reshape(shp) for a in (g2, dl, nm, nv))

    def pack_small(prefix):
        flat = jnp.concatenate([args[prefix + n].reshape(-1) for n in SMALL])
        return _pad_rows(flat, SUBLANES)

    n_small = red_small.shape[0]
    gs = _pad_rows(red_small, SUBLANES)
    dl, nm, nv = _adamw(pack_small(""), gs, pack_small("m_"), pack_small("v_"), _pick_tile(gs.shape[0], 256, 8))
    off = 0
    for name in SMALL:
        shp = args[name].shape
        k = int(np.prod(shp))
        for dst, src in ((grad, gs), (delta_w, dl), (new_m, nm), (new_v, nv)):
            dst[name] = src.reshape(-1)[off:off + k].reshape(shp)
        off += k
    assert off == n_small

    return (loss, grad_x, *[grad[n] for n in order], *[delta_w[n] for n in order],
            *[new_m[n] for n in order], *[new_v[n] for n in order])


def _both(tab, tab_adj):
    return jnp.concatenate([tab, tab_adj], axis=0)
```

```python
import functools
import math

import numpy as np
import jax
import jax.numpy as jnp
from jax import lax
from jax.experimental import pallas as pl
from jax.experimental.pallas import tpu as pltpu

F32 = jnp.float32
BF16 = jnp.bfloat16

N_META = 16
GRID_W = 64
HEAD_DIM = 64
GQA_REP = 4
SSM_GROUP = 16
SSM_STATE = 64
ROPE_THETA = 10000.0
NORM_EPS = 1e-6
EIG_RE_MAX = -1e-4
ADAM_LR, ADAM_B1, ADAM_B2, ADAM_EPS, ADAM_WD, ADAM_STEP = 0.001, 0.9, 0.999, 1e-08, 0.01, 10

SUBLANES = 8
LANES = 128
CHUNK = 256
KSTEPS = CHUNK // SUBLANES
SCAN_LANES = 512
MXU_DIM = 256
SSM_BLOCK = MXU_DIM
SEQ_ALIGN = MXU_DIM
ATTN_Q_TILE = 384
ATTN_K_TILE = 11 * MXU_DIM
VMEM_LIMIT = 56 << 20
MASK_VALUE = -1e30
MESH_ID = pl.DeviceIdType.MESH


def _dot(a, b):
    return jnp.dot(a, b, preferred_element_type=F32)


def _dot_nt(a, b):
    return lax.dot_general(a, b, (((1,), (1,)), ((), ())), preferred_element_type=F32)


def _dot_tn(a, b):
    return lax.dot_general(a, b, (((0,), (0,)), ((), ())), preferred_element_type=F32)


def _row(tm, width):
    return pl.BlockSpec((tm, width), lambda i: (i, 0))


def _full(shape):
    nd = len(shape)
    return pl.BlockSpec(shape, lambda i: (0,) * nd)


def _params(sem):
    return pltpu.CompilerParams(dimension_semantics=sem, vmem_limit_bytes=VMEM_LIMIT)


def _pick_tile(n, cap, mult=16):
    best = None
    for t in range(mult, min(n, cap) + 1, mult):
        if n % t == 0:
            best = t
    assert best is not None, (n, cap)
    return best


def _rstd(x):
    return lax.rsqrt(jnp.mean(x * x, axis=-1, keepdims=True) + NORM_EPS)


def _rms(x, g):
    return x * _rstd(x) * g


def _rms_bwd(dy, x, g):
    r = _rstd(x)
    xh = x * r
    gdy = dy * g
    dx = r * (gdy - xh * jnp.mean(gdy * xh, axis=-1, keepdims=True))
    return dx, dy * xh


def _split_dot(x, m):
    hi = x.astype(BF16)
    lo = (x - hi.astype(F32)).astype(BF16)
    return _dot(hi, m) + _dot(lo, m)


def _sigmoid(x):
    return 1.0 / (1.0 + jnp.exp(-x))


def _acc_rows(ref, val, first):
    s = jnp.sum(val, axis=0, keepdims=True)

    @pl.when(first)
    def _():
        ref[...] = s

    @pl.when(jnp.logical_not(first))
    def _():
        ref[...] += s


def _in_proj(xin, g, w4, tm):
    lp, d = xin.shape
    hd = d // 2

    def body(x_ref, g_ref, w_ref, u_ref, qkv_ref, gt_ref):
        h = _rms(x_ref[...], g_ref[...]).astype(BF16)
        p0 = _dot(h, w_ref[0])
        u_ref[...] = p0[:, :hd]
        qkv_ref[:, :hd] = p0[:, hd:]
        qkv_ref[:, hd:] = _dot(h, w_ref[1])
        gt_ref[:, :d] = _dot(h, w_ref[2])
        gt_ref[:, d:] = _dot(h, w_ref[3])

    return pl.pallas_call(
        body, name="in_proj", grid=(lp // tm,),
        in_specs=[_row(tm, d), _full((1, d)), _full((4, d, d))],
        out_specs=[_row(tm, hd), _row(tm, 3 * hd), _row(tm, 2 * d)],
        out_shape=[jax.ShapeDtypeStruct((lp, hd), F32), jax.ShapeDtypeStruct((lp, 3 * hd), F32),
                   jax.ShapeDtypeStruct((lp, 2 * d), F32)],
        compiler_params=_params(("parallel",)),
    )(xin, g, w4)


def _gelu(y):
    return 0.5 * y * (1.0 + lax.erf(y * (1.0 / math.sqrt(2.0))))


def _gelu_grad(y):
    return 0.5 * (1.0 + lax.erf(y * (1.0 / math.sqrt(2.0)))) + y * jnp.exp(-0.5 * y * y) * (1.0 / math.sqrt(2.0 * math.pi))


def _glu_fwd(u, y0, y1, dskip, w_glu, b_glu, tm):
    lp, w = u.shape

    def body(u_ref, y0_ref, y1_ref, d_ref, w_ref, b_ref, o_ref):
        y = u_ref[...] * d_ref[...] + y0_ref[...] + y1_ref[...]
        z = _gelu(y)
        t = _dot(z.astype(BF16), w_ref[...]) + b_ref[...]
        o_ref[...] = (z * _sigmoid(t)).astype(BF16)

    return pl.pallas_call(
        body, name="glu_fwd", grid=(lp // tm,),
        in_specs=[_row(tm, w), _row(tm, w), _row(tm, w), _full((1, w)), _full((w, w)), _full((1, w))],
        out_specs=_row(tm, w), out_shape=jax.ShapeDtypeStruct((lp, w), BF16),
        compiler_params=_params(("parallel",)),
    )(u, y0, y1, dskip, w_glu, b_glu)


def _glu_bwd(dyssm, u, y0, y1, dskip, w_glu, b_glu, tm):
    lp, w = u.shape

    def body(g_ref, u_ref, y0_ref, y1_ref, d_ref, w_ref, b_ref, dy_ref, dw_ref, db_ref, dd_ref):
        first = pl.program_id(0) == 0
        uu = u_ref[...]
        y = uu * d_ref[...] + y0_ref[...] + y1_ref[...]
        z = _gelu(y)
        zb = z.astype(BF16)
        sg = _sigmoid(_dot(zb, w_ref[...]) + b_ref[...])
        g = g_ref[...]
        dt = g * z * sg * (1.0 - sg)
        dtb = dt.astype(BF16)
        dz = g * sg + _dot_nt(dtb, w_ref[...])
        dy = dz * _gelu_grad(y)
        dy_ref[...] = dy
        dw = _dot_tn(zb, dtb)

        @pl.when(first)
        def _():
            dw_ref[...] = dw

        @pl.when(jnp.logical_not(first))
        def _():
            dw_ref[...] += dw

        _acc_rows(db_ref, dt, first)
        _acc_rows(dd_ref, dy * uu, first)

    return pl.pallas_call(
        body, name="glu_bwd", grid=(lp // tm,),
        in_specs=[_row(tm, w), _row(tm, w), _row(tm, w), _row(tm, w), _full((1, w)), _full((w, w)), _full((1, w))],
        out_specs=[_row(tm, w), _full((w, w)), _full((1, w)), _full((1, w))],
        out_shape=[jax.ShapeDtypeStruct((lp, w), F32), jax.ShapeDtypeStruct((w, w), F32),
                   jax.ShapeDtypeStruct((1, w), F32), jax.ShapeDtypeStruct((1, w), F32)],
        compiler_params=_params(("arbitrary",)),
    )(dyssm, u, y0, y1, dskip, w_glu, b_glu)


def _merge_fwd(yssm, o, gates, xin, wsp4, wap, wo, tm):
    lp, d = xin.shape
    w = yssm.shape[1]
    ns = d // 4

    def body(y_ref, o_ref, g_ref, x_ref, wsp_ref, wap_ref, wo_ref, h_ref, m_ref):
        yb = y_ref[...]
        ms = jnp.concatenate([_dot(yb, wsp_ref[s]) for s in range(4)], axis=1)
        ma = _dot(o_ref[...], wap_ref[...])
        merged = (_sigmoid(g_ref[:, :d]) * ms + _sigmoid(g_ref[:, d:]) * ma).astype(BF16)
        m_ref[...] = merged
        h_ref[...] = x_ref[...] + _dot(merged, wo_ref[...])

    return pl.pallas_call(
        body, name="merge_fwd", grid=(lp // tm,),
        in_specs=[_row(tm, w), _row(tm, d), _row(tm, 2 * d), _row(tm, d),
                  _full((4, w, ns)), _full((d, d)), _full((d, d))],
        out_specs=[_row(tm, d), _row(tm, d)],
        out_shape=[jax.ShapeDtypeStruct((lp, d), F32), jax.ShapeDtypeStruct((lp, d), BF16)],
        compiler_params=_params(("parallel",)),
    )(yssm, o, gates, xin, wsp4, wap, wo)


def _merge_bwd(dh1, yssm, o, gates, wsp4, wap, wo, sel, tm):
    lp, d = dh1.shape
    w = yssm.shape[1]
    ns = d // 4
    nsel = sel.shape[1]

    def body(dh_ref, y_ref, o_ref, g_ref, wsp_ref, wap_ref, wo_ref, sel_ref,
             dg_ref, dms_ref, dma_ref, dy_ref, do_ref, dl_ref, dhb_ref):
        dhb = dh_ref[...].astype(BF16)
        dhb_ref[...] = dhb
        dm = _dot_nt(dhb, wo_ref[...])
        yb = y_ref[...]
        ob = o_ref[...]
        ms = jnp.concatenate([_dot(yb, wsp_ref[s]) for s in range(4)], axis=1)
        ma = _dot(ob, wap_ref[...])
        ss = _sigmoid(g_ref[:, :d])
        sa = _sigmoid(g_ref[:, d:])
        dg_ref[:, :d] = dm * ms * ss * (1.0 - ss)
        dg_ref[:, d:] = dm * ma * sa * (1.0 - sa)
        dms = (dm * ss).astype(BF16)
        dma = (dm * sa).astype(BF16)
        dms_ref[...] = dms
        dma_ref[...] = dma
        dy = _dot_nt(dms[:, :ns], wsp_ref[0])
        for s in range(1, 4):
            dy += _dot_nt(dms[:, s * ns:(s + 1) * ns], wsp_ref[s])
        dy_ref[...] = dy
        do = _dot_nt(dma, wap_ref[...])
        do_ref[...] = do.astype(BF16)
        dl_ref[...] = _split_dot(do * ob.astype(F32), sel_ref[...])

    return pl.pallas_call(
        body, name="merge_bwd", grid=(lp // tm,),
        in_specs=[_row(tm, d), _row(tm, w), _row(tm, d), _row(tm, 2 * d),
                  _full((4, w, ns)), _full((d, d)), _full((d, d)), _full((d, nsel))],
        out_specs=[_row(tm, 2 * d), _row(tm, d), _row(tm, d), _row(tm, w), _row(tm, d), _row(tm, nsel), _row(tm, d)],
        out_shape=[jax.ShapeDtypeStruct((lp, 2 * d), F32), jax.ShapeDtypeStruct((lp, d), BF16),
                   jax.ShapeDtypeStruct((lp, d), BF16), jax.ShapeDtypeStruct((lp, w), F32),
                   jax.ShapeDtypeStruct((lp, d), BF16), jax.ShapeDtypeStruct((lp, nsel), F32),
                   jax.ShapeDtypeStruct((lp, d), BF16)],
        compiler_params=_params(("parallel",)),
    )(dh1, yssm, o, gates, wsp4, wap, wo, sel)


def _mlp_in(h1, g, w4, tm):
    lp, d = h1.shape

    def body(x_ref, g_ref, w_ref, r_ref):
        h = _rms(x_ref[...], g_ref[...]).astype(BF16)
        for s in range(4):
            r_ref[:, s * d:(s + 1) * d] = jnp.maximum(_dot(h, w_ref[s]), 0.0).astype(BF16)

    return pl.pallas_call(
        body, name="mlp_in", grid=(lp // tm,),
        in_specs=[_row(tm, d), _full((1, d)), _full((4, d, d))],
        out_specs=_row(tm, 4 * d), out_shape=jax.ShapeDtypeStruct((lp, 4 * d), BF16),
        compiler_params=_params(("parallel",)),
    )(h1, g, w4)


def _square_bf16(r):
    rf = r.astype(F32)
    return (rf * rf).astype(BF16)


def _mlp_out(h1, r, w2, tm):
    lp, d = h1.shape
    ff = r.shape[1]

    def body(x_ref, r_ref, w_ref, o_ref):
        o_ref[...] = x_ref[...] + _dot(_square_bf16(r_ref[...]), w_ref[...])

    return pl.pallas_call(
        body, name="mlp_out", grid=(lp // tm,),
        in_specs=[_row(tm, d), _row(tm, ff), _full((ff, d))],
        out_specs=_row(tm, d), out_shape=jax.ShapeDtypeStruct((lp, d), F32),
        compiler_params=_params(("parallel",)),
    )(h1, r, w2)


def _final_loss(h3, g, tgt, rowmask, tm):
    lp, d = h3.shape

    def body(x_ref, g_ref, t_ref, m_ref, loss_ref, dx_ref, dg_ref):
        first = pl.program_id(0) == 0
        x = x_ref[...]
        gg = g_ref[...]
        err = (_rms(x, gg) - t_ref[...]) * m_ref[...]
        part = 0.5 * jnp.sum(jnp.sum(err * err, axis=1, keepdims=True), axis=0, keepdims=True) * (1.0 / d)
        part = jnp.broadcast_to(part, (SUBLANES, LANES))

        @pl.when(first)
        def _():
            loss_ref[...] = part

        @pl.when(jnp.logical_not(first))
        def _():
            loss_ref[...] += part

        dx, dgr = _rms_bwd(err * (1.0 / d), x, gg)
        dx_ref[...] = dx
        _acc_rows(dg_ref, dgr, first)

    return pl.pallas_call(
        body, name="final_loss", grid=(lp // tm,),
        in_specs=[_row(tm, d), _full((1, d)), _row(tm, d), _row(tm, 1)],
        out_specs=[_full((SUBLANES, LANES)), _row(tm, d), _full((1, d))],
        out_shape=[jax.ShapeDtypeStruct((SUBLANES, LANES), F32), jax.ShapeDtypeStruct((lp, d), F32),
                   jax.ShapeDtypeStruct((1, d), F32)],
        compiler_params=_params(("arbitrary",)),
    )(h3, g, tgt, rowmask)


def _mlp_bwd_a(dh3, r, w2, tm):
    lp, d = dh3.shape
    ff = r.shape[1]

    def body(dh_ref, r_ref, w_ref, dz_ref, dhb_ref):
        dhb = dh_ref[...].astype(BF16)
        dhb_ref[...] = dhb
        da = _dot_nt(dhb, w_ref[...])
        dz_ref[...] = (da * (2.0 * r_ref[...].astype(F32))).astype(BF16)

    return pl.pallas_call(
        body, name="mlp_bwd_a", grid=(lp // tm,),
        in_specs=[_row(tm, d), _row(tm, ff), _full((ff, d))],
        out_specs=[_row(tm, ff), _row(tm, d)],
        out_shape=[jax.ShapeDtypeStruct((lp, ff), BF16), jax.ShapeDtypeStruct((lp, d), BF16)],
        compiler_params=_params(("parallel",)),
    )(dh3, r, w2)


def _mlp_bwd_b(dz, dh3, h1, g, w4, tm):
    lp, d = h1.shape

    def body(dz_ref, dh_ref, x_ref, g_ref, w_ref, dx_ref, dg_ref):
        first = pl.program_id(0) == 0
        dh2 = _dot_nt(dz_ref[:, :d], w_ref[0])
        for s in range(1, 4):
            dh2 += _dot_nt(dz_ref[:, s * d:(s + 1) * d], w_ref[s])
        dx, dgr = _rms_bwd(dh2, x_ref[...], g_ref[...])
        dx_ref[...] = dh_ref[...] + dx
        _acc_rows(dg_ref, dgr, first)

    return pl.pallas_call(
        body, name="mlp_bwd_b", grid=(lp // tm,),
        in_specs=[_row(tm, 4 * d), _row(tm, d), _row(tm, d), _full((1, d)), _full((4, d, d))],
        out_specs=[_row(tm, d), _full((1, d))],
        out_shape=[jax.ShapeDtypeStruct((lp, d), F32), jax.ShapeDtypeStruct((1, d), F32)],
        compiler_params=_params(("arbitrary",)),
    )(dz, dh3, h1, g, w4)


def _in_proj_bwd(dyv, du0, du1, dskip, dqkv, dgates, dres, xin, g, w4, tm):
    lp, d = xin.shape
    hd = d // 2

    def body(dy_ref, a_ref, b_ref, ds_ref, dq_ref, dgt_ref, dr_ref, x_ref, g_ref, w_ref, dx_ref, dg_ref, dp_ref):
        first = pl.program_id(0) == 0
        du = (dy_ref[...] * ds_ref[...] + a_ref[...] + b_ref[...]).astype(BF16)
        dq = dq_ref[...].astype(BF16)
        dgt = dgt_ref[...].astype(BF16)
        dp_ref[:, :hd] = du
        dp_ref[:, hd:2 * d] = dq
        dp_ref[:, 2 * d:] = dgt
        dh = _dot_nt(du, w_ref[0, :, :hd]) + _dot_nt(dq[:, :hd], w_ref[0, :, hd:])
        dh += _dot_nt(dq[:, hd:], w_ref[1])
        dh += _dot_nt(dgt[:, :d], w_ref[2]) + _dot_nt(dgt[:, d:], w_ref[3])
        dx, dgr = _rms_bwd(dh, x_ref[...], g_ref[...])
        dx_ref[...] = dr_ref[...] + dx
        _acc_rows(dg_ref, dgr, first)

    return pl.pallas_call(
        body, name="in_proj_bwd", grid=(lp // tm,),
        in_specs=[_row(tm, hd), _row(tm, hd), _row(tm, hd), _full((1, hd)), _row(tm, 3 * hd), _row(tm, 2 * d),
                  _row(tm, d), _row(tm, d), _full((1, d)), _full((4, d, d))],
        out_specs=[_row(tm, d), _full((1, d)), _row(tm, 4 * d)],
        out_shape=[jax.ShapeDtypeStruct((lp, d), F32), jax.ShapeDtypeStruct((1, d), F32),
                   jax.ShapeDtypeStruct((lp, 4 * d), BF16)],
        compiler_params=_params(("arbitrary",)),
    )(dyv, du0, du1, dskip, dqkv, dgates, dres, xin, g, w4)


def _wgrad(a, dy, nshard, tm, tn, name, gain=None, square=False):
    lp, k = a.shape
    n = dy.shape[1]
    ns = n // nshard
    assert ns % tn == 0
    per = ns // tn

    def body(*refs):
        if gain is not None:
            a_ref, g_ref, dy_ref, o_ref = refs
            at = _rms(a_ref[...], g_ref[...]).astype(BF16)
        else:
            a_ref, dy_ref, o_ref = refs
            at = _square_bf16(a_ref[...]) if square else a_ref[...]
        i = pl.program_id(1)
        acc = _dot_tn(at, dy_ref[...])

        @pl.when(i == 0)
        def _():
            o_ref[0] = acc

        @pl.when(i != 0)
        def _():
            o_ref[0] += acc

    in_specs = [pl.BlockSpec((tm, k), lambda j, i: (i, 0))]
    args = [a]
    if gain is not None:
        in_specs.append(pl.BlockSpec((1, k), lambda j, i: (0, 0)))
        args.append(gain)
    in_specs.append(pl.BlockSpec((tm, tn), lambda j, i: (i, j)))
    args.append(dy)
    return pl.pallas_call(
        body, name=name, grid=(n // tn, lp // tm), in_specs=in_specs,
        out_specs=pl.BlockSpec((1, k, tn), lambda j, i: (j // per, 0, j % per)),
        out_shape=jax.ShapeDtypeStruct((nshard, k, ns), F32),
        compiler_params=_params(("parallel", "arbitrary")),
    )(*args)


def _head_tables(d):
    idx = np.arange(LANES)
    mean = (idx[:, None] // HEAD_DIM == idx[None, :] // HEAD_DIM).astype(np.float32) / HEAD_DIM
    n_heads = d // HEAD_DIM
    kvh = n_heads // GQA_REP
    c = np.arange(d)
    col = np.arange(kvh * LANES)
    head_of_col = (col // LANES) * GQA_REP + (col % LANES)
    sel = ((c[:, None] // HEAD_DIM == head_of_col[None, :]) & ((col % LANES) < GQA_REP)[None, :]).astype(np.float32)
    return jnp.asarray(mean, BF16), jnp.asarray(sel, BF16)


def _swap_pairs(y):
    lane = lax.broadcasted_iota(jnp.int32, y.shape, 1)
    return jnp.where(lane % 2 == 0, pltpu.roll(y, LANES - 1, 1), pltpu.roll(y, 1, 1))


def _qk_prep(qkv, cos_t, sin_t, qg, kg, mean_m, tm):
    lp, wq = qkv.shape
    d = wq * 2 // 3
    kvw = d // 4
    kvh = kvw // HEAD_DIM
    scale = HEAD_DIM ** -0.5

    def body(x_ref, c_ref, s_ref, qg_ref, kg_ref, m_ref, q_ref, k_ref, v_ref):
        cs, sn, mm = c_ref[...], s_ref[...], m_ref[...]
        for b in range((d + kvw) // LANES):
            x = x_ref[:, b * LANES:(b + 1) * LANES]
            gg = qg_ref[...] if b < d // LANES else kg_ref[...]
            y = x * lax.rsqrt(_split_dot(x * x, mm) + NORM_EPS) * gg
            out = y * cs + _swap_pairs(y) * sn
            if b < d // LANES:
                q_ref[:, b * LANES:(b + 1) * LANES] = (out * scale).astype(BF16)
            else:
                kb = b - d // LANES
                k_ref[2 * kb] = out[:, :HEAD_DIM].astype(BF16)
                k_ref[2 * kb + 1] = out[:, HEAD_DIM:].astype(BF16)
        lane = lax.broadcasted_iota(jnp.int32, (tm, LANES - HEAD_DIM), 1)
        ones_col = (lane == 0).astype(BF16)
        for h in range(kvh):
            vh = x_ref[:, d + kvw + h * HEAD_DIM:d + kvw + (h + 1) * HEAD_DIM].astype(BF16)
            v_ref[h] = jnp.concatenate([vh, ones_col], axis=1)

    k_spec = pl.BlockSpec((kvh, tm, HEAD_DIM), lambda i: (0, i, 0))
    v_spec = pl.BlockSpec((kvh, tm, LANES), lambda i: (0, i, 0))
    return pl.pallas_call(
        body, name="qk_prep", grid=(lp // tm,),
        in_specs=[_row(tm, wq), _row(tm, LANES), _row(tm, LANES), _full((1, LANES)), _full((1, LANES)),
                  _full((LANES, LANES))],
        out_specs=[_row(tm, d), k_spec, v_spec],
        out_shape=[jax.ShapeDtypeStruct((lp, d), BF16), jax.ShapeDtypeStruct((kvh, lp, HEAD_DIM), BF16),
                   jax.ShapeDtypeStruct((kvh, lp, LANES), BF16)],
        compiler_params=_params(("parallel",)),
    )(qkv, cos_t, sin_t, qg, kg, mean_m)


def _qk_bwd(qkv, dq, dk, dv, cos_t, sin_t, qg, kg, mean_m, tm):
    lp, wq = qkv.shape
    d = wq * 2 // 3
    kvw = d // 4
    kvh = kvw // HEAD_DIM
    scale = HEAD_DIM ** -0.5

    def body(x_ref, dq_ref, dk_ref, dv_ref, c_ref, s_ref, qg_ref, kg_ref, m_ref, o_ref, dqg_ref, dkg_ref):
        first = pl.program_id(0) == 0
        cs, sn, mm = c_ref[...], s_ref[...], m_ref[...]
        sums = [None, None]
        for b in range((d + kvw) // LANES):
            is_q = b < d // LANES
            x = x_ref[:, b * LANES:(b + 1) * LANES]
            gg = qg_ref[...] if is_q else kg_ref[...]
            r = lax.rsqrt(_split_dot(x * x, mm) + NORM_EPS)
            nrm = x * r
            if is_q:
                dout = dq_ref[:, b * LANES:(b + 1) * LANES] * scale
            else:
                kb = b - d // LANES
                dout = jnp.concatenate([dk_ref[2 * kb], dk_ref[2 * kb + 1]], axis=1)
            dy = dout * cs + _swap_pairs(dout * sn)
            part = jnp.sum(dy * nrm, axis=0, keepdims=True)
            sums[0 if is_q else 1] = part if sums[0 if is_q else 1] is None else sums[0 if is_q else 1] + part
            dn = dy * gg
            o_ref[:, b * LANES:(b + 1) * LANES] = r * (dn - nrm * _split_dot(dn * nrm, mm))
        for h in range(kvh):
            o_ref[:, d + kvw + h * HEAD_DIM:d + kvw + (h + 1) * HEAD_DIM] = dv_ref[h]
        for ref, s in ((dqg_ref, sums[0]), (dkg_ref, sums[1])):
            s = s + pltpu.roll(s, HEAD_DIM, 1)

            @pl.when(first)
            def _(ref=ref, s=s):
                ref[...] = s

            @pl.when(jnp.logical_not(first))
            def _(ref=ref, s=s):
                ref[...] += s

    kv_spec = pl.BlockSpec((kvh, tm, HEAD_DIM), lambda i: (0, i, 0))
    return pl.pallas_call(
        body, name="qk_bwd", grid=(lp // tm,),
        in_specs=[_row(tm, wq), _row(tm, d), kv_spec, kv_spec, _row(tm, LANES), _row(tm, LANES),
                  _full((1, LANES)), _full((1, LANES)), _full((LANES, LANES))],
        out_specs=[_row(tm, wq), _full((1, LANES)), _full((1, LANES))],
        out_shape=[jax.ShapeDtypeStruct((lp, wq), F32), jax.ShapeDtypeStruct((1, LANES), F32),
                   jax.ShapeDtypeStruct((1, LANES), F32)],
        compiler_params=_params(("arbitrary",)),
    )(qkv, dq, dk, dv, cos_t, sin_t, qg, kg, mean_m)


def _attn_fwd(q, k, v, kbias, tq, tk):
    lp, d = q.shape
    kvh = k.shape[0]
    rw = GQA_REP * HEAD_DIM
    nk = lp // tk

    def body(q_ref, k_ref, v_ref, kb_ref, o_ref, lse_ref, m_s, acc_s):
        j = pl.program_id(2)

        @pl.when(j == 0)
        def _():
            m_s[...] = jnp.full(m_s.shape, MASK_VALUE, F32)
            acc_s[...] = jnp.zeros(acc_s.shape, F32)

        def heads(masked):
            kk, vv = k_ref[0], v_ref[0]

            def scores(h):
                return _dot_nt(q_ref[:, h * HEAD_DIM:(h + 1) * HEAD_DIM], kk)

            def softmax(h, s):
                if masked:
                    s = s + kb_ref[...]
                m_prev = m_s[h]
                m_new = jnp.maximum(m_prev, jnp.max(s, axis=1, keepdims=True))
                m_s[h] = m_new
                return jnp.exp(s - m_new[:, :1]).astype(BF16), jnp.exp(m_prev - m_new)

            def accumulate(h, p, alpha):
                acc_s[h] = acc_s[h] * alpha + _dot(p, vv)

            ss = [scores(h) for h in range(GQA_REP)]
            pa = [softmax(h, ss[h]) for h in range(GQA_REP)]
            for h in range(GQA_REP):
                accumulate(h, *pa[h])

        pl.when(j != nk - 1)(functools.partial(heads, False))
        pl.when(j == nk - 1)(functools.partial(heads, True))

        @pl.when(j == nk - 1)
        def _():
            lane = lax.broadcasted_iota(jnp.int32, (tq, LANES), 1)
            lse = jnp.zeros((tq, LANES), F32)
            outs = []
            for h in range(GQA_REP):
                acc = acc_s[h]
                l = acc[:, HEAD_DIM:HEAD_DIM + 1]
                outs.append(acc[:, :HEAD_DIM] / l)
                lse = jnp.where(lane == h, m_s[h][:, :1] + jnp.log(l), lse)
            o_ref[...] = jnp.concatenate(outs, axis=1).astype(BF16)
            lse_ref[...] = lse

    return pl.pallas_call(
        body, name="attn_fwd", grid=(kvh, lp // tq, nk),
        in_specs=[pl.BlockSpec((tq, rw), lambda g, i, j: (i, g)),
                  pl.BlockSpec((1, tk, HEAD_DIM), lambda g, i, j: (g, j, 0)),
                  pl.BlockSpec((1, tk, LANES), lambda g, i, j: (g, j, 0)),
                  pl.BlockSpec((1, tk), lambda g, i, j: (0, j))],
        out_specs=[pl.BlockSpec((tq, rw), lambda g, i, j: (i, g)),
                   pl.BlockSpec((tq, LANES), lambda g, i, j: (i, g))],
        out_shape=[jax.ShapeDtypeStruct((lp, d), BF16), jax.ShapeDtypeStruct((lp, kvh * LANES), F32)],
        scratch_shapes=[pltpu.VMEM((GQA_REP, tq, LANES), F32), pltpu.VMEM((GQA_REP, tq, LANES), F32)],
        compiler_params=_params(("parallel", "parallel", "arbitrary")),
    )(q, k, v, kbias)


def _attn_bwd(q, k, v, kbias, do, lse, delta, tq, tk):
    lp, d = q.shape
    kvh = k.shape[0]
    rw = GQA_REP * HEAD_DIM
    nq = lp // tq

    def body(q_ref, k_ref, v_ref, kb_ref, do_ref, lse_ref, dl_ref, dq_ref, dk_ref, dv_ref, dk_s, dv_s):
        j = pl.program_id(1)
        i = pl.program_id(2)

        @pl.when(jnp.logical_and(i == 0, j == 0))
        def _():
            dq_ref[...] = jnp.zeros(dq_ref.shape, F32)

        @pl.when(i == 0)
        def _():
            dk_s[...] = jnp.zeros(dk_s.shape, F32)
            dv_s[...] = jnp.zeros(dv_s.shape, F32)

        def heads(masked):
            kk, vv = k_ref[0], v_ref[0][:, :HEAD_DIM]
            lse, dl = lse_ref[...], dl_ref[...]
            dqs = []
            for h in range(GQA_REP):
                qh = q_ref[:, h * HEAD_DIM:(h + 1) * HEAD_DIM]
                doh = do_ref[:, h * HEAD_DIM:(h + 1) * HEAD_DIM]
                s = _dot_nt(qh, kk)
                if masked:
                    s = s + kb_ref[...]
                p = jnp.exp(s - lse[:, h:h + 1])
                ds = (p * (_dot_nt(doh, vv) - dl[:, h:h + 1])).astype(BF16)
                dv_s[...] += _dot_tn(p.astype(BF16), doh)
                dk_s[...] += _dot_tn(ds, qh)
                dqs.append(_dot(ds, kk))
            rows = pl.ds(pl.multiple_of(i * tq, tq), tq)
            dq_ref[rows, :] += jnp.concatenate(dqs, axis=1)

        nk = lp // tk
        pl.when(j != nk - 1)(functools.partial(heads, False))
        pl.when(j == nk - 1)(functools.partial(heads, True))

        @pl.when(i == nq - 1)
        def _():
            dk_ref[0] = dk_s[...]
            dv_ref[0] = dv_s[...]

    return pl.pallas_call(
        body, name="attn_bwd", grid=(kvh, lp // tk, nq),
        in_specs=[pl.BlockSpec((tq, rw), lambda g, j, i: (i, g)),
                  pl.BlockSpec((1, tk, HEAD_DIM), lambda g, j, i: (g, j, 0)),
                  pl.BlockSpec((1, tk, LANES), lambda g, j, i: (g, j, 0)),
                  pl.BlockSpec((1, tk), lambda g, j, i: (0, j)),
                  pl.BlockSpec((tq, rw), lambda g, j, i: (i, g)),
                  pl.BlockSpec((tq, LANES), lambda g, j, i: (i, g)),
                  pl.BlockSpec((tq, LANES), lambda g, j, i: (i, g))],
        out_specs=[pl.BlockSpec((lp, rw), lambda g, j, i: (0, g)),
                   pl.BlockSpec((1, tk, HEAD_DIM), lambda g, j, i: (g, j, 0)),
                   pl.BlockSpec((1, tk, HEAD_DIM), lambda g, j, i: (g, j, 0))],
        out_shape=[jax.ShapeDtypeStruct((lp, d), F32), jax.ShapeDtypeStruct((kvh, lp, HEAD_DIM), F32),
                   jax.ShapeDtypeStruct((kvh, lp, HEAD_DIM), F32)],
        scratch_shapes=[pltpu.VMEM((tk, HEAD_DIM), F32), pltpu.VMEM((tk, HEAD_DIM), F32)],
        compiler_params=_params(("parallel", "arbitrary", "arbitrary")),
    )(q, k, v, kbias, do, lse, delta)


def _ssm_math(a_re, a_im, log_dt, bt_re, bt_im):
    dt = jnp.exp(log_dt)
    lam_re = jnp.minimum(a_re, EIG_RE_MAX)
    lam_im = a_im
    mag = jnp.exp(lam_re * dt)
    ang = lam_im * dt
    lb_re = mag * jnp.cos(ang)
    lb_im = mag * jnp.sin(ang)
    num_re = lb_re - 1.0
    num_im = lb_im
    den = lam_re * lam_re + lam_im * lam_im
    f_re = (num_re * lam_re + num_im * lam_im) / den
    f_im = (num_im * lam_re - num_re * lam_im) / den
    bb_re = f_re[:, None, :] * bt_re - f_im[:, None, :] * bt_im
    bb_im = f_re[:, None, :] * bt_im + f_im[:, None, :] * bt_re
    return lb_re, lb_im, bb_re, bb_im


def _ssm_discretize(a_re, a_im, log_dt, bt_re, bt_im):
    nd, g, n = a_re.shape
    p = bt_re.shape[2]

    def body(ar_ref, ai_ref, ld_ref, br_ref, bi_ref, bbr_ref, bbi_ref, pr_ref, pi_ref, hr_ref, hi_ref):
        lb_re, lb_im, bb_re, bb_im = _ssm_math(ar_ref[0], ai_ref[0], ld_ref[0], br_ref[0], bi_ref[0])
        bbr_ref[0] = bb_re
        bbi_ref[0] = bb_im
        cr, ci = lb_re, lb_im
        for k in range(KSTEPS):
            pr_ref[0, k] = cr
            pi_ref[0, k] = ci
            if k < KSTEPS - 1:
                cr, ci = cr * lb_re - ci * lb_im, cr * lb_im + ci * lb_re
        for t in range(2):
            cr, ci = cr * cr - ci * ci, 2.0 * cr * ci
            hr_ref[0, t] = cr
            hi_ref[0, t] = ci

    s3 = pl.BlockSpec((1, g, n), lambda i: (i, 0, 0))
    s4 = pl.BlockSpec((1, g, p, n), lambda i: (i, 0, 0, 0))
    sp = pl.BlockSpec((1, KSTEPS, g, n), lambda i: (i, 0, 0, 0))
    sh = pl.BlockSpec((1, 2, g, n), lambda i: (i, 0, 0, 0))
    return pl.pallas_call(
        body, name="ssm_discretize", grid=(nd,),
        in_specs=[s3, s3, pl.BlockSpec((1, g, 1), lambda i: (i, 0, 0)), s4, s4],
        out_specs=[s4, s4, sp, sp, sh, sh],
        out_shape=[jax.ShapeDtypeStruct((nd, g, p, n), F32)] * 2 + [jax.ShapeDtypeStruct((nd, KSTEPS, g, n), F32)] * 2
        + [jax.ShapeDtypeStruct((nd, 2, g, n), F32)] * 2,
        compiler_params=_params(("parallel",)),
    )(a_re, a_im, log_dt, bt_re, bt_im)


def _ssm_param_bwd(a_re, a_im, log_dt, bt_re, bt_im, dlb_re, dlb_im, dbb_re, dbb_im):
    nd, g, n = a_re.shape
    p = bt_re.shape[2]

    def body(ar_ref, ai_ref, ld_ref, br_ref, bi_ref, c0_ref, c1_ref, c2_ref, c3_ref,
             o0_ref, o1_ref, o2_ref, o3_ref, o4_ref):
        _, vjp = jax.vjp(_ssm_math, ar_ref[0], ai_ref[0], ld_ref[0], br_ref[0], bi_ref[0])
        outs = vjp((c0_ref[0], c1_ref[0], c2_ref[0], c3_ref[0]))
        for ref, val in zip((o0_ref, o1_ref, o2_ref, o3_ref, o4_ref), outs):
            ref[0] = val

    s3 = pl.BlockSpec((1, g, n), lambda i: (i, 0, 0))
    s1 = pl.BlockSpec((1, g, 1), lambda i: (i, 0, 0))
    s4 = pl.BlockSpec((1, g, p, n), lambda i: (i, 0, 0, 0))
    return pl.pallas_call(
        body, name="ssm_param_bwd", grid=(nd,),
        in_specs=[s3, s3, s1, s4, s4, s3, s3, s4, s4],
        out_specs=[s3, s3, s1, s4, s4],
        out_shape=[jax.ShapeDtypeStruct((nd, g, n), F32)] * 2 + [jax.ShapeDtypeStruct((nd, g, 1), F32)]
        + [jax.ShapeDtypeStruct((nd, g, p, n), F32)] * 2,
        compiler_params=_params(("parallel",)),
    )(a_re, a_im, log_dt, bt_re, bt_im, dlb_re, dlb_im, dbb_re, dbb_im)


def _cmul(ar, ai, xr, xi, conj):
    if conj:
        return ar * xr + ai * xi, ar * xi - ai * xr
    return ar * xr - ai * xi, ar * xi + ai * xr


def _scan_chunk(buf, tab, carry, ein, nj, rev, conj, base=0):
    ks = list(range(KSTEPS))
    if rev:
        ks = ks[::-1]
    sub = lax.broadcasted_iota(jnp.int32, (SUBLANES, SCAN_LANES), 0)
    edge = sub == (SUBLANES - 1 if rev else 0)

    def step(j, _):
        jr, ji = j, nj + j
        ar, ai = tab[base, jr], tab[base, ji]
        hr = jnp.zeros((SUBLANES, SCAN_LANES), F32)
        hi = jnp.zeros((SUBLANES, SCAN_LANES), F32)
        for k in ks:
            rows = pl.ds(k * SUBLANES, SUBLANES)
            pr, pi_ = _cmul(ar, ai, hr, hi, conj)
            hr = pr + buf[jr, rows, :]
            hi = pi_ + buf[ji, rows, :]
            buf[jr, rows, :] = hr
            buf[ji, rows, :] = hi
        shift = SUBLANES - 1 if rev else 1
        er = jnp.where(edge, carry[jr], pltpu.roll(hr, shift, 0))
        ei = jnp.where(edge, carry[ji], pltpu.roll(hi, shift, 0))
        for t, dist in enumerate((1, 2, 4)):
            sh = SUBLANES - dist if rev else dist
            pr, pi_ = _cmul(tab[base + 1 + t, jr], tab[base + 1 + t, ji], pltpu.roll(er, sh, 0), pltpu.roll(ei, sh, 0), conj)
            er, ei = er + pr, ei + pi_
        ein[jr] = er
        ein[ji] = ei
        pr, pi_ = _cmul(tab[base + 4 + KSTEPS - 1, jr], tab[base + 4 + KSTEPS - 1, ji], er, ei, conj)
        last = 0 if rev else SUBLANES - 1
        carry[jr] = jnp.broadcast_to((hr + pr)[last:last + 1, :], (SUBLANES, SCAN_LANES))
        carry[ji] = jnp.broadcast_to((hi + pi_)[last:last + 1, :], (SUBLANES, SCAN_LANES))
        for n, k in enumerate(ks):
            rows = pl.ds(k * SUBLANES, SUBLANES)
            pr, pi_ = _cmul(tab[base + 4 + n, jr], tab[base + 4 + n, ji], er, ei, conj)
            buf[jr, rows, :] += pr
            buf[ji, rows, :] += pi_
        return 0

    lax.fori_loop(0, nj, step, 0)


def _state_lanes(b):
    per = SCAN_LANES // SSM_BLOCK
    return b // per, slice((b % per) * SSM_BLOCK, (b % per + 1) * SSM_BLOCK)


def _project_in(src, w_ref, buf, nj):
    nb, cb, _ = w_ref.shape
    for b in range(nb):
        res = _dot(src[:, b * cb:(b + 1) * cb], w_ref[b])
        j, lanes = _state_lanes(b)
        buf[j, :, lanes] = res[:, :SSM_BLOCK]
        buf[nj + j, :, lanes] = res[:, SSM_BLOCK:]


def _state_block(buf, b, nj):
    j, lanes = _state_lanes(b)
    return jnp.concatenate([buf[j, :, lanes], buf[nj + j, :, lanes]], axis=1).astype(BF16)


def _project_out(buf, w_ref, nj):
    return jnp.concatenate([_dot_nt(_state_block(buf, b, nj), w_ref[b]) for b in range(w_ref.shape[0])], axis=1)


def _ssm_fwd(u, wb, wct, tab, rev, name):
    lp, w = u.shape
    nb, cb, _ = wb.shape
    nj = nb * SSM_BLOCK // SCAN_LANES
    nc = lp // CHUNK
    ntab = tab.shape[0]
    cidx = (lambda c: nc - 1 - c) if rev else (lambda c: c)

    def body(u_ref, wb_ref, wct_ref, tab_ref, y_ref, ck_ref, buf, carry, ein):
        @pl.when(pl.program_id(0) == 0)
        def _():
            carry[...] = jnp.zeros(carry.shape, F32)

        _project_in(u_ref[...].astype(BF16), wb_ref, buf, nj)
        ck_ref[0] = carry[...]
        _scan_chunk(buf, tab_ref, carry, ein, nj, rev, False)
        y_ref[...] = _project_out(buf, wct_ref, nj)

    wshape = (nb, cb, 2 * SSM_BLOCK)
    return pl.pallas_call(
        body, name=name, grid=(nc,),
        in_specs=[pl.BlockSpec((CHUNK, w), lambda c: (cidx(c), 0)), _full(wshape), _full(wshape),
                  _full((ntab, 2 * nj, SUBLANES, SCAN_LANES))],
        out_specs=[pl.BlockSpec((CHUNK, w), lambda c: (cidx(c), 0)),
                   pl.BlockSpec((1, 2 * nj, SUBLANES, SCAN_LANES), lambda c: (cidx(c), 0, 0, 0))],
        out_shape=[jax.ShapeDtypeStruct((lp, w), F32), jax.ShapeDtypeStruct((nc, 2 * nj, SUBLANES, SCAN_LANES), F32)],
        scratch_shapes=[pltpu.VMEM((2 * nj, CHUNK, SCAN_LANES), F32), pltpu.VMEM((2 * nj, SUBLANES, SCAN_LANES), F32),
                        pltpu.VMEM((2 * nj, SUBLANES, SCAN_LANES), F32)],
        compiler_params=_params(("arbitrary",)),
    )(u, wb, wct, tab)


def _ssm_bwd(u, dy, ckpt, wb, wct, tab, rev, name):
    lp, w = u.shape
    nb, cb, _ = wb.shape
    nj = nb * SSM_BLOCK // SCAN_LANES
    nc = lp // CHUNK
    ntab = tab.shape[0]
    cidx = (lambda c: c) if rev else (lambda c: nc - 1 - c)

    def body(u_ref, dy_ref, ck_ref, wb_ref, wct_ref, tab_hbm, du_ref, dbb_ref, dcc_ref, dlb_ref,
             tab_ref, dwb_ref, dwc_ref, xs, ls, xcar, lcar, xin, lin):
        c = pl.program_id(0)

        @pl.when(c == 0)
        def _():
            pltpu.sync_copy(tab_hbm, tab_ref)
            lcar[...] = jnp.zeros(lcar.shape, F32)
            dwb_ref[...] = jnp.zeros(dwb_ref.shape, F32)
            dwc_ref[...] = jnp.zeros(dwc_ref.shape, F32)
            dlb_ref[...] = jnp.zeros(dlb_ref.shape, F32)

        ub = u_ref[...].astype(BF16)
        dyb = dy_ref[...].astype(BF16)
        _project_in(ub, wb_ref, xs, nj)
        xcar[...] = ck_ref[0]
        _scan_chunk(xs, tab_ref, xcar, xin, nj, rev, False)
        _project_in(dyb, wct_ref, ls, nj)
        _scan_chunk(ls, tab_ref, lcar, lin, nj, not rev, True, base=ntab // 2)
        dus = []
        for b in range(nb):
            chans = slice(b * cb, (b + 1) * cb)
            xb = _state_block(xs, b, nj)
            lb = _state_block(ls, b, nj)
            dwc_ref[b] += _dot_tn(dyb[:, chans], xb)
            dwb_ref[b] += _dot_tn(ub[:, chans], lb)
            dus.append(_dot_nt(lb, wb_ref[b]))
        du_ref[...] = jnp.concatenate(dus, axis=1)

        def step(j, _):
            jr, ji = j, nj + j
            ar = jnp.zeros((SUBLANES, SCAN_LANES), F32)
            ai = jnp.zeros((SUBLANES, SCAN_LANES), F32)
            for k in range(KSTEPS):
                kp = k + 1 if rev else k - 1
                rows = pl.ds(k * SUBLANES, SUBLANES)
                if 0 <= kp < KSTEPS:
                    prow = pl.ds(kp * SUBLANES, SUBLANES)
                    xr, xi = xs[jr, prow, :], xs[ji, prow, :]
                else:
                    xr, xi = xin[jr], xin[ji]
                lr, li = ls[jr, rows, :], ls[ji, rows, :]
                ar += lr * xr + li * xi
                ai += li * xr - lr * xi
            dlb_ref[jr] += ar
            dlb_ref[ji] += ai
            return 0

        lax.fori_loop(0, nj, step, 0)

        @pl.when(c == nc - 1)
        def _():
            for b in range(2 * nj):
                dlb_ref[b] = jnp.broadcast_to(jnp.sum(dlb_ref[b], axis=0, keepdims=True), (SUBLANES, SCAN_LANES))
            for g in range(w // SSM_GROUP):
                b, gl = divmod(g, cb // SSM_GROUP)
                rows = slice(gl * SSM_GROUP, (gl + 1) * SSM_GROUP)
                for part in range(2):
                    cols = slice(part * SSM_BLOCK + gl * SSM_STATE, part * SSM_BLOCK + (gl + 1) * SSM_STATE)
                    dbb_ref[part, g * SSM_GROUP:(g + 1) * SSM_GROUP, :] = dwb_ref[b, rows, cols]
                    dcc_ref[part, g * SSM_GROUP:(g + 1) * SSM_GROUP, :] = dwc_ref[b, rows, cols]

    st = (2 * nj, SUBLANES, SCAN_LANES)
    wshape = (nb, cb, 2 * SSM_BLOCK)
    return pl.pallas_call(
        body, name=name, grid=(nc,),
        in_specs=[pl.BlockSpec((CHUNK, w), lambda c: (cidx(c), 0)), pl.BlockSpec((CHUNK, w), lambda c: (cidx(c), 0)),
                  pl.BlockSpec((1,) + st, lambda c: (cidx(c), 0, 0, 0)), _full(wshape), _full(wshape), _ANY],
        out_specs=[pl.BlockSpec((CHUNK, w), lambda c: (cidx(c), 0)), _full((2, w, SSM_STATE)),
                   _full((2, w, SSM_STATE)), _full(st)],
        out_shape=[jax.ShapeDtypeStruct((lp, w), F32), jax.ShapeDtypeStruct((2, w, SSM_STATE), F32),
                   jax.ShapeDtypeStruct((2, w, SSM_STATE), F32), jax.ShapeDtypeStruct(st, F32)],
        scratch_shapes=[pltpu.VMEM((ntab,) + st, F32), pltpu.VMEM(wshape, F32), pltpu.VMEM(wshape, F32),
                        pltpu.VMEM((2 * nj, CHUNK, SCAN_LANES), F32), pltpu.VMEM((2 * nj, CHUNK, SCAN_LANES), F32),
                        pltpu.VMEM(st, F32), pltpu.VMEM(st, F32), pltpu.VMEM(st, F32), pltpu.VMEM(st, F32)],
        compiler_params=_params(("arbitrary",)),
    )(u, dy, ckpt, wb, wct, tab)


def _embed_blocks(t_re, t_im):
    g, p, n = t_re.shape
    gb = SSM_BLOCK // n
    eye = jnp.eye(gb, dtype=t_re.dtype)
    parts = [jnp.einsum('bgpn,gh->bgphn', t.reshape(g // gb, gb, p, n), eye).reshape(g // gb, gb * p, gb * n)
             for t in (t_re, t_im)]
    return jnp.concatenate(parts, axis=2)


def _scan_tables(pw_re, pw_im, hi_re, hi_im, rev):
    s = pw_re.shape[1] * pw_re.shape[2]
    nj = s // SCAN_LANES
    sub = np.arange(SUBLANES)
    live = np.ones((4 + KSTEPS, 1, SUBLANES, 1), bool)
    for row, dist in ((1, 1), (2, 2), (3, 4)):
        live[row, 0, :, 0] = (sub < SUBLANES - dist) if rev else (sub >= dist)

    def lay(pw, hi):
        rows = jnp.concatenate([pw[:1], pw[KSTEPS - 1:], hi, pw], axis=0).reshape(4 + KSTEPS, nj, 1, SCAN_LANES)
        return jnp.where(live, jnp.broadcast_to(rows, (4 + KSTEPS, nj, SUBLANES, SCAN_LANES)), 0.0)

    return jnp.concatenate([lay(pw_re, hi_re), lay(pw_im, hi_im)], axis=1)


def _adamw(w, g, m, v, tm):
    r, c = w.shape
    c1 = 1.0 - ADAM_B1 ** ADAM_STEP
    c2 = 1.0 - ADAM_B2 ** ADAM_STEP

    def body(w_ref, g_ref, m_ref, v_ref, d_ref, nm_ref, nv_ref):
        gg = g_ref[...]
        nm = ADAM_B1 * m_ref[...] + (1.0 - ADAM_B1) * gg
        nv = ADAM_B2 * v_ref[...] + (1.0 - ADAM_B2) * (gg * gg)
        nm_ref[...] = nm
        nv_ref[...] = nv
        d_ref[...] = -ADAM_LR * ((nm / c1) / (jnp.sqrt(nv / c2) + ADAM_EPS) + ADAM_WD * w_ref[...])

    spec = _row(tm, c)
    return pl.pallas_call(
        body, name="adamw", grid=(r // tm,), in_specs=[spec] * 4, out_specs=[spec] * 3,
        out_shape=[jax.ShapeDtypeStruct((r, c), F32)] * 3, compiler_params=_params(("parallel",)),
    )(w, g, m, v)


def _pair_sum(g42, got, core, out_dtype, tm, name):
    _, _, r, c = g42.shape

    def body(core_ref, a_ref, b_ref, o_ref):
        o_ref[...] = (a_ref[...] + b_ref[...]).astype(out_dtype)

    grid_spec = pltpu.PrefetchScalarGridSpec(
        num_scalar_prefetch=1, grid=(4, r // tm),
        in_specs=[pl.BlockSpec((1, None, tm, c), lambda s, i, core_ref: (s, core_ref[0], i, 0)),
                  pl.BlockSpec((1, tm, c), lambda s, i, core_ref: (s, i, 0))],
        out_specs=pl.BlockSpec((1, tm, c), lambda s, i, core_ref: (s, i, 0)))
    return pl.pallas_call(
        body, name=name, grid_spec=grid_spec, out_shape=jax.ShapeDtypeStruct((4, r, c), out_dtype),
        compiler_params=_params(("parallel", "parallel")),
    )(core, g42, got)


def _sum4(a, core, tm, name):
    _, r, c = a.shape

    def body(core_ref, a_ref, o_ref):
        o_ref[...] = ((a_ref[0].astype(F32) + a_ref[1].astype(F32)) + a_ref[2].astype(F32)) + a_ref[3].astype(F32)

    grid_spec = pltpu.PrefetchScalarGridSpec(
        num_scalar_prefetch=1, grid=(r // tm,),
        in_specs=[pl.BlockSpec((4, tm, c), lambda i, core_ref: (0, i, 0))],
        out_specs=pl.BlockSpec((None, tm, c), lambda i, core_ref: (core_ref[0], i, 0)))
    return pl.pallas_call(
        body, name=name, grid_spec=grid_spec, out_shape=jax.ShapeDtypeStruct((2, r, c), F32),
        compiler_params=_params(("parallel",)),
    )(core, a)


_ANY = pl.BlockSpec(memory_space=pl.ANY)


def _all_gather8(blocks, name):
    n = len(blocks)

    def body(*refs):
        xs, outs = refs[:n], refs[n:2 * n]
        send_sems, recv_sems, local_sems = refs[2 * n:]
        x, y, c = lax.axis_index("x"), lax.axis_index("y"), lax.axis_index("c")
        me, sibling = (x, y, c), (x, y, 1 - c)
        chips = [(1 - x, y), (x, 1 - y), (1 - x, 1 - y)]

        def slot(t, px, py, pc):
            return outs[t].at[4 * px + 2 * py + pc]

        def copy(t, k, blk, to, src=None):
            return pltpu.make_async_remote_copy(
                src_ref=slot(t, *blk) if src is None else src, dst_ref=slot(t, *blk),
                send_sem=send_sems.at[t, k], recv_sem=recv_sems.at[t, k], device_id=to, device_id_type=MESH_ID)

        mine = [pltpu.make_async_copy(xs[t], slot(t, *me), local_sems.at[t]) for t in range(n)]
        for cp in mine:
            cp.start()
        first = [[copy(t, 0, me, sibling, src=xs[t])]
                 + [copy(t, 1 + j, me, (*chip, c), src=xs[t]) for j, chip in enumerate(chips)] for t in range(n)]
        for t in range(n):
            for cp in first[t]:
                cp.start()
        passed = [[copy(t, 4 + j, (*chip, c), sibling) for j, chip in enumerate(chips)] for t in range(n)]
        for j, chip in enumerate(chips):
            for t in range(n):
                copy(t, 1 + j, (*chip, c), me).wait_recv()
                passed[t][j].start()
        for t in range(n):
            copy(t, 0, sibling, me).wait_recv()
        for j, chip in enumerate(chips):
            for t in range(n):
                copy(t, 4 + j, (*chip, 1 - c), me).wait_recv()
        for t in range(n):
            for cp in first[t] + passed[t]:
                cp.wait_send()
        for cp in mine:
            cp.wait()

    return pl.pallas_call(
        body, name=name, out_shape=[jax.ShapeDtypeStruct((8,) + b.shape, b.dtype) for b in blocks],
        in_specs=[_ANY] * n, out_specs=[_ANY] * n,
        scratch_shapes=[pltpu.SemaphoreType.DMA((n, 7)), pltpu.SemaphoreType.DMA((n, 7)),
                        pltpu.SemaphoreType.DMA((n,))],
    )(*blocks)


def _pair_exchange(gs, name):
    n = len(gs)

    def body(*refs):
        g_refs, outs = refs[:n], refs[n:2 * n]
        send_sems, recv_sems = refs[2 * n:]
        x, y, c = lax.axis_index("x"), lax.axis_index("y"), lax.axis_index("c")
        cps = [pltpu.make_async_remote_copy(
            src_ref=g_refs[t].at[:, 1 - c], dst_ref=outs[t], send_sem=send_sems.at[t], recv_sem=recv_sems.at[t],
            device_id=(x, y, 1 - c), device_id_type=MESH_ID) for t in range(n)]
        for cp in cps:
            cp.start()
        for cp in cps:
            cp.wait()

    return pl.pallas_call(
        body, name=name,
        out_shape=[jax.ShapeDtypeStruct((g.shape[0],) + g.shape[2:], g.dtype) for g in gs],
        in_specs=[_ANY] * n, out_specs=[_ANY] * n,
        scratch_shapes=[pltpu.SemaphoreType.DMA((n,)), pltpu.SemaphoreType.DMA((n,))],
    )(*gs)


def _chip_scatter(ps, name):
    n = len(ps)

    def body(*refs):
        p_refs, outs = refs[:n], refs[n:2 * n]
        send_sems, recv_sems, local_sems = refs[2 * n:]
        x, y, c = lax.axis_index("x"), lax.axis_index("y"), lax.axis_index("c")
        mine = 2 * x + y
        chips = [(1 - x, y), (x, 1 - y), (1 - x, 1 - y)]
        own = [pltpu.make_async_copy(p_refs[t].at[mine], outs[t].at[mine], local_sems.at[t]) for t in range(n)]
        for cp in own:
            cp.start()

        def copy(t, k, src_slab, dst_slab, chip):
            return pltpu.make_async_remote_copy(
                src_ref=p_refs[t].at[src_slab], dst_ref=outs[t].at[dst_slab], send_sem=send_sems.at[t, k],
                recv_sem=recv_sems.at[t, k], device_id=(*chip, c), device_id_type=MESH_ID)

        sends = [[copy(t, k, 2 * cx + cy, mine, (cx, cy)) for k, (cx, cy) in enumerate(chips)] for t in range(n)]
        for k in range(3):
            for t in range(n):
                sends[t][k].start()
        for k, (cx, cy) in enumerate(chips):
            for t in range(n):
                copy(t, k, mine, 2 * cx + cy, (cx, cy)).wait_recv()
        for t in range(n):
            for cp in sends[t]:
                cp.wait_send()
        for cp in own:
            cp.wait()

    return pl.pallas_call(
        body, name=name, out_shape=[jax.ShapeDtypeStruct(p.shape, p.dtype) for p in ps],
        in_specs=[_ANY] * n, out_specs=[_ANY] * n,
        scratch_shapes=[pltpu.SemaphoreType.DMA((n, 3)), pltpu.SemaphoreType.DMA((n, 3)),
                        pltpu.SemaphoreType.DMA((n,))],
    )(*ps)


def _pair_gather(rs, name):
    n = len(rs)

    def body(*refs):
        ins, outs = refs[:n], refs[n:2 * n]
        send_sems, recv_sems = refs[2 * n:]
        x, y, c = lax.axis_index("x"), lax.axis_index("y"), lax.axis_index("c")

        def copy(t, slab):
            return pltpu.make_async_remote_copy(
                src_ref=ins[t].at[slab], dst_ref=outs[t].at[slab], send_sem=send_sems.at[t],
                recv_sem=recv_sems.at[t], device_id=(x, y, 1 - c), device_id_type=MESH_ID)

        sends = [copy(t, c) for t in range(n)]
        for cp in sends:
            cp.start()
        for t in range(n):
            copy(t, 1 - c).wait_recv()
        for cp in sends:
            cp.wait_send()

    return pl.pallas_call(
        body, name=name, out_shape=[jax.ShapeDtypeStruct(r.shape, r.dtype) for r in rs],
        in_specs=[_ANY] * n, out_specs=[_ANY] * n, input_output_aliases={t: t for t in range(n)},
        scratch_shapes=[pltpu.SemaphoreType.DMA((n,)), pltpu.SemaphoreType.DMA((n,))],
    )(*rs)


PACK_COLS = 1024
BIG = (("meta_tokens", 1), ("w_in", 1), ("w_glu", 0), ("w_ssm_proj", 1), ("w_attn_proj", 0), ("w_out", 0),
       ("w_mlp_in", 1), ("w_mlp_out", 0))
SMALL = ("norm_mix_g", "ssm_a_re", "ssm_a_im", "ssm_log_dt", "ssm_b_re", "ssm_b_im", "ssm_c_re", "ssm_c_im",
         "ssm_d", "b_glu", "q_norm_g", "k_norm_g", "norm_mlp_g", "norm_final_g")


def _pad_rows(flat, mult_rows):
    n = flat.shape[0]
    unit = PACK_COLS * mult_rows
    total = -(-n // unit) * unit
    return jnp.pad(flat, (0, total - n)).reshape(total // PACK_COLS, PACK_COLS)


def _half(t, c):
    return lax.dynamic_slice_in_dim(t, c * (t.shape[0] // 2), t.shape[0] // 2, 0)


def _gather_weights(shards, c):
    names = [name for name, _ in BIG]
    blocks = [_half(shards[name], c) if name == "meta_tokens" else _half(shards[name], c).astype(BF16)
              for name in names]
    got = _all_gather8(blocks, "weight_all_gather")
    return {name: g.reshape((4, 2 * g.shape[1]) + g.shape[2:]) for name, g in zip(names, got)}


def _reduce_gradients(big4, small_flat, c):
    names = [name for name, _ in BIG]
    n_small = small_flat.shape[0]
    unit = 8 * SUBLANES * PACK_COLS
    k = -(-n_small // unit) * unit
    small42 = jnp.pad(small_flat, (0, k - n_small)).reshape(4, 2, k // (8 * PACK_COLS), PACK_COLS)
    g42 = [big4[name].reshape(4, 2, big4[name].shape[1] // 2, big4[name].shape[2]) for name in names] + [small42]
    labels = names + ["small"]
    wire = [F32 if name == "meta_tokens" else BF16 for name in names] + [F32]
    tiles = [_pick_tile(g.shape[2], 256, SUBLANES if dt == F32 else 2 * SUBLANES) for g, dt in zip(g42, wire)]
    core = c.astype(jnp.int32).reshape(1)
    got = _pair_exchange(g42, "grad_pair_exchange")
    pair = [_pair_sum(g, o, core, dt, tm, "pair_sum_" + lb) for g, o, dt, tm, lb in zip(g42, got, wire, tiles, labels)]
    by_src = _chip_scatter(pair, "grad_chip_scatter")
    red = [_sum4(b, core, tm, "chip_sum_" + lb) for b, tm, lb in zip(by_src, tiles, labels)]
    both = _pair_gather(red[:-1], "grad_pair_gather")
    out = {name: b.reshape(2 * b.shape[1], b.shape[2]) for name, b in zip(names, both)}
    small_piece = lax.dynamic_index_in_dim(red[-1], c, 0, keepdims=False)
    small = _all_gather8([small_piece], "small_grad_all_gather")[0].reshape(-1)[:n_small]
    return out, small


def _to_chunk_order(a):
    lp = a.shape[0]
    rest = a.shape[1:]
    a = a.reshape((lp // CHUNK, SUBLANES, KSTEPS) + rest)
    return a.swapaxes(1, 2).reshape((lp,) + rest)


def _from_chunk_order(a):
    lp = a.shape[0]
    rest = a.shape[1:]
    a = a.reshape((lp // CHUNK, KSTEPS, SUBLANES) + rest)
    return a.swapaxes(1, 2).reshape((lp,) + rest)


def _rope_tables(l_total, lp):
    n_real = l_total - N_META
    pos = np.arange(n_real)
    row_id = (pos // GRID_W).astype(np.float32)
    col_id = (pos % GRID_W).astype(np.float32)
    ppa = HEAD_DIM // 4
    inv_freq = (ROPE_THETA ** (-np.arange(ppa, dtype=np.float64) / ppa)).astype(np.float32)
    ang = np.concatenate([row_id[:, None] * inv_freq, col_id[:, None] * inv_freq], axis=-1)
    ang = np.concatenate([np.zeros((N_META, HEAD_DIM // 2), np.float32), ang,
                          np.zeros((lp - l_total, HEAD_DIM // 2), np.float32)], axis=0).astype(np.float64)
    cos = np.repeat(np.cos(ang), 2, axis=1)
    sin = np.repeat(np.sin(ang), 2, axis=1) * np.tile(np.asarray([-1.0, 1.0]), HEAD_DIM // 2)
    reps = (1, LANES // HEAD_DIM)
    return np.tile(cos, reps).astype(np.float32), np.tile(sin, reps).astype(np.float32)


def kernel(x, meta_tokens, norm_mix_g, w_in, ssm_a_re, ssm_a_im, ssm_log_dt, ssm_b_re, ssm_b_im, ssm_c_re, ssm_c_im, ssm_d, w_glu, b_glu, q_norm_g, k_norm_g, w_ssm_proj, w_attn_proj, w_out, norm_mlp_g, w_mlp_in, w_mlp_out, norm_final_g, loss_target, m_meta_tokens, m_norm_mix_g, m_w_in, m_ssm_a_re, m_ssm_a_im, m_ssm_log_dt, m_ssm_b_re, m_ssm_b_im, m_ssm_c_re, m_ssm_c_im, m_ssm_d, m_w_glu, m_b_glu, m_q_norm_g, m_k_norm_g, m_w_ssm_proj, m_w_attn_proj, m_w_out, m_norm_mlp_g, m_w_mlp_in, m_w_mlp_out, m_norm_final_g, v_meta_tokens, v_norm_mix_g, v_w_in, v_ssm_a_re, v_ssm_a_im, v_ssm_log_dt, v_ssm_b_re, v_ssm_b_im, v_ssm_c_re, v_ssm_c_im, v_ssm_d, v_w_glu, v_b_glu, v_q_norm_g, v_k_norm_g, v_w_ssm_proj, v_w_attn_proj, v_w_out, v_norm_mlp_g, v_w_mlp_in, v_w_mlp_out, v_norm_final_g):
    args = dict(locals())
    names = list(dict.fromkeys([n for n, _ in BIG] + list(SMALL)))
    order = ['meta_tokens', 'norm_mix_g', 'w_in', 'ssm_a_re', 'ssm_a_im', 'ssm_log_dt', 'ssm_b_re', 'ssm_b_im',
             'ssm_c_re', 'ssm_c_im', 'ssm_d', 'w_glu', 'b_glu', 'q_norm_g', 'k_norm_g', 'w_ssm_proj', 'w_attn_proj',
             'w_out', 'norm_mlp_g', 'w_mlp_in', 'w_mlp_out', 'norm_final_g']
    assert sorted(names) == sorted(order)
    c_idx = lax.axis_index("c")

    seq, d = x.shape[1], x.shape[2]
    l_total = seq + N_META
    lp = -(-l_total // SEQ_ALIGN) * SEQ_ALIGN
    hd = d // 2
    n_groups = hd // SSM_GROUP
    n_state = n_groups * SSM_STATE
    nj = n_state // SCAN_LANES
    kvh = d // HEAD_DIM // GQA_REP

    shard2d = {}
    for name, _ in BIG:
        t = args[name]
        shard2d[name] = t.reshape(t.shape[-2], t.shape[-1])
    full = _gather_weights(shard2d, c_idx)
    meta_full = jnp.transpose(full["meta_tokens"], (1, 0, 2)).reshape(N_META, d)
    w_in4 = full["w_in"]
    w_mlp_in4 = full["w_mlp_in"]
    w_ssm_proj4 = full["w_ssm_proj"]
    w_glu_f = full["w_glu"].reshape(hd, hd)
    w_attn_proj_f = full["w_attn_proj"].reshape(d, d)
    w_out_f = full["w_out"].reshape(d, d)
    w_mlp_out_f = full["w_mlp_out"].reshape(4 * d, d)

    xin = jnp.concatenate([meta_full, x[0], jnp.zeros((lp - l_total, d), F32)], axis=0)
    xin = _to_chunk_order(xin)
    tgt = _to_chunk_order(jnp.pad(loss_target[0], ((N_META, lp - l_total), (0, 0))))
    pos = np.arange(lp)
    rowmask = jnp.asarray(_to_chunk_order(((pos >= N_META) & (pos < l_total)).astype(np.float32)[:, None]))
    kbias = jnp.asarray(_to_chunk_order(np.where(pos < l_total, 0.0, MASK_VALUE).astype(np.float32)[:, None])
                        .reshape(1, lp))
    cos_t, sin_t = (jnp.asarray(_to_chunk_order(t)) for t in _rope_tables(l_total, lp))
    mean_m, sel = _head_tables(d)

    tm = _pick_tile(lp, 320)
    tm_big = _pick_tile(lp, 640)
    tq = _pick_tile(lp, ATTN_Q_TILE, LANES)
    tk = _pick_tile(lp, ATTN_K_TILE, MXU_DIM)
    assert lp - tk <= (l_total // CHUNK) * CHUNK
    g_mix = norm_mix_g.reshape(1, d)
    g_mlp = norm_mlp_g.reshape(1, d)
    g_fin = norm_final_g.reshape(1, d)
    qg = jnp.tile(q_norm_g.reshape(1, HEAD_DIM), (1, LANES // HEAD_DIM))
    kg = jnp.tile(k_norm_g.reshape(1, HEAD_DIM), (1, LANES // HEAD_DIM))
    dskip = ssm_d.reshape(1, hd)
    bglu = b_glu.reshape(1, hd)

    a_re, a_im = ssm_a_re[0], ssm_a_im[0]
    log_dt = ssm_log_dt[0][..., None]
    bt_re = jnp.swapaxes(ssm_b_re[0], 2, 3)
    bt_im = jnp.swapaxes(ssm_b_im[0], 2, 3)
    bb_re, bb_im, pw_re, pw_im, hi_re, hi_im = _ssm_discretize(a_re, a_im, log_dt, bt_re, bt_im)
    wb = [_embed_blocks(bb_re[i], bb_im[i]).astype(BF16) for i in range(2)]
    wct = [_embed_blocks(ssm_c_re[0, i], -ssm_c_im[0, i]).astype(BF16) for i in range(2)]
    tabs = [_scan_tables(pw_re[i], pw_im[i], hi_re[i], hi_im[i], rev=(i == 1)) for i in range(2)]
    tabs_adj = [_scan_tables(pw_re[i], pw_im[i], hi_re[i], hi_im[i], rev=(i == 0)) for i in range(2)]

    u, qkv, gates = _in_proj(xin, g_mix, w_in4, tm)
    y0, ck0 = _ssm_fwd(u, wb[0], wct[0], tabs[0], False, "ssm_fwd_0")
    y1, ck1 = _ssm_fwd(u, wb[1], wct[1], tabs[1], True, "ssm_fwd_1")
    yssm = _glu_fwd(u, y0, y1, dskip, w_glu_f, bglu, tm_big)
    q, k, v = _qk_prep(qkv, cos_t, sin_t, qg, kg, mean_m, tm)
    o, lse = _attn_fwd(q, k, v, kbias, tq, tk)
    h1, merged = _merge_fwd(yssm, o, gates, xin, w_ssm_proj4, w_attn_proj_f, w_out_f, tm)
    r = _mlp_in(h1, g_mlp, w_mlp_in4, tm)
    h3 = _mlp_out(h1, r, w_mlp_out_f, tm)
    loss_tile, dh3, d_gfin = _final_loss(h3, g_fin, tgt, rowmask, tm_big)
    loss = lax.psum(loss_tile[0, 0], ("x", "y", "c"))

    dz, dh3b = _mlp_bwd_a(dh3, r, w_mlp_out_f, tm)
    dh1, d_gmlp = _mlp_bwd_b(dz, dh3, h1, g_mlp, w_mlp_in4, tm)
    dgates, dms, dma, dyssm, do, delta, dh1b = _merge_bwd(dh1, yssm, o, gates, w_ssm_proj4, w_attn_proj_f, w_out_f,
                                                          sel, tm)
    dyv, d_wglu, d_bglu, d_dskip = _glu_bwd(dyssm, u, y0, y1, dskip, w_glu_f, bglu, tm_big)
    du0, dbb0, dcc0, dlb0 = _ssm_bwd(u, dyv, ck0, wb[0], wct[0], _both(tabs[0], tabs_adj[0]), False, "ssm_bwd_0")
    du1, dbb1, dcc1, dlb1 = _ssm_bwd(u, dyv, ck1, wb[1], wct[1], _both(tabs[1], tabs_adj[1]), True, "ssm_bwd_1")
    dq, dk, dv = _attn_bwd(q, k, v, kbias, do, lse, delta, tq, tk)
    dqkv, d_qg, d_kg = _qk_bwd(qkv, dq, dk, dv, cos_t, sin_t, qg, kg, mean_m, tm)
    dxin, d_gmix, dproj = _in_proj_bwd(dyv, du0, du1, dskip, dqkv, dgates, dh1, xin, g_mix, w_in4, tm)

    tn = min(d, 1024)
    tm_w = _pick_tile(lp, 3 * MXU_DIM, MXU_DIM)
    grads4 = {
        "w_in": _wgrad(xin, dproj, 4, tm_w, tn, "wgrad_in", gain=g_mix),
        "w_mlp_in": _wgrad(h1, dz, 4, tm_w, tn, "wgrad_mlp_in", gain=g_mlp),
        "w_mlp_out": _wgrad(r, dh3b, 1, tm_w, min(d, 256), "wgrad_mlp_out", square=True).reshape(4, d, d),
        "w_out": _wgrad(merged, dh1b, 1, tm_w, tn, "wgrad_out").reshape(4, d // 4, d),
        "w_attn_proj": _wgrad(o, dma, 1, tm_w, tn, "wgrad_attn_proj").reshape(4, d // 4, d),
        "w_ssm_proj": _wgrad(yssm, dms, 4, tm_w, d // 4, "wgrad_ssm_proj"),
        "w_glu": d_wglu.reshape(4, hd // 4, hd),
    }
    dx_nat = _from_chunk_order(dxin)
    grads4["meta_tokens"] = jnp.swapaxes(dx_nat[:N_META].reshape(N_META, 4, d // 4), 0, 1)
    grad_x = dx_nat[N_META:l_total][None]

    dlb = jnp.stack([dlb0, dlb1])[:, :, 0, :]
    dlb_re = dlb[:, :nj].reshape(2, n_groups, SSM_STATE)
    dlb_im = dlb[:, nj:].reshape(2, n_groups, SSM_STATE)
    gpn = (2, 2, n_groups, SSM_GROUP, SSM_STATE)
    dbb = jnp.stack([dbb0, dbb1]).reshape(gpn)
    dcc = jnp.stack([dcc0, dcc1]).reshape(gpn)
    d_are, d_aim, d_logdt, d_btre, d_btim = _ssm_param_bwd(a_re, a_im, log_dt, bt_re, bt_im, dlb_re, dlb_im,
                                                           dbb[:, 0], dbb[:, 1])
    small_grads = {
        "norm_mix_g": d_gmix, "ssm_a_re": d_are, "ssm_a_im": d_aim, "ssm_log_dt": d_logdt,
        "ssm_b_re": jnp.swapaxes(d_btre, 2, 3), "ssm_b_im": jnp.swapaxes(d_btim, 2, 3),
        "ssm_c_re": dcc[:, 0], "ssm_c_im": -dcc[:, 1],
        "ssm_d": d_dskip, "b_glu": d_bglu, "q_norm_g": d_qg[:, :HEAD_DIM], "k_norm_g": d_kg[:, :HEAD_DIM],
        "norm_mlp_g": d_gmlp, "norm_final_g": d_gfin,
    }
    small_flat = jnp.concatenate([small_grads[n].reshape(-1) for n in SMALL])

    red_big, red_small = _reduce_gradients(grads4, small_flat, c_idx)
    grad, delta_w, new_m, new_v = {}, {}, {}, {}
    for name, _ in BIG:
        w2 = shard2d[name]
        shp = args[name].shape
        g2 = red_big[name]
        t = _pick_tile(w2.shape[0], 256, 8)
        dl, nm, nv = _adamw(w2, g2, args["m_" + name].reshape(w2.shape), args["v_" + name].reshape(w2.shape), t)
        grad[name], delta_w[name], new_m[name], new_v[name] = (a.reshape(shp) for a in (g2, dl, nm, nv))

    def pack_small(prefix):
        flat = jnp.concatenate([args[prefix + n].reshape(-1) for n in SMALL])
        return _pad_rows(flat, SUBLANES)

    n_small = red_small.shape[0]
    gs = _pad_rows(red_small, SUBLANES)
    dl, nm, nv = _adamw(pack_small(""), gs, pack_small("m_"), pack_small("v_"), _pick_tile(gs.shape[0], 256, 8))
    off = 0
    for name in SMALL:
        shp = args[name].shape
        k = int(np.prod(shp))
        for dst, src in ((grad, gs), (delta_w, dl), (new_m, nm), (new_v, nv)):
            dst[name] = src.reshape(-1)[off:off + k].reshape(shp)
        off += k
    assert off == n_small

    return (loss, grad_x, *[grad[n] for n in order], *[delta_w[n] for n in order],
            *[new_m[n] for n in order], *[new_v[n] for n in order])


def _both(tab, tab_adj):
    return jnp.concatenate([tab, tab_adj], axis=0)
```

```python
import functools
import math

import numpy as np
import jax
import jax.numpy as jnp
from jax import lax
from jax.experimental import pallas as pl
from jax.experimental.pallas import tpu as pltpu

F32 = jnp.float32
BF16 = jnp.bfloat16

N_META = 16
GRID_W = 64
HEAD_DIM = 64
GQA_REP = 4
SSM_GROUP = 16
SSM_STATE = 64
ROPE_THETA = 10000.0
NORM_EPS = 1e-6
EIG_RE_MAX = -1e-4
ADAM_LR, ADAM_B1, ADAM_B2, ADAM_EPS, ADAM_WD, ADAM_STEP = 0.001, 0.9, 0.999, 1e-08, 0.01, 10

SUBLANES = 8
LANES = 128
CHUNK = 256
KSTEPS = CHUNK // SUBLANES
SCAN_LANES = 512
MXU_DIM = 256
SSM_BLOCK = MXU_DIM
SEQ_ALIGN = MXU_DIM
ATTN_Q_TILE = 384
ATTN_K_TILE = 11 * MXU_DIM
VMEM_LIMIT = 56 << 20
MASK_VALUE = -1e30
MESH_ID = pl.DeviceIdType.MESH


def _dot(a, b):
    return jnp.dot(a, b, preferred_element_type=F32)


def _dot_nt(a, b):
    return lax.dot_general(a, b, (((1,), (1,)), ((), ())), preferred_element_type=F32)


def _dot_tn(a, b):
    return lax.dot_general(a, b, (((0,), (0,)), ((), ())), preferred_element_type=F32)


def _row(tm, width):
    return pl.BlockSpec((tm, width), lambda i: (i, 0))


def _full(shape):
    nd = len(shape)
    return pl.BlockSpec(shape, lambda i: (0,) * nd)


def _params(sem):
    return pltpu.CompilerParams(dimension_semantics=sem, vmem_limit_bytes=VMEM_LIMIT)


def _pick_tile(n, cap, mult=16):
    best = None
    for t in range(mult, min(n, cap) + 1, mult):
        if n % t == 0:
            best = t
    assert best is not None, (n, cap)
    return best


def _rstd(x):
    return lax.rsqrt(jnp.mean(x * x, axis=-1, keepdims=True) + NORM_EPS)


def _rms(x, g):
    return x * _rstd(x) * g


def _rms_bwd(dy, x, g):
    r = _rstd(x)
    xh = x * r
    gdy = dy * g
    dx = r * (gdy - xh * jnp.mean(gdy * xh, axis=-1, keepdims=True))
    return dx, dy * xh


def _split_dot(x, m):
    hi = x.astype(BF16)
    lo = (x - hi.astype(F32)).astype(BF16)
    return _dot(hi, m) + _dot(lo, m)


def _sigmoid(x):
    return 1.0 / (1.0 + jnp.exp(-x))


def _acc_rows(ref, val, first):
    s = jnp.sum(val, axis=0, keepdims=True)

    @pl.when(first)
    def _():
        ref[...] = s

    @pl.when(jnp.logical_not(first))
    def _():
        ref[...] += s


def _in_proj(xin, g, w4, tm):
    lp, d = xin.shape
    hd = d // 2

    def body(x_ref, g_ref, w_ref, u_ref, qkv_ref, gt_ref):
        h = _rms(x_ref[...], g_ref[...]).astype(BF16)
        p0 = _dot(h, w_ref[0])
        u_ref[...] = p0[:, :hd]
        qkv_ref[:, :hd] = p0[:, hd:]
        qkv_ref[:, hd:] = _dot(h, w_ref[1])
        gt_ref[:, :d] = _dot(h, w_ref[2])
        gt_ref[:, d:] = _dot(h, w_ref[3])

    return pl.pallas_call(
        body, name="in_proj", grid=(lp // tm,),
        in_specs=[_row(tm, d), _full((1, d)), _full((4, d, d))],
        out_specs=[_row(tm, hd), _row(tm, 3 * hd), _row(tm, 2 * d)],
        out_shape=[jax.ShapeDtypeStruct((lp, hd), F32), jax.ShapeDtypeStruct((lp, 3 * hd), F32),
                   jax.ShapeDtypeStruct((lp, 2 * d), F32)],
        compiler_params=_params(("parallel",)),
    )(xin, g, w4)


def _gelu(y):
    return 0.5 * y * (1.0 + lax.erf(y * (1.0 / math.sqrt(2.0))))


def _gelu_grad(y):
    return 0.5 * (1.0 + lax.erf(y * (1.0 / math.sqrt(2.0)))) + y * jnp.exp(-0.5 * y * y) * (1.0 / math.sqrt(2.0 * math.pi))


def _glu_fwd(u, y0, y1, dskip, w_glu, b_glu, tm):
    lp, w = u.shape

    def body(u_ref, y0_ref, y1_ref, d_ref, w_ref, b_ref, o_ref):
        y = u_ref[...] * d_ref[...] + y0_ref[...] + y1_ref[...]
        z = _gelu(y)
        t = _dot(z.astype(BF16), w_ref[...]) + b_ref[...]
        o_ref[...] = (z * _sigmoid(t)).astype(BF16)

    return pl.pallas_call(
        body, name="glu_fwd", grid=(lp // tm,),
        in_specs=[_row(tm, w), _row(tm, w), _row(tm, w), _full((1, w)), _full((w, w)), _full((1, w))],
        out_specs=_row(tm, w), out_shape=jax.ShapeDtypeStruct((lp, w), BF16),
        compiler_params=_params(("parallel",)),
    )(u, y0, y1, dskip, w_glu, b_glu)


def _glu_bwd(dyssm, u, y0, y1, dskip, w_glu, b_glu, tm):
    lp, w = u.shape

    def body(g_ref, u_ref, y0_ref, y1_ref, d_ref, w_ref, b_ref, dy_ref, dw_ref, db_ref, dd_ref):
        first = pl.program_id(0) == 0
        uu = u_ref[...]
        y = uu * d_ref[...] + y0_ref[...] + y1_ref[...]
        z = _gelu(y)
        zb = z.astype(BF16)
        sg = _sigmoid(_dot(zb, w_ref[...]) + b_ref[...])
        g = g_ref[...]
        dt = g * z * sg * (1.0 - sg)
        dtb = dt.astype(BF16)
        dz = g * sg + _dot_nt(dtb, w_ref[...])
        dy = dz * _gelu_grad(y)
        dy_ref[...] = dy
        dw = _dot_tn(zb, dtb)

        @pl.when(first)
        def _():
            dw_ref[...] = dw

        @pl.when(jnp.logical_not(first))
        def _():
            dw_ref[...] += dw

        _acc_rows(db_ref, dt, first)
        _acc_rows(dd_ref, dy * uu, first)

    return pl.pallas_call(
        body, name="glu_bwd", grid=(lp // tm,),
        in_specs=[_row(tm, w), _row(tm, w), _row(tm, w), _row(tm, w), _full((1, w)), _full((w, w)), _full((1, w))],
        out_specs=[_row(tm, w), _full((w, w)), _full((1, w)), _full((1, w))],
        out_shape=[jax.ShapeDtypeStruct((lp, w), F32), jax.ShapeDtypeStruct((w, w), F32),
                   jax.ShapeDtypeStruct((1, w), F32), jax.ShapeDtypeStruct((1, w), F32)],
        compiler_params=_params(("arbitrary",)),
    )(dyssm, u, y0, y1, dskip, w_glu, b_glu)


def _merge_fwd(yssm, o, gates, xin, wsp4, wap, wo, tm):
    lp, d = xin.shape
    w = yssm.shape[1]
    ns = d // 4

    def body(y_ref, o_ref, g_ref, x_ref, wsp_ref, wap_ref, wo_ref, h_ref, m_ref):
        yb = y_ref[...]
        ms = jnp.concatenate([_dot(yb, wsp_ref[s]) for s in range(4)], axis=1)
        ma = _dot(o_ref[...], wap_ref[...])
        merged = (_sigmoid(g_ref[:, :d]) * ms + _sigmoid(g_ref[:, d:]) * ma).astype(BF16)
        m_ref[...] = merged
        h_ref[...] = x_ref[...] + _dot(merged, wo_ref[...])

    return pl.pallas_call(
        body, name="merge_fwd", grid=(lp // tm,),
        in_specs=[_row(tm, w), _row(tm, d), _row(tm, 2 * d), _row(tm, d),
                  _full((4, w, ns)), _full((d, d)), _full((d, d))],
        out_specs=[_row(tm, d), _row(tm, d)],
        out_shape=[jax.ShapeDtypeStruct((lp, d), F32), jax.ShapeDtypeStruct((lp, d), BF16)],
        compiler_params=_params(("parallel",)),
    )(yssm, o, gates, xin, wsp4, wap, wo)


def _merge_bwd(dh1, yssm, o, gates, wsp4, wap, wo, sel, tm):
    lp, d = dh1.shape
    w = yssm.shape[1]
    ns = d // 4
    nsel = sel.shape[1]

    def body(dh_ref, y_ref, o_ref, g_ref, wsp_ref, wap_ref, wo_ref, sel_ref,
             dg_ref, dms_ref, dma_ref, dy_ref, do_ref, dl_ref, dhb_ref):
        dhb = dh_ref[...].astype(BF16)
        dhb_ref[...] = dhb
        dm = _dot_nt(dhb, wo_ref[...])
        yb = y_ref[...]
        ob = o_ref[...]
        ms = jnp.concatenate([_dot(yb, wsp_ref[s]) for s in range(4)], axis=1)
        ma = _dot(ob, wap_ref[...])
        ss = _sigmoid(g_ref[:, :d])
        sa = _sigmoid(g_ref[:, d:])
        dg_ref[:, :d] = dm * ms * ss * (1.0 - ss)
        dg_ref[:, d:] = dm * ma * sa * (1.0 - sa)
        dms = (dm * ss).astype(BF16)
        dma = (dm * sa).astype(BF16)
        dms_ref[...] = dms
        dma_ref[...] = dma
        dy = _dot_nt(dms[:, :ns], wsp_ref[0])
        for s in range(1, 4):
            dy += _dot_nt(dms[:, s * ns:(s + 1) * ns], wsp_ref[s])
        dy_ref[...] = dy
        do = _dot_nt(dma, wap_ref[...])
        do_ref[...] = do.astype(BF16)
        dl_ref[...] = _split_dot(do * ob.astype(F32), sel_ref[...])

    return pl.pallas_call(
        body, name="merge_bwd", grid=(lp // tm,),
        in_specs=[_row(tm, d), _row(tm, w), _row(tm, d), _row(tm, 2 * d),
                  _full((4, w, ns)), _full((d, d)), _full((d, d)), _full((d, nsel))],
        out_specs=[_row(tm, 2 * d), _row(tm, d), _row(tm, d), _row(tm, w), _row(tm, d), _row(tm, nsel), _row(tm, d)],
        out_shape=[jax.ShapeDtypeStruct((lp, 2 * d), F32), jax.ShapeDtypeStruct((lp, d), BF16),
                   jax.ShapeDtypeStruct((lp, d), BF16), jax.ShapeDtypeStruct((lp, w), F32),
                   jax.ShapeDtypeStruct((lp, d), BF16), jax.ShapeDtypeStruct((lp, nsel), F32),
                   jax.ShapeDtypeStruct((lp, d), BF16)],
        compiler_params=_params(("parallel",)),
    )(dh1, yssm, o, gates, wsp4, wap, wo, sel)


def _mlp_in(h1, g, w4, tm):
    lp, d = h1.shape

    def body(x_ref, g_ref, w_ref, r_ref):
        h = _rms(x_ref[...], g_ref[...]).astype(BF16)
        for s in range(4):
            r_ref[:, s * d:(s + 1) * d] = jnp.maximum(_dot(h, w_ref[s]), 0.0).astype(BF16)

    return pl.pallas_call(
        body, name="mlp_in", grid=(lp // tm,),
        in_specs=[_row(tm, d), _full((1, d)), _full((4, d, d))],
        out_specs=_row(tm, 4 * d), out_shape=jax.ShapeDtypeStruct((lp, 4 * d), BF16),
        compiler_params=_params(("parallel",)),
    )(h1, g, w4)


def _square_bf16(r):
    rf = r.astype(F32)
    return (rf * rf).astype(BF16)


def _mlp_out(h1, r, w2, tm):
    lp, d = h1.shape
    ff = r.shape[1]

    def body(x_ref, r_ref, w_ref, o_ref):
        o_ref[...] = x_ref[...] + _dot(_square_bf16(r_ref[...]), w_ref[...])

    return pl.pallas_call(
        body, name="mlp_out", grid=(lp // tm,),
        in_specs=[_row(tm, d), _row(tm, ff), _full((ff, d))],
        out_specs=_row(tm, d), out_shape=jax.ShapeDtypeStruct((lp, d), F32),
        compiler_params=_params(("parallel",)),
    )(h1, r, w2)


def _final_loss(h3, g, tgt, rowmask, tm):
    lp, d = h3.shape

    def body(x_ref, g_ref, t_ref, m_ref, loss_ref, dx_ref, dg_ref):
        first = pl.program_id(0) == 0
        x = x_ref[...]
        gg = g_ref[...]
        err = (_rms(x, gg) - t_ref[...]) * m_ref[...]
        part = 0.5 * jnp.sum(jnp.sum(err * err, axis=1, keepdims=True), axis=0, keepdims=True) * (1.0 / d)
        part = jnp.broadcast_to(part, (SUBLANES, LANES))

        @pl.when(first)
        def _():
            loss_ref[...] = part

        @pl.when(jnp.logical_not(first))
        def _():
            loss_ref[...] += part

        dx, dgr = _rms_bwd(err * (1.0 / d), x, gg)
        dx_ref[...] = dx
        _acc_rows(dg_ref, dgr, first)

    return pl.pallas_call(
        body, name="final_loss", grid=(lp // tm,),
        in_specs=[_row(tm, d), _full((1, d)), _row(tm, d), _row(tm, 1)],
        out_specs=[_full((SUBLANES, LANES)), _row(tm, d), _full((1, d))],
        out_shape=[jax.ShapeDtypeStruct((SUBLANES, LANES), F32), jax.ShapeDtypeStruct((lp, d), F32),
                   jax.ShapeDtypeStruct((1, d), F32)],
        compiler_params=_params(("arbitrary",)),
    )(h3, g, tgt, rowmask)


def _mlp_bwd_a(dh3, r, w2, tm):
    lp, d = dh3.shape
    ff = r.shape[1]

    def body(dh_ref, r_ref, w_ref, dz_ref, dhb_ref):
        dhb = dh_ref[...].astype(BF16)
        dhb_ref[...] = dhb
        da = _dot_nt(dhb, w_ref[...])
        dz_ref[...] = (da * (2.0 * r_ref[...].astype(F32))).astype(BF16)

    return pl.pallas_call(
        body, name="mlp_bwd_a", grid=(lp // tm,),
        in_specs=[_row(tm, d), _row(tm, ff), _full((ff, d))],
        out_specs=[_row(tm, ff), _row(tm, d)],
        out_shape=[jax.ShapeDtypeStruct((lp, ff), BF16), jax.ShapeDtypeStruct((lp, d), BF16)],
        compiler_params=_params(("parallel",)),
    )(dh3, r, w2)


def _mlp_bwd_b(dz, dh3, h1, g, w4, tm):
    lp, d = h1.shape

    def body(dz_ref, dh_ref, x_ref, g_ref, w_ref, dx_ref, dg_ref):
        first = pl.program_id(0) == 0
        dh2 = _dot_nt(dz_ref[:, :d], w_ref[0])
        for s in range(1, 4):
            dh2 += _dot_nt(dz_ref[:, s * d:(s + 1) * d], w_ref[s])
        dx, dgr = _rms_bwd(dh2, x_ref[...], g_ref[...])
        dx_ref[...] = dh_ref[...] + dx
        _acc_rows(dg_ref, dgr, first)

    return pl.pallas_call(
        body, name="mlp_bwd_b", grid=(lp // tm,),
        in_specs=[_row(tm, 4 * d), _row(tm, d), _row(tm, d), _full((1, d)), _full((4, d, d))],
        out_specs=[_row(tm, d), _full((1, d))],
        out_shape=[jax.ShapeDtypeStruct((lp, d), F32), jax.ShapeDtypeStruct((1, d), F32)],
        compiler_params=_params(("arbitrary",)),
    )(dz, dh3, h1, g, w4)


def _in_proj_bwd(dyv, du0, du1, dskip, dqkv, dgates, dres, xin, g, w4, tm):
    lp, d = xin.shape
    hd = d // 2

    def body(dy_ref, a_ref, b_ref, ds_ref, dq_ref, dgt_ref, dr_ref, x_ref, g_ref, w_ref, dx_ref, dg_ref, dp_ref):
        first = pl.program_id(0) == 0
        du = (dy_ref[...] * ds_ref[...] + a_ref[...] + b_ref[...]).astype(BF16)
        dq = dq_ref[...].astype(BF16)
        dgt = dgt_ref[...].astype(BF16)
        dp_ref[:, :hd] = du
        dp_ref[:, hd:2 * d] = dq
        dp_ref[:, 2 * d:] = dgt
        dh = _dot_nt(du, w_ref[0, :, :hd]) + _dot_nt(dq[:, :hd], w_ref[0, :, hd:])
        dh += _dot_nt(dq[:, hd:], w_ref[1])
        dh += _dot_nt(dgt[:, :d], w_ref[2]) + _dot_nt(dgt[:, d:], w_ref[3])
        dx, dgr = _rms_bwd(dh, x_ref[...], g_ref[...])
        dx_ref[...] = dr_ref[...] + dx
        _acc_rows(dg_ref, dgr, first)

    return pl.pallas_call(
        body, name="in_proj_bwd", grid=(lp // tm,),
        in_specs=[_row(tm, hd), _row(tm, hd), _row(tm, hd), _full((1, hd)), _row(tm, 3 * hd), _row(tm, 2 * d),
                  _row(tm, d), _row(tm, d), _full((1, d)), _full((4, d, d))],
        out_specs=[_row(tm, d), _full((1, d)), _row(tm, 4 * d)],
        out_shape=[jax.ShapeDtypeStruct((lp, d), F32), jax.ShapeDtypeStruct((1, d), F32),
                   jax.ShapeDtypeStruct((lp, 4 * d), BF16)],
        compiler_params=_params(("arbitrary",)),
    )(dyv, du0, du1, dskip, dqkv, dgates, dres, xin, g, w4)


def _wgrad(a, dy, nshard, tm, tn, name, gain=None, square=False):
    lp, k = a.shape
    n = dy.shape[1]
    ns = n // nshard
    assert ns % tn == 0
    per = ns // tn

    def body(*refs):
        if gain is not None:
            a_ref, g_ref, dy_ref, o_ref = refs
            at = _rms(a_ref[...], g_ref[...]).astype(BF16)
        else:
            a_ref, dy_ref, o_ref = refs
            at = _square_bf16(a_ref[...]) if square else a_ref[...]
        i = pl.program_id(1)
        acc = _dot_tn(at, dy_ref[...])

        @pl.when(i == 0)
        def _():
            o_ref[0] = acc

        @pl.when(i != 0)
        def _():
            o_ref[0] += acc

    in_specs = [pl.BlockSpec((tm, k), lambda j, i: (i, 0))]
    args = [a]
    if gain is not None:
        in_specs.append(pl.BlockSpec((1, k), lambda j, i: (0, 0)))
        args.append(gain)
    in_specs.append(pl.BlockSpec((tm, tn), lambda j, i: (i, j)))
    args.append(dy)
    return pl.pallas_call(
        body, name=name, grid=(n // tn, lp // tm), in_specs=in_specs,
        out_specs=pl.BlockSpec((1, k, tn), lambda j, i: (j // per, 0, j % per)),
        out_shape=jax.ShapeDtypeStruct((nshard, k, ns), F32),
        compiler_params=_params(("parallel", "arbitrary")),
    )(*args)


def _head_tables(d):
    idx = np.arange(LANES)
    mean = (idx[:, None] // HEAD_DIM == idx[None, :] // HEAD_DIM).astype(np.float32) / HEAD_DIM
    n_heads = d // HEAD_DIM
    kvh = n_heads // GQA_REP
    c = np.arange(d)
    col = np.arange(kvh * LANES)
    head_of_col = (col // LANES) * GQA_REP + (col % LANES)
    sel = ((c[:, None] // HEAD_DIM == head_of_col[None, :]) & ((col % LANES) < GQA_REP)[None, :]).astype(np.float32)
    return jnp.asarray(mean, BF16), jnp.asarray(sel, BF16)


def _swap_pairs(y):
    lane = lax.broadcasted_iota(jnp.int32, y.shape, 1)
    return jnp.where(lane % 2 == 0, pltpu.roll(y, LANES - 1, 1), pltpu.roll(y, 1, 1))


def _qk_prep(qkv, cos_t, sin_t, qg, kg, mean_m, tm):
    lp, wq = qkv.shape
    d = wq * 2 // 3
    kvw = d // 4
    kvh = kvw // HEAD_DIM
    scale = HEAD_DIM ** -0.5

    def body(x_ref, c_ref, s_ref, qg_ref, kg_ref, m_ref, q_ref, k_ref, v_ref):
        cs, sn, mm = c_ref[...], s_ref[...], m_ref[...]
        for b in range((d + kvw) // LANES):
            x = x_ref[:, b * LANES:(b + 1) * LANES]
            gg = qg_ref[...] if b < d // LANES else kg_ref[...]
            y = x * lax.rsqrt(_split_dot(x * x, mm) + NORM_EPS) * gg
            out = y * cs + _swap_pairs(y) * sn
            if b < d // LANES:
                q_ref[:, b * LANES:(b + 1) * LANES] = (out * scale).astype(BF16)
            else:
                kb = b - d // LANES
                k_ref[2 * kb] = out[:, :HEAD_DIM].astype(BF16)
                k_ref[2 * kb + 1] = out[:, HEAD_DIM:].astype(BF16)
        lane = lax.broadcasted_iota(jnp.int32, (tm, LANES - HEAD_DIM), 1)
        ones_col = (lane == 0).astype(BF16)
        for h in range(kvh):
            vh = x_ref[:, d + kvw + h * HEAD_DIM:d + kvw + (h + 1) * HEAD_DIM].astype(BF16)
            v_ref[h] = jnp.concatenate([vh, ones_col], axis=1)

    k_spec = pl.BlockSpec((kvh, tm, HEAD_DIM), lambda i: (0, i, 0))
    v_spec = pl.BlockSpec((kvh, tm, LANES), lambda i: (0, i, 0))
    return pl.pallas_call(
        body, name="qk_prep", grid=(lp // tm,),
        in_specs=[_row(tm, wq), _row(tm, LANES), _row(tm, LANES), _full((1, LANES)), _full((1, LANES)),
                  _full((LANES, LANES))],
        out_specs=[_row(tm, d), k_spec, v_spec],
        out_shape=[jax.ShapeDtypeStruct((lp, d), BF16), jax.ShapeDtypeStruct((kvh, lp, HEAD_DIM), BF16),
                   jax.ShapeDtypeStruct((kvh, lp, LANES), BF16)],
        compiler_params=_params(("parallel",)),
    )(qkv, cos_t, sin_t, qg, kg, mean_m)


def _qk_bwd(qkv, dq, dk, dv, cos_t, sin_t, qg, kg, mean_m, tm):
    lp, wq = qkv.shape
    d = wq * 2 // 3
    kvw = d // 4
    kvh = kvw // HEAD_DIM
    scale = HEAD_DIM ** -0.5

    def body(x_ref, dq_ref, dk_ref, dv_ref, c_ref, s_ref, qg_ref, kg_ref, m_ref, o_ref, dqg_ref, dkg_ref):
        first = pl.program_id(0) == 0
        cs, sn, mm = c_ref[...], s_ref[...], m_ref[...]
        sums = [None, None]
        for b in range((d + kvw) // LANES):
            is_q = b < d // LANES
            x = x_ref[:, b * LANES:(b + 1) * LANES]
            gg = qg_ref[...] if is_q else kg_ref[...]
            r = lax.rsqrt(_split_dot(x * x, mm) + NORM_EPS)
            nrm = x * r
            if is_q:
                dout = dq_ref[:, b * LANES:(b + 1) * LANES] * scale
            else:
                kb = b - d // LANES
                dout = jnp.concatenate([dk_ref[2 * kb], dk_ref[2 * kb + 1]], axis=1)
            dy = dout * cs + _swap_pairs(dout * sn)
            part = jnp.sum(dy * nrm, axis=0, keepdims=True)
            sums[0 if is_q else 1] = part if sums[0 if is_q else 1] is None else sums[0 if is_q else 1] + part
            dn = dy * gg
            o_ref[:, b * LANES:(b + 1) * LANES] = r * (dn - nrm * _split_dot(dn * nrm, mm))
        for h in range(kvh):
            o_ref[:, d + kvw + h * HEAD_DIM:d + kvw + (h + 1) * HEAD_DIM] = dv_ref[h]
        for ref, s in ((dqg_ref, sums[0]), (dkg_ref, sums[1])):
            s = s + pltpu.roll(s, HEAD_DIM, 1)

            @pl.when(first)
            def _(ref=ref, s=s):
                ref[...] = s

            @pl.when(jnp.logical_not(first))
            def _(ref=ref, s=s):
                ref[...] += s

    kv_spec = pl.BlockSpec((kvh, tm, HEAD_DIM), lambda i: (0, i, 0))
    return pl.pallas_call(
        body, name="qk_bwd", grid=(lp // tm,),
        in_specs=[_row(tm, wq), _row(tm, d), kv_spec, kv_spec, _row(tm, LANES), _row(tm, LANES),
                  _full((1, LANES)), _full((1, LANES)), _full((LANES, LANES))],
        out_specs=[_row(tm, wq), _full((1, LANES)), _full((1, LANES))],
        out_shape=[jax.ShapeDtypeStruct((lp, wq), F32), jax.ShapeDtypeStruct((1, LANES), F32),
                   jax.ShapeDtypeStruct((1, LANES), F32)],
        compiler_params=_params(("arbitrary",)),
    )(qkv, dq, dk, dv, cos_t, sin_t, qg, kg, mean_m)


def _attn_fwd(q, k, v, kbias, tq, tk):
    lp, d = q.shape
    kvh = k.shape[0]
    rw = GQA_REP * HEAD_DIM
    nk = lp // tk

    def body(q_ref, k_ref, v_ref, kb_ref, o_ref, lse_ref, m_s, acc_s):
        j = pl.program_id(2)

        @pl.when(j == 0)
        def _():
            m_s[...] = jnp.full(m_s.shape, MASK_VALUE, F32)
            acc_s[...] = jnp.zeros(acc_s.shape, F32)

        def heads(masked):
            kk, vv = k_ref[0], v_ref[0]

            def scores(h):
                return _dot_nt(q_ref[:, h * HEAD_DIM:(h + 1) * HEAD_DIM], kk)

            def softmax(h, s):
                if masked:
                    s = s + kb_ref[...]
                m_prev = m_s[h]
                m_new = jnp.maximum(m_prev, jnp.max(s, axis=1, keepdims=True))
                m_s[h] = m_new
                return jnp.exp(s - m_new[:, :1]).astype(BF16), jnp.exp(m_prev - m_new)

            def accumulate(h, p, alpha):
                acc_s[h] = acc_s[h] * alpha + _dot(p, vv)

            ss = [scores(h) for h in range(GQA_REP)]
            pa = [softmax(h, ss[h]) for h in range(GQA_REP)]
            for h in range(GQA_REP):
                accumulate(h, *pa[h])

        pl.when(j != nk - 1)(functools.partial(heads, False))
        pl.when(j == nk - 1)(functools.partial(heads, True))

        @pl.when(j == nk - 1)
        def _():
            lane = lax.broadcasted_iota(jnp.int32, (tq, LANES), 1)
            lse = jnp.zeros((tq, LANES), F32)
            outs = []
            for h in range(GQA_REP):
                acc = acc_s[h]
                l = acc[:, HEAD_DIM:HEAD_DIM + 1]
                outs.append(acc[:, :HEAD_DIM] / l)
                lse = jnp.where(lane == h, m_s[h][:, :1] + jnp.log(l), lse)
            o_ref[...] = jnp.concatenate(outs, axis=1).astype(BF16)
            lse_ref[...] = lse

    return pl.pallas_call(
        body, name="attn_fwd", grid=(kvh, lp // tq, nk),
        in_specs=[pl.BlockSpec((tq, rw), lambda g, i, j: (i, g)),
                  pl.BlockSpec((1, tk, HEAD_DIM), lambda g, i, j: (g, j, 0)),
                  pl.BlockSpec((1, tk, LANES), lambda g, i, j: (g, j, 0)),
                  pl.BlockSpec((1, tk), lambda g, i, j: (0, j))],
        out_specs=[pl.BlockSpec((tq, rw), lambda g, i, j: (i, g)),
                   pl.BlockSpec((tq, LANES), lambda g, i, j: (i, g))],
        out_shape=[jax.ShapeDtypeStruct((lp, d), BF16), jax.ShapeDtypeStruct((lp, kvh * LANES), F32)],
        scratch_shapes=[pltpu.VMEM((GQA_REP, tq, LANES), F32), pltpu.VMEM((GQA_REP, tq, LANES), F32)],
        compiler_params=_params(("parallel", "parallel", "arbitrary")),
    )(q, k, v, kbias)


def _attn_bwd(q, k, v, kbias, do, lse, delta, tq, tk):
    lp, d = q.shape
    kvh = k.shape[0]
    rw = GQA_REP * HEAD_DIM
    nq = lp // tq

    def body(q_ref, k_ref, v_ref, kb_ref, do_ref, lse_ref, dl_ref, dq_ref, dk_ref, dv_ref, dk_s, dv_s):
        j = pl.program_id(1)
        i = pl.program_id(2)

        @pl.when(jnp.logical_and(i == 0, j == 0))
        def _():
            dq_ref[...] = jnp.zeros(dq_ref.shape, F32)

        @pl.when(i == 0)
        def _():
            dk_s[...] = jnp.zeros(dk_s.shape, F32)
            dv_s[...] = jnp.zeros(dv_s.shape, F32)

        def heads(masked):
            kk, vv = k_ref[0], v_ref[0][:, :HEAD_DIM]
            lse, dl = lse_ref[...], dl_ref[...]
            dqs = []
            for h in range(GQA_REP):
                qh = q_ref[:, h * HEAD_DIM:(h + 1) * HEAD_DIM]
                doh = do_ref[:, h * HEAD_DIM:(h + 1) * HEAD_DIM]
                s = _dot_nt(qh, kk)
                if masked:
                    s = s + kb_ref[...]
                p = jnp.exp(s - lse[:, h:h + 1])
                ds = (p * (_dot_nt(doh, vv) - dl[:, h:h + 1])).astype(BF16)
                dv_s[...] += _dot_tn(p.astype(BF16), doh)
                dk_s[...] += _dot_tn(ds, qh)
                dqs.append(_dot(ds, kk))
            rows = pl.ds(pl.multiple_of(i * tq, tq), tq)
            dq_ref[rows, :] += jnp.concatenate(dqs, axis=1)

        nk = lp // tk
        pl.when(j != nk - 1)(functools.partial(heads, False))
        pl.when(j == nk - 1)(functools.partial(heads, True))

        @pl.when(i == nq - 1)
        def _():
            dk_ref[0] = dk_s[...]
            dv_ref[0] = dv_s[...]

    return pl.pallas_call(
        body, name="attn_bwd", grid=(kvh, lp // tk, nq),
        in_specs=[pl.BlockSpec((tq, rw), lambda g, j, i: (i, g)),
                  pl.BlockSpec((1, tk, HEAD_DIM), lambda g, j, i: (g, j, 0)),
                  pl.BlockSpec((1, tk, LANES), lambda g, j, i: (g, j, 0)),
                  pl.BlockSpec((1, tk), lambda g, j, i: (0, j)),
                  pl.BlockSpec((tq, rw), lambda g, j, i: (i, g)),
                  pl.BlockSpec((tq, LANES), lambda g, j, i: (i, g)),
                  pl.BlockSpec((tq, LANES), lambda g, j, i: (i, g))],
        out_specs=[pl.BlockSpec((lp, rw), lambda g, j, i: (0, g)),
                   pl.BlockSpec((1, tk, HEAD_DIM), lambda g, j, i: (g, j, 0)),
                   pl.BlockSpec((1, tk, HEAD_DIM), lambda g, j, i: (g, j, 0))],
        out_shape=[jax.ShapeDtypeStruct((lp, d), F32), jax.ShapeDtypeStruct((kvh, lp, HEAD_DIM), F32),
                   jax.ShapeDtypeStruct((kvh, lp, HEAD_DIM), F32)],
        scratch_shapes=[pltpu.VMEM((tk, HEAD_DIM), F32), pltpu.VMEM((tk, HEAD_DIM), F32)],
        compiler_params=_params(("parallel", "arbitrary", "arbitrary")),
    )(q, k, v, kbias, do, lse, delta)


def _ssm_math(a_re, a_im, log_dt, bt_re, bt_im):
    dt = jnp.exp(log_dt)
    lam_re = jnp.minimum(a_re, EIG_RE_MAX)
    lam_im = a_im
    mag = jnp.exp(lam_re * dt)
    ang = lam_im * dt
    lb_re = mag * jnp.cos(ang)
    lb_im = mag * jnp.sin(ang)
    num_re = lb_re - 1.0
    num_im = lb_im
    den = lam_re * lam_re + lam_im * lam_im
    f_re = (num_re * lam_re + num_im * lam_im) / den
    f_im = (num_im * lam_re - num_re * lam_im) / den
    bb_re = f_re[:, None, :] * bt_re - f_im[:, None, :] * bt_im
    bb_im = f_re[:, None, :] * bt_im + f_im[:, None, :] * bt_re
    return lb_re, lb_im, bb_re, bb_im


def _ssm_discretize(a_re, a_im, log_dt, bt_re, bt_im):
    nd, g, n = a_re.shape
    p = bt_re.shape[2]

    def body(ar_ref, ai_ref, ld_ref, br_ref, bi_ref, bbr_ref, bbi_ref, pr_ref, pi_ref, hr_ref, hi_ref):
        lb_re, lb_im, bb_re, bb_im = _ssm_math(ar_ref[0], ai_ref[0], ld_ref[0], br_ref[0], bi_ref[0])
        bbr_ref[0] = bb_re
        bbi_ref[0] = bb_im
        cr, ci = lb_re, lb_im
        for k in range(KSTEPS):
            pr_ref[0, k] = cr
            pi_ref[0, k] = ci
            if k < KSTEPS - 1:
                cr, ci = cr * lb_re - ci * lb_im, cr * lb_im + ci * lb_re
        for t in range(2):
            cr, ci = cr * cr - ci * ci, 2.0 * cr * ci
            hr_ref[0, t] = cr
            hi_ref[0, t] = ci

    s3 = pl.BlockSpec((1, g, n), lambda i: (i, 0, 0))
    s4 = pl.BlockSpec((1, g, p, n), lambda i: (i, 0, 0, 0))
    sp = pl.BlockSpec((1, KSTEPS, g, n), lambda i: (i, 0, 0, 0))
    sh = pl.BlockSpec((1, 2, g, n), lambda i: (i, 0, 0, 0))
    return pl.pallas_call(
        body, name="ssm_discretize", grid=(nd,),
        in_specs=[s3, s3, pl.BlockSpec((1, g, 1), lambda i: (i, 0, 0)), s4, s4],
        out_specs=[s4, s4, sp, sp, sh, sh],
        out_shape=[jax.ShapeDtypeStruct((nd, g, p, n), F32)] * 2 + [jax.ShapeDtypeStruct((nd, KSTEPS, g, n), F32)] * 2
        + [jax.ShapeDtypeStruct((nd, 2, g, n), F32)] * 2,
        compiler_params=_params(("parallel",)),
    )(a_re, a_im, log_dt, bt_re, bt_im)


def _ssm_param_bwd(a_re, a_im, log_dt, bt_re, bt_im, dlb_re, dlb_im, dbb_re, dbb_im):
    nd, g, n = a_re.shape
    p = bt_re.shape[2]

    def body(ar_ref, ai_ref, ld_ref, br_ref, bi_ref, c0_ref, c1_ref, c2_ref, c3_ref,
             o0_ref, o1_ref, o2_ref, o3_ref, o4_ref):
        _, vjp = jax.vjp(_ssm_math, ar_ref[0], ai_ref[0], ld_ref[0], br_ref[0], bi_ref[0])
        outs = vjp((c0_ref[0], c1_ref[0], c2_ref[0], c3_ref[0]))
        for ref, val in zip((o0_ref, o1_ref, o2_ref, o3_ref, o4_ref), outs):
            ref[0] = val

    s3 = pl.BlockSpec((1, g, n), lambda i: (i, 0, 0))
    s1 = pl.BlockSpec((1, g, 1), lambda i: (i, 0, 0))
    s4 = pl.BlockSpec((1, g, p, n), lambda i: (i, 0, 0, 0))
    return pl.pallas_call(
        body, name="ssm_param_bwd", grid=(nd,),
        in_specs=[s3, s3, s1, s4, s4, s3, s3, s4, s4],
        out_specs=[s3, s3, s1, s4, s4],
        out_shape=[jax.ShapeDtypeStruct((nd, g, n), F32)] * 2 + [jax.ShapeDtypeStruct((nd, g, 1), F32)]
        + [jax.ShapeDtypeStruct((nd, g, p, n), F32)] * 2,
        compiler_params=_params(("parallel",)),
    )(a_re, a_im, log_dt, bt_re, bt_im, dlb_re, dlb_im, dbb_re, dbb_im)


def _cmul(ar, ai, xr, xi, conj):
    if conj:
        return ar * xr + ai * xi, ar * xi - ai * xr
    return ar * xr - ai * xi, ar * xi + ai * xr


def _scan_chunk(buf, tab, carry, ein, nj, rev, conj, base=0):
    ks = list(range(KSTEPS))
    if rev:
        ks = ks[::-1]
    sub = lax.broadcasted_iota(jnp.int32, (SUBLANES, SCAN_LANES), 0)
    edge = sub == (SUBLANES - 1 if rev else 0)

    def step(j, _):
        jr, ji = j, nj + j
        ar, ai = tab[base, jr], tab[base, ji]
        hr = jnp.zeros((SUBLANES, SCAN_LANES), F32)
        hi = jnp.zeros((SUBLANES, SCAN_LANES), F32)
        for k in ks:
            rows = pl.ds(k * SUBLANES, SUBLANES)
            pr, pi_ = _cmul(ar, ai, hr, hi, conj)
            hr = pr + buf[jr, rows, :]
            hi = pi_ + buf[ji, rows, :]
            buf[jr, rows, :] = hr
            buf[ji, rows, :] = hi
        shift = SUBLANES - 1 if rev else 1
        er = jnp.where(edge, carry[jr], pltpu.roll(hr, shift, 0))
        ei = jnp.where(edge, carry[ji], pltpu.roll(hi, shift, 0))
        for t, dist in enumerate((1, 2, 4)):
            sh = SUBLANES - dist if rev else dist
            pr, pi_ = _cmul(tab[base + 1 + t, jr], tab[base + 1 + t, ji], pltpu.roll(er, sh, 0), pltpu.roll(ei, sh, 0), conj)
            er, ei = er + pr, ei + pi_
        ein[jr] = er
        ein[ji] = ei
        pr, pi_ = _cmul(tab[base + 4 + KSTEPS - 1, jr], tab[base + 4 + KSTEPS - 1, ji], er, ei, conj)
        last = 0 if rev else SUBLANES - 1
        carry[jr] = jnp.broadcast_to((hr + pr)[last:last + 1, :], (SUBLANES, SCAN_LANES))
        carry[ji] = jnp.broadcast_to((hi + pi_)[last:last + 1, :], (SUBLANES, SCAN_LANES))
        for n, k in enumerate(ks):
            rows = pl.ds(k * SUBLANES, SUBLANES)
            pr, pi_ = _cmul(tab[base + 4 + n, jr], tab[base + 4 + n, ji], er, ei, conj)
            buf[jr, rows, :] += pr
            buf[ji, rows, :] += pi_
        return 0

    lax.fori_loop(0, nj, step, 0)


def _state_lanes(b):
    per = SCAN_LANES // SSM_BLOCK
    return b // per, slice((b % per) * SSM_BLOCK, (b % per + 1) * SSM_BLOCK)


def _project_in(src, w_ref, buf, nj):
    nb, cb, _ = w_ref.shape
    for b in range(nb):
        res = _dot(src[:, b * cb:(b + 1) * cb], w_ref[b])
        j, lanes = _state_lanes(b)
        buf[j, :, lanes] = res[:, :SSM_BLOCK]
        buf[nj + j, :, lanes] = res[:, SSM_BLOCK:]


def _state_block(buf, b, nj):
    j, lanes = _state_lanes(b)
    return jnp.concatenate([buf[j, :, lanes], buf[nj + j, :, lanes]], axis=1).astype(BF16)


def _project_out(buf, w_ref, nj):
    return jnp.concatenate([_dot_nt(_state_block(buf, b, nj), w_ref[b]) for b in range(w_ref.shape[0])], axis=1)


def _ssm_fwd(u, wb, wct, tab, rev, name):
    lp, w = u.shape
    nb, cb, _ = wb.shape
    nj = nb * SSM_BLOCK // SCAN_LANES
    nc = lp // CHUNK
    ntab = tab.shape[0]
    cidx = (lambda c: nc - 1 - c) if rev else (lambda c: c)

    def body(u_ref, wb_ref, wct_ref, tab_ref, y_ref, ck_ref, buf, carry, ein):
        @pl.when(pl.program_id(0) == 0)
        def _():
            carry[...] = jnp.zeros(carry.shape, F32)

        _project_in(u_ref[...].astype(BF16), wb_ref, buf, nj)
        ck_ref[0] = carry[...]
        _scan_chunk(buf, tab_ref, carry, ein, nj, rev, False)
        y_ref[...] = _project_out(buf, wct_ref, nj)

    wshape = (nb, cb, 2 * SSM_BLOCK)
    return pl.pallas_call(
        body, name=name, grid=(nc,),
        in_specs=[pl.BlockSpec((CHUNK, w), lambda c: (cidx(c), 0)), _full(wshape), _full(wshape),
                  _full((ntab, 2 * nj, SUBLANES, SCAN_LANES))],
        out_specs=[pl.BlockSpec((CHUNK, w), lambda c: (cidx(c), 0)),
                   pl.BlockSpec((1, 2 * nj, SUBLANES, SCAN_LANES), lambda c: (cidx(c), 0, 0, 0))],
        out_shape=[jax.ShapeDtypeStruct((lp, w), F32), jax.ShapeDtypeStruct((nc, 2 * nj, SUBLANES, SCAN_LANES), F32)],
        scratch_shapes=[pltpu.VMEM((2 * nj, CHUNK, SCAN_LANES), F32), pltpu.VMEM((2 * nj, SUBLANES, SCAN_LANES), F32),
                        pltpu.VMEM((2 * nj, SUBLANES, SCAN_LANES), F32)],
        compiler_params=_params(("arbitrary",)),
    )(u, wb, wct, tab)


def _ssm_bwd(u, dy, ckpt, wb, wct, tab, rev, name):
    lp, w = u.shape
    nb, cb, _ = wb.shape
    nj = nb * SSM_BLOCK // SCAN_LANES
    nc = lp // CHUNK
    ntab = tab.shape[0]
    cidx = (lambda c: c) if rev else (lambda c: nc - 1 - c)

    def body(u_ref, dy_ref, ck_ref, wb_ref, wct_ref, tab_hbm, du_ref, dbb_ref, dcc_ref, dlb_ref,
             tab_ref, dwb_ref, dwc_ref, xs, ls, xcar, lcar, xin, lin):
        c = pl.program_id(0)

        @pl.when(c == 0)
        def _():
            pltpu.sync_copy(tab_hbm, tab_ref)
            lcar[...] = jnp.zeros(lcar.shape, F32)
            dwb_ref[...] = jnp.zeros(dwb_ref.shape, F32)
            dwc_ref[...] = jnp.zeros(dwc_ref.shape, F32)
            dlb_ref[...] = jnp.zeros(dlb_ref.shape, F32)

        ub = u_ref[...].astype(BF16)
        dyb = dy_ref[...].astype(BF16)
        _project_in(ub, wb_ref, xs, nj)
        xcar[...] = ck_ref[0]
        _scan_chunk(xs, tab_ref, xcar, xin, nj, rev, False)
        _project_in(dyb, wct_ref, ls, nj)
        _scan_chunk(ls, tab_ref, lcar, lin, nj, not rev, True, base=ntab // 2)
        dus = []
        for b in range(nb):
            chans = slice(b * cb, (b + 1) * cb)
            xb = _state_block(xs, b, nj)
            lb = _state_block(ls, b, nj)
            dwc_ref[b] += _dot_tn(dyb[:, chans], xb)
            dwb_ref[b] += _dot_tn(ub[:, chans], lb)
            dus.append(_dot_nt(lb, wb_ref[b]))
        du_ref[...] = jnp.concatenate(dus, axis=1)

        def step(j, _):
            jr, ji = j, nj + j
            ar = jnp.zeros((SUBLANES, SCAN_LANES), F32)
            ai = jnp.zeros((SUBLANES, SCAN_LANES), F32)
            for k in range(KSTEPS):
                kp = k + 1 if rev else k - 1
                rows = pl.ds(k * SUBLANES, SUBLANES)
                if 0 <= kp < KSTEPS:
                    prow = pl.ds(kp * SUBLANES, SUBLANES)
                    xr, xi = xs[jr, prow, :], xs[ji, prow, :]
                else:
                    xr, xi = xin[jr], xin[ji]
                lr, li = ls[jr, rows, :], ls[ji, rows, :]
                ar += lr * xr + li * xi
                ai += li * xr - lr * xi
            dlb_ref[jr] += ar
            dlb_ref[ji] += ai
            return 0

        lax.fori_loop(0, nj, step, 0)

        @pl.when(c == nc - 1)
        def _():
            for b in range(2 * nj):
                dlb_ref[b] = jnp.broadcast_to(jnp.sum(dlb_ref[b], axis=0, keepdims=True), (SUBLANES, SCAN_LANES))
            for g in range(w // SSM_GROUP):
                b, gl = divmod(g, cb // SSM_GROUP)
                rows = slice(gl * SSM_GROUP, (gl + 1) * SSM_GROUP)
                for part in range(2):
                    cols = slice(part * SSM_BLOCK + gl * SSM_STATE, part * SSM_BLOCK + (gl + 1) * SSM_STATE)
                    dbb_ref[part, g * SSM_GROUP:(g + 1) * SSM_GROUP, :] = dwb_ref[b, rows, cols]
                    dcc_ref[part, g * SSM_GROUP:(g + 1) * SSM_GROUP, :] = dwc_ref[b, rows, cols]

    st = (2 * nj, SUBLANES, SCAN_LANES)
    wshape = (nb, cb, 2 * SSM_BLOCK)
    return pl.pallas_call(
        body, name=name, grid=(nc,),
        in_specs=[pl.BlockSpec((CHUNK, w), lambda c: (cidx(c), 0)), pl.BlockSpec((CHUNK, w), lambda c: (cidx(c), 0)),
                  pl.BlockSpec((1,) + st, lambda c: (cidx(c), 0, 0, 0)), _full(wshape), _full(wshape), _ANY],
        out_specs=[pl.BlockSpec((CHUNK, w), lambda c: (cidx(c), 0)), _full((2, w, SSM_STATE)),
                   _full((2, w, SSM_STATE)), _full(st)],
        out_shape=[jax.ShapeDtypeStruct((lp, w), F32), jax.ShapeDtypeStruct((2, w, SSM_STATE), F32),
                   jax.ShapeDtypeStruct((2, w, SSM_STATE), F32), jax.ShapeDtypeStruct(st, F32)],
        scratch_shapes=[pltpu.VMEM((ntab,) + st, F32), pltpu.VMEM(wshape, F32), pltpu.VMEM(wshape, F32),
                        pltpu.VMEM((2 * nj, CHUNK, SCAN_LANES), F32), pltpu.VMEM((2 * nj, CHUNK, SCAN_LANES), F32),
                        pltpu.VMEM(st, F32), pltpu.VMEM(st, F32), pltpu.VMEM(st, F32), pltpu.VMEM(st, F32)],
        compiler_params=_params(("arbitrary",)),
    )(u, dy, ckpt, wb, wct, tab)


def _embed_blocks(t_re, t_im):
    g, p, n = t_re.shape
    gb = SSM_BLOCK // n
    eye = jnp.eye(gb, dtype=t_re.dtype)
    parts = [jnp.einsum('bgpn,gh->bgphn', t.reshape(g // gb, gb, p, n), eye).reshape(g // gb, gb * p, gb * n)
             for t in (t_re, t_im)]
    return jnp.concatenate(parts, axis=2)


def _scan_tables(pw_re, pw_im, hi_re, hi_im, rev):
    s = pw_re.shape[1] * pw_re.shape[2]
    nj = s // SCAN_LANES
    sub = np.arange(SUBLANES)
    live = np.ones((4 + KSTEPS, 1, SUBLANES, 1), bool)
    for row, dist in ((1, 1), (2, 2), (3, 4)):
        live[row, 0, :, 0] = (sub < SUBLANES - dist) if rev else (sub >= dist)

    def lay(pw, hi):
        rows = jnp.concatenate([pw[:1], pw[KSTEPS - 1:], hi, pw], axis=0).reshape(4 + KSTEPS, nj, 1, SCAN_LANES)
        return jnp.where(live, jnp.broadcast_to(rows, (4 + KSTEPS, nj, SUBLANES, SCAN_LANES)), 0.0)

    return jnp.concatenate([lay(pw_re, hi_re), lay(pw_im, hi_im)], axis=1)


def _adamw(w, g, m, v, tm):
    r, c = w.shape
    c1 = 1.0 - ADAM_B1 ** ADAM_STEP
    c2 = 1.0 - ADAM_B2 ** ADAM_STEP

    def body(w_ref, g_ref, m_ref, v_ref, d_ref, nm_ref, nv_ref):
        gg = g_ref[...]
        nm = ADAM_B1 * m_ref[...] + (1.0 - ADAM_B1) * gg
        nv = ADAM_B2 * v_ref[...] + (1.0 - ADAM_B2) * (gg * gg)
        nm_ref[...] = nm
        nv_ref[...] = nv
        d_ref[...] = -ADAM_LR * ((nm / c1) / (jnp.sqrt(nv / c2) + ADAM_EPS) + ADAM_WD * w_ref[...])

    spec = _row(tm, c)
    return pl.pallas_call(
        body, name="adamw", grid=(r // tm,), in_specs=[spec] * 4, out_specs=[spec] * 3,
        out_shape=[jax.ShapeDtypeStruct((r, c), F32)] * 3, compiler_params=_params(("parallel",)),
    )(w, g, m, v)


def _pair_sum(g42, got, core, out_dtype, tm, name):
    _, _, r, c = g42.shape

    def body(core_ref, a_ref, b_ref, o_ref):
        o_ref[...] = (a_ref[...] + b_ref[...]).astype(out_dtype)

    grid_spec = pltpu.PrefetchScalarGridSpec(
        num_scalar_prefetch=1, grid=(4, r // tm),
        in_specs=[pl.BlockSpec((1, None, tm, c), lambda s, i, core_ref: (s, core_ref[0], i, 0)),
                  pl.BlockSpec((1, tm, c), lambda s, i, core_ref: (s, i, 0))],
        out_specs=pl.BlockSpec((1, tm, c), lambda s, i, core_ref: (s, i, 0)))
    return pl.pallas_call(
        body, name=name, grid_spec=grid_spec, out_shape=jax.ShapeDtypeStruct((4, r, c), out_dtype),
        compiler_params=_params(("parallel", "parallel")),
    )(core, g42, got)


def _sum4(a, core, tm, name):
    _, r, c = a.shape

    def body(core_ref, a_ref, o_ref):
        o_ref[...] = ((a_ref[0].astype(F32) + a_ref[1].astype(F32)) + a_ref[2].astype(F32)) + a_ref[3].astype(F32)

    grid_spec = pltpu.PrefetchScalarGridSpec(
        num_scalar_prefetch=1, grid=(r // tm,),
        in_specs=[pl.BlockSpec((4, tm, c), lambda i, core_ref: (0, i, 0))],
        out_specs=pl.BlockSpec((None, tm, c), lambda i, core_ref: (core_ref[0], i, 0)))
    return pl.pallas_call(
        body, name=name, grid_spec=grid_spec, out_shape=jax.ShapeDtypeStruct((2, r, c), F32),
        compiler_params=_params(("parallel",)),
    )(core, a)


_ANY = pl.BlockSpec(memory_space=pl.ANY)


def _all_gather8(blocks, name):
    n = len(blocks)

    def body(*refs):
        xs, outs = refs[:n], refs[n:2 * n]
        send_sems, recv_sems, local_sems = refs[2 * n:]
        x, y, c = lax.axis_index("x"), lax.axis_index("y"), lax.axis_index("c")
        me, sibling = (x, y, c), (x, y, 1 - c)
        chips = [(1 - x, y), (x, 1 - y), (1 - x, 1 - y)]

        def slot(t, px, py, pc):
            return outs[t].at[4 * px + 2 * py + pc]

        def copy(t, k, blk, to, src=None):
            return pltpu.make_async_remote_copy(
                src_ref=slot(t, *blk) if src is None else src, dst_ref=slot(t, *blk),
                send_sem=send_sems.at[t, k], recv_sem=recv_sems.at[t, k], device_id=to, device_id_type=MESH_ID)

        mine = [pltpu.make_async_copy(xs[t], slot(t, *me), local_sems.at[t]) for t in range(n)]
        for cp in mine:
            cp.start()
        first = [[copy(t, 0, me, sibling, src=xs[t])]
                 + [copy(t, 1 + j, me, (*chip, c), src=xs[t]) for j, chip in enumerate(chips)] for t in range(n)]
        for t in range(n):
            for cp in first[t]:
                cp.start()
        passed = [[copy(t, 4 + j, (*chip, c), sibling) for j, chip in enumerate(chips)] for t in range(n)]
        for j, chip in enumerate(chips):
            for t in range(n):
                copy(t, 1 + j, (*chip, c), me).wait_recv()
                passed[t][j].start()
        for t in range(n):
            copy(t, 0, sibling, me).wait_recv()
        for j, chip in enumerate(chips):
            for t in range(n):
                copy(t, 4 + j, (*chip, 1 - c), me).wait_recv()
        for t in range(n):
            for cp in first[t] + passed[t]:
                cp.wait_send()
        for cp in mine:
            cp.wait()

    return pl.pallas_call(
        body, name=name, out_shape=[jax.ShapeDtypeStruct((8,) + b.shape, b.dtype) for b in blocks],
        in_specs=[_ANY] * n, out_specs=[_ANY] * n,
        scratch_shapes=[pltpu.SemaphoreType.DMA((n, 7)), pltpu.SemaphoreType.DMA((n, 7)),
                        pltpu.SemaphoreType.DMA((n,))],
    )(*blocks)


def _pair_exchange(gs, name):
    n = len(gs)

    def body(*refs):
        g_refs, outs = refs[:n], refs[n:2 * n]
        send_sems, recv_sems = refs[2 * n:]
        x, y, c = lax.axis_index("x"), lax.axis_index("y"), lax.axis_index("c")
        cps = [pltpu.make_async_remote_copy(
            src_ref=g_refs[t].at[:, 1 - c], dst_ref=outs[t], send_sem=send_sems.at[t], recv_sem=recv_sems.at[t],
            device_id=(x, y, 1 - c), device_id_type=MESH_ID) for t in range(n)]
        for cp in cps:
            cp.start()
        for cp in cps:
            cp.wait()

    return pl.pallas_call(
        body, name=name,
        out_shape=[jax.ShapeDtypeStruct((g.shape[0],) + g.shape[2:], g.dtype) for g in gs],
        in_specs=[_ANY] * n, out_specs=[_ANY] * n,
        scratch_shapes=[pltpu.SemaphoreType.DMA((n,)), pltpu.SemaphoreType.DMA((n,))],
    )(*gs)


def _chip_scatter(ps, name):
    n = len(ps)

    def body(*refs):
        p_refs, outs = refs[:n], refs[n:2 * n]
        send_sems, recv_sems, local_sems = refs[2 * n:]
        x, y, c = lax.axis_index("x"), lax.axis_index("y"), lax.axis_index("c")
        mine = 2 * x + y
        chips = [(1 - x, y), (x, 1 - y), (1 - x, 1 - y)]
        own = [pltpu.make_async_copy(p_refs[t].at[mine], outs[t].at[mine], local_sems.at[t]) for t in range(n)]
        for cp in own:
            cp.start()

        def copy(t, k, src_slab, dst_slab, chip):
            return pltpu.make_async_remote_copy(
                src_ref=p_refs[t].at[src_slab], dst_ref=outs[t].at[dst_slab], send_sem=send_sems.at[t, k],
                recv_sem=recv_sems.at[t, k], device_id=(*chip, c), device_id_type=MESH_ID)

        sends = [[copy(t, k, 2 * cx + cy, mine, (cx, cy)) for k, (cx, cy) in enumerate(chips)] for t in range(n)]
        for k in range(3):
            for t in range(n):
                sends[t][k].start()
        for k, (cx, cy) in enumerate(chips):
            for t in range(n):
                copy(t, k, mine, 2 * cx + cy, (cx, cy)).wait_recv()
        for t in range(n):
            for cp in sends[t]:
                cp.wait_send()
        for cp in own:
            cp.wait()

    return pl.pallas_call(
        body, name=name, out_shape=[jax.ShapeDtypeStruct(p.shape, p.dtype) for p in ps],
        in_specs=[_ANY] * n, out_specs=[_ANY] * n,
        scratch_shapes=[pltpu.SemaphoreType.DMA((n, 3)), pltpu.SemaphoreType.DMA((n, 3)),
                        pltpu.SemaphoreType.DMA((n,))],
    )(*ps)


def _pair_gather(rs, name):
    n = len(rs)

    def body(*refs):
        ins, outs = refs[:n], refs[n:2 * n]
        send_sems, recv_sems = refs[2 * n:]
        x, y, c = lax.axis_index("x"), lax.axis_index("y"), lax.axis_index("c")

        def copy(t, slab):
            return pltpu.make_async_remote_copy(
                src_ref=ins[t].at[slab], dst_ref=outs[t].at[slab], send_sem=send_sems.at[t],
                recv_sem=recv_sems.at[t], device_id=(x, y, 1 - c), device_id_type=MESH_ID)

        sends = [copy(t, c) for t in range(n)]
        for cp in sends:
            cp.start()
        for t in range(n):
            copy(t, 1 - c).wait_recv()
        for cp in sends:
            cp.wait_send()

    return pl.pallas_call(
        body, name=name, out_shape=[jax.ShapeDtypeStruct(r.shape, r.dtype) for r in rs],
        in_specs=[_ANY] * n, out_specs=[_ANY] * n, input_output_aliases={t: t for t in range(n)},
        scratch_shapes=[pltpu.SemaphoreType.DMA((n,)), pltpu.SemaphoreType.DMA((n,))],
    )(*rs)


PACK_COLS = 1024
BIG = (("meta_tokens", 1), ("w_in", 1), ("w_glu", 0), ("w_ssm_proj", 1), ("w_attn_proj", 0), ("w_out", 0),
       ("w_mlp_in", 1), ("w_mlp_out", 0))
SMALL = ("norm_mix_g", "ssm_a_re", "ssm_a_im", "ssm_log_dt", "ssm_b_re", "ssm_b_im", "ssm_c_re", "ssm_c_im",
         "ssm_d", "b_glu", "q_norm_g", "k_norm_g", "norm_mlp_g", "norm_final_g")


def _pad_rows(flat, mult_rows):
    n = flat.shape[0]
    unit = PACK_COLS * mult_rows
    total = -(-n // unit) * unit
    return jnp.pad(flat, (0, total - n)).reshape(total // PACK_COLS, PACK_COLS)


def _half(t, c):
    return lax.dynamic_slice_in_dim(t, c * (t.shape[0] // 2), t.shape[0] // 2, 0)


def _gather_weights(shards, c):
    names = [name for name, _ in BIG]
    blocks = [_half(shards[name], c) if name == "meta_tokens" else _half(shards[name], c).astype(BF16)
              for name in names]
    got = _all_gather8(blocks, "weight_all_gather")
    return {name: g.reshape((4, 2 * g.shape[1]) + g.shape[2:]) for name, g in zip(names, got)}


def _reduce_gradients(big4, small_flat, c):
    names = [name for name, _ in BIG]
    n_small = small_flat.shape[0]
    unit = 8 * SUBLANES * PACK_COLS
    k = -(-n_small // unit) * unit
    small42 = jnp.pad(small_flat, (0, k - n_small)).reshape(4, 2, k // (8 * PACK_COLS), PACK_COLS)
    g42 = [big4[name].reshape(4, 2, big4[name].shape[1] // 2, big4[name].shape[2]) for name in names] + [small42]
    labels = names + ["small"]
    wire = [F32 if name == "meta_tokens" else BF16 for name in names] + [F32]
    tiles = [_pick_tile(g.shape[2], 256, SUBLANES if dt == F32 else 2 * SUBLANES) for g, dt in zip(g42, wire)]
    core = c.astype(jnp.int32).reshape(1)
    got = _pair_exchange(g42, "grad_pair_exchange")
    pair = [_pair_sum(g, o, core, dt, tm, "pair_sum_" + lb) for g, o, dt, tm, lb in zip(g42, got, wire, tiles, labels)]
    by_src = _chip_scatter(pair, "grad_chip_scatter")
    red = [_sum4(b, core, tm, "chip_sum_" + lb) for b, tm, lb in zip(by_src, tiles, labels)]
    both = _pair_gather(red[:-1], "grad_pair_gather")
    out = {name: b.reshape(2 * b.shape[1], b.shape[2]) for name, b in zip(names, both)}
    small_piece = lax.dynamic_index_in_dim(red[-1], c, 0, keepdims=False)
    small = _all_gather8([small_piece], "small_grad_all_gather")[0].reshape(-1)[:n_small]
    return out, small


def _to_chunk_order(a):
    lp = a.shape[0]
    rest = a.shape[1:]
    a = a.reshape((lp // CHUNK, SUBLANES, KSTEPS) + rest)
    return a.swapaxes(1, 2).reshape((lp,) + rest)


def _from_chunk_order(a):
    lp = a.shape[0]
    rest = a.shape[1:]
    a = a.reshape((lp // CHUNK, KSTEPS, SUBLANES) + rest)
    return a.swapaxes(1, 2).reshape((lp,) + rest)


def _rope_tables(l_total, lp):
    n_real = l_total - N_META
    pos = np.arange(n_real)
    row_id = (pos // GRID_W).astype(np.float32)
    col_id = (pos % GRID_W).astype(np.float32)
    ppa = HEAD_DIM // 4
    inv_freq = (ROPE_THETA ** (-np.arange(ppa, dtype=np.float64) / ppa)).astype(np.float32)
    ang = np.concatenate([row_id[:, None] * inv_freq, col_id[:, None] * inv_freq], axis=-1)
    ang = np.concatenate([np.zeros((N_META, HEAD_DIM // 2), np.float32), ang,
                          np.zeros((lp - l_total, HEAD_DIM // 2), np.float32)], axis=0).astype(np.float64)
    cos = np.repeat(np.cos(ang), 2, axis=1)
    sin = np.repeat(np.sin(ang), 2, axis=1) * np.tile(np.asarray([-1.0, 1.0]), HEAD_DIM // 2)
    reps = (1, LANES // HEAD_DIM)
    return np.tile(cos, reps).astype(np.float32), np.tile(sin, reps).astype(np.float32)


def kernel(x, meta_tokens, norm_mix_g, w_in, ssm_a_re, ssm_a_im, ssm_log_dt, ssm_b_re, ssm_b_im, ssm_c_re, ssm_c_im, ssm_d, w_glu, b_glu, q_norm_g, k_norm_g, w_ssm_proj, w_attn_proj, w_out, norm_mlp_g, w_mlp_in, w_mlp_out, norm_final_g, loss_target, m_meta_tokens, m_norm_mix_g, m_w_in, m_ssm_a_re, m_ssm_a_im, m_ssm_log_dt, m_ssm_b_re, m_ssm_b_im, m_ssm_c_re, m_ssm_c_im, m_ssm_d, m_w_glu, m_b_glu, m_q_norm_g, m_k_norm_g, m_w_ssm_proj, m_w_attn_proj, m_w_out, m_norm_mlp_g, m_w_mlp_in, m_w_mlp_out, m_norm_final_g, v_meta_tokens, v_norm_mix_g, v_w_in, v_ssm_a_re, v_ssm_a_im, v_ssm_log_dt, v_ssm_b_re, v_ssm_b_im, v_ssm_c_re, v_ssm_c_im, v_ssm_d, v_w_glu, v_b_glu, v_q_norm_g, v_k_norm_g, v_w_ssm_proj, v_w_attn_proj, v_w_out, v_norm_mlp_g, v_w_mlp_in, v_w_mlp_out, v_norm_final_g):
    args = dict(locals())
    names = list(dict.fromkeys([n for n, _ in BIG] + list(SMALL)))
    order = ['meta_tokens', 'norm_mix_g', 'w_in', 'ssm_a_re', 'ssm_a_im', 'ssm_log_dt', 'ssm_b_re', 'ssm_b_im',
             'ssm_c_re', 'ssm_c_im', 'ssm_d', 'w_glu', 'b_glu', 'q_norm_g', 'k_norm_g', 'w_ssm_proj', 'w_attn_proj',
             'w_out', 'norm_mlp_g', 'w_mlp_in', 'w_mlp_out', 'norm_final_g']
    assert sorted(names) == sorted(order)
    c_idx = lax.axis_index("c")

    seq, d = x.shape[1], x.shape[2]
    l_total = seq + N_META
    lp = -(-l_total // SEQ_ALIGN) * SEQ_ALIGN
    hd = d // 2
    n_groups = hd // SSM_GROUP
    n_state = n_groups * SSM_STATE
    nj = n_state // SCAN_LANES
    kvh = d // HEAD_DIM // GQA_REP

    shard2d = {}
    for name, _ in BIG:
        t = args[name]
        shard2d[name] = t.reshape(t.shape[-2], t.shape[-1])
    full = _gather_weights(shard2d, c_idx)
    meta_full = jnp.transpose(full["meta_tokens"], (1, 0, 2)).reshape(N_META, d)
    w_in4 = full["w_in"]
    w_mlp_in4 = full["w_mlp_in"]
    w_ssm_proj4 = full["w_ssm_proj"]
    w_glu_f = full["w_glu"].reshape(hd, hd)
    w_attn_proj_f = full["w_attn_proj"].reshape(d, d)
    w_out_f = full["w_out"].reshape(d, d)
    w_mlp_out_f = full["w_mlp_out"].reshape(4 * d, d)

    xin = jnp.concatenate([meta_full, x[0], jnp.zeros((lp - l_total, d), F32)], axis=0)
    xin = _to_chunk_order(xin)
    tgt = _to_chunk_order(jnp.pad(loss_target[0], ((N_META, lp - l_total), (0, 0))))
    pos = np.arange(lp)
    rowmask = jnp.asarray(_to_chunk_order(((pos >= N_META) & (pos < l_total)).astype(np.float32)[:, None]))
    kbias = jnp.asarray(_to_chunk_order(np.where(pos < l_total, 0.0, MASK_VALUE).astype(np.float32)[:, None])
                        .reshape(1, lp))
    cos_t, sin_t = (jnp.asarray(_to_chunk_order(t)) for t in _rope_tables(l_total, lp))
    mean_m, sel = _head_tables(d)

    tm = _pick_tile(lp, 320)
    tm_mid = _pick_tile(lp, 384)
    tm_big = _pick_tile(lp, 640)
    tq = _pick_tile(lp, ATTN_Q_TILE, LANES)
    tk = _pick_tile(lp, ATTN_K_TILE, MXU_DIM)
    assert lp - tk <= (l_total // CHUNK) * CHUNK
    g_mix = norm_mix_g.reshape(1, d)
    g_mlp = norm_mlp_g.reshape(1, d)
    g_fin = norm_final_g.reshape(1, d)
    qg = jnp.tile(q_norm_g.reshape(1, HEAD_DIM), (1, LANES // HEAD_DIM))
    kg = jnp.tile(k_norm_g.reshape(1, HEAD_DIM), (1, LANES // HEAD_DIM))
    dskip = ssm_d.reshape(1, hd)
    bglu = b_glu.reshape(1, hd)

    a_re, a_im = ssm_a_re[0], ssm_a_im[0]
    log_dt = ssm_log_dt[0][..., None]
    bt_re = jnp.swapaxes(ssm_b_re[0], 2, 3)
    bt_im = jnp.swapaxes(ssm_b_im[0], 2, 3)
    bb_re, bb_im, pw_re, pw_im, hi_re, hi_im = _ssm_discretize(a_re, a_im, log_dt, bt_re, bt_im)
    wb = [_embed_blocks(bb_re[i], bb_im[i]).astype(BF16) for i in range(2)]
    wct = [_embed_blocks(ssm_c_re[0, i], -ssm_c_im[0, i]).astype(BF16) for i in range(2)]
    tabs = [_scan_tables(pw_re[i], pw_im[i], hi_re[i], hi_im[i], rev=(i == 1)) for i in range(2)]
    tabs_adj = [_scan_tables(pw_re[i], pw_im[i], hi_re[i], hi_im[i], rev=(i == 0)) for i in range(2)]

    u, qkv, gates = _in_proj(xin, g_mix, w_in4, tm_mid)
    y0, ck0 = _ssm_fwd(u, wb[0], wct[0], tabs[0], False, "ssm_fwd_0")
    y1, ck1 = _ssm_fwd(u, wb[1], wct[1], tabs[1], True, "ssm_fwd_1")
    yssm = _glu_fwd(u, y0, y1, dskip, w_glu_f, bglu, tm_big)
    q, k, v = _qk_prep(qkv, cos_t, sin_t, qg, kg, mean_m, tm)
    o, lse = _attn_fwd(q, k, v, kbias, tq, tk)
    h1, merged = _merge_fwd(yssm, o, gates, xin, w_ssm_proj4, w_attn_proj_f, w_out_f, tm_mid)
    r = _mlp_in(h1, g_mlp, w_mlp_in4, tm_mid)
    h3 = _mlp_out(h1, r, w_mlp_out_f, tm_mid)
    loss_tile, dh3, d_gfin = _final_loss(h3, g_fin, tgt, rowmask, tm_big)
    loss = lax.psum(loss_tile[0, 0], ("x", "y", "c"))

    dz, dh3b = _mlp_bwd_a(dh3, r, w_mlp_out_f, tm_mid)
    dh1, d_gmlp = _mlp_bwd_b(dz, dh3, h1, g_mlp, w_mlp_in4, tm_mid)
    dgates, dms, dma, dyssm, do, delta, dh1b = _merge_bwd(dh1, yssm, o, gates, w_ssm_proj4, w_attn_proj_f, w_out_f,
                                                          sel, tm)
    dyv, d_wglu, d_bglu, d_dskip = _glu_bwd(dyssm, u, y0, y1, dskip, w_glu_f, bglu, tm_big)
    du0, dbb0, dcc0, dlb0 = _ssm_bwd(u, dyv, ck0, wb[0], wct[0], _both(tabs[0], tabs_adj[0]), False, "ssm_bwd_0")
    du1, dbb1, dcc1, dlb1 = _ssm_bwd(u, dyv, ck1, wb[1], wct[1], _both(tabs[1], tabs_adj[1]), True, "ssm_bwd_1")
    dq, dk, dv = _attn_bwd(q, k, v, kbias, do, lse, delta, tq, tk)
    dqkv, d_qg, d_kg = _qk_bwd(qkv, dq, dk, dv, cos_t, sin_t, qg, kg, mean_m, tm)
    dxin, d_gmix, dproj = _in_proj_bwd(dyv, du0, du1, dskip, dqkv, dgates, dh1, xin, g_mix, w_in4, tm)

    tn = min(d, 1024)
    tm_w = _pick_tile(lp, 3 * MXU_DIM, MXU_DIM)
    grads4 = {
        "w_in": _wgrad(xin, dproj, 4, tm_w, tn, "wgrad_in", gain=g_mix),
        "w_mlp_in": _wgrad(h1, dz, 4, tm_w, tn, "wgrad_mlp_in", gain=g_mlp),
        "w_mlp_out": _wgrad(r, dh3b, 1, tm_w, min(d, 256), "wgrad_mlp_out", square=True).reshape(4, d, d),
        "w_out": _wgrad(merged, dh1b, 1, tm_w, tn, "wgrad_out").reshape(4, d // 4, d),
        "w_attn_proj": _wgrad(o, dma, 1, tm_w, tn, "wgrad_attn_proj").reshape(4, d // 4, d),
        "w_ssm_proj": _wgrad(yssm, dms, 4, tm_w, d // 4, "wgrad_ssm_proj"),
        "w_glu": d_wglu.reshape(4, hd // 4, hd),
    }
    dx_nat = _from_chunk_order(dxin)
    grads4["meta_tokens"] = jnp.swapaxes(dx_nat[:N_META].reshape(N_META, 4, d // 4), 0, 1)
    grad_x = dx_nat[N_META:l_total][None]

    dlb = jnp.stack([dlb0, dlb1])[:, :, 0, :]
    dlb_re = dlb[:, :nj].reshape(2, n_groups, SSM_STATE)
    dlb_im = dlb[:, nj:].reshape(2, n_groups, SSM_STATE)
    gpn = (2, 2, n_groups, SSM_GROUP, SSM_STATE)
    dbb = jnp.stack([dbb0, dbb1]).reshape(gpn)
    dcc = jnp.stack([dcc0, dcc1]).reshape(gpn)
    d_are, d_aim, d_logdt, d_btre, d_btim = _ssm_param_bwd(a_re, a_im, log_dt, bt_re, bt_im, dlb_re, dlb_im,
                                                           dbb[:, 0], dbb[:, 1])
    small_grads = {
        "norm_mix_g": d_gmix, "ssm_a_re": d_are, "ssm_a_im": d_aim, "ssm_log_dt": d_logdt,
        "ssm_b_re": jnp.swapaxes(d_btre, 2, 3), "ssm_b_im": jnp.swapaxes(d_btim, 2, 3),
        "ssm_c_re": dcc[:, 0], "ssm_c_im": -dcc[:, 1],
        "ssm_d": d_dskip, "b_glu": d_bglu, "q_norm_g": d_qg[:, :HEAD_DIM], "k_norm_g": d_kg[:, :HEAD_DIM],
        "norm_mlp_g": d_gmlp, "norm_final_g": d_gfin,
    }
    small_flat = jnp.concatenate([small_grads[n].reshape(-1) for n in SMALL])

    red_big, red_small = _reduce_gradients(grads4, small_flat, c_idx)
    grad, delta_w, new_m, new_v = {}, {}, {}, {}
    for name, _ in BIG:
        w2 = shard2d[name]
        shp = args[name].shape
        g2 = red_big[name]
        t = _pick_tile(w2.shape[0], 256, 8)
        dl, nm, nv = _adamw(w2, g2, args["m_" + name].reshape(w2.shape), args["v_" + name].reshape(w2.shape), t)
        grad[name], delta_w[name], new_m[name], new_v[name] = (a.reshape(shp) for a in (g2, dl, nm, nv))

    def pack_small(prefix):
        flat = jnp.concatenate([args[prefix + n].reshape(-1) for n in SMALL])
        return _pad_rows(flat, SUBLANES)

    n_small = red_small.shape[0]
    gs = _pad_rows(red_small, SUBLANES)
    dl, nm, nv = _adamw(pack_small(""), gs, pack_small("m_"), pack_small("v_"), _pick_tile(gs.shape[0], 256, 8))
    off = 0
    for name in SMALL:
        shp = args[name].shape
        k = int(np.prod(shp))
        for dst, src in ((grad, gs), (delta_w, dl), (new_m, nm), (new_v, nv)):
            dst[name] = src.reshape(-1)[off:off + k].reshape(shp)
        off += k
    assert off == n_small

    return (loss, grad_x, *[grad[n] for n in order], *[delta_w[n] for n in order],
            *[new_m[n] for n in order], *[new_v[n] for n in order])


def _both(tab, tab_adj):
    return jnp.concatenate([tab, tab_adj], axis=0)
```

```python
import functools
import math

import numpy as np
import jax
import jax.numpy as jnp
from jax import lax
from jax.experimental import pallas as pl
from jax.experimental.pallas import tpu as pltpu

F32 = jnp.float32
BF16 = jnp.bfloat16

N_META = 16
GRID_W = 64
HEAD_DIM = 64
GQA_REP = 4
SSM_GROUP = 16
SSM_STATE = 64
ROPE_THETA = 10000.0
NORM_EPS = 1e-6
EIG_RE_MAX = -1e-4
ADAM_LR, ADAM_B1, ADAM_B2, ADAM_EPS, ADAM_WD, ADAM_STEP = 0.001, 0.9, 0.999, 1e-08, 0.01, 10

SUBLANES = 8
LANES = 128
CHUNK = 256
KSTEPS = CHUNK // SUBLANES
SCAN_LANES = 512
MXU_DIM = 256
SSM_BLOCK = MXU_DIM
SEQ_ALIGN = MXU_DIM
ATTN_Q_TILE = 384
ATTN_K_TILE = 11 * MXU_DIM
VMEM_LIMIT = 56 << 20
MASK_VALUE = -1e30
MESH_ID = pl.DeviceIdType.MESH


def _dot(a, b):
    return jnp.dot(a, b, preferred_element_type=F32)


def _dot_nt(a, b):
    return lax.dot_general(a, b, (((1,), (1,)), ((), ())), preferred_element_type=F32)


def _dot_tn(a, b):
    return lax.dot_general(a, b, (((0,), (0,)), ((), ())), preferred_element_type=F32)


def _row(tm, width):
    return pl.BlockSpec((tm, width), lambda i: (i, 0))


def _full(shape):
    nd = len(shape)
    return pl.BlockSpec(shape, lambda i: (0,) * nd)


def _params(sem):
    return pltpu.CompilerParams(dimension_semantics=sem, vmem_limit_bytes=VMEM_LIMIT)


def _pick_tile(n, cap, mult=16):
    best = None
    for t in range(mult, min(n, cap) + 1, mult):
        if n % t == 0:
            best = t
    assert best is not None, (n, cap)
    return best


def _rstd(x):
    return lax.rsqrt(jnp.mean(x * x, axis=-1, keepdims=True) + NORM_EPS)


def _rms(x, g):
    return x * _rstd(x) * g


def _rms_bwd(dy, x, g):
    r = _rstd(x)
    xh = x * r
    gdy = dy * g
    dx = r * (gdy - xh * jnp.mean(gdy * xh, axis=-1, keepdims=True))
    return dx, dy * xh


def _split_dot(x, m):
    hi = x.astype(BF16)
    lo = (x - hi.astype(F32)).astype(BF16)
    return _dot(hi, m) + _dot(lo, m)


def _sigmoid(x):
    return 1.0 / (1.0 + jnp.exp(-x))


def _acc_rows(ref, val, first):
    s = jnp.sum(val, axis=0, keepdims=True)

    @pl.when(first)
    def _():
        ref[...] = s

    @pl.when(jnp.logical_not(first))
    def _():
        ref[...] += s


def _in_proj(xin, g, w4, tm):
    lp, d = xin.shape
    hd = d // 2

    def body(x_ref, g_ref, w_ref, u_ref, qkv_ref, gt_ref):
        h = _rms(x_ref[...], g_ref[...]).astype(BF16)
        p0 = _dot(h, w_ref[0])
        u_ref[...] = p0[:, :hd]
        qkv_ref[:, :hd] = p0[:, hd:]
        qkv_ref[:, hd:] = _dot(h, w_ref[1])
        gt_ref[:, :d] = _dot(h, w_ref[2])
        gt_ref[:, d:] = _dot(h, w_ref[3])

    return pl.pallas_call(
        body, name="in_proj", grid=(lp // tm,),
        in_specs=[_row(tm, d), _full((1, d)), _full((4, d, d))],
        out_specs=[_row(tm, hd), _row(tm, 3 * hd), _row(tm, 2 * d)],
        out_shape=[jax.ShapeDtypeStruct((lp, hd), F32), jax.ShapeDtypeStruct((lp, 3 * hd), F32),
                   jax.ShapeDtypeStruct((lp, 2 * d), F32)],
        compiler_params=_params(("parallel",)),
    )(xin, g, w4)


def _gelu(y):
    return 0.5 * y * (1.0 + lax.erf(y * (1.0 / math.sqrt(2.0))))


def _gelu_grad(y):
    return 0.5 * (1.0 + lax.erf(y * (1.0 / math.sqrt(2.0)))) + y * jnp.exp(-0.5 * y * y) * (1.0 / math.sqrt(2.0 * math.pi))


def _glu_fwd(u, y0, y1, dskip, w_glu, b_glu, tm):
    lp, w = u.shape

    def body(u_ref, y0_ref, y1_ref, d_ref, w_ref, b_ref, o_ref):
        y = u_ref[...] * d_ref[...] + y0_ref[...] + y1_ref[...]
        z = _gelu(y)
        t = _dot(z.astype(BF16), w_ref[...]) + b_ref[...]
        o_ref[...] = (z * _sigmoid(t)).astype(BF16)

    return pl.pallas_call(
        body, name="glu_fwd", grid=(lp // tm,),
        in_specs=[_row(tm, w), _row(tm, w), _row(tm, w), _full((1, w)), _full((w, w)), _full((1, w))],
        out_specs=_row(tm, w), out_shape=jax.ShapeDtypeStruct((lp, w), BF16),
        compiler_params=_params(("parallel",)),
    )(u, y0, y1, dskip, w_glu, b_glu)


def _glu_bwd(dyssm, u, y0, y1, dskip, w_glu, b_glu, tm):
    lp, w = u.shape

    def body(g_ref, u_ref, y0_ref, y1_ref, d_ref, w_ref, b_ref, dy_ref, dw_ref, db_ref, dd_ref):
        first = pl.program_id(0) == 0
        uu = u_ref[...]
        y = uu * d_ref[...] + y0_ref[...] + y1_ref[...]
        z = _gelu(y)
        zb = z.astype(BF16)
        sg = _sigmoid(_dot(zb, w_ref[...]) + b_ref[...])
        g = g_ref[...]
        dt = g * z * sg * (1.0 - sg)
        dtb = dt.astype(BF16)
        dz = g * sg + _dot_nt(dtb, w_ref[...])
        dy = dz * _gelu_grad(y)
        dy_ref[...] = dy
        dw = _dot_tn(zb, dtb)

        @pl.when(first)
        def _():
            dw_ref[...] = dw

        @pl.when(jnp.logical_not(first))
        def _():
            dw_ref[...] += dw

        _acc_rows(db_ref, dt, first)
        _acc_rows(dd_ref, dy * uu, first)

    return pl.pallas_call(
        body, name="glu_bwd", grid=(lp // tm,),
        in_specs=[_row(tm, w), _row(tm, w), _row(tm, w), _row(tm, w), _full((1, w)), _full((w, w)), _full((1, w))],
        out_specs=[_row(tm, w), _full((w, w)), _full((1, w)), _full((1, w))],
        out_shape=[jax.ShapeDtypeStruct((lp, w), F32), jax.ShapeDtypeStruct((w, w), F32),
                   jax.ShapeDtypeStruct((1, w), F32), jax.ShapeDtypeStruct((1, w), F32)],
        compiler_params=_params(("arbitrary",)),
    )(dyssm, u, y0, y1, dskip, w_glu, b_glu)


def _merge_fwd(yssm, o, gates, xin, wsp4, wap, wo, tm):
    lp, d = xin.shape
    w = yssm.shape[1]
    ns = d // 4

    def body(y_ref, o_ref, g_ref, x_ref, wsp_ref, wap_ref, wo_ref, h_ref, m_ref):
        yb = y_ref[...]
        ms = jnp.concatenate([_dot(yb, wsp_ref[s]) for s in range(4)], axis=1)
        ma = _dot(o_ref[...], wap_ref[...])
        merged = (_sigmoid(g_ref[:, :d]) * ms + _sigmoid(g_ref[:, d:]) * ma).astype(BF16)
        m_ref[...] = merged
        h_ref[...] = x_ref[...] + _dot(merged, wo_ref[...])

    return pl.pallas_call(
        body, name="merge_fwd", grid=(lp // tm,),
        in_specs=[_row(tm, w), _row(tm, d), _row(tm, 2 * d), _row(tm, d),
                  _full((4, w, ns)), _full((d, d)), _full((d, d))],
        out_specs=[_row(tm, d), _row(tm, d)],
        out_shape=[jax.ShapeDtypeStruct((lp, d), F32), jax.ShapeDtypeStruct((lp, d), BF16)],
        compiler_params=_params(("parallel",)),
    )(yssm, o, gates, xin, wsp4, wap, wo)


def _merge_bwd(dh1, yssm, o, gates, wsp4, wap, wo, sel, tm):
    lp, d = dh1.shape
    w = yssm.shape[1]
    ns = d // 4
    nsel = sel.shape[1]

    def body(dh_ref, y_ref, o_ref, g_ref, wsp_ref, wap_ref, wo_ref, sel_ref,
             dg_ref, dms_ref, dma_ref, dy_ref, do_ref, dl_ref, dhb_ref):
        dhb = dh_ref[...].astype(BF16)
        dhb_ref[...] = dhb
        dm = _dot_nt(dhb, wo_ref[...])
        yb = y_ref[...]
        ob = o_ref[...]
        ms = jnp.concatenate([_dot(yb, wsp_ref[s]) for s in range(4)], axis=1)
        ma = _dot(ob, wap_ref[...])
        ss = _sigmoid(g_ref[:, :d])
        sa = _sigmoid(g_ref[:, d:])
        dg_ref[:, :d] = dm * ms * ss * (1.0 - ss)
        dg_ref[:, d:] = dm * ma * sa * (1.0 - sa)
        dms = (dm * ss).astype(BF16)
        dma = (dm * sa).astype(BF16)
        dms_ref[...] = dms
        dma_ref[...] = dma
        dy = _dot_nt(dms[:, :ns], wsp_ref[0])
        for s in range(1, 4):
            dy += _dot_nt(dms[:, s * ns:(s + 1) * ns], wsp_ref[s])
        dy_ref[...] = dy
        do = _dot_nt(dma, wap_ref[...])
        do_ref[...] = do.astype(BF16)
        dl_ref[...] = _split_dot(do * ob.astype(F32), sel_ref[...])

    return pl.pallas_call(
        body, name="merge_bwd", grid=(lp // tm,),
        in_specs=[_row(tm, d), _row(tm, w), _row(tm, d), _row(tm, 2 * d),
                  _full((4, w, ns)), _full((d, d)), _full((d, d)), _full((d, nsel))],
        out_specs=[_row(tm, 2 * d), _row(tm, d), _row(tm, d), _row(tm, w), _row(tm, d), _row(tm, nsel), _row(tm, d)],
        out_shape=[jax.ShapeDtypeStruct((lp, 2 * d), F32), jax.ShapeDtypeStruct((lp, d), BF16),
                   jax.ShapeDtypeStruct((lp, d), BF16), jax.ShapeDtypeStruct((lp, w), F32),
                   jax.ShapeDtypeStruct((lp, d), BF16), jax.ShapeDtypeStruct((lp, nsel), F32),
                   jax.ShapeDtypeStruct((lp, d), BF16)],
        compiler_params=_params(("parallel",)),
    )(dh1, yssm, o, gates, wsp4, wap, wo, sel)


def _mlp_in(h1, g, w4, tm):
    lp, d = h1.shape

    def body(x_ref, g_ref, w_ref, r_ref):
        h = _rms(x_ref[...], g_ref[...]).astype(BF16)
        for s in range(4):
            r_ref[:, s * d:(s + 1) * d] = jnp.maximum(_dot(h, w_ref[s]), 0.0).astype(BF16)

    return pl.pallas_call(
        body, name="mlp_in", grid=(lp // tm,),
        in_specs=[_row(tm, d), _full((1, d)), _full((4, d, d))],
        out_specs=_row(tm, 4 * d), out_shape=jax.ShapeDtypeStruct((lp, 4 * d), BF16),
        compiler_params=_params(("parallel",)),
    )(h1, g, w4)


def _square_bf16(r):
    rf = r.astype(F32)
    return (rf * rf).astype(BF16)


def _mlp_out(h1, r, w2, tm):
    lp, d = h1.shape
    ff = r.shape[1]

    def body(x_ref, r_ref, w_ref, o_ref):
        o_ref[...] = x_ref[...] + _dot(_square_bf16(r_ref[...]), w_ref[...])

    return pl.pallas_call(
        body, name="mlp_out", grid=(lp // tm,),
        in_specs=[_row(tm, d), _row(tm, ff), _full((ff, d))],
        out_specs=_row(tm, d), out_shape=jax.ShapeDtypeStruct((lp, d), F32),
        compiler_params=_params(("parallel",)),
    )(h1, r, w2)


def _final_loss(h3, g, tgt, rowmask, tm):
    lp, d = h3.shape

    def body(x_ref, g_ref, t_ref, m_ref, loss_ref, dx_ref, dg_ref):
        first = pl.program_id(0) == 0
        x = x_ref[...]
        gg = g_ref[...]
        err = (_rms(x, gg) - t_ref[...]) * m_ref[...]
        part = 0.5 * jnp.sum(jnp.sum(err * err, axis=1, keepdims=True), axis=0, keepdims=True) * (1.0 / d)
        part = jnp.broadcast_to(part, (SUBLANES, LANES))

        @pl.when(first)
        def _():
            loss_ref[...] = part

        @pl.when(jnp.logical_not(first))
        def _():
            loss_ref[...] += part

        dx, dgr = _rms_bwd(err * (1.0 / d), x, gg)
        dx_ref[...] = dx
        _acc_rows(dg_ref, dgr, first)

    return pl.pallas_call(
        body, name="final_loss", grid=(lp // tm,),
        in_specs=[_row(tm, d), _full((1, d)), _row(tm, d), _row(tm, 1)],
        out_specs=[_full((SUBLANES, LANES)), _row(tm, d), _full((1, d))],
        out_shape=[jax.ShapeDtypeStruct((SUBLANES, LANES), F32), jax.ShapeDtypeStruct((lp, d), F32),
                   jax.ShapeDtypeStruct((1, d), F32)],
        compiler_params=_params(("arbitrary",)),
    )(h3, g, tgt, rowmask)


def _mlp_bwd_a(dh3, r, w2, tm):
    lp, d = dh3.shape
    ff = r.shape[1]

    def body(dh_ref, r_ref, w_ref, dz_ref, dhb_ref):
        dhb = dh_ref[...].astype(BF16)
        dhb_ref[...] = dhb
        da = _dot_nt(dhb, w_ref[...])
        dz_ref[...] = (da * (2.0 * r_ref[...].astype(F32))).astype(BF16)

    return pl.pallas_call(
        body, name="mlp_bwd_a", grid=(lp // tm,),
        in_specs=[_row(tm, d), _row(tm, ff), _full((ff, d))],
        out_specs=[_row(tm, ff), _row(tm, d)],
        out_shape=[jax.ShapeDtypeStruct((lp, ff), BF16), jax.ShapeDtypeStruct((lp, d), BF16)],
        compiler_params=_params(("parallel",)),
    )(dh3, r, w2)


def _mlp_bwd_b(dz, dh3, h1, g, w4, tm):
    lp, d = h1.shape

    def body(dz_ref, dh_ref, x_ref, g_ref, w_ref, dx_ref, dg_ref):
        first = pl.program_id(0) == 0
        dh2 = _dot_nt(dz_ref[:, :d], w_ref[0])
        for s in range(1, 4):
            dh2 += _dot_nt(dz_ref[:, s * d:(s + 1) * d], w_ref[s])
        dx, dgr = _rms_bwd(dh2, x_ref[...], g_ref[...])
        dx_ref[...] = dh_ref[...] + dx
        _acc_rows(dg_ref, dgr, first)

    return pl.pallas_call(
        body, name="mlp_bwd_b", grid=(lp // tm,),
        in_specs=[_row(tm, 4 * d), _row(tm, d), _row(tm, d), _full((1, d)), _full((4, d, d))],
        out_specs=[_row(tm, d), _full((1, d))],
        out_shape=[jax.ShapeDtypeStruct((lp, d), F32), jax.ShapeDtypeStruct((1, d), F32)],
        compiler_params=_params(("arbitrary",)),
    )(dz, dh3, h1, g, w4)


def _in_proj_bwd(dyv, du0, du1, dskip, dqkv, dgates, dres, xin, g, w4, tm):
    lp, d = xin.shape
    hd = d // 2

    def body(dy_ref, a_ref, b_ref, ds_ref, dq_ref, dgt_ref, dr_ref, x_ref, g_ref, w_ref, dx_ref, dg_ref, dp_ref):
        first = pl.program_id(0) == 0
        du = (dy_ref[...] * ds_ref[...] + a_ref[...] + b_ref[...]).astype(BF16)
        dq = dq_ref[...].astype(BF16)
        dgt = dgt_ref[...].astype(BF16)
        dp_ref[:, :hd] = du
        dp_ref[:, hd:2 * d] = dq
        dp_ref[:, 2 * d:] = dgt
        dh = _dot_nt(du, w_ref[0, :, :hd]) + _dot_nt(dq[:, :hd], w_ref[0, :, hd:])
        dh += _dot_nt(dq[:, hd:], w_ref[1])
        dh += _dot_nt(dgt[:, :d], w_ref[2]) + _dot_nt(dgt[:, d:], w_ref[3])
        dx, dgr = _rms_bwd(dh, x_ref[...], g_ref[...])
        dx_ref[...] = dr_ref[...] + dx
        _acc_rows(dg_ref, dgr, first)

    return pl.pallas_call(
        body, name="in_proj_bwd", grid=(lp // tm,),
        in_specs=[_row(tm, hd), _row(tm, hd), _row(tm, hd), _full((1, hd)), _row(tm, 3 * hd), _row(tm, 2 * d),
                  _row(tm, d), _row(tm, d), _full((1, d)), _full((4, d, d))],
        out_specs=[_row(tm, d), _full((1, d)), _row(tm, 4 * d)],
        out_shape=[jax.ShapeDtypeStruct((lp, d), F32), jax.ShapeDtypeStruct((1, d), F32),
                   jax.ShapeDtypeStruct((lp, 4 * d), BF16)],
        compiler_params=_params(("arbitrary",)),
    )(dyv, du0, du1, dskip, dqkv, dgates, dres, xin, g, w4)


def _wgrad(a, dy, nshard, tm, tn, name, gain=None, square=False):
    lp, k = a.shape
    n = dy.shape[1]
    ns = n // nshard
    assert ns % tn == 0
    per = ns // tn

    def body(*refs):
        if gain is not None:
            a_ref, g_ref, dy_ref, o_ref = refs
            at = _rms(a_ref[...], g_ref[...]).astype(BF16)
        else:
            a_ref, dy_ref, o_ref = refs
            at = _square_bf16(a_ref[...]) if square else a_ref[...]
        i = pl.program_id(1)
        acc = _dot_tn(at, dy_ref[...])

        @pl.when(i == 0)
        def _():
            o_ref[0] = acc

        @pl.when(i != 0)
        def _():
            o_ref[0] += acc

    in_specs = [pl.BlockSpec((tm, k), lambda j, i: (i, 0))]
    args = [a]
    if gain is not None:
        in_specs.append(pl.BlockSpec((1, k), lambda j, i: (0, 0)))
        args.append(gain)
    in_specs.append(pl.BlockSpec((tm, tn), lambda j, i: (i, j)))
    args.append(dy)
    return pl.pallas_call(
        body, name=name, grid=(n // tn, lp // tm), in_specs=in_specs,
        out_specs=pl.BlockSpec((1, k, tn), lambda j, i: (j // per, 0, j % per)),
        out_shape=jax.ShapeDtypeStruct((nshard, k, ns), F32),
        compiler_params=_params(("parallel", "arbitrary")),
    )(*args)


def _head_tables(d):
    idx = np.arange(LANES)
    mean = (idx[:, None] // HEAD_DIM == idx[None, :] // HEAD_DIM).astype(np.float32) / HEAD_DIM
    n_heads = d // HEAD_DIM
    kvh = n_heads // GQA_REP
    c = np.arange(d)
    col = np.arange(kvh * LANES)
    head_of_col = (col // LANES) * GQA_REP + (col % LANES)
    sel = ((c[:, None] // HEAD_DIM == head_of_col[None, :]) & ((col % LANES) < GQA_REP)[None, :]).astype(np.float32)
    return jnp.asarray(mean, BF16), jnp.asarray(sel, BF16)


def _swap_pairs(y):
    lane = lax.broadcasted_iota(jnp.int32, y.shape, 1)
    return jnp.where(lane % 2 == 0, pltpu.roll(y, LANES - 1, 1), pltpu.roll(y, 1, 1))


def _qk_prep(qkv, cos_t, sin_t, qg, kg, mean_m, tm):
    lp, wq = qkv.shape
    d = wq * 2 // 3
    kvw = d // 4
    kvh = kvw // HEAD_DIM
    scale = HEAD_DIM ** -0.5

    def body(x_ref, c_ref, s_ref, qg_ref, kg_ref, m_ref, q_ref, k_ref, v_ref):
        cs, sn, mm = c_ref[...], s_ref[...], m_ref[...]
        for b in range((d + kvw) // LANES):
            x = x_ref[:, b * LANES:(b + 1) * LANES]
            gg = qg_ref[...] if b < d // LANES else kg_ref[...]
            y = x * lax.rsqrt(_split_dot(x * x, mm) + NORM_EPS) * gg
            out = y * cs + _swap_pairs(y) * sn
            if b < d // LANES:
                q_ref[:, b * LANES:(b + 1) * LANES] = (out * scale).astype(BF16)
            else:
                kb = b - d // LANES
                k_ref[2 * kb] = out[:, :HEAD_DIM].astype(BF16)
                k_ref[2 * kb + 1] = out[:, HEAD_DIM:].astype(BF16)
        lane = lax.broadcasted_iota(jnp.int32, (tm, LANES - HEAD_DIM), 1)
        ones_col = (lane == 0).astype(BF16)
        for h in range(kvh):
            vh = x_ref[:, d + kvw + h * HEAD_DIM:d + kvw + (h + 1) * HEAD_DIM].astype(BF16)
            v_ref[h] = jnp.concatenate([vh, ones_col], axis=1)

    k_spec = pl.BlockSpec((kvh, tm, HEAD_DIM), lambda i: (0, i, 0))
    v_spec = pl.BlockSpec((kvh, tm, LANES), lambda i: (0, i, 0))
    return pl.pallas_call(
        body, name="qk_prep", grid=(lp // tm,),
        in_specs=[_row(tm, wq), _row(tm, LANES), _row(tm, LANES), _full((1, LANES)), _full((1, LANES)),
                  _full((LANES, LANES))],
        out_specs=[_row(tm, d), k_spec, v_spec],
        out_shape=[jax.ShapeDtypeStruct((lp, d), BF16), jax.ShapeDtypeStruct((kvh, lp, HEAD_DIM), BF16),
                   jax.ShapeDtypeStruct((kvh, lp, LANES), BF16)],
        compiler_params=_params(("parallel",)),
    )(qkv, cos_t, sin_t, qg, kg, mean_m)


def _qk_bwd(qkv, dq, dk, dv, cos_t, sin_t, qg, kg, mean_m, tm):
    lp, wq = qkv.shape
    d = wq * 2 // 3
    kvw = d // 4
    kvh = kvw // HEAD_DIM
    scale = HEAD_DIM ** -0.5

    def body(x_ref, dq_ref, dk_ref, dv_ref, c_ref, s_ref, qg_ref, kg_ref, m_ref, o_ref, dqg_ref, dkg_ref):
        first = pl.program_id(0) == 0
        cs, sn, mm = c_ref[...], s_ref[...], m_ref[...]
        sums = [None, None]
        for b in range((d + kvw) // LANES):
            is_q = b < d // LANES
            x = x_ref[:, b * LANES:(b + 1) * LANES]
            gg = qg_ref[...] if is_q else kg_ref[...]
            r = lax.rsqrt(_split_dot(x * x, mm) + NORM_EPS)
            nrm = x * r
            if is_q:
                dout = dq_ref[:, b * LANES:(b + 1) * LANES] * scale
            else:
                kb = b - d // LANES
                dout = jnp.concatenate([dk_ref[2 * kb], dk_ref[2 * kb + 1]], axis=1)
            dy = dout * cs + _swap_pairs(dout * sn)
            part = jnp.sum(dy * nrm, axis=0, keepdims=True)
            sums[0 if is_q else 1] = part if sums[0 if is_q else 1] is None else sums[0 if is_q else 1] + part
            dn = dy * gg
            o_ref[:, b * LANES:(b + 1) * LANES] = r * (dn - nrm * _split_dot(dn * nrm, mm))
        for h in range(kvh):
            o_ref[:, d + kvw + h * HEAD_DIM:d + kvw + (h + 1) * HEAD_DIM] = dv_ref[h]
        for ref, s in ((dqg_ref, sums[0]), (dkg_ref, sums[1])):
            s = s + pltpu.roll(s, HEAD_DIM, 1)

            @pl.when(first)
            def _(ref=ref, s=s):
                ref[...] = s

            @pl.when(jnp.logical_not(first))
            def _(ref=ref, s=s):
                ref[...] += s

    kv_spec = pl.BlockSpec((kvh, tm, HEAD_DIM), lambda i: (0, i, 0))
    return pl.pallas_call(
        body, name="qk_bwd", grid=(lp // tm,),
        in_specs=[_row(tm, wq), _row(tm, d), kv_spec, kv_spec, _row(tm, LANES), _row(tm, LANES),
                  _full((1, LANES)), _full((1, LANES)), _full((LANES, LANES))],
        out_specs=[_row(tm, wq), _full((1, LANES)), _full((1, LANES))],
        out_shape=[jax.ShapeDtypeStruct((lp, wq), F32), jax.ShapeDtypeStruct((1, LANES), F32),
                   jax.ShapeDtypeStruct((1, LANES), F32)],
        compiler_params=_params(("arbitrary",)),
    )(qkv, dq, dk, dv, cos_t, sin_t, qg, kg, mean_m)


def _attn_fwd(q, k, v, kbias, tq, tk, gather=()):
    lp, d = q.shape
    kvh = k.shape[0]
    rw = GQA_REP * HEAD_DIM
    nk = lp // tk

    ng = len(gather)
    steps = kvh * (lp // tq) * nk

    def body(*refs):
        q_ref, k_ref, v_ref, kb_ref = refs[:4]
        o_ref, lse_ref = refs[4 + ng:6 + ng]
        m_s, acc_s = refs[6 + 2 * ng:8 + 2 * ng]
        j = pl.program_id(2)

        if ng:
            phases = _gather_phases(refs[4:4 + ng], refs[6 + ng:6 + 2 * ng], *refs[8 + 2 * ng:])
            step = (pl.program_id(0) * (lp // tq) + pl.program_id(1)) * nk + j
            for n, phase in enumerate(phases):
                pl.when(step == n * steps // 3)(phase)

        @pl.when(j == 0)
        def _():
            m_s[...] = jnp.full(m_s.shape, MASK_VALUE, F32)
            acc_s[...] = jnp.zeros(acc_s.shape, F32)

        def heads(masked):
            kk, vv = k_ref[0], v_ref[0]

            def scores(h):
                return _dot_nt(q_ref[:, h * HEAD_DIM:(h + 1) * HEAD_DIM], kk)

            def softmax(h, s):
                if masked:
                    s = s + kb_ref[...]
                m_prev = m_s[h]
                m_new = jnp.maximum(m_prev, jnp.max(s, axis=1, keepdims=True))
                m_s[h] = m_new
                return jnp.exp(s - m_new[:, :1]).astype(BF16), jnp.exp(m_prev - m_new)

            def accumulate(h, p, alpha):
                acc_s[h] = acc_s[h] * alpha + _dot(p, vv)

            ss = [scores(h) for h in range(GQA_REP)]
            pa = [softmax(h, ss[h]) for h in range(GQA_REP)]
            for h in range(GQA_REP):
                accumulate(h, *pa[h])

        pl.when(j != nk - 1)(functools.partial(heads, False))
        pl.when(j == nk - 1)(functools.partial(heads, True))

        @pl.when(j == nk - 1)
        def _():
            lane = lax.broadcasted_iota(jnp.int32, (tq, LANES), 1)
            lse = jnp.zeros((tq, LANES), F32)
            outs = []
            for h in range(GQA_REP):
                acc = acc_s[h]
                l = acc[:, HEAD_DIM:HEAD_DIM + 1]
                outs.append(acc[:, :HEAD_DIM] / l)
                lse = jnp.where(lane == h, m_s[h][:, :1] + jnp.log(l), lse)
            o_ref[...] = jnp.concatenate(outs, axis=1).astype(BF16)
            lse_ref[...] = lse

    sems = [pltpu.SemaphoreType.DMA((ng, 7)), pltpu.SemaphoreType.DMA((ng, 7)), pltpu.SemaphoreType.DMA((ng,))]
    res = pl.pallas_call(
        body, name="attn_fwd", grid=(kvh, lp // tq, nk),
        in_specs=[pl.BlockSpec((tq, rw), lambda g, i, j: (i, g)),
                  pl.BlockSpec((1, tk, HEAD_DIM), lambda g, i, j: (g, j, 0)),
                  pl.BlockSpec((1, tk, LANES), lambda g, i, j: (g, j, 0)),
                  pl.BlockSpec((1, tk), lambda g, i, j: (0, j))] + [_ANY] * ng,
        out_specs=[pl.BlockSpec((tq, rw), lambda g, i, j: (i, g)),
                   pl.BlockSpec((tq, LANES), lambda g, i, j: (i, g))] + [_ANY] * ng,
        out_shape=[jax.ShapeDtypeStruct((lp, d), BF16), jax.ShapeDtypeStruct((lp, kvh * LANES), F32)]
        + [jax.ShapeDtypeStruct((8,) + b.shape, b.dtype) for b in gather],
        scratch_shapes=[pltpu.VMEM((GQA_REP, tq, LANES), F32), pltpu.VMEM((GQA_REP, tq, LANES), F32)]
        + (sems if ng else []),
        compiler_params=_params(("arbitrary", "arbitrary", "arbitrary")),
    )(q, k, v, kbias, *gather)
    return res[0], res[1], list(res[2:])


def _attn_bwd(q, k, v, kbias, do, lse, delta, tq, tk):
    lp, d = q.shape
    kvh = k.shape[0]
    rw = GQA_REP * HEAD_DIM
    nq = lp // tq

    def body(q_ref, k_ref, v_ref, kb_ref, do_ref, lse_ref, dl_ref, dq_ref, dk_ref, dv_ref, dk_s, dv_s):
        j = pl.program_id(1)
        i = pl.program_id(2)

        @pl.when(jnp.logical_and(i == 0, j == 0))
        def _():
            dq_ref[...] = jnp.zeros(dq_ref.shape, F32)

        @pl.when(i == 0)
        def _():
            dk_s[...] = jnp.zeros(dk_s.shape, F32)
            dv_s[...] = jnp.zeros(dv_s.shape, F32)

        def heads(masked):
            kk, vv = k_ref[0], v_ref[0][:, :HEAD_DIM]
            lse, dl = lse_ref[...], dl_ref[...]
            dqs = []
            for h in range(GQA_REP):
                qh = q_ref[:, h * HEAD_DIM:(h + 1) * HEAD_DIM]
                doh = do_ref[:, h * HEAD_DIM:(h + 1) * HEAD_DIM]
                s = _dot_nt(qh, kk)
                if masked:
                    s = s + kb_ref[...]
                p = jnp.exp(s - lse[:, h:h + 1])
                ds = (p * (_dot_nt(doh, vv) - dl[:, h:h + 1])).astype(BF16)
                dv_s[...] += _dot_tn(p.astype(BF16), doh)
                dk_s[...] += _dot_tn(ds, qh)
                dqs.append(_dot(ds, kk))
            rows = pl.ds(pl.multiple_of(i * tq, tq), tq)
            dq_ref[rows, :] += jnp.concatenate(dqs, axis=1)

        nk = lp // tk
        pl.when(j != nk - 1)(functools.partial(heads, False))
        pl.when(j == nk - 1)(functools.partial(heads, True))

        @pl.when(i == nq - 1)
        def _():
            dk_ref[0] = dk_s[...]
            dv_ref[0] = dv_s[...]

    return pl.pallas_call(
        body, name="attn_bwd", grid=(kvh, lp // tk, nq),
        in_specs=[pl.BlockSpec((tq, rw), lambda g, j, i: (i, g)),
                  pl.BlockSpec((1, tk, HEAD_DIM), lambda g, j, i: (g, j, 0)),
                  pl.BlockSpec((1, tk, LANES), lambda g, j, i: (g, j, 0)),
                  pl.BlockSpec((1, tk), lambda g, j, i: (0, j)),
                  pl.BlockSpec((tq, rw), lambda g, j, i: (i, g)),
                  pl.BlockSpec((tq, LANES), lambda g, j, i: (i, g)),
                  pl.BlockSpec((tq, LANES), lambda g, j, i: (i, g))],
        out_specs=[pl.BlockSpec((lp, rw), lambda g, j, i: (0, g)),
                   pl.BlockSpec((1, tk, HEAD_DIM), lambda g, j, i: (g, j, 0)),
                   pl.BlockSpec((1, tk, HEAD_DIM), lambda g, j, i: (g, j, 0))],
        out_shape=[jax.ShapeDtypeStruct((lp, d), F32), jax.ShapeDtypeStruct((kvh, lp, HEAD_DIM), F32),
                   jax.ShapeDtypeStruct((kvh, lp, HEAD_DIM), F32)],
        scratch_shapes=[pltpu.VMEM((tk, HEAD_DIM), F32), pltpu.VMEM((tk, HEAD_DIM), F32)],
        compiler_params=_params(("parallel", "arbitrary", "arbitrary")),
    )(q, k, v, kbias, do, lse, delta)


def _ssm_math(a_re, a_im, log_dt, bt_re, bt_im):
    dt = jnp.exp(log_dt)
    lam_re = jnp.minimum(a_re, EIG_RE_MAX)
    lam_im = a_im
    mag = jnp.exp(lam_re * dt)
    ang = lam_im * dt
    lb_re = mag * jnp.cos(ang)
    lb_im = mag * jnp.sin(ang)
    num_re = lb_re - 1.0
    num_im = lb_im
    den = lam_re * lam_re + lam_im * lam_im
    f_re = (num_re * lam_re + num_im * lam_im) / den
    f_im = (num_im * lam_re - num_re * lam_im) / den
    bb_re = f_re[:, None, :] * bt_re - f_im[:, None, :] * bt_im
    bb_im = f_re[:, None, :] * bt_im + f_im[:, None, :] * bt_re
    return lb_re, lb_im, bb_re, bb_im


def _ssm_discretize(a_re, a_im, log_dt, bt_re, bt_im):
    nd, g, n = a_re.shape
    p = bt_re.shape[2]

    def body(ar_ref, ai_ref, ld_ref, br_ref, bi_ref, bbr_ref, bbi_ref, pr_ref, pi_ref, hr_ref, hi_ref):
        lb_re, lb_im, bb_re, bb_im = _ssm_math(ar_ref[0], ai_ref[0], ld_ref[0], br_ref[0], bi_ref[0])
        bbr_ref[0] = bb_re
        bbi_ref[0] = bb_im
        cr, ci = lb_re, lb_im
        for k in range(KSTEPS):
            pr_ref[0, k] = cr
            pi_ref[0, k] = ci
            if k < KSTEPS - 1:
                cr, ci = cr * lb_re - ci * lb_im, cr * lb_im + ci * lb_re
        for t in range(2):
            cr, ci = cr * cr - ci * ci, 2.0 * cr * ci
            hr_ref[0, t] = cr
            hi_ref[0, t] = ci

    s3 = pl.BlockSpec((1, g, n), lambda i: (i, 0, 0))
    s4 = pl.BlockSpec((1, g, p, n), lambda i: (i, 0, 0, 0))
    sp = pl.BlockSpec((1, KSTEPS, g, n), lambda i: (i, 0, 0, 0))
    sh = pl.BlockSpec((1, 2, g, n), lambda i: (i, 0, 0, 0))
    return pl.pallas_call(
        body, name="ssm_discretize", grid=(nd,),
        in_specs=[s3, s3, pl.BlockSpec((1, g, 1), lambda i: (i, 0, 0)), s4, s4],
        out_specs=[s4, s4, sp, sp, sh, sh],
        out_shape=[jax.ShapeDtypeStruct((nd, g, p, n), F32)] * 2 + [jax.ShapeDtypeStruct((nd, KSTEPS, g, n), F32)] * 2
        + [jax.ShapeDtypeStruct((nd, 2, g, n), F32)] * 2,
        compiler_params=_params(("parallel",)),
    )(a_re, a_im, log_dt, bt_re, bt_im)


def _ssm_param_bwd(a_re, a_im, log_dt, bt_re, bt_im, dlb_re, dlb_im, dbb_re, dbb_im):
    nd, g, n = a_re.shape
    p = bt_re.shape[2]

    def body(ar_ref, ai_ref, ld_ref, br_ref, bi_ref, c0_ref, c1_ref, c2_ref, c3_ref,
             o0_ref, o1_ref, o2_ref, o3_ref, o4_ref):
        _, vjp = jax.vjp(_ssm_math, ar_ref[0], ai_ref[0], ld_ref[0], br_ref[0], bi_ref[0])
        outs = vjp((c0_ref[0], c1_ref[0], c2_ref[0], c3_ref[0]))
        for ref, val in zip((o0_ref, o1_ref, o2_ref, o3_ref, o4_ref), outs):
            ref[0] = val

    s3 = pl.BlockSpec((1, g, n), lambda i: (i, 0, 0))
    s1 = pl.BlockSpec((1, g, 1), lambda i: (i, 0, 0))
    s4 = pl.BlockSpec((1, g, p, n), lambda i: (i, 0, 0, 0))
    return pl.pallas_call(
        body, name="ssm_param_bwd", grid=(nd,),
        in_specs=[s3, s3, s1, s4, s4, s3, s3, s4, s4],
        out_specs=[s3, s3, s1, s4, s4],
        out_shape=[jax.ShapeDtypeStruct((nd, g, n), F32)] * 2 + [jax.ShapeDtypeStruct((nd, g, 1), F32)]
        + [jax.ShapeDtypeStruct((nd, g, p, n), F32)] * 2,
        compiler_params=_params(("parallel",)),
    )(a_re, a_im, log_dt, bt_re, bt_im, dlb_re, dlb_im, dbb_re, dbb_im)


def _cmul(ar, ai, xr, xi, conj):
    if conj:
        return ar * xr + ai * xi, ar * xi - ai * xr
    return ar * xr - ai * xi, ar * xi + ai * xr


def _scan_chunk(buf, tab, carry, ein, nj, rev, conj, base=0):
    ks = list(range(KSTEPS))
    if rev:
        ks = ks[::-1]
    sub = lax.broadcasted_iota(jnp.int32, (SUBLANES, SCAN_LANES), 0)
    edge = sub == (SUBLANES - 1 if rev else 0)

    def step(j, _):
        jr, ji = j, nj + j
        ar, ai = tab[base, jr], tab[base, ji]
        hr = jnp.zeros((SUBLANES, SCAN_LANES), F32)
        hi = jnp.zeros((SUBLANES, SCAN_LANES), F32)
        for k in ks:
            rows = pl.ds(k * SUBLANES, SUBLANES)
            pr, pi_ = _cmul(ar, ai, hr, hi, conj)
            hr = pr + buf[jr, rows, :]
            hi = pi_ + buf[ji, rows, :]
            buf[jr, rows, :] = hr
            buf[ji, rows, :] = hi
        shift = SUBLANES - 1 if rev else 1
        er = jnp.where(edge, carry[jr], pltpu.roll(hr, shift, 0))
        ei = jnp.where(edge, carry[ji], pltpu.roll(hi, shift, 0))
        for t, dist in enumerate((1, 2, 4)):
            sh = SUBLANES - dist if rev else dist
            pr, pi_ = _cmul(tab[base + 1 + t, jr], tab[base + 1 + t, ji], pltpu.roll(er, sh, 0), pltpu.roll(ei, sh, 0), conj)
            er, ei = er + pr, ei + pi_
        ein[jr] = er
        ein[ji] = ei
        pr, pi_ = _cmul(tab[base + 4 + KSTEPS - 1, jr], tab[base + 4 + KSTEPS - 1, ji], er, ei, conj)
        last = 0 if rev else SUBLANES - 1
        carry[jr] = jnp.broadcast_to((hr + pr)[last:last + 1, :], (SUBLANES, SCAN_LANES))
        carry[ji] = jnp.broadcast_to((hi + pi_)[last:last + 1, :], (SUBLANES, SCAN_LANES))
        for n, k in enumerate(ks):
            rows = pl.ds(k * SUBLANES, SUBLANES)
            pr, pi_ = _cmul(tab[base + 4 + n, jr], tab[base + 4 + n, ji], er, ei, conj)
            buf[jr, rows, :] += pr
            buf[ji, rows, :] += pi_
        return 0

    lax.fori_loop(0, nj, step, 0)


def _state_lanes(b):
    per = SCAN_LANES // SSM_BLOCK
    return b // per, slice((b % per) * SSM_BLOCK, (b % per + 1) * SSM_BLOCK)


def _project_in(src, w_ref, buf, nj):
    nb, cb, _ = w_ref.shape
    for b in range(nb):
        res = _dot(src[:, b * cb:(b + 1) * cb], w_ref[b])
        j, lanes = _state_lanes(b)
        buf[j, :, lanes] = res[:, :SSM_BLOCK]
        buf[nj + j, :, lanes] = res[:, SSM_BLOCK:]


def _state_block(buf, b, nj):
    j, lanes = _state_lanes(b)
    return jnp.concatenate([buf[j, :, lanes], buf[nj + j, :, lanes]], axis=1).astype(BF16)


def _project_out(buf, w_ref, nj):
    return jnp.concatenate([_dot_nt(_state_block(buf, b, nj), w_ref[b]) for b in range(w_ref.shape[0])], axis=1)


def _ssm_fwd(u, wb, wct, tab, rev, name):
    lp, w = u.shape
    nb, cb, _ = wb.shape
    nj = nb * SSM_BLOCK // SCAN_LANES
    nc = lp // CHUNK
    ntab = tab.shape[0]
    cidx = (lambda c: nc - 1 - c) if rev else (lambda c: c)

    def body(u_ref, wb_ref, wct_ref, tab_ref, y_ref, ck_ref, buf, carry, ein):
        @pl.when(pl.program_id(0) == 0)
        def _():
            carry[...] = jnp.zeros(carry.shape, F32)

        _project_in(u_ref[...].astype(BF16), wb_ref, buf, nj)
        ck_ref[0] = carry[...]
        _scan_chunk(buf, tab_ref, carry, ein, nj, rev, False)
        y_ref[...] = _project_out(buf, wct_ref, nj)

    wshape = (nb, cb, 2 * SSM_BLOCK)
    return pl.pallas_call(
        body, name=name, grid=(nc,),
        in_specs=[pl.BlockSpec((CHUNK, w), lambda c: (cidx(c), 0)), _full(wshape), _full(wshape),
                  _full((ntab, 2 * nj, SUBLANES, SCAN_LANES))],
        out_specs=[pl.BlockSpec((CHUNK, w), lambda c: (cidx(c), 0)),
                   pl.BlockSpec((1, 2 * nj, SUBLANES, SCAN_LANES), lambda c: (cidx(c), 0, 0, 0))],
        out_shape=[jax.ShapeDtypeStruct((lp, w), F32), jax.ShapeDtypeStruct((nc, 2 * nj, SUBLANES, SCAN_LANES), F32)],
        scratch_shapes=[pltpu.VMEM((2 * nj, CHUNK, SCAN_LANES), F32), pltpu.VMEM((2 * nj, SUBLANES, SCAN_LANES), F32),
                        pltpu.VMEM((2 * nj, SUBLANES, SCAN_LANES), F32)],
        compiler_params=_params(("arbitrary",)),
    )(u, wb, wct, tab)


def _ssm_bwd(u, dy, ckpt, wb, wct, tab, rev, name):
    lp, w = u.shape
    nb, cb, _ = wb.shape
    nj = nb * SSM_BLOCK // SCAN_LANES
    nc = lp // CHUNK
    ntab = tab.shape[0]
    cidx = (lambda c: c) if rev else (lambda c: nc - 1 - c)

    def body(u_ref, dy_ref, ck_ref, wb_ref, wct_ref, tab_hbm, du_ref, dbb_ref, dcc_ref, dlb_ref,
             tab_ref, dwb_ref, dwc_ref, xs, ls, xcar, lcar, xin, lin):
        c = pl.program_id(0)

        @pl.when(c == 0)
        def _():
            pltpu.sync_copy(tab_hbm, tab_ref)
            lcar[...] = jnp.zeros(lcar.shape, F32)
            dwb_ref[...] = jnp.zeros(dwb_ref.shape, F32)
            dwc_ref[...] = jnp.zeros(dwc_ref.shape, F32)
            dlb_ref[...] = jnp.zeros(dlb_ref.shape, F32)

        ub = u_ref[...].astype(BF16)
        dyb = dy_ref[...].astype(BF16)
        _project_in(ub, wb_ref, xs, nj)
        xcar[...] = ck_ref[0]
        _scan_chunk(xs, tab_ref, xcar, xin, nj, rev, False)
        _project_in(dyb, wct_ref, ls, nj)
        _scan_chunk(ls, tab_ref, lcar, lin, nj, not rev, True, base=ntab // 2)
        dus = []
        for b in range(nb):
            chans = slice(b * cb, (b + 1) * cb)
            xb = _state_block(xs, b, nj)
            lb = _state_block(ls, b, nj)
            dwc_ref[b] += _dot_tn(dyb[:, chans], xb)
            dwb_ref[b] += _dot_tn(ub[:, chans], lb)
            dus.append(_dot_nt(lb, wb_ref[b]))
        du_ref[...] = jnp.concatenate(dus, axis=1)

        def step(j, _):
            jr, ji = j, nj + j
            ar = jnp.zeros((SUBLANES, SCAN_LANES), F32)
            ai = jnp.zeros((SUBLANES, SCAN_LANES), F32)
            for k in range(KSTEPS):
                kp = k + 1 if rev else k - 1
                rows = pl.ds(k * SUBLANES, SUBLANES)
                if 0 <= kp < KSTEPS:
                    prow = pl.ds(kp * SUBLANES, SUBLANES)
                    xr, xi = xs[jr, prow, :], xs[ji, prow, :]
                else:
                    xr, xi = xin[jr], xin[ji]
                lr, li = ls[jr, rows, :], ls[ji, rows, :]
                ar += lr * xr + li * xi
                ai += li * xr - lr * xi
            dlb_ref[jr] += ar
            dlb_ref[ji] += ai
            return 0

        lax.fori_loop(0, nj, step, 0)

        @pl.when(c == nc - 1)
        def _():
            for b in range(2 * nj):
                dlb_ref[b] = jnp.broadcast_to(jnp.sum(dlb_ref[b], axis=0, keepdims=True), (SUBLANES, SCAN_LANES))
            for g in range(w // SSM_GROUP):
                b, gl = divmod(g, cb // SSM_GROUP)
                rows = slice(gl * SSM_GROUP, (gl + 1) * SSM_GROUP)
                for part in range(2):
                    cols = slice(part * SSM_BLOCK + gl * SSM_STATE, part * SSM_BLOCK + (gl + 1) * SSM_STATE)
                    dbb_ref[part, g * SSM_GROUP:(g + 1) * SSM_GROUP, :] = dwb_ref[b, rows, cols]
                    dcc_ref[part, g * SSM_GROUP:(g + 1) * SSM_GROUP, :] = dwc_ref[b, rows, cols]

    st = (2 * nj, SUBLANES, SCAN_LANES)
    wshape = (nb, cb, 2 * SSM_BLOCK)
    return pl.pallas_call(
        body, name=name, grid=(nc,),
        in_specs=[pl.BlockSpec((CHUNK, w), lambda c: (cidx(c), 0)), pl.BlockSpec((CHUNK, w), lambda c: (cidx(c), 0)),
                  pl.BlockSpec((1,) + st, lambda c: (cidx(c), 0, 0, 0)), _full(wshape), _full(wshape), _ANY],
        out_specs=[pl.BlockSpec((CHUNK, w), lambda c: (cidx(c), 0)), _full((2, w, SSM_STATE)),
                   _full((2, w, SSM_STATE)), _full(st)],
        out_shape=[jax.ShapeDtypeStruct((lp, w), F32), jax.ShapeDtypeStruct((2, w, SSM_STATE), F32),
                   jax.ShapeDtypeStruct((2, w, SSM_STATE), F32), jax.ShapeDtypeStruct(st, F32)],
        scratch_shapes=[pltpu.VMEM((ntab,) + st, F32), pltpu.VMEM(wshape, F32), pltpu.VMEM(wshape, F32),
                        pltpu.VMEM((2 * nj, CHUNK, SCAN_LANES), F32), pltpu.VMEM((2 * nj, CHUNK, SCAN_LANES), F32),
                        pltpu.VMEM(st, F32), pltpu.VMEM(st, F32), pltpu.VMEM(st, F32), pltpu.VMEM(st, F32)],
        compiler_params=_params(("arbitrary",)),
    )(u, dy, ckpt, wb, wct, tab)


def _embed_blocks(t_re, t_im):
    g, p, n = t_re.shape
    gb = SSM_BLOCK // n
    eye = jnp.eye(gb, dtype=t_re.dtype)
    parts = [jnp.einsum('bgpn,gh->bgphn', t.reshape(g // gb, gb, p, n), eye).reshape(g // gb, gb * p, gb * n)
             for t in (t_re, t_im)]
    return jnp.concatenate(parts, axis=2)


def _scan_tables(pw_re, pw_im, hi_re, hi_im, rev):
    s = pw_re.shape[1] * pw_re.shape[2]
    nj = s // SCAN_LANES
    sub = np.arange(SUBLANES)
    live = np.ones((4 + KSTEPS, 1, SUBLANES, 1), bool)
    for row, dist in ((1, 1), (2, 2), (3, 4)):
        live[row, 0, :, 0] = (sub < SUBLANES - dist) if rev else (sub >= dist)

    def lay(pw, hi):
        rows = jnp.concatenate([pw[:1], pw[KSTEPS - 1:], hi, pw], axis=0).reshape(4 + KSTEPS, nj, 1, SCAN_LANES)
        return jnp.where(live, jnp.broadcast_to(rows, (4 + KSTEPS, nj, SUBLANES, SCAN_LANES)), 0.0)

    return jnp.concatenate([lay(pw_re, hi_re), lay(pw_im, hi_im)], axis=1)


def _adamw(w, g, m, v, tm):
    r, c = w.shape
    c1 = 1.0 - ADAM_B1 ** ADAM_STEP
    c2 = 1.0 - ADAM_B2 ** ADAM_STEP

    def body(w_ref, g_ref, m_ref, v_ref, d_ref, nm_ref, nv_ref):
        gg = g_ref[...]
        nm = ADAM_B1 * m_ref[...] + (1.0 - ADAM_B1) * gg
        nv = ADAM_B2 * v_ref[...] + (1.0 - ADAM_B2) * (gg * gg)
        nm_ref[...] = nm
        nv_ref[...] = nv
        d_ref[...] = -ADAM_LR * ((nm / c1) / (jnp.sqrt(nv / c2) + ADAM_EPS) + ADAM_WD * w_ref[...])

    spec = _row(tm, c)
    return pl.pallas_call(
        body, name="adamw", grid=(r // tm,), in_specs=[spec] * 4, out_specs=[spec] * 3,
        out_shape=[jax.ShapeDtypeStruct((r, c), F32)] * 3, compiler_params=_params(("parallel",)),
    )(w, g, m, v)


def _pair_sum(g42, got, core, out_dtype, tm, name):
    _, _, r, c = g42.shape

    def body(core_ref, a_ref, b_ref, o_ref):
        o_ref[...] = (a_ref[...] + b_ref[...]).astype(out_dtype)

    grid_spec = pltpu.PrefetchScalarGridSpec(
        num_scalar_prefetch=1, grid=(4, r // tm),
        in_specs=[pl.BlockSpec((1, None, tm, c), lambda s, i, core_ref: (s, core_ref[0], i, 0)),
                  pl.BlockSpec((1, tm, c), lambda s, i, core_ref: (s, i, 0))],
        out_specs=pl.BlockSpec((1, tm, c), lambda s, i, core_ref: (s, i, 0)))
    return pl.pallas_call(
        body, name=name, grid_spec=grid_spec, out_shape=jax.ShapeDtypeStruct((4, r, c), out_dtype),
        compiler_params=_params(("parallel", "parallel")),
    )(core, g42, got)


def _sum4(a, core, tm, name):
    _, r, c = a.shape

    def body(core_ref, a_ref, o_ref):
        o_ref[...] = ((a_ref[0].astype(F32) + a_ref[1].astype(F32)) + a_ref[2].astype(F32)) + a_ref[3].astype(F32)

    grid_spec = pltpu.PrefetchScalarGridSpec(
        num_scalar_prefetch=1, grid=(r // tm,),
        in_specs=[pl.BlockSpec((4, tm, c), lambda i, core_ref: (0, i, 0))],
        out_specs=pl.BlockSpec((None, tm, c), lambda i, core_ref: (core_ref[0], i, 0)))
    return pl.pallas_call(
        body, name=name, grid_spec=grid_spec, out_shape=jax.ShapeDtypeStruct((2, r, c), F32),
        compiler_params=_params(("parallel",)),
    )(core, a)


_ANY = pl.BlockSpec(memory_space=pl.ANY)


def _gather_phases(xs, outs, send_sems, recv_sems, local_sems):
    n = len(xs)

    def parts():
        x, y, c = lax.axis_index("x"), lax.axis_index("y"), lax.axis_index("c")
        return c, (x, y, c), (x, y, 1 - c), [(1 - x, y), (x, 1 - y), (1 - x, 1 - y)]

    def slot(t, px, py, pc):
        return outs[t].at[4 * px + 2 * py + pc]

    def copy(t, k, blk, to, src=None):
        return pltpu.make_async_remote_copy(
            src_ref=slot(t, *blk) if src is None else src, dst_ref=slot(t, *blk),
            send_sem=send_sems.at[t, k], recv_sem=recv_sems.at[t, k], device_id=to, device_id_type=MESH_ID)

    def own(t, me):
        return pltpu.make_async_copy(xs[t], slot(t, *me), local_sems.at[t])

    def first(t, c, me, sibling, chips):
        return [copy(t, 0, me, sibling, src=xs[t])] + [copy(t, 1 + j, me, (*chip, c), src=xs[t])
                                                       for j, chip in enumerate(chips)]

    def passed(t, c, sibling, chips):
        return [copy(t, 4 + j, (*chip, c), sibling) for j, chip in enumerate(chips)]

    def start():
        c, me, sibling, chips = parts()
        for t in range(n):
            own(t, me).start()
        for t in range(n):
            for cp in first(t, c, me, sibling, chips):
                cp.start()

    def forward():
        c, me, sibling, chips = parts()
        for j, chip in enumerate(chips):
            for t in range(n):
                copy(t, 1 + j, (*chip, c), me).wait_recv()
                passed(t, c, sibling, chips)[j].start()

    def finish():
        c, me, sibling, chips = parts()
        for t in range(n):
            copy(t, 0, sibling, me).wait_recv()
        for j, chip in enumerate(chips):
            for t in range(n):
                copy(t, 4 + j, (*chip, 1 - c), me).wait_recv()
        for t in range(n):
            for cp in first(t, c, me, sibling, chips) + passed(t, c, sibling, chips):
                cp.wait_send()
            own(t, me).wait()

    return start, forward, finish


def _all_gather8(blocks, name):
    n = len(blocks)

    def body(*refs):
        for phase in _gather_phases(refs[:n], refs[n:2 * n], *refs[2 * n:]):
            phase()

    return pl.pallas_call(
        body, name=name, out_shape=[jax.ShapeDtypeStruct((8,) + b.shape, b.dtype) for b in blocks],
        in_specs=[_ANY] * n, out_specs=[_ANY] * n,
        scratch_shapes=[pltpu.SemaphoreType.DMA((n, 7)), pltpu.SemaphoreType.DMA((n, 7)),
                        pltpu.SemaphoreType.DMA((n,))],
    )(*blocks)


def _pair_exchange(gs, name):
    n = len(gs)

    def body(*refs):
        g_refs, outs = refs[:n], refs[n:2 * n]
        send_sems, recv_sems = refs[2 * n:]
        x, y, c = lax.axis_index("x"), lax.axis_index("y"), lax.axis_index("c")
        cps = [pltpu.make_async_remote_copy(
            src_ref=g_refs[t].at[:, 1 - c], dst_ref=outs[t], send_sem=send_sems.at[t], recv_sem=recv_sems.at[t],
            device_id=(x, y, 1 - c), device_id_type=MESH_ID) for t in range(n)]
        for cp in cps:
            cp.start()
        for cp in cps:
            cp.wait()

    return pl.pallas_call(
        body, name=name,
        out_shape=[jax.ShapeDtypeStruct((g.shape[0],) + g.shape[2:], g.dtype) for g in gs],
        in_specs=[_ANY] * n, out_specs=[_ANY] * n,
        scratch_shapes=[pltpu.SemaphoreType.DMA((n,)), pltpu.SemaphoreType.DMA((n,))],
    )(*gs)


def _chip_scatter(ps, name):
    n = len(ps)

    def body(*refs):
        p_refs, outs = refs[:n], refs[n:2 * n]
        send_sems, recv_sems, local_sems = refs[2 * n:]
        x, y, c = lax.axis_index("x"), lax.axis_index("y"), lax.axis_index("c")
        mine = 2 * x + y
        chips = [(1 - x, y), (x, 1 - y), (1 - x, 1 - y)]
        own = [pltpu.make_async_copy(p_refs[t].at[mine], outs[t].at[mine], local_sems.at[t]) for t in range(n)]
        for cp in own:
            cp.start()

        def copy(t, k, src_slab, dst_slab, chip):
            return pltpu.make_async_remote_copy(
                src_ref=p_refs[t].at[src_slab], dst_ref=outs[t].at[dst_slab], send_sem=send_sems.at[t, k],
                recv_sem=recv_sems.at[t, k], device_id=(*chip, c), device_id_type=MESH_ID)

        sends = [[copy(t, k, 2 * cx + cy, mine, (cx, cy)) for k, (cx, cy) in enumerate(chips)] for t in range(n)]
        for k in range(3):
            for t in range(n):
                sends[t][k].start()
        for k, (cx, cy) in enumerate(chips):
            for t in range(n):
                copy(t, k, mine, 2 * cx + cy, (cx, cy)).wait_recv()
        for t in range(n):
            for cp in sends[t]:
                cp.wait_send()
        for cp in own:
            cp.wait()

    return pl.pallas_call(
        body, name=name, out_shape=[jax.ShapeDtypeStruct(p.shape, p.dtype) for p in ps],
        in_specs=[_ANY] * n, out_specs=[_ANY] * n,
        scratch_shapes=[pltpu.SemaphoreType.DMA((n, 3)), pltpu.SemaphoreType.DMA((n, 3)),
                        pltpu.SemaphoreType.DMA((n,))],
    )(*ps)


def _pair_gather(rs, name):
    n = len(rs)

    def body(*refs):
        ins, outs = refs[:n], refs[n:2 * n]
        send_sems, recv_sems = refs[2 * n:]
        x, y, c = lax.axis_index("x"), lax.axis_index("y"), lax.axis_index("c")

        def copy(t, slab):
            return pltpu.make_async_remote_copy(
                src_ref=ins[t].at[slab], dst_ref=outs[t].at[slab], send_sem=send_sems.at[t],
                recv_sem=recv_sems.at[t], device_id=(x, y, 1 - c), device_id_type=MESH_ID)

        sends = [copy(t, c) for t in range(n)]
        for cp in sends:
            cp.start()
        for t in range(n):
            copy(t, 1 - c).wait_recv()
        for cp in sends:
            cp.wait_send()

    return pl.pallas_call(
        body, name=name, out_shape=[jax.ShapeDtypeStruct(r.shape, r.dtype) for r in rs],
        in_specs=[_ANY] * n, out_specs=[_ANY] * n, input_output_aliases={t: t for t in range(n)},
        scratch_shapes=[pltpu.SemaphoreType.DMA((n,)), pltpu.SemaphoreType.DMA((n,))],
    )(*rs)


PACK_COLS = 1024
BIG = (("meta_tokens", 1), ("w_in", 1), ("w_glu", 0), ("w_ssm_proj", 1), ("w_attn_proj", 0), ("w_out", 0),
       ("w_mlp_in", 1), ("w_mlp_out", 0))
SMALL = ("norm_mix_g", "ssm_a_re", "ssm_a_im", "ssm_log_dt", "ssm_b_re", "ssm_b_im", "ssm_c_re", "ssm_c_im",
         "ssm_d", "b_glu", "q_norm_g", "k_norm_g", "norm_mlp_g", "norm_final_g")


def _pad_rows(flat, mult_rows):
    n = flat.shape[0]
    unit = PACK_COLS * mult_rows
    total = -(-n // unit) * unit
    return jnp.pad(flat, (0, total - n)).reshape(total // PACK_COLS, PACK_COLS)


def _half(t, c):
    return lax.dynamic_slice_in_dim(t, c * (t.shape[0] // 2), t.shape[0] // 2, 0)


EARLY_WEIGHTS = ("meta_tokens", "w_in", "w_glu")
LATE_WEIGHTS = tuple(name for name, _ in BIG if name not in EARLY_WEIGHTS)


def _weight_blocks(shards, c, names):
    return [_half(shards[name], c) if name == "meta_tokens" else _half(shards[name], c).astype(BF16) for name in names]


def _shard_major(names, gathered):
    return {name: g.reshape((4, 2 * g.shape[1]) + g.shape[2:]) for name, g in zip(names, gathered)}


def _reduce_gradients(big4, small_flat, c):
    names = [name for name, _ in BIG]
    n_small = small_flat.shape[0]
    unit = 8 * SUBLANES * PACK_COLS
    k = -(-n_small // unit) * unit
    small42 = jnp.pad(small_flat, (0, k - n_small)).reshape(4, 2, k // (8 * PACK_COLS), PACK_COLS)
    g42 = [big4[name].reshape(4, 2, big4[name].shape[1] // 2, big4[name].shape[2]) for name in names] + [small42]
    labels = names + ["small"]
    wire = [F32 if name == "meta_tokens" else BF16 for name in names] + [F32]
    tiles = [_pick_tile(g.shape[2], 256, SUBLANES if dt == F32 else 2 * SUBLANES) for g, dt in zip(g42, wire)]
    core = c.astype(jnp.int32).reshape(1)
    got = _pair_exchange(g42, "grad_pair_exchange")
    pair = [_pair_sum(g, o, core, dt, tm, "pair_sum_" + lb) for g, o, dt, tm, lb in zip(g42, got, wire, tiles, labels)]
    by_src = _chip_scatter(pair, "grad_chip_scatter")
    red = [_sum4(b, core, tm, "chip_sum_" + lb) for b, tm, lb in zip(by_src, tiles, labels)]
    both = _pair_gather(red[:-1], "grad_pair_gather")
    out = {name: b.reshape(2 * b.shape[1], b.shape[2]) for name, b in zip(names, both)}
    small_piece = lax.dynamic_index_in_dim(red[-1], c, 0, keepdims=False)
    small = _all_gather8([small_piece], "small_grad_all_gather")[0].reshape(-1)[:n_small]
    return out, small


def _to_chunk_order(a):
    lp = a.shape[0]
    rest = a.shape[1:]
    a = a.reshape((lp // CHUNK, SUBLANES, KSTEPS) + rest)
    return a.swapaxes(1, 2).reshape((lp,) + rest)


def _from_chunk_order(a):
    lp = a.shape[0]
    rest = a.shape[1:]
    a = a.reshape((lp // CHUNK, KSTEPS, SUBLANES) + rest)
    return a.swapaxes(1, 2).reshape((lp,) + rest)


def _rope_tables(l_total, lp):
    n_real = l_total - N_META
    pos = np.arange(n_real)
    row_id = (pos // GRID_W).astype(np.float32)
    col_id = (pos % GRID_W).astype(np.float32)
    ppa = HEAD_DIM // 4
    inv_freq = (ROPE_THETA ** (-np.arange(ppa, dtype=np.float64) / ppa)).astype(np.float32)
    ang = np.concatenate([row_id[:, None] * inv_freq, col_id[:, None] * inv_freq], axis=-1)
    ang = np.concatenate([np.zeros((N_META, HEAD_DIM // 2), np.float32), ang,
                          np.zeros((lp - l_total, HEAD_DIM // 2), np.float32)], axis=0).astype(np.float64)
    cos = np.repeat(np.cos(ang), 2, axis=1)
    sin = np.repeat(np.sin(ang), 2, axis=1) * np.tile(np.asarray([-1.0, 1.0]), HEAD_DIM // 2)
    reps = (1, LANES // HEAD_DIM)
    return np.tile(cos, reps).astype(np.float32), np.tile(sin, reps).astype(np.float32)


def kernel(x, meta_tokens, norm_mix_g, w_in, ssm_a_re, ssm_a_im, ssm_log_dt, ssm_b_re, ssm_b_im, ssm_c_re, ssm_c_im, ssm_d, w_glu, b_glu, q_norm_g, k_norm_g, w_ssm_proj, w_attn_proj, w_out, norm_mlp_g, w_mlp_in, w_mlp_out, norm_final_g, loss_target, m_meta_tokens, m_norm_mix_g, m_w_in, m_ssm_a_re, m_ssm_a_im, m_ssm_log_dt, m_ssm_b_re, m_ssm_b_im, m_ssm_c_re, m_ssm_c_im, m_ssm_d, m_w_glu, m_b_glu, m_q_norm_g, m_k_norm_g, m_w_ssm_proj, m_w_attn_proj, m_w_out, m_norm_mlp_g, m_w_mlp_in, m_w_mlp_out, m_norm_final_g, v_meta_tokens, v_norm_mix_g, v_w_in, v_ssm_a_re, v_ssm_a_im, v_ssm_log_dt, v_ssm_b_re, v_ssm_b_im, v_ssm_c_re, v_ssm_c_im, v_ssm_d, v_w_glu, v_b_glu, v_q_norm_g, v_k_norm_g, v_w_ssm_proj, v_w_attn_proj, v_w_out, v_norm_mlp_g, v_w_mlp_in, v_w_mlp_out, v_norm_final_g):
    args = dict(locals())
    names = list(dict.fromkeys([n for n, _ in BIG] + list(SMALL)))
    order = ['meta_tokens', 'norm_mix_g', 'w_in', 'ssm_a_re', 'ssm_a_im', 'ssm_log_dt', 'ssm_b_re', 'ssm_b_im',
             'ssm_c_re', 'ssm_c_im', 'ssm_d', 'w_glu', 'b_glu', 'q_norm_g', 'k_norm_g', 'w_ssm_proj', 'w_attn_proj',
             'w_out', 'norm_mlp_g', 'w_mlp_in', 'w_mlp_out', 'norm_final_g']
    assert sorted(names) == sorted(order)
    c_idx = lax.axis_index("c")

    seq, d = x.shape[1], x.shape[2]
    l_total = seq + N_META
    lp = -(-l_total // SEQ_ALIGN) * SEQ_ALIGN
    hd = d // 2
    n_groups = hd // SSM_GROUP
    n_state = n_groups * SSM_STATE
    nj = n_state // SCAN_LANES
    kvh = d // HEAD_DIM // GQA_REP

    shard2d = {}
    for name, _ in BIG:
        t = args[name]
        shard2d[name] = t.reshape(t.shape[-2], t.shape[-1])
    full = _shard_major(EARLY_WEIGHTS, _all_gather8(_weight_blocks(shard2d, c_idx, EARLY_WEIGHTS), "weight_all_gather"))
    meta_full = jnp.transpose(full["meta_tokens"], (1, 0, 2)).reshape(N_META, d)
    w_in4 = full["w_in"]
    w_glu_f = full["w_glu"].reshape(hd, hd)

    xin = jnp.concatenate([meta_full, x[0], jnp.zeros((lp - l_total, d), F32)], axis=0)
    xin = _to_chunk_order(xin)
    tgt = _to_chunk_order(jnp.pad(loss_target[0], ((N_META, lp - l_total), (0, 0))))
    pos = np.arange(lp)
    rowmask = jnp.asarray(_to_chunk_order(((pos >= N_META) & (pos < l_total)).astype(np.float32)[:, None]))
    kbias = jnp.asarray(_to_chunk_order(np.where(pos < l_total, 0.0, MASK_VALUE).astype(np.float32)[:, None])
                        .reshape(1, lp))
    cos_t, sin_t = (jnp.asarray(_to_chunk_order(t)) for t in _rope_tables(l_total, lp))
    mean_m, sel = _head_tables(d)

    tm = _pick_tile(lp, 320)
    tm_mid = _pick_tile(lp, 384)
    tm_big = _pick_tile(lp, 640)
    tq = _pick_tile(lp, ATTN_Q_TILE, LANES)
    tk = _pick_tile(lp, ATTN_K_TILE, MXU_DIM)
    assert lp - tk <= (l_total // CHUNK) * CHUNK
    g_mix = norm_mix_g.reshape(1, d)
    g_mlp = norm_mlp_g.reshape(1, d)
    g_fin = norm_final_g.reshape(1, d)
    qg = jnp.tile(q_norm_g.reshape(1, HEAD_DIM), (1, LANES // HEAD_DIM))
    kg = jnp.tile(k_norm_g.reshape(1, HEAD_DIM), (1, LANES // HEAD_DIM))
    dskip = ssm_d.reshape(1, hd)
    bglu = b_glu.reshape(1, hd)

    a_re, a_im = ssm_a_re[0], ssm_a_im[0]
    log_dt = ssm_log_dt[0][..., None]
    bt_re = jnp.swapaxes(ssm_b_re[0], 2, 3)
    bt_im = jnp.swapaxes(ssm_b_im[0], 2, 3)
    bb_re, bb_im, pw_re, pw_im, hi_re, hi_im = _ssm_discretize(a_re, a_im, log_dt, bt_re, bt_im)
    wb = [_embed_blocks(bb_re[i], bb_im[i]).astype(BF16) for i in range(2)]
    wct = [_embed_blocks(ssm_c_re[0, i], -ssm_c_im[0, i]).astype(BF16) for i in range(2)]
    tabs = [_scan_tables(pw_re[i], pw_im[i], hi_re[i], hi_im[i], rev=(i == 1)) for i in range(2)]
    tabs_adj = [_scan_tables(pw_re[i], pw_im[i], hi_re[i], hi_im[i], rev=(i == 0)) for i in range(2)]

    u, qkv, gates = _in_proj(xin, g_mix, w_in4, tm_mid)
    y0, ck0 = _ssm_fwd(u, wb[0], wct[0], tabs[0], False, "ssm_fwd_0")
    y1, ck1 = _ssm_fwd(u, wb[1], wct[1], tabs[1], True, "ssm_fwd_1")
    yssm = _glu_fwd(u, y0, y1, dskip, w_glu_f, bglu, tm_big)
    q, k, v = _qk_prep(qkv, cos_t, sin_t, qg, kg, mean_m, tm)
    o, lse, late = _attn_fwd(q, k, v, kbias, tq, tk, gather=_weight_blocks(shard2d, c_idx, LATE_WEIGHTS))
    full = _shard_major(LATE_WEIGHTS, late)
    w_mlp_in4 = full["w_mlp_in"]
    w_ssm_proj4 = full["w_ssm_proj"]
    w_attn_proj_f = full["w_attn_proj"].reshape(d, d)
    w_out_f = full["w_out"].reshape(d, d)
    w_mlp_out_f = full["w_mlp_out"].reshape(4 * d, d)
    h1, merged = _merge_fwd(yssm, o, gates, xin, w_ssm_proj4, w_attn_proj_f, w_out_f, tm_mid)
    r = _mlp_in(h1, g_mlp, w_mlp_in4, tm_mid)
    h3 = _mlp_out(h1, r, w_mlp_out_f, tm_mid)
    loss_tile, dh3, d_gfin = _final_loss(h3, g_fin, tgt, rowmask, tm_big)

    dz, dh3b = _mlp_bwd_a(dh3, r, w_mlp_out_f, tm_mid)
    dh1, d_gmlp = _mlp_bwd_b(dz, dh3, h1, g_mlp, w_mlp_in4, tm_mid)
    dgates, dms, dma, dyssm, do, delta, dh1b = _merge_bwd(dh1, yssm, o, gates, w_ssm_proj4, w_attn_proj_f, w_out_f,
                                                          sel, tm)
    dyv, d_wglu, d_bglu, d_dskip = _glu_bwd(dyssm, u, y0, y1, dskip, w_glu_f, bglu, tm_big)
    du0, dbb0, dcc0, dlb0 = _ssm_bwd(u, dyv, ck0, wb[0], wct[0], _both(tabs[0], tabs_adj[0]), False, "ssm_bwd_0")
    du1, dbb1, dcc1, dlb1 = _ssm_bwd(u, dyv, ck1, wb[1], wct[1], _both(tabs[1], tabs_adj[1]), True, "ssm_bwd_1")
    dq, dk, dv = _attn_bwd(q, k, v, kbias, do, lse, delta, tq, tk)
    dqkv, d_qg, d_kg = _qk_bwd(qkv, dq, dk, dv, cos_t, sin_t, qg, kg, mean_m, tm)
    dxin, d_gmix, dproj = _in_proj_bwd(dyv, du0, du1, dskip, dqkv, dgates, dh1, xin, g_mix, w_in4, tm)

    tn = min(d, 1024)
    tm_w = _pick_tile(lp, 3 * MXU_DIM, MXU_DIM)
    grads4 = {
        "w_in": _wgrad(xin, dproj, 4, tm_w, tn, "wgrad_in", gain=g_mix),
        "w_mlp_in": _wgrad(h1, dz, 4, tm_w, tn, "wgrad_mlp_in", gain=g_mlp),
        "w_mlp_out": _wgrad(r, dh3b, 1, tm_w, min(d, 256), "wgrad_mlp_out", square=True).reshape(4, d, d),
        "w_out": _wgrad(merged, dh1b, 1, tm_w, tn, "wgrad_out").reshape(4, d // 4, d),
        "w_attn_proj": _wgrad(o, dma, 1, tm_w, tn, "wgrad_attn_proj").reshape(4, d // 4, d),
        "w_ssm_proj": _wgrad(yssm, dms, 4, tm_w, d // 4, "wgrad_ssm_proj"),
        "w_glu": d_wglu.reshape(4, hd // 4, hd),
    }
    dx_nat = _from_chunk_order(dxin)
    grads4["meta_tokens"] = jnp.swapaxes(dx_nat[:N_META].reshape(N_META, 4, d // 4), 0, 1)
    grad_x = dx_nat[N_META:l_total][None]

    dlb = jnp.stack([dlb0, dlb1])[:, :, 0, :]
    dlb_re = dlb[:, :nj].reshape(2, n_groups, SSM_STATE)
    dlb_im = dlb[:, nj:].reshape(2, n_groups, SSM_STATE)
    gpn = (2, 2, n_groups, SSM_GROUP, SSM_STATE)
    dbb = jnp.stack([dbb0, dbb1]).reshape(gpn)
    dcc = jnp.stack([dcc0, dcc1]).reshape(gpn)
    d_are, d_aim, d_logdt, d_btre, d_btim = _ssm_param_bwd(a_re, a_im, log_dt, bt_re, bt_im, dlb_re, dlb_im,
                                                           dbb[:, 0], dbb[:, 1])
    small_grads = {
        "norm_mix_g": d_gmix, "ssm_a_re": d_are, "ssm_a_im": d_aim, "ssm_log_dt": d_logdt,
        "ssm_b_re": jnp.swapaxes(d_btre, 2, 3), "ssm_b_im": jnp.swapaxes(d_btim, 2, 3),
        "ssm_c_re": dcc[:, 0], "ssm_c_im": -dcc[:, 1],
        "ssm_d": d_dskip, "b_glu": d_bglu, "q_norm_g": d_qg[:, :HEAD_DIM], "k_norm_g": d_kg[:, :HEAD_DIM],
        "norm_mlp_g": d_gmlp, "norm_final_g": d_gfin,
    }
    small_flat = jnp.concatenate([small_grads[n].reshape(-1) for n in SMALL] + [loss_tile[0, :1]])

    red_big, red_small = _reduce_gradients(grads4, small_flat, c_idx)
    loss, red_small = red_small[-1], red_small[:-1]
    grad, delta_w, new_m, new_v = {}, {}, {}, {}
    for name, _ in BIG:
        w2 = shard2d[name]
        shp = args[name].shape
        g2 = red_big[name]
        t = _pick_tile(w2.shape[0], 256, 8)
        dl, nm, nv = _adamw(w2, g2, args["m_" + name].reshape(w2.shape), args["v_" + name].reshape(w2.shape), t)
        grad[name], delta_w[name], new_m[name], new_v[name] = (a.reshape(shp) for a in (g2, dl, nm, nv))

    def pack_small(prefix):
        flat = jnp.concatenate([args[prefix + n].reshape(-1) for n in SMALL])
        return _pad_rows(flat, SUBLANES)

    n_small = red_small.shape[0]
    gs = _pad_rows(red_small, SUBLANES)
    dl, nm, nv = _adamw(pack_small(""), gs, pack_small("m_"), pack_small("v_"), _pick_tile(gs.shape[0], 256, 8))
    off = 0
    for name in SMALL:
        shp = args[name].shape
        k = int(np.prod(shp))
        for dst, src in ((grad, gs), (delta_w, dl), (new_m, nm), (new_v, nv)):
            dst[name] = src.reshape(-1)[off:off + k].reshape(shp)
        off += k
    assert off == n_small

    return (loss, grad_x, *[grad[n] for n in order], *[delta_w[n] for n in order],
            *[new_m[n] for n in order], *[new_v[n] for n in order])


def _both(tab, tab_adj):
    return jnp.concatenate([tab, tab_adj], axis=0)
```

```python
import functools
import math

import numpy as np
import jax
import jax.numpy as jnp
from jax import lax
from jax.experimental import pallas as pl
from jax.experimental.pallas import tpu as pltpu

F32 = jnp.float32
BF16 = jnp.bfloat16

N_META = 16
GRID_W = 64
HEAD_DIM = 64
GQA_REP = 4
SSM_GROUP = 16
SSM_STATE = 64
ROPE_THETA = 10000.0
NORM_EPS = 1e-6
EIG_RE_MAX = -1e-4
ADAM_LR, ADAM_B1, ADAM_B2, ADAM_EPS, ADAM_WD, ADAM_STEP = 0.001, 0.9, 0.999, 1e-08, 0.01, 10

SUBLANES = 8
LANES = 128
CHUNK = 256
KSTEPS = CHUNK // SUBLANES
SCAN_LANES = 512
MXU_DIM = 256
SSM_BLOCK = MXU_DIM
SEQ_ALIGN = MXU_DIM
ATTN_Q_TILE = 384
ATTN_K_TILE = 11 * MXU_DIM
VMEM_LIMIT = 56 << 20
MASK_VALUE = -1e30
MESH_ID = pl.DeviceIdType.MESH


def _dot(a, b):
    return jnp.dot(a, b, preferred_element_type=F32)


def _dot_nt(a, b):
    return lax.dot_general(a, b, (((1,), (1,)), ((), ())), preferred_element_type=F32)


def _dot_tn(a, b):
    return lax.dot_general(a, b, (((0,), (0,)), ((), ())), preferred_element_type=F32)


def _row(tm, width):
    return pl.BlockSpec((tm, width), lambda i: (i, 0))


def _full(shape):
    nd = len(shape)
    return pl.BlockSpec(shape, lambda i: (0,) * nd)


def _params(sem):
    return pltpu.CompilerParams(dimension_semantics=sem, vmem_limit_bytes=VMEM_LIMIT)


def _pick_tile(n, cap, mult=16):
    best = None
    for t in range(mult, min(n, cap) + 1, mult):
        if n % t == 0:
            best = t
    assert best is not None, (n, cap)
    return best


def _rstd(x):
    return lax.rsqrt(jnp.mean(x * x, axis=-1, keepdims=True) + NORM_EPS)


def _rms(x, g):
    return x * _rstd(x) * g


def _rms_bwd(dy, x, g):
    r = _rstd(x)
    xh = x * r
    gdy = dy * g
    dx = r * (gdy - xh * jnp.mean(gdy * xh, axis=-1, keepdims=True))
    return dx, dy * xh


def _split_dot(x, m):
    hi = x.astype(BF16)
    lo = (x - hi.astype(F32)).astype(BF16)
    return _dot(hi, m) + _dot(lo, m)


def _sigmoid(x):
    return 1.0 / (1.0 + jnp.exp(-x))


def _acc_rows(ref, val, first):
    s = jnp.sum(val, axis=0, keepdims=True)

    @pl.when(first)
    def _():
        ref[...] = s

    @pl.when(jnp.logical_not(first))
    def _():
        ref[...] += s


def _in_proj(xin, g, w4, tm):
    lp, d = xin.shape
    hd = d // 2

    def body(x_ref, g_ref, w_ref, u_ref, qkv_ref, gt_ref):
        h = _rms(x_ref[...], g_ref[...]).astype(BF16)
        p0 = _dot(h, w_ref[0])
        u_ref[...] = p0[:, :hd]
        qkv_ref[:, :hd] = p0[:, hd:]
        qkv_ref[:, hd:] = _dot(h, w_ref[1])
        gt_ref[:, :d] = _dot(h, w_ref[2])
        gt_ref[:, d:] = _dot(h, w_ref[3])

    return pl.pallas_call(
        body, name="in_proj", grid=(lp // tm,),
        in_specs=[_row(tm, d), _full((1, d)), _full((4, d, d))],
        out_specs=[_row(tm, hd), _row(tm, 3 * hd), _row(tm, 2 * d)],
        out_shape=[jax.ShapeDtypeStruct((lp, hd), F32), jax.ShapeDtypeStruct((lp, 3 * hd), F32),
                   jax.ShapeDtypeStruct((lp, 2 * d), F32)],
        compiler_params=_params(("parallel",)),
    )(xin, g, w4)


def _gelu(y):
    return 0.5 * y * (1.0 + lax.erf(y * (1.0 / math.sqrt(2.0))))


def _gelu_grad(y):
    return 0.5 * (1.0 + lax.erf(y * (1.0 / math.sqrt(2.0)))) + y * jnp.exp(-0.5 * y * y) * (1.0 / math.sqrt(2.0 * math.pi))


def _glu_fwd(u, y0, y1, dskip, w_glu, b_glu, tm):
    lp, w = u.shape

    def body(u_ref, y0_ref, y1_ref, d_ref, w_ref, b_ref, o_ref):
        y = u_ref[...] * d_ref[...] + y0_ref[...] + y1_ref[...]
        z = _gelu(y)
        t = _dot(z.astype(BF16), w_ref[...]) + b_ref[...]
        o_ref[...] = (z * _sigmoid(t)).astype(BF16)

    return pl.pallas_call(
        body, name="glu_fwd", grid=(lp // tm,),
        in_specs=[_row(tm, w), _row(tm, w), _row(tm, w), _full((1, w)), _full((w, w)), _full((1, w))],
        out_specs=_row(tm, w), out_shape=jax.ShapeDtypeStruct((lp, w), BF16),
        compiler_params=_params(("parallel",)),
    )(u, y0, y1, dskip, w_glu, b_glu)


def _glu_bwd(dyssm, u, y0, y1, dskip, w_glu, b_glu, tm):
    lp, w = u.shape

    def body(g_ref, u_ref, y0_ref, y1_ref, d_ref, w_ref, b_ref, dy_ref, dw_ref, db_ref, dd_ref):
        first = pl.program_id(0) == 0
        uu = u_ref[...]
        y = uu * d_ref[...] + y0_ref[...] + y1_ref[...]
        z = _gelu(y)
        zb = z.astype(BF16)
        sg = _sigmoid(_dot(zb, w_ref[...]) + b_ref[...])
        g = g_ref[...]
        dt = g * z * sg * (1.0 - sg)
        dtb = dt.astype(BF16)
        dz = g * sg + _dot_nt(dtb, w_ref[...])
        dy = dz * _gelu_grad(y)
        dy_ref[...] = dy
        dw = _dot_tn(zb, dtb)

        @pl.when(first)
        def _():
            dw_ref[...] = dw

        @pl.when(jnp.logical_not(first))
        def _():
            dw_ref[...] += dw

        _acc_rows(db_ref, dt, first)
        _acc_rows(dd_ref, dy * uu, first)

    return pl.pallas_call(
        body, name="glu_bwd", grid=(lp // tm,),
        in_specs=[_row(tm, w), _row(tm, w), _row(tm, w), _row(tm, w), _full((1, w)), _full((w, w)), _full((1, w))],
        out_specs=[_row(tm, w), _full((w, w)), _full((1, w)), _full((1, w))],
        out_shape=[jax.ShapeDtypeStruct((lp, w), F32), jax.ShapeDtypeStruct((w, w), F32),
                   jax.ShapeDtypeStruct((1, w), F32), jax.ShapeDtypeStruct((1, w), F32)],
        compiler_params=_params(("arbitrary",)),
    )(dyssm, u, y0, y1, dskip, w_glu, b_glu)


def _merge_fwd(yssm, o, gates, xin, wsp4, wap, wo, tm):
    lp, d = xin.shape
    w = yssm.shape[1]
    ns = d // 4

    def body(y_ref, o_ref, g_ref, x_ref, wsp_ref, wap_ref, wo_ref, h_ref, m_ref):
        yb = y_ref[...]
        ms = jnp.concatenate([_dot(yb, wsp_ref[s]) for s in range(4)], axis=1)
        ma = _dot(o_ref[...], wap_ref[...])
        merged = (_sigmoid(g_ref[:, :d]) * ms + _sigmoid(g_ref[:, d:]) * ma).astype(BF16)
        m_ref[...] = merged
        h_ref[...] = x_ref[...] + _dot(merged, wo_ref[...])

    return pl.pallas_call(
        body, name="merge_fwd", grid=(lp // tm,),
        in_specs=[_row(tm, w), _row(tm, d), _row(tm, 2 * d), _row(tm, d),
                  _full((4, w, ns)), _full((d, d)), _full((d, d))],
        out_specs=[_row(tm, d), _row(tm, d)],
        out_shape=[jax.ShapeDtypeStruct((lp, d), F32), jax.ShapeDtypeStruct((lp, d), BF16)],
        compiler_params=_params(("parallel",)),
    )(yssm, o, gates, xin, wsp4, wap, wo)


def _merge_bwd(dh1, yssm, o, gates, wsp4, wap, wo, sel, tm):
    lp, d = dh1.shape
    w = yssm.shape[1]
    ns = d // 4
    nsel = sel.shape[1]

    def body(dh_ref, y_ref, o_ref, g_ref, wsp_ref, wap_ref, wo_ref, sel_ref,
             dg_ref, dms_ref, dma_ref, dy_ref, do_ref, dl_ref, dhb_ref):
        dhb = dh_ref[...].astype(BF16)
        dhb_ref[...] = dhb
        dm = _dot_nt(dhb, wo_ref[...])
        yb = y_ref[...]
        ob = o_ref[...]
        ms = jnp.concatenate([_dot(yb, wsp_ref[s]) for s in range(4)], axis=1)
        ma = _dot(ob, wap_ref[...])
        ss = _sigmoid(g_ref[:, :d])
        sa = _sigmoid(g_ref[:, d:])
        dg_ref[:, :d] = dm * ms * ss * (1.0 - ss)
        dg_ref[:, d:] = dm * ma * sa * (1.0 - sa)
        dms = (dm * ss).astype(BF16)
        dma = (dm * sa).astype(BF16)
        dms_ref[...] = dms
        dma_ref[...] = dma
        dy = _dot_nt(dms[:, :ns], wsp_ref[0])
        for s in range(1, 4):
            dy += _dot_nt(dms[:, s * ns:(s + 1) * ns], wsp_ref[s])
        dy_ref[...] = dy
        do = _dot_nt(dma, wap_ref[...])
        do_ref[...] = do.astype(BF16)
        dl_ref[...] = _split_dot(do * ob.astype(F32), sel_ref[...])

    return pl.pallas_call(
        body, name="merge_bwd", grid=(lp // tm,),
        in_specs=[_row(tm, d), _row(tm, w), _row(tm, d), _row(tm, 2 * d),
                  _full((4, w, ns)), _full((d, d)), _full((d, d)), _full((d, nsel))],
        out_specs=[_row(tm, 2 * d), _row(tm, d), _row(tm, d), _row(tm, w), _row(tm, d), _row(tm, nsel), _row(tm, d)],
        out_shape=[jax.ShapeDtypeStruct((lp, 2 * d), F32), jax.ShapeDtypeStruct((lp, d), BF16),
                   jax.ShapeDtypeStruct((lp, d), BF16), jax.ShapeDtypeStruct((lp, w), F32),
                   jax.ShapeDtypeStruct((lp, d), BF16), jax.ShapeDtypeStruct((lp, nsel), F32),
                   jax.ShapeDtypeStruct((lp, d), BF16)],
        compiler_params=_params(("parallel",)),
    )(dh1, yssm, o, gates, wsp4, wap, wo, sel)


def _mlp_in(h1, g, w4, tm):
    lp, d = h1.shape

    def body(x_ref, g_ref, w_ref, r_ref):
        h = _rms(x_ref[...], g_ref[...]).astype(BF16)
        for s in range(4):
            r_ref[:, s * d:(s + 1) * d] = jnp.maximum(_dot(h, w_ref[s]), 0.0).astype(BF16)

    return pl.pallas_call(
        body, name="mlp_in", grid=(lp // tm,),
        in_specs=[_row(tm, d), _full((1, d)), _full((4, d, d))],
        out_specs=_row(tm, 4 * d), out_shape=jax.ShapeDtypeStruct((lp, 4 * d), BF16),
        compiler_params=_params(("parallel",)),
    )(h1, g, w4)


def _square_bf16(r):
    rf = r.astype(F32)
    return (rf * rf).astype(BF16)


def _mlp_out(h1, r, w2, tm):
    lp, d = h1.shape
    ff = r.shape[1]

    def body(x_ref, r_ref, w_ref, o_ref):
        o_ref[...] = x_ref[...] + _dot(_square_bf16(r_ref[...]), w_ref[...])

    return pl.pallas_call(
        body, name="mlp_out", grid=(lp // tm,),
        in_specs=[_row(tm, d), _row(tm, ff), _full((ff, d))],
        out_specs=_row(tm, d), out_shape=jax.ShapeDtypeStruct((lp, d), F32),
        compiler_params=_params(("parallel",)),
    )(h1, r, w2)


def _final_loss(h3, g, tgt, rowmask, tm):
    lp, d = h3.shape

    def body(x_ref, g_ref, t_ref, m_ref, loss_ref, dx_ref, dg_ref):
        first = pl.program_id(0) == 0
        x = x_ref[...]
        gg = g_ref[...]
        err = (_rms(x, gg) - t_ref[...]) * m_ref[...]
        part = 0.5 * jnp.sum(jnp.sum(err * err, axis=1, keepdims=True), axis=0, keepdims=True) * (1.0 / d)
        part = jnp.broadcast_to(part, (SUBLANES, LANES))

        @pl.when(first)
        def _():
            loss_ref[...] = part

        @pl.when(jnp.logical_not(first))
        def _():
            loss_ref[...] += part

        dx, dgr = _rms_bwd(err * (1.0 / d), x, gg)
        dx_ref[...] = dx
        _acc_rows(dg_ref, dgr, first)

    return pl.pallas_call(
        body, name="final_loss", grid=(lp // tm,),
        in_specs=[_row(tm, d), _full((1, d)), _row(tm, d), _row(tm, 1)],
        out_specs=[_full((SUBLANES, LANES)), _row(tm, d), _full((1, d))],
        out_shape=[jax.ShapeDtypeStruct((SUBLANES, LANES), F32), jax.ShapeDtypeStruct((lp, d), F32),
                   jax.ShapeDtypeStruct((1, d), F32)],
        compiler_params=_params(("arbitrary",)),
    )(h3, g, tgt, rowmask)


def _mlp_bwd_a(dh3, r, w2, tm):
    lp, d = dh3.shape
    ff = r.shape[1]

    def body(dh_ref, r_ref, w_ref, dz_ref, dhb_ref):
        dhb = dh_ref[...].astype(BF16)
        dhb_ref[...] = dhb
        da = _dot_nt(dhb, w_ref[...])
        dz_ref[...] = (da * (2.0 * r_ref[...].astype(F32))).astype(BF16)

    return pl.pallas_call(
        body, name="mlp_bwd_a", grid=(lp // tm,),
        in_specs=[_row(tm, d), _row(tm, ff), _full((ff, d))],
        out_specs=[_row(tm, ff), _row(tm, d)],
        out_shape=[jax.ShapeDtypeStruct((lp, ff), BF16), jax.ShapeDtypeStruct((lp, d), BF16)],
        compiler_params=_params(("parallel",)),
    )(dh3, r, w2)


def _mlp_bwd_b(dz, dh3, h1, g, w4, tm):
    lp, d = h1.shape

    def body(dz_ref, dh_ref, x_ref, g_ref, w_ref, dx_ref, dg_ref):
        first = pl.program_id(0) == 0
        dh2 = _dot_nt(dz_ref[:, :d], w_ref[0])
        for s in range(1, 4):
            dh2 += _dot_nt(dz_ref[:, s * d:(s + 1) * d], w_ref[s])
        dx, dgr = _rms_bwd(dh2, x_ref[...], g_ref[...])
        dx_ref[...] = dh_ref[...] + dx
        _acc_rows(dg_ref, dgr, first)

    return pl.pallas_call(
        body, name="mlp_bwd_b", grid=(lp // tm,),
        in_specs=[_row(tm, 4 * d), _row(tm, d), _row(tm, d), _full((1, d)), _full((4, d, d))],
        out_specs=[_row(tm, d), _full((1, d))],
        out_shape=[jax.ShapeDtypeStruct((lp, d), F32), jax.ShapeDtypeStruct((1, d), F32)],
        compiler_params=_params(("arbitrary",)),
    )(dz, dh3, h1, g, w4)


def _in_proj_bwd(dyv, du0, du1, dskip, dqkv, dgates, dres, xin, g, w4, tm):
    lp, d = xin.shape
    hd = d // 2

    def body(dy_ref, a_ref, b_ref, ds_ref, dq_ref, dgt_ref, dr_ref, x_ref, g_ref, w_ref, dx_ref, dg_ref, dp_ref):
        first = pl.program_id(0) == 0
        du = (dy_ref[...] * ds_ref[...] + a_ref[...] + b_ref[...]).astype(BF16)
        dq = dq_ref[...].astype(BF16)
        dgt = dgt_ref[...].astype(BF16)
        dp_ref[:, :hd] = du
        dp_ref[:, hd:2 * d] = dq
        dp_ref[:, 2 * d:] = dgt
        dh = _dot_nt(du, w_ref[0, :, :hd]) + _dot_nt(dq[:, :hd], w_ref[0, :, hd:])
        dh += _dot_nt(dq[:, hd:], w_ref[1])
        dh += _dot_nt(dgt[:, :d], w_ref[2]) + _dot_nt(dgt[:, d:], w_ref[3])
        dx, dgr = _rms_bwd(dh, x_ref[...], g_ref[...])
        dx_ref[...] = dr_ref[...] + dx
        _acc_rows(dg_ref, dgr, first)

    return pl.pallas_call(
        body, name="in_proj_bwd", grid=(lp // tm,),
        in_specs=[_row(tm, hd), _row(tm, hd), _row(tm, hd), _full((1, hd)), _row(tm, 3 * hd), _row(tm, 2 * d),
                  _row(tm, d), _row(tm, d), _full((1, d)), _full((4, d, d))],
        out_specs=[_row(tm, d), _full((1, d)), _row(tm, 4 * d)],
        out_shape=[jax.ShapeDtypeStruct((lp, d), F32), jax.ShapeDtypeStruct((1, d), F32),
                   jax.ShapeDtypeStruct((lp, 4 * d), BF16)],
        compiler_params=_params(("arbitrary",)),
    )(dyv, du0, du1, dskip, dqkv, dgates, dres, xin, g, w4)


def _wgrad(a, dy, nshard, tm, tn, name, gain=None, square=False):
    lp, k = a.shape
    n = dy.shape[1]
    ns = n // nshard
    assert ns % tn == 0
    per = ns // tn

    def body(*refs):
        if gain is not None:
            a_ref, g_ref, dy_ref, o_ref = refs
            at = _rms(a_ref[...], g_ref[...]).astype(BF16)
        else:
            a_ref, dy_ref, o_ref = refs
            at = _square_bf16(a_ref[...]) if square else a_ref[...]
        i = pl.program_id(1)
        acc = _dot_tn(at, dy_ref[...])

        @pl.when(i == 0)
        def _():
            o_ref[0] = acc

        @pl.when(i != 0)
        def _():
            o_ref[0] += acc

    in_specs = [pl.BlockSpec((tm, k), lambda j, i: (i, 0))]
    args = [a]
    if gain is not None:
        in_specs.append(pl.BlockSpec((1, k), lambda j, i: (0, 0)))
        args.append(gain)
    in_specs.append(pl.BlockSpec((tm, tn), lambda j, i: (i, j)))
    args.append(dy)
    return pl.pallas_call(
        body, name=name, grid=(n // tn, lp // tm), in_specs=in_specs,
        out_specs=pl.BlockSpec((1, k, tn), lambda j, i: (j // per, 0, j % per)),
        out_shape=jax.ShapeDtypeStruct((nshard, k, ns), F32),
        compiler_params=_params(("parallel", "arbitrary")),
    )(*args)


def _head_tables(d):
    idx = np.arange(LANES)
    mean = (idx[:, None] // HEAD_DIM == idx[None, :] // HEAD_DIM).astype(np.float32) / HEAD_DIM
    n_heads = d // HEAD_DIM
    kvh = n_heads // GQA_REP
    c = np.arange(d)
    col = np.arange(kvh * LANES)
    head_of_col = (col // LANES) * GQA_REP + (col % LANES)
    sel = ((c[:, None] // HEAD_DIM == head_of_col[None, :]) & ((col % LANES) < GQA_REP)[None, :]).astype(np.float32)
    return jnp.asarray(mean, BF16), jnp.asarray(sel, BF16)


def _swap_pairs(y):
    lane = lax.broadcasted_iota(jnp.int32, y.shape, 1)
    return jnp.where(lane % 2 == 0, pltpu.roll(y, LANES - 1, 1), pltpu.roll(y, 1, 1))


def _qk_prep(qkv, cos_t, sin_t, qg, kg, mean_m, tm):
    lp, wq = qkv.shape
    d = wq * 2 // 3
    kvw = d // 4
    kvh = kvw // HEAD_DIM
    scale = HEAD_DIM ** -0.5

    def body(x_ref, c_ref, s_ref, qg_ref, kg_ref, m_ref, q_ref, k_ref, v_ref):
        cs, sn, mm = c_ref[...], s_ref[...], m_ref[...]
        for b in range((d + kvw) // LANES):
            x = x_ref[:, b * LANES:(b + 1) * LANES]
            gg = qg_ref[...] if b < d // LANES else kg_ref[...]
            y = x * lax.rsqrt(_split_dot(x * x, mm) + NORM_EPS) * gg
            out = y * cs + _swap_pairs(y) * sn
            if b < d // LANES:
                q_ref[:, b * LANES:(b + 1) * LANES] = (out * scale).astype(BF16)
            else:
                kb = b - d // LANES
                k_ref[2 * kb] = out[:, :HEAD_DIM].astype(BF16)
                k_ref[2 * kb + 1] = out[:, HEAD_DIM:].astype(BF16)
        lane = lax.broadcasted_iota(jnp.int32, (tm, LANES - HEAD_DIM), 1)
        ones_col = (lane == 0).astype(BF16)
        for h in range(kvh):
            vh = x_ref[:, d + kvw + h * HEAD_DIM:d + kvw + (h + 1) * HEAD_DIM].astype(BF16)
            v_ref[h] = jnp.concatenate([vh, ones_col], axis=1)

    k_spec = pl.BlockSpec((kvh, tm, HEAD_DIM), lambda i: (0, i, 0))
    v_spec = pl.BlockSpec((kvh, tm, LANES), lambda i: (0, i, 0))
    return pl.pallas_call(
        body, name="qk_prep", grid=(lp // tm,),
        in_specs=[_row(tm, wq), _row(tm, LANES), _row(tm, LANES), _full((1, LANES)), _full((1, LANES)),
                  _full((LANES, LANES))],
        out_specs=[_row(tm, d), k_spec, v_spec],
        out_shape=[jax.ShapeDtypeStruct((lp, d), BF16), jax.ShapeDtypeStruct((kvh, lp, HEAD_DIM), BF16),
                   jax.ShapeDtypeStruct((kvh, lp, LANES), BF16)],
        compiler_params=_params(("parallel",)),
    )(qkv, cos_t, sin_t, qg, kg, mean_m)


def _qk_bwd(qkv, dq, dk, dv, cos_t, sin_t, qg, kg, mean_m, tm):
    lp, wq = qkv.shape
    d = wq * 2 // 3
    kvw = d // 4
    kvh = kvw // HEAD_DIM
    scale = HEAD_DIM ** -0.5

    def body(x_ref, dq_ref, dk_ref, dv_ref, c_ref, s_ref, qg_ref, kg_ref, m_ref, o_ref, dqg_ref, dkg_ref):
        first = pl.program_id(0) == 0
        cs, sn, mm = c_ref[...], s_ref[...], m_ref[...]
        sums = [None, None]
        for b in range((d + kvw) // LANES):
            is_q = b < d // LANES
            x = x_ref[:, b * LANES:(b + 1) * LANES]
            gg = qg_ref[...] if is_q else kg_ref[...]
            r = lax.rsqrt(_split_dot(x * x, mm) + NORM_EPS)
            nrm = x * r
            if is_q:
                dout = dq_ref[:, b * LANES:(b + 1) * LANES] * scale
            else:
                kb = b - d // LANES
                dout = jnp.concatenate([dk_ref[2 * kb], dk_ref[2 * kb + 1]], axis=1)
            dy = dout * cs + _swap_pairs(dout * sn)
            part = jnp.sum(dy * nrm, axis=0, keepdims=True)
            sums[0 if is_q else 1] = part if sums[0 if is_q else 1] is None else sums[0 if is_q else 1] + part
            dn = dy * gg
            o_ref[:, b * LANES:(b + 1) * LANES] = r * (dn - nrm * _split_dot(dn * nrm, mm))
        for h in range(kvh):
            o_ref[:, d + kvw + h * HEAD_DIM:d + kvw + (h + 1) * HEAD_DIM] = dv_ref[h]
        for ref, s in ((dqg_ref, sums[0]), (dkg_ref, sums[1])):
            s = s + pltpu.roll(s, HEAD_DIM, 1)

            @pl.when(first)
            def _(ref=ref, s=s):
                ref[...] = s

            @pl.when(jnp.logical_not(first))
            def _(ref=ref, s=s):
                ref[...] += s

    kv_spec = pl.BlockSpec((kvh, tm, HEAD_DIM), lambda i: (0, i, 0))
    return pl.pallas_call(
        body, name="qk_bwd", grid=(lp // tm,),
        in_specs=[_row(tm, wq), _row(tm, d), kv_spec, kv_spec, _row(tm, LANES), _row(tm, LANES),
                  _full((1, LANES)), _full((1, LANES)), _full((LANES, LANES))],
        out_specs=[_row(tm, wq), _full((1, LANES)), _full((1, LANES))],
        out_shape=[jax.ShapeDtypeStruct((lp, wq), F32), jax.ShapeDtypeStruct((1, LANES), F32),
                   jax.ShapeDtypeStruct((1, LANES), F32)],
        compiler_params=_params(("arbitrary",)),
    )(qkv, dq, dk, dv, cos_t, sin_t, qg, kg, mean_m)


def _attn_fwd(q, k, v, kbias, tq, tk, gather=()):
    lp, d = q.shape
    kvh = k.shape[0]
    rw = GQA_REP * HEAD_DIM
    nk = lp // tk

    ng = len(gather)
    steps = kvh * (lp // tq) * nk

    def body(*refs):
        q_ref, k_ref, v_ref, kb_ref = refs[:4]
        o_ref, lse_ref = refs[4 + ng:6 + ng]
        m_s, acc_s = refs[6 + 2 * ng:8 + 2 * ng]
        j = pl.program_id(2)

        if ng:
            phases = _gather_phases(refs[4:4 + ng], refs[6 + ng:6 + 2 * ng], *refs[8 + 2 * ng:])
            step = (pl.program_id(0) * (lp // tq) + pl.program_id(1)) * nk + j
            for n, phase in enumerate(phases):
                pl.when(step == n * steps // 3)(phase)

        @pl.when(j == 0)
        def _():
            m_s[...] = jnp.full(m_s.shape, MASK_VALUE, F32)
            acc_s[...] = jnp.zeros(acc_s.shape, F32)

        def heads(masked):
            kk, vv = k_ref[0], v_ref[0]

            def scores(h):
                return _dot_nt(q_ref[:, h * HEAD_DIM:(h + 1) * HEAD_DIM], kk)

            def softmax(h, s):
                if masked:
                    s = s + kb_ref[...]
                m_prev = m_s[h]
                m_new = jnp.maximum(m_prev, jnp.max(s, axis=1, keepdims=True))
                m_s[h] = m_new
                return jnp.exp(s - m_new[:, :1]).astype(BF16), jnp.exp(m_prev - m_new)

            def accumulate(h, p, alpha):
                acc_s[h] = acc_s[h] * alpha + _dot(p, vv)

            ss = [scores(h) for h in range(GQA_REP)]
            pa = [softmax(h, ss[h]) for h in range(GQA_REP)]
            for h in range(GQA_REP):
                accumulate(h, *pa[h])

        pl.when(j != nk - 1)(functools.partial(heads, False))
        pl.when(j == nk - 1)(functools.partial(heads, True))

        @pl.when(j == nk - 1)
        def _():
            lane = lax.broadcasted_iota(jnp.int32, (tq, LANES), 1)
            lse = jnp.zeros((tq, LANES), F32)
            outs = []
            for h in range(GQA_REP):
                acc = acc_s[h]
                l = acc[:, HEAD_DIM:HEAD_DIM + 1]
                outs.append(acc[:, :HEAD_DIM] / l)
                lse = jnp.where(lane == h, m_s[h][:, :1] + jnp.log(l), lse)
            o_ref[...] = jnp.concatenate(outs, axis=1).astype(BF16)
            lse_ref[...] = lse

    sems = [pltpu.SemaphoreType.DMA((ng, 7)), pltpu.SemaphoreType.DMA((ng, 7)), pltpu.SemaphoreType.DMA((ng,))]
    res = pl.pallas_call(
        body, name="attn_fwd", grid=(kvh, lp // tq, nk),
        in_specs=[pl.BlockSpec((tq, rw), lambda g, i, j: (i, g)),
                  pl.BlockSpec((1, tk, HEAD_DIM), lambda g, i, j: (g, j, 0)),
                  pl.BlockSpec((1, tk, LANES), lambda g, i, j: (g, j, 0)),
                  pl.BlockSpec((1, tk), lambda g, i, j: (0, j))] + [_ANY] * ng,
        out_specs=[pl.BlockSpec((tq, rw), lambda g, i, j: (i, g)),
                   pl.BlockSpec((tq, LANES), lambda g, i, j: (i, g))] + [_ANY] * ng,
        out_shape=[jax.ShapeDtypeStruct((lp, d), BF16), jax.ShapeDtypeStruct((lp, kvh * LANES), F32)]
        + [jax.ShapeDtypeStruct((8,) + b.shape, b.dtype) for b in gather],
        scratch_shapes=[pltpu.VMEM((GQA_REP, tq, LANES), F32), pltpu.VMEM((GQA_REP, tq, LANES), F32)]
        + (sems if ng else []),
        compiler_params=_params(("arbitrary", "arbitrary", "arbitrary")),
    )(q, k, v, kbias, *gather)
    return res[0], res[1], list(res[2:])


def _attn_bwd(q, k, v, kbias, do, lse, delta, tq, tk):
    lp, d = q.shape
    kvh = k.shape[0]
    rw = GQA_REP * HEAD_DIM
    nq = lp // tq

    def body(q_ref, k_ref, v_ref, kb_ref, do_ref, lse_ref, dl_ref, dq_ref, dk_ref, dv_ref, dk_s, dv_s):
        j = pl.program_id(1)
        i = pl.program_id(2)

        @pl.when(jnp.logical_and(i == 0, j == 0))
        def _():
            dq_ref[...] = jnp.zeros(dq_ref.shape, F32)

        @pl.when(i == 0)
        def _():
            dk_s[...] = jnp.zeros(dk_s.shape, F32)
            dv_s[...] = jnp.zeros(dv_s.shape, F32)

        def heads(masked):
            kk, vv = k_ref[0], v_ref[0][:, :HEAD_DIM]
            lse, dl = lse_ref[...], dl_ref[...]
            dqs = []
            for h in range(GQA_REP):
                qh = q_ref[:, h * HEAD_DIM:(h + 1) * HEAD_DIM]
                doh = do_ref[:, h * HEAD_DIM:(h + 1) * HEAD_DIM]
                s = _dot_nt(qh, kk)
                if masked:
                    s = s + kb_ref[...]
                p = jnp.exp(s - lse[:, h:h + 1])
                ds = (p * (_dot_nt(doh, vv) - dl[:, h:h + 1])).astype(BF16)
                dv_s[...] += _dot_tn(p.astype(BF16), doh)
                dk_s[...] += _dot_tn(ds, qh)
                dqs.append(_dot(ds, kk))
            rows = pl.ds(pl.multiple_of(i * tq, tq), tq)
            dq_ref[rows, :] += jnp.concatenate(dqs, axis=1)

        nk = lp // tk
        pl.when(j != nk - 1)(functools.partial(heads, False))
        pl.when(j == nk - 1)(functools.partial(heads, True))

        @pl.when(i == nq - 1)
        def _():
            dk_ref[0] = dk_s[...]
            dv_ref[0] = dv_s[...]

    return pl.pallas_call(
        body, name="attn_bwd", grid=(kvh, lp // tk, nq),
        in_specs=[pl.BlockSpec((tq, rw), lambda g, j, i: (i, g)),
                  pl.BlockSpec((1, tk, HEAD_DIM), lambda g, j, i: (g, j, 0)),
                  pl.BlockSpec((1, tk, LANES), lambda g, j, i: (g, j, 0)),
                  pl.BlockSpec((1, tk), lambda g, j, i: (0, j)),
                  pl.BlockSpec((tq, rw), lambda g, j, i: (i, g)),
                  pl.BlockSpec((tq, LANES), lambda g, j, i: (i, g)),
                  pl.BlockSpec((tq, LANES), lambda g, j, i: (i, g))],
        out_specs=[pl.BlockSpec((lp, rw), lambda g, j, i: (0, g)),
                   pl.BlockSpec((1, tk, HEAD_DIM), lambda g, j, i: (g, j, 0)),
                   pl.BlockSpec((1, tk, HEAD_DIM), lambda g, j, i: (g, j, 0))],
        out_shape=[jax.ShapeDtypeStruct((lp, d), F32), jax.ShapeDtypeStruct((kvh, lp, HEAD_DIM), F32),
                   jax.ShapeDtypeStruct((kvh, lp, HEAD_DIM), F32)],
        scratch_shapes=[pltpu.VMEM((tk, HEAD_DIM), F32), pltpu.VMEM((tk, HEAD_DIM), F32)],
        compiler_params=_params(("parallel", "arbitrary", "arbitrary")),
    )(q, k, v, kbias, do, lse, delta)


def _ssm_math(a_re, a_im, log_dt, bt_re, bt_im):
    dt = jnp.exp(log_dt)
    lam_re = jnp.minimum(a_re, EIG_RE_MAX)
    lam_im = a_im
    mag = jnp.exp(lam_re * dt)
    ang = lam_im * dt
    lb_re = mag * jnp.cos(ang)
    lb_im = mag * jnp.sin(ang)
    num_re = lb_re - 1.0
    num_im = lb_im
    den = lam_re * lam_re + lam_im * lam_im
    f_re = (num_re * lam_re + num_im * lam_im) / den
    f_im = (num_im * lam_re - num_re * lam_im) / den
    bb_re = f_re[:, None, :] * bt_re - f_im[:, None, :] * bt_im
    bb_im = f_re[:, None, :] * bt_im + f_im[:, None, :] * bt_re
    return lb_re, lb_im, bb_re, bb_im


def _ssm_discretize(a_re, a_im, log_dt, bt_re, bt_im):
    nd, g, n = a_re.shape
    p = bt_re.shape[2]

    def body(ar_ref, ai_ref, ld_ref, br_ref, bi_ref, bbr_ref, bbi_ref, pr_ref, pi_ref, hr_ref, hi_ref):
        lb_re, lb_im, bb_re, bb_im = _ssm_math(ar_ref[0], ai_ref[0], ld_ref[0], br_ref[0], bi_ref[0])
        bbr_ref[0] = bb_re
        bbi_ref[0] = bb_im
        cr, ci = lb_re, lb_im
        for k in range(KSTEPS):
            pr_ref[0, k] = cr
            pi_ref[0, k] = ci
            if k < KSTEPS - 1:
                cr, ci = cr * lb_re - ci * lb_im, cr * lb_im + ci * lb_re
        for t in range(2):
            cr, ci = cr * cr - ci * ci, 2.0 * cr * ci
            hr_ref[0, t] = cr
            hi_ref[0, t] = ci

    s3 = pl.BlockSpec((1, g, n), lambda i: (i, 0, 0))
    s4 = pl.BlockSpec((1, g, p, n), lambda i: (i, 0, 0, 0))
    sp = pl.BlockSpec((1, KSTEPS, g, n), lambda i: (i, 0, 0, 0))
    sh = pl.BlockSpec((1, 2, g, n), lambda i: (i, 0, 0, 0))
    return pl.pallas_call(
        body, name="ssm_discretize", grid=(nd,),
        in_specs=[s3, s3, pl.BlockSpec((1, g, 1), lambda i: (i, 0, 0)), s4, s4],
        out_specs=[s4, s4, sp, sp, sh, sh],
        out_shape=[jax.ShapeDtypeStruct((nd, g, p, n), F32)] * 2 + [jax.ShapeDtypeStruct((nd, KSTEPS, g, n), F32)] * 2
        + [jax.ShapeDtypeStruct((nd, 2, g, n), F32)] * 2,
        compiler_params=_params(("parallel",)),
    )(a_re, a_im, log_dt, bt_re, bt_im)


def _ssm_param_bwd(a_re, a_im, log_dt, bt_re, bt_im, dlb_re, dlb_im, dbb_re, dbb_im):
    nd, g, n = a_re.shape
    p = bt_re.shape[2]

    def body(ar_ref, ai_ref, ld_ref, br_ref, bi_ref, c0_ref, c1_ref, c2_ref, c3_ref,
             o0_ref, o1_ref, o2_ref, o3_ref, o4_ref):
        _, vjp = jax.vjp(_ssm_math, ar_ref[0], ai_ref[0], ld_ref[0], br_ref[0], bi_ref[0])
        outs = vjp((c0_ref[0], c1_ref[0], c2_ref[0], c3_ref[0]))
        for ref, val in zip((o0_ref, o1_ref, o2_ref, o3_ref, o4_ref), outs):
            ref[0] = val

    s3 = pl.BlockSpec((1, g, n), lambda i: (i, 0, 0))
    s1 = pl.BlockSpec((1, g, 1), lambda i: (i, 0, 0))
    s4 = pl.BlockSpec((1, g, p, n), lambda i: (i, 0, 0, 0))
    return pl.pallas_call(
        body, name="ssm_param_bwd", grid=(nd,),
        in_specs=[s3, s3, s1, s4, s4, s3, s3, s4, s4],
        out_specs=[s3, s3, s1, s4, s4],
        out_shape=[jax.ShapeDtypeStruct((nd, g, n), F32)] * 2 + [jax.ShapeDtypeStruct((nd, g, 1), F32)]
        + [jax.ShapeDtypeStruct((nd, g, p, n), F32)] * 2,
        compiler_params=_params(("parallel",)),
    )(a_re, a_im, log_dt, bt_re, bt_im, dlb_re, dlb_im, dbb_re, dbb_im)


def _cmul(ar, ai, xr, xi, conj):
    if conj:
        return ar * xr + ai * xi, ar * xi - ai * xr
    return ar * xr - ai * xi, ar * xi + ai * xr


def _scan_chunk(buf, tab, carry, ein, nj, rev, conj, base=0):
    ks = list(range(KSTEPS))
    if rev:
        ks = ks[::-1]
    sub = lax.broadcasted_iota(jnp.int32, (SUBLANES, SCAN_LANES), 0)
    edge = sub == (SUBLANES - 1 if rev else 0)

    def step(j, _):
        jr, ji = j, nj + j
        ar, ai = tab[base, jr], tab[base, ji]
        hr = jnp.zeros((SUBLANES, SCAN_LANES), F32)
        hi = jnp.zeros((SUBLANES, SCAN_LANES), F32)
        for k in ks:
            rows = pl.ds(k * SUBLANES, SUBLANES)
            pr, pi_ = _cmul(ar, ai, hr, hi, conj)
            hr = pr + buf[jr, rows, :]
            hi = pi_ + buf[ji, rows, :]
            buf[jr, rows, :] = hr
            buf[ji, rows, :] = hi
        shift = SUBLANES - 1 if rev else 1
        er = jnp.where(edge, carry[jr], pltpu.roll(hr, shift, 0))
        ei = jnp.where(edge, carry[ji], pltpu.roll(hi, shift, 0))
        for t, dist in enumerate((1, 2, 4)):
            sh = SUBLANES - dist if rev else dist
            pr, pi_ = _cmul(tab[base + 1 + t, jr], tab[base + 1 + t, ji], pltpu.roll(er, sh, 0), pltpu.roll(ei, sh, 0), conj)
            er, ei = er + pr, ei + pi_
        ein[jr] = er
        ein[ji] = ei
        pr, pi_ = _cmul(tab[base + 4 + KSTEPS - 1, jr], tab[base + 4 + KSTEPS - 1, ji], er, ei, conj)
        last = 0 if rev else SUBLANES - 1
        carry[jr] = jnp.broadcast_to((hr + pr)[last:last + 1, :], (SUBLANES, SCAN_LANES))
        carry[ji] = jnp.broadcast_to((hi + pi_)[last:last + 1, :], (SUBLANES, SCAN_LANES))
        for n, k in enumerate(ks):
            rows = pl.ds(k * SUBLANES, SUBLANES)
            pr, pi_ = _cmul(tab[base + 4 + n, jr], tab[base + 4 + n, ji], er, ei, conj)
            buf[jr, rows, :] += pr
            buf[ji, rows, :] += pi_
        return 0

    lax.fori_loop(0, nj, step, 0)


def _state_lanes(b):
    per = SCAN_LANES // SSM_BLOCK
    return b // per, slice((b % per) * SSM_BLOCK, (b % per + 1) * SSM_BLOCK)


def _project_in(src, w_ref, buf, nj):
    nb, cb, _ = w_ref.shape
    for b in range(nb):
        res = _dot(src[:, b * cb:(b + 1) * cb], w_ref[b])
        j, lanes = _state_lanes(b)
        buf[j, :, lanes] = res[:, :SSM_BLOCK]
        buf[nj + j, :, lanes] = res[:, SSM_BLOCK:]


def _state_block(buf, b, nj):
    j, lanes = _state_lanes(b)
    return jnp.concatenate([buf[j, :, lanes], buf[nj + j, :, lanes]], axis=1).astype(BF16)


def _project_out(buf, w_ref, nj):
    return jnp.concatenate([_dot_nt(_state_block(buf, b, nj), w_ref[b]) for b in range(w_ref.shape[0])], axis=1)


def _ssm_fwd(u, wb, wct, tab, rev, name):
    lp, w = u.shape
    nb, cb, _ = wb.shape
    nj = nb * SSM_BLOCK // SCAN_LANES
    nc = lp // CHUNK
    ntab = tab.shape[0]
    cidx = (lambda c: nc - 1 - c) if rev else (lambda c: c)

    def body(u_ref, wb_ref, wct_ref, tab_ref, y_ref, ck_ref, buf, carry, ein):
        @pl.when(pl.program_id(0) == 0)
        def _():
            carry[...] = jnp.zeros(carry.shape, F32)

        _project_in(u_ref[...].astype(BF16), wb_ref, buf, nj)
        ck_ref[0] = carry[...]
        _scan_chunk(buf, tab_ref, carry, ein, nj, rev, False)
        y_ref[...] = _project_out(buf, wct_ref, nj)

    wshape = (nb, cb, 2 * SSM_BLOCK)
    return pl.pallas_call(
        body, name=name, grid=(nc,),
        in_specs=[pl.BlockSpec((CHUNK, w), lambda c: (cidx(c), 0)), _full(wshape), _full(wshape),
                  _full((ntab, 2 * nj, SUBLANES, SCAN_LANES))],
        out_specs=[pl.BlockSpec((CHUNK, w), lambda c: (cidx(c), 0)),
                   pl.BlockSpec((1, 2 * nj, SUBLANES, SCAN_LANES), lambda c: (cidx(c), 0, 0, 0))],
        out_shape=[jax.ShapeDtypeStruct((lp, w), F32), jax.ShapeDtypeStruct((nc, 2 * nj, SUBLANES, SCAN_LANES), F32)],
        scratch_shapes=[pltpu.VMEM((2 * nj, CHUNK, SCAN_LANES), F32), pltpu.VMEM((2 * nj, SUBLANES, SCAN_LANES), F32),
                        pltpu.VMEM((2 * nj, SUBLANES, SCAN_LANES), F32)],
        compiler_params=_params(("arbitrary",)),
    )(u, wb, wct, tab)


def _ssm_bwd(u, dy, ckpt, wb, wct, tab, rev, name, scatter=()):
    lp, w = u.shape
    nb, cb, _ = wb.shape
    nj = nb * SSM_BLOCK // SCAN_LANES
    nc = lp // CHUNK
    ntab = tab.shape[0]
    cidx = (lambda c: c) if rev else (lambda c: nc - 1 - c)

    ns = len(scatter)

    def body(*refs):
        u_ref, dy_ref, ck_ref, wb_ref, wct_ref, tab_hbm = refs[:6]
        du_ref, dbb_ref, dcc_ref, dlb_ref = refs[6 + ns:10 + ns]
        tab_ref, dwb_ref, dwc_ref, xs, ls, xcar, lcar, xin, lin = refs[10 + 2 * ns:19 + 2 * ns]
        c = pl.program_id(0)

        if ns:
            start, finish = _scatter_phases(refs[6:6 + ns], refs[10 + ns:10 + 2 * ns], *refs[19 + 2 * ns:])
            pl.when(c == 0)(start)
            pl.when(c == nc - 1)(finish)

        @pl.when(c == 0)
        def _():
            pltpu.sync_copy(tab_hbm, tab_ref)
            lcar[...] = jnp.zeros(lcar.shape, F32)
            dwb_ref[...] = jnp.zeros(dwb_ref.shape, F32)
            dwc_ref[...] = jnp.zeros(dwc_ref.shape, F32)
            dlb_ref[...] = jnp.zeros(dlb_ref.shape, F32)

        ub = u_ref[...].astype(BF16)
        dyb = dy_ref[...].astype(BF16)
        _project_in(ub, wb_ref, xs, nj)
        xcar[...] = ck_ref[0]
        _scan_chunk(xs, tab_ref, xcar, xin, nj, rev, False)
        _project_in(dyb, wct_ref, ls, nj)
        _scan_chunk(ls, tab_ref, lcar, lin, nj, not rev, True, base=ntab // 2)
        dus = []
        for b in range(nb):
            chans = slice(b * cb, (b + 1) * cb)
            xb = _state_block(xs, b, nj)
            lb = _state_block(ls, b, nj)
            dwc_ref[b] += _dot_tn(dyb[:, chans], xb)
            dwb_ref[b] += _dot_tn(ub[:, chans], lb)
            dus.append(_dot_nt(lb, wb_ref[b]))
        du_ref[...] = jnp.concatenate(dus, axis=1)

        def step(j, _):
            jr, ji = j, nj + j
            ar = jnp.zeros((SUBLANES, SCAN_LANES), F32)
            ai = jnp.zeros((SUBLANES, SCAN_LANES), F32)
            for k in range(KSTEPS):
                kp = k + 1 if rev else k - 1
                rows = pl.ds(k * SUBLANES, SUBLANES)
                if 0 <= kp < KSTEPS:
                    prow = pl.ds(kp * SUBLANES, SUBLANES)
                    xr, xi = xs[jr, prow, :], xs[ji, prow, :]
                else:
                    xr, xi = xin[jr], xin[ji]
                lr, li = ls[jr, rows, :], ls[ji, rows, :]
                ar += lr * xr + li * xi
                ai += li * xr - lr * xi
            dlb_ref[jr] += ar
            dlb_ref[ji] += ai
            return 0

        lax.fori_loop(0, nj, step, 0)

        @pl.when(c == nc - 1)
        def _():
            for b in range(2 * nj):
                dlb_ref[b] = jnp.broadcast_to(jnp.sum(dlb_ref[b], axis=0, keepdims=True), (SUBLANES, SCAN_LANES))
            for g in range(w // SSM_GROUP):
                b, gl = divmod(g, cb // SSM_GROUP)
                rows = slice(gl * SSM_GROUP, (gl + 1) * SSM_GROUP)
                for part in range(2):
                    cols = slice(part * SSM_BLOCK + gl * SSM_STATE, part * SSM_BLOCK + (gl + 1) * SSM_STATE)
                    dbb_ref[part, g * SSM_GROUP:(g + 1) * SSM_GROUP, :] = dwb_ref[b, rows, cols]
                    dcc_ref[part, g * SSM_GROUP:(g + 1) * SSM_GROUP, :] = dwc_ref[b, rows, cols]

    st = (2 * nj, SUBLANES, SCAN_LANES)
    wshape = (nb, cb, 2 * SSM_BLOCK)
    sems = [pltpu.SemaphoreType.DMA((ns, 3)), pltpu.SemaphoreType.DMA((ns, 3)), pltpu.SemaphoreType.DMA((ns,))]
    res = pl.pallas_call(
        body, name=name, grid=(nc,),
        in_specs=[pl.BlockSpec((CHUNK, w), lambda c: (cidx(c), 0)), pl.BlockSpec((CHUNK, w), lambda c: (cidx(c), 0)),
                  pl.BlockSpec((1,) + st, lambda c: (cidx(c), 0, 0, 0)), _full(wshape), _full(wshape), _ANY]
        + [_ANY] * ns,
        out_specs=[pl.BlockSpec((CHUNK, w), lambda c: (cidx(c), 0)), _full((2, w, SSM_STATE)),
                   _full((2, w, SSM_STATE)), _full(st)] + [_ANY] * ns,
        out_shape=[jax.ShapeDtypeStruct((lp, w), F32), jax.ShapeDtypeStruct((2, w, SSM_STATE), F32),
                   jax.ShapeDtypeStruct((2, w, SSM_STATE), F32), jax.ShapeDtypeStruct(st, F32)]
        + [jax.ShapeDtypeStruct(p.shape, p.dtype) for p in scatter],
        scratch_shapes=[pltpu.VMEM((ntab,) + st, F32), pltpu.VMEM(wshape, F32), pltpu.VMEM(wshape, F32),
                        pltpu.VMEM((2 * nj, CHUNK, SCAN_LANES), F32), pltpu.VMEM((2 * nj, CHUNK, SCAN_LANES), F32),
                        pltpu.VMEM(st, F32), pltpu.VMEM(st, F32), pltpu.VMEM(st, F32), pltpu.VMEM(st, F32)]
        + (sems if ns else []),
        compiler_params=_params(("arbitrary",)),
    )(u, dy, ckpt, wb, wct, tab, *scatter)
    return res[0], res[1], res[2], res[3], list(res[4:])


def _embed_blocks(t_re, t_im):
    g, p, n = t_re.shape
    gb = SSM_BLOCK // n
    eye = jnp.eye(gb, dtype=t_re.dtype)
    parts = [jnp.einsum('bgpn,gh->bgphn', t.reshape(g // gb, gb, p, n), eye).reshape(g // gb, gb * p, gb * n)
             for t in (t_re, t_im)]
    return jnp.concatenate(parts, axis=2)


def _scan_tables(pw_re, pw_im, hi_re, hi_im, rev):
    s = pw_re.shape[1] * pw_re.shape[2]
    nj = s // SCAN_LANES
    sub = np.arange(SUBLANES)
    live = np.ones((4 + KSTEPS, 1, SUBLANES, 1), bool)
    for row, dist in ((1, 1), (2, 2), (3, 4)):
        live[row, 0, :, 0] = (sub < SUBLANES - dist) if rev else (sub >= dist)

    def lay(pw, hi):
        rows = jnp.concatenate([pw[:1], pw[KSTEPS - 1:], hi, pw], axis=0).reshape(4 + KSTEPS, nj, 1, SCAN_LANES)
        return jnp.where(live, jnp.broadcast_to(rows, (4 + KSTEPS, nj, SUBLANES, SCAN_LANES)), 0.0)

    return jnp.concatenate([lay(pw_re, hi_re), lay(pw_im, hi_im)], axis=1)


def _adamw(w, g, m, v, tm):
    r, c = w.shape
    c1 = 1.0 - ADAM_B1 ** ADAM_STEP
    c2 = 1.0 - ADAM_B2 ** ADAM_STEP

    def body(w_ref, g_ref, m_ref, v_ref, d_ref, nm_ref, nv_ref):
        gg = g_ref[...]
        nm = ADAM_B1 * m_ref[...] + (1.0 - ADAM_B1) * gg
        nv = ADAM_B2 * v_ref[...] + (1.0 - ADAM_B2) * (gg * gg)
        nm_ref[...] = nm
        nv_ref[...] = nv
        d_ref[...] = -ADAM_LR * ((nm / c1) / (jnp.sqrt(nv / c2) + ADAM_EPS) + ADAM_WD * w_ref[...])

    spec = _row(tm, c)
    return pl.pallas_call(
        body, name="adamw", grid=(r // tm,), in_specs=[spec] * 4, out_specs=[spec] * 3,
        out_shape=[jax.ShapeDtypeStruct((r, c), F32)] * 3, compiler_params=_params(("parallel",)),
    )(w, g, m, v)


def _pair_sum(g42, got, core, out_dtype, tm, name):
    _, _, r, c = g42.shape

    def body(core_ref, a_ref, b_ref, o_ref):
        o_ref[...] = (a_ref[...] + b_ref[...]).astype(out_dtype)

    grid_spec = pltpu.PrefetchScalarGridSpec(
        num_scalar_prefetch=1, grid=(4, r // tm),
        in_specs=[pl.BlockSpec((1, None, tm, c), lambda s, i, core_ref: (s, core_ref[0], i, 0)),
                  pl.BlockSpec((1, tm, c), lambda s, i, core_ref: (s, i, 0))],
        out_specs=pl.BlockSpec((1, tm, c), lambda s, i, core_ref: (s, i, 0)))
    return pl.pallas_call(
        body, name=name, grid_spec=grid_spec, out_shape=jax.ShapeDtypeStruct((4, r, c), out_dtype),
        compiler_params=_params(("parallel", "parallel")),
    )(core, g42, got)


def _sum4(a, core, tm, name):
    _, r, c = a.shape

    def body(core_ref, a_ref, o_ref):
        o_ref[...] = ((a_ref[0].astype(F32) + a_ref[1].astype(F32)) + a_ref[2].astype(F32)) + a_ref[3].astype(F32)

    grid_spec = pltpu.PrefetchScalarGridSpec(
        num_scalar_prefetch=1, grid=(r // tm,),
        in_specs=[pl.BlockSpec((4, tm, c), lambda i, core_ref: (0, i, 0))],
        out_specs=pl.BlockSpec((None, tm, c), lambda i, core_ref: (core_ref[0], i, 0)))
    return pl.pallas_call(
        body, name=name, grid_spec=grid_spec, out_shape=jax.ShapeDtypeStruct((2, r, c), F32),
        compiler_params=_params(("parallel",)),
    )(core, a)


_ANY = pl.BlockSpec(memory_space=pl.ANY)


def _gather_phases(xs, outs, send_sems, recv_sems, local_sems):
    n = len(xs)

    def parts():
        x, y, c = lax.axis_index("x"), lax.axis_index("y"), lax.axis_index("c")
        return c, (x, y, c), (x, y, 1 - c), [(1 - x, y), (x, 1 - y), (1 - x, 1 - y)]

    def slot(t, px, py, pc):
        return outs[t].at[4 * px + 2 * py + pc]

    def copy(t, k, blk, to, src=None):
        return pltpu.make_async_remote_copy(
            src_ref=slot(t, *blk) if src is None else src, dst_ref=slot(t, *blk),
            send_sem=send_sems.at[t, k], recv_sem=recv_sems.at[t, k], device_id=to, device_id_type=MESH_ID)

    def own(t, me):
        return pltpu.make_async_copy(xs[t], slot(t, *me), local_sems.at[t])

    def first(t, c, me, sibling, chips):
        return [copy(t, 0, me, sibling, src=xs[t])] + [copy(t, 1 + j, me, (*chip, c), src=xs[t])
                                                       for j, chip in enumerate(chips)]

    def passed(t, c, sibling, chips):
        return [copy(t, 4 + j, (*chip, c), sibling) for j, chip in enumerate(chips)]

    def start():
        c, me, sibling, chips = parts()
        for t in range(n):
            own(t, me).start()
        for t in range(n):
            for cp in first(t, c, me, sibling, chips):
                cp.start()

    def forward():
        c, me, sibling, chips = parts()
        for j, chip in enumerate(chips):
            for t in range(n):
                copy(t, 1 + j, (*chip, c), me).wait_recv()
                passed(t, c, sibling, chips)[j].start()

    def finish():
        c, me, sibling, chips = parts()
        for t in range(n):
            copy(t, 0, sibling, me).wait_recv()
        for j, chip in enumerate(chips):
            for t in range(n):
                copy(t, 4 + j, (*chip, 1 - c), me).wait_recv()
        for t in range(n):
            for cp in first(t, c, me, sibling, chips) + passed(t, c, sibling, chips):
                cp.wait_send()
            own(t, me).wait()

    return start, forward, finish


def _all_gather8(blocks, name):
    n = len(blocks)

    def body(*refs):
        for phase in _gather_phases(refs[:n], refs[n:2 * n], *refs[2 * n:]):
            phase()

    return pl.pallas_call(
        body, name=name, out_shape=[jax.ShapeDtypeStruct((8,) + b.shape, b.dtype) for b in blocks],
        in_specs=[_ANY] * n, out_specs=[_ANY] * n,
        scratch_shapes=[pltpu.SemaphoreType.DMA((n, 7)), pltpu.SemaphoreType.DMA((n, 7)),
                        pltpu.SemaphoreType.DMA((n,))],
    )(*blocks)


def _pair_exchange(gs, name):
    n = len(gs)

    def body(*refs):
        g_refs, outs = refs[:n], refs[n:2 * n]
        send_sems, recv_sems = refs[2 * n:]
        x, y, c = lax.axis_index("x"), lax.axis_index("y"), lax.axis_index("c")
        cps = [pltpu.make_async_remote_copy(
            src_ref=g_refs[t].at[:, 1 - c], dst_ref=outs[t], send_sem=send_sems.at[t], recv_sem=recv_sems.at[t],
            device_id=(x, y, 1 - c), device_id_type=MESH_ID) for t in range(n)]
        for cp in cps:
            cp.start()
        for cp in cps:
            cp.wait()

    return pl.pallas_call(
        body, name=name,
        out_shape=[jax.ShapeDtypeStruct((g.shape[0],) + g.shape[2:], g.dtype) for g in gs],
        in_specs=[_ANY] * n, out_specs=[_ANY] * n,
        scratch_shapes=[pltpu.SemaphoreType.DMA((n,)), pltpu.SemaphoreType.DMA((n,))],
    )(*gs)


def _scatter_phases(p_refs, outs, send_sems, recv_sems, local_sems):
    n = len(p_refs)

    def parts():
        x, y, c = lax.axis_index("x"), lax.axis_index("y"), lax.axis_index("c")
        return c, 2 * x + y, [(1 - x, y), (x, 1 - y), (1 - x, 1 - y)]

    def copy(t, k, src_slab, dst_slab, chip, c):
        return pltpu.make_async_remote_copy(
            src_ref=p_refs[t].at[src_slab], dst_ref=outs[t].at[dst_slab], send_sem=send_sems.at[t, k],
            recv_sem=recv_sems.at[t, k], device_id=(*chip, c), device_id_type=MESH_ID)

    def own(t, mine):
        return pltpu.make_async_copy(p_refs[t].at[mine], outs[t].at[mine], local_sems.at[t])

    def start():
        c, mine, chips = parts()
        for t in range(n):
            own(t, mine).start()
        for k, (cx, cy) in enumerate(chips):
            for t in range(n):
                copy(t, k, 2 * cx + cy, mine, (cx, cy), c).start()

    def finish():
        c, mine, chips = parts()
        for k, (cx, cy) in enumerate(chips):
            for t in range(n):
                copy(t, k, mine, 2 * cx + cy, (cx, cy), c).wait_recv()
        for t in range(n):
            for k, (cx, cy) in enumerate(chips):
                copy(t, k, 2 * cx + cy, mine, (cx, cy), c).wait_send()
            own(t, mine).wait()

    return start, finish


def _chip_scatter(ps, name):
    n = len(ps)

    def body(*refs):
        for phase in _scatter_phases(refs[:n], refs[n:2 * n], *refs[2 * n:]):
            phase()

    return pl.pallas_call(
        body, name=name, out_shape=[jax.ShapeDtypeStruct(p.shape, p.dtype) for p in ps],
        in_specs=[_ANY] * n, out_specs=[_ANY] * n,
        scratch_shapes=[pltpu.SemaphoreType.DMA((n, 3)), pltpu.SemaphoreType.DMA((n, 3)),
                        pltpu.SemaphoreType.DMA((n,))],
    )(*ps)


def _pair_gather(rs, name):
    n = len(rs)

    def body(*refs):
        ins, outs = refs[:n], refs[n:2 * n]
        send_sems, recv_sems = refs[2 * n:]
        x, y, c = lax.axis_index("x"), lax.axis_index("y"), lax.axis_index("c")

        def copy(t, slab):
            return pltpu.make_async_remote_copy(
                src_ref=ins[t].at[slab], dst_ref=outs[t].at[slab], send_sem=send_sems.at[t],
                recv_sem=recv_sems.at[t], device_id=(x, y, 1 - c), device_id_type=MESH_ID)

        sends = [copy(t, c) for t in range(n)]
        for cp in sends:
            cp.start()
        for t in range(n):
            copy(t, 1 - c).wait_recv()
        for cp in sends:
            cp.wait_send()

    return pl.pallas_call(
        body, name=name, out_shape=[jax.ShapeDtypeStruct(r.shape, r.dtype) for r in rs],
        in_specs=[_ANY] * n, out_specs=[_ANY] * n, input_output_aliases={t: t for t in range(n)},
        scratch_shapes=[pltpu.SemaphoreType.DMA((n,)), pltpu.SemaphoreType.DMA((n,))],
    )(*rs)


PACK_COLS = 1024
BIG = (("meta_tokens", 1), ("w_in", 1), ("w_glu", 0), ("w_ssm_proj", 1), ("w_attn_proj", 0), ("w_out", 0),
       ("w_mlp_in", 1), ("w_mlp_out", 0))
SMALL = ("norm_mix_g", "ssm_a_re", "ssm_a_im", "ssm_log_dt", "ssm_b_re", "ssm_b_im", "ssm_c_re", "ssm_c_im",
         "ssm_d", "b_glu", "q_norm_g", "k_norm_g", "norm_mlp_g", "norm_final_g")


def _pad_rows(flat, mult_rows):
    n = flat.shape[0]
    unit = PACK_COLS * mult_rows
    total = -(-n // unit) * unit
    return jnp.pad(flat, (0, total - n)).reshape(total // PACK_COLS, PACK_COLS)


def _half(t, c):
    return lax.dynamic_slice_in_dim(t, c * (t.shape[0] // 2), t.shape[0] // 2, 0)


EARLY_WEIGHTS = ("meta_tokens", "w_in", "w_glu")
LATE_WEIGHTS = tuple(name for name, _ in BIG if name not in EARLY_WEIGHTS)


def _weight_blocks(shards, c, names):
    return [_half(shards[name], c) if name == "meta_tokens" else _half(shards[name], c).astype(BF16) for name in names]


def _shard_major(names, gathered):
    return {name: g.reshape((4, 2 * g.shape[1]) + g.shape[2:]) for name, g in zip(names, gathered)}


class _Reduction:
    def __init__(self, grads, wire, labels, c, tag):
        self.labels, self.wire, self.c, self.tag = labels, wire, c, tag
        self.core = c.astype(jnp.int32).reshape(1)
        g42 = [g.reshape(4, 2, g.shape[1] // 2, g.shape[2]) for g in grads]
        self.tiles = [_pick_tile(g.shape[2], 256, SUBLANES if dt == F32 else 2 * SUBLANES) for g, dt in zip(g42, wire)]
        got = _pair_exchange(g42, "grad_pair_exchange_" + tag)
        self.pair = [_pair_sum(g, o, self.core, dt, tm, "pair_sum_" + lb)
                     for g, o, dt, tm, lb in zip(g42, got, wire, self.tiles, labels)]

    def finish(self, by_src, gathered):
        red = [_sum4(b, self.core, tm, "chip_sum_" + lb) for b, tm, lb in zip(by_src, self.tiles, self.labels)]
        both = _pair_gather(red[:gathered], "grad_pair_gather_" + self.tag)
        pieces = [lax.dynamic_index_in_dim(r, self.c, 0, keepdims=False) for r in red[gathered:]]
        return [b.reshape(2 * b.shape[1], b.shape[2]) for b in both], pieces


def _small_as_shards(small_flat):
    unit = 8 * SUBLANES * PACK_COLS
    k = -(-small_flat.shape[0] // unit) * unit
    return jnp.pad(small_flat, (0, k - small_flat.shape[0])).reshape(4, k // (4 * PACK_COLS), PACK_COLS)


def _to_chunk_order(a):
    lp = a.shape[0]
    rest = a.shape[1:]
    a = a.reshape((lp // CHUNK, SUBLANES, KSTEPS) + rest)
    return a.swapaxes(1, 2).reshape((lp,) + rest)


def _from_chunk_order(a):
    lp = a.shape[0]
    rest = a.shape[1:]
    a = a.reshape((lp // CHUNK, KSTEPS, SUBLANES) + rest)
    return a.swapaxes(1, 2).reshape((lp,) + rest)


def _rope_tables(l_total, lp):
    n_real = l_total - N_META
    pos = np.arange(n_real)
    row_id = (pos // GRID_W).astype(np.float32)
    col_id = (pos % GRID_W).astype(np.float32)
    ppa = HEAD_DIM // 4
    inv_freq = (ROPE_THETA ** (-np.arange(ppa, dtype=np.float64) / ppa)).astype(np.float32)
    ang = np.concatenate([row_id[:, None] * inv_freq, col_id[:, None] * inv_freq], axis=-1)
    ang = np.concatenate([np.zeros((N_META, HEAD_DIM // 2), np.float32), ang,
                          np.zeros((lp - l_total, HEAD_DIM // 2), np.float32)], axis=0).astype(np.float64)
    cos = np.repeat(np.cos(ang), 2, axis=1)
    sin = np.repeat(np.sin(ang), 2, axis=1) * np.tile(np.asarray([-1.0, 1.0]), HEAD_DIM // 2)
    reps = (1, LANES // HEAD_DIM)
    return np.tile(cos, reps).astype(np.float32), np.tile(sin, reps).astype(np.float32)


def kernel(x, meta_tokens, norm_mix_g, w_in, ssm_a_re, ssm_a_im, ssm_log_dt, ssm_b_re, ssm_b_im, ssm_c_re, ssm_c_im, ssm_d, w_glu, b_glu, q_norm_g, k_norm_g, w_ssm_proj, w_attn_proj, w_out, norm_mlp_g, w_mlp_in, w_mlp_out, norm_final_g, loss_target, m_meta_tokens, m_norm_mix_g, m_w_in, m_ssm_a_re, m_ssm_a_im, m_ssm_log_dt, m_ssm_b_re, m_ssm_b_im, m_ssm_c_re, m_ssm_c_im, m_ssm_d, m_w_glu, m_b_glu, m_q_norm_g, m_k_norm_g, m_w_ssm_proj, m_w_attn_proj, m_w_out, m_norm_mlp_g, m_w_mlp_in, m_w_mlp_out, m_norm_final_g, v_meta_tokens, v_norm_mix_g, v_w_in, v_ssm_a_re, v_ssm_a_im, v_ssm_log_dt, v_ssm_b_re, v_ssm_b_im, v_ssm_c_re, v_ssm_c_im, v_ssm_d, v_w_glu, v_b_glu, v_q_norm_g, v_k_norm_g, v_w_ssm_proj, v_w_attn_proj, v_w_out, v_norm_mlp_g, v_w_mlp_in, v_w_mlp_out, v_norm_final_g):
    args = dict(locals())
    names = list(dict.fromkeys([n for n, _ in BIG] + list(SMALL)))
    order = ['meta_tokens', 'norm_mix_g', 'w_in', 'ssm_a_re', 'ssm_a_im', 'ssm_log_dt', 'ssm_b_re', 'ssm_b_im',
             'ssm_c_re', 'ssm_c_im', 'ssm_d', 'w_glu', 'b_glu', 'q_norm_g', 'k_norm_g', 'w_ssm_proj', 'w_attn_proj',
             'w_out', 'norm_mlp_g', 'w_mlp_in', 'w_mlp_out', 'norm_final_g']
    assert sorted(names) == sorted(order)
    c_idx = lax.axis_index("c")

    seq, d = x.shape[1], x.shape[2]
    l_total = seq + N_META
    lp = -(-l_total // SEQ_ALIGN) * SEQ_ALIGN
    hd = d // 2
    n_groups = hd // SSM_GROUP
    n_state = n_groups * SSM_STATE
    nj = n_state // SCAN_LANES
    kvh = d // HEAD_DIM // GQA_REP

    shard2d = {}
    for name, _ in BIG:
        t = args[name]
        shard2d[name] = t.reshape(t.shape[-2], t.shape[-1])
    full = _shard_major(EARLY_WEIGHTS, _all_gather8(_weight_blocks(shard2d, c_idx, EARLY_WEIGHTS), "weight_all_gather"))
    meta_full = jnp.transpose(full["meta_tokens"], (1, 0, 2)).reshape(N_META, d)
    w_in4 = full["w_in"]
    w_glu_f = full["w_glu"].reshape(hd, hd)

    xin = jnp.concatenate([meta_full, x[0], jnp.zeros((lp - l_total, d), F32)], axis=0)
    xin = _to_chunk_order(xin)
    tgt = _to_chunk_order(jnp.pad(loss_target[0], ((N_META, lp - l_total), (0, 0))))
    pos = np.arange(lp)
    rowmask = jnp.asarray(_to_chunk_order(((pos >= N_META) & (pos < l_total)).astype(np.float32)[:, None]))
    kbias = jnp.asarray(_to_chunk_order(np.where(pos < l_total, 0.0, MASK_VALUE).astype(np.float32)[:, None])
                        .reshape(1, lp))
    cos_t, sin_t = (jnp.asarray(_to_chunk_order(t)) for t in _rope_tables(l_total, lp))
    mean_m, sel = _head_tables(d)

    tm = _pick_tile(lp, 320)
    tm_mid = _pick_tile(lp, 384)
    tm_big = _pick_tile(lp, 640)
    tq = _pick_tile(lp, ATTN_Q_TILE, LANES)
    tk = _pick_tile(lp, ATTN_K_TILE, MXU_DIM)
    assert lp - tk <= (l_total // CHUNK) * CHUNK
    g_mix = norm_mix_g.reshape(1, d)
    g_mlp = norm_mlp_g.reshape(1, d)
    g_fin = norm_final_g.reshape(1, d)
    qg = jnp.tile(q_norm_g.reshape(1, HEAD_DIM), (1, LANES // HEAD_DIM))
    kg = jnp.tile(k_norm_g.reshape(1, HEAD_DIM), (1, LANES // HEAD_DIM))
    dskip = ssm_d.reshape(1, hd)
    bglu = b_glu.reshape(1, hd)

    a_re, a_im = ssm_a_re[0], ssm_a_im[0]
    log_dt = ssm_log_dt[0][..., None]
    bt_re = jnp.swapaxes(ssm_b_re[0], 2, 3)
    bt_im = jnp.swapaxes(ssm_b_im[0], 2, 3)
    bb_re, bb_im, pw_re, pw_im, hi_re, hi_im = _ssm_discretize(a_re, a_im, log_dt, bt_re, bt_im)
    wb = [_embed_blocks(bb_re[i], bb_im[i]).astype(BF16) for i in range(2)]
    wct = [_embed_blocks(ssm_c_re[0, i], -ssm_c_im[0, i]).astype(BF16) for i in range(2)]
    tabs = [_scan_tables(pw_re[i], pw_im[i], hi_re[i], hi_im[i], rev=(i == 1)) for i in range(2)]
    tabs_adj = [_scan_tables(pw_re[i], pw_im[i], hi_re[i], hi_im[i], rev=(i == 0)) for i in range(2)]

    u, qkv, gates = _in_proj(xin, g_mix, w_in4, tm_mid)
    y0, ck0 = _ssm_fwd(u, wb[0], wct[0], tabs[0], False, "ssm_fwd_0")
    y1, ck1 = _ssm_fwd(u, wb[1], wct[1], tabs[1], True, "ssm_fwd_1")
    yssm = _glu_fwd(u, y0, y1, dskip, w_glu_f, bglu, tm_big)
    q, k, v = _qk_prep(qkv, cos_t, sin_t, qg, kg, mean_m, tm)
    o, lse, late = _attn_fwd(q, k, v, kbias, tq, tk, gather=_weight_blocks(shard2d, c_idx, LATE_WEIGHTS))
    full = _shard_major(LATE_WEIGHTS, late)
    w_mlp_in4 = full["w_mlp_in"]
    w_ssm_proj4 = full["w_ssm_proj"]
    w_attn_proj_f = full["w_attn_proj"].reshape(d, d)
    w_out_f = full["w_out"].reshape(d, d)
    w_mlp_out_f = full["w_mlp_out"].reshape(4 * d, d)
    h1, merged = _merge_fwd(yssm, o, gates, xin, w_ssm_proj4, w_attn_proj_f, w_out_f, tm_mid)
    r = _mlp_in(h1, g_mlp, w_mlp_in4, tm_mid)
    h3 = _mlp_out(h1, r, w_mlp_out_f, tm_mid)
    loss_tile, dh3, d_gfin = _final_loss(h3, g_fin, tgt, rowmask, tm_big)

    dz, dh3b = _mlp_bwd_a(dh3, r, w_mlp_out_f, tm_mid)
    dh1, d_gmlp = _mlp_bwd_b(dz, dh3, h1, g_mlp, w_mlp_in4, tm_mid)
    dgates, dms, dma, dyssm, do, delta, dh1b = _merge_bwd(dh1, yssm, o, gates, w_ssm_proj4, w_attn_proj_f, w_out_f,
                                                          sel, tm)
    dyv, d_wglu, d_bglu, d_dskip = _glu_bwd(dyssm, u, y0, y1, dskip, w_glu_f, bglu, tm_big)

    tn = min(d, 1024)
    tm_w = _pick_tile(lp, 3 * MXU_DIM, MXU_DIM)
    grads4 = {
        "w_mlp_in": _wgrad(h1, dz, 4, tm_w, tn, "wgrad_mlp_in", gain=g_mlp),
        "w_mlp_out": _wgrad(r, dh3b, 1, tm_w, min(d, 256), "wgrad_mlp_out", square=True).reshape(4, d, d),
        "w_out": _wgrad(merged, dh1b, 1, tm_w, tn, "wgrad_out").reshape(4, d // 4, d),
        "w_attn_proj": _wgrad(o, dma, 1, tm_w, tn, "wgrad_attn_proj").reshape(4, d // 4, d),
        "w_ssm_proj": _wgrad(yssm, dms, 4, tm_w, d // 4, "wgrad_ssm_proj"),
        "w_glu": d_wglu.reshape(4, hd // 4, hd),
    }
    first_names = list(grads4)
    first = _Reduction([grads4[n] for n in first_names], [BF16] * len(first_names), first_names, c_idx, "first")

    du0, dbb0, dcc0, dlb0, first_by_src = _ssm_bwd(u, dyv, ck0, wb[0], wct[0], _both(tabs[0], tabs_adj[0]), False,
                                                   "ssm_bwd_0", scatter=first.pair)
    du1, dbb1, dcc1, dlb1, _ = _ssm_bwd(u, dyv, ck1, wb[1], wct[1], _both(tabs[1], tabs_adj[1]), True, "ssm_bwd_1")
    dq, dk, dv = _attn_bwd(q, k, v, kbias, do, lse, delta, tq, tk)
    dqkv, d_qg, d_kg = _qk_bwd(qkv, dq, dk, dv, cos_t, sin_t, qg, kg, mean_m, tm)
    dxin, d_gmix, dproj = _in_proj_bwd(dyv, du0, du1, dskip, dqkv, dgates, dh1, xin, g_mix, w_in4, tm)
    red_big = dict(zip(first_names, first.finish(first_by_src, len(first_names))[0]))

    grads4["w_in"] = _wgrad(xin, dproj, 4, tm_w, tn, "wgrad_in", gain=g_mix)
    dx_nat = _from_chunk_order(dxin)
    grads4["meta_tokens"] = jnp.swapaxes(dx_nat[:N_META].reshape(N_META, 4, d // 4), 0, 1)
    grad_x = dx_nat[N_META:l_total][None]

    dlb = jnp.stack([dlb0, dlb1])[:, :, 0, :]
    dlb_re = dlb[:, :nj].reshape(2, n_groups, SSM_STATE)
    dlb_im = dlb[:, nj:].reshape(2, n_groups, SSM_STATE)
    gpn = (2, 2, n_groups, SSM_GROUP, SSM_STATE)
    dbb = jnp.stack([dbb0, dbb1]).reshape(gpn)
    dcc = jnp.stack([dcc0, dcc1]).reshape(gpn)
    d_are, d_aim, d_logdt, d_btre, d_btim = _ssm_param_bwd(a_re, a_im, log_dt, bt_re, bt_im, dlb_re, dlb_im,
                                                           dbb[:, 0], dbb[:, 1])
    small_grads = {
        "norm_mix_g": d_gmix, "ssm_a_re": d_are, "ssm_a_im": d_aim, "ssm_log_dt": d_logdt,
        "ssm_b_re": jnp.swapaxes(d_btre, 2, 3), "ssm_b_im": jnp.swapaxes(d_btim, 2, 3),
        "ssm_c_re": dcc[:, 0], "ssm_c_im": -dcc[:, 1],
        "ssm_d": d_dskip, "b_glu": d_bglu, "q_norm_g": d_qg[:, :HEAD_DIM], "k_norm_g": d_kg[:, :HEAD_DIM],
        "norm_mlp_g": d_gmlp, "norm_final_g": d_gfin,
    }
    small_flat = jnp.concatenate([small_grads[n].reshape(-1) for n in SMALL] + [loss_tile[0, :1]])

    last = _Reduction([grads4["meta_tokens"], grads4["w_in"], _small_as_shards(small_flat)], [F32, BF16, F32],
                      ["meta_tokens", "w_in", "small"], c_idx, "last")
    (red_big["meta_tokens"], red_big["w_in"]), (small_piece,) = last.finish(
        _chip_scatter(last.pair, "grad_chip_scatter"), 2)
    red_small = _all_gather8([small_piece], "small_grad_all_gather")[0].reshape(-1)[:small_flat.shape[0]]
    loss, red_small = red_small[-1], red_small[:-1]
    grad, delta_w, new_m, new_v = {}, {}, {}, {}
    for name, _ in BIG:
        w2 = shard2d[name]
        shp = args[name].shape
        g2 = red_big[name]
        t = _pick_tile(w2.shape[0], 256, 8)
        dl, nm, nv = _adamw(w2, g2, args["m_" + name].reshape(w2.shape), args["v_" + name].reshape(w2.shape), t)
        grad[name], delta_w[name], new_m[name], new_v[name] = (a.reshape(shp) for a in (g2, dl, nm, nv))

    def pack_small(prefix):
        flat = jnp.concatenate([args[prefix + n].reshape(-1) for n in SMALL])
        return _pad_rows(flat, SUBLANES)

    n_small = red_small.shape[0]
    gs = _pad_rows(red_small, SUBLANES)
    dl, nm, nv = _adamw(pack_small(""), gs, pack_small("m_"), pack_small("v_"), _pick_tile(gs.shape[0], 256, 8))
    off = 0
    for name in SMALL:
        shp = args[name].shape
        k = int(np.prod(shp))
        for dst, src in ((grad, gs), (delta_w, dl), (new_m, nm), (new_v, nv)):
            dst[name] = src.reshape(-1)[off:off + k].reshape(shp)
        off += k
    assert off == n_small

    return (loss, grad_x, *[grad[n] for n in order], *[delta_w[n] for n in order],
            *[new_m[n] for n in order], *[new_v[n] for n in order])


def _both(tab, tab_adj):
    return jnp.concatenate([tab, tab_adj], axis=0)
```

```python
import functools
import math

import numpy as np
import jax
import jax.numpy as jnp
from jax import lax
from jax.experimental import pallas as pl
from jax.experimental.pallas import tpu as pltpu

F32 = jnp.float32
BF16 = jnp.bfloat16

N_META = 16
GRID_W = 64
HEAD_DIM = 64
GQA_REP = 4
SSM_GROUP = 16
SSM_STATE = 64
ROPE_THETA = 10000.0
NORM_EPS = 1e-6
EIG_RE_MAX = -1e-4
ADAM_LR, ADAM_B1, ADAM_B2, ADAM_EPS, ADAM_WD, ADAM_STEP = 0.001, 0.9, 0.999, 1e-08, 0.01, 10

SUBLANES = 8
LANES = 128
CHUNK = 256
KSTEPS = CHUNK // SUBLANES
SCAN_LANES = 512
MXU_DIM = 256
SSM_BLOCK = MXU_DIM
SEQ_ALIGN = MXU_DIM
ATTN_Q_TILE = 384
ATTN_K_TILE = 11 * MXU_DIM
VMEM_LIMIT = 56 << 20
MASK_VALUE = -1e30
MESH_ID = pl.DeviceIdType.MESH


def _dot(a, b):
    return jnp.dot(a, b, preferred_element_type=F32)


def _dot_nt(a, b):
    return lax.dot_general(a, b, (((1,), (1,)), ((), ())), preferred_element_type=F32)


def _dot_tn(a, b):
    return lax.dot_general(a, b, (((0,), (0,)), ((), ())), preferred_element_type=F32)


def _row(tm, width):
    return pl.BlockSpec((tm, width), lambda i: (i, 0))


def _full(shape):
    nd = len(shape)
    return pl.BlockSpec(shape, lambda i: (0,) * nd)


def _params(sem):
    return pltpu.CompilerParams(dimension_semantics=sem, vmem_limit_bytes=VMEM_LIMIT)


def _pick_tile(n, cap, mult=16):
    best = None
    for t in range(mult, min(n, cap) + 1, mult):
        if n % t == 0:
            best = t
    assert best is not None, (n, cap)
    return best


def _rstd(x):
    return lax.rsqrt(jnp.mean(x * x, axis=-1, keepdims=True) + NORM_EPS)


def _rms(x, g):
    return x * _rstd(x) * g


def _rms_bwd(dy, x, g):
    r = _rstd(x)
    xh = x * r
    gdy = dy * g
    dx = r * (gdy - xh * jnp.mean(gdy * xh, axis=-1, keepdims=True))
    return dx, dy * xh


def _split_dot(x, m):
    hi = x.astype(BF16)
    lo = (x - hi.astype(F32)).astype(BF16)
    return _dot(hi, m) + _dot(lo, m)


def _sigmoid(x):
    return 1.0 / (1.0 + jnp.exp(-x))


def _acc_rows(ref, val, first):
    s = jnp.sum(val, axis=0, keepdims=True)

    @pl.when(first)
    def _():
        ref[...] = s

    @pl.when(jnp.logical_not(first))
    def _():
        ref[...] += s


def _in_proj(xin, g, w4, tm):
    lp, d = xin.shape
    hd = d // 2

    def body(x_ref, g_ref, w_ref, u_ref, qkv_ref, gt_ref):
        h = _rms(x_ref[...], g_ref[...]).astype(BF16)
        p0 = _dot(h, w_ref[0])
        u_ref[...] = p0[:, :hd]
        qkv_ref[:, :hd] = p0[:, hd:]
        qkv_ref[:, hd:] = _dot(h, w_ref[1])
        gt_ref[:, :d] = _dot(h, w_ref[2])
        gt_ref[:, d:] = _dot(h, w_ref[3])

    return pl.pallas_call(
        body, name="in_proj", grid=(lp // tm,),
        in_specs=[_row(tm, d), _full((1, d)), _full((4, d, d))],
        out_specs=[_row(tm, hd), _row(tm, 3 * hd), _row(tm, 2 * d)],
        out_shape=[jax.ShapeDtypeStruct((lp, hd), F32), jax.ShapeDtypeStruct((lp, 3 * hd), F32),
                   jax.ShapeDtypeStruct((lp, 2 * d), F32)],
        compiler_params=_params(("parallel",)),
    )(xin, g, w4)


def _gelu(y):
    return 0.5 * y * (1.0 + lax.erf(y * (1.0 / math.sqrt(2.0))))


def _gelu_grad(y):
    return 0.5 * (1.0 + lax.erf(y * (1.0 / math.sqrt(2.0)))) + y * jnp.exp(-0.5 * y * y) * (1.0 / math.sqrt(2.0 * math.pi))


def _glu_fwd(u, y0, y1, dskip, w_glu, b_glu, tm):
    lp, w = u.shape

    def body(u_ref, y0_ref, y1_ref, d_ref, w_ref, b_ref, o_ref):
        y = u_ref[...] * d_ref[...] + y0_ref[...] + y1_ref[...]
        z = _gelu(y)
        t = _dot(z.astype(BF16), w_ref[...]) + b_ref[...]
        o_ref[...] = (z * _sigmoid(t)).astype(BF16)

    return pl.pallas_call(
        body, name="glu_fwd", grid=(lp // tm,),
        in_specs=[_row(tm, w), _row(tm, w), _row(tm, w), _full((1, w)), _full((w, w)), _full((1, w))],
        out_specs=_row(tm, w), out_shape=jax.ShapeDtypeStruct((lp, w), BF16),
        compiler_params=_params(("parallel",)),
    )(u, y0, y1, dskip, w_glu, b_glu)


def _glu_bwd(dyssm, u, y0, y1, dskip, w_glu, b_glu, tm):
    lp, w = u.shape

    def body(g_ref, u_ref, y0_ref, y1_ref, d_ref, w_ref, b_ref, dy_ref, dw_ref, db_ref, dd_ref):
        first = pl.program_id(0) == 0
        uu = u_ref[...]
        y = uu * d_ref[...] + y0_ref[...] + y1_ref[...]
        z = _gelu(y)
        zb = z.astype(BF16)
        sg = _sigmoid(_dot(zb, w_ref[...]) + b_ref[...])
        g = g_ref[...]
        dt = g * z * sg * (1.0 - sg)
        dtb = dt.astype(BF16)
        dz = g * sg + _dot_nt(dtb, w_ref[...])
        dy = dz * _gelu_grad(y)
        dy_ref[...] = dy
        dw = _dot_tn(zb, dtb)

        @pl.when(first)
        def _():
            dw_ref[...] = dw

        @pl.when(jnp.logical_not(first))
        def _():
            dw_ref[...] += dw

        _acc_rows(db_ref, dt, first)
        _acc_rows(dd_ref, dy * uu, first)

    return pl.pallas_call(
        body, name="glu_bwd", grid=(lp // tm,),
        in_specs=[_row(tm, w), _row(tm, w), _row(tm, w), _row(tm, w), _full((1, w)), _full((w, w)), _full((1, w))],
        out_specs=[_row(tm, w), _full((w, w)), _full((1, w)), _full((1, w))],
        out_shape=[jax.ShapeDtypeStruct((lp, w), F32), jax.ShapeDtypeStruct((w, w), F32),
                   jax.ShapeDtypeStruct((1, w), F32), jax.ShapeDtypeStruct((1, w), F32)],
        compiler_params=_params(("arbitrary",)),
    )(dyssm, u, y0, y1, dskip, w_glu, b_glu)


def _merge_fwd(yssm, o, gates, xin, wsp4, wap, wo, tm):
    lp, d = xin.shape
    w = yssm.shape[1]
    ns = d // 4

    def body(y_ref, o_ref, g_ref, x_ref, wsp_ref, wap_ref, wo_ref, h_ref, m_ref):
        yb = y_ref[...]
        ms = jnp.concatenate([_dot(yb, wsp_ref[s]) for s in range(4)], axis=1)
        ma = _dot(o_ref[...], wap_ref[...])
        merged = (_sigmoid(g_ref[:, :d]) * ms + _sigmoid(g_ref[:, d:]) * ma).astype(BF16)
        m_ref[...] = merged
        h_ref[...] = x_ref[...] + _dot(merged, wo_ref[...])

    return pl.pallas_call(
        body, name="merge_fwd", grid=(lp // tm,),
        in_specs=[_row(tm, w), _row(tm, d), _row(tm, 2 * d), _row(tm, d),
                  _full((4, w, ns)), _full((d, d)), _full((d, d))],
        out_specs=[_row(tm, d), _row(tm, d)],
        out_shape=[jax.ShapeDtypeStruct((lp, d), F32), jax.ShapeDtypeStruct((lp, d), BF16)],
        compiler_params=_params(("parallel",)),
    )(yssm, o, gates, xin, wsp4, wap, wo)


def _merge_bwd(dh1, yssm, o, gates, wsp4, wap, wo, sel, tm):
    lp, d = dh1.shape
    w = yssm.shape[1]
    ns = d // 4
    nsel = sel.shape[1]

    def body(dh_ref, y_ref, o_ref, g_ref, wsp_ref, wap_ref, wo_ref, sel_ref,
             dg_ref, dms_ref, dma_ref, dy_ref, do_ref, dl_ref, dhb_ref):
        dhb = dh_ref[...].astype(BF16)
        dhb_ref[...] = dhb
        dm = _dot_nt(dhb, wo_ref[...])
        yb = y_ref[...]
        ob = o_ref[...]
        ms = jnp.concatenate([_dot(yb, wsp_ref[s]) for s in range(4)], axis=1)
        ma = _dot(ob, wap_ref[...])
        ss = _sigmoid(g_ref[:, :d])
        sa = _sigmoid(g_ref[:, d:])
        dg_ref[:, :d] = dm * ms * ss * (1.0 - ss)
        dg_ref[:, d:] = dm * ma * sa * (1.0 - sa)
        dms = (dm * ss).astype(BF16)
        dma = (dm * sa).astype(BF16)
        dms_ref[...] = dms
        dma_ref[...] = dma
        dy = _dot_nt(dms[:, :ns], wsp_ref[0])
        for s in range(1, 4):
            dy += _dot_nt(dms[:, s * ns:(s + 1) * ns], wsp_ref[s])
        dy_ref[...] = dy
        do = _dot_nt(dma, wap_ref[...])
        do_ref[...] = do.astype(BF16)
        dl_ref[...] = _split_dot(do * ob.astype(F32), sel_ref[...])

    return pl.pallas_call(
        body, name="merge_bwd", grid=(lp // tm,),
        in_specs=[_row(tm, d), _row(tm, w), _row(tm, d), _row(tm, 2 * d),
                  _full((4, w, ns)), _full((d, d)), _full((d, d)), _full((d, nsel))],
        out_specs=[_row(tm, 2 * d), _row(tm, d), _row(tm, d), _row(tm, w), _row(tm, d), _row(tm, nsel), _row(tm, d)],
        out_shape=[jax.ShapeDtypeStruct((lp, 2 * d), F32), jax.ShapeDtypeStruct((lp, d), BF16),
                   jax.ShapeDtypeStruct((lp, d), BF16), jax.ShapeDtypeStruct((lp, w), F32),
                   jax.ShapeDtypeStruct((lp, d), BF16), jax.ShapeDtypeStruct((lp, nsel), F32),
                   jax.ShapeDtypeStruct((lp, d), BF16)],
        compiler_params=_params(("parallel",)),
    )(dh1, yssm, o, gates, wsp4, wap, wo, sel)


def _mlp_in(h1, g, w4, tm):
    lp, d = h1.shape

    def body(x_ref, g_ref, w_ref, r_ref):
        h = _rms(x_ref[...], g_ref[...]).astype(BF16)
        for s in range(4):
            r_ref[:, s * d:(s + 1) * d] = jnp.maximum(_dot(h, w_ref[s]), 0.0).astype(BF16)

    return pl.pallas_call(
        body, name="mlp_in", grid=(lp // tm,),
        in_specs=[_row(tm, d), _full((1, d)), _full((4, d, d))],
        out_specs=_row(tm, 4 * d), out_shape=jax.ShapeDtypeStruct((lp, 4 * d), BF16),
        compiler_params=_params(("parallel",)),
    )(h1, g, w4)


def _square_bf16(r):
    rf = r.astype(F32)
    return (rf * rf).astype(BF16)


def _mlp_out(h1, r, w2, tm):
    lp, d = h1.shape
    ff = r.shape[1]

    def body(x_ref, r_ref, w_ref, o_ref):
        o_ref[...] = x_ref[...] + _dot(_square_bf16(r_ref[...]), w_ref[...])

    return pl.pallas_call(
        body, name="mlp_out", grid=(lp // tm,),
        in_specs=[_row(tm, d), _row(tm, ff), _full((ff, d))],
        out_specs=_row(tm, d), out_shape=jax.ShapeDtypeStruct((lp, d), F32),
        compiler_params=_params(("parallel",)),
    )(h1, r, w2)


def _final_loss(h3, g, tgt, rowmask, tm):
    lp, d = h3.shape

    def body(x_ref, g_ref, t_ref, m_ref, loss_ref, dx_ref, dg_ref):
        first = pl.program_id(0) == 0
        x = x_ref[...]
        gg = g_ref[...]
        err = (_rms(x, gg) - t_ref[...]) * m_ref[...]
        part = 0.5 * jnp.sum(jnp.sum(err * err, axis=1, keepdims=True), axis=0, keepdims=True) * (1.0 / d)
        part = jnp.broadcast_to(part, (SUBLANES, LANES))

        @pl.when(first)
        def _():
            loss_ref[...] = part

        @pl.when(jnp.logical_not(first))
        def _():
            loss_ref[...] += part

        dx, dgr = _rms_bwd(err * (1.0 / d), x, gg)
        dx_ref[...] = dx
        _acc_rows(dg_ref, dgr, first)

    return pl.pallas_call(
        body, name="final_loss", grid=(lp // tm,),
        in_specs=[_row(tm, d), _full((1, d)), _row(tm, d), _row(tm, 1)],
        out_specs=[_full((SUBLANES, LANES)), _row(tm, d), _full((1, d))],
        out_shape=[jax.ShapeDtypeStruct((SUBLANES, LANES), F32), jax.ShapeDtypeStruct((lp, d), F32),
                   jax.ShapeDtypeStruct((1, d), F32)],
        compiler_params=_params(("arbitrary",)),
    )(h3, g, tgt, rowmask)


def _mlp_bwd_a(dh3, r, w2, tm):
    lp, d = dh3.shape
    ff = r.shape[1]

    def body(dh_ref, r_ref, w_ref, dz_ref, dhb_ref):
        dhb = dh_ref[...].astype(BF16)
        dhb_ref[...] = dhb
        da = _dot_nt(dhb, w_ref[...])
        dz_ref[...] = (da * (2.0 * r_ref[...].astype(F32))).astype(BF16)

    return pl.pallas_call(
        body, name="mlp_bwd_a", grid=(lp // tm,),
        in_specs=[_row(tm, d), _row(tm, ff), _full((ff, d))],
        out_specs=[_row(tm, ff), _row(tm, d)],
        out_shape=[jax.ShapeDtypeStruct((lp, ff), BF16), jax.ShapeDtypeStruct((lp, d), BF16)],
        compiler_params=_params(("parallel",)),
    )(dh3, r, w2)


def _mlp_bwd_b(dz, dh3, h1, g, w4, tm):
    lp, d = h1.shape

    def body(dz_ref, dh_ref, x_ref, g_ref, w_ref, dx_ref, dg_ref):
        first = pl.program_id(0) == 0
        dh2 = _dot_nt(dz_ref[:, :d], w_ref[0])
        for s in range(1, 4):
            dh2 += _dot_nt(dz_ref[:, s * d:(s + 1) * d], w_ref[s])
        dx, dgr = _rms_bwd(dh2, x_ref[...], g_ref[...])
        dx_ref[...] = dh_ref[...] + dx
        _acc_rows(dg_ref, dgr, first)

    return pl.pallas_call(
        body, name="mlp_bwd_b", grid=(lp // tm,),
        in_specs=[_row(tm, 4 * d), _row(tm, d), _row(tm, d), _full((1, d)), _full((4, d, d))],
        out_specs=[_row(tm, d), _full((1, d))],
        out_shape=[jax.ShapeDtypeStruct((lp, d), F32), jax.ShapeDtypeStruct((1, d), F32)],
        compiler_params=_params(("arbitrary",)),
    )(dz, dh3, h1, g, w4)


def _in_proj_bwd(dyv, du0, du1, dskip, dqkv, dgates, dres, xin, g, w4, tm):
    lp, d = xin.shape
    hd = d // 2

    def body(dy_ref, a_ref, b_ref, ds_ref, dq_ref, dgt_ref, dr_ref, x_ref, g_ref, w_ref, dx_ref, dg_ref, dp_ref):
        first = pl.program_id(0) == 0
        du = (dy_ref[...] * ds_ref[...] + a_ref[...] + b_ref[...]).astype(BF16)
        dq = dq_ref[...].astype(BF16)
        dgt = dgt_ref[...].astype(BF16)
        dp_ref[:, :hd] = du
        dp_ref[:, hd:2 * d] = dq
        dp_ref[:, 2 * d:] = dgt
        dh = _dot_nt(du, w_ref[0, :, :hd]) + _dot_nt(dq[:, :hd], w_ref[0, :, hd:])
        dh += _dot_nt(dq[:, hd:], w_ref[1])
        dh += _dot_nt(dgt[:, :d], w_ref[2]) + _dot_nt(dgt[:, d:], w_ref[3])
        dx, dgr = _rms_bwd(dh, x_ref[...], g_ref[...])
        dx_ref[...] = dr_ref[...] + dx
        _acc_rows(dg_ref, dgr, first)

    return pl.pallas_call(
        body, name="in_proj_bwd", grid=(lp // tm,),
        in_specs=[_row(tm, hd), _row(tm, hd), _row(tm, hd), _full((1, hd)), _row(tm, 3 * hd), _row(tm, 2 * d),
                  _row(tm, d), _row(tm, d), _full((1, d)), _full((4, d, d))],
        out_specs=[_row(tm, d), _full((1, d)), _row(tm, 4 * d)],
        out_shape=[jax.ShapeDtypeStruct((lp, d), F32), jax.ShapeDtypeStruct((1, d), F32),
                   jax.ShapeDtypeStruct((lp, 4 * d), BF16)],
        compiler_params=_params(("arbitrary",)),
    )(dyv, du0, du1, dskip, dqkv, dgates, dres, xin, g, w4)


def _wgrad(a, dy, nshard, tm, tn, name, gain=None, square=False):
    lp, k = a.shape
    n = dy.shape[1]
    ns = n // nshard
    assert ns % tn == 0
    per = ns // tn

    def body(*refs):
        if gain is not None:
            a_ref, g_ref, dy_ref, o_ref = refs
            at = _rms(a_ref[...], g_ref[...]).astype(BF16)
        else:
            a_ref, dy_ref, o_ref = refs
            at = _square_bf16(a_ref[...]) if square else a_ref[...]
        i = pl.program_id(1)
        acc = _dot_tn(at, dy_ref[...])

        @pl.when(i == 0)
        def _():
            o_ref[0] = acc

        @pl.when(i != 0)
        def _():
            o_ref[0] += acc

    in_specs = [pl.BlockSpec((tm, k), lambda j, i: (i, 0))]
    args = [a]
    if gain is not None:
        in_specs.append(pl.BlockSpec((1, k), lambda j, i: (0, 0)))
        args.append(gain)
    in_specs.append(pl.BlockSpec((tm, tn), lambda j, i: (i, j)))
    args.append(dy)
    return pl.pallas_call(
        body, name=name, grid=(n // tn, lp // tm), in_specs=in_specs,
        out_specs=pl.BlockSpec((1, k, tn), lambda j, i: (j // per, 0, j % per)),
        out_shape=jax.ShapeDtypeStruct((nshard, k, ns), F32),
        compiler_params=_params(("parallel", "arbitrary")),
    )(*args)


def _head_tables(d):
    idx = np.arange(LANES)
    mean = (idx[:, None] // HEAD_DIM == idx[None, :] // HEAD_DIM).astype(np.float32) / HEAD_DIM
    n_heads = d // HEAD_DIM
    kvh = n_heads // GQA_REP
    c = np.arange(d)
    col = np.arange(kvh * LANES)
    head_of_col = (col // LANES) * GQA_REP + (col % LANES)
    sel = ((c[:, None] // HEAD_DIM == head_of_col[None, :]) & ((col % LANES) < GQA_REP)[None, :]).astype(np.float32)
    return jnp.asarray(mean, BF16), jnp.asarray(sel, BF16)


def _swap_pairs(y):
    lane = lax.broadcasted_iota(jnp.int32, y.shape, 1)
    return jnp.where(lane % 2 == 0, pltpu.roll(y, LANES - 1, 1), pltpu.roll(y, 1, 1))


def _qk_prep(qkv, cos_t, sin_t, qg, kg, mean_m, tm):
    lp, wq = qkv.shape
    d = wq * 2 // 3
    kvw = d // 4
    kvh = kvw // HEAD_DIM
    scale = HEAD_DIM ** -0.5

    def body(x_ref, c_ref, s_ref, qg_ref, kg_ref, m_ref, q_ref, k_ref, v_ref):
        cs, sn, mm = c_ref[...], s_ref[...], m_ref[...]
        for b in range((d + kvw) // LANES):
            x = x_ref[:, b * LANES:(b + 1) * LANES]
            gg = qg_ref[...] if b < d // LANES else kg_ref[...]
            y = x * lax.rsqrt(_split_dot(x * x, mm) + NORM_EPS) * gg
            out = y * cs + _swap_pairs(y) * sn
            if b < d // LANES:
                q_ref[:, b * LANES:(b + 1) * LANES] = (out * scale).astype(BF16)
            else:
                kb = b - d // LANES
                k_ref[2 * kb] = out[:, :HEAD_DIM].astype(BF16)
                k_ref[2 * kb + 1] = out[:, HEAD_DIM:].astype(BF16)
        lane = lax.broadcasted_iota(jnp.int32, (tm, LANES - HEAD_DIM), 1)
        ones_col = (lane == 0).astype(BF16)
        for h in range(kvh):
            vh = x_ref[:, d + kvw + h * HEAD_DIM:d + kvw + (h + 1) * HEAD_DIM].astype(BF16)
            v_ref[h] = jnp.concatenate([vh, ones_col], axis=1)

    k_spec = pl.BlockSpec((kvh, tm, HEAD_DIM), lambda i: (0, i, 0))
    v_spec = pl.BlockSpec((kvh, tm, LANES), lambda i: (0, i, 0))
    return pl.pallas_call(
        body, name="qk_prep", grid=(lp // tm,),
        in_specs=[_row(tm, wq), _row(tm, LANES), _row(tm, LANES), _full((1, LANES)), _full((1, LANES)),
                  _full((LANES, LANES))],
        out_specs=[_row(tm, d), k_spec, v_spec],
        out_shape=[jax.ShapeDtypeStruct((lp, d), BF16), jax.ShapeDtypeStruct((kvh, lp, HEAD_DIM), BF16),
                   jax.ShapeDtypeStruct((kvh, lp, LANES), BF16)],
        compiler_params=_params(("parallel",)),
    )(qkv, cos_t, sin_t, qg, kg, mean_m)


def _qk_bwd(qkv, dq, dk, dv, cos_t, sin_t, qg, kg, mean_m, tm):
    lp, wq = qkv.shape
    d = wq * 2 // 3
    kvw = d // 4
    kvh = kvw // HEAD_DIM
    scale = HEAD_DIM ** -0.5

    def body(x_ref, dq_ref, dk_ref, dv_ref, c_ref, s_ref, qg_ref, kg_ref, m_ref, o_ref, dqg_ref, dkg_ref):
        first = pl.program_id(0) == 0
        cs, sn, mm = c_ref[...], s_ref[...], m_ref[...]
        sums = [None, None]
        for b in range((d + kvw) // LANES):
            is_q = b < d // LANES
            x = x_ref[:, b * LANES:(b + 1) * LANES]
            gg = qg_ref[...] if is_q else kg_ref[...]
            r = lax.rsqrt(_split_dot(x * x, mm) + NORM_EPS)
            nrm = x * r
            if is_q:
                dout = dq_ref[:, b * LANES:(b + 1) * LANES] * scale
            else:
                kb = b - d // LANES
                dout = jnp.concatenate([dk_ref[2 * kb], dk_ref[2 * kb + 1]], axis=1)
            dy = dout * cs + _swap_pairs(dout * sn)
            part = jnp.sum(dy * nrm, axis=0, keepdims=True)
            sums[0 if is_q else 1] = part if sums[0 if is_q else 1] is None else sums[0 if is_q else 1] + part
            dn = dy * gg
            o_ref[:, b * LANES:(b + 1) * LANES] = r * (dn - nrm * _split_dot(dn * nrm, mm))
        for h in range(kvh):
            o_ref[:, d + kvw + h * HEAD_DIM:d + kvw + (h + 1) * HEAD_DIM] = dv_ref[h]
        for ref, s in ((dqg_ref, sums[0]), (dkg_ref, sums[1])):
            s = s + pltpu.roll(s, HEAD_DIM, 1)

            @pl.when(first)
            def _(ref=ref, s=s):
                ref[...] = s

            @pl.when(jnp.logical_not(first))
            def _(ref=ref, s=s):
                ref[...] += s

    kv_spec = pl.BlockSpec((kvh, tm, HEAD_DIM), lambda i: (0, i, 0))
    return pl.pallas_call(
        body, name="qk_bwd", grid=(lp // tm,),
        in_specs=[_row(tm, wq), _row(tm, d), kv_spec, kv_spec, _row(tm, LANES), _row(tm, LANES),
                  _full((1, LANES)), _full((1, LANES)), _full((LANES, LANES))],
        out_specs=[_row(tm, wq), _full((1, LANES)), _full((1, LANES))],
        out_shape=[jax.ShapeDtypeStruct((lp, wq), F32), jax.ShapeDtypeStruct((1, LANES), F32),
                   jax.ShapeDtypeStruct((1, LANES), F32)],
        compiler_params=_params(("arbitrary",)),
    )(qkv, dq, dk, dv, cos_t, sin_t, qg, kg, mean_m)


def _attn_fwd(q, k, v, kbias, tq, tk, gather=()):
    lp, d = q.shape
    kvh = k.shape[0]
    rw = GQA_REP * HEAD_DIM
    nk = lp // tk

    ng = len(gather)
    steps = kvh * (lp // tq) * nk

    def body(*refs):
        q_ref, k_ref, v_ref, kb_ref = refs[:4]
        o_ref, lse_ref, pt_ref, mb_ref = refs[4 + ng:8 + ng]
        m_s, acc_s = refs[8 + 2 * ng:10 + 2 * ng]
        j = pl.program_id(2)

        if ng:
            phases = _gather_phases(refs[4:4 + ng], refs[8 + ng:8 + 2 * ng], *refs[10 + 2 * ng:])
            step = (pl.program_id(0) * (lp // tq) + pl.program_id(1)) * nk + j
            for n, phase in enumerate(phases):
                pl.when(step == n * steps // 3)(phase)

        @pl.when(j == 0)
        def _():
            m_s[...] = jnp.full(m_s.shape, MASK_VALUE, F32)
            acc_s[...] = jnp.zeros(acc_s.shape, F32)

        def heads(masked):
            kk, vv = k_ref[0], v_ref[0]

            def scores(h):
                return _dot_nt(q_ref[:, h * HEAD_DIM:(h + 1) * HEAD_DIM], kk)

            def softmax(h, s):
                if masked:
                    s = s + kb_ref[...]
                m_prev = m_s[h]
                m_new = jnp.maximum(m_prev, jnp.max(s, axis=1, keepdims=True))
                m_s[h] = m_new
                p = jnp.exp(s - m_new[:, :1]).astype(BF16)
                pt_ref[h] = p
                return p, jnp.exp(m_prev - m_new), m_new

            def accumulate(h, p, alpha):
                acc_s[h] = acc_s[h] * alpha + _dot(p, vv)

            ss = [scores(h) for h in range(GQA_REP)]
            pa = [softmax(h, ss[h]) for h in range(GQA_REP)]
            for h in range(GQA_REP):
                accumulate(h, *pa[h][:2])
            lane = lax.broadcasted_iota(jnp.int32, (tq, LANES), 1)
            mb = jnp.zeros((tq, LANES), F32)
            for h in range(GQA_REP):
                mb = jnp.where(lane == h, pa[h][2], mb)
            mb_ref[0] = mb

        pl.when(j != nk - 1)(functools.partial(heads, False))
        pl.when(j == nk - 1)(functools.partial(heads, True))

        @pl.when(j == nk - 1)
        def _():
            lane = lax.broadcasted_iota(jnp.int32, (tq, LANES), 1)
            lse = jnp.zeros((tq, LANES), F32)
            outs = []
            for h in range(GQA_REP):
                acc = acc_s[h]
                l = acc[:, HEAD_DIM:HEAD_DIM + 1]
                outs.append(acc[:, :HEAD_DIM] / l)
                lse = jnp.where(lane == h, m_s[h][:, :1] + jnp.log(l), lse)
            o_ref[...] = jnp.concatenate(outs, axis=1).astype(BF16)
            lse_ref[...] = lse

    sems = [pltpu.SemaphoreType.DMA((ng, 7)), pltpu.SemaphoreType.DMA((ng, 7)), pltpu.SemaphoreType.DMA((ng,))]
    res = pl.pallas_call(
        body, name="attn_fwd", grid=(kvh, lp // tq, nk),
        in_specs=[pl.BlockSpec((tq, rw), lambda g, i, j: (i, g)),
                  pl.BlockSpec((1, tk, HEAD_DIM), lambda g, i, j: (g, j, 0)),
                  pl.BlockSpec((1, tk, LANES), lambda g, i, j: (g, j, 0)),
                  pl.BlockSpec((1, tk), lambda g, i, j: (0, j))] + [_ANY] * ng,
        out_specs=[pl.BlockSpec((tq, rw), lambda g, i, j: (i, g)),
                   pl.BlockSpec((tq, LANES), lambda g, i, j: (i, g)),
                   pl.BlockSpec((GQA_REP, tq, tk), lambda g, i, j: (g, i, j)),
                   pl.BlockSpec((1, tq, LANES), lambda g, i, j: (j, i, g))] + [_ANY] * ng,
        out_shape=[jax.ShapeDtypeStruct((lp, d), BF16), jax.ShapeDtypeStruct((lp, kvh * LANES), F32),
                   jax.ShapeDtypeStruct((kvh * GQA_REP, lp, lp), BF16), jax.ShapeDtypeStruct((nk, lp, kvh * LANES), F32)]
        + [jax.ShapeDtypeStruct((8,) + b.shape, b.dtype) for b in gather],
        scratch_shapes=[pltpu.VMEM((GQA_REP, tq, LANES), F32), pltpu.VMEM((GQA_REP, tq, LANES), F32)]
        + (sems if ng else []),
        compiler_params=_params(("arbitrary", "arbitrary", "arbitrary")),
    )(q, k, v, kbias, *gather)
    return res[0], res[1], res[2], res[3], list(res[4:])


def _attn_bwd(q, k, v, pt, mblk, do, lse, delta, tq, tk):
    lp, d = q.shape
    kvh = k.shape[0]
    rw = GQA_REP * HEAD_DIM
    nq = lp // tq

    def body(q_ref, k_ref, v_ref, pt_ref, mb_ref, do_ref, lse_ref, dl_ref, dq_ref, dk_ref, dv_ref, dk_s, dv_s):
        j = pl.program_id(1)
        i = pl.program_id(2)

        @pl.when(jnp.logical_and(i == 0, j == 0))
        def _():
            dq_ref[...] = jnp.zeros(dq_ref.shape, F32)

        @pl.when(i == 0)
        def _():
            dk_s[...] = jnp.zeros(dk_s.shape, F32)
            dv_s[...] = jnp.zeros(dv_s.shape, F32)

        kk, vv = k_ref[0], v_ref[0][:, :HEAD_DIM]
        scale = jnp.exp(mb_ref[0] - lse_ref[...])
        dl = dl_ref[...] * scale
        dqs = []
        for pair in ((0, 1), (2, 3)):
            dos = {h: (do_ref[:, h * HEAD_DIM:(h + 1) * HEAD_DIM].astype(F32) * scale[:, h:h + 1]).astype(BF16)
                   for h in pair}
            dps = {h: _dot_nt(dos[h], vv) for h in pair}
            for h in pair:
                dv_s[...] += _dot_tn(pt_ref[h], dos[h])
            dss = {h: (pt_ref[h].astype(F32) * (dps[h] - dl[:, h:h + 1])).astype(BF16) for h in pair}
            for h in pair:
                dk_s[...] += _dot_tn(dss[h], q_ref[:, h * HEAD_DIM:(h + 1) * HEAD_DIM])
                dqs.append(_dot(dss[h], kk))
        rows = pl.ds(pl.multiple_of(i * tq, tq), tq)
        dq_ref[rows, :] += jnp.concatenate(dqs, axis=1)

        @pl.when(i == nq - 1)
        def _():
            dk_ref[0] = dk_s[...]
            dv_ref[0] = dv_s[...]

    return pl.pallas_call(
        body, name="attn_bwd", grid=(kvh, lp // tk, nq),
        in_specs=[pl.BlockSpec((tq, rw), lambda g, j, i: (i, g)),
                  pl.BlockSpec((1, tk, HEAD_DIM), lambda g, j, i: (g, j, 0)),
                  pl.BlockSpec((1, tk, LANES), lambda g, j, i: (g, j, 0)),
                  pl.BlockSpec((GQA_REP, tq, tk), lambda g, j, i: (g, i, j)),
                  pl.BlockSpec((1, tq, LANES), lambda g, j, i: (j, i, g)),
                  pl.BlockSpec((tq, rw), lambda g, j, i: (i, g)),
                  pl.BlockSpec((tq, LANES), lambda g, j, i: (i, g)),
                  pl.BlockSpec((tq, LANES), lambda g, j, i: (i, g))],
        out_specs=[pl.BlockSpec((lp, rw), lambda g, j, i: (0, g)),
                   pl.BlockSpec((1, tk, HEAD_DIM), lambda g, j, i: (g, j, 0)),
                   pl.BlockSpec((1, tk, HEAD_DIM), lambda g, j, i: (g, j, 0))],
        out_shape=[jax.ShapeDtypeStruct((lp, d), F32), jax.ShapeDtypeStruct((kvh, lp, HEAD_DIM), F32),
                   jax.ShapeDtypeStruct((kvh, lp, HEAD_DIM), F32)],
        scratch_shapes=[pltpu.VMEM((tk, HEAD_DIM), F32), pltpu.VMEM((tk, HEAD_DIM), F32)],
        compiler_params=_params(("parallel", "arbitrary", "arbitrary")),
    )(q, k, v, pt, mblk, do, lse, delta)


def _ssm_math(a_re, a_im, log_dt, bt_re, bt_im):
    dt = jnp.exp(log_dt)
    lam_re = jnp.minimum(a_re, EIG_RE_MAX)
    lam_im = a_im
    mag = jnp.exp(lam_re * dt)
    ang = lam_im * dt
    lb_re = mag * jnp.cos(ang)
    lb_im = mag * jnp.sin(ang)
    num_re = lb_re - 1.0
    num_im = lb_im
    den = lam_re * lam_re + lam_im * lam_im
    f_re = (num_re * lam_re + num_im * lam_im) / den
    f_im = (num_im * lam_re - num_re * lam_im) / den
    bb_re = f_re[:, None, :] * bt_re - f_im[:, None, :] * bt_im
    bb_im = f_re[:, None, :] * bt_im + f_im[:, None, :] * bt_re
    return lb_re, lb_im, bb_re, bb_im


def _ssm_discretize(a_re, a_im, log_dt, bt_re, bt_im):
    nd, g, n = a_re.shape
    p = bt_re.shape[2]

    def body(ar_ref, ai_ref, ld_ref, br_ref, bi_ref, bbr_ref, bbi_ref, pr_ref, pi_ref, hr_ref, hi_ref):
        lb_re, lb_im, bb_re, bb_im = _ssm_math(ar_ref[0], ai_ref[0], ld_ref[0], br_ref[0], bi_ref[0])
        bbr_ref[0] = bb_re
        bbi_ref[0] = bb_im
        cr, ci = lb_re, lb_im
        for k in range(KSTEPS):
            pr_ref[0, k] = cr
            pi_ref[0, k] = ci
            if k < KSTEPS - 1:
                cr, ci = cr * lb_re - ci * lb_im, cr * lb_im + ci * lb_re
        for t in range(2):
            cr, ci = cr * cr - ci * ci, 2.0 * cr * ci
            hr_ref[0, t] = cr
            hi_ref[0, t] = ci

    s3 = pl.BlockSpec((1, g, n), lambda i: (i, 0, 0))
    s4 = pl.BlockSpec((1, g, p, n), lambda i: (i, 0, 0, 0))
    sp = pl.BlockSpec((1, KSTEPS, g, n), lambda i: (i, 0, 0, 0))
    sh = pl.BlockSpec((1, 2, g, n), lambda i: (i, 0, 0, 0))
    return pl.pallas_call(
        body, name="ssm_discretize", grid=(nd,),
        in_specs=[s3, s3, pl.BlockSpec((1, g, 1), lambda i: (i, 0, 0)), s4, s4],
        out_specs=[s4, s4, sp, sp, sh, sh],
        out_shape=[jax.ShapeDtypeStruct((nd, g, p, n), F32)] * 2 + [jax.ShapeDtypeStruct((nd, KSTEPS, g, n), F32)] * 2
        + [jax.ShapeDtypeStruct((nd, 2, g, n), F32)] * 2,
        compiler_params=_params(("parallel",)),
    )(a_re, a_im, log_dt, bt_re, bt_im)


def _ssm_param_bwd(a_re, a_im, log_dt, bt_re, bt_im, dlb_re, dlb_im, dbb_re, dbb_im):
    nd, g, n = a_re.shape
    p = bt_re.shape[2]

    def body(ar_ref, ai_ref, ld_ref, br_ref, bi_ref, c0_ref, c1_ref, c2_ref, c3_ref,
             o0_ref, o1_ref, o2_ref, o3_ref, o4_ref):
        _, vjp = jax.vjp(_ssm_math, ar_ref[0], ai_ref[0], ld_ref[0], br_ref[0], bi_ref[0])
        outs = vjp((c0_ref[0], c1_ref[0], c2_ref[0], c3_ref[0]))
        for ref, val in zip((o0_ref, o1_ref, o2_ref, o3_ref, o4_ref), outs):
            ref[0] = val

    s3 = pl.BlockSpec((1, g, n), lambda i: (i, 0, 0))
    s1 = pl.BlockSpec((1, g, 1), lambda i: (i, 0, 0))
    s4 = pl.BlockSpec((1, g, p, n), lambda i: (i, 0, 0, 0))
    return pl.pallas_call(
        body, name="ssm_param_bwd", grid=(nd,),
        in_specs=[s3, s3, s1, s4, s4, s3, s3, s4, s4],
        out_specs=[s3, s3, s1, s4, s4],
        out_shape=[jax.ShapeDtypeStruct((nd, g, n), F32)] * 2 + [jax.ShapeDtypeStruct((nd, g, 1), F32)]
        + [jax.ShapeDtypeStruct((nd, g, p, n), F32)] * 2,
        compiler_params=_params(("parallel",)),
    )(a_re, a_im, log_dt, bt_re, bt_im, dlb_re, dlb_im, dbb_re, dbb_im)


def _cmul(ar, ai, xr, xi, conj):
    if conj:
        return ar * xr + ai * xi, ar * xi - ai * xr
    return ar * xr - ai * xi, ar * xi + ai * xr


def _scan_chunk(buf, tab, carry, ein, nj, rev, conj, base=0):
    ks = list(range(KSTEPS))
    if rev:
        ks = ks[::-1]
    sub = lax.broadcasted_iota(jnp.int32, (SUBLANES, SCAN_LANES), 0)
    edge = sub == (SUBLANES - 1 if rev else 0)

    def step(j, _):
        jr, ji = j, nj + j
        ar, ai = tab[base, jr], tab[base, ji]
        hr = jnp.zeros((SUBLANES, SCAN_LANES), F32)
        hi = jnp.zeros((SUBLANES, SCAN_LANES), F32)
        for k in ks:
            rows = pl.ds(k * SUBLANES, SUBLANES)
            pr, pi_ = _cmul(ar, ai, hr, hi, conj)
            hr = pr + buf[jr, rows, :]
            hi = pi_ + buf[ji, rows, :]
            buf[jr, rows, :] = hr
            buf[ji, rows, :] = hi
        shift = SUBLANES - 1 if rev else 1
        er = jnp.where(edge, carry[jr], pltpu.roll(hr, shift, 0))
        ei = jnp.where(edge, carry[ji], pltpu.roll(hi, shift, 0))
        for t, dist in enumerate((1, 2, 4)):
            sh = SUBLANES - dist if rev else dist
            pr, pi_ = _cmul(tab[base + 1 + t, jr], tab[base + 1 + t, ji], pltpu.roll(er, sh, 0), pltpu.roll(ei, sh, 0), conj)
            er, ei = er + pr, ei + pi_
        ein[jr] = er
        ein[ji] = ei
        pr, pi_ = _cmul(tab[base + 4 + KSTEPS - 1, jr], tab[base + 4 + KSTEPS - 1, ji], er, ei, conj)
        last = 0 if rev else SUBLANES - 1
        carry[jr] = jnp.broadcast_to((hr + pr)[last:last + 1, :], (SUBLANES, SCAN_LANES))
        carry[ji] = jnp.broadcast_to((hi + pi_)[last:last + 1, :], (SUBLANES, SCAN_LANES))
        for n, k in enumerate(ks):
            rows = pl.ds(k * SUBLANES, SUBLANES)
            pr, pi_ = _cmul(tab[base + 4 + n, jr], tab[base + 4 + n, ji], er, ei, conj)
            buf[jr, rows, :] += pr
            buf[ji, rows, :] += pi_
        return 0

    lax.fori_loop(0, nj, step, 0)


def _state_lanes(b):
    per = SCAN_LANES // SSM_BLOCK
    return b // per, slice((b % per) * SSM_BLOCK, (b % per + 1) * SSM_BLOCK)


def _project_in(src, w_ref, buf, nj):
    nb, cb, _ = w_ref.shape
    for b in range(nb):
        res = _dot(src[:, b * cb:(b + 1) * cb], w_ref[b])
        j, lanes = _state_lanes(b)
        buf[j, :, lanes] = res[:, :SSM_BLOCK]
        buf[nj + j, :, lanes] = res[:, SSM_BLOCK:]


def _state_block(buf, b, nj):
    j, lanes = _state_lanes(b)
    return jnp.concatenate([buf[j, :, lanes], buf[nj + j, :, lanes]], axis=1).astype(BF16)


def _project_out(buf, w_ref, nj):
    return jnp.concatenate([_dot_nt(_state_block(buf, b, nj), w_ref[b]) for b in range(w_ref.shape[0])], axis=1)


def _ssm_fwd(u, wb, wct, tab, rev, name):
    lp, w = u.shape
    nb, cb, _ = wb.shape
    nj = nb * SSM_BLOCK // SCAN_LANES
    nc = lp // CHUNK
    ntab = tab.shape[0]
    cidx = (lambda c: nc - 1 - c) if rev else (lambda c: c)

    def body(u_ref, wb_ref, wct_ref, tab_ref, y_ref, ck_ref, buf, carry, ein):
        @pl.when(pl.program_id(0) == 0)
        def _():
            carry[...] = jnp.zeros(carry.shape, F32)

        _project_in(u_ref[...].astype(BF16), wb_ref, buf, nj)
        ck_ref[0] = carry[...]
        _scan_chunk(buf, tab_ref, carry, ein, nj, rev, False)
        y_ref[...] = _project_out(buf, wct_ref, nj)

    wshape = (nb, cb, 2 * SSM_BLOCK)
    return pl.pallas_call(
        body, name=name, grid=(nc,),
        in_specs=[pl.BlockSpec((CHUNK, w), lambda c: (cidx(c), 0)), _full(wshape), _full(wshape),
                  _full((ntab, 2 * nj, SUBLANES, SCAN_LANES))],
        out_specs=[pl.BlockSpec((CHUNK, w), lambda c: (cidx(c), 0)),
                   pl.BlockSpec((1, 2 * nj, SUBLANES, SCAN_LANES), lambda c: (cidx(c), 0, 0, 0))],
        out_shape=[jax.ShapeDtypeStruct((lp, w), F32), jax.ShapeDtypeStruct((nc, 2 * nj, SUBLANES, SCAN_LANES), F32)],
        scratch_shapes=[pltpu.VMEM((2 * nj, CHUNK, SCAN_LANES), F32), pltpu.VMEM((2 * nj, SUBLANES, SCAN_LANES), F32),
                        pltpu.VMEM((2 * nj, SUBLANES, SCAN_LANES), F32)],
        compiler_params=_params(("arbitrary",)),
    )(u, wb, wct, tab)


def _ssm_bwd(u, dy, ckpt, wb, wct, tab, rev, name, scatter=()):
    lp, w = u.shape
    nb, cb, _ = wb.shape
    nj = nb * SSM_BLOCK // SCAN_LANES
    nc = lp // CHUNK
    ntab = tab.shape[0]
    cidx = (lambda c: c) if rev else (lambda c: nc - 1 - c)

    ns = len(scatter)

    def body(*refs):
        u_ref, dy_ref, ck_ref, wb_ref, wct_ref, tab_hbm = refs[:6]
        du_ref, dbb_ref, dcc_ref, dlb_ref = refs[6 + ns:10 + ns]
        tab_ref, dwb_ref, dwc_ref, xs, ls, xcar, lcar, xin, lin = refs[10 + 2 * ns:19 + 2 * ns]
        c = pl.program_id(0)

        if ns:
            start, finish = _scatter_phases(refs[6:6 + ns], refs[10 + ns:10 + 2 * ns], *refs[19 + 2 * ns:])
            pl.when(c == 0)(start)
            pl.when(c == nc - 1)(finish)

        @pl.when(c == 0)
        def _():
            pltpu.sync_copy(tab_hbm, tab_ref)
            lcar[...] = jnp.zeros(lcar.shape, F32)
            dwb_ref[...] = jnp.zeros(dwb_ref.shape, F32)
            dwc_ref[...] = jnp.zeros(dwc_ref.shape, F32)
            dlb_ref[...] = jnp.zeros(dlb_ref.shape, F32)

        ub = u_ref[...].astype(BF16)
        dyb = dy_ref[...].astype(BF16)
        _project_in(ub, wb_ref, xs, nj)
        xcar[...] = ck_ref[0]
        _scan_chunk(xs, tab_ref, xcar, xin, nj, rev, False)
        _project_in(dyb, wct_ref, ls, nj)
        _scan_chunk(ls, tab_ref, lcar, lin, nj, not rev, True, base=ntab // 2)
        dus = []
        for b in range(nb):
            chans = slice(b * cb, (b + 1) * cb)
            xb = _state_block(xs, b, nj)
            lb = _state_block(ls, b, nj)
            dwc_ref[b] += _dot_tn(dyb[:, chans], xb)
            dwb_ref[b] += _dot_tn(ub[:, chans], lb)
            dus.append(_dot_nt(lb, wb_ref[b]))
        du_ref[...] = jnp.concatenate(dus, axis=1)

        def step(j, _):
            jr, ji = j, nj + j
            ar = jnp.zeros((SUBLANES, SCAN_LANES), F32)
            ai = jnp.zeros((SUBLANES, SCAN_LANES), F32)
            for k in range(KSTEPS):
                kp = k + 1 if rev else k - 1
                rows = pl.ds(k * SUBLANES, SUBLANES)
                if 0 <= kp < KSTEPS:
                    prow = pl.ds(kp * SUBLANES, SUBLANES)
                    xr, xi = xs[jr, prow, :], xs[ji, prow, :]
                else:
                    xr, xi = xin[jr], xin[ji]
                lr, li = ls[jr, rows, :], ls[ji, rows, :]
                ar += lr * xr + li * xi
                ai += li * xr - lr * xi
            dlb_ref[jr] += ar
            dlb_ref[ji] += ai
            return 0

        lax.fori_loop(0, nj, step, 0)

        @pl.when(c == nc - 1)
        def _():
            for b in range(2 * nj):
                dlb_ref[b] = jnp.broadcast_to(jnp.sum(dlb_ref[b], axis=0, keepdims=True), (SUBLANES, SCAN_LANES))
            for g in range(w // SSM_GROUP):
                b, gl = divmod(g, cb // SSM_GROUP)
                rows = slice(gl * SSM_GROUP, (gl + 1) * SSM_GROUP)
                for part in range(2):
                    cols = slice(part * SSM_BLOCK + gl * SSM_STATE, part * SSM_BLOCK + (gl + 1) * SSM_STATE)
                    dbb_ref[part, g * SSM_GROUP:(g + 1) * SSM_GROUP, :] = dwb_ref[b, rows, cols]
                    dcc_ref[part, g * SSM_GROUP:(g + 1) * SSM_GROUP, :] = dwc_ref[b, rows, cols]

    st = (2 * nj, SUBLANES, SCAN_LANES)
    wshape = (nb, cb, 2 * SSM_BLOCK)
    sems = [pltpu.SemaphoreType.DMA((ns, 3)), pltpu.SemaphoreType.DMA((ns, 3)), pltpu.SemaphoreType.DMA((ns,))]
    res = pl.pallas_call(
        body, name=name, grid=(nc,),
        in_specs=[pl.BlockSpec((CHUNK, w), lambda c: (cidx(c), 0)), pl.BlockSpec((CHUNK, w), lambda c: (cidx(c), 0)),
                  pl.BlockSpec((1,) + st, lambda c: (cidx(c), 0, 0, 0)), _full(wshape), _full(wshape), _ANY]
        + [_ANY] * ns,
        out_specs=[pl.BlockSpec((CHUNK, w), lambda c: (cidx(c), 0)), _full((2, w, SSM_STATE)),
                   _full((2, w, SSM_STATE)), _full(st)] + [_ANY] * ns,
        out_shape=[jax.ShapeDtypeStruct((lp, w), F32), jax.ShapeDtypeStruct((2, w, SSM_STATE), F32),
                   jax.ShapeDtypeStruct((2, w, SSM_STATE), F32), jax.ShapeDtypeStruct(st, F32)]
        + [jax.ShapeDtypeStruct(p.shape, p.dtype) for p in scatter],
        scratch_shapes=[pltpu.VMEM((ntab,) + st, F32), pltpu.VMEM(wshape, F32), pltpu.VMEM(wshape, F32),
                        pltpu.VMEM((2 * nj, CHUNK, SCAN_LANES), F32), pltpu.VMEM((2 * nj, CHUNK, SCAN_LANES), F32),
                        pltpu.VMEM(st, F32), pltpu.VMEM(st, F32), pltpu.VMEM(st, F32), pltpu.VMEM(st, F32)]
        + (sems if ns else []),
        compiler_params=_params(("arbitrary",)),
    )(u, dy, ckpt, wb, wct, tab, *scatter)
    return res[0], res[1], res[2], res[3], list(res[4:])


def _embed_blocks(t_re, t_im):
    g, p, n = t_re.shape
    gb = SSM_BLOCK // n
    eye = jnp.eye(gb, dtype=t_re.dtype)
    parts = [jnp.einsum('bgpn,gh->bgphn', t.reshape(g // gb, gb, p, n), eye).reshape(g // gb, gb * p, gb * n)
             for t in (t_re, t_im)]
    return jnp.concatenate(parts, axis=2)


def _scan_tables(pw_re, pw_im, hi_re, hi_im, rev):
    s = pw_re.shape[1] * pw_re.shape[2]
    nj = s // SCAN_LANES
    sub = np.arange(SUBLANES)
    live = np.ones((4 + KSTEPS, 1, SUBLANES, 1), bool)
    for row, dist in ((1, 1), (2, 2), (3, 4)):
        live[row, 0, :, 0] = (sub < SUBLANES - dist) if rev else (sub >= dist)

    def lay(pw, hi):
        rows = jnp.concatenate([pw[:1], pw[KSTEPS - 1:], hi, pw], axis=0).reshape(4 + KSTEPS, nj, 1, SCAN_LANES)
        return jnp.where(live, jnp.broadcast_to(rows, (4 + KSTEPS, nj, SUBLANES, SCAN_LANES)), 0.0)

    return jnp.concatenate([lay(pw_re, hi_re), lay(pw_im, hi_im)], axis=1)


def _adamw(w, g, m, v, tm):
    r, c = w.shape
    c1 = 1.0 - ADAM_B1 ** ADAM_STEP
    c2 = 1.0 - ADAM_B2 ** ADAM_STEP

    def body(w_ref, g_ref, m_ref, v_ref, d_ref, nm_ref, nv_ref):
        gg = g_ref[...]
        nm = ADAM_B1 * m_ref[...] + (1.0 - ADAM_B1) * gg
        nv = ADAM_B2 * v_ref[...] + (1.0 - ADAM_B2) * (gg * gg)
        nm_ref[...] = nm
        nv_ref[...] = nv
        d_ref[...] = -ADAM_LR * ((nm / c1) / (jnp.sqrt(nv / c2) + ADAM_EPS) + ADAM_WD * w_ref[...])

    spec = _row(tm, c)
    return pl.pallas_call(
        body, name="adamw", grid=(r // tm,), in_specs=[spec] * 4, out_specs=[spec] * 3,
        out_shape=[jax.ShapeDtypeStruct((r, c), F32)] * 3, compiler_params=_params(("parallel",)),
    )(w, g, m, v)


def _pair_sum(g42, got, core, out_dtype, tm, name):
    _, _, r, c = g42.shape

    def body(core_ref, a_ref, b_ref, o_ref):
        o_ref[...] = (a_ref[...] + b_ref[...]).astype(out_dtype)

    grid_spec = pltpu.PrefetchScalarGridSpec(
        num_scalar_prefetch=1, grid=(4, r // tm),
        in_specs=[pl.BlockSpec((1, None, tm, c), lambda s, i, core_ref: (s, core_ref[0], i, 0)),
                  pl.BlockSpec((1, tm, c), lambda s, i, core_ref: (s, i, 0))],
        out_specs=pl.BlockSpec((1, tm, c), lambda s, i, core_ref: (s, i, 0)))
    return pl.pallas_call(
        body, name=name, grid_spec=grid_spec, out_shape=jax.ShapeDtypeStruct((4, r, c), out_dtype),
        compiler_params=_params(("parallel", "parallel")),
    )(core, g42, got)


def _sum4(a, core, tm, name):
    _, r, c = a.shape

    def body(core_ref, a_ref, o_ref):
        o_ref[...] = ((a_ref[0].astype(F32) + a_ref[1].astype(F32)) + a_ref[2].astype(F32)) + a_ref[3].astype(F32)

    grid_spec = pltpu.PrefetchScalarGridSpec(
        num_scalar_prefetch=1, grid=(r // tm,),
        in_specs=[pl.BlockSpec((4, tm, c), lambda i, core_ref: (0, i, 0))],
        out_specs=pl.BlockSpec((None, tm, c), lambda i, core_ref: (core_ref[0], i, 0)))
    return pl.pallas_call(
        body, name=name, grid_spec=grid_spec, out_shape=jax.ShapeDtypeStruct((2, r, c), F32),
        compiler_params=_params(("parallel",)),
    )(core, a)


_ANY = pl.BlockSpec(memory_space=pl.ANY)


def _gather_phases(xs, outs, send_sems, recv_sems, local_sems):
    n = len(xs)

    def parts():
        x, y, c = lax.axis_index("x"), lax.axis_index("y"), lax.axis_index("c")
        return c, (x, y, c), (x, y, 1 - c), [(1 - x, y), (x, 1 - y), (1 - x, 1 - y)]

    def slot(t, px, py, pc):
        return outs[t].at[4 * px + 2 * py + pc]

    def copy(t, k, blk, to, src=None):
        return pltpu.make_async_remote_copy(
            src_ref=slot(t, *blk) if src is None else src, dst_ref=slot(t, *blk),
            send_sem=send_sems.at[t, k], recv_sem=recv_sems.at[t, k], device_id=to, device_id_type=MESH_ID)

    def own(t, me):
        return pltpu.make_async_copy(xs[t], slot(t, *me), local_sems.at[t])

    def first(t, c, me, sibling, chips):
        return [copy(t, 0, me, sibling, src=xs[t])] + [copy(t, 1 + j, me, (*chip, c), src=xs[t])
                                                       for j, chip in enumerate(chips)]

    def passed(t, c, sibling, chips):
        return [copy(t, 4 + j, (*chip, c), sibling) for j, chip in enumerate(chips)]

    def start():
        c, me, sibling, chips = parts()
        for t in range(n):
            own(t, me).start()
        for t in range(n):
            for cp in first(t, c, me, sibling, chips):
                cp.start()

    def forward():
        c, me, sibling, chips = parts()
        for j, chip in enumerate(chips):
            for t in range(n):
                copy(t, 1 + j, (*chip, c), me).wait_recv()
                passed(t, c, sibling, chips)[j].start()

    def finish():
        c, me, sibling, chips = parts()
        for t in range(n):
            copy(t, 0, sibling, me).wait_recv()
        for j, chip in enumerate(chips):
            for t in range(n):
                copy(t, 4 + j, (*chip, 1 - c), me).wait_recv()
        for t in range(n):
            for cp in first(t, c, me, sibling, chips) + passed(t, c, sibling, chips):
                cp.wait_send()
            own(t, me).wait()

    return start, forward, finish


def _all_gather8(blocks, name):
    n = len(blocks)

    def body(*refs):
        for phase in _gather_phases(refs[:n], refs[n:2 * n], *refs[2 * n:]):
            phase()

    return pl.pallas_call(
        body, name=name, out_shape=[jax.ShapeDtypeStruct((8,) + b.shape, b.dtype) for b in blocks],
        in_specs=[_ANY] * n, out_specs=[_ANY] * n,
        scratch_shapes=[pltpu.SemaphoreType.DMA((n, 7)), pltpu.SemaphoreType.DMA((n, 7)),
                        pltpu.SemaphoreType.DMA((n,))],
    )(*blocks)


def _pair_exchange(gs, name):
    n = len(gs)

    def body(*refs):
        g_refs, outs = refs[:n], refs[n:2 * n]
        send_sems, recv_sems = refs[2 * n:]
        x, y, c = lax.axis_index("x"), lax.axis_index("y"), lax.axis_index("c")
        cps = [pltpu.make_async_remote_copy(
            src_ref=g_refs[t].at[:, 1 - c], dst_ref=outs[t], send_sem=send_sems.at[t], recv_sem=recv_sems.at[t],
            device_id=(x, y, 1 - c), device_id_type=MESH_ID) for t in range(n)]
        for cp in cps:
            cp.start()
        for cp in cps:
            cp.wait()

    return pl.pallas_call(
        body, name=name,
        out_shape=[jax.ShapeDtypeStruct((g.shape[0],) + g.shape[2:], g.dtype) for g in gs],
        in_specs=[_ANY] * n, out_specs=[_ANY] * n,
        scratch_shapes=[pltpu.SemaphoreType.DMA((n,)), pltpu.SemaphoreType.DMA((n,))],
    )(*gs)


def _scatter_phases(p_refs, outs, send_sems, recv_sems, local_sems):
    n = len(p_refs)

    def parts():
        x, y, c = lax.axis_index("x"), lax.axis_index("y"), lax.axis_index("c")
        return c, 2 * x + y, [(1 - x, y), (x, 1 - y), (1 - x, 1 - y)]

    def copy(t, k, src_slab, dst_slab, chip, c):
        return pltpu.make_async_remote_copy(
            src_ref=p_refs[t].at[src_slab], dst_ref=outs[t].at[dst_slab], send_sem=send_sems.at[t, k],
            recv_sem=recv_sems.at[t, k], device_id=(*chip, c), device_id_type=MESH_ID)

    def own(t, mine):
        return pltpu.make_async_copy(p_refs[t].at[mine], outs[t].at[mine], local_sems.at[t])

    def start():
        c, mine, chips = parts()
        for t in range(n):
            own(t, mine).start()
        for k, (cx, cy) in enumerate(chips):
            for t in range(n):
                copy(t, k, 2 * cx + cy, mine, (cx, cy), c).start()

    def finish():
        c, mine, chips = parts()
        for k, (cx, cy) in enumerate(chips):
            for t in range(n):
                copy(t, k, mine, 2 * cx + cy, (cx, cy), c).wait_recv()
        for t in range(n):
            for k, (cx, cy) in enumerate(chips):
                copy(t, k, 2 * cx + cy, mine, (cx, cy), c).wait_send()
            own(t, mine).wait()

    return start, finish


def _chip_scatter(ps, name):
    n = len(ps)

    def body(*refs):
        for phase in _scatter_phases(refs[:n], refs[n:2 * n], *refs[2 * n:]):
            phase()

    return pl.pallas_call(
        body, name=name, out_shape=[jax.ShapeDtypeStruct(p.shape, p.dtype) for p in ps],
        in_specs=[_ANY] * n, out_specs=[_ANY] * n,
        scratch_shapes=[pltpu.SemaphoreType.DMA((n, 3)), pltpu.SemaphoreType.DMA((n, 3)),
                        pltpu.SemaphoreType.DMA((n,))],
    )(*ps)


def _pair_gather(rs, name):
    n = len(rs)

    def body(*refs):
        ins, outs = refs[:n], refs[n:2 * n]
        send_sems, recv_sems = refs[2 * n:]
        x, y, c = lax.axis_index("x"), lax.axis_index("y"), lax.axis_index("c")

        def copy(t, slab):
            return pltpu.make_async_remote_copy(
                src_ref=ins[t].at[slab], dst_ref=outs[t].at[slab], send_sem=send_sems.at[t],
                recv_sem=recv_sems.at[t], device_id=(x, y, 1 - c), device_id_type=MESH_ID)

        sends = [copy(t, c) for t in range(n)]
        for cp in sends:
            cp.start()
        for t in range(n):
            copy(t, 1 - c).wait_recv()
        for cp in sends:
            cp.wait_send()

    return pl.pallas_call(
        body, name=name, out_shape=[jax.ShapeDtypeStruct(r.shape, r.dtype) for r in rs],
        in_specs=[_ANY] * n, out_specs=[_ANY] * n, input_output_aliases={t: t for t in range(n)},
        scratch_shapes=[pltpu.SemaphoreType.DMA((n,)), pltpu.SemaphoreType.DMA((n,))],
    )(*rs)


PACK_COLS = 1024
BIG = (("meta_tokens", 1), ("w_in", 1), ("w_glu", 0), ("w_ssm_proj", 1), ("w_attn_proj", 0), ("w_out", 0),
       ("w_mlp_in", 1), ("w_mlp_out", 0))
SMALL = ("norm_mix_g", "ssm_a_re", "ssm_a_im", "ssm_log_dt", "ssm_b_re", "ssm_b_im", "ssm_c_re", "ssm_c_im",
         "ssm_d", "b_glu", "q_norm_g", "k_norm_g", "norm_mlp_g", "norm_final_g")


def _pad_rows(flat, mult_rows):
    n = flat.shape[0]
    unit = PACK_COLS * mult_rows
    total = -(-n // unit) * unit
    return jnp.pad(flat, (0, total - n)).reshape(total // PACK_COLS, PACK_COLS)


def _half(t, c):
    return lax.dynamic_slice_in_dim(t, c * (t.shape[0] // 2), t.shape[0] // 2, 0)


EARLY_WEIGHTS = ("meta_tokens", "w_in", "w_glu")
LATE_WEIGHTS = tuple(name for name, _ in BIG if name not in EARLY_WEIGHTS)


def _weight_blocks(shards, c, names):
    return [_half(shards[name], c) if name == "meta_tokens" else _half(shards[name], c).astype(BF16) for name in names]


def _shard_major(names, gathered):
    return {name: g.reshape((4, 2 * g.shape[1]) + g.shape[2:]) for name, g in zip(names, gathered)}


class _Reduction:
    def __init__(self, grads, wire, labels, c, tag):
        self.labels, self.wire, self.c, self.tag = labels, wire, c, tag
        self.core = c.astype(jnp.int32).reshape(1)
        g42 = [g.reshape(4, 2, g.shape[1] // 2, g.shape[2]) for g in grads]
        self.tiles = [_pick_tile(g.shape[2], 256, SUBLANES if dt == F32 else 2 * SUBLANES) for g, dt in zip(g42, wire)]
        got = _pair_exchange(g42, "grad_pair_exchange_" + tag)
        self.pair = [_pair_sum(g, o, self.core, dt, tm, "pair_sum_" + lb)
                     for g, o, dt, tm, lb in zip(g42, got, wire, self.tiles, labels)]

    def finish(self, by_src, gathered):
        red = [_sum4(b, self.core, tm, "chip_sum_" + lb) for b, tm, lb in zip(by_src, self.tiles, self.labels)]
        both = _pair_gather(red[:gathered], "grad_pair_gather_" + self.tag)
        pieces = [lax.dynamic_index_in_dim(r, self.c, 0, keepdims=False) for r in red[gathered:]]
        return [b.reshape(2 * b.shape[1], b.shape[2]) for b in both], pieces


def _small_as_shards(small_flat):
    unit = 8 * SUBLANES * PACK_COLS
    k = -(-small_flat.shape[0] // unit) * unit
    return jnp.pad(small_flat, (0, k - small_flat.shape[0])).reshape(4, k // (4 * PACK_COLS), PACK_COLS)


def _to_chunk_order(a):
    lp = a.shape[0]
    rest = a.shape[1:]
    a = a.reshape((lp // CHUNK, SUBLANES, KSTEPS) + rest)
    return a.swapaxes(1, 2).reshape((lp,) + rest)


def _from_chunk_order(a):
    lp = a.shape[0]
    rest = a.shape[1:]
    a = a.reshape((lp // CHUNK, KSTEPS, SUBLANES) + rest)
    return a.swapaxes(1, 2).reshape((lp,) + rest)


def _rope_tables(l_total, lp):
    n_real = l_total - N_META
    pos = np.arange(n_real)
    row_id = (pos // GRID_W).astype(np.float32)
    col_id = (pos % GRID_W).astype(np.float32)
    ppa = HEAD_DIM // 4
    inv_freq = (ROPE_THETA ** (-np.arange(ppa, dtype=np.float64) / ppa)).astype(np.float32)
    ang = np.concatenate([row_id[:, None] * inv_freq, col_id[:, None] * inv_freq], axis=-1)
    ang = np.concatenate([np.zeros((N_META, HEAD_DIM // 2), np.float32), ang,
                          np.zeros((lp - l_total, HEAD_DIM // 2), np.float32)], axis=0).astype(np.float64)
    cos = np.repeat(np.cos(ang), 2, axis=1)
    sin = np.repeat(np.sin(ang), 2, axis=1) * np.tile(np.asarray([-1.0, 1.0]), HEAD_DIM // 2)
    reps = (1, LANES // HEAD_DIM)
    return np.tile(cos, reps).astype(np.float32), np.tile(sin, reps).astype(np.float32)


def kernel(x, meta_tokens, norm_mix_g, w_in, ssm_a_re, ssm_a_im, ssm_log_dt, ssm_b_re, ssm_b_im, ssm_c_re, ssm_c_im, ssm_d, w_glu, b_glu, q_norm_g, k_norm_g, w_ssm_proj, w_attn_proj, w_out, norm_mlp_g, w_mlp_in, w_mlp_out, norm_final_g, loss_target, m_meta_tokens, m_norm_mix_g, m_w_in, m_ssm_a_re, m_ssm_a_im, m_ssm_log_dt, m_ssm_b_re, m_ssm_b_im, m_ssm_c_re, m_ssm_c_im, m_ssm_d, m_w_glu, m_b_glu, m_q_norm_g, m_k_norm_g, m_w_ssm_proj, m_w_attn_proj, m_w_out, m_norm_mlp_g, m_w_mlp_in, m_w_mlp_out, m_norm_final_g, v_meta_tokens, v_norm_mix_g, v_w_in, v_ssm_a_re, v_ssm_a_im, v_ssm_log_dt, v_ssm_b_re, v_ssm_b_im, v_ssm_c_re, v_ssm_c_im, v_ssm_d, v_w_glu, v_b_glu, v_q_norm_g, v_k_norm_g, v_w_ssm_proj, v_w_attn_proj, v_w_out, v_norm_mlp_g, v_w_mlp_in, v_w_mlp_out, v_norm_final_g):
    args = dict(locals())
    names = list(dict.fromkeys([n for n, _ in BIG] + list(SMALL)))
    order = ['meta_tokens', 'norm_mix_g', 'w_in', 'ssm_a_re', 'ssm_a_im', 'ssm_log_dt', 'ssm_b_re', 'ssm_b_im',
             'ssm_c_re', 'ssm_c_im', 'ssm_d', 'w_glu', 'b_glu', 'q_norm_g', 'k_norm_g', 'w_ssm_proj', 'w_attn_proj',
             'w_out', 'norm_mlp_g', 'w_mlp_in', 'w_mlp_out', 'norm_final_g']
    assert sorted(names) == sorted(order)
    c_idx = lax.axis_index("c")

    seq, d = x.shape[1], x.shape[2]
    l_total = seq + N_META
    lp = -(-l_total // SEQ_ALIGN) * SEQ_ALIGN
    hd = d // 2
    n_groups = hd // SSM_GROUP
    n_state = n_groups * SSM_STATE
    nj = n_state // SCAN_LANES
    kvh = d // HEAD_DIM // GQA_REP

    shard2d = {}
    for name, _ in BIG:
        t = args[name]
        shard2d[name] = t.reshape(t.shape[-2], t.shape[-1])
    full = _shard_major(EARLY_WEIGHTS, _all_gather8(_weight_blocks(shard2d, c_idx, EARLY_WEIGHTS), "weight_all_gather"))
    meta_full = jnp.transpose(full["meta_tokens"], (1, 0, 2)).reshape(N_META, d)
    w_in4 = full["w_in"]
    w_glu_f = full["w_glu"].reshape(hd, hd)

    xin = jnp.concatenate([meta_full, x[0], jnp.zeros((lp - l_total, d), F32)], axis=0)
    xin = _to_chunk_order(xin)
    tgt = _to_chunk_order(jnp.pad(loss_target[0], ((N_META, lp - l_total), (0, 0))))
    pos = np.arange(lp)
    rowmask = jnp.asarray(_to_chunk_order(((pos >= N_META) & (pos < l_total)).astype(np.float32)[:, None]))
    kbias = jnp.asarray(_to_chunk_order(np.where(pos < l_total, 0.0, MASK_VALUE).astype(np.float32)[:, None])
                        .reshape(1, lp))
    cos_t, sin_t = (jnp.asarray(_to_chunk_order(t)) for t in _rope_tables(l_total, lp))
    mean_m, sel = _head_tables(d)

    tm = _pick_tile(lp, 320)
    tm_mid = _pick_tile(lp, 384)
    tm_big = _pick_tile(lp, 640)
    tq = _pick_tile(lp, ATTN_Q_TILE, LANES)
    tk = _pick_tile(lp, ATTN_K_TILE, MXU_DIM)
    assert lp - tk <= (l_total // CHUNK) * CHUNK
    g_mix = norm_mix_g.reshape(1, d)
    g_mlp = norm_mlp_g.reshape(1, d)
    g_fin = norm_final_g.reshape(1, d)
    qg = jnp.tile(q_norm_g.reshape(1, HEAD_DIM), (1, LANES // HEAD_DIM))
    kg = jnp.tile(k_norm_g.reshape(1, HEAD_DIM), (1, LANES // HEAD_DIM))
    dskip = ssm_d.reshape(1, hd)
    bglu = b_glu.reshape(1, hd)

    a_re, a_im = ssm_a_re[0], ssm_a_im[0]
    log_dt = ssm_log_dt[0][..., None]
    bt_re = jnp.swapaxes(ssm_b_re[0], 2, 3)
    bt_im = jnp.swapaxes(ssm_b_im[0], 2, 3)
    bb_re, bb_im, pw_re, pw_im, hi_re, hi_im = _ssm_discretize(a_re, a_im, log_dt, bt_re, bt_im)
    wb = [_embed_blocks(bb_re[i], bb_im[i]).astype(BF16) for i in range(2)]
    wct = [_embed_blocks(ssm_c_re[0, i], -ssm_c_im[0, i]).astype(BF16) for i in range(2)]
    tabs = [_scan_tables(pw_re[i], pw_im[i], hi_re[i], hi_im[i], rev=(i == 1)) for i in range(2)]
    tabs_adj = [_scan_tables(pw_re[i], pw_im[i], hi_re[i], hi_im[i], rev=(i == 0)) for i in range(2)]

    u, qkv, gates = _in_proj(xin, g_mix, w_in4, tm_mid)
    y0, ck0 = _ssm_fwd(u, wb[0], wct[0], tabs[0], False, "ssm_fwd_0")
    y1, ck1 = _ssm_fwd(u, wb[1], wct[1], tabs[1], True, "ssm_fwd_1")
    yssm = _glu_fwd(u, y0, y1, dskip, w_glu_f, bglu, tm_big)
    q, k, v = _qk_prep(qkv, cos_t, sin_t, qg, kg, mean_m, tm)
    o, lse, pt, mblk, late = _attn_fwd(q, k, v, kbias, tq, tk, gather=_weight_blocks(shard2d, c_idx, LATE_WEIGHTS))
    full = _shard_major(LATE_WEIGHTS, late)
    w_mlp_in4 = full["w_mlp_in"]
    w_ssm_proj4 = full["w_ssm_proj"]
    w_attn_proj_f = full["w_attn_proj"].reshape(d, d)
    w_out_f = full["w_out"].reshape(d, d)
    w_mlp_out_f = full["w_mlp_out"].reshape(4 * d, d)
    h1, merged = _merge_fwd(yssm, o, gates, xin, w_ssm_proj4, w_attn_proj_f, w_out_f, tm_mid)
    r = _mlp_in(h1, g_mlp, w_mlp_in4, tm_mid)
    h3 = _mlp_out(h1, r, w_mlp_out_f, tm_mid)
    loss_tile, dh3, d_gfin = _final_loss(h3, g_fin, tgt, rowmask, tm_big)

    dz, dh3b = _mlp_bwd_a(dh3, r, w_mlp_out_f, tm_mid)
    dh1, d_gmlp = _mlp_bwd_b(dz, dh3, h1, g_mlp, w_mlp_in4, tm_mid)
    dgates, dms, dma, dyssm, do, delta, dh1b = _merge_bwd(dh1, yssm, o, gates, w_ssm_proj4, w_attn_proj_f, w_out_f,
                                                          sel, tm)
    dyv, d_wglu, d_bglu, d_dskip = _glu_bwd(dyssm, u, y0, y1, dskip, w_glu_f, bglu, tm_big)

    tn = min(d, 1024)
    tm_w = _pick_tile(lp, 3 * MXU_DIM, MXU_DIM)
    grads4 = {
        "w_mlp_in": _wgrad(h1, dz, 4, tm_w, tn, "wgrad_mlp_in", gain=g_mlp),
        "w_mlp_out": _wgrad(r, dh3b, 1, tm_w, min(d, 256), "wgrad_mlp_out", square=True).reshape(4, d, d),
        "w_out": _wgrad(merged, dh1b, 1, tm_w, tn, "wgrad_out").reshape(4, d // 4, d),
        "w_attn_proj": _wgrad(o, dma, 1, tm_w, tn, "wgrad_attn_proj").reshape(4, d // 4, d),
        "w_ssm_proj": _wgrad(yssm, dms, 4, tm_w, d // 4, "wgrad_ssm_proj"),
        "w_glu": d_wglu.reshape(4, hd // 4, hd),
    }
    first_names = list(grads4)
    first = _Reduction([grads4[n] for n in first_names], [BF16] * len(first_names), first_names, c_idx, "first")

    du0, dbb0, dcc0, dlb0, first_by_src = _ssm_bwd(u, dyv, ck0, wb[0], wct[0], _both(tabs[0], tabs_adj[0]), False,
                                                   "ssm_bwd_0", scatter=first.pair)
    du1, dbb1, dcc1, dlb1, _ = _ssm_bwd(u, dyv, ck1, wb[1], wct[1], _both(tabs[1], tabs_adj[1]), True, "ssm_bwd_1")
    dq, dk, dv = _attn_bwd(q, k, v, pt, mblk, do, lse, delta, _pick_tile(lp, MXU_DIM, LANES), tk)
    dqkv, d_qg, d_kg = _qk_bwd(qkv, dq, dk, dv, cos_t, sin_t, qg, kg, mean_m, tm)
    dxin, d_gmix, dproj = _in_proj_bwd(dyv, du0, du1, dskip, dqkv, dgates, dh1, xin, g_mix, w_in4, tm)
    red_big = dict(zip(first_names, first.finish(first_by_src, len(first_names))[0]))

    grads4["w_in"] = _wgrad(xin, dproj, 4, tm_w, tn, "wgrad_in", gain=g_mix)
    dx_nat = _from_chunk_order(dxin)
    grads4["meta_tokens"] = jnp.swapaxes(dx_nat[:N_META].reshape(N_META, 4, d // 4), 0, 1)
    grad_x = dx_nat[N_META:l_total][None]

    dlb = jnp.stack([dlb0, dlb1])[:, :, 0, :]
    dlb_re = dlb[:, :nj].reshape(2, n_groups, SSM_STATE)
    dlb_im = dlb[:, nj:].reshape(2, n_groups, SSM_STATE)
    gpn = (2, 2, n_groups, SSM_GROUP, SSM_STATE)
    dbb = jnp.stack([dbb0, dbb1]).reshape(gpn)
    dcc = jnp.stack([dcc0, dcc1]).reshape(gpn)
    d_are, d_aim, d_logdt, d_btre, d_btim = _ssm_param_bwd(a_re, a_im, log_dt, bt_re, bt_im, dlb_re, dlb_im,
                                                           dbb[:, 0], dbb[:, 1])
    small_grads = {
        "norm_mix_g": d_gmix, "ssm_a_re": d_are, "ssm_a_im": d_aim, "ssm_log_dt": d_logdt,
        "ssm_b_re": jnp.swapaxes(d_btre, 2, 3), "ssm_b_im": jnp.swapaxes(d_btim, 2, 3),
        "ssm_c_re": dcc[:, 0], "ssm_c_im": -dcc[:, 1],
        "ssm_d": d_dskip, "b_glu": d_bglu, "q_norm_g": d_qg[:, :HEAD_DIM], "k_norm_g": d_kg[:, :HEAD_DIM],
        "norm_mlp_g": d_gmlp, "norm_final_g": d_gfin,
    }
    small_flat = jnp.concatenate([small_grads[n].reshape(-1) for n in SMALL] + [loss_tile[0, :1]])

    last = _Reduction([grads4["meta_tokens"], grads4["w_in"], _small_as_shards(small_flat)], [F32, BF16, F32],
                      ["meta_tokens", "w_in", "small"], c_idx, "last")
    (red_big["meta_tokens"], red_big["w_in"]), (small_piece,) = last.finish(
        _chip_scatter(last.pair, "grad_chip_scatter"), 2)
    red_small = _all_gather8([small_piece], "small_grad_all_gather")[0].reshape(-1)[:small_flat.shape[0]]
    loss, red_small = red_small[-1], red_small[:-1]
    grad, delta_w, new_m, new_v = {}, {}, {}, {}
    for name, _ in BIG:
        w2 = shard2d[name]
        shp = args[name].shape
        g2 = red_big[name]
        t = _pick_tile(w2.shape[0], 256, 8)
        dl, nm, nv = _adamw(w2, g2, args["m_" + name].reshape(w2.shape), args["v_" + name].reshape(w2.shape), t)
        grad[name], delta_w[name], new_m[name], new_v[name] = (a.reshape(shp) for a in (g2, dl, nm, nv))

    def pack_small(prefix):
        flat = jnp.concatenate([args[prefix + n].reshape(-1) for n in SMALL])
        return _pad_rows(flat, SUBLANES)

    n_small = red_small.shape[0]
    gs = _pad_rows(red_small, SUBLANES)
    dl, nm, nv = _adamw(pack_small(""), gs, pack_small("m_"), pack_small("v_"), _pick_tile(gs.shape[0], 256, 8))
    off = 0
    for name in SMALL:
        shp = args[name].shape
        k = int(np.prod(shp))
        for dst, src in ((grad, gs), (delta_w, dl), (new_m, nm), (new_v, nv)):
            dst[name] = src.reshape(-1)[off:off + k].reshape(shp)
        off += k
    assert off == n_small

    return (loss, grad_x, *[grad[n] for n in order], *[delta_w[n] for n in order],
            *[new_m[n] for n in order], *[new_v[n] for n in order])


def _both(tab, tab_adj):
    return jnp.concatenate([tab, tab_adj], axis=0)
```

```python
import functools
import math

import numpy as np
import jax
import jax.numpy as jnp
from jax import lax
from jax.experimental import pallas as pl
from jax.experimental.pallas import tpu as pltpu

F32 = jnp.float32
BF16 = jnp.bfloat16

N_META = 16
GRID_W = 64
HEAD_DIM = 64
GQA_REP = 4
SSM_GROUP = 16
SSM_STATE = 64
ROPE_THETA = 10000.0
NORM_EPS = 1e-6
EIG_RE_MAX = -1e-4
ADAM_LR, ADAM_B1, ADAM_B2, ADAM_EPS, ADAM_WD, ADAM_STEP = 0.001, 0.9, 0.999, 1e-08, 0.01, 10

SUBLANES = 8
LANES = 128
CHUNK = 256
KSTEPS = CHUNK // SUBLANES
SCAN_LANES = 512
MXU_DIM = 256
SSM_BLOCK = MXU_DIM
SEQ_ALIGN = MXU_DIM
ATTN_Q_TILE = 384
ATTN_K_TILE = 11 * MXU_DIM
VMEM_LIMIT = 56 << 20
MASK_VALUE = -1e30
MESH_ID = pl.DeviceIdType.MESH


def _dot(a, b):
    return jnp.dot(a, b, preferred_element_type=F32)


def _dot_nt(a, b):
    return lax.dot_general(a, b, (((1,), (1,)), ((), ())), preferred_element_type=F32)


def _dot_tn(a, b):
    return lax.dot_general(a, b, (((0,), (0,)), ((), ())), preferred_element_type=F32)


def _row(tm, width):
    return pl.BlockSpec((tm, width), lambda i: (i, 0))


def _full(shape):
    nd = len(shape)
    return pl.BlockSpec(shape, lambda i: (0,) * nd)


def _params(sem):
    return pltpu.CompilerParams(dimension_semantics=sem, vmem_limit_bytes=VMEM_LIMIT)


def _pick_tile(n, cap, mult=16):
    best = None
    for t in range(mult, min(n, cap) + 1, mult):
        if n % t == 0:
            best = t
    assert best is not None, (n, cap)
    return best


def _rstd(x):
    return lax.rsqrt(jnp.mean(x * x, axis=-1, keepdims=True) + NORM_EPS)


def _rms(x, g):
    return x * _rstd(x) * g


def _rms_bwd(dy, x, g):
    r = _rstd(x)
    xh = x * r
    gdy = dy * g
    dx = r * (gdy - xh * jnp.mean(gdy * xh, axis=-1, keepdims=True))
    return dx, dy * xh


def _split_dot(x, m):
    hi = x.astype(BF16)
    lo = (x - hi.astype(F32)).astype(BF16)
    return _dot(hi, m) + _dot(lo, m)


def _sigmoid(x):
    return 1.0 / (1.0 + jnp.exp(-x))


def _acc_rows(ref, val, first):
    s = jnp.sum(val, axis=0, keepdims=True)

    @pl.when(first)
    def _():
        ref[...] = s

    @pl.when(jnp.logical_not(first))
    def _():
        ref[...] += s


def _in_proj(xin, g, w4, tm):
    lp, d = xin.shape
    hd = d // 2

    def body(x_ref, g_ref, w_ref, u_ref, qkv_ref, gt_ref):
        h = _rms(x_ref[...], g_ref[...]).astype(BF16)
        p0 = _dot(h, w_ref[0])
        u_ref[...] = p0[:, :hd]
        qkv_ref[:, :hd] = p0[:, hd:]
        qkv_ref[:, hd:] = _dot(h, w_ref[1])
        gt_ref[:, :d] = _dot(h, w_ref[2])
        gt_ref[:, d:] = _dot(h, w_ref[3])

    return pl.pallas_call(
        body, name="in_proj", grid=(lp // tm,),
        in_specs=[_row(tm, d), _full((1, d)), _full((4, d, d))],
        out_specs=[_row(tm, hd), _row(tm, 3 * hd), _row(tm, 2 * d)],
        out_shape=[jax.ShapeDtypeStruct((lp, hd), F32), jax.ShapeDtypeStruct((lp, 3 * hd), F32),
                   jax.ShapeDtypeStruct((lp, 2 * d), F32)],
        compiler_params=_params(("parallel",)),
    )(xin, g, w4)


def _gelu(y):
    return 0.5 * y * (1.0 + lax.erf(y * (1.0 / math.sqrt(2.0))))


def _gelu_grad(y):
    return 0.5 * (1.0 + lax.erf(y * (1.0 / math.sqrt(2.0)))) + y * jnp.exp(-0.5 * y * y) * (1.0 / math.sqrt(2.0 * math.pi))


def _glu_fwd(u, y0, y1, dskip, w_glu, b_glu, tm):
    lp, w = u.shape

    def body(u_ref, y0_ref, y1_ref, d_ref, w_ref, b_ref, o_ref):
        y = u_ref[...] * d_ref[...] + y0_ref[...] + y1_ref[...]
        z = _gelu(y)
        t = _dot(z.astype(BF16), w_ref[...]) + b_ref[...]
        o_ref[...] = (z * _sigmoid(t)).astype(BF16)

    return pl.pallas_call(
        body, name="glu_fwd", grid=(lp // tm,),
        in_specs=[_row(tm, w), _row(tm, w), _row(tm, w), _full((1, w)), _full((w, w)), _full((1, w))],
        out_specs=_row(tm, w), out_shape=jax.ShapeDtypeStruct((lp, w), BF16),
        compiler_params=_params(("parallel",)),
    )(u, y0, y1, dskip, w_glu, b_glu)


def _glu_bwd(dyssm, u, y0, y1, dskip, w_glu, b_glu, tm):
    lp, w = u.shape

    def body(g_ref, u_ref, y0_ref, y1_ref, d_ref, w_ref, b_ref, dy_ref, dw_ref, db_ref, dd_ref):
        first = pl.program_id(0) == 0
        uu = u_ref[...]
        y = uu * d_ref[...] + y0_ref[...] + y1_ref[...]
        z = _gelu(y)
        zb = z.astype(BF16)
        sg = _sigmoid(_dot(zb, w_ref[...]) + b_ref[...])
        g = g_ref[...]
        dt = g * z * sg * (1.0 - sg)
        dtb = dt.astype(BF16)
        dz = g * sg + _dot_nt(dtb, w_ref[...])
        dy = dz * _gelu_grad(y)
        dy_ref[...] = dy
        dw = _dot_tn(zb, dtb)

        @pl.when(first)
        def _():
            dw_ref[...] = dw

        @pl.when(jnp.logical_not(first))
        def _():
            dw_ref[...] += dw

        _acc_rows(db_ref, dt, first)
        _acc_rows(dd_ref, dy * uu, first)

    return pl.pallas_call(
        body, name="glu_bwd", grid=(lp // tm,),
        in_specs=[_row(tm, w), _row(tm, w), _row(tm, w), _row(tm, w), _full((1, w)), _full((w, w)), _full((1, w))],
        out_specs=[_row(tm, w), _full((w, w)), _full((1, w)), _full((1, w))],
        out_shape=[jax.ShapeDtypeStruct((lp, w), F32), jax.ShapeDtypeStruct((w, w), F32),
                   jax.ShapeDtypeStruct((1, w), F32), jax.ShapeDtypeStruct((1, w), F32)],
        compiler_params=_params(("arbitrary",)),
    )(dyssm, u, y0, y1, dskip, w_glu, b_glu)


def _merge_fwd(yssm, o, gates, xin, wsp4, wap, wo, tm):
    lp, d = xin.shape
    w = yssm.shape[1]
    ns = d // 4

    def body(y_ref, o_ref, g_ref, x_ref, wsp_ref, wap_ref, wo_ref, h_ref, m_ref):
        yb = y_ref[...]
        ms = jnp.concatenate([_dot(yb, wsp_ref[s]) for s in range(4)], axis=1)
        ma = _dot(o_ref[...], wap_ref[...])
        merged = (_sigmoid(g_ref[:, :d]) * ms + _sigmoid(g_ref[:, d:]) * ma).astype(BF16)
        m_ref[...] = merged
        h_ref[...] = x_ref[...] + _dot(merged, wo_ref[...])

    return pl.pallas_call(
        body, name="merge_fwd", grid=(lp // tm,),
        in_specs=[_row(tm, w), _row(tm, d), _row(tm, 2 * d), _row(tm, d),
                  _full((4, w, ns)), _full((d, d)), _full((d, d))],
        out_specs=[_row(tm, d), _row(tm, d)],
        out_shape=[jax.ShapeDtypeStruct((lp, d), F32), jax.ShapeDtypeStruct((lp, d), BF16)],
        compiler_params=_params(("parallel",)),
    )(yssm, o, gates, xin, wsp4, wap, wo)


def _merge_bwd(dh1, yssm, o, gates, wsp4, wap, wo, sel, tm):
    lp, d = dh1.shape
    w = yssm.shape[1]
    ns = d // 4
    nsel = sel.shape[1]

    def body(dh_ref, y_ref, o_ref, g_ref, wsp_ref, wap_ref, wo_ref, sel_ref,
             dg_ref, dms_ref, dma_ref, dy_ref, do_ref, dl_ref, dhb_ref):
        dhb = dh_ref[...].astype(BF16)
        dhb_ref[...] = dhb
        dm = _dot_nt(dhb, wo_ref[...])
        yb = y_ref[...]
        ob = o_ref[...]
        ms = jnp.concatenate([_dot(yb, wsp_ref[s]) for s in range(4)], axis=1)
        ma = _dot(ob, wap_ref[...])
        ss = _sigmoid(g_ref[:, :d])
        sa = _sigmoid(g_ref[:, d:])
        dg_ref[:, :d] = dm * ms * ss * (1.0 - ss)
        dg_ref[:, d:] = dm * ma * sa * (1.0 - sa)
        dms = (dm * ss).astype(BF16)
        dma = (dm * sa).astype(BF16)
        dms_ref[...] = dms
        dma_ref[...] = dma
        dy = _dot_nt(dms[:, :ns], wsp_ref[0])
        for s in range(1, 4):
            dy += _dot_nt(dms[:, s * ns:(s + 1) * ns], wsp_ref[s])
        dy_ref[...] = dy
        do = _dot_nt(dma, wap_ref[...])
        do_ref[...] = do.astype(BF16)
        dl_ref[...] = _split_dot(do * ob.astype(F32), sel_ref[...])

    return pl.pallas_call(
        body, name="merge_bwd", grid=(lp // tm,),
        in_specs=[_row(tm, d), _row(tm, w), _row(tm, d), _row(tm, 2 * d),
                  _full((4, w, ns)), _full((d, d)), _full((d, d)), _full((d, nsel))],
        out_specs=[_row(tm, 2 * d), _row(tm, d), _row(tm, d), _row(tm, w), _row(tm, d), _row(tm, nsel), _row(tm, d)],
        out_shape=[jax.ShapeDtypeStruct((lp, 2 * d), F32), jax.ShapeDtypeStruct((lp, d), BF16),
                   jax.ShapeDtypeStruct((lp, d), BF16), jax.ShapeDtypeStruct((lp, w), F32),
                   jax.ShapeDtypeStruct((lp, d), BF16), jax.ShapeDtypeStruct((lp, nsel), F32),
                   jax.ShapeDtypeStruct((lp, d), BF16)],
        compiler_params=_params(("parallel",)),
    )(dh1, yssm, o, gates, wsp4, wap, wo, sel)


def _mlp_in(h1, g, w4, tm):
    lp, d = h1.shape

    def body(x_ref, g_ref, w_ref, r_ref):
        h = _rms(x_ref[...], g_ref[...]).astype(BF16)
        for s in range(4):
            r_ref[:, s * d:(s + 1) * d] = jnp.maximum(_dot(h, w_ref[s]), 0.0).astype(BF16)

    return pl.pallas_call(
        body, name="mlp_in", grid=(lp // tm,),
        in_specs=[_row(tm, d), _full((1, d)), _full((4, d, d))],
        out_specs=_row(tm, 4 * d), out_shape=jax.ShapeDtypeStruct((lp, 4 * d), BF16),
        compiler_params=_params(("parallel",)),
    )(h1, g, w4)


def _square_bf16(r):
    rf = r.astype(F32)
    return (rf * rf).astype(BF16)


def _mlp_out(h1, r, w2, tm):
    lp, d = h1.shape
    ff = r.shape[1]

    def body(x_ref, r_ref, w_ref, o_ref):
        o_ref[...] = x_ref[...] + _dot(_square_bf16(r_ref[...]), w_ref[...])

    return pl.pallas_call(
        body, name="mlp_out", grid=(lp // tm,),
        in_specs=[_row(tm, d), _row(tm, ff), _full((ff, d))],
        out_specs=_row(tm, d), out_shape=jax.ShapeDtypeStruct((lp, d), F32),
        compiler_params=_params(("parallel",)),
    )(h1, r, w2)


def _final_loss(h3, g, tgt, rowmask, tm):
    lp, d = h3.shape

    def body(x_ref, g_ref, t_ref, m_ref, loss_ref, dx_ref, dg_ref):
        first = pl.program_id(0) == 0
        x = x_ref[...]
        gg = g_ref[...]
        err = (_rms(x, gg) - t_ref[...]) * m_ref[...]
        part = 0.5 * jnp.sum(jnp.sum(err * err, axis=1, keepdims=True), axis=0, keepdims=True) * (1.0 / d)
        part = jnp.broadcast_to(part, (SUBLANES, LANES))

        @pl.when(first)
        def _():
            loss_ref[...] = part

        @pl.when(jnp.logical_not(first))
        def _():
            loss_ref[...] += part

        dx, dgr = _rms_bwd(err * (1.0 / d), x, gg)
        dx_ref[...] = dx
        _acc_rows(dg_ref, dgr, first)

    return pl.pallas_call(
        body, name="final_loss", grid=(lp // tm,),
        in_specs=[_row(tm, d), _full((1, d)), _row(tm, d), _row(tm, 1)],
        out_specs=[_full((SUBLANES, LANES)), _row(tm, d), _full((1, d))],
        out_shape=[jax.ShapeDtypeStruct((SUBLANES, LANES), F32), jax.ShapeDtypeStruct((lp, d), F32),
                   jax.ShapeDtypeStruct((1, d), F32)],
        compiler_params=_params(("arbitrary",)),
    )(h3, g, tgt, rowmask)


def _mlp_bwd_a(dh3, r, w2, tm):
    lp, d = dh3.shape
    ff = r.shape[1]

    def body(dh_ref, r_ref, w_ref, dz_ref, dhb_ref):
        dhb = dh_ref[...].astype(BF16)
        dhb_ref[...] = dhb
        da = _dot_nt(dhb, w_ref[...])
        dz_ref[...] = (da * (2.0 * r_ref[...].astype(F32))).astype(BF16)

    return pl.pallas_call(
        body, name="mlp_bwd_a", grid=(lp // tm,),
        in_specs=[_row(tm, d), _row(tm, ff), _full((ff, d))],
        out_specs=[_row(tm, ff), _row(tm, d)],
        out_shape=[jax.ShapeDtypeStruct((lp, ff), BF16), jax.ShapeDtypeStruct((lp, d), BF16)],
        compiler_params=_params(("parallel",)),
    )(dh3, r, w2)


def _mlp_bwd_b(dz, dh3, h1, g, w4, tm):
    lp, d = h1.shape

    def body(dz_ref, dh_ref, x_ref, g_ref, w_ref, dx_ref, dg_ref):
        first = pl.program_id(0) == 0
        dh2 = _dot_nt(dz_ref[:, :d], w_ref[0])
        for s in range(1, 4):
            dh2 += _dot_nt(dz_ref[:, s * d:(s + 1) * d], w_ref[s])
        dx, dgr = _rms_bwd(dh2, x_ref[...], g_ref[...])
        dx_ref[...] = dh_ref[...] + dx
        _acc_rows(dg_ref, dgr, first)

    return pl.pallas_call(
        body, name="mlp_bwd_b", grid=(lp // tm,),
        in_specs=[_row(tm, 4 * d), _row(tm, d), _row(tm, d), _full((1, d)), _full((4, d, d))],
        out_specs=[_row(tm, d), _full((1, d))],
        out_shape=[jax.ShapeDtypeStruct((lp, d), F32), jax.ShapeDtypeStruct((1, d), F32)],
        compiler_params=_params(("arbitrary",)),
    )(dz, dh3, h1, g, w4)


def _in_proj_bwd(dyv, du0, du1, dskip, dqkv, dgates, dres, xin, g, w4, tm):
    lp, d = xin.shape
    hd = d // 2

    def body(dy_ref, a_ref, b_ref, ds_ref, dq_ref, dgt_ref, dr_ref, x_ref, g_ref, w_ref, dx_ref, dg_ref, dp_ref):
        first = pl.program_id(0) == 0
        du = (dy_ref[...] * ds_ref[...] + a_ref[...] + b_ref[...]).astype(BF16)
        dq = dq_ref[...].astype(BF16)
        dgt = dgt_ref[...].astype(BF16)
        dp_ref[:, :hd] = du
        dp_ref[:, hd:2 * d] = dq
        dp_ref[:, 2 * d:] = dgt
        dh = _dot_nt(du, w_ref[0, :, :hd]) + _dot_nt(dq[:, :hd], w_ref[0, :, hd:])
        dh += _dot_nt(dq[:, hd:], w_ref[1])
        dh += _dot_nt(dgt[:, :d], w_ref[2]) + _dot_nt(dgt[:, d:], w_ref[3])
        dx, dgr = _rms_bwd(dh, x_ref[...], g_ref[...])
        dx_ref[...] = dr_ref[...] + dx
        _acc_rows(dg_ref, dgr, first)

    return pl.pallas_call(
        body, name="in_proj_bwd", grid=(lp // tm,),
        in_specs=[_row(tm, hd), _row(tm, hd), _row(tm, hd), _full((1, hd)), _row(tm, 3 * hd), _row(tm, 2 * d),
                  _row(tm, d), _row(tm, d), _full((1, d)), _full((4, d, d))],
        out_specs=[_row(tm, d), _full((1, d)), _row(tm, 4 * d)],
        out_shape=[jax.ShapeDtypeStruct((lp, d), F32), jax.ShapeDtypeStruct((1, d), F32),
                   jax.ShapeDtypeStruct((lp, 4 * d), BF16)],
        compiler_params=_params(("arbitrary",)),
    )(dyv, du0, du1, dskip, dqkv, dgates, dres, xin, g, w4)


def _wgrad(a, dy, nshard, tm, tn, name, gain=None, square=False):
    lp, k = a.shape
    n = dy.shape[1]
    ns = n // nshard
    assert ns % tn == 0
    per = ns // tn

    def body(*refs):
        if gain is not None:
            a_ref, g_ref, dy_ref, o_ref = refs
            at = _rms(a_ref[...], g_ref[...]).astype(BF16)
        else:
            a_ref, dy_ref, o_ref = refs
            at = _square_bf16(a_ref[...]) if square else a_ref[...]
        i = pl.program_id(1)
        acc = _dot_tn(at, dy_ref[...])

        @pl.when(i == 0)
        def _():
            o_ref[0] = acc

        @pl.when(i != 0)
        def _():
            o_ref[0] += acc

    in_specs = [pl.BlockSpec((tm, k), lambda j, i: (i, 0))]
    args = [a]
    if gain is not None:
        in_specs.append(pl.BlockSpec((1, k), lambda j, i: (0, 0)))
        args.append(gain)
    in_specs.append(pl.BlockSpec((tm, tn), lambda j, i: (i, j)))
    args.append(dy)
    return pl.pallas_call(
        body, name=name, grid=(n // tn, lp // tm), in_specs=in_specs,
        out_specs=pl.BlockSpec((1, k, tn), lambda j, i: (j // per, 0, j % per)),
        out_shape=jax.ShapeDtypeStruct((nshard, k, ns), F32),
        compiler_params=_params(("parallel", "arbitrary")),
    )(*args)


def _head_tables(d):
    idx = np.arange(LANES)
    mean = (idx[:, None] // HEAD_DIM == idx[None, :] // HEAD_DIM).astype(np.float32) / HEAD_DIM
    n_heads = d // HEAD_DIM
    kvh = n_heads // GQA_REP
    c = np.arange(d)
    col = np.arange(kvh * LANES)
    head_of_col = (col // LANES) * GQA_REP + (col % LANES)
    sel = ((c[:, None] // HEAD_DIM == head_of_col[None, :]) & ((col % LANES) < GQA_REP)[None, :]).astype(np.float32)
    return jnp.asarray(mean, BF16), jnp.asarray(sel, BF16)


def _swap_pairs(y):
    lane = lax.broadcasted_iota(jnp.int32, y.shape, 1)
    return jnp.where(lane % 2 == 0, pltpu.roll(y, LANES - 1, 1), pltpu.roll(y, 1, 1))


def _qk_prep(qkv, cos_t, sin_t, qg, kg, mean_m, tm):
    lp, wq = qkv.shape
    d = wq * 2 // 3
    kvw = d // 4
    kvh = kvw // HEAD_DIM
    scale = HEAD_DIM ** -0.5

    def body(x_ref, c_ref, s_ref, qg_ref, kg_ref, m_ref, q_ref, k_ref, v_ref):
        cs, sn, mm = c_ref[...], s_ref[...], m_ref[...]
        for b in range((d + kvw) // LANES):
            x = x_ref[:, b * LANES:(b + 1) * LANES]
            gg = qg_ref[...] if b < d // LANES else kg_ref[...]
            y = x * lax.rsqrt(_split_dot(x * x, mm) + NORM_EPS) * gg
            out = y * cs + _swap_pairs(y) * sn
            if b < d // LANES:
                q_ref[:, b * LANES:(b + 1) * LANES] = (out * scale).astype(BF16)
            else:
                kb = b - d // LANES
                k_ref[2 * kb] = out[:, :HEAD_DIM].astype(BF16)
                k_ref[2 * kb + 1] = out[:, HEAD_DIM:].astype(BF16)
        ones = jnp.ones((tm, LANES - HEAD_DIM), BF16)
        for h in range(kvh):
            vh = x_ref[:, d + kvw + h * HEAD_DIM:d + kvw + (h + 1) * HEAD_DIM].astype(BF16)
            v_ref[h] = jnp.concatenate([vh, ones], axis=1)

    k_spec = pl.BlockSpec((kvh, tm, HEAD_DIM), lambda i: (0, i, 0))
    v_spec = pl.BlockSpec((kvh, tm, LANES), lambda i: (0, i, 0))
    return pl.pallas_call(
        body, name="qk_prep", grid=(lp // tm,),
        in_specs=[_row(tm, wq), _row(tm, LANES), _row(tm, LANES), _full((1, LANES)), _full((1, LANES)),
                  _full((LANES, LANES))],
        out_specs=[_row(tm, d), k_spec, v_spec],
        out_shape=[jax.ShapeDtypeStruct((lp, d), BF16), jax.ShapeDtypeStruct((kvh, lp, HEAD_DIM), BF16),
                   jax.ShapeDtypeStruct((kvh, lp, LANES), BF16)],
        compiler_params=_params(("parallel",)),
    )(qkv, cos_t, sin_t, qg, kg, mean_m)


def _qk_bwd(qkv, dq, dk, dv, cos_t, sin_t, qg, kg, mean_m, tm):
    lp, wq = qkv.shape
    d = wq * 2 // 3
    kvw = d // 4
    kvh = kvw // HEAD_DIM
    scale = HEAD_DIM ** -0.5

    def body(x_ref, dq_ref, dk_ref, dv_ref, c_ref, s_ref, qg_ref, kg_ref, m_ref, o_ref, dqg_ref, dkg_ref):
        first = pl.program_id(0) == 0
        cs, sn, mm = c_ref[...], s_ref[...], m_ref[...]
        sums = [None, None]
        for b in range((d + kvw) // LANES):
            is_q = b < d // LANES
            x = x_ref[:, b * LANES:(b + 1) * LANES]
            gg = qg_ref[...] if is_q else kg_ref[...]
            r = lax.rsqrt(_split_dot(x * x, mm) + NORM_EPS)
            nrm = x * r
            if is_q:
                dout = dq_ref[:, b * LANES:(b + 1) * LANES] * scale
            else:
                kb = b - d // LANES
                dout = jnp.concatenate([dk_ref[2 * kb], dk_ref[2 * kb + 1]], axis=1)
            dy = dout * cs + _swap_pairs(dout * sn)
            part = jnp.sum(dy * nrm, axis=0, keepdims=True)
            sums[0 if is_q else 1] = part if sums[0 if is_q else 1] is None else sums[0 if is_q else 1] + part
            dn = dy * gg
            o_ref[:, b * LANES:(b + 1) * LANES] = r * (dn - nrm * _split_dot(dn * nrm, mm))
        for h in range(kvh):
            o_ref[:, d + kvw + h * HEAD_DIM:d + kvw + (h + 1) * HEAD_DIM] = dv_ref[h]
        for ref, s in ((dqg_ref, sums[0]), (dkg_ref, sums[1])):
            s = s + pltpu.roll(s, HEAD_DIM, 1)

            @pl.when(first)
            def _(ref=ref, s=s):
                ref[...] = s

            @pl.when(jnp.logical_not(first))
            def _(ref=ref, s=s):
                ref[...] += s

    kv_spec = pl.BlockSpec((kvh, tm, HEAD_DIM), lambda i: (0, i, 0))
    return pl.pallas_call(
        body, name="qk_bwd", grid=(lp // tm,),
        in_specs=[_row(tm, wq), _row(tm, d), kv_spec, kv_spec, _row(tm, LANES), _row(tm, LANES),
                  _full((1, LANES)), _full((1, LANES)), _full((LANES, LANES))],
        out_specs=[_row(tm, wq), _full((1, LANES)), _full((1, LANES))],
        out_shape=[jax.ShapeDtypeStruct((lp, wq), F32), jax.ShapeDtypeStruct((1, LANES), F32),
                   jax.ShapeDtypeStruct((1, LANES), F32)],
        compiler_params=_params(("arbitrary",)),
    )(qkv, dq, dk, dv, cos_t, sin_t, qg, kg, mean_m)


def _attn_fwd(q, k, v, kbias, tq, tk, gather=()):
    lp, d = q.shape
    kvh = k.shape[0]
    rw = GQA_REP * HEAD_DIM
    nk = lp // tk

    ng = len(gather)
    steps = kvh * (lp // tq) * nk

    def body(*refs):
        q_ref, k_ref, v_ref, kb_ref = refs[:4]
        o_ref, lse_ref, pt_ref, mb_ref = refs[4 + ng:8 + ng]
        m_s, acc_s = refs[8 + 2 * ng:10 + 2 * ng]
        j = pl.program_id(2)

        if ng:
            phases = _gather_phases(refs[4:4 + ng], refs[8 + ng:8 + 2 * ng], *refs[10 + 2 * ng:])
            step = (pl.program_id(0) * (lp // tq) + pl.program_id(1)) * nk + j
            for n, phase in enumerate(phases):
                pl.when(step == n * steps // 3)(phase)

        @pl.when(j == 0)
        def _():
            m_s[...] = jnp.full(m_s.shape, MASK_VALUE, F32)
            acc_s[...] = jnp.zeros(acc_s.shape, F32)

        def heads(masked):
            kk, vv = k_ref[0], v_ref[0]

            def scores(h):
                return _dot_nt(q_ref[:, h * HEAD_DIM:(h + 1) * HEAD_DIM], kk)

            def softmax(h, s):
                if masked:
                    s = jnp.concatenate([s[:, :tk - CHUNK], s[:, tk - CHUNK:] + kb_ref[:, tk - CHUNK:]], axis=1)
                m_prev = m_s[h]
                m_new = jnp.maximum(m_prev, jnp.max(s, axis=1, keepdims=True))
                m_s[h] = m_new
                p = jnp.exp(s - m_new[:, :1]).astype(BF16)
                pt_ref[h] = p
                return p, jnp.exp(m_prev - m_new), m_new

            def accumulate(h, p, alpha):
                acc_s[h] = acc_s[h] * alpha + _dot(p, vv)

            ss = [scores(h) for h in range(GQA_REP)]
            pa = [softmax(h, ss[h]) for h in range(GQA_REP)]
            for h in range(GQA_REP):
                accumulate(h, *pa[h][:2])
            lane = lax.broadcasted_iota(jnp.int32, (tq, LANES), 1)
            mb = jnp.zeros((tq, LANES), F32)
            for h in range(GQA_REP):
                mb = jnp.where(lane == h, pa[h][2], mb)
            mb_ref[0] = mb

        pl.when(j != nk - 1)(functools.partial(heads, False))
        pl.when(j == nk - 1)(functools.partial(heads, True))

        @pl.when(j == nk - 1)
        def _():
            lane = lax.broadcasted_iota(jnp.int32, (tq, LANES), 1)
            lse = jnp.zeros((tq, LANES), F32)
            outs = []
            for h in range(GQA_REP):
                acc = acc_s[h]
                den = pltpu.roll(acc, HEAD_DIM, 1)
                outs.append((acc / den)[:, :HEAD_DIM])
                lse = jnp.where(lane == h, m_s[h] + jnp.log(den), lse)
            o_ref[...] = jnp.concatenate(outs, axis=1).astype(BF16)
            lse_ref[...] = lse

    sems = [pltpu.SemaphoreType.DMA((ng, 7)), pltpu.SemaphoreType.DMA((ng, 7)), pltpu.SemaphoreType.DMA((ng,))]
    res = pl.pallas_call(
        body, name="attn_fwd", grid=(kvh, lp // tq, nk),
        in_specs=[pl.BlockSpec((tq, rw), lambda g, i, j: (i, g)),
                  pl.BlockSpec((1, tk, HEAD_DIM), lambda g, i, j: (g, j, 0)),
                  pl.BlockSpec((1, tk, LANES), lambda g, i, j: (g, j, 0)),
                  pl.BlockSpec((1, tk), lambda g, i, j: (0, j))] + [_ANY] * ng,
        out_specs=[pl.BlockSpec((tq, rw), lambda g, i, j: (i, g)),
                   pl.BlockSpec((tq, LANES), lambda g, i, j: (i, g)),
                   pl.BlockSpec((GQA_REP, tq, tk), lambda g, i, j: (g, i, j)),
                   pl.BlockSpec((1, tq, LANES), lambda g, i, j: (j, i, g))] + [_ANY] * ng,
        out_shape=[jax.ShapeDtypeStruct((lp, d), BF16), jax.ShapeDtypeStruct((lp, kvh * LANES), F32),
                   jax.ShapeDtypeStruct((kvh * GQA_REP, lp, lp), BF16), jax.ShapeDtypeStruct((nk, lp, kvh * LANES), F32)]
        + [jax.ShapeDtypeStruct((8,) + b.shape, b.dtype) for b in gather],
        scratch_shapes=[pltpu.VMEM((GQA_REP, tq, LANES), F32), pltpu.VMEM((GQA_REP, tq, LANES), F32)]
        + (sems if ng else []),
        compiler_params=_params(("arbitrary", "arbitrary", "arbitrary")),
    )(q, k, v, kbias, *gather)
    return res[0], res[1], res[2], res[3], list(res[4:])


def _attn_bwd(q, k, v, pt, mblk, do, lse, delta, tq, tk):
    lp, d = q.shape
    kvh = k.shape[0]
    rw = GQA_REP * HEAD_DIM
    nq = lp // tq

    def body(q_ref, k_ref, v_ref, pt_ref, mb_ref, do_ref, lse_ref, dl_ref, dq_ref, dk_ref, dv_ref, dk_s, dv_s):
        j = pl.program_id(1)
        i = pl.program_id(2)

        @pl.when(jnp.logical_and(i == 0, j == 0))
        def _():
            dq_ref[...] = jnp.zeros(dq_ref.shape, F32)

        @pl.when(i == 0)
        def _():
            dk_s[...] = jnp.zeros(dk_s.shape, F32)
            dv_s[...] = jnp.zeros(dv_s.shape, F32)

        kk, vv = k_ref[0], v_ref[0][:, :HEAD_DIM]
        scale = jnp.exp(mb_ref[0] - lse_ref[...])
        dl = dl_ref[...] * scale
        dqs = []
        for pair in ((0, 1), (2, 3)):
            dos = {h: (do_ref[:, h * HEAD_DIM:(h + 1) * HEAD_DIM].astype(F32) * scale[:, h:h + 1]).astype(BF16)
                   for h in pair}
            dps = {h: _dot_nt(dos[h], vv) for h in pair}
            for h in pair:
                dv_s[...] += _dot_tn(pt_ref[h], dos[h])
            dss = {h: (pt_ref[h].astype(F32) * (dps[h] - dl[:, h:h + 1])).astype(BF16) for h in pair}
            for h in pair:
                dk_s[...] += _dot_tn(dss[h], q_ref[:, h * HEAD_DIM:(h + 1) * HEAD_DIM])
                dqs.append(_dot(dss[h], kk))
        rows = pl.ds(pl.multiple_of(i * tq, tq), tq)
        dq_ref[rows, :] += jnp.concatenate(dqs, axis=1)

        @pl.when(i == nq - 1)
        def _():
            dk_ref[0] = dk_s[...]
            dv_ref[0] = dv_s[...]

    return pl.pallas_call(
        body, name="attn_bwd", grid=(kvh, lp // tk, nq),
        in_specs=[pl.BlockSpec((tq, rw), lambda g, j, i: (i, g)),
                  pl.BlockSpec((1, tk, HEAD_DIM), lambda g, j, i: (g, j, 0)),
                  pl.BlockSpec((1, tk, LANES), lambda g, j, i: (g, j, 0)),
                  pl.BlockSpec((GQA_REP, tq, tk), lambda g, j, i: (g, i, j)),
                  pl.BlockSpec((1, tq, LANES), lambda g, j, i: (j, i, g)),
                  pl.BlockSpec((tq, rw), lambda g, j, i: (i, g)),
                  pl.BlockSpec((tq, LANES), lambda g, j, i: (i, g)),
                  pl.BlockSpec((tq, LANES), lambda g, j, i: (i, g))],
        out_specs=[pl.BlockSpec((lp, rw), lambda g, j, i: (0, g)),
                   pl.BlockSpec((1, tk, HEAD_DIM), lambda g, j, i: (g, j, 0)),
                   pl.BlockSpec((1, tk, HEAD_DIM), lambda g, j, i: (g, j, 0))],
        out_shape=[jax.ShapeDtypeStruct((lp, d), F32), jax.ShapeDtypeStruct((kvh, lp, HEAD_DIM), F32),
                   jax.ShapeDtypeStruct((kvh, lp, HEAD_DIM), F32)],
        scratch_shapes=[pltpu.VMEM((tk, HEAD_DIM), F32), pltpu.VMEM((tk, HEAD_DIM), F32)],
        compiler_params=_params(("parallel", "arbitrary", "arbitrary")),
    )(q, k, v, pt, mblk, do, lse, delta)


def _ssm_math(a_re, a_im, log_dt, bt_re, bt_im):
    dt = jnp.exp(log_dt)
    lam_re = jnp.minimum(a_re, EIG_RE_MAX)
    lam_im = a_im
    mag = jnp.exp(lam_re * dt)
    ang = lam_im * dt
    lb_re = mag * jnp.cos(ang)
    lb_im = mag * jnp.sin(ang)
    num_re = lb_re - 1.0
    num_im = lb_im
    den = lam_re * lam_re + lam_im * lam_im
    f_re = (num_re * lam_re + num_im * lam_im) / den
    f_im = (num_im * lam_re - num_re * lam_im) / den
    bb_re = f_re[:, None, :] * bt_re - f_im[:, None, :] * bt_im
    bb_im = f_re[:, None, :] * bt_im + f_im[:, None, :] * bt_re
    return lb_re, lb_im, bb_re, bb_im


def _ssm_discretize(a_re, a_im, log_dt, bt_re, bt_im):
    nd, g, n = a_re.shape
    p = bt_re.shape[2]

    def body(ar_ref, ai_ref, ld_ref, br_ref, bi_ref, bbr_ref, bbi_ref, pr_ref, pi_ref, hr_ref, hi_ref):
        lb_re, lb_im, bb_re, bb_im = _ssm_math(ar_ref[0], ai_ref[0], ld_ref[0], br_ref[0], bi_ref[0])
        bbr_ref[0] = bb_re
        bbi_ref[0] = bb_im
        cr, ci = lb_re, lb_im
        for k in range(KSTEPS):
            pr_ref[0, k] = cr
            pi_ref[0, k] = ci
            if k < KSTEPS - 1:
                cr, ci = cr * lb_re - ci * lb_im, cr * lb_im + ci * lb_re
        for t in range(2):
            cr, ci = cr * cr - ci * ci, 2.0 * cr * ci
            hr_ref[0, t] = cr
            hi_ref[0, t] = ci

    s3 = pl.BlockSpec((1, g, n), lambda i: (i, 0, 0))
    s4 = pl.BlockSpec((1, g, p, n), lambda i: (i, 0, 0, 0))
    sp = pl.BlockSpec((1, KSTEPS, g, n), lambda i: (i, 0, 0, 0))
    sh = pl.BlockSpec((1, 2, g, n), lambda i: (i, 0, 0, 0))
    return pl.pallas_call(
        body, name="ssm_discretize", grid=(nd,),
        in_specs=[s3, s3, pl.BlockSpec((1, g, 1), lambda i: (i, 0, 0)), s4, s4],
        out_specs=[s4, s4, sp, sp, sh, sh],
        out_shape=[jax.ShapeDtypeStruct((nd, g, p, n), F32)] * 2 + [jax.ShapeDtypeStruct((nd, KSTEPS, g, n), F32)] * 2
        + [jax.ShapeDtypeStruct((nd, 2, g, n), F32)] * 2,
        compiler_params=_params(("parallel",)),
    )(a_re, a_im, log_dt, bt_re, bt_im)


def _ssm_param_bwd(a_re, a_im, log_dt, bt_re, bt_im, dlb_re, dlb_im, dbb_re, dbb_im):
    nd, g, n = a_re.shape
    p = bt_re.shape[2]

    def body(ar_ref, ai_ref, ld_ref, br_ref, bi_ref, c0_ref, c1_ref, c2_ref, c3_ref,
             o0_ref, o1_ref, o2_ref, o3_ref, o4_ref):
        _, vjp = jax.vjp(_ssm_math, ar_ref[0], ai_ref[0], ld_ref[0], br_ref[0], bi_ref[0])
        outs = vjp((c0_ref[0], c1_ref[0], c2_ref[0], c3_ref[0]))
        for ref, val in zip((o0_ref, o1_ref, o2_ref, o3_ref, o4_ref), outs):
            ref[0] = val

    s3 = pl.BlockSpec((1, g, n), lambda i: (i, 0, 0))
    s1 = pl.BlockSpec((1, g, 1), lambda i: (i, 0, 0))
    s4 = pl.BlockSpec((1, g, p, n), lambda i: (i, 0, 0, 0))
    return pl.pallas_call(
        body, name="ssm_param_bwd", grid=(nd,),
        in_specs=[s3, s3, s1, s4, s4, s3, s3, s4, s4],
        out_specs=[s3, s3, s1, s4, s4],
        out_shape=[jax.ShapeDtypeStruct((nd, g, n), F32)] * 2 + [jax.ShapeDtypeStruct((nd, g, 1), F32)]
        + [jax.ShapeDtypeStruct((nd, g, p, n), F32)] * 2,
        compiler_params=_params(("parallel",)),
    )(a_re, a_im, log_dt, bt_re, bt_im, dlb_re, dlb_im, dbb_re, dbb_im)


def _cmul(ar, ai, xr, xi, conj):
    if conj:
        return ar * xr + ai * xi, ar * xi - ai * xr
    return ar * xr - ai * xi, ar * xi + ai * xr


def _scan_chunk(buf, tab, carry, ein, nj, rev, conj, base=0):
    ks = list(range(KSTEPS))
    if rev:
        ks = ks[::-1]
    sub = lax.broadcasted_iota(jnp.int32, (SUBLANES, SCAN_LANES), 0)
    edge = sub == (SUBLANES - 1 if rev else 0)

    def step(j, _):
        jr, ji = j, nj + j
        ar, ai = tab[base, jr], tab[base, ji]
        hr = jnp.zeros((SUBLANES, SCAN_LANES), F32)
        hi = jnp.zeros((SUBLANES, SCAN_LANES), F32)
        for k in ks:
            rows = pl.ds(k * SUBLANES, SUBLANES)
            pr, pi_ = _cmul(ar, ai, hr, hi, conj)
            hr = pr + buf[jr, rows, :]
            hi = pi_ + buf[ji, rows, :]
            buf[jr, rows, :] = hr
            buf[ji, rows, :] = hi
        shift = SUBLANES - 1 if rev else 1
        er = jnp.where(edge, carry[jr], pltpu.roll(hr, shift, 0))
        ei = jnp.where(edge, carry[ji], pltpu.roll(hi, shift, 0))
        for t, dist in enumerate((1, 2, 4)):
            sh = SUBLANES - dist if rev else dist
            pr, pi_ = _cmul(tab[base + 1 + t, jr], tab[base + 1 + t, ji], pltpu.roll(er, sh, 0), pltpu.roll(ei, sh, 0), conj)
            er, ei = er + pr, ei + pi_
        ein[jr] = er
        ein[ji] = ei
        pr, pi_ = _cmul(tab[base + 4 + KSTEPS - 1, jr], tab[base + 4 + KSTEPS - 1, ji], er, ei, conj)
        last = 0 if rev else SUBLANES - 1
        carry[jr] = jnp.broadcast_to((hr + pr)[last:last + 1, :], (SUBLANES, SCAN_LANES))
        carry[ji] = jnp.broadcast_to((hi + pi_)[last:last + 1, :], (SUBLANES, SCAN_LANES))
        for n, k in enumerate(ks):
            rows = pl.ds(k * SUBLANES, SUBLANES)
            pr, pi_ = _cmul(tab[base + 4 + n, jr], tab[base + 4 + n, ji], er, ei, conj)
            buf[jr, rows, :] += pr
            buf[ji, rows, :] += pi_
        return 0

    lax.fori_loop(0, nj, step, 0)


def _state_lanes(b):
    per = SCAN_LANES // SSM_BLOCK
    return b // per, slice((b % per) * SSM_BLOCK, (b % per + 1) * SSM_BLOCK)


def _project_in(src, w_ref, buf, nj):
    nb, cb, _ = w_ref.shape
    for b in range(nb):
        res = _dot(src[:, b * cb:(b + 1) * cb], w_ref[b])
        j, lanes = _state_lanes(b)
        buf[j, :, lanes] = res[:, :SSM_BLOCK]
        buf[nj + j, :, lanes] = res[:, SSM_BLOCK:]


def _state_block(buf, b, nj):
    j, lanes = _state_lanes(b)
    return jnp.concatenate([buf[j, :, lanes], buf[nj + j, :, lanes]], axis=1).astype(BF16)


def _project_out(buf, w_ref, nj):
    return jnp.concatenate([_dot_nt(_state_block(buf, b, nj), w_ref[b]) for b in range(w_ref.shape[0])], axis=1)


def _ssm_fwd(u, wb, wct, tab, rev, name):
    lp, w = u.shape
    nb, cb, _ = wb.shape
    nj = nb * SSM_BLOCK // SCAN_LANES
    nc = lp // CHUNK
    ntab = tab.shape[0]
    cidx = (lambda c: nc - 1 - c) if rev else (lambda c: c)

    def body(u_ref, wb_ref, wct_ref, tab_ref, y_ref, ck_ref, buf, carry, ein):
        @pl.when(pl.program_id(0) == 0)
        def _():
            carry[...] = jnp.zeros(carry.shape, F32)

        _project_in(u_ref[...].astype(BF16), wb_ref, buf, nj)
        ck_ref[0] = carry[...]
        _scan_chunk(buf, tab_ref, carry, ein, nj, rev, False)
        y_ref[...] = _project_out(buf, wct_ref, nj)

    wshape = (nb, cb, 2 * SSM_BLOCK)
    return pl.pallas_call(
        body, name=name, grid=(nc,),
        in_specs=[pl.BlockSpec((CHUNK, w), lambda c: (cidx(c), 0)), _full(wshape), _full(wshape),
                  _full((ntab, 2 * nj, SUBLANES, SCAN_LANES))],
        out_specs=[pl.BlockSpec((CHUNK, w), lambda c: (cidx(c), 0)),
                   pl.BlockSpec((1, 2 * nj, SUBLANES, SCAN_LANES), lambda c: (cidx(c), 0, 0, 0))],
        out_shape=[jax.ShapeDtypeStruct((lp, w), F32), jax.ShapeDtypeStruct((nc, 2 * nj, SUBLANES, SCAN_LANES), F32)],
        scratch_shapes=[pltpu.VMEM((2 * nj, CHUNK, SCAN_LANES), F32), pltpu.VMEM((2 * nj, SUBLANES, SCAN_LANES), F32),
                        pltpu.VMEM((2 * nj, SUBLANES, SCAN_LANES), F32)],
        compiler_params=_params(("arbitrary",)),
    )(u, wb, wct, tab)


def _ssm_bwd(u, dy, ckpt, wb, wct, tab, rev, name, scatter=()):
    lp, w = u.shape
    nb, cb, _ = wb.shape
    nj = nb * SSM_BLOCK // SCAN_LANES
    nc = lp // CHUNK
    ntab = tab.shape[0]
    cidx = (lambda c: c) if rev else (lambda c: nc - 1 - c)

    ns = len(scatter)

    def body(*refs):
        u_ref, dy_ref, ck_ref, wb_ref, wct_ref, tab_hbm = refs[:6]
        du_ref, dbb_ref, dcc_ref, dlb_ref = refs[6 + ns:10 + ns]
        tab_ref, dwb_ref, dwc_ref, xs, ls, xcar, lcar, xin, lin = refs[10 + 2 * ns:19 + 2 * ns]
        c = pl.program_id(0)

        if ns:
            start, finish = _scatter_phases(refs[6:6 + ns], refs[10 + ns:10 + 2 * ns], *refs[19 + 2 * ns:])
            pl.when(c == 0)(start)
            pl.when(c == nc - 1)(finish)

        @pl.when(c == 0)
        def _():
            pltpu.sync_copy(tab_hbm, tab_ref)
            lcar[...] = jnp.zeros(lcar.shape, F32)
            dwb_ref[...] = jnp.zeros(dwb_ref.shape, F32)
            dwc_ref[...] = jnp.zeros(dwc_ref.shape, F32)
            dlb_ref[...] = jnp.zeros(dlb_ref.shape, F32)

        ub = u_ref[...].astype(BF16)
        dyb = dy_ref[...].astype(BF16)
        _project_in(ub, wb_ref, xs, nj)
        xcar[...] = ck_ref[0]
        _scan_chunk(xs, tab_ref, xcar, xin, nj, rev, False)
        _project_in(dyb, wct_ref, ls, nj)
        _scan_chunk(ls, tab_ref, lcar, lin, nj, not rev, True, base=ntab // 2)
        dus = []
        for b in range(nb):
            chans = slice(b * cb, (b + 1) * cb)
            xb = _state_block(xs, b, nj)
            lb = _state_block(ls, b, nj)
            dwc_ref[b] += _dot_tn(dyb[:, chans], xb)
            dwb_ref[b] += _dot_tn(ub[:, chans], lb)
            dus.append(_dot_nt(lb, wb_ref[b]))
        du_ref[...] = jnp.concatenate(dus, axis=1)

        def step(j, _):
            jr, ji = j, nj + j
            ar = jnp.zeros((SUBLANES, SCAN_LANES), F32)
            ai = jnp.zeros((SUBLANES, SCAN_LANES), F32)
            for k in range(KSTEPS):
                kp = k + 1 if rev else k - 1
                rows = pl.ds(k * SUBLANES, SUBLANES)
                if 0 <= kp < KSTEPS:
                    prow = pl.ds(kp * SUBLANES, SUBLANES)
                    xr, xi = xs[jr, prow, :], xs[ji, prow, :]
                else:
                    xr, xi = xin[jr], xin[ji]
                lr, li = ls[jr, rows, :], ls[ji, rows, :]
                ar += lr * xr + li * xi
                ai += li * xr - lr * xi
            dlb_ref[jr] += ar
            dlb_ref[ji] += ai
            return 0

        lax.fori_loop(0, nj, step, 0)

        @pl.when(c == nc - 1)
        def _():
            for b in range(2 * nj):
                dlb_ref[b] = jnp.broadcast_to(jnp.sum(dlb_ref[b], axis=0, keepdims=True), (SUBLANES, SCAN_LANES))
            for g in range(w // SSM_GROUP):
                b, gl = divmod(g, cb // SSM_GROUP)
                rows = slice(gl * SSM_GROUP, (gl + 1) * SSM_GROUP)
                for part in range(2):
                    cols = slice(part * SSM_BLOCK + gl * SSM_STATE, part * SSM_BLOCK + (gl + 1) * SSM_STATE)
                    dbb_ref[part, g * SSM_GROUP:(g + 1) * SSM_GROUP, :] = dwb_ref[b, rows, cols]
                    dcc_ref[part, g * SSM_GROUP:(g + 1) * SSM_GROUP, :] = dwc_ref[b, rows, cols]

    st = (2 * nj, SUBLANES, SCAN_LANES)
    wshape = (nb, cb, 2 * SSM_BLOCK)
    sems = [pltpu.SemaphoreType.DMA((ns, 3)), pltpu.SemaphoreType.DMA((ns, 3)), pltpu.SemaphoreType.DMA((ns,))]
    res = pl.pallas_call(
        body, name=name, grid=(nc,),
        in_specs=[pl.BlockSpec((CHUNK, w), lambda c: (cidx(c), 0)), pl.BlockSpec((CHUNK, w), lambda c: (cidx(c), 0)),
                  pl.BlockSpec((1,) + st, lambda c: (cidx(c), 0, 0, 0)), _full(wshape), _full(wshape), _ANY]
        + [_ANY] * ns,
        out_specs=[pl.BlockSpec((CHUNK, w), lambda c: (cidx(c), 0)), _full((2, w, SSM_STATE)),
                   _full((2, w, SSM_STATE)), _full(st)] + [_ANY] * ns,
        out_shape=[jax.ShapeDtypeStruct((lp, w), F32), jax.ShapeDtypeStruct((2, w, SSM_STATE), F32),
                   jax.ShapeDtypeStruct((2, w, SSM_STATE), F32), jax.ShapeDtypeStruct(st, F32)]
        + [jax.ShapeDtypeStruct(p.shape, p.dtype) for p in scatter],
        scratch_shapes=[pltpu.VMEM((ntab,) + st, F32), pltpu.VMEM(wshape, F32), pltpu.VMEM(wshape, F32),
                        pltpu.VMEM((2 * nj, CHUNK, SCAN_LANES), F32), pltpu.VMEM((2 * nj, CHUNK, SCAN_LANES), F32),
                        pltpu.VMEM(st, F32), pltpu.VMEM(st, F32), pltpu.VMEM(st, F32), pltpu.VMEM(st, F32)]
        + (sems if ns else []),
        compiler_params=_params(("arbitrary",)),
    )(u, dy, ckpt, wb, wct, tab, *scatter)
    return res[0], res[1], res[2], res[3], list(res[4:])


def _embed_blocks(t_re, t_im):
    g, p, n = t_re.shape
    gb = SSM_BLOCK // n
    eye = jnp.eye(gb, dtype=t_re.dtype)
    parts = [jnp.einsum('bgpn,gh->bgphn', t.reshape(g // gb, gb, p, n), eye).reshape(g // gb, gb * p, gb * n)
             for t in (t_re, t_im)]
    return jnp.concatenate(parts, axis=2)


def _scan_tables(pw_re, pw_im, hi_re, hi_im, rev):
    s = pw_re.shape[1] * pw_re.shape[2]
    nj = s // SCAN_LANES
    sub = np.arange(SUBLANES)
    live = np.ones((4 + KSTEPS, 1, SUBLANES, 1), bool)
    for row, dist in ((1, 1), (2, 2), (3, 4)):
        live[row, 0, :, 0] = (sub < SUBLANES - dist) if rev else (sub >= dist)

    def lay(pw, hi):
        rows = jnp.concatenate([pw[:1], pw[KSTEPS - 1:], hi, pw], axis=0).reshape(4 + KSTEPS, nj, 1, SCAN_LANES)
        return jnp.where(live, jnp.broadcast_to(rows, (4 + KSTEPS, nj, SUBLANES, SCAN_LANES)), 0.0)

    return jnp.concatenate([lay(pw_re, hi_re), lay(pw_im, hi_im)], axis=1)


def _adamw(w, g, m, v, tm):
    r, c = w.shape
    c1 = 1.0 - ADAM_B1 ** ADAM_STEP
    c2 = 1.0 - ADAM_B2 ** ADAM_STEP

    def body(w_ref, g_ref, m_ref, v_ref, d_ref, nm_ref, nv_ref):
        gg = g_ref[...]
        nm = ADAM_B1 * m_ref[...] + (1.0 - ADAM_B1) * gg
        nv = ADAM_B2 * v_ref[...] + (1.0 - ADAM_B2) * (gg * gg)
        nm_ref[...] = nm
        nv_ref[...] = nv
        d_ref[...] = -ADAM_LR * ((nm / c1) / (jnp.sqrt(nv / c2) + ADAM_EPS) + ADAM_WD * w_ref[...])

    spec = _row(tm, c)
    return pl.pallas_call(
        body, name="adamw", grid=(r // tm,), in_specs=[spec] * 4, out_specs=[spec] * 3,
        out_shape=[jax.ShapeDtypeStruct((r, c), F32)] * 3, compiler_params=_params(("parallel",)),
    )(w, g, m, v)


def _pair_sum(g42, got, core, out_dtype, tm, name):
    _, _, r, c = g42.shape

    def body(core_ref, a_ref, b_ref, o_ref):
        o_ref[...] = (a_ref[...] + b_ref[...]).astype(out_dtype)

    grid_spec = pltpu.PrefetchScalarGridSpec(
        num_scalar_prefetch=1, grid=(4, r // tm),
        in_specs=[pl.BlockSpec((1, None, tm, c), lambda s, i, core_ref: (s, core_ref[0], i, 0)),
                  pl.BlockSpec((1, tm, c), lambda s, i, core_ref: (s, i, 0))],
        out_specs=pl.BlockSpec((1, tm, c), lambda s, i, core_ref: (s, i, 0)))
    return pl.pallas_call(
        body, name=name, grid_spec=grid_spec, out_shape=jax.ShapeDtypeStruct((4, r, c), out_dtype),
        compiler_params=_params(("parallel", "parallel")),
    )(core, g42, got)


def _sum4(a, core, tm, name):
    _, r, c = a.shape

    def body(core_ref, a_ref, o_ref):
        o_ref[...] = ((a_ref[0].astype(F32) + a_ref[1].astype(F32)) + a_ref[2].astype(F32)) + a_ref[3].astype(F32)

    grid_spec = pltpu.PrefetchScalarGridSpec(
        num_scalar_prefetch=1, grid=(r // tm,),
        in_specs=[pl.BlockSpec((4, tm, c), lambda i, core_ref: (0, i, 0))],
        out_specs=pl.BlockSpec((None, tm, c), lambda i, core_ref: (core_ref[0], i, 0)))
    return pl.pallas_call(
        body, name=name, grid_spec=grid_spec, out_shape=jax.ShapeDtypeStruct((2, r, c), F32),
        compiler_params=_params(("parallel",)),
    )(core, a)


_ANY = pl.BlockSpec(memory_space=pl.ANY)


def _gather_phases(xs, outs, send_sems, recv_sems, local_sems):
    n = len(xs)

    def parts():
        x, y, c = lax.axis_index("x"), lax.axis_index("y"), lax.axis_index("c")
        return c, (x, y, c), (x, y, 1 - c), [(1 - x, y), (x, 1 - y), (1 - x, 1 - y)]

    def slot(t, px, py, pc):
        return outs[t].at[4 * px + 2 * py + pc]

    def copy(t, k, blk, to, src=None):
        return pltpu.make_async_remote_copy(
            src_ref=slot(t, *blk) if src is None else src, dst_ref=slot(t, *blk),
            send_sem=send_sems.at[t, k], recv_sem=recv_sems.at[t, k], device_id=to, device_id_type=MESH_ID)

    def own(t, me):
        return pltpu.make_async_copy(xs[t], slot(t, *me), local_sems.at[t])

    def first(t, c, me, sibling, chips):
        return [copy(t, 0, me, sibling, src=xs[t])] + [copy(t, 1 + j, me, (*chip, c), src=xs[t])
                                                       for j, chip in enumerate(chips)]

    def passed(t, c, sibling, chips):
        return [copy(t, 4 + j, (*chip, c), sibling) for j, chip in enumerate(chips)]

    def start():
        c, me, sibling, chips = parts()
        for t in range(n):
            own(t, me).start()
        for t in range(n):
            for cp in first(t, c, me, sibling, chips):
                cp.start()

    def forward():
        c, me, sibling, chips = parts()
        for j, chip in enumerate(chips):
            for t in range(n):
                copy(t, 1 + j, (*chip, c), me).wait_recv()
                passed(t, c, sibling, chips)[j].start()

    def finish():
        c, me, sibling, chips = parts()
        for t in range(n):
            copy(t, 0, sibling, me).wait_recv()
        for j, chip in enumerate(chips):
            for t in range(n):
                copy(t, 4 + j, (*chip, 1 - c), me).wait_recv()
        for t in range(n):
            for cp in first(t, c, me, sibling, chips) + passed(t, c, sibling, chips):
                cp.wait_send()
            own(t, me).wait()

    return start, forward, finish


def _all_gather8(blocks, name):
    n = len(blocks)

    def body(*refs):
        for phase in _gather_phases(refs[:n], refs[n:2 * n], *refs[2 * n:]):
            phase()

    return pl.pallas_call(
        body, name=name, out_shape=[jax.ShapeDtypeStruct((8,) + b.shape, b.dtype) for b in blocks],
        in_specs=[_ANY] * n, out_specs=[_ANY] * n,
        scratch_shapes=[pltpu.SemaphoreType.DMA((n, 7)), pltpu.SemaphoreType.DMA((n, 7)),
                        pltpu.SemaphoreType.DMA((n,))],
    )(*blocks)


def _pair_exchange(gs, name):
    n = len(gs)

    def body(*refs):
        g_refs, outs = refs[:n], refs[n:2 * n]
        send_sems, recv_sems = refs[2 * n:]
        x, y, c = lax.axis_index("x"), lax.axis_index("y"), lax.axis_index("c")
        cps = [pltpu.make_async_remote_copy(
            src_ref=g_refs[t].at[:, 1 - c], dst_ref=outs[t], send_sem=send_sems.at[t], recv_sem=recv_sems.at[t],
            device_id=(x, y, 1 - c), device_id_type=MESH_ID) for t in range(n)]
        for cp in cps:
            cp.start()
        for cp in cps:
            cp.wait()

    return pl.pallas_call(
        body, name=name,
        out_shape=[jax.ShapeDtypeStruct((g.shape[0],) + g.shape[2:], g.dtype) for g in gs],
        in_specs=[_ANY] * n, out_specs=[_ANY] * n,
        scratch_shapes=[pltpu.SemaphoreType.DMA((n,)), pltpu.SemaphoreType.DMA((n,))],
    )(*gs)


def _scatter_phases(p_refs, outs, send_sems, recv_sems, local_sems):
    n = len(p_refs)

    def parts():
        x, y, c = lax.axis_index("x"), lax.axis_index("y"), lax.axis_index("c")
        return c, 2 * x + y, [(1 - x, y), (x, 1 - y), (1 - x, 1 - y)]

    def copy(t, k, src_slab, dst_slab, chip, c):
        return pltpu.make_async_remote_copy(
            src_ref=p_refs[t].at[src_slab], dst_ref=outs[t].at[dst_slab], send_sem=send_sems.at[t, k],
            recv_sem=recv_sems.at[t, k], device_id=(*chip, c), device_id_type=MESH_ID)

    def own(t, mine):
        return pltpu.make_async_copy(p_refs[t].at[mine], outs[t].at[mine], local_sems.at[t])

    def start():
        c, mine, chips = parts()
        for t in range(n):
            own(t, mine).start()
        for k, (cx, cy) in enumerate(chips):
            for t in range(n):
                copy(t, k, 2 * cx + cy, mine, (cx, cy), c).start()

    def finish():
        c, mine, chips = parts()
        for k, (cx, cy) in enumerate(chips):
            for t in range(n):
                copy(t, k, mine, 2 * cx + cy, (cx, cy), c).wait_recv()
        for t in range(n):
            for k, (cx, cy) in enumerate(chips):
                copy(t, k, 2 * cx + cy, mine, (cx, cy), c).wait_send()
            own(t, mine).wait()

    return start, finish


def _chip_scatter(ps, name):
    n = len(ps)

    def body(*refs):
        for phase in _scatter_phases(refs[:n], refs[n:2 * n], *refs[2 * n:]):
            phase()

    return pl.pallas_call(
        body, name=name, out_shape=[jax.ShapeDtypeStruct(p.shape, p.dtype) for p in ps],
        in_specs=[_ANY] * n, out_specs=[_ANY] * n,
        scratch_shapes=[pltpu.SemaphoreType.DMA((n, 3)), pltpu.SemaphoreType.DMA((n, 3)),
                        pltpu.SemaphoreType.DMA((n,))],
    )(*ps)


def _pair_gather(rs, name):
    n = len(rs)

    def body(*refs):
        ins, outs = refs[:n], refs[n:2 * n]
        send_sems, recv_sems = refs[2 * n:]
        x, y, c = lax.axis_index("x"), lax.axis_index("y"), lax.axis_index("c")

        def copy(t, slab):
            return pltpu.make_async_remote_copy(
                src_ref=ins[t].at[slab], dst_ref=outs[t].at[slab], send_sem=send_sems.at[t],
                recv_sem=recv_sems.at[t], device_id=(x, y, 1 - c), device_id_type=MESH_ID)

        sends = [copy(t, c) for t in range(n)]
        for cp in sends:
            cp.start()
        for t in range(n):
            copy(t, 1 - c).wait_recv()
        for cp in sends:
            cp.wait_send()

    return pl.pallas_call(
        body, name=name, out_shape=[jax.ShapeDtypeStruct(r.shape, r.dtype) for r in rs],
        in_specs=[_ANY] * n, out_specs=[_ANY] * n, input_output_aliases={t: t for t in range(n)},
        scratch_shapes=[pltpu.SemaphoreType.DMA((n,)), pltpu.SemaphoreType.DMA((n,))],
    )(*rs)


PACK_COLS = 1024
BIG = (("meta_tokens", 1), ("w_in", 1), ("w_glu", 0), ("w_ssm_proj", 1), ("w_attn_proj", 0), ("w_out", 0),
       ("w_mlp_in", 1), ("w_mlp_out", 0))
SMALL = ("norm_mix_g", "ssm_a_re", "ssm_a_im", "ssm_log_dt", "ssm_b_re", "ssm_b_im", "ssm_c_re", "ssm_c_im",
         "ssm_d", "b_glu", "q_norm_g", "k_norm_g", "norm_mlp_g", "norm_final_g")


def _pad_rows(flat, mult_rows):
    n = flat.shape[0]
    unit = PACK_COLS * mult_rows
    total = -(-n // unit) * unit
    return jnp.pad(flat, (0, total - n)).reshape(total // PACK_COLS, PACK_COLS)


def _half(t, c):
    return lax.dynamic_slice_in_dim(t, c * (t.shape[0] // 2), t.shape[0] // 2, 0)


EARLY_WEIGHTS = ("meta_tokens", "w_in", "w_glu")
LATE_WEIGHTS = tuple(name for name, _ in BIG if name not in EARLY_WEIGHTS)


def _weight_blocks(shards, c, names):
    return [_half(shards[name], c) if name == "meta_tokens" else _half(shards[name], c).astype(BF16) for name in names]


def _shard_major(names, gathered):
    return {name: g.reshape((4, 2 * g.shape[1]) + g.shape[2:]) for name, g in zip(names, gathered)}


class _Reduction:
    def __init__(self, grads, wire, labels, c, tag):
        self.labels, self.wire, self.c, self.tag = labels, wire, c, tag
        self.core = c.astype(jnp.int32).reshape(1)
        g42 = [g.reshape(4, 2, g.shape[1] // 2, g.shape[2]) for g in grads]
        self.tiles = [_pick_tile(g.shape[2], 256, SUBLANES if dt == F32 else 2 * SUBLANES) for g, dt in zip(g42, wire)]
        got = _pair_exchange(g42, "grad_pair_exchange_" + tag)
        self.pair = [_pair_sum(g, o, self.core, dt, tm, "pair_sum_" + lb)
                     for g, o, dt, tm, lb in zip(g42, got, wire, self.tiles, labels)]

    def finish(self, by_src, gathered):
        red = [_sum4(b, self.core, tm, "chip_sum_" + lb) for b, tm, lb in zip(by_src, self.tiles, self.labels)]
        both = _pair_gather(red[:gathered], "grad_pair_gather_" + self.tag)
        pieces = [lax.dynamic_index_in_dim(r, self.c, 0, keepdims=False) for r in red[gathered:]]
        return [b.reshape(2 * b.shape[1], b.shape[2]) for b in both], pieces


def _small_as_shards(small_flat):
    unit = 8 * SUBLANES * PACK_COLS
    k = -(-small_flat.shape[0] // unit) * unit
    return jnp.pad(small_flat, (0, k - small_flat.shape[0])).reshape(4, k // (4 * PACK_COLS), PACK_COLS)


def _to_chunk_order(a):
    lp = a.shape[0]
    rest = a.shape[1:]
    a = a.reshape((lp // CHUNK, SUBLANES, KSTEPS) + rest)
    return a.swapaxes(1, 2).reshape((lp,) + rest)


def _from_chunk_order(a):
    lp = a.shape[0]
    rest = a.shape[1:]
    a = a.reshape((lp // CHUNK, KSTEPS, SUBLANES) + rest)
    return a.swapaxes(1, 2).reshape((lp,) + rest)


def _rope_tables(l_total, lp):
    n_real = l_total - N_META
    pos = np.arange(n_real)
    row_id = (pos // GRID_W).astype(np.float32)
    col_id = (pos % GRID_W).astype(np.float32)
    ppa = HEAD_DIM // 4
    inv_freq = (ROPE_THETA ** (-np.arange(ppa, dtype=np.float64) / ppa)).astype(np.float32)
    ang = np.concatenate([row_id[:, None] * inv_freq, col_id[:, None] * inv_freq], axis=-1)
    ang = np.concatenate([np.zeros((N_META, HEAD_DIM // 2), np.float32), ang,
                          np.zeros((lp - l_total, HEAD_DIM // 2), np.float32)], axis=0).astype(np.float64)
    cos = np.repeat(np.cos(ang), 2, axis=1)
    sin = np.repeat(np.sin(ang), 2, axis=1) * np.tile(np.asarray([-1.0, 1.0]), HEAD_DIM // 2)
    reps = (1, LANES // HEAD_DIM)
    return np.tile(cos, reps).astype(np.float32), np.tile(sin, reps).astype(np.float32)


def kernel(x, meta_tokens, norm_mix_g, w_in, ssm_a_re, ssm_a_im, ssm_log_dt, ssm_b_re, ssm_b_im, ssm_c_re, ssm_c_im, ssm_d, w_glu, b_glu, q_norm_g, k_norm_g, w_ssm_proj, w_attn_proj, w_out, norm_mlp_g, w_mlp_in, w_mlp_out, norm_final_g, loss_target, m_meta_tokens, m_norm_mix_g, m_w_in, m_ssm_a_re, m_ssm_a_im, m_ssm_log_dt, m_ssm_b_re, m_ssm_b_im, m_ssm_c_re, m_ssm_c_im, m_ssm_d, m_w_glu, m_b_glu, m_q_norm_g, m_k_norm_g, m_w_ssm_proj, m_w_attn_proj, m_w_out, m_norm_mlp_g, m_w_mlp_in, m_w_mlp_out, m_norm_final_g, v_meta_tokens, v_norm_mix_g, v_w_in, v_ssm_a_re, v_ssm_a_im, v_ssm_log_dt, v_ssm_b_re, v_ssm_b_im, v_ssm_c_re, v_ssm_c_im, v_ssm_d, v_w_glu, v_b_glu, v_q_norm_g, v_k_norm_g, v_w_ssm_proj, v_w_attn_proj, v_w_out, v_norm_mlp_g, v_w_mlp_in, v_w_mlp_out, v_norm_final_g):
    args = dict(locals())
    names = list(dict.fromkeys([n for n, _ in BIG] + list(SMALL)))
    order = ['meta_tokens', 'norm_mix_g', 'w_in', 'ssm_a_re', 'ssm_a_im', 'ssm_log_dt', 'ssm_b_re', 'ssm_b_im',
             'ssm_c_re', 'ssm_c_im', 'ssm_d', 'w_glu', 'b_glu', 'q_norm_g', 'k_norm_g', 'w_ssm_proj', 'w_attn_proj',
             'w_out', 'norm_mlp_g', 'w_mlp_in', 'w_mlp_out', 'norm_final_g']
    assert sorted(names) == sorted(order)
    c_idx = lax.axis_index("c")

    seq, d = x.shape[1], x.shape[2]
    l_total = seq + N_META
    lp = -(-l_total // SEQ_ALIGN) * SEQ_ALIGN
    hd = d // 2
    n_groups = hd // SSM_GROUP
    n_state = n_groups * SSM_STATE
    nj = n_state // SCAN_LANES
    kvh = d // HEAD_DIM // GQA_REP

    shard2d = {}
    for name, _ in BIG:
        t = args[name]
        shard2d[name] = t.reshape(t.shape[-2], t.shape[-1])
    full = _shard_major(EARLY_WEIGHTS, _all_gather8(_weight_blocks(shard2d, c_idx, EARLY_WEIGHTS), "weight_all_gather"))
    meta_full = jnp.transpose(full["meta_tokens"], (1, 0, 2)).reshape(N_META, d)
    w_in4 = full["w_in"]
    w_glu_f = full["w_glu"].reshape(hd, hd)

    xin = jnp.concatenate([meta_full, x[0], jnp.zeros((lp - l_total, d), F32)], axis=0)
    xin = _to_chunk_order(xin)
    tgt = _to_chunk_order(jnp.pad(loss_target[0], ((N_META, lp - l_total), (0, 0))))
    pos = np.arange(lp)
    rowmask = jnp.asarray(_to_chunk_order(((pos >= N_META) & (pos < l_total)).astype(np.float32)[:, None]))
    kbias = jnp.asarray(_to_chunk_order(np.where(pos < l_total, 0.0, MASK_VALUE).astype(np.float32)[:, None])
                        .reshape(1, lp))
    cos_t, sin_t = (jnp.asarray(_to_chunk_order(t)) for t in _rope_tables(l_total, lp))
    mean_m, sel = _head_tables(d)

    tm = _pick_tile(lp, 320)
    tm_mid = _pick_tile(lp, 384)
    tm_big = _pick_tile(lp, 640)
    tq = _pick_tile(lp, ATTN_Q_TILE, LANES)
    tk = _pick_tile(lp, ATTN_K_TILE, MXU_DIM)
    assert lp - CHUNK <= (l_total // CHUNK) * CHUNK and tk >= CHUNK
    g_mix = norm_mix_g.reshape(1, d)
    g_mlp = norm_mlp_g.reshape(1, d)
    g_fin = norm_final_g.reshape(1, d)
    qg = jnp.tile(q_norm_g.reshape(1, HEAD_DIM), (1, LANES // HEAD_DIM))
    kg = jnp.tile(k_norm_g.reshape(1, HEAD_DIM), (1, LANES // HEAD_DIM))
    dskip = ssm_d.reshape(1, hd)
    bglu = b_glu.reshape(1, hd)

    a_re, a_im = ssm_a_re[0], ssm_a_im[0]
    log_dt = ssm_log_dt[0][..., None]
    bt_re = jnp.swapaxes(ssm_b_re[0], 2, 3)
    bt_im = jnp.swapaxes(ssm_b_im[0], 2, 3)
    bb_re, bb_im, pw_re, pw_im, hi_re, hi_im = _ssm_discretize(a_re, a_im, log_dt, bt_re, bt_im)
    wb = [_embed_blocks(bb_re[i], bb_im[i]).astype(BF16) for i in range(2)]
    wct = [_embed_blocks(ssm_c_re[0, i], -ssm_c_im[0, i]).astype(BF16) for i in range(2)]
    tabs = [_scan_tables(pw_re[i], pw_im[i], hi_re[i], hi_im[i], rev=(i == 1)) for i in range(2)]
    tabs_adj = [_scan_tables(pw_re[i], pw_im[i], hi_re[i], hi_im[i], rev=(i == 0)) for i in range(2)]

    u, qkv, gates = _in_proj(xin, g_mix, w_in4, tm_mid)
    y0, ck0 = _ssm_fwd(u, wb[0], wct[0], tabs[0], False, "ssm_fwd_0")
    y1, ck1 = _ssm_fwd(u, wb[1], wct[1], tabs[1], True, "ssm_fwd_1")
    yssm = _glu_fwd(u, y0, y1, dskip, w_glu_f, bglu, tm_big)
    q, k, v = _qk_prep(qkv, cos_t, sin_t, qg, kg, mean_m, tm)
    o, lse, pt, mblk, late = _attn_fwd(q, k, v, kbias, tq, tk, gather=_weight_blocks(shard2d, c_idx, LATE_WEIGHTS))
    full = _shard_major(LATE_WEIGHTS, late)
    w_mlp_in4 = full["w_mlp_in"]
    w_ssm_proj4 = full["w_ssm_proj"]
    w_attn_proj_f = full["w_attn_proj"].reshape(d, d)
    w_out_f = full["w_out"].reshape(d, d)
    w_mlp_out_f = full["w_mlp_out"].reshape(4 * d, d)
    h1, merged = _merge_fwd(yssm, o, gates, xin, w_ssm_proj4, w_attn_proj_f, w_out_f, tm_mid)
    r = _mlp_in(h1, g_mlp, w_mlp_in4, tm_mid)
    h3 = _mlp_out(h1, r, w_mlp_out_f, tm_mid)
    loss_tile, dh3, d_gfin = _final_loss(h3, g_fin, tgt, rowmask, tm_big)

    dz, dh3b = _mlp_bwd_a(dh3, r, w_mlp_out_f, tm_mid)
    dh1, d_gmlp = _mlp_bwd_b(dz, dh3, h1, g_mlp, w_mlp_in4, tm_mid)
    dgates, dms, dma, dyssm, do, delta, dh1b = _merge_bwd(dh1, yssm, o, gates, w_ssm_proj4, w_attn_proj_f, w_out_f,
                                                          sel, tm)
    dyv, d_wglu, d_bglu, d_dskip = _glu_bwd(dyssm, u, y0, y1, dskip, w_glu_f, bglu, tm_big)

    tn = min(d, 1024)
    tm_w = _pick_tile(lp, 3 * MXU_DIM, MXU_DIM)
    grads4 = {
        "w_mlp_in": _wgrad(h1, dz, 4, tm_w, tn, "wgrad_mlp_in", gain=g_mlp),
        "w_mlp_out": _wgrad(r, dh3b, 1, tm_w, min(d, 512), "wgrad_mlp_out", square=True).reshape(4, d, d),
        "w_out": _wgrad(merged, dh1b, 1, tm_w, tn, "wgrad_out").reshape(4, d // 4, d),
        "w_attn_proj": _wgrad(o, dma, 1, tm_w, tn, "wgrad_attn_proj").reshape(4, d // 4, d),
        "w_ssm_proj": _wgrad(yssm, dms, 4, tm_w, d // 4, "wgrad_ssm_proj"),
        "w_glu": d_wglu.reshape(4, hd // 4, hd),
    }
    first_names = list(grads4)
    first = _Reduction([grads4[n] for n in first_names], [BF16] * len(first_names), first_names, c_idx, "first")

    du0, dbb0, dcc0, dlb0, first_by_src = _ssm_bwd(u, dyv, ck0, wb[0], wct[0], _both(tabs[0], tabs_adj[0]), False,
                                                   "ssm_bwd_0", scatter=first.pair)
    du1, dbb1, dcc1, dlb1, _ = _ssm_bwd(u, dyv, ck1, wb[1], wct[1], _both(tabs[1], tabs_adj[1]), True, "ssm_bwd_1")
    dq, dk, dv = _attn_bwd(q, k, v, pt, mblk, do, lse, delta, _pick_tile(lp, MXU_DIM, LANES), tk)
    dqkv, d_qg, d_kg = _qk_bwd(qkv, dq, dk, dv, cos_t, sin_t, qg, kg, mean_m, tm)
    dxin, d_gmix, dproj = _in_proj_bwd(dyv, du0, du1, dskip, dqkv, dgates, dh1, xin, g_mix, w_in4, tm)
    red_big = dict(zip(first_names, first.finish(first_by_src, len(first_names))[0]))

    grads4["w_in"] = _wgrad(xin, dproj, 4, tm_w, tn, "wgrad_in", gain=g_mix)
    dx_nat = _from_chunk_order(dxin)
    grads4["meta_tokens"] = jnp.swapaxes(dx_nat[:N_META].reshape(N_META, 4, d // 4), 0, 1)
    grad_x = dx_nat[N_META:l_total][None]

    dlb = jnp.stack([dlb0, dlb1])[:, :, 0, :]
    dlb_re = dlb[:, :nj].reshape(2, n_groups, SSM_STATE)
    dlb_im = dlb[:, nj:].reshape(2, n_groups, SSM_STATE)
    gpn = (2, 2, n_groups, SSM_GROUP, SSM_STATE)
    dbb = jnp.stack([dbb0, dbb1]).reshape(gpn)
    dcc = jnp.stack([dcc0, dcc1]).reshape(gpn)
    d_are, d_aim, d_logdt, d_btre, d_btim = _ssm_param_bwd(a_re, a_im, log_dt, bt_re, bt_im, dlb_re, dlb_im,
                                                           dbb[:, 0], dbb[:, 1])
    small_grads = {
        "norm_mix_g": d_gmix, "ssm_a_re": d_are, "ssm_a_im": d_aim, "ssm_log_dt": d_logdt,
        "ssm_b_re": jnp.swapaxes(d_btre, 2, 3), "ssm_b_im": jnp.swapaxes(d_btim, 2, 3),
        "ssm_c_re": dcc[:, 0], "ssm_c_im": -dcc[:, 1],
        "ssm_d": d_dskip, "b_glu": d_bglu, "q_norm_g": d_qg[:, :HEAD_DIM], "k_norm_g": d_kg[:, :HEAD_DIM],
        "norm_mlp_g": d_gmlp, "norm_final_g": d_gfin,
    }
    small_flat = jnp.concatenate([small_grads[n].reshape(-1) for n in SMALL] + [loss_tile[0, :1]])

    last = _Reduction([grads4["meta_tokens"], grads4["w_in"], _small_as_shards(small_flat)], [F32, BF16, F32],
                      ["meta_tokens", "w_in", "small"], c_idx, "last")
    (red_big["meta_tokens"], red_big["w_in"]), (small_piece,) = last.finish(
        _chip_scatter(last.pair, "grad_chip_scatter"), 2)
    red_small = _all_gather8([small_piece], "small_grad_all_gather")[0].reshape(-1)[:small_flat.shape[0]]
    loss, red_small = red_small[-1], red_small[:-1]
    grad, delta_w, new_m, new_v = {}, {}, {}, {}
    for name, _ in BIG:
        w2 = shard2d[name]
        shp = args[name].shape
        g2 = red_big[name]
        t = _pick_tile(w2.shape[0], 256, 8)
        dl, nm, nv = _adamw(w2, g2, args["m_" + name].reshape(w2.shape), args["v_" + name].reshape(w2.shape), t)
        grad[name], delta_w[name], new_m[name], new_v[name] = (a.reshape(shp) for a in (g2, dl, nm, nv))

    def pack_small(prefix):
        flat = jnp.concatenate([args[prefix + n].reshape(-1) for n in SMALL])
        return _pad_rows(flat, SUBLANES)

    n_small = red_small.shape[0]
    gs = _pad_rows(red_small, SUBLANES)
    dl, nm, nv = _adamw(pack_small(""), gs, pack_small("m_"), pack_small("v_"), _pick_tile(gs.shape[0], 256, 8))
    off = 0
    for name in SMALL:
        shp = args[name].shape
        k = int(np.prod(shp))
        for dst, src in ((grad, gs), (delta_w, dl), (new_m, nm), (new_v, nv)):
            dst[name] = src.reshape(-1)[off:off + k].reshape(shp)
        off += k
    assert off == n_small

    return (loss, grad_x, *[grad[n] for n in order], *[delta_w[n] for n in order],
            *[new_m[n] for n in order], *[new_v[n] for n in order])


def _both(tab, tab_adj):
    return jnp.concatenate([tab, tab_adj], axis=0)
```

```python
import functools
import math

import numpy as np
import jax
import jax.numpy as jnp
from jax import lax
from jax.experimental import pallas as pl
from jax.experimental.pallas import tpu as pltpu

F32 = jnp.float32
BF16 = jnp.bfloat16

N_META = 16
GRID_W = 64
HEAD_DIM = 64
GQA_REP = 4
SSM_GROUP = 16
SSM_STATE = 64
ROPE_THETA = 10000.0
NORM_EPS = 1e-6
EIG_RE_MAX = -1e-4
ADAM_LR, ADAM_B1, ADAM_B2, ADAM_EPS, ADAM_WD, ADAM_STEP = 0.001, 0.9, 0.999, 1e-08, 0.01, 10

SUBLANES = 8
LANES = 128
CHUNK = 256
KSTEPS = CHUNK // SUBLANES
SCAN_LANES = 512
MXU_DIM = 256
SSM_BLOCK = MXU_DIM
SEQ_ALIGN = MXU_DIM
ATTN_Q_TILE = 384
ATTN_K_TILE = 11 * MXU_DIM
VMEM_LIMIT = 56 << 20
MASK_VALUE = -1e30
MESH_ID = pl.DeviceIdType.MESH


def _dot(a, b):
    return jnp.dot(a, b, preferred_element_type=F32)


def _dot_nt(a, b):
    return lax.dot_general(a, b, (((1,), (1,)), ((), ())), preferred_element_type=F32)


def _dot_tn(a, b):
    return lax.dot_general(a, b, (((0,), (0,)), ((), ())), preferred_element_type=F32)


def _row(tm, width):
    return pl.BlockSpec((tm, width), lambda i: (i, 0))


def _full(shape):
    nd = len(shape)
    return pl.BlockSpec(shape, lambda i: (0,) * nd)


def _params(sem):
    return pltpu.CompilerParams(dimension_semantics=sem, vmem_limit_bytes=VMEM_LIMIT)


def _pick_tile(n, cap, mult=16):
    best = None
    for t in range(mult, min(n, cap) + 1, mult):
        if n % t == 0:
            best = t
    assert best is not None, (n, cap)
    return best


def _rstd(x):
    return lax.rsqrt(jnp.mean(x * x, axis=-1, keepdims=True) + NORM_EPS)


def _rms(x, g):
    return x * _rstd(x) * g


def _rms_bwd(dy, x, g):
    r = _rstd(x)
    xh = x * r
    gdy = dy * g
    dx = r * (gdy - xh * jnp.mean(gdy * xh, axis=-1, keepdims=True))
    return dx, dy * xh


def _split_dot(x, m):
    hi = x.astype(BF16)
    lo = (x - hi.astype(F32)).astype(BF16)
    return _dot(hi, m) + _dot(lo, m)


def _sigmoid(x):
    return 1.0 / (1.0 + jnp.exp(-x))


def _acc_rows(ref, val, first):
    s = jnp.sum(val, axis=0, keepdims=True)

    @pl.when(first)
    def _():
        ref[...] = s

    @pl.when(jnp.logical_not(first))
    def _():
        ref[...] += s


def _in_proj(xin, g, w4, tm):
    lp, d = xin.shape
    hd = d // 2

    def body(x_ref, g_ref, w_ref, u_ref, qkv_ref, gt_ref, h_ref):
        h = _rms(x_ref[...], g_ref[...]).astype(BF16)
        h_ref[...] = h
        p0 = _dot(h, w_ref[0])
        u_ref[...] = p0[:, :hd]
        qkv_ref[:, :hd] = p0[:, hd:]
        qkv_ref[:, hd:] = _dot(h, w_ref[1])
        gt_ref[:, :d] = _dot(h, w_ref[2])
        gt_ref[:, d:] = _dot(h, w_ref[3])

    return pl.pallas_call(
        body, name="in_proj", grid=(lp // tm,),
        in_specs=[_row(tm, d), _full((1, d)), _full((4, d, d))],
        out_specs=[_row(tm, hd), _row(tm, 3 * hd), _row(tm, 2 * d), _row(tm, d)],
        out_shape=[jax.ShapeDtypeStruct((lp, hd), F32), jax.ShapeDtypeStruct((lp, 3 * hd), F32),
                   jax.ShapeDtypeStruct((lp, 2 * d), F32), jax.ShapeDtypeStruct((lp, d), BF16)],
        compiler_params=_params(("parallel",)),
    )(xin, g, w4)


def _gelu(y):
    return 0.5 * y * (1.0 + lax.erf(y * (1.0 / math.sqrt(2.0))))


def _gelu_grad(y):
    return 0.5 * (1.0 + lax.erf(y * (1.0 / math.sqrt(2.0)))) + y * jnp.exp(-0.5 * y * y) * (1.0 / math.sqrt(2.0 * math.pi))


def _glu_fwd(u, y0, y1, dskip, w_glu, b_glu, tm):
    lp, w = u.shape

    def body(u_ref, y0_ref, y1_ref, d_ref, w_ref, b_ref, o_ref):
        y = u_ref[...] * d_ref[...] + y0_ref[...] + y1_ref[...]
        z = _gelu(y)
        t = _dot(z.astype(BF16), w_ref[...]) + b_ref[...]
        o_ref[...] = (z * _sigmoid(t)).astype(BF16)

    return pl.pallas_call(
        body, name="glu_fwd", grid=(lp // tm,),
        in_specs=[_row(tm, w), _row(tm, w), _row(tm, w), _full((1, w)), _full((w, w)), _full((1, w))],
        out_specs=_row(tm, w), out_shape=jax.ShapeDtypeStruct((lp, w), BF16),
        compiler_params=_params(("parallel",)),
    )(u, y0, y1, dskip, w_glu, b_glu)


def _glu_bwd(dyssm, u, y0, y1, dskip, w_glu, b_glu, tm):
    lp, w = u.shape

    def body(g_ref, u_ref, y0_ref, y1_ref, d_ref, w_ref, b_ref, dy_ref, dw_ref, db_ref, dd_ref):
        first = pl.program_id(0) == 0
        uu = u_ref[...]
        y = uu * d_ref[...] + y0_ref[...] + y1_ref[...]
        z = _gelu(y)
        zb = z.astype(BF16)
        sg = _sigmoid(_dot(zb, w_ref[...]) + b_ref[...])
        g = g_ref[...]
        dt = g * z * sg * (1.0 - sg)
        dtb = dt.astype(BF16)
        dz = g * sg + _dot_nt(dtb, w_ref[...])
        dy = dz * _gelu_grad(y)
        dy_ref[...] = dy
        dw = _dot_tn(zb, dtb)

        @pl.when(first)
        def _():
            dw_ref[...] = dw

        @pl.when(jnp.logical_not(first))
        def _():
            dw_ref[...] += dw

        _acc_rows(db_ref, dt, first)
        _acc_rows(dd_ref, dy * uu, first)

    return pl.pallas_call(
        body, name="glu_bwd", grid=(lp // tm,),
        in_specs=[_row(tm, w), _row(tm, w), _row(tm, w), _row(tm, w), _full((1, w)), _full((w, w)), _full((1, w))],
        out_specs=[_row(tm, w), _full((w, w)), _full((1, w)), _full((1, w))],
        out_shape=[jax.ShapeDtypeStruct((lp, w), F32), jax.ShapeDtypeStruct((w, w), F32),
                   jax.ShapeDtypeStruct((1, w), F32), jax.ShapeDtypeStruct((1, w), F32)],
        compiler_params=_params(("arbitrary",)),
    )(dyssm, u, y0, y1, dskip, w_glu, b_glu)


def _merge_fwd(yssm, o, gates, xin, wsp4, wap, wo, tm):
    lp, d = xin.shape
    w = yssm.shape[1]
    ns = d // 4

    def body(y_ref, o_ref, g_ref, x_ref, wsp_ref, wap_ref, wo_ref, h_ref, m_ref):
        yb = y_ref[...]
        ms = jnp.concatenate([_dot(yb, wsp_ref[s]) for s in range(4)], axis=1)
        ma = _dot(o_ref[...], wap_ref[...])
        merged = (_sigmoid(g_ref[:, :d]) * ms + _sigmoid(g_ref[:, d:]) * ma).astype(BF16)
        m_ref[...] = merged
        h_ref[...] = x_ref[...] + _dot(merged, wo_ref[...])

    return pl.pallas_call(
        body, name="merge_fwd", grid=(lp // tm,),
        in_specs=[_row(tm, w), _row(tm, d), _row(tm, 2 * d), _row(tm, d),
                  _full((4, w, ns)), _full((d, d)), _full((d, d))],
        out_specs=[_row(tm, d), _row(tm, d)],
        out_shape=[jax.ShapeDtypeStruct((lp, d), F32), jax.ShapeDtypeStruct((lp, d), BF16)],
        compiler_params=_params(("parallel",)),
    )(yssm, o, gates, xin, wsp4, wap, wo)


def _merge_bwd(dh1, yssm, o, gates, wsp4, wap, wo, sel, tm):
    lp, d = dh1.shape
    w = yssm.shape[1]
    ns = d // 4
    nsel = sel.shape[1]

    def body(dh_ref, y_ref, o_ref, g_ref, wsp_ref, wap_ref, wo_ref, sel_ref,
             dg_ref, dms_ref, dma_ref, dy_ref, do_ref, dl_ref, dhb_ref):
        dhb = dh_ref[...].astype(BF16)
        dhb_ref[...] = dhb
        dm = _dot_nt(dhb, wo_ref[...])
        yb = y_ref[...]
        ob = o_ref[...]
        ms = jnp.concatenate([_dot(yb, wsp_ref[s]) for s in range(4)], axis=1)
        ma = _dot(ob, wap_ref[...])
        ss = _sigmoid(g_ref[:, :d])
        sa = _sigmoid(g_ref[:, d:])
        dg_ref[:, :d] = dm * ms * ss * (1.0 - ss)
        dg_ref[:, d:] = dm * ma * sa * (1.0 - sa)
        dms = (dm * ss).astype(BF16)
        dma = (dm * sa).astype(BF16)
        dms_ref[...] = dms
        dma_ref[...] = dma
        dy = _dot_nt(dms[:, :ns], wsp_ref[0])
        for s in range(1, 4):
            dy += _dot_nt(dms[:, s * ns:(s + 1) * ns], wsp_ref[s])
        dy_ref[...] = dy
        do = _dot_nt(dma, wap_ref[...])
        do_ref[...] = do.astype(BF16)
        dl_ref[...] = _split_dot(do * ob.astype(F32), sel_ref[...])

    return pl.pallas_call(
        body, name="merge_bwd", grid=(lp // tm,),
        in_specs=[_row(tm, d), _row(tm, w), _row(tm, d), _row(tm, 2 * d),
                  _full((4, w, ns)), _full((d, d)), _full((d, d)), _full((d, nsel))],
        out_specs=[_row(tm, 2 * d), _row(tm, d), _row(tm, d), _row(tm, w), _row(tm, d), _row(tm, nsel), _row(tm, d)],
        out_shape=[jax.ShapeDtypeStruct((lp, 2 * d), F32), jax.ShapeDtypeStruct((lp, d), BF16),
                   jax.ShapeDtypeStruct((lp, d), BF16), jax.ShapeDtypeStruct((lp, w), F32),
                   jax.ShapeDtypeStruct((lp, d), BF16), jax.ShapeDtypeStruct((lp, nsel), F32),
                   jax.ShapeDtypeStruct((lp, d), BF16)],
        compiler_params=_params(("parallel",)),
    )(dh1, yssm, o, gates, wsp4, wap, wo, sel)


def _mlp_in(h1, g, w4, tm):
    lp, d = h1.shape

    def body(x_ref, g_ref, w_ref, r_ref, h_ref):
        h = _rms(x_ref[...], g_ref[...]).astype(BF16)
        h_ref[...] = h
        for s in range(4):
            r_ref[:, s * d:(s + 1) * d] = jnp.maximum(_dot(h, w_ref[s]), 0.0).astype(BF16)

    return pl.pallas_call(
        body, name="mlp_in", grid=(lp // tm,),
        in_specs=[_row(tm, d), _full((1, d)), _full((4, d, d))],
        out_specs=[_row(tm, 4 * d), _row(tm, d)],
        out_shape=[jax.ShapeDtypeStruct((lp, 4 * d), BF16), jax.ShapeDtypeStruct((lp, d), BF16)],
        compiler_params=_params(("parallel",)),
    )(h1, g, w4)


def _square_bf16(r):
    rf = r.astype(F32)
    return (rf * rf).astype(BF16)


def _mlp_out(h1, r, w2, tm):
    lp, d = h1.shape
    ff = r.shape[1]

    def body(x_ref, r_ref, w_ref, o_ref):
        o_ref[...] = x_ref[...] + _dot(_square_bf16(r_ref[...]), w_ref[...])

    return pl.pallas_call(
        body, name="mlp_out", grid=(lp // tm,),
        in_specs=[_row(tm, d), _row(tm, ff), _full((ff, d))],
        out_specs=_row(tm, d), out_shape=jax.ShapeDtypeStruct((lp, d), F32),
        compiler_params=_params(("parallel",)),
    )(h1, r, w2)


def _final_loss(h3, g, tgt, rowmask, tm):
    lp, d = h3.shape

    def body(x_ref, g_ref, t_ref, m_ref, loss_ref, dx_ref, dg_ref):
        first = pl.program_id(0) == 0
        x = x_ref[...]
        gg = g_ref[...]
        err = (_rms(x, gg) - t_ref[...]) * m_ref[...]
        part = 0.5 * jnp.sum(jnp.sum(err * err, axis=1, keepdims=True), axis=0, keepdims=True) * (1.0 / d)
        part = jnp.broadcast_to(part, (SUBLANES, LANES))

        @pl.when(first)
        def _():
            loss_ref[...] = part

        @pl.when(jnp.logical_not(first))
        def _():
            loss_ref[...] += part

        dx, dgr = _rms_bwd(err * (1.0 / d), x, gg)
        dx_ref[...] = dx
        _acc_rows(dg_ref, dgr, first)

    return pl.pallas_call(
        body, name="final_loss", grid=(lp // tm,),
        in_specs=[_row(tm, d), _full((1, d)), _row(tm, d), _row(tm, 1)],
        out_specs=[_full((SUBLANES, LANES)), _row(tm, d), _full((1, d))],
        out_shape=[jax.ShapeDtypeStruct((SUBLANES, LANES), F32), jax.ShapeDtypeStruct((lp, d), F32),
                   jax.ShapeDtypeStruct((1, d), F32)],
        compiler_params=_params(("arbitrary",)),
    )(h3, g, tgt, rowmask)


def _mlp_bwd_a(dh3, r, w2, tm):
    lp, d = dh3.shape
    ff = r.shape[1]

    def body(dh_ref, r_ref, w_ref, dz_ref, dhb_ref):
        dhb = dh_ref[...].astype(BF16)
        dhb_ref[...] = dhb
        da = _dot_nt(dhb, w_ref[...])
        dz_ref[...] = (da * (2.0 * r_ref[...].astype(F32))).astype(BF16)

    return pl.pallas_call(
        body, name="mlp_bwd_a", grid=(lp // tm,),
        in_specs=[_row(tm, d), _row(tm, ff), _full((ff, d))],
        out_specs=[_row(tm, ff), _row(tm, d)],
        out_shape=[jax.ShapeDtypeStruct((lp, ff), BF16), jax.ShapeDtypeStruct((lp, d), BF16)],
        compiler_params=_params(("parallel",)),
    )(dh3, r, w2)


def _mlp_bwd_b(dz, dh3, h1, g, w4, tm):
    lp, d = h1.shape

    def body(dz_ref, dh_ref, x_ref, g_ref, w_ref, dx_ref, dg_ref):
        first = pl.program_id(0) == 0
        dh2 = _dot_nt(dz_ref[:, :d], w_ref[0])
        for s in range(1, 4):
            dh2 += _dot_nt(dz_ref[:, s * d:(s + 1) * d], w_ref[s])
        dx, dgr = _rms_bwd(dh2, x_ref[...], g_ref[...])
        dx_ref[...] = dh_ref[...] + dx
        _acc_rows(dg_ref, dgr, first)

    return pl.pallas_call(
        body, name="mlp_bwd_b", grid=(lp // tm,),
        in_specs=[_row(tm, 4 * d), _row(tm, d), _row(tm, d), _full((1, d)), _full((4, d, d))],
        out_specs=[_row(tm, d), _full((1, d))],
        out_shape=[jax.ShapeDtypeStruct((lp, d), F32), jax.ShapeDtypeStruct((1, d), F32)],
        compiler_params=_params(("arbitrary",)),
    )(dz, dh3, h1, g, w4)


def _in_proj_bwd(dyv, du0, du1, dskip, dqkv, dgates, dres, xin, g, w4, tm):
    lp, d = xin.shape
    hd = d // 2

    def body(dy_ref, a_ref, b_ref, ds_ref, dq_ref, dgt_ref, dr_ref, x_ref, g_ref, w_ref, dx_ref, dg_ref, dp_ref):
        first = pl.program_id(0) == 0
        du = (dy_ref[...] * ds_ref[...] + a_ref[...] + b_ref[...]).astype(BF16)
        dq = dq_ref[...].astype(BF16)
        dgt = dgt_ref[...].astype(BF16)
        dp_ref[:, :hd] = du
        dp_ref[:, hd:2 * d] = dq
        dp_ref[:, 2 * d:] = dgt
        dh = _dot_nt(du, w_ref[0, :, :hd]) + _dot_nt(dq[:, :hd], w_ref[0, :, hd:])
        dh += _dot_nt(dq[:, hd:], w_ref[1])
        dh += _dot_nt(dgt[:, :d], w_ref[2]) + _dot_nt(dgt[:, d:], w_ref[3])
        dx, dgr = _rms_bwd(dh, x_ref[...], g_ref[...])
        dx_ref[...] = dr_ref[...] + dx
        _acc_rows(dg_ref, dgr, first)

    return pl.pallas_call(
        body, name="in_proj_bwd", grid=(lp // tm,),
        in_specs=[_row(tm, hd), _row(tm, hd), _row(tm, hd), _full((1, hd)), _row(tm, 3 * hd), _row(tm, 2 * d),
                  _row(tm, d), _row(tm, d), _full((1, d)), _full((4, d, d))],
        out_specs=[_row(tm, d), _full((1, d)), _row(tm, 4 * d)],
        out_shape=[jax.ShapeDtypeStruct((lp, d), F32), jax.ShapeDtypeStruct((1, d), F32),
                   jax.ShapeDtypeStruct((lp, 4 * d), BF16)],
        compiler_params=_params(("arbitrary",)),
    )(dyv, du0, du1, dskip, dqkv, dgates, dres, xin, g, w4)


def _wgrad(a, dy, nshard, tm, tn, name, square=False):
    lp, k = a.shape
    n = dy.shape[1]
    ns = n // nshard
    assert ns % tn == 0
    per = ns // tn

    def body(a_ref, dy_ref, o_ref):
        i = pl.program_id(1)
        acc = _dot_tn(_square_bf16(a_ref[...]) if square else a_ref[...], dy_ref[...])

        @pl.when(i == 0)
        def _():
            o_ref[0] = acc

        @pl.when(i != 0)
        def _():
            o_ref[0] += acc

    return pl.pallas_call(
        body, name=name, grid=(n // tn, lp // tm),
        in_specs=[pl.BlockSpec((tm, k), lambda j, i: (i, 0)), pl.BlockSpec((tm, tn), lambda j, i: (i, j))],
        out_specs=pl.BlockSpec((1, k, tn), lambda j, i: (j // per, 0, j % per)),
        out_shape=jax.ShapeDtypeStruct((nshard, k, ns), F32),
        compiler_params=_params(("parallel", "arbitrary")),
    )(a, dy)


def _head_tables(d):
    idx = np.arange(LANES)
    mean = (idx[:, None] // HEAD_DIM == idx[None, :] // HEAD_DIM).astype(np.float32) / HEAD_DIM
    n_heads = d // HEAD_DIM
    kvh = n_heads // GQA_REP
    c = np.arange(d)
    col = np.arange(kvh * LANES)
    head_of_col = (col // LANES) * GQA_REP + (col % LANES)
    sel = ((c[:, None] // HEAD_DIM == head_of_col[None, :]) & ((col % LANES) < GQA_REP)[None, :]).astype(np.float32)
    return jnp.asarray(mean, BF16), jnp.asarray(sel, BF16)


def _swap_pairs(y):
    lane = lax.broadcasted_iota(jnp.int32, y.shape, 1)
    return jnp.where(lane % 2 == 0, pltpu.roll(y, LANES - 1, 1), pltpu.roll(y, 1, 1))


def _qk_prep(qkv, cos_t, sin_t, qg, kg, mean_m, tm):
    lp, wq = qkv.shape
    d = wq * 2 // 3
    kvw = d // 4
    kvh = kvw // HEAD_DIM
    scale = HEAD_DIM ** -0.5

    def body(x_ref, c_ref, s_ref, qg_ref, kg_ref, m_ref, q_ref, k_ref, v_ref):
        cs, sn, mm = c_ref[...], s_ref[...], m_ref[...]
        for b in range((d + kvw) // LANES):
            x = x_ref[:, b * LANES:(b + 1) * LANES]
            gg = qg_ref[...] if b < d // LANES else kg_ref[...]
            y = x * lax.rsqrt(_split_dot(x * x, mm) + NORM_EPS) * gg
            out = y * cs + _swap_pairs(y) * sn
            if b < d // LANES:
                q_ref[:, b * LANES:(b + 1) * LANES] = (out * scale).astype(BF16)
            else:
                kb = b - d // LANES
                k_ref[2 * kb] = out[:, :HEAD_DIM].astype(BF16)
                k_ref[2 * kb + 1] = out[:, HEAD_DIM:].astype(BF16)
        ones = jnp.ones((tm, LANES - HEAD_DIM), BF16)
        for h in range(kvh):
            vh = x_ref[:, d + kvw + h * HEAD_DIM:d + kvw + (h + 1) * HEAD_DIM].astype(BF16)
            v_ref[h] = jnp.concatenate([vh, ones], axis=1)

    k_spec = pl.BlockSpec((kvh, tm, HEAD_DIM), lambda i: (0, i, 0))
    v_spec = pl.BlockSpec((kvh, tm, LANES), lambda i: (0, i, 0))
    return pl.pallas_call(
        body, name="qk_prep", grid=(lp // tm,),
        in_specs=[_row(tm, wq), _row(tm, LANES), _row(tm, LANES), _full((1, LANES)), _full((1, LANES)),
                  _full((LANES, LANES))],
        out_specs=[_row(tm, d), k_spec, v_spec],
        out_shape=[jax.ShapeDtypeStruct((lp, d), BF16), jax.ShapeDtypeStruct((kvh, lp, HEAD_DIM), BF16),
                   jax.ShapeDtypeStruct((kvh, lp, LANES), BF16)],
        compiler_params=_params(("parallel",)),
    )(qkv, cos_t, sin_t, qg, kg, mean_m)


def _qk_bwd(qkv, dq, dk, dv, cos_t, sin_t, qg, kg, mean_m, tm):
    lp, wq = qkv.shape
    d = wq * 2 // 3
    kvw = d // 4
    kvh = kvw // HEAD_DIM
    scale = HEAD_DIM ** -0.5

    def body(x_ref, dq_ref, dk_ref, dv_ref, c_ref, s_ref, qg_ref, kg_ref, m_ref, o_ref, dqg_ref, dkg_ref):
        first = pl.program_id(0) == 0
        cs, sn, mm = c_ref[...], s_ref[...], m_ref[...]
        sums = [None, None]
        for b in range((d + kvw) // LANES):
            is_q = b < d // LANES
            x = x_ref[:, b * LANES:(b + 1) * LANES]
            gg = qg_ref[...] if is_q else kg_ref[...]
            r = lax.rsqrt(_split_dot(x * x, mm) + NORM_EPS)
            nrm = x * r
            if is_q:
                dout = dq_ref[:, b * LANES:(b + 1) * LANES] * scale
            else:
                kb = b - d // LANES
                dout = jnp.concatenate([dk_ref[2 * kb], dk_ref[2 * kb + 1]], axis=1)
            dy = dout * cs + _swap_pairs(dout * sn)
            part = jnp.sum(dy * nrm, axis=0, keepdims=True)
            sums[0 if is_q else 1] = part if sums[0 if is_q else 1] is None else sums[0 if is_q else 1] + part
            dn = dy * gg
            o_ref[:, b * LANES:(b + 1) * LANES] = r * (dn - nrm * _split_dot(dn * nrm, mm))
        for h in range(kvh):
            o_ref[:, d + kvw + h * HEAD_DIM:d + kvw + (h + 1) * HEAD_DIM] = dv_ref[h]
        for ref, s in ((dqg_ref, sums[0]), (dkg_ref, sums[1])):
            s = s + pltpu.roll(s, HEAD_DIM, 1)

            @pl.when(first)
            def _(ref=ref, s=s):
                ref[...] = s

            @pl.when(jnp.logical_not(first))
            def _(ref=ref, s=s):
                ref[...] += s

    kv_spec = pl.BlockSpec((kvh, tm, HEAD_DIM), lambda i: (0, i, 0))
    return pl.pallas_call(
        body, name="qk_bwd", grid=(lp // tm,),
        in_specs=[_row(tm, wq), _row(tm, d), kv_spec, kv_spec, _row(tm, LANES), _row(tm, LANES),
                  _full((1, LANES)), _full((1, LANES)), _full((LANES, LANES))],
        out_specs=[_row(tm, wq), _full((1, LANES)), _full((1, LANES))],
        out_shape=[jax.ShapeDtypeStruct((lp, wq), F32), jax.ShapeDtypeStruct((1, LANES), F32),
                   jax.ShapeDtypeStruct((1, LANES), F32)],
        compiler_params=_params(("arbitrary",)),
    )(qkv, dq, dk, dv, cos_t, sin_t, qg, kg, mean_m)


def _attn_fwd(q, k, v, kbias, tq, tk, gather=()):
    lp, d = q.shape
    kvh = k.shape[0]
    rw = GQA_REP * HEAD_DIM
    nk = lp // tk

    ng = len(gather)
    steps = kvh * (lp // tq) * nk

    def body(*refs):
        q_ref, k_ref, v_ref, kb_ref = refs[:4]
        o_ref, lse_ref, pt_ref, mb_ref = refs[4 + ng:8 + ng]
        m_s, acc_s = refs[8 + 2 * ng:10 + 2 * ng]
        j = pl.program_id(2)

        if ng:
            phases = _gather_phases(refs[4:4 + ng], refs[8 + ng:8 + 2 * ng], *refs[10 + 2 * ng:])
            step = (pl.program_id(0) * (lp // tq) + pl.program_id(1)) * nk + j
            for n, phase in enumerate(phases):
                pl.when(step == n * steps // 3)(phase)

        @pl.when(j == 0)
        def _():
            m_s[...] = jnp.full(m_s.shape, MASK_VALUE, F32)
            acc_s[...] = jnp.zeros(acc_s.shape, F32)

        def heads(masked):
            kk, vv = k_ref[0], v_ref[0]

            def scores(h):
                return _dot_nt(q_ref[:, h * HEAD_DIM:(h + 1) * HEAD_DIM], kk)

            def softmax(h, s):
                if masked:
                    s = jnp.concatenate([s[:, :tk - CHUNK], s[:, tk - CHUNK:] + kb_ref[:, tk - CHUNK:]], axis=1)
                m_prev = m_s[h]
                m_new = jnp.maximum(m_prev, jnp.max(s, axis=1, keepdims=True))
                m_s[h] = m_new
                p = jnp.exp(s - m_new[:, :1]).astype(BF16)
                pt_ref[h] = p
                return p, jnp.exp(m_prev - m_new), m_new

            def accumulate(h, p, alpha):
                acc_s[h] = acc_s[h] * alpha + _dot(p, vv)

            ss = [scores(h) for h in range(GQA_REP)]
            pa = [softmax(h, ss[h]) for h in range(GQA_REP)]
            for h in range(GQA_REP):
                accumulate(h, *pa[h][:2])
            lane = lax.broadcasted_iota(jnp.int32, (tq, LANES), 1)
            mb = jnp.zeros((tq, LANES), F32)
            for h in range(GQA_REP):
                mb = jnp.where(lane == h, pa[h][2], mb)
            mb_ref[0] = mb

        pl.when(j != nk - 1)(functools.partial(heads, False))
        pl.when(j == nk - 1)(functools.partial(heads, True))

        @pl.when(j == nk - 1)
        def _():
            lane = lax.broadcasted_iota(jnp.int32, (tq, LANES), 1)
            lse = jnp.zeros((tq, LANES), F32)
            outs = []
            for h in range(GQA_REP):
                acc = acc_s[h]
                den = pltpu.roll(acc, HEAD_DIM, 1)
                outs.append((acc / den)[:, :HEAD_DIM])
                lse = jnp.where(lane == h, m_s[h] + jnp.log(den), lse)
            o_ref[...] = jnp.concatenate(outs, axis=1).astype(BF16)
            lse_ref[...] = lse

    sems = [pltpu.SemaphoreType.DMA((ng, 7)), pltpu.SemaphoreType.DMA((ng, 7)), pltpu.SemaphoreType.DMA((ng,))]
    res = pl.pallas_call(
        body, name="attn_fwd", grid=(kvh, lp // tq, nk),
        in_specs=[pl.BlockSpec((tq, rw), lambda g, i, j: (i, g)),
                  pl.BlockSpec((1, tk, HEAD_DIM), lambda g, i, j: (g, j, 0)),
                  pl.BlockSpec((1, tk, LANES), lambda g, i, j: (g, j, 0)),
                  pl.BlockSpec((1, tk), lambda g, i, j: (0, j))] + [_ANY] * ng,
        out_specs=[pl.BlockSpec((tq, rw), lambda g, i, j: (i, g)),
                   pl.BlockSpec((tq, LANES), lambda g, i, j: (i, g)),
                   pl.BlockSpec((GQA_REP, tq, tk), lambda g, i, j: (g, i, j)),
                   pl.BlockSpec((1, tq, LANES), lambda g, i, j: (j, i, g))] + [_ANY] * ng,
        out_shape=[jax.ShapeDtypeStruct((lp, d), BF16), jax.ShapeDtypeStruct((lp, kvh * LANES), F32),
                   jax.ShapeDtypeStruct((kvh * GQA_REP, lp, lp), BF16), jax.ShapeDtypeStruct((nk, lp, kvh * LANES), F32)]
        + [jax.ShapeDtypeStruct((8,) + b.shape, b.dtype) for b in gather],
        scratch_shapes=[pltpu.VMEM((GQA_REP, tq, LANES), F32), pltpu.VMEM((GQA_REP, tq, LANES), F32)]
        + (sems if ng else []),
        compiler_params=_params(("arbitrary", "arbitrary", "arbitrary")),
    )(q, k, v, kbias, *gather)
    return res[0], res[1], res[2], res[3], list(res[4:])


def _attn_bwd(q, k, v, pt, mblk, do, lse, delta, tq, tk):
    lp, d = q.shape
    kvh = k.shape[0]
    rw = GQA_REP * HEAD_DIM
    nq = lp // tq

    def body(q_ref, k_ref, v_ref, pt_ref, mb_ref, do_ref, lse_ref, dl_ref, dq_ref, dk_ref, dv_ref, dk_s, dv_s):
        j = pl.program_id(1)
        i = pl.program_id(2)

        @pl.when(jnp.logical_and(i == 0, j == 0))
        def _():
            dq_ref[...] = jnp.zeros(dq_ref.shape, F32)

        @pl.when(i == 0)
        def _():
            dk_s[...] = jnp.zeros(dk_s.shape, F32)
            dv_s[...] = jnp.zeros(dv_s.shape, F32)

        kk, vv = k_ref[0], v_ref[0][:, :HEAD_DIM]
        scale = jnp.exp(mb_ref[0] - lse_ref[...])
        dl = dl_ref[...] * scale
        dqs = []
        for pair in ((0, 1), (2, 3)):
            dos = {h: (do_ref[:, h * HEAD_DIM:(h + 1) * HEAD_DIM].astype(F32) * scale[:, h:h + 1]).astype(BF16)
                   for h in pair}
            dps = {h: _dot_nt(dos[h], vv) for h in pair}
            for h in pair:
                dv_s[...] += _dot_tn(pt_ref[h], dos[h])
            dss = {h: pt_ref[h] * (dps[h] - dl[:, h:h + 1]).astype(BF16) for h in pair}
            for h in pair:
                dk_s[...] += _dot_tn(dss[h], q_ref[:, h * HEAD_DIM:(h + 1) * HEAD_DIM])
                dqs.append(_dot(dss[h], kk))
        rows = pl.ds(pl.multiple_of(i * tq, tq), tq)
        dq_ref[rows, :] += jnp.concatenate(dqs, axis=1)

        @pl.when(i == nq - 1)
        def _():
            dk_ref[0] = dk_s[...]
            dv_ref[0] = dv_s[...]

    return pl.pallas_call(
        body, name="attn_bwd", grid=(kvh, lp // tk, nq),
        in_specs=[pl.BlockSpec((tq, rw), lambda g, j, i: (i, g)),
                  pl.BlockSpec((1, tk, HEAD_DIM), lambda g, j, i: (g, j, 0)),
                  pl.BlockSpec((1, tk, LANES), lambda g, j, i: (g, j, 0)),
                  pl.BlockSpec((GQA_REP, tq, tk), lambda g, j, i: (g, i, j)),
                  pl.BlockSpec((1, tq, LANES), lambda g, j, i: (j, i, g)),
                  pl.BlockSpec((tq, rw), lambda g, j, i: (i, g)),
                  pl.BlockSpec((tq, LANES), lambda g, j, i: (i, g)),
                  pl.BlockSpec((tq, LANES), lambda g, j, i: (i, g))],
        out_specs=[pl.BlockSpec((lp, rw), lambda g, j, i: (0, g)),
                   pl.BlockSpec((1, tk, HEAD_DIM), lambda g, j, i: (g, j, 0)),
                   pl.BlockSpec((1, tk, HEAD_DIM), lambda g, j, i: (g, j, 0))],
        out_shape=[jax.ShapeDtypeStruct((lp, d), F32), jax.ShapeDtypeStruct((kvh, lp, HEAD_DIM), F32),
                   jax.ShapeDtypeStruct((kvh, lp, HEAD_DIM), F32)],
        scratch_shapes=[pltpu.VMEM((tk, HEAD_DIM), F32), pltpu.VMEM((tk, HEAD_DIM), F32)],
        compiler_params=_params(("parallel", "arbitrary", "arbitrary")),
    )(q, k, v, pt, mblk, do, lse, delta)


def _ssm_math(a_re, a_im, log_dt, bt_re, bt_im):
    dt = jnp.exp(log_dt)
    lam_re = jnp.minimum(a_re, EIG_RE_MAX)
    lam_im = a_im
    mag = jnp.exp(lam_re * dt)
    ang = lam_im * dt
    lb_re = mag * jnp.cos(ang)
    lb_im = mag * jnp.sin(ang)
    num_re = lb_re - 1.0
    num_im = lb_im
    den = lam_re * lam_re + lam_im * lam_im
    f_re = (num_re * lam_re + num_im * lam_im) / den
    f_im = (num_im * lam_re - num_re * lam_im) / den
    bb_re = f_re[:, None, :] * bt_re - f_im[:, None, :] * bt_im
    bb_im = f_re[:, None, :] * bt_im + f_im[:, None, :] * bt_re
    return lb_re, lb_im, bb_re, bb_im


def _ssm_discretize(a_re, a_im, log_dt, bt_re, bt_im):
    nd, g, n = a_re.shape
    p = bt_re.shape[2]

    def body(ar_ref, ai_ref, ld_ref, br_ref, bi_ref, bbr_ref, bbi_ref, pr_ref, pi_ref, hr_ref, hi_ref):
        lb_re, lb_im, bb_re, bb_im = _ssm_math(ar_ref[0], ai_ref[0], ld_ref[0], br_ref[0], bi_ref[0])
        bbr_ref[0] = bb_re
        bbi_ref[0] = bb_im
        cr, ci = lb_re, lb_im
        for k in range(KSTEPS):
            pr_ref[0, k] = cr
            pi_ref[0, k] = ci
            if k < KSTEPS - 1:
                cr, ci = cr * lb_re - ci * lb_im, cr * lb_im + ci * lb_re
        for t in range(2):
            cr, ci = cr * cr - ci * ci, 2.0 * cr * ci
            hr_ref[0, t] = cr
            hi_ref[0, t] = ci

    s3 = pl.BlockSpec((1, g, n), lambda i: (i, 0, 0))
    s4 = pl.BlockSpec((1, g, p, n), lambda i: (i, 0, 0, 0))
    sp = pl.BlockSpec((1, KSTEPS, g, n), lambda i: (i, 0, 0, 0))
    sh = pl.BlockSpec((1, 2, g, n), lambda i: (i, 0, 0, 0))
    return pl.pallas_call(
        body, name="ssm_discretize", grid=(nd,),
        in_specs=[s3, s3, pl.BlockSpec((1, g, 1), lambda i: (i, 0, 0)), s4, s4],
        out_specs=[s4, s4, sp, sp, sh, sh],
        out_shape=[jax.ShapeDtypeStruct((nd, g, p, n), F32)] * 2 + [jax.ShapeDtypeStruct((nd, KSTEPS, g, n), F32)] * 2
        + [jax.ShapeDtypeStruct((nd, 2, g, n), F32)] * 2,
        compiler_params=_params(("parallel",)),
    )(a_re, a_im, log_dt, bt_re, bt_im)


def _ssm_param_bwd(a_re, a_im, log_dt, bt_re, bt_im, dlb_re, dlb_im, dbb_re, dbb_im):
    nd, g, n = a_re.shape
    p = bt_re.shape[2]

    def body(ar_ref, ai_ref, ld_ref, br_ref, bi_ref, c0_ref, c1_ref, c2_ref, c3_ref,
             o0_ref, o1_ref, o2_ref, o3_ref, o4_ref):
        _, vjp = jax.vjp(_ssm_math, ar_ref[0], ai_ref[0], ld_ref[0], br_ref[0], bi_ref[0])
        outs = vjp((c0_ref[0], c1_ref[0], c2_ref[0], c3_ref[0]))
        for ref, val in zip((o0_ref, o1_ref, o2_ref, o3_ref, o4_ref), outs):
            ref[0] = val

    s3 = pl.BlockSpec((1, g, n), lambda i: (i, 0, 0))
    s1 = pl.BlockSpec((1, g, 1), lambda i: (i, 0, 0))
    s4 = pl.BlockSpec((1, g, p, n), lambda i: (i, 0, 0, 0))
    return pl.pallas_call(
        body, name="ssm_param_bwd", grid=(nd,),
        in_specs=[s3, s3, s1, s4, s4, s3, s3, s4, s4],
        out_specs=[s3, s3, s1, s4, s4],
        out_shape=[jax.ShapeDtypeStruct((nd, g, n), F32)] * 2 + [jax.ShapeDtypeStruct((nd, g, 1), F32)]
        + [jax.ShapeDtypeStruct((nd, g, p, n), F32)] * 2,
        compiler_params=_params(("parallel",)),
    )(a_re, a_im, log_dt, bt_re, bt_im, dlb_re, dlb_im, dbb_re, dbb_im)


def _cmul(ar, ai, xr, xi, conj):
    if conj:
        return ar * xr + ai * xi, ar * xi - ai * xr
    return ar * xr - ai * xi, ar * xi + ai * xr


def _scan_chunk(buf, tab, carry, ein, nj, rev, conj, base=0):
    ks = list(range(KSTEPS))
    if rev:
        ks = ks[::-1]
    sub = lax.broadcasted_iota(jnp.int32, (SUBLANES, SCAN_LANES), 0)
    edge = sub == (SUBLANES - 1 if rev else 0)

    def step(j, _):
        jr, ji = j, nj + j
        ar, ai = tab[base, jr], tab[base, ji]
        hr = jnp.zeros((SUBLANES, SCAN_LANES), F32)
        hi = jnp.zeros((SUBLANES, SCAN_LANES), F32)
        for k in ks:
            rows = pl.ds(k * SUBLANES, SUBLANES)
            pr, pi_ = _cmul(ar, ai, hr, hi, conj)
            hr = pr + buf[jr, rows, :]
            hi = pi_ + buf[ji, rows, :]
            buf[jr, rows, :] = hr
            buf[ji, rows, :] = hi
        shift = SUBLANES - 1 if rev else 1
        er = jnp.where(edge, carry[jr], pltpu.roll(hr, shift, 0))
        ei = jnp.where(edge, carry[ji], pltpu.roll(hi, shift, 0))
        for t, dist in enumerate((1, 2, 4)):
            sh = SUBLANES - dist if rev else dist
            pr, pi_ = _cmul(tab[base + 1 + t, jr], tab[base + 1 + t, ji], pltpu.roll(er, sh, 0), pltpu.roll(ei, sh, 0), conj)
            er, ei = er + pr, ei + pi_
        ein[jr] = er
        ein[ji] = ei
        pr, pi_ = _cmul(tab[base + 4 + KSTEPS - 1, jr], tab[base + 4 + KSTEPS - 1, ji], er, ei, conj)
        last = 0 if rev else SUBLANES - 1
        carry[jr] = jnp.broadcast_to((hr + pr)[last:last + 1, :], (SUBLANES, SCAN_LANES))
        carry[ji] = jnp.broadcast_to((hi + pi_)[last:last + 1, :], (SUBLANES, SCAN_LANES))
        for n, k in enumerate(ks):
            rows = pl.ds(k * SUBLANES, SUBLANES)
            pr, pi_ = _cmul(tab[base + 4 + n, jr], tab[base + 4 + n, ji], er, ei, conj)
            buf[jr, rows, :] += pr
            buf[ji, rows, :] += pi_
        return 0

    lax.fori_loop(0, nj, step, 0)


def _state_lanes(b):
    per = SCAN_LANES // SSM_BLOCK
    return b // per, slice((b % per) * SSM_BLOCK, (b % per + 1) * SSM_BLOCK)


def _project_in(src, w_ref, buf, nj):
    nb, cb, _ = w_ref.shape
    for b in range(nb):
        res = _dot(src[:, b * cb:(b + 1) * cb], w_ref[b])
        j, lanes = _state_lanes(b)
        buf[j, :, lanes] = res[:, :SSM_BLOCK]
        buf[nj + j, :, lanes] = res[:, SSM_BLOCK:]


def _state_block(buf, b, nj):
    j, lanes = _state_lanes(b)
    return jnp.concatenate([buf[j, :, lanes], buf[nj + j, :, lanes]], axis=1).astype(BF16)


def _project_out(buf, w_ref, nj):
    return jnp.concatenate([_dot_nt(_state_block(buf, b, nj), w_ref[b]) for b in range(w_ref.shape[0])], axis=1)


def _ssm_fwd(u, wb, wct, tab, rev, name):
    lp, w = u.shape
    nb, cb, _ = wb.shape
    nj = nb * SSM_BLOCK // SCAN_LANES
    nc = lp // CHUNK
    ntab = tab.shape[0]
    cidx = (lambda c: nc - 1 - c) if rev else (lambda c: c)

    def body(u_ref, wb_ref, wct_ref, tab_ref, y_ref, ck_ref, buf, carry, ein):
        @pl.when(pl.program_id(0) == 0)
        def _():
            carry[...] = jnp.zeros(carry.shape, F32)

        _project_in(u_ref[...].astype(BF16), wb_ref, buf, nj)
        ck_ref[0] = carry[...]
        _scan_chunk(buf, tab_ref, carry, ein, nj, rev, False)
        y_ref[...] = _project_out(buf, wct_ref, nj)

    wshape = (nb, cb, 2 * SSM_BLOCK)
    return pl.pallas_call(
        body, name=name, grid=(nc,),
        in_specs=[pl.BlockSpec((CHUNK, w), lambda c: (cidx(c), 0)), _full(wshape), _full(wshape),
                  _full((ntab, 2 * nj, SUBLANES, SCAN_LANES))],
        out_specs=[pl.BlockSpec((CHUNK, w), lambda c: (cidx(c), 0)),
                   pl.BlockSpec((1, 2 * nj, SUBLANES, SCAN_LANES), lambda c: (cidx(c), 0, 0, 0))],
        out_shape=[jax.ShapeDtypeStruct((lp, w), F32), jax.ShapeDtypeStruct((nc, 2 * nj, SUBLANES, SCAN_LANES), F32)],
        scratch_shapes=[pltpu.VMEM((2 * nj, CHUNK, SCAN_LANES), F32), pltpu.VMEM((2 * nj, SUBLANES, SCAN_LANES), F32),
                        pltpu.VMEM((2 * nj, SUBLANES, SCAN_LANES), F32)],
        compiler_params=_params(("arbitrary",)),
    )(u, wb, wct, tab)


def _ssm_bwd(u, dy, ckpt, wb, wct, tab, rev, name, scatter=()):
    lp, w = u.shape
    nb, cb, _ = wb.shape
    nj = nb * SSM_BLOCK // SCAN_LANES
    nc = lp // CHUNK
    ntab = tab.shape[0]
    cidx = (lambda c: c) if rev else (lambda c: nc - 1 - c)

    ns = len(scatter)

    def body(*refs):
        u_ref, dy_ref, ck_ref, wb_ref, wct_ref, tab_hbm = refs[:6]
        du_ref, dbb_ref, dcc_ref, dlb_ref = refs[6 + ns:10 + ns]
        tab_ref, dwb_ref, dwc_ref, xs, ls, xcar, lcar, xin, lin = refs[10 + 2 * ns:19 + 2 * ns]
        c = pl.program_id(0)

        if ns:
            start, finish = _scatter_phases(refs[6:6 + ns], refs[10 + ns:10 + 2 * ns], *refs[19 + 2 * ns:])
            pl.when(c == 0)(start)
            pl.when(c == nc - 1)(finish)

        @pl.when(c == 0)
        def _():
            pltpu.sync_copy(tab_hbm, tab_ref)
            lcar[...] = jnp.zeros(lcar.shape, F32)
            dwb_ref[...] = jnp.zeros(dwb_ref.shape, F32)
            dwc_ref[...] = jnp.zeros(dwc_ref.shape, F32)
            dlb_ref[...] = jnp.zeros(dlb_ref.shape, F32)

        ub = u_ref[...].astype(BF16)
        dyb = dy_ref[...].astype(BF16)
        _project_in(ub, wb_ref, xs, nj)
        xcar[...] = ck_ref[0]
        _scan_chunk(xs, tab_ref, xcar, xin, nj, rev, False)
        _project_in(dyb, wct_ref, ls, nj)
        _scan_chunk(ls, tab_ref, lcar, lin, nj, not rev, True, base=ntab // 2)
        dus = []
        for b in range(nb):
            chans = slice(b * cb, (b + 1) * cb)
            xb = _state_block(xs, b, nj)
            lb = _state_block(ls, b, nj)
            dwc_ref[b] += _dot_tn(dyb[:, chans], xb)
            dwb_ref[b] += _dot_tn(ub[:, chans], lb)
            dus.append(_dot_nt(lb, wb_ref[b]))
        du_ref[...] = jnp.concatenate(dus, axis=1)

        def step(j, _):
            jr, ji = j, nj + j
            ar = jnp.zeros((SUBLANES, SCAN_LANES), F32)
            ai = jnp.zeros((SUBLANES, SCAN_LANES), F32)
            for k in range(KSTEPS):
                kp = k + 1 if rev else k - 1
                rows = pl.ds(k * SUBLANES, SUBLANES)
                if 0 <= kp < KSTEPS:
                    prow = pl.ds(kp * SUBLANES, SUBLANES)
                    xr, xi = xs[jr, prow, :], xs[ji, prow, :]
                else:
                    xr, xi = xin[jr], xin[ji]
                lr, li = ls[jr, rows, :], ls[ji, rows, :]
                ar += lr * xr + li * xi
                ai += li * xr - lr * xi
            dlb_ref[jr] += ar
            dlb_ref[ji] += ai
            return 0

        lax.fori_loop(0, nj, step, 0)

        @pl.when(c == nc - 1)
        def _():
            for b in range(2 * nj):
                dlb_ref[b] = jnp.broadcast_to(jnp.sum(dlb_ref[b], axis=0, keepdims=True), (SUBLANES, SCAN_LANES))
            for g in range(w // SSM_GROUP):
                b, gl = divmod(g, cb // SSM_GROUP)
                rows = slice(gl * SSM_GROUP, (gl + 1) * SSM_GROUP)
                for part in range(2):
                    cols = slice(part * SSM_BLOCK + gl * SSM_STATE, part * SSM_BLOCK + (gl + 1) * SSM_STATE)
                    dbb_ref[part, g * SSM_GROUP:(g + 1) * SSM_GROUP, :] = dwb_ref[b, rows, cols]
                    dcc_ref[part, g * SSM_GROUP:(g + 1) * SSM_GROUP, :] = dwc_ref[b, rows, cols]

    st = (2 * nj, SUBLANES, SCAN_LANES)
    wshape = (nb, cb, 2 * SSM_BLOCK)
    sems = [pltpu.SemaphoreType.DMA((ns, 3)), pltpu.SemaphoreType.DMA((ns, 3)), pltpu.SemaphoreType.DMA((ns,))]
    res = pl.pallas_call(
        body, name=name, grid=(nc,),
        in_specs=[pl.BlockSpec((CHUNK, w), lambda c: (cidx(c), 0)), pl.BlockSpec((CHUNK, w), lambda c: (cidx(c), 0)),
                  pl.BlockSpec((1,) + st, lambda c: (cidx(c), 0, 0, 0)), _full(wshape), _full(wshape), _ANY]
        + [_ANY] * ns,
        out_specs=[pl.BlockSpec((CHUNK, w), lambda c: (cidx(c), 0)), _full((2, w, SSM_STATE)),
                   _full((2, w, SSM_STATE)), _full(st)] + [_ANY] * ns,
        out_shape=[jax.ShapeDtypeStruct((lp, w), F32), jax.ShapeDtypeStruct((2, w, SSM_STATE), F32),
                   jax.ShapeDtypeStruct((2, w, SSM_STATE), F32), jax.ShapeDtypeStruct(st, F32)]
        + [jax.ShapeDtypeStruct(p.shape, p.dtype) for p in scatter],
        scratch_shapes=[pltpu.VMEM((ntab,) + st, F32), pltpu.VMEM(wshape, F32), pltpu.VMEM(wshape, F32),
                        pltpu.VMEM((2 * nj, CHUNK, SCAN_LANES), F32), pltpu.VMEM((2 * nj, CHUNK, SCAN_LANES), F32),
                        pltpu.VMEM(st, F32), pltpu.VMEM(st, F32), pltpu.VMEM(st, F32), pltpu.VMEM(st, F32)]
        + (sems if ns else []),
        compiler_params=_params(("arbitrary",)),
    )(u, dy, ckpt, wb, wct, tab, *scatter)
    return res[0], res[1], res[2], res[3], list(res[4:])


def _embed_blocks(t_re, t_im):
    g, p, n = t_re.shape
    gb = SSM_BLOCK // n
    eye = jnp.eye(gb, dtype=t_re.dtype)
    parts = [jnp.einsum('bgpn,gh->bgphn', t.reshape(g // gb, gb, p, n), eye).reshape(g // gb, gb * p, gb * n)
             for t in (t_re, t_im)]
    return jnp.concatenate(parts, axis=2)


def _scan_tables(pw_re, pw_im, hi_re, hi_im, rev):
    s = pw_re.shape[1] * pw_re.shape[2]
    nj = s // SCAN_LANES
    sub = np.arange(SUBLANES)
    live = np.ones((4 + KSTEPS, 1, SUBLANES, 1), bool)
    for row, dist in ((1, 1), (2, 2), (3, 4)):
        live[row, 0, :, 0] = (sub < SUBLANES - dist) if rev else (sub >= dist)

    def lay(pw, hi):
        rows = jnp.concatenate([pw[:1], pw[KSTEPS - 1:], hi, pw], axis=0).reshape(4 + KSTEPS, nj, 1, SCAN_LANES)
        return jnp.where(live, jnp.broadcast_to(rows, (4 + KSTEPS, nj, SUBLANES, SCAN_LANES)), 0.0)

    return jnp.concatenate([lay(pw_re, hi_re), lay(pw_im, hi_im)], axis=1)


def _adamw(w, g, m, v, tm):
    r, c = w.shape
    c1 = 1.0 - ADAM_B1 ** ADAM_STEP
    c2 = 1.0 - ADAM_B2 ** ADAM_STEP

    def body(w_ref, g_ref, m_ref, v_ref, d_ref, nm_ref, nv_ref):
        gg = g_ref[...]
        nm = ADAM_B1 * m_ref[...] + (1.0 - ADAM_B1) * gg
        nv = ADAM_B2 * v_ref[...] + (1.0 - ADAM_B2) * (gg * gg)
        nm_ref[...] = nm
        nv_ref[...] = nv
        d_ref[...] = -ADAM_LR * ((nm / c1) / (jnp.sqrt(nv / c2) + ADAM_EPS) + ADAM_WD * w_ref[...])

    spec = _row(tm, c)
    return pl.pallas_call(
        body, name="adamw", grid=(r // tm,), in_specs=[spec] * 4, out_specs=[spec] * 3,
        out_shape=[jax.ShapeDtypeStruct((r, c), F32)] * 3, compiler_params=_params(("parallel",)),
    )(w, g, m, v)


def _pair_sum(g42, got, core, out_dtype, tm, name):
    _, _, r, c = g42.shape

    def body(core_ref, a_ref, b_ref, o_ref):
        o_ref[...] = (a_ref[...] + b_ref[...]).astype(out_dtype)

    grid_spec = pltpu.PrefetchScalarGridSpec(
        num_scalar_prefetch=1, grid=(4, r // tm),
        in_specs=[pl.BlockSpec((1, None, tm, c), lambda s, i, core_ref: (s, core_ref[0], i, 0)),
                  pl.BlockSpec((1, tm, c), lambda s, i, core_ref: (s, i, 0))],
        out_specs=pl.BlockSpec((1, tm, c), lambda s, i, core_ref: (s, i, 0)))
    return pl.pallas_call(
        body, name=name, grid_spec=grid_spec, out_shape=jax.ShapeDtypeStruct((4, r, c), out_dtype),
        compiler_params=_params(("parallel", "parallel")),
    )(core, g42, got)


def _sum4(a, core, tm, name):
    _, r, c = a.shape

    def body(core_ref, a_ref, o_ref):
        o_ref[...] = ((a_ref[0].astype(F32) + a_ref[1].astype(F32)) + a_ref[2].astype(F32)) + a_ref[3].astype(F32)

    grid_spec = pltpu.PrefetchScalarGridSpec(
        num_scalar_prefetch=1, grid=(r // tm,),
        in_specs=[pl.BlockSpec((4, tm, c), lambda i, core_ref: (0, i, 0))],
        out_specs=pl.BlockSpec((None, tm, c), lambda i, core_ref: (core_ref[0], i, 0)))
    return pl.pallas_call(
        body, name=name, grid_spec=grid_spec, out_shape=jax.ShapeDtypeStruct((2, r, c), F32),
        compiler_params=_params(("parallel",)),
    )(core, a)


_ANY = pl.BlockSpec(memory_space=pl.ANY)


def _gather_phases(xs, outs, send_sems, recv_sems, local_sems):
    n = len(xs)

    def parts():
        x, y, c = lax.axis_index("x"), lax.axis_index("y"), lax.axis_index("c")
        return c, (x, y, c), (x, y, 1 - c), [(1 - x, y), (x, 1 - y), (1 - x, 1 - y)]

    def slot(t, px, py, pc):
        return outs[t].at[4 * px + 2 * py + pc]

    def copy(t, k, blk, to, src=None):
        return pltpu.make_async_remote_copy(
            src_ref=slot(t, *blk) if src is None else src, dst_ref=slot(t, *blk),
            send_sem=send_sems.at[t, k], recv_sem=recv_sems.at[t, k], device_id=to, device_id_type=MESH_ID)

    def own(t, me):
        return pltpu.make_async_copy(xs[t], slot(t, *me), local_sems.at[t])

    def first(t, c, me, sibling, chips):
        return [copy(t, 0, me, sibling, src=xs[t])] + [copy(t, 1 + j, me, (*chip, c), src=xs[t])
                                                       for j, chip in enumerate(chips)]

    def passed(t, c, sibling, chips):
        return [copy(t, 4 + j, (*chip, c), sibling) for j, chip in enumerate(chips)]

    def start():
        c, me, sibling, chips = parts()
        for t in range(n):
            own(t, me).start()
        for t in range(n):
            for cp in first(t, c, me, sibling, chips):
                cp.start()

    def forward():
        c, me, sibling, chips = parts()
        for j, chip in enumerate(chips):
            for t in range(n):
                copy(t, 1 + j, (*chip, c), me).wait_recv()
                passed(t, c, sibling, chips)[j].start()

    def finish():
        c, me, sibling, chips = parts()
        for t in range(n):
            copy(t, 0, sibling, me).wait_recv()
        for j, chip in enumerate(chips):
            for t in range(n):
                copy(t, 4 + j, (*chip, 1 - c), me).wait_recv()
        for t in range(n):
            for cp in first(t, c, me, sibling, chips) + passed(t, c, sibling, chips):
                cp.wait_send()
            own(t, me).wait()

    return start, forward, finish


def _all_gather8(blocks, name):
    n = len(blocks)

    def body(*refs):
        for phase in _gather_phases(refs[:n], refs[n:2 * n], *refs[2 * n:]):
            phase()

    return pl.pallas_call(
        body, name=name, out_shape=[jax.ShapeDtypeStruct((8,) + b.shape, b.dtype) for b in blocks],
        in_specs=[_ANY] * n, out_specs=[_ANY] * n,
        scratch_shapes=[pltpu.SemaphoreType.DMA((n, 7)), pltpu.SemaphoreType.DMA((n, 7)),
                        pltpu.SemaphoreType.DMA((n,))],
    )(*blocks)


def _pair_exchange(gs, name):
    n = len(gs)

    def body(*refs):
        g_refs, outs = refs[:n], refs[n:2 * n]
        send_sems, recv_sems = refs[2 * n:]
        x, y, c = lax.axis_index("x"), lax.axis_index("y"), lax.axis_index("c")
        cps = [pltpu.make_async_remote_copy(
            src_ref=g_refs[t].at[:, 1 - c], dst_ref=outs[t], send_sem=send_sems.at[t], recv_sem=recv_sems.at[t],
            device_id=(x, y, 1 - c), device_id_type=MESH_ID) for t in range(n)]
        for cp in cps:
            cp.start()
        for cp in cps:
            cp.wait()

    return pl.pallas_call(
        body, name=name,
        out_shape=[jax.ShapeDtypeStruct((g.shape[0],) + g.shape[2:], g.dtype) for g in gs],
        in_specs=[_ANY] * n, out_specs=[_ANY] * n,
        scratch_shapes=[pltpu.SemaphoreType.DMA((n,)), pltpu.SemaphoreType.DMA((n,))],
    )(*gs)


def _scatter_phases(p_refs, outs, send_sems, recv_sems, local_sems):
    n = len(p_refs)

    def parts():
        x, y, c = lax.axis_index("x"), lax.axis_index("y"), lax.axis_index("c")
        return c, 2 * x + y, [(1 - x, y), (x, 1 - y), (1 - x, 1 - y)]

    def copy(t, k, src_slab, dst_slab, chip, c):
        return pltpu.make_async_remote_copy(
            src_ref=p_refs[t].at[src_slab], dst_ref=outs[t].at[dst_slab], send_sem=send_sems.at[t, k],
            recv_sem=recv_sems.at[t, k], device_id=(*chip, c), device_id_type=MESH_ID)

    def own(t, mine):
        return pltpu.make_async_copy(p_refs[t].at[mine], outs[t].at[mine], local_sems.at[t])

    def start():
        c, mine, chips = parts()
        for t in range(n):
            own(t, mine).start()
        for k, (cx, cy) in enumerate(chips):
            for t in range(n):
                copy(t, k, 2 * cx + cy, mine, (cx, cy), c).start()

    def finish():
        c, mine, chips = parts()
        for k, (cx, cy) in enumerate(chips):
            for t in range(n):
                copy(t, k, mine, 2 * cx + cy, (cx, cy), c).wait_recv()
        for t in range(n):
            for k, (cx, cy) in enumerate(chips):
                copy(t, k, 2 * cx + cy, mine, (cx, cy), c).wait_send()
            own(t, mine).wait()

    return start, finish


def _chip_scatter(ps, name):
    n = len(ps)

    def body(*refs):
        for phase in _scatter_phases(refs[:n], refs[n:2 * n], *refs[2 * n:]):
            phase()

    return pl.pallas_call(
        body, name=name, out_shape=[jax.ShapeDtypeStruct(p.shape, p.dtype) for p in ps],
        in_specs=[_ANY] * n, out_specs=[_ANY] * n,
        scratch_shapes=[pltpu.SemaphoreType.DMA((n, 3)), pltpu.SemaphoreType.DMA((n, 3)),
                        pltpu.SemaphoreType.DMA((n,))],
    )(*ps)


def _pair_gather(rs, name):
    n = len(rs)

    def body(*refs):
        ins, outs = refs[:n], refs[n:2 * n]
        send_sems, recv_sems = refs[2 * n:]
        x, y, c = lax.axis_index("x"), lax.axis_index("y"), lax.axis_index("c")

        def copy(t, slab):
            return pltpu.make_async_remote_copy(
                src_ref=ins[t].at[slab], dst_ref=outs[t].at[slab], send_sem=send_sems.at[t],
                recv_sem=recv_sems.at[t], device_id=(x, y, 1 - c), device_id_type=MESH_ID)

        sends = [copy(t, c) for t in range(n)]
        for cp in sends:
            cp.start()
        for t in range(n):
            copy(t, 1 - c).wait_recv()
        for cp in sends:
            cp.wait_send()

    return pl.pallas_call(
        body, name=name, out_shape=[jax.ShapeDtypeStruct(r.shape, r.dtype) for r in rs],
        in_specs=[_ANY] * n, out_specs=[_ANY] * n, input_output_aliases={t: t for t in range(n)},
        scratch_shapes=[pltpu.SemaphoreType.DMA((n,)), pltpu.SemaphoreType.DMA((n,))],
    )(*rs)


PACK_COLS = 1024
BIG = (("meta_tokens", 1), ("w_in", 1), ("w_glu", 0), ("w_ssm_proj", 1), ("w_attn_proj", 0), ("w_out", 0),
       ("w_mlp_in", 1), ("w_mlp_out", 0))
SMALL = ("norm_mix_g", "ssm_a_re", "ssm_a_im", "ssm_log_dt", "ssm_b_re", "ssm_b_im", "ssm_c_re", "ssm_c_im",
         "ssm_d", "b_glu", "q_norm_g", "k_norm_g", "norm_mlp_g", "norm_final_g")


def _pad_rows(flat, mult_rows):
    n = flat.shape[0]
    unit = PACK_COLS * mult_rows
    total = -(-n // unit) * unit
    return jnp.pad(flat, (0, total - n)).reshape(total // PACK_COLS, PACK_COLS)


def _half(t, c):
    return lax.dynamic_slice_in_dim(t, c * (t.shape[0] // 2), t.shape[0] // 2, 0)


EARLY_WEIGHTS = ("meta_tokens", "w_in", "w_glu")
LATE_WEIGHTS = tuple(name for name, _ in BIG if name not in EARLY_WEIGHTS)


def _weight_blocks(shards, c, names):
    return [_half(shards[name], c) if name == "meta_tokens" else _half(shards[name], c).astype(BF16) for name in names]


def _shard_major(names, gathered):
    return {name: g.reshape((4, 2 * g.shape[1]) + g.shape[2:]) for name, g in zip(names, gathered)}


class _Reduction:
    def __init__(self, grads, wire, labels, c, tag):
        self.labels, self.wire, self.c, self.tag = labels, wire, c, tag
        self.core = c.astype(jnp.int32).reshape(1)
        g42 = [g.reshape(4, 2, g.shape[1] // 2, g.shape[2]) for g in grads]
        self.tiles = [_pick_tile(g.shape[2], 256, SUBLANES if dt == F32 else 2 * SUBLANES) for g, dt in zip(g42, wire)]
        got = _pair_exchange(g42, "grad_pair_exchange_" + tag)
        self.pair = [_pair_sum(g, o, self.core, dt, tm, "pair_sum_" + lb)
                     for g, o, dt, tm, lb in zip(g42, got, wire, self.tiles, labels)]

    def finish(self, by_src, gathered):
        red = [_sum4(b, self.core, tm, "chip_sum_" + lb) for b, tm, lb in zip(by_src, self.tiles, self.labels)]
        both = _pair_gather(red[:gathered], "grad_pair_gather_" + self.tag)
        pieces = [lax.dynamic_index_in_dim(r, self.c, 0, keepdims=False) for r in red[gathered:]]
        return [b.reshape(2 * b.shape[1], b.shape[2]) for b in both], pieces


def _small_as_shards(small_flat):
    unit = 8 * SUBLANES * PACK_COLS
    k = -(-small_flat.shape[0] // unit) * unit
    return jnp.pad(small_flat, (0, k - small_flat.shape[0])).reshape(4, k // (4 * PACK_COLS), PACK_COLS)


def _to_chunk_order(a):
    lp = a.shape[0]
    rest = a.shape[1:]
    a = a.reshape((lp // CHUNK, SUBLANES, KSTEPS) + rest)
    return a.swapaxes(1, 2).reshape((lp,) + rest)


def _from_chunk_order(a):
    lp = a.shape[0]
    rest = a.shape[1:]
    a = a.reshape((lp // CHUNK, KSTEPS, SUBLANES) + rest)
    return a.swapaxes(1, 2).reshape((lp,) + rest)


def _rope_tables(l_total, lp):
    n_real = l_total - N_META
    pos = np.arange(n_real)
    row_id = (pos // GRID_W).astype(np.float32)
    col_id = (pos % GRID_W).astype(np.float32)
    ppa = HEAD_DIM // 4
    inv_freq = (ROPE_THETA ** (-np.arange(ppa, dtype=np.float64) / ppa)).astype(np.float32)
    ang = np.concatenate([row_id[:, None] * inv_freq, col_id[:, None] * inv_freq], axis=-1)
    ang = np.concatenate([np.zeros((N_META, HEAD_DIM // 2), np.float32), ang,
                          np.zeros((lp - l_total, HEAD_DIM // 2), np.float32)], axis=0).astype(np.float64)
    cos = np.repeat(np.cos(ang), 2, axis=1)
    sin = np.repeat(np.sin(ang), 2, axis=1) * np.tile(np.asarray([-1.0, 1.0]), HEAD_DIM // 2)
    reps = (1, LANES // HEAD_DIM)
    return np.tile(cos, reps).astype(np.float32), np.tile(sin, reps).astype(np.float32)


def kernel(x, meta_tokens, norm_mix_g, w_in, ssm_a_re, ssm_a_im, ssm_log_dt, ssm_b_re, ssm_b_im, ssm_c_re, ssm_c_im, ssm_d, w_glu, b_glu, q_norm_g, k_norm_g, w_ssm_proj, w_attn_proj, w_out, norm_mlp_g, w_mlp_in, w_mlp_out, norm_final_g, loss_target, m_meta_tokens, m_norm_mix_g, m_w_in, m_ssm_a_re, m_ssm_a_im, m_ssm_log_dt, m_ssm_b_re, m_ssm_b_im, m_ssm_c_re, m_ssm_c_im, m_ssm_d, m_w_glu, m_b_glu, m_q_norm_g, m_k_norm_g, m_w_ssm_proj, m_w_attn_proj, m_w_out, m_norm_mlp_g, m_w_mlp_in, m_w_mlp_out, m_norm_final_g, v_meta_tokens, v_norm_mix_g, v_w_in, v_ssm_a_re, v_ssm_a_im, v_ssm_log_dt, v_ssm_b_re, v_ssm_b_im, v_ssm_c_re, v_ssm_c_im, v_ssm_d, v_w_glu, v_b_glu, v_q_norm_g, v_k_norm_g, v_w_ssm_proj, v_w_attn_proj, v_w_out, v_norm_mlp_g, v_w_mlp_in, v_w_mlp_out, v_norm_final_g):
    args = dict(locals())
    names = list(dict.fromkeys([n for n, _ in BIG] + list(SMALL)))
    order = ['meta_tokens', 'norm_mix_g', 'w_in', 'ssm_a_re', 'ssm_a_im', 'ssm_log_dt', 'ssm_b_re', 'ssm_b_im',
             'ssm_c_re', 'ssm_c_im', 'ssm_d', 'w_glu', 'b_glu', 'q_norm_g', 'k_norm_g', 'w_ssm_proj', 'w_attn_proj',
             'w_out', 'norm_mlp_g', 'w_mlp_in', 'w_mlp_out', 'norm_final_g']
    assert sorted(names) == sorted(order)
    c_idx = lax.axis_index("c")

    seq, d = x.shape[1], x.shape[2]
    l_total = seq + N_META
    lp = -(-l_total // SEQ_ALIGN) * SEQ_ALIGN
    hd = d // 2
    n_groups = hd // SSM_GROUP
    n_state = n_groups * SSM_STATE
    nj = n_state // SCAN_LANES
    kvh = d // HEAD_DIM // GQA_REP

    shard2d = {}
    for name, _ in BIG:
        t = args[name]
        shard2d[name] = t.reshape(t.shape[-2], t.shape[-1])
    full = _shard_major(EARLY_WEIGHTS, _all_gather8(_weight_blocks(shard2d, c_idx, EARLY_WEIGHTS), "weight_all_gather"))
    meta_full = jnp.transpose(full["meta_tokens"], (1, 0, 2)).reshape(N_META, d)
    w_in4 = full["w_in"]
    w_glu_f = full["w_glu"].reshape(hd, hd)

    xin = jnp.concatenate([meta_full, x[0], jnp.zeros((lp - l_total, d), F32)], axis=0)
    xin = _to_chunk_order(xin)
    tgt = _to_chunk_order(jnp.pad(loss_target[0], ((N_META, lp - l_total), (0, 0))))
    pos = np.arange(lp)
    rowmask = jnp.asarray(_to_chunk_order(((pos >= N_META) & (pos < l_total)).astype(np.float32)[:, None]))
    kbias = jnp.asarray(_to_chunk_order(np.where(pos < l_total, 0.0, MASK_VALUE).astype(np.float32)[:, None])
                        .reshape(1, lp))
    cos_t, sin_t = (jnp.asarray(_to_chunk_order(t)) for t in _rope_tables(l_total, lp))
    mean_m, sel = _head_tables(d)

    tm = _pick_tile(lp, 320)
    tm_mid = _pick_tile(lp, 384)
    tm_big = _pick_tile(lp, 640)
    tq = _pick_tile(lp, ATTN_Q_TILE, LANES)
    tk = _pick_tile(lp, ATTN_K_TILE, MXU_DIM)
    assert lp - CHUNK <= (l_total // CHUNK) * CHUNK and tk >= CHUNK
    g_mix = norm_mix_g.reshape(1, d)
    g_mlp = norm_mlp_g.reshape(1, d)
    g_fin = norm_final_g.reshape(1, d)
    qg = jnp.tile(q_norm_g.reshape(1, HEAD_DIM), (1, LANES // HEAD_DIM))
    kg = jnp.tile(k_norm_g.reshape(1, HEAD_DIM), (1, LANES // HEAD_DIM))
    dskip = ssm_d.reshape(1, hd)
    bglu = b_glu.reshape(1, hd)

    a_re, a_im = ssm_a_re[0], ssm_a_im[0]
    log_dt = ssm_log_dt[0][..., None]
    bt_re = jnp.swapaxes(ssm_b_re[0], 2, 3)
    bt_im = jnp.swapaxes(ssm_b_im[0], 2, 3)
    bb_re, bb_im, pw_re, pw_im, hi_re, hi_im = _ssm_discretize(a_re, a_im, log_dt, bt_re, bt_im)
    wb = [_embed_blocks(bb_re[i], bb_im[i]).astype(BF16) for i in range(2)]
    wct = [_embed_blocks(ssm_c_re[0, i], -ssm_c_im[0, i]).astype(BF16) for i in range(2)]
    tabs = [_scan_tables(pw_re[i], pw_im[i], hi_re[i], hi_im[i], rev=(i == 1)) for i in range(2)]
    tabs_adj = [_scan_tables(pw_re[i], pw_im[i], hi_re[i], hi_im[i], rev=(i == 0)) for i in range(2)]

    u, qkv, gates, hb = _in_proj(xin, g_mix, w_in4, tm_mid)
    y0, ck0 = _ssm_fwd(u, wb[0], wct[0], tabs[0], False, "ssm_fwd_0")
    y1, ck1 = _ssm_fwd(u, wb[1], wct[1], tabs[1], True, "ssm_fwd_1")
    yssm = _glu_fwd(u, y0, y1, dskip, w_glu_f, bglu, tm_big)
    q, k, v = _qk_prep(qkv, cos_t, sin_t, qg, kg, mean_m, tm)
    o, lse, pt, mblk, late = _attn_fwd(q, k, v, kbias, tq, tk, gather=_weight_blocks(shard2d, c_idx, LATE_WEIGHTS))
    full = _shard_major(LATE_WEIGHTS, late)
    w_mlp_in4 = full["w_mlp_in"]
    w_ssm_proj4 = full["w_ssm_proj"]
    w_attn_proj_f = full["w_attn_proj"].reshape(d, d)
    w_out_f = full["w_out"].reshape(d, d)
    w_mlp_out_f = full["w_mlp_out"].reshape(4 * d, d)
    h1, merged = _merge_fwd(yssm, o, gates, xin, w_ssm_proj4, w_attn_proj_f, w_out_f, tm_mid)
    r, h2b = _mlp_in(h1, g_mlp, w_mlp_in4, tm_mid)
    h3 = _mlp_out(h1, r, w_mlp_out_f, tm_mid)
    loss_tile, dh3, d_gfin = _final_loss(h3, g_fin, tgt, rowmask, tm_big)

    dz, dh3b = _mlp_bwd_a(dh3, r, w_mlp_out_f, tm_mid)
    dh1, d_gmlp = _mlp_bwd_b(dz, dh3, h1, g_mlp, w_mlp_in4, tm_mid)
    dgates, dms, dma, dyssm, do, delta, dh1b = _merge_bwd(dh1, yssm, o, gates, w_ssm_proj4, w_attn_proj_f, w_out_f,
                                                          sel, tm_mid)
    dyv, d_wglu, d_bglu, d_dskip = _glu_bwd(dyssm, u, y0, y1, dskip, w_glu_f, bglu, tm_big)

    tn = min(d, 1024)
    tm_w = _pick_tile(lp, 3 * MXU_DIM, MXU_DIM)
    grads4 = {
        "w_mlp_in": _wgrad(h2b, dz, 4, tm_w, tn, "wgrad_mlp_in"),
        "w_mlp_out": _wgrad(r, dh3b, 1, tm_w, min(d, 512), "wgrad_mlp_out", square=True).reshape(4, d, d),
        "w_out": _wgrad(merged, dh1b, 1, tm_w, tn, "wgrad_out").reshape(4, d // 4, d),
        "w_attn_proj": _wgrad(o, dma, 1, tm_w, tn, "wgrad_attn_proj").reshape(4, d // 4, d),
        "w_ssm_proj": _wgrad(yssm, dms, 4, tm_w, d // 4, "wgrad_ssm_proj"),
        "w_glu": d_wglu.reshape(4, hd // 4, hd),
    }
    first_names = list(grads4)
    first = _Reduction([grads4[n] for n in first_names], [BF16] * len(first_names), first_names, c_idx, "first")

    du0, dbb0, dcc0, dlb0, first_by_src = _ssm_bwd(u, dyv, ck0, wb[0], wct[0], _both(tabs[0], tabs_adj[0]), False,
                                                   "ssm_bwd_0", scatter=first.pair)
    du1, dbb1, dcc1, dlb1, _ = _ssm_bwd(u, dyv, ck1, wb[1], wct[1], _both(tabs[1], tabs_adj[1]), True, "ssm_bwd_1")
    dq, dk, dv = _attn_bwd(q, k, v, pt, mblk, do, lse, delta, _pick_tile(lp, MXU_DIM, LANES), tk)
    dqkv, d_qg, d_kg = _qk_bwd(qkv, dq, dk, dv, cos_t, sin_t, qg, kg, mean_m, tm)
    dxin, d_gmix, dproj = _in_proj_bwd(dyv, du0, du1, dskip, dqkv, dgates, dh1, xin, g_mix, w_in4, tm)
    red_big = dict(zip(first_names, first.finish(first_by_src, len(first_names))[0]))

    grads4["w_in"] = _wgrad(hb, dproj, 4, tm_w, tn, "wgrad_in")
    dx_nat = _from_chunk_order(dxin)
    grads4["meta_tokens"] = jnp.swapaxes(dx_nat[:N_META].reshape(N_META, 4, d // 4), 0, 1)
    grad_x = dx_nat[N_META:l_total][None]

    dlb = jnp.stack([dlb0, dlb1])[:, :, 0, :]
    dlb_re = dlb[:, :nj].reshape(2, n_groups, SSM_STATE)
    dlb_im = dlb[:, nj:].reshape(2, n_groups, SSM_STATE)
    gpn = (2, 2, n_groups, SSM_GROUP, SSM_STATE)
    dbb = jnp.stack([dbb0, dbb1]).reshape(gpn)
    dcc = jnp.stack([dcc0, dcc1]).reshape(gpn)
    d_are, d_aim, d_logdt, d_btre, d_btim = _ssm_param_bwd(a_re, a_im, log_dt, bt_re, bt_im, dlb_re, dlb_im,
                                                           dbb[:, 0], dbb[:, 1])
    small_grads = {
        "norm_mix_g": d_gmix, "ssm_a_re": d_are, "ssm_a_im": d_aim, "ssm_log_dt": d_logdt,
        "ssm_b_re": jnp.swapaxes(d_btre, 2, 3), "ssm_b_im": jnp.swapaxes(d_btim, 2, 3),
        "ssm_c_re": dcc[:, 0], "ssm_c_im": -dcc[:, 1],
        "ssm_d": d_dskip, "b_glu": d_bglu, "q_norm_g": d_qg[:, :HEAD_DIM], "k_norm_g": d_kg[:, :HEAD_DIM],
        "norm_mlp_g": d_gmlp, "norm_final_g": d_gfin,
    }
    small_flat = jnp.concatenate([small_grads[n].reshape(-1) for n in SMALL] + [loss_tile[0, :1]])

    last = _Reduction([grads4["meta_tokens"], grads4["w_in"], _small_as_shards(small_flat)], [F32, BF16, F32],
                      ["meta_tokens", "w_in", "small"], c_idx, "last")
    (red_big["meta_tokens"], red_big["w_in"]), (small_piece,) = last.finish(
        _chip_scatter(last.pair, "grad_chip_scatter"), 2)
    red_small = _all_gather8([small_piece], "small_grad_all_gather")[0].reshape(-1)[:small_flat.shape[0]]
    loss, red_small = red_small[-1], red_small[:-1]
    grad, delta_w, new_m, new_v = {}, {}, {}, {}
    for name, _ in BIG:
        w2 = shard2d[name]
        shp = args[name].shape
        g2 = red_big[name]
        t = _pick_tile(w2.shape[0], 256, 8)
        dl, nm, nv = _adamw(w2, g2, args["m_" + name].reshape(w2.shape), args["v_" + name].reshape(w2.shape), t)
        grad[name], delta_w[name], new_m[name], new_v[name] = (a.reshape(shp) for a in (g2, dl, nm, nv))

    def pack_small(prefix):
        flat = jnp.concatenate([args[prefix + n].reshape(-1) for n in SMALL])
        return _pad_rows(flat, SUBLANES)

    n_small = red_small.shape[0]
    gs = _pad_rows(red_small, SUBLANES)
    dl, nm, nv = _adamw(pack_small(""), gs, pack_small("m_"), pack_small("v_"), _pick_tile(gs.shape[0], 256, 8))
    off = 0
    for name in SMALL:
        shp = args[name].shape
        k = int(np.prod(shp))
        for dst, src in ((grad, gs), (delta_w, dl), (new_m, nm), (new_v, nv)):
            dst[name] = src.reshape(-1)[off:off + k].reshape(shp)
        off += k
    assert off == n_small

    return (loss, grad_x, *[grad[n] for n in order], *[delta_w[n] for n in order],
            *[new_m[n] for n in order], *[new_v[n] for n in order])


def _both(tab, tab_adj):
    return jnp.concatenate([tab, tab_adj], axis=0)
```

```python
import functools
import math

import numpy as np
import jax
import jax.numpy as jnp
from jax import lax
from jax.experimental import pallas as pl
from jax.experimental.pallas import tpu as pltpu

F32 = jnp.float32
BF16 = jnp.bfloat16

N_META = 16
GRID_W = 64
HEAD_DIM = 64
GQA_REP = 4
SSM_GROUP = 16
SSM_STATE = 64
ROPE_THETA = 10000.0
NORM_EPS = 1e-6
EIG_RE_MAX = -1e-4
ADAM_LR, ADAM_B1, ADAM_B2, ADAM_EPS, ADAM_WD, ADAM_STEP = 0.001, 0.9, 0.999, 1e-08, 0.01, 10

SUBLANES = 8
LANES = 128
CHUNK = 256
KSTEPS = CHUNK // SUBLANES
SCAN_LANES = 512
MXU_DIM = 256
SSM_BLOCK = MXU_DIM
SEQ_ALIGN = MXU_DIM
ATTN_Q_TILE = 384
ATTN_K_TILE = 11 * MXU_DIM
VMEM_LIMIT = 56 << 20
MASK_VALUE = -1e30
MESH_ID = pl.DeviceIdType.MESH


def _dot(a, b):
    return jnp.dot(a, b, preferred_element_type=F32)


def _dot_nt(a, b):
    return lax.dot_general(a, b, (((1,), (1,)), ((), ())), preferred_element_type=F32)


def _dot_tn(a, b):
    return lax.dot_general(a, b, (((0,), (0,)), ((), ())), preferred_element_type=F32)


def _row(tm, width):
    return pl.BlockSpec((tm, width), lambda i: (i, 0))


def _full(shape):
    nd = len(shape)
    return pl.BlockSpec(shape, lambda i: (0,) * nd)


def _params(sem):
    return pltpu.CompilerParams(dimension_semantics=sem, vmem_limit_bytes=VMEM_LIMIT)


def _pick_tile(n, cap, mult=16):
    best = None
    for t in range(mult, min(n, cap) + 1, mult):
        if n % t == 0:
            best = t
    assert best is not None, (n, cap)
    return best


def _rstd(x):
    return lax.rsqrt(jnp.mean(x * x, axis=-1, keepdims=True) + NORM_EPS)


def _rms(x, g):
    return x * _rstd(x) * g


def _rms_bwd(dy, x, g):
    r = _rstd(x)
    xh = x * r
    gdy = dy * g
    dx = r * (gdy - xh * jnp.mean(gdy * xh, axis=-1, keepdims=True))
    return dx, dy * xh


def _split_dot(x, m):
    hi = x.astype(BF16)
    lo = (x - hi.astype(F32)).astype(BF16)
    return _dot(hi, m) + _dot(lo, m)


def _sigmoid(x):
    return 1.0 / (1.0 + jnp.exp(-x))


def _acc_rows(ref, val, first):
    s = jnp.sum(val, axis=0, keepdims=True)

    @pl.when(first)
    def _():
        ref[...] = s

    @pl.when(jnp.logical_not(first))
    def _():
        ref[...] += s


def _in_proj(xin, g, w4, tm):
    lp, d = xin.shape
    hd = d // 2

    def body(x_ref, g_ref, w_ref, u_ref, qkv_ref, gt_ref, h_ref):
        h = _rms(x_ref[...], g_ref[...]).astype(BF16)
        h_ref[...] = h
        p0 = _dot(h, w_ref[0])
        u_ref[...] = p0[:, :hd]
        qkv_ref[:, :hd] = p0[:, hd:]
        qkv_ref[:, hd:] = _dot(h, w_ref[1])
        gt_ref[:, :d] = _dot(h, w_ref[2])
        gt_ref[:, d:] = _dot(h, w_ref[3])

    return pl.pallas_call(
        body, name="in_proj", grid=(lp // tm,),
        in_specs=[_row(tm, d), _full((1, d)), _full((4, d, d))],
        out_specs=[_row(tm, hd), _row(tm, 3 * hd), _row(tm, 2 * d), _row(tm, d)],
        out_shape=[jax.ShapeDtypeStruct((lp, hd), F32), jax.ShapeDtypeStruct((lp, 3 * hd), F32),
                   jax.ShapeDtypeStruct((lp, 2 * d), F32), jax.ShapeDtypeStruct((lp, d), BF16)],
        compiler_params=_params(("parallel",)),
    )(xin, g, w4)


def _gelu(y):
    return 0.5 * y * (1.0 + lax.erf(y * (1.0 / math.sqrt(2.0))))


def _gelu_grad(y):
    return 0.5 * (1.0 + lax.erf(y * (1.0 / math.sqrt(2.0)))) + y * jnp.exp(-0.5 * y * y) * (1.0 / math.sqrt(2.0 * math.pi))


def _glu_fwd(u, y0, y1, dskip, w_glu, b_glu, tm):
    lp, w = u.shape

    def body(u_ref, y0_ref, y1_ref, d_ref, w_ref, b_ref, o_ref):
        y = u_ref[...] * d_ref[...] + y0_ref[...] + y1_ref[...]
        z = _gelu(y)
        t = _dot(z.astype(BF16), w_ref[...]) + b_ref[...]
        o_ref[...] = (z * _sigmoid(t)).astype(BF16)

    return pl.pallas_call(
        body, name="glu_fwd", grid=(lp // tm,),
        in_specs=[_row(tm, w), _row(tm, w), _row(tm, w), _full((1, w)), _full((w, w)), _full((1, w))],
        out_specs=_row(tm, w), out_shape=jax.ShapeDtypeStruct((lp, w), BF16),
        compiler_params=_params(("parallel",)),
    )(u, y0, y1, dskip, w_glu, b_glu)


def _glu_bwd(dyssm, u, y0, y1, dskip, w_glu, b_glu, tm):
    lp, w = u.shape

    def body(g_ref, u_ref, y0_ref, y1_ref, d_ref, w_ref, b_ref, dy_ref, dw_ref, db_ref, dd_ref):
        first = pl.program_id(0) == 0
        uu = u_ref[...]
        y = uu * d_ref[...] + y0_ref[...] + y1_ref[...]
        z = _gelu(y)
        zb = z.astype(BF16)
        sg = _sigmoid(_dot(zb, w_ref[...]) + b_ref[...])
        g = g_ref[...]
        dt = g * z * sg * (1.0 - sg)
        dtb = dt.astype(BF16)
        dz = g * sg + _dot_nt(dtb, w_ref[...])
        dy = dz * _gelu_grad(y)
        dy_ref[...] = dy
        dw = _dot_tn(zb, dtb)

        @pl.when(first)
        def _():
            dw_ref[...] = dw

        @pl.when(jnp.logical_not(first))
        def _():
            dw_ref[...] += dw

        _acc_rows(db_ref, dt, first)
        _acc_rows(dd_ref, dy * uu, first)

    return pl.pallas_call(
        body, name="glu_bwd", grid=(lp // tm,),
        in_specs=[_row(tm, w), _row(tm, w), _row(tm, w), _row(tm, w), _full((1, w)), _full((w, w)), _full((1, w))],
        out_specs=[_row(tm, w), _full((w, w)), _full((1, w)), _full((1, w))],
        out_shape=[jax.ShapeDtypeStruct((lp, w), F32), jax.ShapeDtypeStruct((w, w), F32),
                   jax.ShapeDtypeStruct((1, w), F32), jax.ShapeDtypeStruct((1, w), F32)],
        compiler_params=_params(("arbitrary",)),
    )(dyssm, u, y0, y1, dskip, w_glu, b_glu)


def _merge_fwd(yssm, o, gates, xin, wsp4, wap, wo, tm):
    lp, d = xin.shape
    w = yssm.shape[1]
    ns = d // 4

    def body(y_ref, o_ref, g_ref, x_ref, wsp_ref, wap_ref, wo_ref, h_ref, m_ref):
        yb = y_ref[...]
        ms = jnp.concatenate([_dot(yb, wsp_ref[s]) for s in range(4)], axis=1)
        ma = _dot(o_ref[...], wap_ref[...])
        merged = (_sigmoid(g_ref[:, :d]) * ms + _sigmoid(g_ref[:, d:]) * ma).astype(BF16)
        m_ref[...] = merged
        h_ref[...] = x_ref[...] + _dot(merged, wo_ref[...])

    return pl.pallas_call(
        body, name="merge_fwd", grid=(lp // tm,),
        in_specs=[_row(tm, w), _row(tm, d), _row(tm, 2 * d), _row(tm, d),
                  _full((4, w, ns)), _full((d, d)), _full((d, d))],
        out_specs=[_row(tm, d), _row(tm, d)],
        out_shape=[jax.ShapeDtypeStruct((lp, d), F32), jax.ShapeDtypeStruct((lp, d), BF16)],
        compiler_params=_params(("parallel",)),
    )(yssm, o, gates, xin, wsp4, wap, wo)


def _merge_bwd(dh1, yssm, o, gates, wsp4, wap, wo, sel, tm):
    lp, d = dh1.shape
    w = yssm.shape[1]
    ns = d // 4
    nsel = sel.shape[1]

    def body(dh_ref, y_ref, o_ref, g_ref, wsp_ref, wap_ref, wo_ref, sel_ref,
             dg_ref, dms_ref, dma_ref, dy_ref, do_ref, dl_ref, dhb_ref):
        dhb = dh_ref[...].astype(BF16)
        dhb_ref[...] = dhb
        dm = _dot_nt(dhb, wo_ref[...])
        yb = y_ref[...]
        ob = o_ref[...]
        ms = jnp.concatenate([_dot(yb, wsp_ref[s]) for s in range(4)], axis=1)
        ma = _dot(ob, wap_ref[...])
        ss = _sigmoid(g_ref[:, :d])
        sa = _sigmoid(g_ref[:, d:])
        dg_ref[:, :d] = (dm * ms * ss * (1.0 - ss)).astype(BF16)
        dg_ref[:, d:] = (dm * ma * sa * (1.0 - sa)).astype(BF16)
        dms = (dm * ss).astype(BF16)
        dma = (dm * sa).astype(BF16)
        dms_ref[...] = dms
        dma_ref[...] = dma
        dy = _dot_nt(dms[:, :ns], wsp_ref[0])
        for s in range(1, 4):
            dy += _dot_nt(dms[:, s * ns:(s + 1) * ns], wsp_ref[s])
        dy_ref[...] = dy
        do = _dot_nt(dma, wap_ref[...])
        do_ref[...] = do.astype(BF16)
        dl_ref[...] = _split_dot(do * ob.astype(F32), sel_ref[...])

    return pl.pallas_call(
        body, name="merge_bwd", grid=(lp // tm,),
        in_specs=[_row(tm, d), _row(tm, w), _row(tm, d), _row(tm, 2 * d),
                  _full((4, w, ns)), _full((d, d)), _full((d, d)), _full((d, nsel))],
        out_specs=[_row(tm, 2 * d), _row(tm, d), _row(tm, d), _row(tm, w), _row(tm, d), _row(tm, nsel), _row(tm, d)],
        out_shape=[jax.ShapeDtypeStruct((lp, 2 * d), BF16), jax.ShapeDtypeStruct((lp, d), BF16),
                   jax.ShapeDtypeStruct((lp, d), BF16), jax.ShapeDtypeStruct((lp, w), F32),
                   jax.ShapeDtypeStruct((lp, d), BF16), jax.ShapeDtypeStruct((lp, nsel), F32),
                   jax.ShapeDtypeStruct((lp, d), BF16)],
        compiler_params=_params(("parallel",)),
    )(dh1, yssm, o, gates, wsp4, wap, wo, sel)


def _mlp_in(h1, g, w4, tm):
    lp, d = h1.shape

    def body(x_ref, g_ref, w_ref, r_ref, h_ref):
        h = _rms(x_ref[...], g_ref[...]).astype(BF16)
        h_ref[...] = h
        for s in range(4):
            r_ref[:, s * d:(s + 1) * d] = jnp.maximum(_dot(h, w_ref[s]), 0.0).astype(BF16)

    return pl.pallas_call(
        body, name="mlp_in", grid=(lp // tm,),
        in_specs=[_row(tm, d), _full((1, d)), _full((4, d, d))],
        out_specs=[_row(tm, 4 * d), _row(tm, d)],
        out_shape=[jax.ShapeDtypeStruct((lp, 4 * d), BF16), jax.ShapeDtypeStruct((lp, d), BF16)],
        compiler_params=_params(("parallel",)),
    )(h1, g, w4)


def _square_bf16(r):
    rf = r.astype(F32)
    return (rf * rf).astype(BF16)


def _mlp_out(h1, r, w2, tm):
    lp, d = h1.shape
    ff = r.shape[1]

    def body(x_ref, r_ref, w_ref, o_ref):
        o_ref[...] = x_ref[...] + _dot(_square_bf16(r_ref[...]), w_ref[...])

    return pl.pallas_call(
        body, name="mlp_out", grid=(lp // tm,),
        in_specs=[_row(tm, d), _row(tm, ff), _full((ff, d))],
        out_specs=_row(tm, d), out_shape=jax.ShapeDtypeStruct((lp, d), F32),
        compiler_params=_params(("parallel",)),
    )(h1, r, w2)


def _final_loss(h3, g, tgt, rowmask, tm):
    lp, d = h3.shape

    def body(x_ref, g_ref, t_ref, m_ref, loss_ref, dx_ref, dg_ref):
        first = pl.program_id(0) == 0
        x = x_ref[...]
        gg = g_ref[...]
        err = (_rms(x, gg) - t_ref[...]) * m_ref[...]
        part = 0.5 * jnp.sum(jnp.sum(err * err, axis=1, keepdims=True), axis=0, keepdims=True) * (1.0 / d)
        part = jnp.broadcast_to(part, (SUBLANES, LANES))

        @pl.when(first)
        def _():
            loss_ref[...] = part

        @pl.when(jnp.logical_not(first))
        def _():
            loss_ref[...] += part

        dx, dgr = _rms_bwd(err * (1.0 / d), x, gg)
        dx_ref[...] = dx
        _acc_rows(dg_ref, dgr, first)

    return pl.pallas_call(
        body, name="final_loss", grid=(lp // tm,),
        in_specs=[_row(tm, d), _full((1, d)), _row(tm, d), _row(tm, 1)],
        out_specs=[_full((SUBLANES, LANES)), _row(tm, d), _full((1, d))],
        out_shape=[jax.ShapeDtypeStruct((SUBLANES, LANES), F32), jax.ShapeDtypeStruct((lp, d), F32),
                   jax.ShapeDtypeStruct((1, d), F32)],
        compiler_params=_params(("arbitrary",)),
    )(h3, g, tgt, rowmask)


def _mlp_bwd_a(dh3, r, w2, tm):
    lp, d = dh3.shape
    ff = r.shape[1]

    def body(dh_ref, r_ref, w_ref, dz_ref, dhb_ref):
        dhb = dh_ref[...].astype(BF16)
        dhb_ref[...] = dhb
        da = _dot_nt(dhb, w_ref[...])
        dz_ref[...] = (da * (2.0 * r_ref[...].astype(F32))).astype(BF16)

    return pl.pallas_call(
        body, name="mlp_bwd_a", grid=(lp // tm,),
        in_specs=[_row(tm, d), _row(tm, ff), _full((ff, d))],
        out_specs=[_row(tm, ff), _row(tm, d)],
        out_shape=[jax.ShapeDtypeStruct((lp, ff), BF16), jax.ShapeDtypeStruct((lp, d), BF16)],
        compiler_params=_params(("parallel",)),
    )(dh3, r, w2)


def _mlp_bwd_b(dz, dh3, h1, g, w4, tm):
    lp, d = h1.shape

    def body(dz_ref, dh_ref, x_ref, g_ref, w_ref, dx_ref, dg_ref):
        first = pl.program_id(0) == 0
        dh2 = _dot_nt(dz_ref[:, :d], w_ref[0])
        for s in range(1, 4):
            dh2 += _dot_nt(dz_ref[:, s * d:(s + 1) * d], w_ref[s])
        dx, dgr = _rms_bwd(dh2, x_ref[...], g_ref[...])
        dx_ref[...] = dh_ref[...] + dx
        _acc_rows(dg_ref, dgr, first)

    return pl.pallas_call(
        body, name="mlp_bwd_b", grid=(lp // tm,),
        in_specs=[_row(tm, 4 * d), _row(tm, d), _row(tm, d), _full((1, d)), _full((4, d, d))],
        out_specs=[_row(tm, d), _full((1, d))],
        out_shape=[jax.ShapeDtypeStruct((lp, d), F32), jax.ShapeDtypeStruct((1, d), F32)],
        compiler_params=_params(("arbitrary",)),
    )(dz, dh3, h1, g, w4)


def _in_proj_bwd(dyv, du0, du1, dskip, dqkv, dgates, dres, xin, g, w4, tm):
    lp, d = xin.shape
    hd = d // 2

    def body(dy_ref, a_ref, b_ref, ds_ref, dq_ref, dgt_ref, dr_ref, x_ref, g_ref, w_ref, dx_ref, dg_ref, dp_ref):
        first = pl.program_id(0) == 0
        du = (dy_ref[...] * ds_ref[...] + a_ref[...] + b_ref[...]).astype(BF16)
        dq = dq_ref[...].astype(BF16)
        dgt = dgt_ref[...].astype(BF16)
        dp_ref[:, :hd] = du
        dp_ref[:, hd:2 * d] = dq
        dp_ref[:, 2 * d:] = dgt
        dh = _dot_nt(du, w_ref[0, :, :hd]) + _dot_nt(dq[:, :hd], w_ref[0, :, hd:])
        dh += _dot_nt(dq[:, hd:], w_ref[1])
        dh += _dot_nt(dgt[:, :d], w_ref[2]) + _dot_nt(dgt[:, d:], w_ref[3])
        dx, dgr = _rms_bwd(dh, x_ref[...], g_ref[...])
        dx_ref[...] = dr_ref[...] + dx
        _acc_rows(dg_ref, dgr, first)

    return pl.pallas_call(
        body, name="in_proj_bwd", grid=(lp // tm,),
        in_specs=[_row(tm, hd), _row(tm, hd), _row(tm, hd), _full((1, hd)), _row(tm, 3 * hd), _row(tm, 2 * d),
                  _row(tm, d), _row(tm, d), _full((1, d)), _full((4, d, d))],
        out_specs=[_row(tm, d), _full((1, d)), _row(tm, 4 * d)],
        out_shape=[jax.ShapeDtypeStruct((lp, d), F32), jax.ShapeDtypeStruct((1, d), F32),
                   jax.ShapeDtypeStruct((lp, 4 * d), BF16)],
        compiler_params=_params(("arbitrary",)),
    )(dyv, du0, du1, dskip, dqkv, dgates, dres, xin, g, w4)


def _wgrad(a, dy, nshard, tm, tn, name, square=False):
    lp, k = a.shape
    n = dy.shape[1]
    ns = n // nshard
    assert ns % tn == 0
    per = ns // tn

    def body(a_ref, dy_ref, o_ref):
        i = pl.program_id(1)
        acc = _dot_tn(_square_bf16(a_ref[...]) if square else a_ref[...], dy_ref[...])

        @pl.when(i == 0)
        def _():
            o_ref[0] = acc

        @pl.when(i != 0)
        def _():
            o_ref[0] += acc

    return pl.pallas_call(
        body, name=name, grid=(n // tn, lp // tm),
        in_specs=[pl.BlockSpec((tm, k), lambda j, i: (i, 0)), pl.BlockSpec((tm, tn), lambda j, i: (i, j))],
        out_specs=pl.BlockSpec((1, k, tn), lambda j, i: (j // per, 0, j % per)),
        out_shape=jax.ShapeDtypeStruct((nshard, k, ns), F32),
        compiler_params=_params(("parallel", "arbitrary")),
    )(a, dy)


def _head_tables(d):
    idx = np.arange(LANES)
    mean = (idx[:, None] // HEAD_DIM == idx[None, :] // HEAD_DIM).astype(np.float32) / HEAD_DIM
    n_heads = d // HEAD_DIM
    kvh = n_heads // GQA_REP
    c = np.arange(d)
    col = np.arange(kvh * LANES)
    head_of_col = (col // LANES) * GQA_REP + (col % LANES)
    sel = ((c[:, None] // HEAD_DIM == head_of_col[None, :]) & ((col % LANES) < GQA_REP)[None, :]).astype(np.float32)
    return jnp.asarray(mean, BF16), jnp.asarray(sel, BF16)


def _swap_pairs(y):
    lane = lax.broadcasted_iota(jnp.int32, y.shape, 1)
    return jnp.where(lane % 2 == 0, pltpu.roll(y, LANES - 1, 1), pltpu.roll(y, 1, 1))


def _qk_prep(qkv, cos_t, sin_t, qg, kg, mean_m, tm):
    lp, wq = qkv.shape
    d = wq * 2 // 3
    kvw = d // 4
    kvh = kvw // HEAD_DIM
    scale = HEAD_DIM ** -0.5

    def body(x_ref, c_ref, s_ref, qg_ref, kg_ref, m_ref, q_ref, k_ref, v_ref):
        cs, sn, mm = c_ref[...], s_ref[...], m_ref[...]
        for b in range((d + kvw) // LANES):
            x = x_ref[:, b * LANES:(b + 1) * LANES]
            gg = qg_ref[...] if b < d // LANES else kg_ref[...]
            y = x * lax.rsqrt(_split_dot(x * x, mm) + NORM_EPS) * gg
            out = y * cs + _swap_pairs(y) * sn
            if b < d // LANES:
                q_ref[:, b * LANES:(b + 1) * LANES] = (out * scale).astype(BF16)
            else:
                kb = b - d // LANES
                k_ref[2 * kb] = out[:, :HEAD_DIM].astype(BF16)
                k_ref[2 * kb + 1] = out[:, HEAD_DIM:].astype(BF16)
        ones = jnp.ones((tm, LANES - HEAD_DIM), BF16)
        for h in range(kvh):
            vh = x_ref[:, d + kvw + h * HEAD_DIM:d + kvw + (h + 1) * HEAD_DIM].astype(BF16)
            v_ref[h] = jnp.concatenate([vh, ones], axis=1)

    k_spec = pl.BlockSpec((kvh, tm, HEAD_DIM), lambda i: (0, i, 0))
    v_spec = pl.BlockSpec((kvh, tm, LANES), lambda i: (0, i, 0))
    return pl.pallas_call(
        body, name="qk_prep", grid=(lp // tm,),
        in_specs=[_row(tm, wq), _row(tm, LANES), _row(tm, LANES), _full((1, LANES)), _full((1, LANES)),
                  _full((LANES, LANES))],
        out_specs=[_row(tm, d), k_spec, v_spec],
        out_shape=[jax.ShapeDtypeStruct((lp, d), BF16), jax.ShapeDtypeStruct((kvh, lp, HEAD_DIM), BF16),
                   jax.ShapeDtypeStruct((kvh, lp, LANES), BF16)],
        compiler_params=_params(("parallel",)),
    )(qkv, cos_t, sin_t, qg, kg, mean_m)


def _qk_bwd(qkv, dq, dk, dv, cos_t, sin_t, qg, kg, mean_m, tm):
    lp, wq = qkv.shape
    d = wq * 2 // 3
    kvw = d // 4
    kvh = kvw // HEAD_DIM
    scale = HEAD_DIM ** -0.5

    def body(x_ref, dq_ref, dk_ref, dv_ref, c_ref, s_ref, qg_ref, kg_ref, m_ref, o_ref, dqg_ref, dkg_ref):
        first = pl.program_id(0) == 0
        cs, sn, mm = c_ref[...], s_ref[...], m_ref[...]
        sums = [None, None]
        for b in range((d + kvw) // LANES):
            is_q = b < d // LANES
            x = x_ref[:, b * LANES:(b + 1) * LANES]
            gg = qg_ref[...] if is_q else kg_ref[...]
            r = lax.rsqrt(_split_dot(x * x, mm) + NORM_EPS)
            nrm = x * r
            if is_q:
                dout = dq_ref[:, b * LANES:(b + 1) * LANES] * scale
            else:
                kb = b - d // LANES
                dout = jnp.concatenate([dk_ref[2 * kb], dk_ref[2 * kb + 1]], axis=1)
            dy = dout * cs + _swap_pairs(dout * sn)
            part = jnp.sum(dy * nrm, axis=0, keepdims=True)
            sums[0 if is_q else 1] = part if sums[0 if is_q else 1] is None else sums[0 if is_q else 1] + part
            dn = dy * gg
            o_ref[:, b * LANES:(b + 1) * LANES] = (r * (dn - nrm * _split_dot(dn * nrm, mm))).astype(BF16)
        for h in range(kvh):
            o_ref[:, d + kvw + h * HEAD_DIM:d + kvw + (h + 1) * HEAD_DIM] = dv_ref[h].astype(BF16)
        for ref, s in ((dqg_ref, sums[0]), (dkg_ref, sums[1])):
            s = s + pltpu.roll(s, HEAD_DIM, 1)

            @pl.when(first)
            def _(ref=ref, s=s):
                ref[...] = s

            @pl.when(jnp.logical_not(first))
            def _(ref=ref, s=s):
                ref[...] += s

    kv_spec = pl.BlockSpec((kvh, tm, HEAD_DIM), lambda i: (0, i, 0))
    return pl.pallas_call(
        body, name="qk_bwd", grid=(lp // tm,),
        in_specs=[_row(tm, wq), _row(tm, d), kv_spec, kv_spec, _row(tm, LANES), _row(tm, LANES),
                  _full((1, LANES)), _full((1, LANES)), _full((LANES, LANES))],
        out_specs=[_row(tm, wq), _full((1, LANES)), _full((1, LANES))],
        out_shape=[jax.ShapeDtypeStruct((lp, wq), BF16), jax.ShapeDtypeStruct((1, LANES), F32),
                   jax.ShapeDtypeStruct((1, LANES), F32)],
        compiler_params=_params(("arbitrary",)),
    )(qkv, dq, dk, dv, cos_t, sin_t, qg, kg, mean_m)


def _attn_fwd(q, k, v, kbias, tq, tk, gather=()):
    lp, d = q.shape
    kvh = k.shape[0]
    rw = GQA_REP * HEAD_DIM
    nk = lp // tk

    ng = len(gather)
    steps = kvh * (lp // tq) * nk

    def body(*refs):
        q_ref, k_ref, v_ref, kb_ref = refs[:4]
        o_ref, lse_ref, pt_ref, mb_ref = refs[4 + ng:8 + ng]
        m_s, acc_s = refs[8 + 2 * ng:10 + 2 * ng]
        j = pl.program_id(2)

        if ng:
            phases = _gather_phases(refs[4:4 + ng], refs[8 + ng:8 + 2 * ng], *refs[10 + 2 * ng:])
            step = (pl.program_id(0) * (lp // tq) + pl.program_id(1)) * nk + j
            for n, phase in enumerate(phases):
                pl.when(step == n * steps // 3)(phase)

        @pl.when(j == 0)
        def _():
            m_s[...] = jnp.full(m_s.shape, MASK_VALUE, F32)
            acc_s[...] = jnp.zeros(acc_s.shape, F32)

        def heads(masked):
            kk, vv = k_ref[0], v_ref[0]

            def scores(h):
                return _dot_nt(q_ref[:, h * HEAD_DIM:(h + 1) * HEAD_DIM], kk)

            def softmax(h, s):
                if masked:
                    s = jnp.concatenate([s[:, :tk - CHUNK], s[:, tk - CHUNK:] + kb_ref[:, tk - CHUNK:]], axis=1)
                m_prev = m_s[h]
                m_new = jnp.maximum(m_prev, jnp.max(s, axis=1, keepdims=True))
                m_s[h] = m_new
                p = jnp.exp(s - m_new[:, :1]).astype(BF16)
                pt_ref[h] = p
                return p, jnp.exp(m_prev - m_new), m_new

            def accumulate(h, p, alpha):
                acc_s[h] = acc_s[h] * alpha + _dot(p, vv)

            ss = [scores(h) for h in range(GQA_REP)]
            pa = [softmax(h, ss[h]) for h in range(GQA_REP)]
            for h in range(GQA_REP):
                accumulate(h, *pa[h][:2])
            lane = lax.broadcasted_iota(jnp.int32, (tq, LANES), 1)
            mb = jnp.zeros((tq, LANES), F32)
            for h in range(GQA_REP):
                mb = jnp.where(lane == h, pa[h][2], mb)
            mb_ref[0] = mb

        pl.when(j != nk - 1)(functools.partial(heads, False))
        pl.when(j == nk - 1)(functools.partial(heads, True))

        @pl.when(j == nk - 1)
        def _():
            lane = lax.broadcasted_iota(jnp.int32, (tq, LANES), 1)
            lse = jnp.zeros((tq, LANES), F32)
            outs = []
            for h in range(GQA_REP):
                acc = acc_s[h]
                den = pltpu.roll(acc, HEAD_DIM, 1)
                outs.append((acc / den)[:, :HEAD_DIM])
                lse = jnp.where(lane == h, m_s[h] + jnp.log(den), lse)
            o_ref[...] = jnp.concatenate(outs, axis=1).astype(BF16)
            lse_ref[...] = lse

    sems = [pltpu.SemaphoreType.DMA((ng, 7)), pltpu.SemaphoreType.DMA((ng, 7)), pltpu.SemaphoreType.DMA((ng,))]
    res = pl.pallas_call(
        body, name="attn_fwd", grid=(kvh, lp // tq, nk),
        in_specs=[pl.BlockSpec((tq, rw), lambda g, i, j: (i, g)),
                  pl.BlockSpec((1, tk, HEAD_DIM), lambda g, i, j: (g, j, 0)),
                  pl.BlockSpec((1, tk, LANES), lambda g, i, j: (g, j, 0)),
                  pl.BlockSpec((1, tk), lambda g, i, j: (0, j))] + [_ANY] * ng,
        out_specs=[pl.BlockSpec((tq, rw), lambda g, i, j: (i, g)),
                   pl.BlockSpec((tq, LANES), lambda g, i, j: (i, g)),
                   pl.BlockSpec((GQA_REP, tq, tk), lambda g, i, j: (g, i, j)),
                   pl.BlockSpec((1, tq, LANES), lambda g, i, j: (j, i, g))] + [_ANY] * ng,
        out_shape=[jax.ShapeDtypeStruct((lp, d), BF16), jax.ShapeDtypeStruct((lp, kvh * LANES), F32),
                   jax.ShapeDtypeStruct((kvh * GQA_REP, lp, lp), BF16), jax.ShapeDtypeStruct((nk, lp, kvh * LANES), F32)]
        + [jax.ShapeDtypeStruct((8,) + b.shape, b.dtype) for b in gather],
        scratch_shapes=[pltpu.VMEM((GQA_REP, tq, LANES), F32), pltpu.VMEM((GQA_REP, tq, LANES), F32)]
        + (sems if ng else []),
        compiler_params=_params(("arbitrary", "arbitrary", "arbitrary")),
    )(q, k, v, kbias, *gather)
    return res[0], res[1], res[2], res[3], list(res[4:])


def _attn_bwd(q, k, v, pt, mblk, do, lse, delta, tq, tk):
    lp, d = q.shape
    kvh = k.shape[0]
    rw = GQA_REP * HEAD_DIM
    nq = lp // tq

    def body(q_ref, k_ref, v_ref, pt_ref, mb_ref, do_ref, lse_ref, dl_ref, dq_ref, dk_ref, dv_ref, dk_s, dv_s):
        j = pl.program_id(1)
        i = pl.program_id(2)

        @pl.when(jnp.logical_and(i == 0, j == 0))
        def _():
            dq_ref[...] = jnp.zeros(dq_ref.shape, F32)

        @pl.when(i == 0)
        def _():
            dk_s[...] = jnp.zeros(dk_s.shape, F32)
            dv_s[...] = jnp.zeros(dv_s.shape, F32)

        kk, vv = k_ref[0], v_ref[0][:, :HEAD_DIM]
        scale = jnp.exp(mb_ref[0] - lse_ref[...])
        dl = dl_ref[...] * scale
        dqs = []
        for pair in ((0, 1), (2, 3)):
            dos = {h: (do_ref[:, h * HEAD_DIM:(h + 1) * HEAD_DIM].astype(F32) * scale[:, h:h + 1]).astype(BF16)
                   for h in pair}
            dps = {h: _dot_nt(dos[h], vv) for h in pair}
            for h in pair:
                dv_s[...] += _dot_tn(pt_ref[h], dos[h])
            dss = {h: pt_ref[h] * (dps[h] - dl[:, h:h + 1]).astype(BF16) for h in pair}
            for h in pair:
                dk_s[...] += _dot_tn(dss[h], q_ref[:, h * HEAD_DIM:(h + 1) * HEAD_DIM])
                dqs.append(_dot(dss[h], kk))
        rows = pl.ds(pl.multiple_of(i * tq, tq), tq)
        dq_ref[rows, :] += jnp.concatenate(dqs, axis=1)

        @pl.when(i == nq - 1)
        def _():
            dk_ref[0] = dk_s[...]
            dv_ref[0] = dv_s[...]

    return pl.pallas_call(
        body, name="attn_bwd", grid=(kvh, lp // tk, nq),
        in_specs=[pl.BlockSpec((tq, rw), lambda g, j, i: (i, g)),
                  pl.BlockSpec((1, tk, HEAD_DIM), lambda g, j, i: (g, j, 0)),
                  pl.BlockSpec((1, tk, LANES), lambda g, j, i: (g, j, 0)),
                  pl.BlockSpec((GQA_REP, tq, tk), lambda g, j, i: (g, i, j)),
                  pl.BlockSpec((1, tq, LANES), lambda g, j, i: (j, i, g)),
                  pl.BlockSpec((tq, rw), lambda g, j, i: (i, g)),
                  pl.BlockSpec((tq, LANES), lambda g, j, i: (i, g)),
                  pl.BlockSpec((tq, LANES), lambda g, j, i: (i, g))],
        out_specs=[pl.BlockSpec((lp, rw), lambda g, j, i: (0, g)),
                   pl.BlockSpec((1, tk, HEAD_DIM), lambda g, j, i: (g, j, 0)),
                   pl.BlockSpec((1, tk, HEAD_DIM), lambda g, j, i: (g, j, 0))],
        out_shape=[jax.ShapeDtypeStruct((lp, d), F32), jax.ShapeDtypeStruct((kvh, lp, HEAD_DIM), F32),
                   jax.ShapeDtypeStruct((kvh, lp, HEAD_DIM), F32)],
        scratch_shapes=[pltpu.VMEM((tk, HEAD_DIM), F32), pltpu.VMEM((tk, HEAD_DIM), F32)],
        compiler_params=_params(("parallel", "arbitrary", "arbitrary")),
    )(q, k, v, pt, mblk, do, lse, delta)


def _ssm_math(a_re, a_im, log_dt, bt_re, bt_im):
    dt = jnp.exp(log_dt)
    lam_re = jnp.minimum(a_re, EIG_RE_MAX)
    lam_im = a_im
    mag = jnp.exp(lam_re * dt)
    ang = lam_im * dt
    lb_re = mag * jnp.cos(ang)
    lb_im = mag * jnp.sin(ang)
    num_re = lb_re - 1.0
    num_im = lb_im
    den = lam_re * lam_re + lam_im * lam_im
    f_re = (num_re * lam_re + num_im * lam_im) / den
    f_im = (num_im * lam_re - num_re * lam_im) / den
    bb_re = f_re[:, None, :] * bt_re - f_im[:, None, :] * bt_im
    bb_im = f_re[:, None, :] * bt_im + f_im[:, None, :] * bt_re
    return lb_re, lb_im, bb_re, bb_im


def _ssm_discretize(a_re, a_im, log_dt, bt_re, bt_im):
    nd, g, n = a_re.shape
    p = bt_re.shape[2]

    def body(ar_ref, ai_ref, ld_ref, br_ref, bi_ref, bbr_ref, bbi_ref, pr_ref, pi_ref, hr_ref, hi_ref):
        lb_re, lb_im, bb_re, bb_im = _ssm_math(ar_ref[0], ai_ref[0], ld_ref[0], br_ref[0], bi_ref[0])
        bbr_ref[0] = bb_re
        bbi_ref[0] = bb_im
        cr, ci = lb_re, lb_im
        for k in range(KSTEPS):
            pr_ref[0, k] = cr
            pi_ref[0, k] = ci
            if k < KSTEPS - 1:
                cr, ci = cr * lb_re - ci * lb_im, cr * lb_im + ci * lb_re
        for t in range(2):
            cr, ci = cr * cr - ci * ci, 2.0 * cr * ci
            hr_ref[0, t] = cr
            hi_ref[0, t] = ci

    s3 = pl.BlockSpec((1, g, n), lambda i: (i, 0, 0))
    s4 = pl.BlockSpec((1, g, p, n), lambda i: (i, 0, 0, 0))
    sp = pl.BlockSpec((1, KSTEPS, g, n), lambda i: (i, 0, 0, 0))
    sh = pl.BlockSpec((1, 2, g, n), lambda i: (i, 0, 0, 0))
    return pl.pallas_call(
        body, name="ssm_discretize", grid=(nd,),
        in_specs=[s3, s3, pl.BlockSpec((1, g, 1), lambda i: (i, 0, 0)), s4, s4],
        out_specs=[s4, s4, sp, sp, sh, sh],
        out_shape=[jax.ShapeDtypeStruct((nd, g, p, n), F32)] * 2 + [jax.ShapeDtypeStruct((nd, KSTEPS, g, n), F32)] * 2
        + [jax.ShapeDtypeStruct((nd, 2, g, n), F32)] * 2,
        compiler_params=_params(("parallel",)),
    )(a_re, a_im, log_dt, bt_re, bt_im)


def _ssm_param_bwd(a_re, a_im, log_dt, bt_re, bt_im, dlb_re, dlb_im, dbb_re, dbb_im):
    nd, g, n = a_re.shape
    p = bt_re.shape[2]

    def body(ar_ref, ai_ref, ld_ref, br_ref, bi_ref, c0_ref, c1_ref, c2_ref, c3_ref,
             o0_ref, o1_ref, o2_ref, o3_ref, o4_ref):
        _, vjp = jax.vjp(_ssm_math, ar_ref[0], ai_ref[0], ld_ref[0], br_ref[0], bi_ref[0])
        outs = vjp((c0_ref[0], c1_ref[0], c2_ref[0], c3_ref[0]))
        for ref, val in zip((o0_ref, o1_ref, o2_ref, o3_ref, o4_ref), outs):
            ref[0] = val

    s3 = pl.BlockSpec((1, g, n), lambda i: (i, 0, 0))
    s1 = pl.BlockSpec((1, g, 1), lambda i: (i, 0, 0))
    s4 = pl.BlockSpec((1, g, p, n), lambda i: (i, 0, 0, 0))
    return pl.pallas_call(
        body, name="ssm_param_bwd", grid=(nd,),
        in_specs=[s3, s3, s1, s4, s4, s3, s3, s4, s4],
        out_specs=[s3, s3, s1, s4, s4],
        out_shape=[jax.ShapeDtypeStruct((nd, g, n), F32)] * 2 + [jax.ShapeDtypeStruct((nd, g, 1), F32)]
        + [jax.ShapeDtypeStruct((nd, g, p, n), F32)] * 2,
        compiler_params=_params(("parallel",)),
    )(a_re, a_im, log_dt, bt_re, bt_im, dlb_re, dlb_im, dbb_re, dbb_im)


def _cmul(ar, ai, xr, xi, conj):
    if conj:
        return ar * xr + ai * xi, ar * xi - ai * xr
    return ar * xr - ai * xi, ar * xi + ai * xr


def _scan_chunk(buf, tab, carry, ein, nj, rev, conj, base=0):
    ks = list(range(KSTEPS))
    if rev:
        ks = ks[::-1]
    sub = lax.broadcasted_iota(jnp.int32, (SUBLANES, SCAN_LANES), 0)
    edge = sub == (SUBLANES - 1 if rev else 0)

    def step(j, _):
        jr, ji = j, nj + j
        ar, ai = tab[base, jr], tab[base, ji]
        hr = jnp.zeros((SUBLANES, SCAN_LANES), F32)
        hi = jnp.zeros((SUBLANES, SCAN_LANES), F32)
        for k in ks:
            rows = pl.ds(k * SUBLANES, SUBLANES)
            pr, pi_ = _cmul(ar, ai, hr, hi, conj)
            hr = pr + buf[jr, rows, :]
            hi = pi_ + buf[ji, rows, :]
            buf[jr, rows, :] = hr
            buf[ji, rows, :] = hi
        shift = SUBLANES - 1 if rev else 1
        er = jnp.where(edge, carry[jr], pltpu.roll(hr, shift, 0))
        ei = jnp.where(edge, carry[ji], pltpu.roll(hi, shift, 0))
        for t, dist in enumerate((1, 2, 4)):
            sh = SUBLANES - dist if rev else dist
            pr, pi_ = _cmul(tab[base + 1 + t, jr], tab[base + 1 + t, ji], pltpu.roll(er, sh, 0), pltpu.roll(ei, sh, 0), conj)
            er, ei = er + pr, ei + pi_
        ein[jr] = er
        ein[ji] = ei
        pr, pi_ = _cmul(tab[base + 4 + KSTEPS - 1, jr], tab[base + 4 + KSTEPS - 1, ji], er, ei, conj)
        last = 0 if rev else SUBLANES - 1
        carry[jr] = jnp.broadcast_to((hr + pr)[last:last + 1, :], (SUBLANES, SCAN_LANES))
        carry[ji] = jnp.broadcast_to((hi + pi_)[last:last + 1, :], (SUBLANES, SCAN_LANES))
        for n, k in enumerate(ks):
            rows = pl.ds(k * SUBLANES, SUBLANES)
            pr, pi_ = _cmul(tab[base + 4 + n, jr], tab[base + 4 + n, ji], er, ei, conj)
            buf[jr, rows, :] += pr
            buf[ji, rows, :] += pi_
        return 0

    lax.fori_loop(0, nj, step, 0)


def _state_lanes(b):
    per = SCAN_LANES // SSM_BLOCK
    return b // per, slice((b % per) * SSM_BLOCK, (b % per + 1) * SSM_BLOCK)


def _project_in(src, w_ref, buf, nj):
    nb, cb, _ = w_ref.shape
    for b in range(nb):
        res = _dot(src[:, b * cb:(b + 1) * cb], w_ref[b])
        j, lanes = _state_lanes(b)
        buf[j, :, lanes] = res[:, :SSM_BLOCK]
        buf[nj + j, :, lanes] = res[:, SSM_BLOCK:]


def _state_block(buf, b, nj):
    j, lanes = _state_lanes(b)
    return jnp.concatenate([buf[j, :, lanes], buf[nj + j, :, lanes]], axis=1).astype(BF16)


def _project_out(buf, w_ref, nj):
    return jnp.concatenate([_dot_nt(_state_block(buf, b, nj), w_ref[b]) for b in range(w_ref.shape[0])], axis=1)


def _ssm_fwd(u, wb, wct, tab, rev, name):
    lp, w = u.shape
    nb, cb, _ = wb.shape
    nj = nb * SSM_BLOCK // SCAN_LANES
    nc = lp // CHUNK
    ntab = tab.shape[0]
    cidx = (lambda c: nc - 1 - c) if rev else (lambda c: c)

    def body(u_ref, wb_ref, wct_ref, tab_ref, y_ref, ck_ref, buf, carry, ein):
        @pl.when(pl.program_id(0) == 0)
        def _():
            carry[...] = jnp.zeros(carry.shape, F32)

        _project_in(u_ref[...].astype(BF16), wb_ref, buf, nj)
        ck_ref[0] = carry[...]
        _scan_chunk(buf, tab_ref, carry, ein, nj, rev, False)
        y_ref[...] = _project_out(buf, wct_ref, nj)

    wshape = (nb, cb, 2 * SSM_BLOCK)
    return pl.pallas_call(
        body, name=name, grid=(nc,),
        in_specs=[pl.BlockSpec((CHUNK, w), lambda c: (cidx(c), 0)), _full(wshape), _full(wshape),
                  _full((ntab, 2 * nj, SUBLANES, SCAN_LANES))],
        out_specs=[pl.BlockSpec((CHUNK, w), lambda c: (cidx(c), 0)),
                   pl.BlockSpec((1, 2 * nj, SUBLANES, SCAN_LANES), lambda c: (cidx(c), 0, 0, 0))],
        out_shape=[jax.ShapeDtypeStruct((lp, w), F32), jax.ShapeDtypeStruct((nc, 2 * nj, SUBLANES, SCAN_LANES), F32)],
        scratch_shapes=[pltpu.VMEM((2 * nj, CHUNK, SCAN_LANES), F32), pltpu.VMEM((2 * nj, SUBLANES, SCAN_LANES), F32),
                        pltpu.VMEM((2 * nj, SUBLANES, SCAN_LANES), F32)],
        compiler_params=_params(("arbitrary",)),
    )(u, wb, wct, tab)


def _ssm_bwd(u, dy, ckpt, wb, wct, tab, rev, name, scatter=()):
    lp, w = u.shape
    nb, cb, _ = wb.shape
    nj = nb * SSM_BLOCK // SCAN_LANES
    nc = lp // CHUNK
    ntab = tab.shape[0]
    cidx = (lambda c: c) if rev else (lambda c: nc - 1 - c)

    ns = len(scatter)

    def body(*refs):
        u_ref, dy_ref, ck_ref, wb_ref, wct_ref, tab_hbm = refs[:6]
        du_ref, dbb_ref, dcc_ref, dlb_ref = refs[6 + ns:10 + ns]
        tab_ref, dwb_ref, dwc_ref, xs, ls, xcar, lcar, xin, lin = refs[10 + 2 * ns:19 + 2 * ns]
        c = pl.program_id(0)

        if ns:
            start, finish = _scatter_phases(refs[6:6 + ns], refs[10 + ns:10 + 2 * ns], *refs[19 + 2 * ns:])
            pl.when(c == 0)(start)
            pl.when(c == nc - 1)(finish)

        @pl.when(c == 0)
        def _():
            pltpu.sync_copy(tab_hbm, tab_ref)
            lcar[...] = jnp.zeros(lcar.shape, F32)
            dwb_ref[...] = jnp.zeros(dwb_ref.shape, F32)
            dwc_ref[...] = jnp.zeros(dwc_ref.shape, F32)
            dlb_ref[...] = jnp.zeros(dlb_ref.shape, F32)

        ub = u_ref[...].astype(BF16)
        dyb = dy_ref[...].astype(BF16)
        _project_in(ub, wb_ref, xs, nj)
        xcar[...] = ck_ref[0]
        _scan_chunk(xs, tab_ref, xcar, xin, nj, rev, False)
        _project_in(dyb, wct_ref, ls, nj)
        _scan_chunk(ls, tab_ref, lcar, lin, nj, not rev, True, base=ntab // 2)
        dus = []
        for b in range(nb):
            chans = slice(b * cb, (b + 1) * cb)
            xb = _state_block(xs, b, nj)
            lb = _state_block(ls, b, nj)
            dwc_ref[b] += _dot_tn(dyb[:, chans], xb)
            dwb_ref[b] += _dot_tn(ub[:, chans], lb)
            dus.append(_dot_nt(lb, wb_ref[b]))
        du_ref[...] = jnp.concatenate(dus, axis=1)

        def step(j, _):
            jr, ji = j, nj + j
            ar = jnp.zeros((SUBLANES, SCAN_LANES), F32)
            ai = jnp.zeros((SUBLANES, SCAN_LANES), F32)
            for k in range(KSTEPS):
                kp = k + 1 if rev else k - 1
                rows = pl.ds(k * SUBLANES, SUBLANES)
                if 0 <= kp < KSTEPS:
                    prow = pl.ds(kp * SUBLANES, SUBLANES)
                    xr, xi = xs[jr, prow, :], xs[ji, prow, :]
                else:
                    xr, xi = xin[jr], xin[ji]
                lr, li = ls[jr, rows, :], ls[ji, rows, :]
                ar += lr * xr + li * xi
                ai += li * xr - lr * xi
            dlb_ref[jr] += ar
            dlb_ref[ji] += ai
            return 0

        lax.fori_loop(0, nj, step, 0)

        @pl.when(c == nc - 1)
        def _():
            for b in range(2 * nj):
                dlb_ref[b] = jnp.broadcast_to(jnp.sum(dlb_ref[b], axis=0, keepdims=True), (SUBLANES, SCAN_LANES))
            for g in range(w // SSM_GROUP):
                b, gl = divmod(g, cb // SSM_GROUP)
                rows = slice(gl * SSM_GROUP, (gl + 1) * SSM_GROUP)
                for part in range(2):
                    cols = slice(part * SSM_BLOCK + gl * SSM_STATE, part * SSM_BLOCK + (gl + 1) * SSM_STATE)
                    dbb_ref[part, g * SSM_GROUP:(g + 1) * SSM_GROUP, :] = dwb_ref[b, rows, cols]
                    dcc_ref[part, g * SSM_GROUP:(g + 1) * SSM_GROUP, :] = dwc_ref[b, rows, cols]

    st = (2 * nj, SUBLANES, SCAN_LANES)
    wshape = (nb, cb, 2 * SSM_BLOCK)
    sems = [pltpu.SemaphoreType.DMA((ns, 3)), pltpu.SemaphoreType.DMA((ns, 3)), pltpu.SemaphoreType.DMA((ns,))]
    res = pl.pallas_call(
        body, name=name, grid=(nc,),
        in_specs=[pl.BlockSpec((CHUNK, w), lambda c: (cidx(c), 0)), pl.BlockSpec((CHUNK, w), lambda c: (cidx(c), 0)),
                  pl.BlockSpec((1,) + st, lambda c: (cidx(c), 0, 0, 0)), _full(wshape), _full(wshape), _ANY]
        + [_ANY] * ns,
        out_specs=[pl.BlockSpec((CHUNK, w), lambda c: (cidx(c), 0)), _full((2, w, SSM_STATE)),
                   _full((2, w, SSM_STATE)), _full(st)] + [_ANY] * ns,
        out_shape=[jax.ShapeDtypeStruct((lp, w), F32), jax.ShapeDtypeStruct((2, w, SSM_STATE), F32),
                   jax.ShapeDtypeStruct((2, w, SSM_STATE), F32), jax.ShapeDtypeStruct(st, F32)]
        + [jax.ShapeDtypeStruct(p.shape, p.dtype) for p in scatter],
        scratch_shapes=[pltpu.VMEM((ntab,) + st, F32), pltpu.VMEM(wshape, F32), pltpu.VMEM(wshape, F32),
                        pltpu.VMEM((2 * nj, CHUNK, SCAN_LANES), F32), pltpu.VMEM((2 * nj, CHUNK, SCAN_LANES), F32),
                        pltpu.VMEM(st, F32), pltpu.VMEM(st, F32), pltpu.VMEM(st, F32), pltpu.VMEM(st, F32)]
        + (sems if ns else []),
        compiler_params=_params(("arbitrary",)),
    )(u, dy, ckpt, wb, wct, tab, *scatter)
    return res[0], res[1], res[2], res[3], list(res[4:])


def _embed_blocks(t_re, t_im):
    g, p, n = t_re.shape
    gb = SSM_BLOCK // n
    eye = jnp.eye(gb, dtype=t_re.dtype)
    parts = [jnp.einsum('bgpn,gh->bgphn', t.reshape(g // gb, gb, p, n), eye).reshape(g // gb, gb * p, gb * n)
             for t in (t_re, t_im)]
    return jnp.concatenate(parts, axis=2)


def _scan_tables(pw_re, pw_im, hi_re, hi_im, rev):
    s = pw_re.shape[1] * pw_re.shape[2]
    nj = s // SCAN_LANES
    sub = np.arange(SUBLANES)
    live = np.ones((4 + KSTEPS, 1, SUBLANES, 1), bool)
    for row, dist in ((1, 1), (2, 2), (3, 4)):
        live[row, 0, :, 0] = (sub < SUBLANES - dist) if rev else (sub >= dist)

    def lay(pw, hi):
        rows = jnp.concatenate([pw[:1], pw[KSTEPS - 1:], hi, pw], axis=0).reshape(4 + KSTEPS, nj, 1, SCAN_LANES)
        return jnp.where(live, jnp.broadcast_to(rows, (4 + KSTEPS, nj, SUBLANES, SCAN_LANES)), 0.0)

    return jnp.concatenate([lay(pw_re, hi_re), lay(pw_im, hi_im)], axis=1)


def _adamw(w, g, m, v, tm):
    r, c = w.shape
    c1 = 1.0 - ADAM_B1 ** ADAM_STEP
    c2 = 1.0 - ADAM_B2 ** ADAM_STEP

    def body(w_ref, g_ref, m_ref, v_ref, d_ref, nm_ref, nv_ref):
        gg = g_ref[...]
        nm = ADAM_B1 * m_ref[...] + (1.0 - ADAM_B1) * gg
        nv = ADAM_B2 * v_ref[...] + (1.0 - ADAM_B2) * (gg * gg)
        nm_ref[...] = nm
        nv_ref[...] = nv
        d_ref[...] = -ADAM_LR * ((nm / c1) / (jnp.sqrt(nv / c2) + ADAM_EPS) + ADAM_WD * w_ref[...])

    spec = _row(tm, c)
    return pl.pallas_call(
        body, name="adamw", grid=(r // tm,), in_specs=[spec] * 4, out_specs=[spec] * 3,
        out_shape=[jax.ShapeDtypeStruct((r, c), F32)] * 3, compiler_params=_params(("parallel",)),
    )(w, g, m, v)


def _pair_sum(g42, got, core, out_dtype, tm, name):
    _, _, r, c = g42.shape

    def body(core_ref, a_ref, b_ref, o_ref):
        o_ref[...] = (a_ref[...] + b_ref[...]).astype(out_dtype)

    grid_spec = pltpu.PrefetchScalarGridSpec(
        num_scalar_prefetch=1, grid=(4, r // tm),
        in_specs=[pl.BlockSpec((1, None, tm, c), lambda s, i, core_ref: (s, core_ref[0], i, 0)),
                  pl.BlockSpec((1, tm, c), lambda s, i, core_ref: (s, i, 0))],
        out_specs=pl.BlockSpec((1, tm, c), lambda s, i, core_ref: (s, i, 0)))
    return pl.pallas_call(
        body, name=name, grid_spec=grid_spec, out_shape=jax.ShapeDtypeStruct((4, r, c), out_dtype),
        compiler_params=_params(("parallel", "parallel")),
    )(core, g42, got)


def _sum4(a, core, tm, name):
    _, r, c = a.shape

    def body(core_ref, a_ref, o_ref):
        o_ref[...] = ((a_ref[0].astype(F32) + a_ref[1].astype(F32)) + a_ref[2].astype(F32)) + a_ref[3].astype(F32)

    grid_spec = pltpu.PrefetchScalarGridSpec(
        num_scalar_prefetch=1, grid=(r // tm,),
        in_specs=[pl.BlockSpec((4, tm, c), lambda i, core_ref: (0, i, 0))],
        out_specs=pl.BlockSpec((None, tm, c), lambda i, core_ref: (core_ref[0], i, 0)))
    return pl.pallas_call(
        body, name=name, grid_spec=grid_spec, out_shape=jax.ShapeDtypeStruct((2, r, c), F32),
        compiler_params=_params(("parallel",)),
    )(core, a)


_ANY = pl.BlockSpec(memory_space=pl.ANY)


def _gather_phases(xs, outs, send_sems, recv_sems, local_sems):
    n = len(xs)

    def parts():
        x, y, c = lax.axis_index("x"), lax.axis_index("y"), lax.axis_index("c")
        return c, (x, y, c), (x, y, 1 - c), [(1 - x, y), (x, 1 - y), (1 - x, 1 - y)]

    def slot(t, px, py, pc):
        return outs[t].at[4 * px + 2 * py + pc]

    def copy(t, k, blk, to, src=None):
        return pltpu.make_async_remote_copy(
            src_ref=slot(t, *blk) if src is None else src, dst_ref=slot(t, *blk),
            send_sem=send_sems.at[t, k], recv_sem=recv_sems.at[t, k], device_id=to, device_id_type=MESH_ID)

    def own(t, me):
        return pltpu.make_async_copy(xs[t], slot(t, *me), local_sems.at[t])

    def first(t, c, me, sibling, chips):
        return [copy(t, 0, me, sibling, src=xs[t])] + [copy(t, 1 + j, me, (*chip, c), src=xs[t])
                                                       for j, chip in enumerate(chips)]

    def passed(t, c, sibling, chips):
        return [copy(t, 4 + j, (*chip, c), sibling) for j, chip in enumerate(chips)]

    def start():
        c, me, sibling, chips = parts()
        for t in range(n):
            own(t, me).start()
        for t in range(n):
            for cp in first(t, c, me, sibling, chips):
                cp.start()

    def forward():
        c, me, sibling, chips = parts()
        for j, chip in enumerate(chips):
            for t in range(n):
                copy(t, 1 + j, (*chip, c), me).wait_recv()
                passed(t, c, sibling, chips)[j].start()

    def finish():
        c, me, sibling, chips = parts()
        for t in range(n):
            copy(t, 0, sibling, me).wait_recv()
        for j, chip in enumerate(chips):
            for t in range(n):
                copy(t, 4 + j, (*chip, 1 - c), me).wait_recv()
        for t in range(n):
            for cp in first(t, c, me, sibling, chips) + passed(t, c, sibling, chips):
                cp.wait_send()
            own(t, me).wait()

    return start, forward, finish


def _all_gather8(blocks, name):
    n = len(blocks)

    def body(*refs):
        for phase in _gather_phases(refs[:n], refs[n:2 * n], *refs[2 * n:]):
            phase()

    return pl.pallas_call(
        body, name=name, out_shape=[jax.ShapeDtypeStruct((8,) + b.shape, b.dtype) for b in blocks],
        in_specs=[_ANY] * n, out_specs=[_ANY] * n,
        scratch_shapes=[pltpu.SemaphoreType.DMA((n, 7)), pltpu.SemaphoreType.DMA((n, 7)),
                        pltpu.SemaphoreType.DMA((n,))],
    )(*blocks)


def _pair_exchange(gs, name):
    n = len(gs)

    def body(*refs):
        g_refs, outs = refs[:n], refs[n:2 * n]
        send_sems, recv_sems = refs[2 * n:]
        x, y, c = lax.axis_index("x"), lax.axis_index("y"), lax.axis_index("c")
        cps = [pltpu.make_async_remote_copy(
            src_ref=g_refs[t].at[:, 1 - c], dst_ref=outs[t], send_sem=send_sems.at[t], recv_sem=recv_sems.at[t],
            device_id=(x, y, 1 - c), device_id_type=MESH_ID) for t in range(n)]
        for cp in cps:
            cp.start()
        for cp in cps:
            cp.wait()

    return pl.pallas_call(
        body, name=name,
        out_shape=[jax.ShapeDtypeStruct((g.shape[0],) + g.shape[2:], g.dtype) for g in gs],
        in_specs=[_ANY] * n, out_specs=[_ANY] * n,
        scratch_shapes=[pltpu.SemaphoreType.DMA((n,)), pltpu.SemaphoreType.DMA((n,))],
    )(*gs)


def _scatter_phases(p_refs, outs, send_sems, recv_sems, local_sems):
    n = len(p_refs)

    def parts():
        x, y, c = lax.axis_index("x"), lax.axis_index("y"), lax.axis_index("c")
        return c, 2 * x + y, [(1 - x, y), (x, 1 - y), (1 - x, 1 - y)]

    def copy(t, k, src_slab, dst_slab, chip, c):
        return pltpu.make_async_remote_copy(
            src_ref=p_refs[t].at[src_slab], dst_ref=outs[t].at[dst_slab], send_sem=send_sems.at[t, k],
            recv_sem=recv_sems.at[t, k], device_id=(*chip, c), device_id_type=MESH_ID)

    def own(t, mine):
        return pltpu.make_async_copy(p_refs[t].at[mine], outs[t].at[mine], local_sems.at[t])

    def start():
        c, mine, chips = parts()
        for t in range(n):
            own(t, mine).start()
        for k, (cx, cy) in enumerate(chips):
            for t in range(n):
                copy(t, k, 2 * cx + cy, mine, (cx, cy), c).start()

    def finish():
        c, mine, chips = parts()
        for k, (cx, cy) in enumerate(chips):
            for t in range(n):
                copy(t, k, mine, 2 * cx + cy, (cx, cy), c).wait_recv()
        for t in range(n):
            for k, (cx, cy) in enumerate(chips):
                copy(t, k, 2 * cx + cy, mine, (cx, cy), c).wait_send()
            own(t, mine).wait()

    return start, finish


def _chip_scatter(ps, name):
    n = len(ps)

    def body(*refs):
        for phase in _scatter_phases(refs[:n], refs[n:2 * n], *refs[2 * n:]):
            phase()

    return pl.pallas_call(
        body, name=name, out_shape=[jax.ShapeDtypeStruct(p.shape, p.dtype) for p in ps],
        in_specs=[_ANY] * n, out_specs=[_ANY] * n,
        scratch_shapes=[pltpu.SemaphoreType.DMA((n, 3)), pltpu.SemaphoreType.DMA((n, 3)),
                        pltpu.SemaphoreType.DMA((n,))],
    )(*ps)


def _pair_gather(rs, name):
    n = len(rs)

    def body(*refs):
        ins, outs = refs[:n], refs[n:2 * n]
        send_sems, recv_sems = refs[2 * n:]
        x, y, c = lax.axis_index("x"), lax.axis_index("y"), lax.axis_index("c")

        def copy(t, slab):
            return pltpu.make_async_remote_copy(
                src_ref=ins[t].at[slab], dst_ref=outs[t].at[slab], send_sem=send_sems.at[t],
                recv_sem=recv_sems.at[t], device_id=(x, y, 1 - c), device_id_type=MESH_ID)

        sends = [copy(t, c) for t in range(n)]
        for cp in sends:
            cp.start()
        for t in range(n):
            copy(t, 1 - c).wait_recv()
        for cp in sends:
            cp.wait_send()

    return pl.pallas_call(
        body, name=name, out_shape=[jax.ShapeDtypeStruct(r.shape, r.dtype) for r in rs],
        in_specs=[_ANY] * n, out_specs=[_ANY] * n, input_output_aliases={t: t for t in range(n)},
        scratch_shapes=[pltpu.SemaphoreType.DMA((n,)), pltpu.SemaphoreType.DMA((n,))],
    )(*rs)


PACK_COLS = 1024
BIG = (("meta_tokens", 1), ("w_in", 1), ("w_glu", 0), ("w_ssm_proj", 1), ("w_attn_proj", 0), ("w_out", 0),
       ("w_mlp_in", 1), ("w_mlp_out", 0))
SMALL = ("norm_mix_g", "ssm_a_re", "ssm_a_im", "ssm_log_dt", "ssm_b_re", "ssm_b_im", "ssm_c_re", "ssm_c_im",
         "ssm_d", "b_glu", "q_norm_g", "k_norm_g", "norm_mlp_g", "norm_final_g")


def _pad_rows(flat, mult_rows):
    n = flat.shape[0]
    unit = PACK_COLS * mult_rows
    total = -(-n // unit) * unit
    return jnp.pad(flat, (0, total - n)).reshape(total // PACK_COLS, PACK_COLS)


def _half(t, c):
    return lax.dynamic_slice_in_dim(t, c * (t.shape[0] // 2), t.shape[0] // 2, 0)


EARLY_WEIGHTS = ("meta_tokens", "w_in", "w_glu")
LATE_WEIGHTS = tuple(name for name, _ in BIG if name not in EARLY_WEIGHTS)


def _weight_blocks(shards, c, names):
    return [_half(shards[name], c) if name == "meta_tokens" else _half(shards[name], c).astype(BF16) for name in names]


def _shard_major(names, gathered):
    return {name: g.reshape((4, 2 * g.shape[1]) + g.shape[2:]) for name, g in zip(names, gathered)}


class _Reduction:
    def __init__(self, grads, wire, labels, c, tag):
        self.labels, self.wire, self.c, self.tag = labels, wire, c, tag
        self.core = c.astype(jnp.int32).reshape(1)
        g42 = [g.reshape(4, 2, g.shape[1] // 2, g.shape[2]) for g in grads]
        self.tiles = [_pick_tile(g.shape[2], 256, SUBLANES if dt == F32 else 2 * SUBLANES) for g, dt in zip(g42, wire)]
        got = _pair_exchange(g42, "grad_pair_exchange_" + tag)
        self.pair = [_pair_sum(g, o, self.core, dt, tm, "pair_sum_" + lb)
                     for g, o, dt, tm, lb in zip(g42, got, wire, self.tiles, labels)]

    def finish(self, by_src, gathered):
        red = [_sum4(b, self.core, tm, "chip_sum_" + lb) for b, tm, lb in zip(by_src, self.tiles, self.labels)]
        both = _pair_gather(red[:gathered], "grad_pair_gather_" + self.tag)
        pieces = [lax.dynamic_index_in_dim(r, self.c, 0, keepdims=False) for r in red[gathered:]]
        return [b.reshape(2 * b.shape[1], b.shape[2]) for b in both], pieces


def _small_as_shards(small_flat):
    unit = 8 * SUBLANES * PACK_COLS
    k = -(-small_flat.shape[0] // unit) * unit
    return jnp.pad(small_flat, (0, k - small_flat.shape[0])).reshape(4, k // (4 * PACK_COLS), PACK_COLS)


def _to_chunk_order(a):
    lp = a.shape[0]
    rest = a.shape[1:]
    a = a.reshape((lp // CHUNK, SUBLANES, KSTEPS) + rest)
    return a.swapaxes(1, 2).reshape((lp,) + rest)


def _from_chunk_order(a):
    lp = a.shape[0]
    rest = a.shape[1:]
    a = a.reshape((lp // CHUNK, KSTEPS, SUBLANES) + rest)
    return a.swapaxes(1, 2).reshape((lp,) + rest)


def _rope_tables(l_total, lp):
    n_real = l_total - N_META
    pos = np.arange(n_real)
    row_id = (pos // GRID_W).astype(np.float32)
    col_id = (pos % GRID_W).astype(np.float32)
    ppa = HEAD_DIM // 4
    inv_freq = (ROPE_THETA ** (-np.arange(ppa, dtype=np.float64) / ppa)).astype(np.float32)
    ang = np.concatenate([row_id[:, None] * inv_freq, col_id[:, None] * inv_freq], axis=-1)
    ang = np.concatenate([np.zeros((N_META, HEAD_DIM // 2), np.float32), ang,
                          np.zeros((lp - l_total, HEAD_DIM // 2), np.float32)], axis=0).astype(np.float64)
    cos = np.repeat(np.cos(ang), 2, axis=1)
    sin = np.repeat(np.sin(ang), 2, axis=1) * np.tile(np.asarray([-1.0, 1.0]), HEAD_DIM // 2)
    reps = (1, LANES // HEAD_DIM)
    return np.tile(cos, reps).astype(np.float32), np.tile(sin, reps).astype(np.float32)


def kernel(x, meta_tokens, norm_mix_g, w_in, ssm_a_re, ssm_a_im, ssm_log_dt, ssm_b_re, ssm_b_im, ssm_c_re, ssm_c_im, ssm_d, w_glu, b_glu, q_norm_g, k_norm_g, w_ssm_proj, w_attn_proj, w_out, norm_mlp_g, w_mlp_in, w_mlp_out, norm_final_g, loss_target, m_meta_tokens, m_norm_mix_g, m_w_in, m_ssm_a_re, m_ssm_a_im, m_ssm_log_dt, m_ssm_b_re, m_ssm_b_im, m_ssm_c_re, m_ssm_c_im, m_ssm_d, m_w_glu, m_b_glu, m_q_norm_g, m_k_norm_g, m_w_ssm_proj, m_w_attn_proj, m_w_out, m_norm_mlp_g, m_w_mlp_in, m_w_mlp_out, m_norm_final_g, v_meta_tokens, v_norm_mix_g, v_w_in, v_ssm_a_re, v_ssm_a_im, v_ssm_log_dt, v_ssm_b_re, v_ssm_b_im, v_ssm_c_re, v_ssm_c_im, v_ssm_d, v_w_glu, v_b_glu, v_q_norm_g, v_k_norm_g, v_w_ssm_proj, v_w_attn_proj, v_w_out, v_norm_mlp_g, v_w_mlp_in, v_w_mlp_out, v_norm_final_g):
    args = dict(locals())
    names = list(dict.fromkeys([n for n, _ in BIG] + list(SMALL)))
    order = ['meta_tokens', 'norm_mix_g', 'w_in', 'ssm_a_re', 'ssm_a_im', 'ssm_log_dt', 'ssm_b_re', 'ssm_b_im',
             'ssm_c_re', 'ssm_c_im', 'ssm_d', 'w_glu', 'b_glu', 'q_norm_g', 'k_norm_g', 'w_ssm_proj', 'w_attn_proj',
             'w_out', 'norm_mlp_g', 'w_mlp_in', 'w_mlp_out', 'norm_final_g']
    assert sorted(names) == sorted(order)
    c_idx = lax.axis_index("c")

    seq, d = x.shape[1], x.shape[2]
    l_total = seq + N_META
    lp = -(-l_total // SEQ_ALIGN) * SEQ_ALIGN
    hd = d // 2
    n_groups = hd // SSM_GROUP
    n_state = n_groups * SSM_STATE
    nj = n_state // SCAN_LANES
    kvh = d // HEAD_DIM // GQA_REP

    shard2d = {}
    for name, _ in BIG:
        t = args[name]
        shard2d[name] = t.reshape(t.shape[-2], t.shape[-1])
    full = _shard_major(EARLY_WEIGHTS, _all_gather8(_weight_blocks(shard2d, c_idx, EARLY_WEIGHTS), "weight_all_gather"))
    meta_full = jnp.transpose(full["meta_tokens"], (1, 0, 2)).reshape(N_META, d)
    w_in4 = full["w_in"]
    w_glu_f = full["w_glu"].reshape(hd, hd)

    xin = jnp.concatenate([meta_full, x[0], jnp.zeros((lp - l_total, d), F32)], axis=0)
    xin = _to_chunk_order(xin)
    tgt = _to_chunk_order(jnp.pad(loss_target[0], ((N_META, lp - l_total), (0, 0))))
    pos = np.arange(lp)
    rowmask = jnp.asarray(_to_chunk_order(((pos >= N_META) & (pos < l_total)).astype(np.float32)[:, None]))
    kbias = jnp.asarray(_to_chunk_order(np.where(pos < l_total, 0.0, MASK_VALUE).astype(np.float32)[:, None])
                        .reshape(1, lp))
    cos_t, sin_t = (jnp.asarray(_to_chunk_order(t)) for t in _rope_tables(l_total, lp))
    mean_m, sel = _head_tables(d)

    tm = _pick_tile(lp, 320)
    tm_mid = _pick_tile(lp, 384)
    tm_big = _pick_tile(lp, 640)
    tq = _pick_tile(lp, ATTN_Q_TILE, LANES)
    tk = _pick_tile(lp, ATTN_K_TILE, MXU_DIM)
    assert lp - CHUNK <= (l_total // CHUNK) * CHUNK and tk >= CHUNK
    g_mix = norm_mix_g.reshape(1, d)
    g_mlp = norm_mlp_g.reshape(1, d)
    g_fin = norm_final_g.reshape(1, d)
    qg = jnp.tile(q_norm_g.reshape(1, HEAD_DIM), (1, LANES // HEAD_DIM))
    kg = jnp.tile(k_norm_g.reshape(1, HEAD_DIM), (1, LANES // HEAD_DIM))
    dskip = ssm_d.reshape(1, hd)
    bglu = b_glu.reshape(1, hd)

    a_re, a_im = ssm_a_re[0], ssm_a_im[0]
    log_dt = ssm_log_dt[0][..., None]
    bt_re = jnp.swapaxes(ssm_b_re[0], 2, 3)
    bt_im = jnp.swapaxes(ssm_b_im[0], 2, 3)
    bb_re, bb_im, pw_re, pw_im, hi_re, hi_im = _ssm_discretize(a_re, a_im, log_dt, bt_re, bt_im)
    wb = [_embed_blocks(bb_re[i], bb_im[i]).astype(BF16) for i in range(2)]
    wct = [_embed_blocks(ssm_c_re[0, i], -ssm_c_im[0, i]).astype(BF16) for i in range(2)]
    tabs = [_scan_tables(pw_re[i], pw_im[i], hi_re[i], hi_im[i], rev=(i == 1)) for i in range(2)]
    tabs_adj = [_scan_tables(pw_re[i], pw_im[i], hi_re[i], hi_im[i], rev=(i == 0)) for i in range(2)]

    u, qkv, gates, hb = _in_proj(xin, g_mix, w_in4, tm_mid)
    y0, ck0 = _ssm_fwd(u, wb[0], wct[0], tabs[0], False, "ssm_fwd_0")
    y1, ck1 = _ssm_fwd(u, wb[1], wct[1], tabs[1], True, "ssm_fwd_1")
    yssm = _glu_fwd(u, y0, y1, dskip, w_glu_f, bglu, tm_big)
    q, k, v = _qk_prep(qkv, cos_t, sin_t, qg, kg, mean_m, tm)
    o, lse, pt, mblk, late = _attn_fwd(q, k, v, kbias, tq, tk, gather=_weight_blocks(shard2d, c_idx, LATE_WEIGHTS))
    full = _shard_major(LATE_WEIGHTS, late)
    w_mlp_in4 = full["w_mlp_in"]
    w_ssm_proj4 = full["w_ssm_proj"]
    w_attn_proj_f = full["w_attn_proj"].reshape(d, d)
    w_out_f = full["w_out"].reshape(d, d)
    w_mlp_out_f = full["w_mlp_out"].reshape(4 * d, d)
    h1, merged = _merge_fwd(yssm, o, gates, xin, w_ssm_proj4, w_attn_proj_f, w_out_f, tm_mid)
    r, h2b = _mlp_in(h1, g_mlp, w_mlp_in4, tm_mid)
    h3 = _mlp_out(h1, r, w_mlp_out_f, tm_mid)
    loss_tile, dh3, d_gfin = _final_loss(h3, g_fin, tgt, rowmask, tm_big)

    dz, dh3b = _mlp_bwd_a(dh3, r, w_mlp_out_f, tm_mid)
    dh1, d_gmlp = _mlp_bwd_b(dz, dh3, h1, g_mlp, w_mlp_in4, tm_mid)
    dgates, dms, dma, dyssm, do, delta, dh1b = _merge_bwd(dh1, yssm, o, gates, w_ssm_proj4, w_attn_proj_f, w_out_f,
                                                          sel, tm_mid)
    dyv, d_wglu, d_bglu, d_dskip = _glu_bwd(dyssm, u, y0, y1, dskip, w_glu_f, bglu, tm_big)

    tn = min(d, 1024)
    tm_w = _pick_tile(lp, 3 * MXU_DIM, MXU_DIM)
    grads4 = {
        "w_mlp_in": _wgrad(h2b, dz, 4, tm_w, tn, "wgrad_mlp_in"),
        "w_mlp_out": _wgrad(r, dh3b, 1, tm_w, min(d, 512), "wgrad_mlp_out", square=True).reshape(4, d, d),
        "w_out": _wgrad(merged, dh1b, 1, tm_w, tn, "wgrad_out").reshape(4, d // 4, d),
        "w_attn_proj": _wgrad(o, dma, 1, tm_w, tn, "wgrad_attn_proj").reshape(4, d // 4, d),
        "w_ssm_proj": _wgrad(yssm, dms, 4, tm_w, d // 4, "wgrad_ssm_proj"),
        "w_glu": d_wglu.reshape(4, hd // 4, hd),
    }
    first_names = list(grads4)
    first = _Reduction([grads4[n] for n in first_names], [BF16] * len(first_names), first_names, c_idx, "first")

    du0, dbb0, dcc0, dlb0, first_by_src = _ssm_bwd(u, dyv, ck0, wb[0], wct[0], _both(tabs[0], tabs_adj[0]), False,
                                                   "ssm_bwd_0", scatter=first.pair)
    du1, dbb1, dcc1, dlb1, _ = _ssm_bwd(u, dyv, ck1, wb[1], wct[1], _both(tabs[1], tabs_adj[1]), True, "ssm_bwd_1")
    dq, dk, dv = _attn_bwd(q, k, v, pt, mblk, do, lse, delta, _pick_tile(lp, MXU_DIM, LANES), tk)
    dqkv, d_qg, d_kg = _qk_bwd(qkv, dq, dk, dv, cos_t, sin_t, qg, kg, mean_m, tm)
    dxin, d_gmix, dproj = _in_proj_bwd(dyv, du0, du1, dskip, dqkv, dgates, dh1, xin, g_mix, w_in4, tm)
    red_big = dict(zip(first_names, first.finish(first_by_src, len(first_names))[0]))

    grads4["w_in"] = _wgrad(hb, dproj, 4, tm_w, tn, "wgrad_in")
    dx_nat = _from_chunk_order(dxin)
    grads4["meta_tokens"] = jnp.swapaxes(dx_nat[:N_META].reshape(N_META, 4, d // 4), 0, 1)
    grad_x = dx_nat[N_META:l_total][None]

    dlb = jnp.stack([dlb0, dlb1])[:, :, 0, :]
    dlb_re = dlb[:, :nj].reshape(2, n_groups, SSM_STATE)
    dlb_im = dlb[:, nj:].reshape(2, n_groups, SSM_STATE)
    gpn = (2, 2, n_groups, SSM_GROUP, SSM_STATE)
    dbb = jnp.stack([dbb0, dbb1]).reshape(gpn)
    dcc = jnp.stack([dcc0, dcc1]).reshape(gpn)
    d_are, d_aim, d_logdt, d_btre, d_btim = _ssm_param_bwd(a_re, a_im, log_dt, bt_re, bt_im, dlb_re, dlb_im,
                                                           dbb[:, 0], dbb[:, 1])
    small_grads = {
        "norm_mix_g": d_gmix, "ssm_a_re": d_are, "ssm_a_im": d_aim, "ssm_log_dt": d_logdt,
        "ssm_b_re": jnp.swapaxes(d_btre, 2, 3), "ssm_b_im": jnp.swapaxes(d_btim, 2, 3),
        "ssm_c_re": dcc[:, 0], "ssm_c_im": -dcc[:, 1],
        "ssm_d": d_dskip, "b_glu": d_bglu, "q_norm_g": d_qg[:, :HEAD_DIM], "k_norm_g": d_kg[:, :HEAD_DIM],
        "norm_mlp_g": d_gmlp, "norm_final_g": d_gfin,
    }
    small_flat = jnp.concatenate([small_grads[n].reshape(-1) for n in SMALL] + [loss_tile[0, :1]])

    last = _Reduction([grads4["meta_tokens"], grads4["w_in"], _small_as_shards(small_flat)], [F32, BF16, F32],
                      ["meta_tokens", "w_in", "small"], c_idx, "last")
    (red_big["meta_tokens"], red_big["w_in"]), (small_piece,) = last.finish(
        _chip_scatter(last.pair, "grad_chip_scatter"), 2)
    red_small = _all_gather8([small_piece], "small_grad_all_gather")[0].reshape(-1)[:small_flat.shape[0]]
    loss, red_small = red_small[-1], red_small[:-1]
    grad, delta_w, new_m, new_v = {}, {}, {}, {}
    for name, _ in BIG:
        w2 = shard2d[name]
        shp = args[name].shape
        g2 = red_big[name]
        t = _pick_tile(w2.shape[0], 256, 8)
        dl, nm, nv = _adamw(w2, g2, args["m_" + name].reshape(w2.shape), args["v_" + name].reshape(w2.shape), t)
        grad[name], delta_w[name], new_m[name], new_v[name] = (a.reshape(shp) for a in (g2, dl, nm, nv))

    def pack_small(prefix):
        flat = jnp.concatenate([args[prefix + n].reshape(-1) for n in SMALL])
        return _pad_rows(flat, SUBLANES)

    n_small = red_small.shape[0]
    gs = _pad_rows(red_small, SUBLANES)
    dl, nm, nv = _adamw(pack_small(""), gs, pack_small("m_"), pack_small("v_"), _pick_tile(gs.shape[0], 256, 8))
    off = 0
    for name in SMALL:
        shp = args[name].shape
        k = int(np.prod(shp))
        for dst, src in ((grad, gs), (delta_w, dl), (new_m, nm), (new_v, nv)):
            dst[name] = src.reshape(-1)[off:off + k].reshape(shp)
        off += k
    assert off == n_small

    return (loss, grad_x, *[grad[n] for n in order], *[delta_w[n] for n in order],
            *[new_m[n] for n in order], *[new_v[n] for n in order])


def _both(tab, tab_adj):
    return jnp.concatenate([tab, tab_adj], axis=0)
```

```python
import functools
import math

import numpy as np
import jax
import jax.numpy as jnp
from jax import lax
from jax.experimental import pallas as pl
from jax.experimental.pallas import tpu as pltpu

F32 = jnp.float32
BF16 = jnp.bfloat16

N_META = 16
GRID_W = 64
HEAD_DIM = 64
GQA_REP = 4
SSM_GROUP = 16
SSM_STATE = 64
ROPE_THETA = 10000.0
NORM_EPS = 1e-6
EIG_RE_MAX = -1e-4
ADAM_LR, ADAM_B1, ADAM_B2, ADAM_EPS, ADAM_WD, ADAM_STEP = 0.001, 0.9, 0.999, 1e-08, 0.01, 10

SUBLANES = 8
LANES = 128
CHUNK = 256
KSTEPS = CHUNK // SUBLANES
SCAN_LANES = 512
MXU_DIM = 256
SSM_BLOCK = MXU_DIM
SEQ_ALIGN = MXU_DIM
ATTN_Q_TILE = 384
ATTN_K_TILE = 11 * MXU_DIM
VMEM_LIMIT = 56 << 20
MASK_VALUE = -1e30
MESH_ID = pl.DeviceIdType.MESH


def _dot(a, b):
    return jnp.dot(a, b, preferred_element_type=F32)


def _dot_nt(a, b):
    return lax.dot_general(a, b, (((1,), (1,)), ((), ())), preferred_element_type=F32)


def _dot_tn(a, b):
    return lax.dot_general(a, b, (((0,), (0,)), ((), ())), preferred_element_type=F32)


def _row(tm, width):
    return pl.BlockSpec((tm, width), lambda i: (i, 0))


def _full(shape):
    nd = len(shape)
    return pl.BlockSpec(shape, lambda i: (0,) * nd)


def _params(sem):
    return pltpu.CompilerParams(dimension_semantics=sem, vmem_limit_bytes=VMEM_LIMIT)


def _pick_tile(n, cap, mult=16):
    best = None
    for t in range(mult, min(n, cap) + 1, mult):
        if n % t == 0:
            best = t
    assert best is not None, (n, cap)
    return best


def _rstd(x):
    return lax.rsqrt(jnp.mean(x * x, axis=-1, keepdims=True) + NORM_EPS)


def _rms(x, g):
    return x * _rstd(x) * g


def _rms_bwd(dy, x, g):
    r = _rstd(x)
    xh = x * r
    gdy = dy * g
    dx = r * (gdy - xh * jnp.mean(gdy * xh, axis=-1, keepdims=True))
    return dx, dy * xh


def _split_dot(x, m):
    hi = x.astype(BF16)
    lo = (x - hi.astype(F32)).astype(BF16)
    return _dot(hi, m) + _dot(lo, m)


def _sigmoid(x):
    return 1.0 / (1.0 + jnp.exp(-x))


def _acc_rows(ref, val, first):
    s = jnp.sum(val, axis=0, keepdims=True)

    @pl.when(first)
    def _():
        ref[...] = s

    @pl.when(jnp.logical_not(first))
    def _():
        ref[...] += s


def _in_proj(xin, g, w4, tm):
    lp, d = xin.shape
    hd = d // 2

    def body(x_ref, g_ref, w_ref, u_ref, qkv_ref, gt_ref, h_ref):
        h = _rms(x_ref[...], g_ref[...]).astype(BF16)
        h_ref[...] = h
        p0 = _dot(h, w_ref[0])
        u_ref[...] = p0[:, :hd]
        qkv_ref[:, :hd] = p0[:, hd:]
        qkv_ref[:, hd:] = _dot(h, w_ref[1])
        gt_ref[:, :d] = _dot(h, w_ref[2])
        gt_ref[:, d:] = _dot(h, w_ref[3])

    return pl.pallas_call(
        body, name="in_proj", grid=(lp // tm,),
        in_specs=[_row(tm, d), _full((1, d)), _full((4, d, d))],
        out_specs=[_row(tm, hd), _row(tm, 3 * hd), _row(tm, 2 * d), _row(tm, d)],
        out_shape=[jax.ShapeDtypeStruct((lp, hd), F32), jax.ShapeDtypeStruct((lp, 3 * hd), F32),
                   jax.ShapeDtypeStruct((lp, 2 * d), F32), jax.ShapeDtypeStruct((lp, d), BF16)],
        compiler_params=_params(("parallel",)),
    )(xin, g, w4)


def _gelu(y):
    return 0.5 * y * (1.0 + lax.erf(y * (1.0 / math.sqrt(2.0))))


def _gelu_grad(y):
    return 0.5 * (1.0 + lax.erf(y * (1.0 / math.sqrt(2.0)))) + y * jnp.exp(-0.5 * y * y) * (1.0 / math.sqrt(2.0 * math.pi))


def _glu_fwd(u, y0, y1, dskip, w_glu, b_glu, tm):
    lp, w = u.shape

    def body(u_ref, y0_ref, y1_ref, d_ref, w_ref, b_ref, o_ref):
        y = u_ref[...] * d_ref[...] + y0_ref[...] + y1_ref[...]
        z = _gelu(y)
        t = _dot(z.astype(BF16), w_ref[...]) + b_ref[...]
        o_ref[...] = (z * _sigmoid(t)).astype(BF16)

    return pl.pallas_call(
        body, name="glu_fwd", grid=(lp // tm,),
        in_specs=[_row(tm, w), _row(tm, w), _row(tm, w), _full((1, w)), _full((w, w)), _full((1, w))],
        out_specs=_row(tm, w), out_shape=jax.ShapeDtypeStruct((lp, w), BF16),
        compiler_params=_params(("parallel",)),
    )(u, y0, y1, dskip, w_glu, b_glu)


def _glu_bwd(dyssm, u, y0, y1, dskip, w_glu, b_glu, tm):
    lp, w = u.shape

    def body(g_ref, u_ref, y0_ref, y1_ref, d_ref, w_ref, b_ref, dy_ref, dw_ref, db_ref, dd_ref):
        first = pl.program_id(0) == 0
        uu = u_ref[...]
        y = uu * d_ref[...] + y0_ref[...] + y1_ref[...]
        z = _gelu(y)
        zb = z.astype(BF16)
        sg = _sigmoid(_dot(zb, w_ref[...]) + b_ref[...])
        g = g_ref[...]
        dt = g * z * sg * (1.0 - sg)
        dtb = dt.astype(BF16)
        dz = g * sg + _dot_nt(dtb, w_ref[...])
        dy = dz * _gelu_grad(y)
        dy_ref[...] = dy
        dw = _dot_tn(zb, dtb)

        @pl.when(first)
        def _():
            dw_ref[...] = dw

        @pl.when(jnp.logical_not(first))
        def _():
            dw_ref[...] += dw

        _acc_rows(db_ref, dt, first)
        _acc_rows(dd_ref, dy * uu, first)

    return pl.pallas_call(
        body, name="glu_bwd", grid=(lp // tm,),
        in_specs=[_row(tm, w), _row(tm, w), _row(tm, w), _row(tm, w), _full((1, w)), _full((w, w)), _full((1, w))],
        out_specs=[_row(tm, w), _full((w, w)), _full((1, w)), _full((1, w))],
        out_shape=[jax.ShapeDtypeStruct((lp, w), F32), jax.ShapeDtypeStruct((w, w), F32),
                   jax.ShapeDtypeStruct((1, w), F32), jax.ShapeDtypeStruct((1, w), F32)],
        compiler_params=_params(("arbitrary",)),
    )(dyssm, u, y0, y1, dskip, w_glu, b_glu)


def _merge_fwd(yssm, o, gates, xin, wsp4, wap, wo, tm):
    lp, d = xin.shape
    w = yssm.shape[1]
    ns = d // 4

    def body(y_ref, o_ref, g_ref, x_ref, wsp_ref, wap_ref, wo_ref, h_ref, m_ref):
        yb = y_ref[...]
        ms = jnp.concatenate([_dot(yb, wsp_ref[s]) for s in range(4)], axis=1)
        ma = _dot(o_ref[...], wap_ref[...])
        merged = (_sigmoid(g_ref[:, :d]) * ms + _sigmoid(g_ref[:, d:]) * ma).astype(BF16)
        m_ref[...] = merged
        h_ref[...] = x_ref[...] + _dot(merged, wo_ref[...])

    return pl.pallas_call(
        body, name="merge_fwd", grid=(lp // tm,),
        in_specs=[_row(tm, w), _row(tm, d), _row(tm, 2 * d), _row(tm, d),
                  _full((4, w, ns)), _full((d, d)), _full((d, d))],
        out_specs=[_row(tm, d), _row(tm, d)],
        out_shape=[jax.ShapeDtypeStruct((lp, d), F32), jax.ShapeDtypeStruct((lp, d), BF16)],
        compiler_params=_params(("parallel",)),
    )(yssm, o, gates, xin, wsp4, wap, wo)


def _merge_bwd(dh1, yssm, o, gates, wsp4, wap, wo, sel, tm):
    lp, d = dh1.shape
    w = yssm.shape[1]
    ns = d // 4
    nsel = sel.shape[1]

    def body(dh_ref, y_ref, o_ref, g_ref, wsp_ref, wap_ref, wo_ref, sel_ref,
             dg_ref, dms_ref, dma_ref, dy_ref, do_ref, dl_ref, dhb_ref):
        dhb = dh_ref[...].astype(BF16)
        dhb_ref[...] = dhb
        dm = _dot_nt(dhb, wo_ref[...])
        yb = y_ref[...]
        ob = o_ref[...]
        ms = jnp.concatenate([_dot(yb, wsp_ref[s]) for s in range(4)], axis=1)
        ma = _dot(ob, wap_ref[...])
        ss = _sigmoid(g_ref[:, :d])
        sa = _sigmoid(g_ref[:, d:])
        dg_ref[:, :d] = (dm * ms * ss * (1.0 - ss)).astype(BF16)
        dg_ref[:, d:] = (dm * ma * sa * (1.0 - sa)).astype(BF16)
        dms = (dm * ss).astype(BF16)
        dma = (dm * sa).astype(BF16)
        dms_ref[...] = dms
        dma_ref[...] = dma
        dy = _dot_nt(dms[:, :ns], wsp_ref[0])
        for s in range(1, 4):
            dy += _dot_nt(dms[:, s * ns:(s + 1) * ns], wsp_ref[s])
        dy_ref[...] = dy
        do = _dot_nt(dma, wap_ref[...])
        do_ref[...] = do.astype(BF16)
        dl_ref[...] = _split_dot(do * ob.astype(F32), sel_ref[...])

    return pl.pallas_call(
        body, name="merge_bwd", grid=(lp // tm,),
        in_specs=[_row(tm, d), _row(tm, w), _row(tm, d), _row(tm, 2 * d),
                  _full((4, w, ns)), _full((d, d)), _full((d, d)), _full((d, nsel))],
        out_specs=[_row(tm, 2 * d), _row(tm, d), _row(tm, d), _row(tm, w), _row(tm, d), _row(tm, nsel), _row(tm, d)],
        out_shape=[jax.ShapeDtypeStruct((lp, 2 * d), BF16), jax.ShapeDtypeStruct((lp, d), BF16),
                   jax.ShapeDtypeStruct((lp, d), BF16), jax.ShapeDtypeStruct((lp, w), F32),
                   jax.ShapeDtypeStruct((lp, d), BF16), jax.ShapeDtypeStruct((lp, nsel), F32),
                   jax.ShapeDtypeStruct((lp, d), BF16)],
        compiler_params=_params(("parallel",)),
    )(dh1, yssm, o, gates, wsp4, wap, wo, sel)


def _mlp_in(h1, g, w4, tm):
    lp, d = h1.shape

    def body(x_ref, g_ref, w_ref, r_ref, h_ref):
        h = _rms(x_ref[...], g_ref[...]).astype(BF16)
        h_ref[...] = h
        for s in range(4):
            r_ref[:, s * d:(s + 1) * d] = jnp.maximum(_dot(h, w_ref[s]), 0.0).astype(BF16)

    return pl.pallas_call(
        body, name="mlp_in", grid=(lp // tm,),
        in_specs=[_row(tm, d), _full((1, d)), _full((4, d, d))],
        out_specs=[_row(tm, 4 * d), _row(tm, d)],
        out_shape=[jax.ShapeDtypeStruct((lp, 4 * d), BF16), jax.ShapeDtypeStruct((lp, d), BF16)],
        compiler_params=_params(("parallel",)),
    )(h1, g, w4)


def _square_bf16(r):
    rf = r.astype(F32)
    return (rf * rf).astype(BF16)


def _mlp_out_loss(h1, r, w2, g, tgt, rowmask, tm):
    lp, d = h1.shape
    ff = r.shape[1]

    def body(h_ref, r_ref, w_ref, g_ref, t_ref, m_ref, loss_ref, dx_ref, dg_ref):
        first = pl.program_id(0) == 0
        x = h_ref[...] + _dot(_square_bf16(r_ref[...]), w_ref[...])
        gg = g_ref[...]
        err = (_rms(x, gg) - t_ref[...]) * m_ref[...]
        part = 0.5 * jnp.sum(jnp.sum(err * err, axis=1, keepdims=True), axis=0, keepdims=True) * (1.0 / d)
        part = jnp.broadcast_to(part, (SUBLANES, LANES))

        @pl.when(first)
        def _():
            loss_ref[...] = part

        @pl.when(jnp.logical_not(first))
        def _():
            loss_ref[...] += part

        dx, dgr = _rms_bwd(err * (1.0 / d), x, gg)
        dx_ref[...] = dx
        _acc_rows(dg_ref, dgr, first)

    return pl.pallas_call(
        body, name="mlp_out_loss", grid=(lp // tm,),
        in_specs=[_row(tm, d), _row(tm, ff), _full((ff, d)), _full((1, d)), _row(tm, d), _row(tm, 1)],
        out_specs=[_full((SUBLANES, LANES)), _row(tm, d), _full((1, d))],
        out_shape=[jax.ShapeDtypeStruct((SUBLANES, LANES), F32), jax.ShapeDtypeStruct((lp, d), F32),
                   jax.ShapeDtypeStruct((1, d), F32)],
        compiler_params=_params(("arbitrary",)),
    )(h1, r, w2, g, tgt, rowmask)


def _mlp_bwd_a(dh3, r, w2, tm):
    lp, d = dh3.shape
    ff = r.shape[1]

    def body(dh_ref, r_ref, w_ref, dz_ref, dhb_ref):
        dhb = dh_ref[...].astype(BF16)
        dhb_ref[...] = dhb
        da = _dot_nt(dhb, w_ref[...])
        dz_ref[...] = (da * (2.0 * r_ref[...].astype(F32))).astype(BF16)

    return pl.pallas_call(
        body, name="mlp_bwd_a", grid=(lp // tm,),
        in_specs=[_row(tm, d), _row(tm, ff), _full((ff, d))],
        out_specs=[_row(tm, ff), _row(tm, d)],
        out_shape=[jax.ShapeDtypeStruct((lp, ff), BF16), jax.ShapeDtypeStruct((lp, d), BF16)],
        compiler_params=_params(("parallel",)),
    )(dh3, r, w2)


def _mlp_bwd_b(dz, dh3, h1, g, w4, tm):
    lp, d = h1.shape

    def body(dz_ref, dh_ref, x_ref, g_ref, w_ref, dx_ref, dg_ref):
        first = pl.program_id(0) == 0
        dh2 = _dot_nt(dz_ref[:, :d], w_ref[0])
        for s in range(1, 4):
            dh2 += _dot_nt(dz_ref[:, s * d:(s + 1) * d], w_ref[s])
        dx, dgr = _rms_bwd(dh2, x_ref[...], g_ref[...])
        dx_ref[...] = dh_ref[...] + dx
        _acc_rows(dg_ref, dgr, first)

    return pl.pallas_call(
        body, name="mlp_bwd_b", grid=(lp // tm,),
        in_specs=[_row(tm, 4 * d), _row(tm, d), _row(tm, d), _full((1, d)), _full((4, d, d))],
        out_specs=[_row(tm, d), _full((1, d))],
        out_shape=[jax.ShapeDtypeStruct((lp, d), F32), jax.ShapeDtypeStruct((1, d), F32)],
        compiler_params=_params(("arbitrary",)),
    )(dz, dh3, h1, g, w4)


def _in_proj_bwd(dyv, du0, du1, dskip, dqkv, dgates, dres, xin, g, w4, tm):
    lp, d = xin.shape
    hd = d // 2

    def body(dy_ref, a_ref, b_ref, ds_ref, dq_ref, dgt_ref, dr_ref, x_ref, g_ref, w_ref, dx_ref, dg_ref, dp_ref):
        first = pl.program_id(0) == 0
        du = (dy_ref[...] * ds_ref[...] + a_ref[...] + b_ref[...]).astype(BF16)
        dq = dq_ref[...].astype(BF16)
        dgt = dgt_ref[...].astype(BF16)
        dp_ref[:, :hd] = du
        dp_ref[:, hd:2 * d] = dq
        dp_ref[:, 2 * d:] = dgt
        dh = _dot_nt(du, w_ref[0, :, :hd]) + _dot_nt(dq[:, :hd], w_ref[0, :, hd:])
        dh += _dot_nt(dq[:, hd:], w_ref[1])
        dh += _dot_nt(dgt[:, :d], w_ref[2]) + _dot_nt(dgt[:, d:], w_ref[3])
        dx, dgr = _rms_bwd(dh, x_ref[...], g_ref[...])
        dx_ref[...] = dr_ref[...] + dx
        _acc_rows(dg_ref, dgr, first)

    return pl.pallas_call(
        body, name="in_proj_bwd", grid=(lp // tm,),
        in_specs=[_row(tm, hd), _row(tm, hd), _row(tm, hd), _full((1, hd)), _row(tm, 3 * hd), _row(tm, 2 * d),
                  _row(tm, d), _row(tm, d), _full((1, d)), _full((4, d, d))],
        out_specs=[_row(tm, d), _full((1, d)), _row(tm, 4 * d)],
        out_shape=[jax.ShapeDtypeStruct((lp, d), F32), jax.ShapeDtypeStruct((1, d), F32),
                   jax.ShapeDtypeStruct((lp, 4 * d), BF16)],
        compiler_params=_params(("arbitrary",)),
    )(dyv, du0, du1, dskip, dqkv, dgates, dres, xin, g, w4)


def _wgrad(a, dy, nshard, tm, tn, name, square=False):
    lp, k = a.shape
    n = dy.shape[1]
    ns = n // nshard
    assert ns % tn == 0
    per = ns // tn

    def body(a_ref, dy_ref, o_ref):
        i = pl.program_id(1)
        acc = _dot_tn(_square_bf16(a_ref[...]) if square else a_ref[...], dy_ref[...])

        @pl.when(i == 0)
        def _():
            o_ref[0] = acc

        @pl.when(i != 0)
        def _():
            o_ref[0] += acc

    return pl.pallas_call(
        body, name=name, grid=(n // tn, lp // tm),
        in_specs=[pl.BlockSpec((tm, k), lambda j, i: (i, 0)), pl.BlockSpec((tm, tn), lambda j, i: (i, j))],
        out_specs=pl.BlockSpec((1, k, tn), lambda j, i: (j // per, 0, j % per)),
        out_shape=jax.ShapeDtypeStruct((nshard, k, ns), F32),
        compiler_params=_params(("parallel", "arbitrary")),
    )(a, dy)


def _head_tables(d):
    idx = np.arange(LANES)
    mean = (idx[:, None] // HEAD_DIM == idx[None, :] // HEAD_DIM).astype(np.float32) / HEAD_DIM
    n_heads = d // HEAD_DIM
    kvh = n_heads // GQA_REP
    c = np.arange(d)
    col = np.arange(kvh * LANES)
    head_of_col = (col // LANES) * GQA_REP + (col % LANES)
    sel = ((c[:, None] // HEAD_DIM == head_of_col[None, :]) & ((col % LANES) < GQA_REP)[None, :]).astype(np.float32)
    return jnp.asarray(mean, BF16), jnp.asarray(sel, BF16)


def _swap_pairs(y):
    lane = lax.broadcasted_iota(jnp.int32, y.shape, 1)
    return jnp.where(lane % 2 == 0, pltpu.roll(y, LANES - 1, 1), pltpu.roll(y, 1, 1))


def _qk_prep(qkv, cos_t, sin_t, qg, kg, mean_m, tm):
    lp, wq = qkv.shape
    d = wq * 2 // 3
    kvw = d // 4
    kvh = kvw // HEAD_DIM
    scale = HEAD_DIM ** -0.5

    def body(x_ref, c_ref, s_ref, qg_ref, kg_ref, m_ref, q_ref, k_ref, v_ref):
        cs, sn, mm = c_ref[...], s_ref[...], m_ref[...]
        for b in range((d + kvw) // LANES):
            x = x_ref[:, b * LANES:(b + 1) * LANES]
            gg = qg_ref[...] if b < d // LANES else kg_ref[...]
            y = x * lax.rsqrt(_split_dot(x * x, mm) + NORM_EPS) * gg
            out = y * cs + _swap_pairs(y) * sn
            if b < d // LANES:
                q_ref[:, b * LANES:(b + 1) * LANES] = (out * scale).astype(BF16)
            else:
                kb = b - d // LANES
                k_ref[2 * kb] = out[:, :HEAD_DIM].astype(BF16)
                k_ref[2 * kb + 1] = out[:, HEAD_DIM:].astype(BF16)
        ones = jnp.ones((tm, LANES - HEAD_DIM), BF16)
        for h in range(kvh):
            vh = x_ref[:, d + kvw + h * HEAD_DIM:d + kvw + (h + 1) * HEAD_DIM].astype(BF16)
            v_ref[h] = jnp.concatenate([vh, ones], axis=1)

    k_spec = pl.BlockSpec((kvh, tm, HEAD_DIM), lambda i: (0, i, 0))
    v_spec = pl.BlockSpec((kvh, tm, LANES), lambda i: (0, i, 0))
    return pl.pallas_call(
        body, name="qk_prep", grid=(lp // tm,),
        in_specs=[_row(tm, wq), _row(tm, LANES), _row(tm, LANES), _full((1, LANES)), _full((1, LANES)),
                  _full((LANES, LANES))],
        out_specs=[_row(tm, d), k_spec, v_spec],
        out_shape=[jax.ShapeDtypeStruct((lp, d), BF16), jax.ShapeDtypeStruct((kvh, lp, HEAD_DIM), BF16),
                   jax.ShapeDtypeStruct((kvh, lp, LANES), BF16)],
        compiler_params=_params(("parallel",)),
    )(qkv, cos_t, sin_t, qg, kg, mean_m)


def _qk_bwd(qkv, dq, dk, dv, cos_t, sin_t, qg, kg, mean_m, tm):
    lp, wq = qkv.shape
    d = wq * 2 // 3
    kvw = d // 4
    kvh = kvw // HEAD_DIM
    scale = HEAD_DIM ** -0.5

    def body(x_ref, dq_ref, dk_ref, dv_ref, c_ref, s_ref, qg_ref, kg_ref, m_ref, o_ref, dqg_ref, dkg_ref):
        first = pl.program_id(0) == 0
        cs, sn, mm = c_ref[...], s_ref[...], m_ref[...]
        sums = [None, None]
        for b in range((d + kvw) // LANES):
            is_q = b < d // LANES
            x = x_ref[:, b * LANES:(b + 1) * LANES]
            gg = qg_ref[...] if is_q else kg_ref[...]
            r = lax.rsqrt(_split_dot(x * x, mm) + NORM_EPS)
            nrm = x * r
            if is_q:
                dout = dq_ref[:, b * LANES:(b + 1) * LANES] * scale
            else:
                kb = b - d // LANES
                dout = jnp.concatenate([dk_ref[2 * kb], dk_ref[2 * kb + 1]], axis=1)
            dy = dout * cs + _swap_pairs(dout * sn)
            part = jnp.sum(dy * nrm, axis=0, keepdims=True)
            sums[0 if is_q else 1] = part if sums[0 if is_q else 1] is None else sums[0 if is_q else 1] + part
            dn = dy * gg
            o_ref[:, b * LANES:(b + 1) * LANES] = (r * (dn - nrm * _split_dot(dn * nrm, mm))).astype(BF16)
        for h in range(kvh):
            o_ref[:, d + kvw + h * HEAD_DIM:d + kvw + (h + 1) * HEAD_DIM] = dv_ref[h].astype(BF16)
        for ref, s in ((dqg_ref, sums[0]), (dkg_ref, sums[1])):
            s = s + pltpu.roll(s, HEAD_DIM, 1)

            @pl.when(first)
            def _(ref=ref, s=s):
                ref[...] = s

            @pl.when(jnp.logical_not(first))
            def _(ref=ref, s=s):
                ref[...] += s

    kv_spec = pl.BlockSpec((kvh, tm, HEAD_DIM), lambda i: (0, i, 0))
    return pl.pallas_call(
        body, name="qk_bwd", grid=(lp // tm,),
        in_specs=[_row(tm, wq), _row(tm, d), kv_spec, kv_spec, _row(tm, LANES), _row(tm, LANES),
                  _full((1, LANES)), _full((1, LANES)), _full((LANES, LANES))],
        out_specs=[_row(tm, wq), _full((1, LANES)), _full((1, LANES))],
        out_shape=[jax.ShapeDtypeStruct((lp, wq), BF16), jax.ShapeDtypeStruct((1, LANES), F32),
                   jax.ShapeDtypeStruct((1, LANES), F32)],
        compiler_params=_params(("arbitrary",)),
    )(qkv, dq, dk, dv, cos_t, sin_t, qg, kg, mean_m)


def _attn_fwd(q, k, v, kbias, tq, tk, gather=()):
    lp, d = q.shape
    kvh = k.shape[0]
    rw = GQA_REP * HEAD_DIM
    nk = lp // tk

    ng = len(gather)
    steps = kvh * (lp // tq) * nk

    def body(*refs):
        q_ref, k_ref, v_ref, kb_ref = refs[:4]
        o_ref, lse_ref, pt_ref, mb_ref = refs[4 + ng:8 + ng]
        m_s, acc_s = refs[8 + 2 * ng:10 + 2 * ng]
        j = pl.program_id(2)

        if ng:
            phases = _gather_phases(refs[4:4 + ng], refs[8 + ng:8 + 2 * ng], *refs[10 + 2 * ng:])
            step = (pl.program_id(0) * (lp // tq) + pl.program_id(1)) * nk + j
            for n, phase in enumerate(phases):
                pl.when(step == n * steps // 3)(phase)

        @pl.when(j == 0)
        def _():
            m_s[...] = jnp.full(m_s.shape, MASK_VALUE, F32)
            acc_s[...] = jnp.zeros(acc_s.shape, F32)

        def heads(masked):
            kk, vv = k_ref[0], v_ref[0]

            def scores(h):
                return _dot_nt(q_ref[:, h * HEAD_DIM:(h + 1) * HEAD_DIM], kk)

            def softmax(h, s):
                if masked:
                    s = jnp.concatenate([s[:, :tk - CHUNK], s[:, tk - CHUNK:] + kb_ref[:, tk - CHUNK:]], axis=1)
                m_prev = m_s[h]
                m_new = jnp.maximum(m_prev, jnp.max(s, axis=1, keepdims=True))
                m_s[h] = m_new
                p = jnp.exp(s - m_new[:, :1]).astype(BF16)
                pt_ref[h] = p
                return p, jnp.exp(m_prev - m_new), m_new

            def accumulate(h, p, alpha):
                acc_s[h] = acc_s[h] * alpha + _dot(p, vv)

            ss = [scores(h) for h in range(GQA_REP)]
            pa = [softmax(h, ss[h]) for h in range(GQA_REP)]
            for h in range(GQA_REP):
                accumulate(h, *pa[h][:2])
            lane = lax.broadcasted_iota(jnp.int32, (tq, LANES), 1)
            mb = jnp.zeros((tq, LANES), F32)
            for h in range(GQA_REP):
                mb = jnp.where(lane == h, pa[h][2], mb)
            mb_ref[0] = mb

        pl.when(j != nk - 1)(functools.partial(heads, False))
        pl.when(j == nk - 1)(functools.partial(heads, True))

        @pl.when(j == nk - 1)
        def _():
            lane = lax.broadcasted_iota(jnp.int32, (tq, LANES), 1)
            lse = jnp.zeros((tq, LANES), F32)
            outs = []
            for h in range(GQA_REP):
                acc = acc_s[h]
                den = pltpu.roll(acc, HEAD_DIM, 1)
                outs.append((acc / den)[:, :HEAD_DIM])
                lse = jnp.where(lane == h, m_s[h] + jnp.log(den), lse)
            o_ref[...] = jnp.concatenate(outs, axis=1).astype(BF16)
            lse_ref[...] = lse

    sems = [pltpu.SemaphoreType.DMA((ng, 7)), pltpu.SemaphoreType.DMA((ng, 7)), pltpu.SemaphoreType.DMA((ng,))]
    res = pl.pallas_call(
        body, name="attn_fwd", grid=(kvh, lp // tq, nk),
        in_specs=[pl.BlockSpec((tq, rw), lambda g, i, j: (i, g)),
                  pl.BlockSpec((1, tk, HEAD_DIM), lambda g, i, j: (g, j, 0)),
                  pl.BlockSpec((1, tk, LANES), lambda g, i, j: (g, j, 0)),
                  pl.BlockSpec((1, tk), lambda g, i, j: (0, j))] + [_ANY] * ng,
        out_specs=[pl.BlockSpec((tq, rw), lambda g, i, j: (i, g)),
                   pl.BlockSpec((tq, LANES), lambda g, i, j: (i, g)),
                   pl.BlockSpec((GQA_REP, tq, tk), lambda g, i, j: (g, i, j)),
                   pl.BlockSpec((1, tq, LANES), lambda g, i, j: (j, i, g))] + [_ANY] * ng,
        out_shape=[jax.ShapeDtypeStruct((lp, d), BF16), jax.ShapeDtypeStruct((lp, kvh * LANES), F32),
                   jax.ShapeDtypeStruct((kvh * GQA_REP, lp, lp), BF16), jax.ShapeDtypeStruct((nk, lp, kvh * LANES), F32)]
        + [jax.ShapeDtypeStruct((8,) + b.shape, b.dtype) for b in gather],
        scratch_shapes=[pltpu.VMEM((GQA_REP, tq, LANES), F32), pltpu.VMEM((GQA_REP, tq, LANES), F32)]
        + (sems if ng else []),
        compiler_params=_params(("arbitrary", "arbitrary", "arbitrary")),
    )(q, k, v, kbias, *gather)
    return res[0], res[1], res[2], res[3], list(res[4:])


def _attn_bwd(q, k, v, pt, mblk, do, lse, delta, tq, tk):
    lp, d = q.shape
    kvh = k.shape[0]
    rw = GQA_REP * HEAD_DIM
    nq = lp // tq

    def body(q_ref, k_ref, v_ref, pt_ref, mb_ref, do_ref, lse_ref, dl_ref, dq_ref, dk_ref, dv_ref, dk_s, dv_s):
        j = pl.program_id(1)
        i = pl.program_id(2)

        @pl.when(jnp.logical_and(i == 0, j == 0))
        def _():
            dq_ref[...] = jnp.zeros(dq_ref.shape, F32)

        @pl.when(i == 0)
        def _():
            dk_s[...] = jnp.zeros(dk_s.shape, F32)
            dv_s[...] = jnp.zeros(dv_s.shape, F32)

        kk, vv = k_ref[0], v_ref[0][:, :HEAD_DIM]
        scale = jnp.exp(mb_ref[0] - lse_ref[...])
        dl = dl_ref[...] * scale
        dqs = []
        for pair in ((0, 1), (2, 3)):
            dos = {h: (do_ref[:, h * HEAD_DIM:(h + 1) * HEAD_DIM].astype(F32) * scale[:, h:h + 1]).astype(BF16)
                   for h in pair}
            dps = {h: _dot_nt(dos[h], vv) for h in pair}
            for h in pair:
                dv_s[...] += _dot_tn(pt_ref[h], dos[h])
            dss = {h: pt_ref[h] * (dps[h] - dl[:, h:h + 1]).astype(BF16) for h in pair}
            for h in pair:
                dk_s[...] += _dot_tn(dss[h], q_ref[:, h * HEAD_DIM:(h + 1) * HEAD_DIM])
                dqs.append(_dot(dss[h], kk))
        rows = pl.ds(pl.multiple_of(i * tq, tq), tq)
        dq_ref[rows, :] += jnp.concatenate(dqs, axis=1)

        @pl.when(i == nq - 1)
        def _():
            dk_ref[0] = dk_s[...]
            dv_ref[0] = dv_s[...]

    return pl.pallas_call(
        body, name="attn_bwd", grid=(kvh, lp // tk, nq),
        in_specs=[pl.BlockSpec((tq, rw), lambda g, j, i: (i, g)),
                  pl.BlockSpec((1, tk, HEAD_DIM), lambda g, j, i: (g, j, 0)),
                  pl.BlockSpec((1, tk, LANES), lambda g, j, i: (g, j, 0)),
                  pl.BlockSpec((GQA_REP, tq, tk), lambda g, j, i: (g, i, j)),
                  pl.BlockSpec((1, tq, LANES), lambda g, j, i: (j, i, g)),
                  pl.BlockSpec((tq, rw), lambda g, j, i: (i, g)),
                  pl.BlockSpec((tq, LANES), lambda g, j, i: (i, g)),
                  pl.BlockSpec((tq, LANES), lambda g, j, i: (i, g))],
        out_specs=[pl.BlockSpec((lp, rw), lambda g, j, i: (0, g)),
                   pl.BlockSpec((1, tk, HEAD_DIM), lambda g, j, i: (g, j, 0)),
                   pl.BlockSpec((1, tk, HEAD_DIM), lambda g, j, i: (g, j, 0))],
        out_shape=[jax.ShapeDtypeStruct((lp, d), F32), jax.ShapeDtypeStruct((kvh, lp, HEAD_DIM), F32),
                   jax.ShapeDtypeStruct((kvh, lp, HEAD_DIM), F32)],
        scratch_shapes=[pltpu.VMEM((tk, HEAD_DIM), F32), pltpu.VMEM((tk, HEAD_DIM), F32)],
        compiler_params=_params(("parallel", "arbitrary", "arbitrary")),
    )(q, k, v, pt, mblk, do, lse, delta)


def _ssm_math(a_re, a_im, log_dt, bt_re, bt_im):
    dt = jnp.exp(log_dt)
    lam_re = jnp.minimum(a_re, EIG_RE_MAX)
    lam_im = a_im
    mag = jnp.exp(lam_re * dt)
    ang = lam_im * dt
    lb_re = mag * jnp.cos(ang)
    lb_im = mag * jnp.sin(ang)
    num_re = lb_re - 1.0
    num_im = lb_im
    den = lam_re * lam_re + lam_im * lam_im
    f_re = (num_re * lam_re + num_im * lam_im) / den
    f_im = (num_im * lam_re - num_re * lam_im) / den
    bb_re = f_re[:, None, :] * bt_re - f_im[:, None, :] * bt_im
    bb_im = f_re[:, None, :] * bt_im + f_im[:, None, :] * bt_re
    return lb_re, lb_im, bb_re, bb_im


def _ssm_discretize(a_re, a_im, log_dt, bt_re, bt_im):
    nd, g, n = a_re.shape
    p = bt_re.shape[2]

    def body(ar_ref, ai_ref, ld_ref, br_ref, bi_ref, bbr_ref, bbi_ref, pr_ref, pi_ref, hr_ref, hi_ref):
        lb_re, lb_im, bb_re, bb_im = _ssm_math(ar_ref[0], ai_ref[0], ld_ref[0], br_ref[0], bi_ref[0])
        bbr_ref[0] = bb_re
        bbi_ref[0] = bb_im
        cr, ci = lb_re, lb_im
        for k in range(KSTEPS):
            pr_ref[0, k] = cr
            pi_ref[0, k] = ci
            if k < KSTEPS - 1:
                cr, ci = cr * lb_re - ci * lb_im, cr * lb_im + ci * lb_re
        for t in range(2):
            cr, ci = cr * cr - ci * ci, 2.0 * cr * ci
            hr_ref[0, t] = cr
            hi_ref[0, t] = ci

    s3 = pl.BlockSpec((1, g, n), lambda i: (i, 0, 0))
    s4 = pl.BlockSpec((1, g, p, n), lambda i: (i, 0, 0, 0))
    sp = pl.BlockSpec((1, KSTEPS, g, n), lambda i: (i, 0, 0, 0))
    sh = pl.BlockSpec((1, 2, g, n), lambda i: (i, 0, 0, 0))
    return pl.pallas_call(
        body, name="ssm_discretize", grid=(nd,),
        in_specs=[s3, s3, pl.BlockSpec((1, g, 1), lambda i: (i, 0, 0)), s4, s4],
        out_specs=[s4, s4, sp, sp, sh, sh],
        out_shape=[jax.ShapeDtypeStruct((nd, g, p, n), F32)] * 2 + [jax.ShapeDtypeStruct((nd, KSTEPS, g, n), F32)] * 2
        + [jax.ShapeDtypeStruct((nd, 2, g, n), F32)] * 2,
        compiler_params=_params(("parallel",)),
    )(a_re, a_im, log_dt, bt_re, bt_im)


def _ssm_param_bwd(a_re, a_im, log_dt, bt_re, bt_im, dlb_re, dlb_im, dbb_re, dbb_im):
    nd, g, n = a_re.shape
    p = bt_re.shape[2]

    def body(ar_ref, ai_ref, ld_ref, br_ref, bi_ref, c0_ref, c1_ref, c2_ref, c3_ref,
             o0_ref, o1_ref, o2_ref, o3_ref, o4_ref):
        _, vjp = jax.vjp(_ssm_math, ar_ref[0], ai_ref[0], ld_ref[0], br_ref[0], bi_ref[0])
        outs = vjp((c0_ref[0], c1_ref[0], c2_ref[0], c3_ref[0]))
        for ref, val in zip((o0_ref, o1_ref, o2_ref, o3_ref, o4_ref), outs):
            ref[0] = val

    s3 = pl.BlockSpec((1, g, n), lambda i: (i, 0, 0))
    s1 = pl.BlockSpec((1, g, 1), lambda i: (i, 0, 0))
    s4 = pl.BlockSpec((1, g, p, n), lambda i: (i, 0, 0, 0))
    return pl.pallas_call(
        body, name="ssm_param_bwd", grid=(nd,),
        in_specs=[s3, s3, s1, s4, s4, s3, s3, s4, s4],
        out_specs=[s3, s3, s1, s4, s4],
        out_shape=[jax.ShapeDtypeStruct((nd, g, n), F32)] * 2 + [jax.ShapeDtypeStruct((nd, g, 1), F32)]
        + [jax.ShapeDtypeStruct((nd, g, p, n), F32)] * 2,
        compiler_params=_params(("parallel",)),
    )(a_re, a_im, log_dt, bt_re, bt_im, dlb_re, dlb_im, dbb_re, dbb_im)


def _cmul(ar, ai, xr, xi, conj):
    if conj:
        return ar * xr + ai * xi, ar * xi - ai * xr
    return ar * xr - ai * xi, ar * xi + ai * xr


def _scan_chunk(buf, tab, carry, ein, nj, rev, conj, base=0):
    ks = list(range(KSTEPS))
    if rev:
        ks = ks[::-1]
    sub = lax.broadcasted_iota(jnp.int32, (SUBLANES, SCAN_LANES), 0)
    edge = sub == (SUBLANES - 1 if rev else 0)

    def step(j, _):
        jr, ji = j, nj + j
        ar, ai = tab[base, jr], tab[base, ji]
        hr = jnp.zeros((SUBLANES, SCAN_LANES), F32)
        hi = jnp.zeros((SUBLANES, SCAN_LANES), F32)
        for k in ks:
            rows = pl.ds(k * SUBLANES, SUBLANES)
            pr, pi_ = _cmul(ar, ai, hr, hi, conj)
            hr = pr + buf[jr, rows, :]
            hi = pi_ + buf[ji, rows, :]
            buf[jr, rows, :] = hr
            buf[ji, rows, :] = hi
        shift = SUBLANES - 1 if rev else 1
        er = jnp.where(edge, carry[jr], pltpu.roll(hr, shift, 0))
        ei = jnp.where(edge, carry[ji], pltpu.roll(hi, shift, 0))
        for t, dist in enumerate((1, 2, 4)):
            sh = SUBLANES - dist if rev else dist
            pr, pi_ = _cmul(tab[base + 1 + t, jr], tab[base + 1 + t, ji], pltpu.roll(er, sh, 0), pltpu.roll(ei, sh, 0), conj)
            er, ei = er + pr, ei + pi_
        ein[jr] = er
        ein[ji] = ei
        pr, pi_ = _cmul(tab[base + 4 + KSTEPS - 1, jr], tab[base + 4 + KSTEPS - 1, ji], er, ei, conj)
        last = 0 if rev else SUBLANES - 1
        carry[jr] = jnp.broadcast_to((hr + pr)[last:last + 1, :], (SUBLANES, SCAN_LANES))
        carry[ji] = jnp.broadcast_to((hi + pi_)[last:last + 1, :], (SUBLANES, SCAN_LANES))
        for n, k in enumerate(ks):
            rows = pl.ds(k * SUBLANES, SUBLANES)
            pr, pi_ = _cmul(tab[base + 4 + n, jr], tab[base + 4 + n, ji], er, ei, conj)
            buf[jr, rows, :] += pr
            buf[ji, rows, :] += pi_
        return 0

    lax.fori_loop(0, nj, step, 0)


def _state_lanes(b):
    per = SCAN_LANES // SSM_BLOCK
    return b // per, slice((b % per) * SSM_BLOCK, (b % per + 1) * SSM_BLOCK)


def _project_in(src, w_ref, buf, nj):
    nb, cb, _ = w_ref.shape
    for b in range(nb):
        res = _dot(src[:, b * cb:(b + 1) * cb], w_ref[b])
        j, lanes = _state_lanes(b)
        buf[j, :, lanes] = res[:, :SSM_BLOCK]
        buf[nj + j, :, lanes] = res[:, SSM_BLOCK:]


def _state_block(buf, b, nj):
    j, lanes = _state_lanes(b)
    return jnp.concatenate([buf[j, :, lanes], buf[nj + j, :, lanes]], axis=1).astype(BF16)


def _project_out(buf, w_ref, nj):
    return jnp.concatenate([_dot_nt(_state_block(buf, b, nj), w_ref[b]) for b in range(w_ref.shape[0])], axis=1)


def _ssm_fwd(u, wb, wct, tab, rev, name):
    lp, w = u.shape
    nb, cb, _ = wb.shape
    nj = nb * SSM_BLOCK // SCAN_LANES
    nc = lp // CHUNK
    ntab = tab.shape[0]
    cidx = (lambda c: nc - 1 - c) if rev else (lambda c: c)

    def body(u_ref, wb_ref, wct_ref, tab_ref, y_ref, ck_ref, buf, carry, ein):
        @pl.when(pl.program_id(0) == 0)
        def _():
            carry[...] = jnp.zeros(carry.shape, F32)

        _project_in(u_ref[...].astype(BF16), wb_ref, buf, nj)
        ck_ref[0] = carry[...]
        _scan_chunk(buf, tab_ref, carry, ein, nj, rev, False)
        y_ref[...] = _project_out(buf, wct_ref, nj)

    wshape = (nb, cb, 2 * SSM_BLOCK)
    return pl.pallas_call(
        body, name=name, grid=(nc,),
        in_specs=[pl.BlockSpec((CHUNK, w), lambda c: (cidx(c), 0)), _full(wshape), _full(wshape),
                  _full((ntab, 2 * nj, SUBLANES, SCAN_LANES))],
        out_specs=[pl.BlockSpec((CHUNK, w), lambda c: (cidx(c), 0)),
                   pl.BlockSpec((1, 2 * nj, SUBLANES, SCAN_LANES), lambda c: (cidx(c), 0, 0, 0))],
        out_shape=[jax.ShapeDtypeStruct((lp, w), F32), jax.ShapeDtypeStruct((nc, 2 * nj, SUBLANES, SCAN_LANES), F32)],
        scratch_shapes=[pltpu.VMEM((2 * nj, CHUNK, SCAN_LANES), F32), pltpu.VMEM((2 * nj, SUBLANES, SCAN_LANES), F32),
                        pltpu.VMEM((2 * nj, SUBLANES, SCAN_LANES), F32)],
        compiler_params=_params(("arbitrary",)),
    )(u, wb, wct, tab)


def _ssm_bwd(u, dy, ckpt, wb, wct, tab, rev, name, scatter=()):
    lp, w = u.shape
    nb, cb, _ = wb.shape
    nj = nb * SSM_BLOCK // SCAN_LANES
    nc = lp // CHUNK
    ntab = tab.shape[0]
    cidx = (lambda c: c) if rev else (lambda c: nc - 1 - c)

    ns = len(scatter)

    def body(*refs):
        u_ref, dy_ref, ck_ref, wb_ref, wct_ref, tab_hbm = refs[:6]
        du_ref, dbb_ref, dcc_ref, dlb_ref = refs[6 + ns:10 + ns]
        tab_ref, dwb_ref, dwc_ref, xs, ls, xcar, lcar, xin, lin = refs[10 + 2 * ns:19 + 2 * ns]
        c = pl.program_id(0)

        if ns:
            start, finish = _scatter_phases(refs[6:6 + ns], refs[10 + ns:10 + 2 * ns], *refs[19 + 2 * ns:])
            pl.when(c == 0)(start)
            pl.when(c == nc - 1)(finish)

        @pl.when(c == 0)
        def _():
            pltpu.sync_copy(tab_hbm, tab_ref)
            lcar[...] = jnp.zeros(lcar.shape, F32)
            dwb_ref[...] = jnp.zeros(dwb_ref.shape, F32)
            dwc_ref[...] = jnp.zeros(dwc_ref.shape, F32)
            dlb_ref[...] = jnp.zeros(dlb_ref.shape, F32)

        ub = u_ref[...].astype(BF16)
        dyb = dy_ref[...].astype(BF16)
        _project_in(ub, wb_ref, xs, nj)
        xcar[...] = ck_ref[0]
        _scan_chunk(xs, tab_ref, xcar, xin, nj, rev, False)
        _project_in(dyb, wct_ref, ls, nj)
        _scan_chunk(ls, tab_ref, lcar, lin, nj, not rev, True, base=ntab // 2)
        dus = []
        for b in range(nb):
            chans = slice(b * cb, (b + 1) * cb)
            xb = _state_block(xs, b, nj)
            lb = _state_block(ls, b, nj)
            dwc_ref[b] += _dot_tn(dyb[:, chans], xb)
            dwb_ref[b] += _dot_tn(ub[:, chans], lb)
            dus.append(_dot_nt(lb, wb_ref[b]))
        du_ref[...] = jnp.concatenate(dus, axis=1)

        def step(j, _):
            jr, ji = j, nj + j
            ar = jnp.zeros((SUBLANES, SCAN_LANES), F32)
            ai = jnp.zeros((SUBLANES, SCAN_LANES), F32)
            for k in range(KSTEPS):
                kp = k + 1 if rev else k - 1
                rows = pl.ds(k * SUBLANES, SUBLANES)
                if 0 <= kp < KSTEPS:
                    prow = pl.ds(kp * SUBLANES, SUBLANES)
                    xr, xi = xs[jr, prow, :], xs[ji, prow, :]
                else:
                    xr, xi = xin[jr], xin[ji]
                lr, li = ls[jr, rows, :], ls[ji, rows, :]
                ar += lr * xr + li * xi
                ai += li * xr - lr * xi
            dlb_ref[jr] += ar
            dlb_ref[ji] += ai
            return 0

        lax.fori_loop(0, nj, step, 0)

        @pl.when(c == nc - 1)
        def _():
            for b in range(2 * nj):
                dlb_ref[b] = jnp.broadcast_to(jnp.sum(dlb_ref[b], axis=0, keepdims=True), (SUBLANES, SCAN_LANES))
            for g in range(w // SSM_GROUP):
                b, gl = divmod(g, cb // SSM_GROUP)
                rows = slice(gl * SSM_GROUP, (gl + 1) * SSM_GROUP)
                for part in range(2):
                    cols = slice(part * SSM_BLOCK + gl * SSM_STATE, part * SSM_BLOCK + (gl + 1) * SSM_STATE)
                    dbb_ref[part, g * SSM_GROUP:(g + 1) * SSM_GROUP, :] = dwb_ref[b, rows, cols]
                    dcc_ref[part, g * SSM_GROUP:(g + 1) * SSM_GROUP, :] = dwc_ref[b, rows, cols]

    st = (2 * nj, SUBLANES, SCAN_LANES)
    wshape = (nb, cb, 2 * SSM_BLOCK)
    sems = [pltpu.SemaphoreType.DMA((ns, 3)), pltpu.SemaphoreType.DMA((ns, 3)), pltpu.SemaphoreType.DMA((ns,))]
    res = pl.pallas_call(
        body, name=name, grid=(nc,),
        in_specs=[pl.BlockSpec((CHUNK, w), lambda c: (cidx(c), 0)), pl.BlockSpec((CHUNK, w), lambda c: (cidx(c), 0)),
                  pl.BlockSpec((1,) + st, lambda c: (cidx(c), 0, 0, 0)), _full(wshape), _full(wshape), _ANY]
        + [_ANY] * ns,
        out_specs=[pl.BlockSpec((CHUNK, w), lambda c: (cidx(c), 0)), _full((2, w, SSM_STATE)),
                   _full((2, w, SSM_STATE)), _full(st)] + [_ANY] * ns,
        out_shape=[jax.ShapeDtypeStruct((lp, w), F32), jax.ShapeDtypeStruct((2, w, SSM_STATE), F32),
                   jax.ShapeDtypeStruct((2, w, SSM_STATE), F32), jax.ShapeDtypeStruct(st, F32)]
        + [jax.ShapeDtypeStruct(p.shape, p.dtype) for p in scatter],
        scratch_shapes=[pltpu.VMEM((ntab,) + st, F32), pltpu.VMEM(wshape, F32), pltpu.VMEM(wshape, F32),
                        pltpu.VMEM((2 * nj, CHUNK, SCAN_LANES), F32), pltpu.VMEM((2 * nj, CHUNK, SCAN_LANES), F32),
                        pltpu.VMEM(st, F32), pltpu.VMEM(st, F32), pltpu.VMEM(st, F32), pltpu.VMEM(st, F32)]
        + (sems if ns else []),
        compiler_params=_params(("arbitrary",)),
    )(u, dy, ckpt, wb, wct, tab, *scatter)
    return res[0], res[1], res[2], res[3], list(res[4:])


def _embed_blocks(t_re, t_im):
    g, p, n = t_re.shape
    gb = SSM_BLOCK // n
    eye = jnp.eye(gb, dtype=t_re.dtype)
    parts = [jnp.einsum('bgpn,gh->bgphn', t.reshape(g // gb, gb, p, n), eye).reshape(g // gb, gb * p, gb * n)
             for t in (t_re, t_im)]
    return jnp.concatenate(parts, axis=2)


def _scan_tables(pw_re, pw_im, hi_re, hi_im, rev):
    s = pw_re.shape[1] * pw_re.shape[2]
    nj = s // SCAN_LANES
    sub = np.arange(SUBLANES)
    live = np.ones((4 + KSTEPS, 1, SUBLANES, 1), bool)
    for row, dist in ((1, 1), (2, 2), (3, 4)):
        live[row, 0, :, 0] = (sub < SUBLANES - dist) if rev else (sub >= dist)

    def lay(pw, hi):
        rows = jnp.concatenate([pw[:1], pw[KSTEPS - 1:], hi, pw], axis=0).reshape(4 + KSTEPS, nj, 1, SCAN_LANES)
        return jnp.where(live, jnp.broadcast_to(rows, (4 + KSTEPS, nj, SUBLANES, SCAN_LANES)), 0.0)

    return jnp.concatenate([lay(pw_re, hi_re), lay(pw_im, hi_im)], axis=1)


def _adamw(w, g, m, v, tm):
    r, c = w.shape
    c1 = 1.0 - ADAM_B1 ** ADAM_STEP
    c2 = 1.0 - ADAM_B2 ** ADAM_STEP

    def body(w_ref, g_ref, m_ref, v_ref, d_ref, nm_ref, nv_ref):
        gg = g_ref[...]
        nm = ADAM_B1 * m_ref[...] + (1.0 - ADAM_B1) * gg
        nv = ADAM_B2 * v_ref[...] + (1.0 - ADAM_B2) * (gg * gg)
        nm_ref[...] = nm
        nv_ref[...] = nv
        d_ref[...] = -ADAM_LR * ((nm / c1) / (jnp.sqrt(nv / c2) + ADAM_EPS) + ADAM_WD * w_ref[...])

    spec = _row(tm, c)
    return pl.pallas_call(
        body, name="adamw", grid=(r // tm,), in_specs=[spec] * 4, out_specs=[spec] * 3,
        out_shape=[jax.ShapeDtypeStruct((r, c), F32)] * 3, compiler_params=_params(("parallel",)),
    )(w, g, m, v)


def _pair_sum(g42, got, core, out_dtype, tm, name):
    _, _, r, c = g42.shape

    def body(core_ref, a_ref, b_ref, o_ref):
        o_ref[...] = (a_ref[...] + b_ref[...]).astype(out_dtype)

    grid_spec = pltpu.PrefetchScalarGridSpec(
        num_scalar_prefetch=1, grid=(4, r // tm),
        in_specs=[pl.BlockSpec((1, None, tm, c), lambda s, i, core_ref: (s, core_ref[0], i, 0)),
                  pl.BlockSpec((1, tm, c), lambda s, i, core_ref: (s, i, 0))],
        out_specs=pl.BlockSpec((1, tm, c), lambda s, i, core_ref: (s, i, 0)))
    return pl.pallas_call(
        body, name=name, grid_spec=grid_spec, out_shape=jax.ShapeDtypeStruct((4, r, c), out_dtype),
        compiler_params=_params(("parallel", "parallel")),
    )(core, g42, got)


def _sum4(a, core, tm, name):
    _, r, c = a.shape

    def body(core_ref, a_ref, o_ref):
        o_ref[...] = ((a_ref[0].astype(F32) + a_ref[1].astype(F32)) + a_ref[2].astype(F32)) + a_ref[3].astype(F32)

    grid_spec = pltpu.PrefetchScalarGridSpec(
        num_scalar_prefetch=1, grid=(r // tm,),
        in_specs=[pl.BlockSpec((4, tm, c), lambda i, core_ref: (0, i, 0))],
        out_specs=pl.BlockSpec((None, tm, c), lambda i, core_ref: (core_ref[0], i, 0)))
    return pl.pallas_call(
        body, name=name, grid_spec=grid_spec, out_shape=jax.ShapeDtypeStruct((2, r, c), F32),
        compiler_params=_params(("parallel",)),
    )(core, a)


_ANY = pl.BlockSpec(memory_space=pl.ANY)


def _gather_phases(xs, outs, send_sems, recv_sems, local_sems):
    n = len(xs)

    def parts():
        x, y, c = lax.axis_index("x"), lax.axis_index("y"), lax.axis_index("c")
        return c, (x, y, c), (x, y, 1 - c), [(1 - x, y), (x, 1 - y), (1 - x, 1 - y)]

    def slot(t, px, py, pc):
        return outs[t].at[4 * px + 2 * py + pc]

    def copy(t, k, blk, to, src=None):
        return pltpu.make_async_remote_copy(
            src_ref=slot(t, *blk) if src is None else src, dst_ref=slot(t, *blk),
            send_sem=send_sems.at[t, k], recv_sem=recv_sems.at[t, k], device_id=to, device_id_type=MESH_ID)

    def own(t, me):
        return pltpu.make_async_copy(xs[t], slot(t, *me), local_sems.at[t])

    def first(t, c, me, sibling, chips):
        return [copy(t, 0, me, sibling, src=xs[t])] + [copy(t, 1 + j, me, (*chip, c), src=xs[t])
                                                       for j, chip in enumerate(chips)]

    def passed(t, c, sibling, chips):
        return [copy(t, 4 + j, (*chip, c), sibling) for j, chip in enumerate(chips)]

    def start():
        c, me, sibling, chips = parts()
        for t in range(n):
            own(t, me).start()
        for t in range(n):
            for cp in first(t, c, me, sibling, chips):
                cp.start()

    def forward():
        c, me, sibling, chips = parts()
        for j, chip in enumerate(chips):
            for t in range(n):
                copy(t, 1 + j, (*chip, c), me).wait_recv()
                passed(t, c, sibling, chips)[j].start()

    def finish():
        c, me, sibling, chips = parts()
        for t in range(n):
            copy(t, 0, sibling, me).wait_recv()
        for j, chip in enumerate(chips):
            for t in range(n):
                copy(t, 4 + j, (*chip, 1 - c), me).wait_recv()
        for t in range(n):
            for cp in first(t, c, me, sibling, chips) + passed(t, c, sibling, chips):
                cp.wait_send()
            own(t, me).wait()

    return start, forward, finish


def _all_gather8(blocks, name):
    n = len(blocks)

    def body(*refs):
        for phase in _gather_phases(refs[:n], refs[n:2 * n], *refs[2 * n:]):
            phase()

    return pl.pallas_call(
        body, name=name, out_shape=[jax.ShapeDtypeStruct((8,) + b.shape, b.dtype) for b in blocks],
        in_specs=[_ANY] * n, out_specs=[_ANY] * n,
        scratch_shapes=[pltpu.SemaphoreType.DMA((n, 7)), pltpu.SemaphoreType.DMA((n, 7)),
                        pltpu.SemaphoreType.DMA((n,))],
    )(*blocks)


def _pair_exchange(gs, name):
    n = len(gs)

    def body(*refs):
        g_refs, outs = refs[:n], refs[n:2 * n]
        send_sems, recv_sems = refs[2 * n:]
        x, y, c = lax.axis_index("x"), lax.axis_index("y"), lax.axis_index("c")
        cps = [pltpu.make_async_remote_copy(
            src_ref=g_refs[t].at[:, 1 - c], dst_ref=outs[t], send_sem=send_sems.at[t], recv_sem=recv_sems.at[t],
            device_id=(x, y, 1 - c), device_id_type=MESH_ID) for t in range(n)]
        for cp in cps:
            cp.start()
        for cp in cps:
            cp.wait()

    return pl.pallas_call(
        body, name=name,
        out_shape=[jax.ShapeDtypeStruct((g.shape[0],) + g.shape[2:], g.dtype) for g in gs],
        in_specs=[_ANY] * n, out_specs=[_ANY] * n,
        scratch_shapes=[pltpu.SemaphoreType.DMA((n,)), pltpu.SemaphoreType.DMA((n,))],
    )(*gs)


def _scatter_phases(p_refs, outs, send_sems, recv_sems, local_sems):
    n = len(p_refs)

    def parts():
        x, y, c = lax.axis_index("x"), lax.axis_index("y"), lax.axis_index("c")
        return c, 2 * x + y, [(1 - x, y), (x, 1 - y), (1 - x, 1 - y)]

    def copy(t, k, src_slab, dst_slab, chip, c):
        return pltpu.make_async_remote_copy(
            src_ref=p_refs[t].at[src_slab], dst_ref=outs[t].at[dst_slab], send_sem=send_sems.at[t, k],
            recv_sem=recv_sems.at[t, k], device_id=(*chip, c), device_id_type=MESH_ID)

    def own(t, mine):
        return pltpu.make_async_copy(p_refs[t].at[mine], outs[t].at[mine], local_sems.at[t])

    def start():
        c, mine, chips = parts()
        for t in range(n):
            own(t, mine).start()
        for k, (cx, cy) in enumerate(chips):
            for t in range(n):
                copy(t, k, 2 * cx + cy, mine, (cx, cy), c).start()

    def finish():
        c, mine, chips = parts()
        for k, (cx, cy) in enumerate(chips):
            for t in range(n):
                copy(t, k, mine, 2 * cx + cy, (cx, cy), c).wait_recv()
        for t in range(n):
            for k, (cx, cy) in enumerate(chips):
                copy(t, k, 2 * cx + cy, mine, (cx, cy), c).wait_send()
            own(t, mine).wait()

    return start, finish


def _chip_scatter(ps, name):
    n = len(ps)

    def body(*refs):
        for phase in _scatter_phases(refs[:n], refs[n:2 * n], *refs[2 * n:]):
            phase()

    return pl.pallas_call(
        body, name=name, out_shape=[jax.ShapeDtypeStruct(p.shape, p.dtype) for p in ps],
        in_specs=[_ANY] * n, out_specs=[_ANY] * n,
        scratch_shapes=[pltpu.SemaphoreType.DMA((n, 3)), pltpu.SemaphoreType.DMA((n, 3)),
                        pltpu.SemaphoreType.DMA((n,))],
    )(*ps)


def _pair_gather(rs, name):
    n = len(rs)

    def body(*refs):
        ins, outs = refs[:n], refs[n:2 * n]
        send_sems, recv_sems = refs[2 * n:]
        x, y, c = lax.axis_index("x"), lax.axis_index("y"), lax.axis_index("c")

        def copy(t, slab):
            return pltpu.make_async_remote_copy(
                src_ref=ins[t].at[slab], dst_ref=outs[t].at[slab], send_sem=send_sems.at[t],
                recv_sem=recv_sems.at[t], device_id=(x, y, 1 - c), device_id_type=MESH_ID)

        sends = [copy(t, c) for t in range(n)]
        for cp in sends:
            cp.start()
        for t in range(n):
            copy(t, 1 - c).wait_recv()
        for cp in sends:
            cp.wait_send()

    return pl.pallas_call(
        body, name=name, out_shape=[jax.ShapeDtypeStruct(r.shape, r.dtype) for r in rs],
        in_specs=[_ANY] * n, out_specs=[_ANY] * n, input_output_aliases={t: t for t in range(n)},
        scratch_shapes=[pltpu.SemaphoreType.DMA((n,)), pltpu.SemaphoreType.DMA((n,))],
    )(*rs)


PACK_COLS = 1024
BIG = (("meta_tokens", 1), ("w_in", 1), ("w_glu", 0), ("w_ssm_proj", 1), ("w_attn_proj", 0), ("w_out", 0),
       ("w_mlp_in", 1), ("w_mlp_out", 0))
SMALL = ("norm_mix_g", "ssm_a_re", "ssm_a_im", "ssm_log_dt", "ssm_b_re", "ssm_b_im", "ssm_c_re", "ssm_c_im",
         "ssm_d", "b_glu", "q_norm_g", "k_norm_g", "norm_mlp_g", "norm_final_g")


def _pad_rows(flat, mult_rows):
    n = flat.shape[0]
    unit = PACK_COLS * mult_rows
    total = -(-n // unit) * unit
    return jnp.pad(flat, (0, total - n)).reshape(total // PACK_COLS, PACK_COLS)


def _half(t, c):
    return lax.dynamic_slice_in_dim(t, c * (t.shape[0] // 2), t.shape[0] // 2, 0)


EARLY_WEIGHTS = ("meta_tokens", "w_in", "w_glu")
LATE_WEIGHTS = tuple(name for name, _ in BIG if name not in EARLY_WEIGHTS)


def _weight_blocks(shards, c, names):
    return [_half(shards[name], c) if name == "meta_tokens" else _half(shards[name], c).astype(BF16) for name in names]


def _shard_major(names, gathered):
    return {name: g.reshape((4, 2 * g.shape[1]) + g.shape[2:]) for name, g in zip(names, gathered)}


class _Reduction:
    def __init__(self, grads, wire, labels, c, tag):
        self.labels, self.wire, self.c, self.tag = labels, wire, c, tag
        self.core = c.astype(jnp.int32).reshape(1)
        g42 = [g.reshape(4, 2, g.shape[1] // 2, g.shape[2]) for g in grads]
        self.tiles = [_pick_tile(g.shape[2], 256, SUBLANES if dt == F32 else 2 * SUBLANES) for g, dt in zip(g42, wire)]
        got = _pair_exchange(g42, "grad_pair_exchange_" + tag)
        self.pair = [_pair_sum(g, o, self.core, dt, tm, "pair_sum_" + lb)
                     for g, o, dt, tm, lb in zip(g42, got, wire, self.tiles, labels)]

    def finish(self, by_src, gathered):
        red = [_sum4(b, self.core, tm, "chip_sum_" + lb) for b, tm, lb in zip(by_src, self.tiles, self.labels)]
        both = _pair_gather(red[:gathered], "grad_pair_gather_" + self.tag)
        pieces = [lax.dynamic_index_in_dim(r, self.c, 0, keepdims=False) for r in red[gathered:]]
        return [b.reshape(2 * b.shape[1], b.shape[2]) for b in both], pieces


def _small_as_shards(small_flat):
    unit = 8 * SUBLANES * PACK_COLS
    k = -(-small_flat.shape[0] // unit) * unit
    return jnp.pad(small_flat, (0, k - small_flat.shape[0])).reshape(4, k // (4 * PACK_COLS), PACK_COLS)


def _to_chunk_order(a):
    lp = a.shape[0]
    rest = a.shape[1:]
    a = a.reshape((lp // CHUNK, SUBLANES, KSTEPS) + rest)
    return a.swapaxes(1, 2).reshape((lp,) + rest)


def _from_chunk_order(a):
    lp = a.shape[0]
    rest = a.shape[1:]
    a = a.reshape((lp // CHUNK, KSTEPS, SUBLANES) + rest)
    return a.swapaxes(1, 2).reshape((lp,) + rest)


def _rope_tables(l_total, lp):
    n_real = l_total - N_META
    pos = np.arange(n_real)
    row_id = (pos // GRID_W).astype(np.float32)
    col_id = (pos % GRID_W).astype(np.float32)
    ppa = HEAD_DIM // 4
    inv_freq = (ROPE_THETA ** (-np.arange(ppa, dtype=np.float64) / ppa)).astype(np.float32)
    ang = np.concatenate([row_id[:, None] * inv_freq, col_id[:, None] * inv_freq], axis=-1)
    ang = np.concatenate([np.zeros((N_META, HEAD_DIM // 2), np.float32), ang,
                          np.zeros((lp - l_total, HEAD_DIM // 2), np.float32)], axis=0).astype(np.float64)
    cos = np.repeat(np.cos(ang), 2, axis=1)
    sin = np.repeat(np.sin(ang), 2, axis=1) * np.tile(np.asarray([-1.0, 1.0]), HEAD_DIM // 2)
    reps = (1, LANES // HEAD_DIM)
    return np.tile(cos, reps).astype(np.float32), np.tile(sin, reps).astype(np.float32)


def kernel(x, meta_tokens, norm_mix_g, w_in, ssm_a_re, ssm_a_im, ssm_log_dt, ssm_b_re, ssm_b_im, ssm_c_re, ssm_c_im, ssm_d, w_glu, b_glu, q_norm_g, k_norm_g, w_ssm_proj, w_attn_proj, w_out, norm_mlp_g, w_mlp_in, w_mlp_out, norm_final_g, loss_target, m_meta_tokens, m_norm_mix_g, m_w_in, m_ssm_a_re, m_ssm_a_im, m_ssm_log_dt, m_ssm_b_re, m_ssm_b_im, m_ssm_c_re, m_ssm_c_im, m_ssm_d, m_w_glu, m_b_glu, m_q_norm_g, m_k_norm_g, m_w_ssm_proj, m_w_attn_proj, m_w_out, m_norm_mlp_g, m_w_mlp_in, m_w_mlp_out, m_norm_final_g, v_meta_tokens, v_norm_mix_g, v_w_in, v_ssm_a_re, v_ssm_a_im, v_ssm_log_dt, v_ssm_b_re, v_ssm_b_im, v_ssm_c_re, v_ssm_c_im, v_ssm_d, v_w_glu, v_b_glu, v_q_norm_g, v_k_norm_g, v_w_ssm_proj, v_w_attn_proj, v_w_out, v_norm_mlp_g, v_w_mlp_in, v_w_mlp_out, v_norm_final_g):
    args = dict(locals())
    names = list(dict.fromkeys([n for n, _ in BIG] + list(SMALL)))
    order = ['meta_tokens', 'norm_mix_g', 'w_in', 'ssm_a_re', 'ssm_a_im', 'ssm_log_dt', 'ssm_b_re', 'ssm_b_im',
             'ssm_c_re', 'ssm_c_im', 'ssm_d', 'w_glu', 'b_glu', 'q_norm_g', 'k_norm_g', 'w_ssm_proj', 'w_attn_proj',
             'w_out', 'norm_mlp_g', 'w_mlp_in', 'w_mlp_out', 'norm_final_g']
    assert sorted(names) == sorted(order)
    c_idx = lax.axis_index("c")

    seq, d = x.shape[1], x.shape[2]
    l_total = seq + N_META
    lp = -(-l_total // SEQ_ALIGN) * SEQ_ALIGN
    hd = d // 2
    n_groups = hd // SSM_GROUP
    n_state = n_groups * SSM_STATE
    nj = n_state // SCAN_LANES
    kvh = d // HEAD_DIM // GQA_REP

    shard2d = {}
    for name, _ in BIG:
        t = args[name]
        shard2d[name] = t.reshape(t.shape[-2], t.shape[-1])
    full = _shard_major(EARLY_WEIGHTS, _all_gather8(_weight_blocks(shard2d, c_idx, EARLY_WEIGHTS), "weight_all_gather"))
    meta_full = jnp.transpose(full["meta_tokens"], (1, 0, 2)).reshape(N_META, d)
    w_in4 = full["w_in"]
    w_glu_f = full["w_glu"].reshape(hd, hd)

    xin = jnp.concatenate([meta_full, x[0], jnp.zeros((lp - l_total, d), F32)], axis=0)
    xin = _to_chunk_order(xin)
    tgt = _to_chunk_order(jnp.pad(loss_target[0], ((N_META, lp - l_total), (0, 0))))
    pos = np.arange(lp)
    rowmask = jnp.asarray(_to_chunk_order(((pos >= N_META) & (pos < l_total)).astype(np.float32)[:, None]))
    kbias = jnp.asarray(_to_chunk_order(np.where(pos < l_total, 0.0, MASK_VALUE).astype(np.float32)[:, None])
                        .reshape(1, lp))
    cos_t, sin_t = (jnp.asarray(_to_chunk_order(t)) for t in _rope_tables(l_total, lp))
    mean_m, sel = _head_tables(d)

    tm = _pick_tile(lp, 320)
    tm_mid = _pick_tile(lp, 384)
    tm_big = _pick_tile(lp, 640)
    tq = _pick_tile(lp, ATTN_Q_TILE, LANES)
    tk = _pick_tile(lp, ATTN_K_TILE, MXU_DIM)
    assert lp - CHUNK <= (l_total // CHUNK) * CHUNK and tk >= CHUNK
    g_mix = norm_mix_g.reshape(1, d)
    g_mlp = norm_mlp_g.reshape(1, d)
    g_fin = norm_final_g.reshape(1, d)
    qg = jnp.tile(q_norm_g.reshape(1, HEAD_DIM), (1, LANES // HEAD_DIM))
    kg = jnp.tile(k_norm_g.reshape(1, HEAD_DIM), (1, LANES // HEAD_DIM))
    dskip = ssm_d.reshape(1, hd)
    bglu = b_glu.reshape(1, hd)

    a_re, a_im = ssm_a_re[0], ssm_a_im[0]
    log_dt = ssm_log_dt[0][..., None]
    bt_re = jnp.swapaxes(ssm_b_re[0], 2, 3)
    bt_im = jnp.swapaxes(ssm_b_im[0], 2, 3)
    bb_re, bb_im, pw_re, pw_im, hi_re, hi_im = _ssm_discretize(a_re, a_im, log_dt, bt_re, bt_im)
    wb = [_embed_blocks(bb_re[i], bb_im[i]).astype(BF16) for i in range(2)]
    wct = [_embed_blocks(ssm_c_re[0, i], -ssm_c_im[0, i]).astype(BF16) for i in range(2)]
    tabs = [_scan_tables(pw_re[i], pw_im[i], hi_re[i], hi_im[i], rev=(i == 1)) for i in range(2)]
    tabs_adj = [_scan_tables(pw_re[i], pw_im[i], hi_re[i], hi_im[i], rev=(i == 0)) for i in range(2)]

    u, qkv, gates, hb = _in_proj(xin, g_mix, w_in4, tm_mid)
    y0, ck0 = _ssm_fwd(u, wb[0], wct[0], tabs[0], False, "ssm_fwd_0")
    y1, ck1 = _ssm_fwd(u, wb[1], wct[1], tabs[1], True, "ssm_fwd_1")
    yssm = _glu_fwd(u, y0, y1, dskip, w_glu_f, bglu, tm_big)
    q, k, v = _qk_prep(qkv, cos_t, sin_t, qg, kg, mean_m, tm)
    o, lse, pt, mblk, late = _attn_fwd(q, k, v, kbias, tq, tk, gather=_weight_blocks(shard2d, c_idx, LATE_WEIGHTS))
    full = _shard_major(LATE_WEIGHTS, late)
    w_mlp_in4 = full["w_mlp_in"]
    w_ssm_proj4 = full["w_ssm_proj"]
    w_attn_proj_f = full["w_attn_proj"].reshape(d, d)
    w_out_f = full["w_out"].reshape(d, d)
    w_mlp_out_f = full["w_mlp_out"].reshape(4 * d, d)
    h1, merged = _merge_fwd(yssm, o, gates, xin, w_ssm_proj4, w_attn_proj_f, w_out_f, tm_mid)
    r, h2b = _mlp_in(h1, g_mlp, w_mlp_in4, tm_mid)
    loss_tile, dh3, d_gfin = _mlp_out_loss(h1, r, w_mlp_out_f, g_fin, tgt, rowmask, tm_mid)

    dz, dh3b = _mlp_bwd_a(dh3, r, w_mlp_out_f, tm_mid)
    dh1, d_gmlp = _mlp_bwd_b(dz, dh3, h1, g_mlp, w_mlp_in4, tm_mid)
    dgates, dms, dma, dyssm, do, delta, dh1b = _merge_bwd(dh1, yssm, o, gates, w_ssm_proj4, w_attn_proj_f, w_out_f,
                                                          sel, tm_mid)
    dyv, d_wglu, d_bglu, d_dskip = _glu_bwd(dyssm, u, y0, y1, dskip, w_glu_f, bglu, tm_big)

    tn = min(d, 1024)
    tm_w = _pick_tile(lp, 3 * MXU_DIM, MXU_DIM)
    grads4 = {
        "w_mlp_in": _wgrad(h2b, dz, 4, tm_w, tn, "wgrad_mlp_in"),
        "w_mlp_out": _wgrad(r, dh3b, 1, tm_w, min(d, 512), "wgrad_mlp_out", square=True).reshape(4, d, d),
        "w_out": _wgrad(merged, dh1b, 1, tm_w, tn, "wgrad_out").reshape(4, d // 4, d),
        "w_attn_proj": _wgrad(o, dma, 1, tm_w, tn, "wgrad_attn_proj").reshape(4, d // 4, d),
        "w_ssm_proj": _wgrad(yssm, dms, 4, tm_w, d // 4, "wgrad_ssm_proj"),
        "w_glu": d_wglu.reshape(4, hd // 4, hd),
    }
    first_names = list(grads4)
    first = _Reduction([grads4[n] for n in first_names], [BF16] * len(first_names), first_names, c_idx, "first")

    du0, dbb0, dcc0, dlb0, first_by_src = _ssm_bwd(u, dyv, ck0, wb[0], wct[0], _both(tabs[0], tabs_adj[0]), False,
                                                   "ssm_bwd_0", scatter=first.pair)
    du1, dbb1, dcc1, dlb1, _ = _ssm_bwd(u, dyv, ck1, wb[1], wct[1], _both(tabs[1], tabs_adj[1]), True, "ssm_bwd_1")
    dq, dk, dv = _attn_bwd(q, k, v, pt, mblk, do, lse, delta, _pick_tile(lp, MXU_DIM, LANES), tk)
    dqkv, d_qg, d_kg = _qk_bwd(qkv, dq, dk, dv, cos_t, sin_t, qg, kg, mean_m, tm)
    dxin, d_gmix, dproj = _in_proj_bwd(dyv, du0, du1, dskip, dqkv, dgates, dh1, xin, g_mix, w_in4, tm)
    red_big = dict(zip(first_names, first.finish(first_by_src, len(first_names))[0]))

    grads4["w_in"] = _wgrad(hb, dproj, 4, tm_w, tn, "wgrad_in")
    dx_nat = _from_chunk_order(dxin)
    grads4["meta_tokens"] = jnp.swapaxes(dx_nat[:N_META].reshape(N_META, 4, d // 4), 0, 1)
    grad_x = dx_nat[N_META:l_total][None]

    dlb = jnp.stack([dlb0, dlb1])[:, :, 0, :]
    dlb_re = dlb[:, :nj].reshape(2, n_groups, SSM_STATE)
    dlb_im = dlb[:, nj:].reshape(2, n_groups, SSM_STATE)
    gpn = (2, 2, n_groups, SSM_GROUP, SSM_STATE)
    dbb = jnp.stack([dbb0, dbb1]).reshape(gpn)
    dcc = jnp.stack([dcc0, dcc1]).reshape(gpn)
    d_are, d_aim, d_logdt, d_btre, d_btim = _ssm_param_bwd(a_re, a_im, log_dt, bt_re, bt_im, dlb_re, dlb_im,
                                                           dbb[:, 0], dbb[:, 1])
    small_grads = {
        "norm_mix_g": d_gmix, "ssm_a_re": d_are, "ssm_a_im": d_aim, "ssm_log_dt": d_logdt,
        "ssm_b_re": jnp.swapaxes(d_btre, 2, 3), "ssm_b_im": jnp.swapaxes(d_btim, 2, 3),
        "ssm_c_re": dcc[:, 0], "ssm_c_im": -dcc[:, 1],
        "ssm_d": d_dskip, "b_glu": d_bglu, "q_norm_g": d_qg[:, :HEAD_DIM], "k_norm_g": d_kg[:, :HEAD_DIM],
        "norm_mlp_g": d_gmlp, "norm_final_g": d_gfin,
    }
    small_flat = jnp.concatenate([small_grads[n].reshape(-1) for n in SMALL] + [loss_tile[0, :1]])

    last = _Reduction([grads4["meta_tokens"], grads4["w_in"], _small_as_shards(small_flat)], [F32, BF16, F32],
                      ["meta_tokens", "w_in", "small"], c_idx, "last")
    (red_big["meta_tokens"], red_big["w_in"]), (small_piece,) = last.finish(
        _chip_scatter(last.pair, "grad_chip_scatter"), 2)
    red_small = _all_gather8([small_piece], "small_grad_all_gather")[0].reshape(-1)[:small_flat.shape[0]]
    loss, red_small = red_small[-1], red_small[:-1]
    grad, delta_w, new_m, new_v = {}, {}, {}, {}
    for name, _ in BIG:
        w2 = shard2d[name]
        shp = args[name].shape
        g2 = red_big[name]
        t = _pick_tile(w2.shape[0], 256, 8)
        dl, nm, nv = _adamw(w2, g2, args["m_" + name].reshape(w2.shape), args["v_" + name].reshape(w2.shape), t)
        grad[name], delta_w[name], new_m[name], new_v[name] = (a.reshape(shp) for a in (g2, dl, nm, nv))

    def pack_small(prefix):
        flat = jnp.concatenate([args[prefix + n].reshape(-1) for n in SMALL])
        return _pad_rows(flat, SUBLANES)

    n_small = red_small.shape[0]
    gs = _pad_rows(red_small, SUBLANES)
    dl, nm, nv = _adamw(pack_small(""), gs, pack_small("m_"), pack_small("v_"), _pick_tile(gs.shape[0], 256, 8))
    off = 0
    for name in SMALL:
        shp = args[name].shape
        k = int(np.prod(shp))
        for dst, src in ((grad, gs), (delta_w, dl), (new_m, nm), (new_v, nv)):
            dst[name] = src.reshape(-1)[off:off + k].reshape(shp)
        off += k
    assert off == n_small

    return (loss, grad_x, *[grad[n] for n in order], *[delta_w[n] for n in order],
            *[new_m[n] for n in order], *[new_v[n] for n in order])


def _both(tab, tab_adj):
    return jnp.concatenate([tab, tab_adj], axis=0)
```

```python
import functools
import math

import numpy as np
import jax
import jax.numpy as jnp
from jax import lax
from jax.experimental import pallas as pl
from jax.experimental.pallas import tpu as pltpu

F32 = jnp.float32
BF16 = jnp.bfloat16

N_META = 16
GRID_W = 64
HEAD_DIM = 64
GQA_REP = 4
SSM_GROUP = 16
SSM_STATE = 64
ROPE_THETA = 10000.0
NORM_EPS = 1e-6
EIG_RE_MAX = -1e-4
ADAM_LR, ADAM_B1, ADAM_B2, ADAM_EPS, ADAM_WD, ADAM_STEP = 0.001, 0.9, 0.999, 1e-08, 0.01, 10

SUBLANES = 8
LANES = 128
CHUNK = 256
KSTEPS = CHUNK // SUBLANES
SCAN_LANES = 512
MXU_DIM = 256
SSM_BLOCK = MXU_DIM
SEQ_ALIGN = MXU_DIM
ATTN_Q_TILE = 384
ATTN_K_TILE = 11 * MXU_DIM
VMEM_LIMIT = 56 << 20
MASK_VALUE = -1e30
MESH_ID = pl.DeviceIdType.MESH


def _dot(a, b):
    return jnp.dot(a, b, preferred_element_type=F32)


def _dot_nt(a, b):
    return lax.dot_general(a, b, (((1,), (1,)), ((), ())), preferred_element_type=F32)


def _dot_tn(a, b):
    return lax.dot_general(a, b, (((0,), (0,)), ((), ())), preferred_element_type=F32)


def _row(tm, width):
    return pl.BlockSpec((tm, width), lambda i: (i, 0))


def _full(shape):
    nd = len(shape)
    return pl.BlockSpec(shape, lambda i: (0,) * nd)


def _params(sem):
    return pltpu.CompilerParams(dimension_semantics=sem, vmem_limit_bytes=VMEM_LIMIT)


def _pick_tile(n, cap, mult=16):
    best = None
    for t in range(mult, min(n, cap) + 1, mult):
        if n % t == 0:
            best = t
    assert best is not None, (n, cap)
    return best


def _rstd(x):
    return lax.rsqrt(jnp.mean(x * x, axis=-1, keepdims=True) + NORM_EPS)


def _rms(x, g):
    return x * _rstd(x) * g


def _rms_bwd(dy, x, g):
    r = _rstd(x)
    xh = x * r
    gdy = dy * g
    dx = r * (gdy - xh * jnp.mean(gdy * xh, axis=-1, keepdims=True))
    return dx, dy * xh


def _split_dot(x, m):
    hi = x.astype(BF16)
    lo = (x - hi.astype(F32)).astype(BF16)
    return _dot(hi, m) + _dot(lo, m)


def _sigmoid(x):
    return 1.0 / (1.0 + jnp.exp(-x))


def _acc_rows(ref, val, first):
    s = jnp.sum(val, axis=0, keepdims=True)

    @pl.when(first)
    def _():
        ref[...] = s

    @pl.when(jnp.logical_not(first))
    def _():
        ref[...] += s


def _in_proj(xin, g, w4, tm):
    lp, d = xin.shape
    hd = d // 2

    def body(x_ref, g_ref, w_ref, u_ref, qkv_ref, gt_ref, h_ref):
        h = _rms(x_ref[...], g_ref[...]).astype(BF16)
        h_ref[...] = h
        p0 = _dot(h, w_ref[0])
        u_ref[...] = p0[:, :hd]
        qkv_ref[:, :hd] = p0[:, hd:]
        qkv_ref[:, hd:] = _dot(h, w_ref[1])
        gt_ref[:, :d] = _dot(h, w_ref[2])
        gt_ref[:, d:] = _dot(h, w_ref[3])

    return pl.pallas_call(
        body, name="in_proj", grid=(lp // tm,),
        in_specs=[_row(tm, d), _full((1, d)), _full((4, d, d))],
        out_specs=[_row(tm, hd), _row(tm, 3 * hd), _row(tm, 2 * d), _row(tm, d)],
        out_shape=[jax.ShapeDtypeStruct((lp, hd), F32), jax.ShapeDtypeStruct((lp, 3 * hd), F32),
                   jax.ShapeDtypeStruct((lp, 2 * d), F32), jax.ShapeDtypeStruct((lp, d), BF16)],
        compiler_params=_params(("parallel",)),
    )(xin, g, w4)


def _gelu(y):
    return 0.5 * y * (1.0 + lax.erf(y * (1.0 / math.sqrt(2.0))))


def _gelu_grad(y):
    return 0.5 * (1.0 + lax.erf(y * (1.0 / math.sqrt(2.0)))) + y * jnp.exp(-0.5 * y * y) * (1.0 / math.sqrt(2.0 * math.pi))


def _glu_fwd(u, y0, y1, dskip, w_glu, b_glu, tm):
    lp, w = u.shape

    def body(u_ref, y0_ref, y1_ref, d_ref, w_ref, b_ref, o_ref):
        y = u_ref[...] * d_ref[...] + y0_ref[...] + y1_ref[...]
        z = _gelu(y)
        t = _dot(z.astype(BF16), w_ref[...]) + b_ref[...]
        o_ref[...] = (z * _sigmoid(t)).astype(BF16)

    return pl.pallas_call(
        body, name="glu_fwd", grid=(lp // tm,),
        in_specs=[_row(tm, w), _row(tm, w), _row(tm, w), _full((1, w)), _full((w, w)), _full((1, w))],
        out_specs=_row(tm, w), out_shape=jax.ShapeDtypeStruct((lp, w), BF16),
        compiler_params=_params(("parallel",)),
    )(u, y0, y1, dskip, w_glu, b_glu)


def _glu_bwd(dyssm, u, y0, y1, dskip, w_glu, b_glu, tm):
    lp, w = u.shape

    def body(g_ref, u_ref, y0_ref, y1_ref, d_ref, w_ref, b_ref, dy_ref, dw_ref, db_ref, dd_ref):
        first = pl.program_id(0) == 0
        uu = u_ref[...]
        y = uu * d_ref[...] + y0_ref[...] + y1_ref[...]
        z = _gelu(y)
        zb = z.astype(BF16)
        sg = _sigmoid(_dot(zb, w_ref[...]) + b_ref[...])
        g = g_ref[...]
        dt = g * z * sg * (1.0 - sg)
        dtb = dt.astype(BF16)
        dz = g * sg + _dot_nt(dtb, w_ref[...])
        dy = dz * _gelu_grad(y)
        dy_ref[...] = dy
        dw = _dot_tn(zb, dtb)

        @pl.when(first)
        def _():
            dw_ref[...] = dw

        @pl.when(jnp.logical_not(first))
        def _():
            dw_ref[...] += dw

        _acc_rows(db_ref, dt, first)
        _acc_rows(dd_ref, dy * uu, first)

    return pl.pallas_call(
        body, name="glu_bwd", grid=(lp // tm,),
        in_specs=[_row(tm, w), _row(tm, w), _row(tm, w), _row(tm, w), _full((1, w)), _full((w, w)), _full((1, w))],
        out_specs=[_row(tm, w), _full((w, w)), _full((1, w)), _full((1, w))],
        out_shape=[jax.ShapeDtypeStruct((lp, w), F32), jax.ShapeDtypeStruct((w, w), F32),
                   jax.ShapeDtypeStruct((1, w), F32), jax.ShapeDtypeStruct((1, w), F32)],
        compiler_params=_params(("arbitrary",)),
    )(dyssm, u, y0, y1, dskip, w_glu, b_glu)


def _merge_fwd(yssm, o, gates, xin, wsp4, wap, wo, tm):
    lp, d = xin.shape
    w = yssm.shape[1]
    ns = d // 4

    def body(y_ref, o_ref, g_ref, x_ref, wsp_ref, wap_ref, wo_ref, h_ref, m_ref):
        yb = y_ref[...]
        ms = jnp.concatenate([_dot(yb, wsp_ref[s]) for s in range(4)], axis=1)
        ma = _dot(o_ref[...], wap_ref[...])
        merged = (_sigmoid(g_ref[:, :d]) * ms + _sigmoid(g_ref[:, d:]) * ma).astype(BF16)
        m_ref[...] = merged
        h_ref[...] = x_ref[...] + _dot(merged, wo_ref[...])

    return pl.pallas_call(
        body, name="merge_fwd", grid=(lp // tm,),
        in_specs=[_row(tm, w), _row(tm, d), _row(tm, 2 * d), _row(tm, d),
                  _full((4, w, ns)), _full((d, d)), _full((d, d))],
        out_specs=[_row(tm, d), _row(tm, d)],
        out_shape=[jax.ShapeDtypeStruct((lp, d), F32), jax.ShapeDtypeStruct((lp, d), BF16)],
        compiler_params=_params(("parallel",)),
    )(yssm, o, gates, xin, wsp4, wap, wo)


def _merge_bwd(dh1, yssm, o, gates, wsp4, wap, wo, sel, tm):
    lp, d = dh1.shape
    w = yssm.shape[1]
    ns = d // 4
    nsel = sel.shape[1]

    def body(dh_ref, y_ref, o_ref, g_ref, wsp_ref, wap_ref, wo_ref, sel_ref,
             dg_ref, dms_ref, dma_ref, dy_ref, do_ref, dl_ref, dhb_ref):
        dhb = dh_ref[...].astype(BF16)
        dhb_ref[...] = dhb
        dm = _dot_nt(dhb, wo_ref[...])
        yb = y_ref[...]
        ob = o_ref[...]
        ms = jnp.concatenate([_dot(yb, wsp_ref[s]) for s in range(4)], axis=1)
        ma = _dot(ob, wap_ref[...])
        ss = _sigmoid(g_ref[:, :d])
        sa = _sigmoid(g_ref[:, d:])
        dg_ref[:, :d] = (dm * ms * ss * (1.0 - ss)).astype(BF16)
        dg_ref[:, d:] = (dm * ma * sa * (1.0 - sa)).astype(BF16)
        dms = (dm * ss).astype(BF16)
        dma = (dm * sa).astype(BF16)
        dms_ref[...] = dms
        dma_ref[...] = dma
        dy = _dot_nt(dms[:, :ns], wsp_ref[0])
        for s in range(1, 4):
            dy += _dot_nt(dms[:, s * ns:(s + 1) * ns], wsp_ref[s])
        dy_ref[...] = dy
        do = _dot_nt(dma, wap_ref[...])
        do_ref[...] = do.astype(BF16)
        dl_ref[...] = _split_dot(do * ob.astype(F32), sel_ref[...])

    return pl.pallas_call(
        body, name="merge_bwd", grid=(lp // tm,),
        in_specs=[_row(tm, d), _row(tm, w), _row(tm, d), _row(tm, 2 * d),
                  _full((4, w, ns)), _full((d, d)), _full((d, d)), _full((d, nsel))],
        out_specs=[_row(tm, 2 * d), _row(tm, d), _row(tm, d), _row(tm, w), _row(tm, d), _row(tm, nsel), _row(tm, d)],
        out_shape=[jax.ShapeDtypeStruct((lp, 2 * d), BF16), jax.ShapeDtypeStruct((lp, d), BF16),
                   jax.ShapeDtypeStruct((lp, d), BF16), jax.ShapeDtypeStruct((lp, w), F32),
                   jax.ShapeDtypeStruct((lp, d), BF16), jax.ShapeDtypeStruct((lp, nsel), F32),
                   jax.ShapeDtypeStruct((lp, d), BF16)],
        compiler_params=_params(("parallel",)),
    )(dh1, yssm, o, gates, wsp4, wap, wo, sel)


def _mlp_in(h1, g, w4, tm):
    lp, d = h1.shape

    def body(x_ref, g_ref, w_ref, r_ref, h_ref):
        h = _rms(x_ref[...], g_ref[...]).astype(BF16)
        h_ref[...] = h
        for s in range(4):
            r_ref[:, s * d:(s + 1) * d] = jnp.maximum(_dot(h, w_ref[s]), 0.0).astype(BF16)

    return pl.pallas_call(
        body, name="mlp_in", grid=(lp // tm,),
        in_specs=[_row(tm, d), _full((1, d)), _full((4, d, d))],
        out_specs=[_row(tm, 4 * d), _row(tm, d)],
        out_shape=[jax.ShapeDtypeStruct((lp, 4 * d), BF16), jax.ShapeDtypeStruct((lp, d), BF16)],
        compiler_params=_params(("parallel",)),
    )(h1, g, w4)


def _square_bf16(r):
    rf = r.astype(F32)
    return (rf * rf).astype(BF16)


def _mlp_out_loss(h1, r, w2, g, tgt, rowmask, tm):
    lp, d = h1.shape
    ff = r.shape[1]

    def body(h_ref, r_ref, w_ref, g_ref, t_ref, m_ref, loss_ref, dx_ref, dg_ref, dz_ref, dhb_ref):
        first = pl.program_id(0) == 0
        x = h_ref[...] + _dot(_square_bf16(r_ref[...]), w_ref[...])
        gg = g_ref[...]
        err = (_rms(x, gg) - t_ref[...]) * m_ref[...]
        part = 0.5 * jnp.sum(jnp.sum(err * err, axis=1, keepdims=True), axis=0, keepdims=True) * (1.0 / d)
        part = jnp.broadcast_to(part, (SUBLANES, LANES))

        @pl.when(first)
        def _():
            loss_ref[...] = part

        @pl.when(jnp.logical_not(first))
        def _():
            loss_ref[...] += part

        dx, dgr = _rms_bwd(err * (1.0 / d), x, gg)
        dx_ref[...] = dx
        _acc_rows(dg_ref, dgr, first)
        dhb = dx.astype(BF16)
        dhb_ref[...] = dhb
        dz_ref[...] = (_dot_nt(dhb, w_ref[...]) * (2.0 * r_ref[...].astype(F32))).astype(BF16)

    return pl.pallas_call(
        body, name="mlp_out_loss", grid=(lp // tm,),
        in_specs=[_row(tm, d), _row(tm, ff), _full((ff, d)), _full((1, d)), _row(tm, d), _row(tm, 1)],
        out_specs=[_full((SUBLANES, LANES)), _row(tm, d), _full((1, d)), _row(tm, ff), _row(tm, d)],
        out_shape=[jax.ShapeDtypeStruct((SUBLANES, LANES), F32), jax.ShapeDtypeStruct((lp, d), F32),
                   jax.ShapeDtypeStruct((1, d), F32), jax.ShapeDtypeStruct((lp, ff), BF16),
                   jax.ShapeDtypeStruct((lp, d), BF16)],
        compiler_params=_params(("arbitrary",)),
    )(h1, r, w2, g, tgt, rowmask)


def _mlp_bwd_b(dz, dh3, h1, g, w4, tm):
    lp, d = h1.shape

    def body(dz_ref, dh_ref, x_ref, g_ref, w_ref, dx_ref, dg_ref):
        first = pl.program_id(0) == 0
        dh2 = _dot_nt(dz_ref[:, :d], w_ref[0])
        for s in range(1, 4):
            dh2 += _dot_nt(dz_ref[:, s * d:(s + 1) * d], w_ref[s])
        dx, dgr = _rms_bwd(dh2, x_ref[...], g_ref[...])
        dx_ref[...] = dh_ref[...] + dx
        _acc_rows(dg_ref, dgr, first)

    return pl.pallas_call(
        body, name="mlp_bwd_b", grid=(lp // tm,),
        in_specs=[_row(tm, 4 * d), _row(tm, d), _row(tm, d), _full((1, d)), _full((4, d, d))],
        out_specs=[_row(tm, d), _full((1, d))],
        out_shape=[jax.ShapeDtypeStruct((lp, d), F32), jax.ShapeDtypeStruct((1, d), F32)],
        compiler_params=_params(("arbitrary",)),
    )(dz, dh3, h1, g, w4)


def _in_proj_bwd(dyv, du0, du1, dskip, dqkv, dgates, dres, xin, g, w4, tm):
    lp, d = xin.shape
    hd = d // 2

    def body(dy_ref, a_ref, b_ref, ds_ref, dq_ref, dgt_ref, dr_ref, x_ref, g_ref, w_ref, dx_ref, dg_ref, dp_ref):
        first = pl.program_id(0) == 0
        du = (dy_ref[...] * ds_ref[...] + a_ref[...] + b_ref[...]).astype(BF16)
        dq = dq_ref[...].astype(BF16)
        dgt = dgt_ref[...].astype(BF16)
        dp_ref[:, :hd] = du
        dp_ref[:, hd:2 * d] = dq
        dp_ref[:, 2 * d:] = dgt
        dh = _dot_nt(du, w_ref[0, :, :hd]) + _dot_nt(dq[:, :hd], w_ref[0, :, hd:])
        dh += _dot_nt(dq[:, hd:], w_ref[1])
        dh += _dot_nt(dgt[:, :d], w_ref[2]) + _dot_nt(dgt[:, d:], w_ref[3])
        dx, dgr = _rms_bwd(dh, x_ref[...], g_ref[...])
        dx_ref[...] = dr_ref[...] + dx
        _acc_rows(dg_ref, dgr, first)

    return pl.pallas_call(
        body, name="in_proj_bwd", grid=(lp // tm,),
        in_specs=[_row(tm, hd), _row(tm, hd), _row(tm, hd), _full((1, hd)), _row(tm, 3 * hd), _row(tm, 2 * d),
                  _row(tm, d), _row(tm, d), _full((1, d)), _full((4, d, d))],
        out_specs=[_row(tm, d), _full((1, d)), _row(tm, 4 * d)],
        out_shape=[jax.ShapeDtypeStruct((lp, d), F32), jax.ShapeDtypeStruct((1, d), F32),
                   jax.ShapeDtypeStruct((lp, 4 * d), BF16)],
        compiler_params=_params(("arbitrary",)),
    )(dyv, du0, du1, dskip, dqkv, dgates, dres, xin, g, w4)


def _wgrad(a, dy, nshard, tm, tn, name, square=False):
    lp, k = a.shape
    n = dy.shape[1]
    ns = n // nshard
    assert ns % tn == 0
    per = ns // tn

    def body(a_ref, dy_ref, o_ref):
        i = pl.program_id(1)
        acc = _dot_tn(_square_bf16(a_ref[...]) if square else a_ref[...], dy_ref[...])

        @pl.when(i == 0)
        def _():
            o_ref[0] = acc

        @pl.when(i != 0)
        def _():
            o_ref[0] += acc

    return pl.pallas_call(
        body, name=name, grid=(n // tn, lp // tm),
        in_specs=[pl.BlockSpec((tm, k), lambda j, i: (i, 0)), pl.BlockSpec((tm, tn), lambda j, i: (i, j))],
        out_specs=pl.BlockSpec((1, k, tn), lambda j, i: (j // per, 0, j % per)),
        out_shape=jax.ShapeDtypeStruct((nshard, k, ns), F32),
        compiler_params=_params(("parallel", "arbitrary")),
    )(a, dy)


def _head_tables(d):
    idx = np.arange(LANES)
    mean = (idx[:, None] // HEAD_DIM == idx[None, :] // HEAD_DIM).astype(np.float32) / HEAD_DIM
    n_heads = d // HEAD_DIM
    kvh = n_heads // GQA_REP
    c = np.arange(d)
    col = np.arange(kvh * LANES)
    head_of_col = (col // LANES) * GQA_REP + (col % LANES)
    sel = ((c[:, None] // HEAD_DIM == head_of_col[None, :]) & ((col % LANES) < GQA_REP)[None, :]).astype(np.float32)
    return jnp.asarray(mean, BF16), jnp.asarray(sel, BF16)


def _swap_pairs(y):
    lane = lax.broadcasted_iota(jnp.int32, y.shape, 1)
    return jnp.where(lane % 2 == 0, pltpu.roll(y, LANES - 1, 1), pltpu.roll(y, 1, 1))


def _qk_prep(qkv, cos_t, sin_t, qg, kg, mean_m, tm):
    lp, wq = qkv.shape
    d = wq * 2 // 3
    kvw = d // 4
    kvh = kvw // HEAD_DIM
    scale = HEAD_DIM ** -0.5

    def body(x_ref, c_ref, s_ref, qg_ref, kg_ref, m_ref, q_ref, k_ref, v_ref):
        cs, sn, mm = c_ref[...], s_ref[...], m_ref[...]
        for b in range((d + kvw) // LANES):
            x = x_ref[:, b * LANES:(b + 1) * LANES]
            gg = qg_ref[...] if b < d // LANES else kg_ref[...]
            y = x * lax.rsqrt(_split_dot(x * x, mm) + NORM_EPS) * gg
            out = y * cs + _swap_pairs(y) * sn
            if b < d // LANES:
                q_ref[:, b * LANES:(b + 1) * LANES] = (out * scale).astype(BF16)
            else:
                kb = b - d // LANES
                k_ref[2 * kb] = out[:, :HEAD_DIM].astype(BF16)
                k_ref[2 * kb + 1] = out[:, HEAD_DIM:].astype(BF16)
        ones = jnp.ones((tm, LANES - HEAD_DIM), BF16)
        for h in range(kvh):
            vh = x_ref[:, d + kvw + h * HEAD_DIM:d + kvw + (h + 1) * HEAD_DIM].astype(BF16)
            v_ref[h] = jnp.concatenate([vh, ones], axis=1)

    k_spec = pl.BlockSpec((kvh, tm, HEAD_DIM), lambda i: (0, i, 0))
    v_spec = pl.BlockSpec((kvh, tm, LANES), lambda i: (0, i, 0))
    return pl.pallas_call(
        body, name="qk_prep", grid=(lp // tm,),
        in_specs=[_row(tm, wq), _row(tm, LANES), _row(tm, LANES), _full((1, LANES)), _full((1, LANES)),
                  _full((LANES, LANES))],
        out_specs=[_row(tm, d), k_spec, v_spec],
        out_shape=[jax.ShapeDtypeStruct((lp, d), BF16), jax.ShapeDtypeStruct((kvh, lp, HEAD_DIM), BF16),
                   jax.ShapeDtypeStruct((kvh, lp, LANES), BF16)],
        compiler_params=_params(("parallel",)),
    )(qkv, cos_t, sin_t, qg, kg, mean_m)


def _qk_bwd(qkv, dq, dk, dv, cos_t, sin_t, qg, kg, mean_m, tm):
    lp, wq = qkv.shape
    d = wq * 2 // 3
    kvw = d // 4
    kvh = kvw // HEAD_DIM
    scale = HEAD_DIM ** -0.5

    def body(x_ref, dq_ref, dk_ref, dv_ref, c_ref, s_ref, qg_ref, kg_ref, m_ref, o_ref, dqg_ref, dkg_ref):
        first = pl.program_id(0) == 0
        cs, sn, mm = c_ref[...], s_ref[...], m_ref[...]
        sums = [None, None]
        for b in range((d + kvw) // LANES):
            is_q = b < d // LANES
            x = x_ref[:, b * LANES:(b + 1) * LANES]
            gg = qg_ref[...] if is_q else kg_ref[...]
            r = lax.rsqrt(_split_dot(x * x, mm) + NORM_EPS)
            nrm = x * r
            if is_q:
                dout = dq_ref[:, b * LANES:(b + 1) * LANES] * scale
            else:
                kb = b - d // LANES
                dout = jnp.concatenate([dk_ref[2 * kb], dk_ref[2 * kb + 1]], axis=1)
            dy = dout * cs + _swap_pairs(dout * sn)
            part = jnp.sum(dy * nrm, axis=0, keepdims=True)
            sums[0 if is_q else 1] = part if sums[0 if is_q else 1] is None else sums[0 if is_q else 1] + part
            dn = dy * gg
            o_ref[:, b * LANES:(b + 1) * LANES] = (r * (dn - nrm * _split_dot(dn * nrm, mm))).astype(BF16)
        for h in range(kvh):
            o_ref[:, d + kvw + h * HEAD_DIM:d + kvw + (h + 1) * HEAD_DIM] = dv_ref[h].astype(BF16)
        for ref, s in ((dqg_ref, sums[0]), (dkg_ref, sums[1])):
            s = s + pltpu.roll(s, HEAD_DIM, 1)

            @pl.when(first)
            def _(ref=ref, s=s):
                ref[...] = s

            @pl.when(jnp.logical_not(first))
            def _(ref=ref, s=s):
                ref[...] += s

    kv_spec = pl.BlockSpec((kvh, tm, HEAD_DIM), lambda i: (0, i, 0))
    return pl.pallas_call(
        body, name="qk_bwd", grid=(lp // tm,),
        in_specs=[_row(tm, wq), _row(tm, d), kv_spec, kv_spec, _row(tm, LANES), _row(tm, LANES),
                  _full((1, LANES)), _full((1, LANES)), _full((LANES, LANES))],
        out_specs=[_row(tm, wq), _full((1, LANES)), _full((1, LANES))],
        out_shape=[jax.ShapeDtypeStruct((lp, wq), BF16), jax.ShapeDtypeStruct((1, LANES), F32),
                   jax.ShapeDtypeStruct((1, LANES), F32)],
        compiler_params=_params(("arbitrary",)),
    )(qkv, dq, dk, dv, cos_t, sin_t, qg, kg, mean_m)


def _attn_fwd(q, k, v, kbias, tq, tk, gather=()):
    lp, d = q.shape
    kvh = k.shape[0]
    rw = GQA_REP * HEAD_DIM
    nk = lp // tk

    ng = len(gather)
    steps = kvh * (lp // tq) * nk

    def body(*refs):
        q_ref, k_ref, v_ref, kb_ref = refs[:4]
        o_ref, lse_ref, pt_ref, mb_ref = refs[4 + ng:8 + ng]
        m_s, acc_s = refs[8 + 2 * ng:10 + 2 * ng]
        j = pl.program_id(2)

        if ng:
            phases = _gather_phases(refs[4:4 + ng], refs[8 + ng:8 + 2 * ng], *refs[10 + 2 * ng:])
            step = (pl.program_id(0) * (lp // tq) + pl.program_id(1)) * nk + j
            for n, phase in enumerate(phases):
                pl.when(step == n * steps // 3)(phase)

        @pl.when(j == 0)
        def _():
            m_s[...] = jnp.full(m_s.shape, MASK_VALUE, F32)
            acc_s[...] = jnp.zeros(acc_s.shape, F32)

        def heads(masked):
            kk, vv = k_ref[0], v_ref[0]

            def scores(h):
                return _dot_nt(q_ref[:, h * HEAD_DIM:(h + 1) * HEAD_DIM], kk)

            def softmax(h, s):
                if masked:
                    s = jnp.concatenate([s[:, :tk - CHUNK], s[:, tk - CHUNK:] + kb_ref[:, tk - CHUNK:]], axis=1)
                m_prev = m_s[h]
                m_new = jnp.maximum(m_prev, jnp.max(s, axis=1, keepdims=True))
                m_s[h] = m_new
                p = jnp.exp(s - m_new[:, :1]).astype(BF16)
                pt_ref[h] = p
                return p, jnp.exp(m_prev - m_new), m_new

            def accumulate(h, p, alpha):
                acc_s[h] = acc_s[h] * alpha + _dot(p, vv)

            ss = [scores(h) for h in range(GQA_REP)]
            pa = [softmax(h, ss[h]) for h in range(GQA_REP)]
            for h in range(GQA_REP):
                accumulate(h, *pa[h][:2])
            lane = lax.broadcasted_iota(jnp.int32, (tq, LANES), 1)
            mb = jnp.zeros((tq, LANES), F32)
            for h in range(GQA_REP):
                mb = jnp.where(lane == h, pa[h][2], mb)
            mb_ref[0] = mb

        pl.when(j != nk - 1)(functools.partial(heads, False))
        pl.when(j == nk - 1)(functools.partial(heads, True))

        @pl.when(j == nk - 1)
        def _():
            lane = lax.broadcasted_iota(jnp.int32, (tq, LANES), 1)
            lse = jnp.zeros((tq, LANES), F32)
            outs = []
            for h in range(GQA_REP):
                acc = acc_s[h]
                den = pltpu.roll(acc, HEAD_DIM, 1)
                outs.append((acc / den)[:, :HEAD_DIM])
                lse = jnp.where(lane == h, m_s[h] + jnp.log(den), lse)
            o_ref[...] = jnp.concatenate(outs, axis=1).astype(BF16)
            lse_ref[...] = lse

    sems = [pltpu.SemaphoreType.DMA((ng, 7)), pltpu.SemaphoreType.DMA((ng, 7)), pltpu.SemaphoreType.DMA((ng,))]
    res = pl.pallas_call(
        body, name="attn_fwd", grid=(kvh, lp // tq, nk),
        in_specs=[pl.BlockSpec((tq, rw), lambda g, i, j: (i, g)),
                  pl.BlockSpec((1, tk, HEAD_DIM), lambda g, i, j: (g, j, 0)),
                  pl.BlockSpec((1, tk, LANES), lambda g, i, j: (g, j, 0)),
                  pl.BlockSpec((1, tk), lambda g, i, j: (0, j))] + [_ANY] * ng,
        out_specs=[pl.BlockSpec((tq, rw), lambda g, i, j: (i, g)),
                   pl.BlockSpec((tq, LANES), lambda g, i, j: (i, g)),
                   pl.BlockSpec((GQA_REP, tq, tk), lambda g, i, j: (g, i, j)),
                   pl.BlockSpec((1, tq, LANES), lambda g, i, j: (j, i, g))] + [_ANY] * ng,
        out_shape=[jax.ShapeDtypeStruct((lp, d), BF16), jax.ShapeDtypeStruct((lp, kvh * LANES), F32),
                   jax.ShapeDtypeStruct((kvh * GQA_REP, lp, lp), BF16), jax.ShapeDtypeStruct((nk, lp, kvh * LANES), F32)]
        + [jax.ShapeDtypeStruct((8,) + b.shape, b.dtype) for b in gather],
        scratch_shapes=[pltpu.VMEM((GQA_REP, tq, LANES), F32), pltpu.VMEM((GQA_REP, tq, LANES), F32)]
        + (sems if ng else []),
        compiler_params=_params(("arbitrary", "arbitrary", "arbitrary")),
    )(q, k, v, kbias, *gather)
    return res[0], res[1], res[2], res[3], list(res[4:])


def _attn_bwd(q, k, v, pt, mblk, do, lse, delta, tq, tk):
    lp, d = q.shape
    kvh = k.shape[0]
    rw = GQA_REP * HEAD_DIM
    nq = lp // tq

    def body(q_ref, k_ref, v_ref, pt_ref, mb_ref, do_ref, lse_ref, dl_ref, dq_ref, dk_ref, dv_ref, dk_s, dv_s):
        j = pl.program_id(1)
        i = pl.program_id(2)

        @pl.when(jnp.logical_and(i == 0, j == 0))
        def _():
            dq_ref[...] = jnp.zeros(dq_ref.shape, F32)

        @pl.when(i == 0)
        def _():
            dk_s[...] = jnp.zeros(dk_s.shape, F32)
            dv_s[...] = jnp.zeros(dv_s.shape, F32)

        kk, vv = k_ref[0], v_ref[0][:, :HEAD_DIM]
        scale = jnp.exp(mb_ref[0] - lse_ref[...])
        dl = dl_ref[...] * scale
        dqs = []
        for pair in ((0, 1), (2, 3)):
            dos = {h: (do_ref[:, h * HEAD_DIM:(h + 1) * HEAD_DIM].astype(F32) * scale[:, h:h + 1]).astype(BF16)
                   for h in pair}
            dps = {h: _dot_nt(dos[h], vv) for h in pair}
            for h in pair:
                dv_s[...] += _dot_tn(pt_ref[h], dos[h])
            dss = {h: pt_ref[h] * (dps[h] - dl[:, h:h + 1]).astype(BF16) for h in pair}
            for h in pair:
                dk_s[...] += _dot_tn(dss[h], q_ref[:, h * HEAD_DIM:(h + 1) * HEAD_DIM])
                dqs.append(_dot(dss[h], kk))
        rows = pl.ds(pl.multiple_of(i * tq, tq), tq)
        dq_ref[rows, :] += jnp.concatenate(dqs, axis=1)

        @pl.when(i == nq - 1)
        def _():
            dk_ref[0] = dk_s[...]
            dv_ref[0] = dv_s[...]

    return pl.pallas_call(
        body, name="attn_bwd", grid=(kvh, lp // tk, nq),
        in_specs=[pl.BlockSpec((tq, rw), lambda g, j, i: (i, g)),
                  pl.BlockSpec((1, tk, HEAD_DIM), lambda g, j, i: (g, j, 0)),
                  pl.BlockSpec((1, tk, LANES), lambda g, j, i: (g, j, 0)),
                  pl.BlockSpec((GQA_REP, tq, tk), lambda g, j, i: (g, i, j)),
                  pl.BlockSpec((1, tq, LANES), lambda g, j, i: (j, i, g)),
                  pl.BlockSpec((tq, rw), lambda g, j, i: (i, g)),
                  pl.BlockSpec((tq, LANES), lambda g, j, i: (i, g)),
                  pl.BlockSpec((tq, LANES), lambda g, j, i: (i, g))],
        out_specs=[pl.BlockSpec((lp, rw), lambda g, j, i: (0, g)),
                   pl.BlockSpec((1, tk, HEAD_DIM), lambda g, j, i: (g, j, 0)),
                   pl.BlockSpec((1, tk, HEAD_DIM), lambda g, j, i: (g, j, 0))],
        out_shape=[jax.ShapeDtypeStruct((lp, d), F32), jax.ShapeDtypeStruct((kvh, lp, HEAD_DIM), F32),
                   jax.ShapeDtypeStruct((kvh, lp, HEAD_DIM), F32)],
        scratch_shapes=[pltpu.VMEM((tk, HEAD_DIM), F32), pltpu.VMEM((tk, HEAD_DIM), F32)],
        compiler_params=_params(("parallel", "arbitrary", "arbitrary")),
    )(q, k, v, pt, mblk, do, lse, delta)


def _ssm_math(a_re, a_im, log_dt, bt_re, bt_im):
    dt = jnp.exp(log_dt)
    lam_re = jnp.minimum(a_re, EIG_RE_MAX)
    lam_im = a_im
    mag = jnp.exp(lam_re * dt)
    ang = lam_im * dt
    lb_re = mag * jnp.cos(ang)
    lb_im = mag * jnp.sin(ang)
    num_re = lb_re - 1.0
    num_im = lb_im
    den = lam_re * lam_re + lam_im * lam_im
    f_re = (num_re * lam_re + num_im * lam_im) / den
    f_im = (num_im * lam_re - num_re * lam_im) / den
    bb_re = f_re[:, None, :] * bt_re - f_im[:, None, :] * bt_im
    bb_im = f_re[:, None, :] * bt_im + f_im[:, None, :] * bt_re
    return lb_re, lb_im, bb_re, bb_im


def _ssm_discretize(a_re, a_im, log_dt, bt_re, bt_im):
    nd, g, n = a_re.shape
    p = bt_re.shape[2]

    def body(ar_ref, ai_ref, ld_ref, br_ref, bi_ref, bbr_ref, bbi_ref, pr_ref, pi_ref, hr_ref, hi_ref):
        lb_re, lb_im, bb_re, bb_im = _ssm_math(ar_ref[0], ai_ref[0], ld_ref[0], br_ref[0], bi_ref[0])
        bbr_ref[0] = bb_re
        bbi_ref[0] = bb_im
        cr, ci = lb_re, lb_im
        for k in range(KSTEPS):
            pr_ref[0, k] = cr
            pi_ref[0, k] = ci
            if k < KSTEPS - 1:
                cr, ci = cr * lb_re - ci * lb_im, cr * lb_im + ci * lb_re
        for t in range(2):
            cr, ci = cr * cr - ci * ci, 2.0 * cr * ci
            hr_ref[0, t] = cr
            hi_ref[0, t] = ci

    s3 = pl.BlockSpec((1, g, n), lambda i: (i, 0, 0))
    s4 = pl.BlockSpec((1, g, p, n), lambda i: (i, 0, 0, 0))
    sp = pl.BlockSpec((1, KSTEPS, g, n), lambda i: (i, 0, 0, 0))
    sh = pl.BlockSpec((1, 2, g, n), lambda i: (i, 0, 0, 0))
    return pl.pallas_call(
        body, name="ssm_discretize", grid=(nd,),
        in_specs=[s3, s3, pl.BlockSpec((1, g, 1), lambda i: (i, 0, 0)), s4, s4],
        out_specs=[s4, s4, sp, sp, sh, sh],
        out_shape=[jax.ShapeDtypeStruct((nd, g, p, n), F32)] * 2 + [jax.ShapeDtypeStruct((nd, KSTEPS, g, n), F32)] * 2
        + [jax.ShapeDtypeStruct((nd, 2, g, n), F32)] * 2,
        compiler_params=_params(("parallel",)),
    )(a_re, a_im, log_dt, bt_re, bt_im)


def _ssm_param_bwd(a_re, a_im, log_dt, bt_re, bt_im, dlb_re, dlb_im, dbb_re, dbb_im):
    nd, g, n = a_re.shape
    p = bt_re.shape[2]

    def body(ar_ref, ai_ref, ld_ref, br_ref, bi_ref, c0_ref, c1_ref, c2_ref, c3_ref,
             o0_ref, o1_ref, o2_ref, o3_ref, o4_ref):
        _, vjp = jax.vjp(_ssm_math, ar_ref[0], ai_ref[0], ld_ref[0], br_ref[0], bi_ref[0])
        outs = vjp((c0_ref[0], c1_ref[0], c2_ref[0], c3_ref[0]))
        for ref, val in zip((o0_ref, o1_ref, o2_ref, o3_ref, o4_ref), outs):
            ref[0] = val

    s3 = pl.BlockSpec((1, g, n), lambda i: (i, 0, 0))
    s1 = pl.BlockSpec((1, g, 1), lambda i: (i, 0, 0))
    s4 = pl.BlockSpec((1, g, p, n), lambda i: (i, 0, 0, 0))
    return pl.pallas_call(
        body, name="ssm_param_bwd", grid=(nd,),
        in_specs=[s3, s3, s1, s4, s4, s3, s3, s4, s4],
        out_specs=[s3, s3, s1, s4, s4],
        out_shape=[jax.ShapeDtypeStruct((nd, g, n), F32)] * 2 + [jax.ShapeDtypeStruct((nd, g, 1), F32)]
        + [jax.ShapeDtypeStruct((nd, g, p, n), F32)] * 2,
        compiler_params=_params(("parallel",)),
    )(a_re, a_im, log_dt, bt_re, bt_im, dlb_re, dlb_im, dbb_re, dbb_im)


def _cmul(ar, ai, xr, xi, conj):
    if conj:
        return ar * xr + ai * xi, ar * xi - ai * xr
    return ar * xr - ai * xi, ar * xi + ai * xr


def _scan_chunk(buf, tab, carry, ein, nj, rev, conj, base=0):
    ks = list(range(KSTEPS))
    if rev:
        ks = ks[::-1]
    sub = lax.broadcasted_iota(jnp.int32, (SUBLANES, SCAN_LANES), 0)
    edge = sub == (SUBLANES - 1 if rev else 0)

    def step(j, _):
        jr, ji = j, nj + j
        ar, ai = tab[base, jr], tab[base, ji]
        hr = jnp.zeros((SUBLANES, SCAN_LANES), F32)
        hi = jnp.zeros((SUBLANES, SCAN_LANES), F32)
        for k in ks:
            rows = pl.ds(k * SUBLANES, SUBLANES)
            pr, pi_ = _cmul(ar, ai, hr, hi, conj)
            hr = pr + buf[jr, rows, :]
            hi = pi_ + buf[ji, rows, :]
            buf[jr, rows, :] = hr
            buf[ji, rows, :] = hi
        shift = SUBLANES - 1 if rev else 1
        er = jnp.where(edge, carry[jr], pltpu.roll(hr, shift, 0))
        ei = jnp.where(edge, carry[ji], pltpu.roll(hi, shift, 0))
        for t, dist in enumerate((1, 2, 4)):
            sh = SUBLANES - dist if rev else dist
            pr, pi_ = _cmul(tab[base + 1 + t, jr], tab[base + 1 + t, ji], pltpu.roll(er, sh, 0), pltpu.roll(ei, sh, 0), conj)
            er, ei = er + pr, ei + pi_
        ein[jr] = er
        ein[ji] = ei
        pr, pi_ = _cmul(tab[base + 4 + KSTEPS - 1, jr], tab[base + 4 + KSTEPS - 1, ji], er, ei, conj)
        last = 0 if rev else SUBLANES - 1
        carry[jr] = jnp.broadcast_to((hr + pr)[last:last + 1, :], (SUBLANES, SCAN_LANES))
        carry[ji] = jnp.broadcast_to((hi + pi_)[last:last + 1, :], (SUBLANES, SCAN_LANES))
        for n, k in enumerate(ks):
            rows = pl.ds(k * SUBLANES, SUBLANES)
            pr, pi_ = _cmul(tab[base + 4 + n, jr], tab[base + 4 + n, ji], er, ei, conj)
            buf[jr, rows, :] += pr
            buf[ji, rows, :] += pi_
        return 0

    lax.fori_loop(0, nj, step, 0)


def _state_lanes(b):
    per = SCAN_LANES // SSM_BLOCK
    return b // per, slice((b % per) * SSM_BLOCK, (b % per + 1) * SSM_BLOCK)


def _project_in(src, w_ref, buf, nj):
    nb, cb, _ = w_ref.shape
    for b in range(nb):
        res = _dot(src[:, b * cb:(b + 1) * cb], w_ref[b])
        j, lanes = _state_lanes(b)
        buf[j, :, lanes] = res[:, :SSM_BLOCK]
        buf[nj + j, :, lanes] = res[:, SSM_BLOCK:]


def _state_block(buf, b, nj):
    j, lanes = _state_lanes(b)
    return jnp.concatenate([buf[j, :, lanes], buf[nj + j, :, lanes]], axis=1).astype(BF16)


def _project_out(buf, w_ref, nj):
    return jnp.concatenate([_dot_nt(_state_block(buf, b, nj), w_ref[b]) for b in range(w_ref.shape[0])], axis=1)


def _ssm_fwd(u, wb, wct, tab, rev, name):
    lp, w = u.shape
    nb, cb, _ = wb.shape
    nj = nb * SSM_BLOCK // SCAN_LANES
    nc = lp // CHUNK
    ntab = tab.shape[0]
    cidx = (lambda c: nc - 1 - c) if rev else (lambda c: c)

    def body(u_ref, wb_ref, wct_ref, tab_ref, y_ref, ck_ref, buf, carry, ein):
        @pl.when(pl.program_id(0) == 0)
        def _():
            carry[...] = jnp.zeros(carry.shape, F32)

        _project_in(u_ref[...].astype(BF16), wb_ref, buf, nj)
        ck_ref[0] = carry[...]
        _scan_chunk(buf, tab_ref, carry, ein, nj, rev, False)
        y_ref[...] = _project_out(buf, wct_ref, nj)

    wshape = (nb, cb, 2 * SSM_BLOCK)
    return pl.pallas_call(
        body, name=name, grid=(nc,),
        in_specs=[pl.BlockSpec((CHUNK, w), lambda c: (cidx(c), 0)), _full(wshape), _full(wshape),
                  _full((ntab, 2 * nj, SUBLANES, SCAN_LANES))],
        out_specs=[pl.BlockSpec((CHUNK, w), lambda c: (cidx(c), 0)),
                   pl.BlockSpec((1, 2 * nj, SUBLANES, SCAN_LANES), lambda c: (cidx(c), 0, 0, 0))],
        out_shape=[jax.ShapeDtypeStruct((lp, w), F32), jax.ShapeDtypeStruct((nc, 2 * nj, SUBLANES, SCAN_LANES), F32)],
        scratch_shapes=[pltpu.VMEM((2 * nj, CHUNK, SCAN_LANES), F32), pltpu.VMEM((2 * nj, SUBLANES, SCAN_LANES), F32),
                        pltpu.VMEM((2 * nj, SUBLANES, SCAN_LANES), F32)],
        compiler_params=_params(("arbitrary",)),
    )(u, wb, wct, tab)


def _ssm_bwd(u, dy, ckpt, wb, wct, tab, rev, name, scatter=()):
    lp, w = u.shape
    nb, cb, _ = wb.shape
    nj = nb * SSM_BLOCK // SCAN_LANES
    nc = lp // CHUNK
    ntab = tab.shape[0]
    cidx = (lambda c: c) if rev else (lambda c: nc - 1 - c)

    ns = len(scatter)

    def body(*refs):
        u_ref, dy_ref, ck_ref, wb_ref, wct_ref, tab_hbm = refs[:6]
        du_ref, dbb_ref, dcc_ref, dlb_ref = refs[6 + ns:10 + ns]
        tab_ref, dwb_ref, dwc_ref, xs, ls, xcar, lcar, xin, lin = refs[10 + 2 * ns:19 + 2 * ns]
        c = pl.program_id(0)

        if ns:
            start, finish = _scatter_phases(refs[6:6 + ns], refs[10 + ns:10 + 2 * ns], *refs[19 + 2 * ns:])
            pl.when(c == 0)(start)
            pl.when(c == nc - 1)(finish)

        @pl.when(c == 0)
        def _():
            pltpu.sync_copy(tab_hbm, tab_ref)
            lcar[...] = jnp.zeros(lcar.shape, F32)
            dwb_ref[...] = jnp.zeros(dwb_ref.shape, F32)
            dwc_ref[...] = jnp.zeros(dwc_ref.shape, F32)
            dlb_ref[...] = jnp.zeros(dlb_ref.shape, F32)

        ub = u_ref[...].astype(BF16)
        dyb = dy_ref[...].astype(BF16)
        _project_in(ub, wb_ref, xs, nj)
        xcar[...] = ck_ref[0]
        _scan_chunk(xs, tab_ref, xcar, xin, nj, rev, False)
        _project_in(dyb, wct_ref, ls, nj)
        _scan_chunk(ls, tab_ref, lcar, lin, nj, not rev, True, base=ntab // 2)
        dus = []
        for b in range(nb):
            chans = slice(b * cb, (b + 1) * cb)
            xb = _state_block(xs, b, nj)
            lb = _state_block(ls, b, nj)
            dwc_ref[b] += _dot_tn(dyb[:, chans], xb)
            dwb_ref[b] += _dot_tn(ub[:, chans], lb)
            dus.append(_dot_nt(lb, wb_ref[b]))
        du_ref[...] = jnp.concatenate(dus, axis=1)

        def step(j, _):
            jr, ji = j, nj + j
            ar = jnp.zeros((SUBLANES, SCAN_LANES), F32)
            ai = jnp.zeros((SUBLANES, SCAN_LANES), F32)
            for k in range(KSTEPS):
                kp = k + 1 if rev else k - 1
                rows = pl.ds(k * SUBLANES, SUBLANES)
                if 0 <= kp < KSTEPS:
                    prow = pl.ds(kp * SUBLANES, SUBLANES)
                    xr, xi = xs[jr, prow, :], xs[ji, prow, :]
                else:
                    xr, xi = xin[jr], xin[ji]
                lr, li = ls[jr, rows, :], ls[ji, rows, :]
                ar += lr * xr + li * xi
                ai += li * xr - lr * xi
            dlb_ref[jr] += ar
            dlb_ref[ji] += ai
            return 0

        lax.fori_loop(0, nj, step, 0)

        @pl.when(c == nc - 1)
        def _():
            for b in range(2 * nj):
                dlb_ref[b] = jnp.broadcast_to(jnp.sum(dlb_ref[b], axis=0, keepdims=True), (SUBLANES, SCAN_LANES))
            for g in range(w // SSM_GROUP):
                b, gl = divmod(g, cb // SSM_GROUP)
                rows = slice(gl * SSM_GROUP, (gl + 1) * SSM_GROUP)
                for part in range(2):
                    cols = slice(part * SSM_BLOCK + gl * SSM_STATE, part * SSM_BLOCK + (gl + 1) * SSM_STATE)
                    dbb_ref[part, g * SSM_GROUP:(g + 1) * SSM_GROUP, :] = dwb_ref[b, rows, cols]
                    dcc_ref[part, g * SSM_GROUP:(g + 1) * SSM_GROUP, :] = dwc_ref[b, rows, cols]

    st = (2 * nj, SUBLANES, SCAN_LANES)
    wshape = (nb, cb, 2 * SSM_BLOCK)
    sems = [pltpu.SemaphoreType.DMA((ns, 3)), pltpu.SemaphoreType.DMA((ns, 3)), pltpu.SemaphoreType.DMA((ns,))]
    res = pl.pallas_call(
        body, name=name, grid=(nc,),
        in_specs=[pl.BlockSpec((CHUNK, w), lambda c: (cidx(c), 0)), pl.BlockSpec((CHUNK, w), lambda c: (cidx(c), 0)),
                  pl.BlockSpec((1,) + st, lambda c: (cidx(c), 0, 0, 0)), _full(wshape), _full(wshape), _ANY]
        + [_ANY] * ns,
        out_specs=[pl.BlockSpec((CHUNK, w), lambda c: (cidx(c), 0)), _full((2, w, SSM_STATE)),
                   _full((2, w, SSM_STATE)), _full(st)] + [_ANY] * ns,
        out_shape=[jax.ShapeDtypeStruct((lp, w), F32), jax.ShapeDtypeStruct((2, w, SSM_STATE), F32),
                   jax.ShapeDtypeStruct((2, w, SSM_STATE), F32), jax.ShapeDtypeStruct(st, F32)]
        + [jax.ShapeDtypeStruct(p.shape, p.dtype) for p in scatter],
        scratch_shapes=[pltpu.VMEM((ntab,) + st, F32), pltpu.VMEM(wshape, F32), pltpu.VMEM(wshape, F32),
                        pltpu.VMEM((2 * nj, CHUNK, SCAN_LANES), F32), pltpu.VMEM((2 * nj, CHUNK, SCAN_LANES), F32),
                        pltpu.VMEM(st, F32), pltpu.VMEM(st, F32), pltpu.VMEM(st, F32), pltpu.VMEM(st, F32)]
        + (sems if ns else []),
        compiler_params=_params(("arbitrary",)),
    )(u, dy, ckpt, wb, wct, tab, *scatter)
    return res[0], res[1], res[2], res[3], list(res[4:])


def _embed_blocks(t_re, t_im):
    g, p, n = t_re.shape
    gb = SSM_BLOCK // n
    eye = jnp.eye(gb, dtype=t_re.dtype)
    parts = [jnp.einsum('bgpn,gh->bgphn', t.reshape(g // gb, gb, p, n), eye).reshape(g // gb, gb * p, gb * n)
             for t in (t_re, t_im)]
    return jnp.concatenate(parts, axis=2)


def _scan_tables(pw_re, pw_im, hi_re, hi_im, rev):
    s = pw_re.shape[1] * pw_re.shape[2]
    nj = s // SCAN_LANES
    sub = np.arange(SUBLANES)
    live = np.ones((4 + KSTEPS, 1, SUBLANES, 1), bool)
    for row, dist in ((1, 1), (2, 2), (3, 4)):
        live[row, 0, :, 0] = (sub < SUBLANES - dist) if rev else (sub >= dist)

    def lay(pw, hi):
        rows = jnp.concatenate([pw[:1], pw[KSTEPS - 1:], hi, pw], axis=0).reshape(4 + KSTEPS, nj, 1, SCAN_LANES)
        return jnp.where(live, jnp.broadcast_to(rows, (4 + KSTEPS, nj, SUBLANES, SCAN_LANES)), 0.0)

    return jnp.concatenate([lay(pw_re, hi_re), lay(pw_im, hi_im)], axis=1)


def _adamw(w, g, m, v, tm):
    r, c = w.shape
    c1 = 1.0 - ADAM_B1 ** ADAM_STEP
    c2 = 1.0 - ADAM_B2 ** ADAM_STEP

    def body(w_ref, g_ref, m_ref, v_ref, d_ref, nm_ref, nv_ref):
        gg = g_ref[...]
        nm = ADAM_B1 * m_ref[...] + (1.0 - ADAM_B1) * gg
        nv = ADAM_B2 * v_ref[...] + (1.0 - ADAM_B2) * (gg * gg)
        nm_ref[...] = nm
        nv_ref[...] = nv
        d_ref[...] = -ADAM_LR * ((nm / c1) / (jnp.sqrt(nv / c2) + ADAM_EPS) + ADAM_WD * w_ref[...])

    spec = _row(tm, c)
    return pl.pallas_call(
        body, name="adamw", grid=(r // tm,), in_specs=[spec] * 4, out_specs=[spec] * 3,
        out_shape=[jax.ShapeDtypeStruct((r, c), F32)] * 3, compiler_params=_params(("parallel",)),
    )(w, g, m, v)


def _pair_sum(g42, got, core, out_dtype, tm, name):
    _, _, r, c = g42.shape

    def body(core_ref, a_ref, b_ref, o_ref):
        o_ref[...] = (a_ref[...] + b_ref[...]).astype(out_dtype)

    grid_spec = pltpu.PrefetchScalarGridSpec(
        num_scalar_prefetch=1, grid=(4, r // tm),
        in_specs=[pl.BlockSpec((1, None, tm, c), lambda s, i, core_ref: (s, core_ref[0], i, 0)),
                  pl.BlockSpec((1, tm, c), lambda s, i, core_ref: (s, i, 0))],
        out_specs=pl.BlockSpec((1, tm, c), lambda s, i, core_ref: (s, i, 0)))
    return pl.pallas_call(
        body, name=name, grid_spec=grid_spec, out_shape=jax.ShapeDtypeStruct((4, r, c), out_dtype),
        compiler_params=_params(("parallel", "parallel")),
    )(core, g42, got)


def _sum4(a, core, tm, name):
    _, r, c = a.shape

    def body(core_ref, a_ref, o_ref):
        o_ref[...] = ((a_ref[0].astype(F32) + a_ref[1].astype(F32)) + a_ref[2].astype(F32)) + a_ref[3].astype(F32)

    grid_spec = pltpu.PrefetchScalarGridSpec(
        num_scalar_prefetch=1, grid=(r // tm,),
        in_specs=[pl.BlockSpec((4, tm, c), lambda i, core_ref: (0, i, 0))],
        out_specs=pl.BlockSpec((None, tm, c), lambda i, core_ref: (core_ref[0], i, 0)))
    return pl.pallas_call(
        body, name=name, grid_spec=grid_spec, out_shape=jax.ShapeDtypeStruct((2, r, c), F32),
        compiler_params=_params(("parallel",)),
    )(core, a)


_ANY = pl.BlockSpec(memory_space=pl.ANY)


def _gather_phases(xs, outs, send_sems, recv_sems, local_sems):
    n = len(xs)

    def parts():
        x, y, c = lax.axis_index("x"), lax.axis_index("y"), lax.axis_index("c")
        return c, (x, y, c), (x, y, 1 - c), [(1 - x, y), (x, 1 - y), (1 - x, 1 - y)]

    def slot(t, px, py, pc):
        return outs[t].at[4 * px + 2 * py + pc]

    def copy(t, k, blk, to, src=None):
        return pltpu.make_async_remote_copy(
            src_ref=slot(t, *blk) if src is None else src, dst_ref=slot(t, *blk),
            send_sem=send_sems.at[t, k], recv_sem=recv_sems.at[t, k], device_id=to, device_id_type=MESH_ID)

    def own(t, me):
        return pltpu.make_async_copy(xs[t], slot(t, *me), local_sems.at[t])

    def first(t, c, me, sibling, chips):
        return [copy(t, 0, me, sibling, src=xs[t])] + [copy(t, 1 + j, me, (*chip, c), src=xs[t])
                                                       for j, chip in enumerate(chips)]

    def passed(t, c, sibling, chips):
        return [copy(t, 4 + j, (*chip, c), sibling) for j, chip in enumerate(chips)]

    def start():
        c, me, sibling, chips = parts()
        for t in range(n):
            own(t, me).start()
        for t in range(n):
            for cp in first(t, c, me, sibling, chips):
                cp.start()

    def forward():
        c, me, sibling, chips = parts()
        for j, chip in enumerate(chips):
            for t in range(n):
                copy(t, 1 + j, (*chip, c), me).wait_recv()
                passed(t, c, sibling, chips)[j].start()

    def finish():
        c, me, sibling, chips = parts()
        for t in range(n):
            copy(t, 0, sibling, me).wait_recv()
        for j, chip in enumerate(chips):
            for t in range(n):
                copy(t, 4 + j, (*chip, 1 - c), me).wait_recv()
        for t in range(n):
            for cp in first(t, c, me, sibling, chips) + passed(t, c, sibling, chips):
                cp.wait_send()
            own(t, me).wait()

    return start, forward, finish


def _all_gather8(blocks, name):
    n = len(blocks)

    def body(*refs):
        for phase in _gather_phases(refs[:n], refs[n:2 * n], *refs[2 * n:]):
            phase()

    return pl.pallas_call(
        body, name=name, out_shape=[jax.ShapeDtypeStruct((8,) + b.shape, b.dtype) for b in blocks],
        in_specs=[_ANY] * n, out_specs=[_ANY] * n,
        scratch_shapes=[pltpu.SemaphoreType.DMA((n, 7)), pltpu.SemaphoreType.DMA((n, 7)),
                        pltpu.SemaphoreType.DMA((n,))],
    )(*blocks)


def _pair_exchange(gs, name):
    n = len(gs)

    def body(*refs):
        g_refs, outs = refs[:n], refs[n:2 * n]
        send_sems, recv_sems = refs[2 * n:]
        x, y, c = lax.axis_index("x"), lax.axis_index("y"), lax.axis_index("c")
        cps = [pltpu.make_async_remote_copy(
            src_ref=g_refs[t].at[:, 1 - c], dst_ref=outs[t], send_sem=send_sems.at[t], recv_sem=recv_sems.at[t],
            device_id=(x, y, 1 - c), device_id_type=MESH_ID) for t in range(n)]
        for cp in cps:
            cp.start()
        for cp in cps:
            cp.wait()

    return pl.pallas_call(
        body, name=name,
        out_shape=[jax.ShapeDtypeStruct((g.shape[0],) + g.shape[2:], g.dtype) for g in gs],
        in_specs=[_ANY] * n, out_specs=[_ANY] * n,
        scratch_shapes=[pltpu.SemaphoreType.DMA((n,)), pltpu.SemaphoreType.DMA((n,))],
    )(*gs)


def _scatter_phases(p_refs, outs, send_sems, recv_sems, local_sems):
    n = len(p_refs)

    def parts():
        x, y, c = lax.axis_index("x"), lax.axis_index("y"), lax.axis_index("c")
        return c, 2 * x + y, [(1 - x, y), (x, 1 - y), (1 - x, 1 - y)]

    def copy(t, k, src_slab, dst_slab, chip, c):
        return pltpu.make_async_remote_copy(
            src_ref=p_refs[t].at[src_slab], dst_ref=outs[t].at[dst_slab], send_sem=send_sems.at[t, k],
            recv_sem=recv_sems.at[t, k], device_id=(*chip, c), device_id_type=MESH_ID)

    def own(t, mine):
        return pltpu.make_async_copy(p_refs[t].at[mine], outs[t].at[mine], local_sems.at[t])

    def start():
        c, mine, chips = parts()
        for t in range(n):
            own(t, mine).start()
        for k, (cx, cy) in enumerate(chips):
            for t in range(n):
                copy(t, k, 2 * cx + cy, mine, (cx, cy), c).start()

    def finish():
        c, mine, chips = parts()
        for k, (cx, cy) in enumerate(chips):
            for t in range(n):
                copy(t, k, mine, 2 * cx + cy, (cx, cy), c).wait_recv()
        for t in range(n):
            for k, (cx, cy) in enumerate(chips):
                copy(t, k, 2 * cx + cy, mine, (cx, cy), c).wait_send()
            own(t, mine).wait()

    return start, finish


def _chip_scatter(ps, name):
    n = len(ps)

    def body(*refs):
        for phase in _scatter_phases(refs[:n], refs[n:2 * n], *refs[2 * n:]):
            phase()

    return pl.pallas_call(
        body, name=name, out_shape=[jax.ShapeDtypeStruct(p.shape, p.dtype) for p in ps],
        in_specs=[_ANY] * n, out_specs=[_ANY] * n,
        scratch_shapes=[pltpu.SemaphoreType.DMA((n, 3)), pltpu.SemaphoreType.DMA((n, 3)),
                        pltpu.SemaphoreType.DMA((n,))],
    )(*ps)


def _pair_gather(rs, name):
    n = len(rs)

    def body(*refs):
        ins, outs = refs[:n], refs[n:2 * n]
        send_sems, recv_sems = refs[2 * n:]
        x, y, c = lax.axis_index("x"), lax.axis_index("y"), lax.axis_index("c")

        def copy(t, slab):
            return pltpu.make_async_remote_copy(
                src_ref=ins[t].at[slab], dst_ref=outs[t].at[slab], send_sem=send_sems.at[t],
                recv_sem=recv_sems.at[t], device_id=(x, y, 1 - c), device_id_type=MESH_ID)

        sends = [copy(t, c) for t in range(n)]
        for cp in sends:
            cp.start()
        for t in range(n):
            copy(t, 1 - c).wait_recv()
        for cp in sends:
            cp.wait_send()

    return pl.pallas_call(
        body, name=name, out_shape=[jax.ShapeDtypeStruct(r.shape, r.dtype) for r in rs],
        in_specs=[_ANY] * n, out_specs=[_ANY] * n, input_output_aliases={t: t for t in range(n)},
        scratch_shapes=[pltpu.SemaphoreType.DMA((n,)), pltpu.SemaphoreType.DMA((n,))],
    )(*rs)


PACK_COLS = 1024
BIG = (("meta_tokens", 1), ("w_in", 1), ("w_glu", 0), ("w_ssm_proj", 1), ("w_attn_proj", 0), ("w_out", 0),
       ("w_mlp_in", 1), ("w_mlp_out", 0))
SMALL = ("norm_mix_g", "ssm_a_re", "ssm_a_im", "ssm_log_dt", "ssm_b_re", "ssm_b_im", "ssm_c_re", "ssm_c_im",
         "ssm_d", "b_glu", "q_norm_g", "k_norm_g", "norm_mlp_g", "norm_final_g")


def _pad_rows(flat, mult_rows):
    n = flat.shape[0]
    unit = PACK_COLS * mult_rows
    total = -(-n // unit) * unit
    return jnp.pad(flat, (0, total - n)).reshape(total // PACK_COLS, PACK_COLS)


def _half(t, c):
    return lax.dynamic_slice_in_dim(t, c * (t.shape[0] // 2), t.shape[0] // 2, 0)


EARLY_WEIGHTS = ("meta_tokens", "w_in", "w_glu")
LATE_WEIGHTS = tuple(name for name, _ in BIG if name not in EARLY_WEIGHTS)


def _weight_blocks(shards, c, names):
    return [_half(shards[name], c) if name == "meta_tokens" else _half(shards[name], c).astype(BF16) for name in names]


def _shard_major(names, gathered):
    return {name: g.reshape((4, 2 * g.shape[1]) + g.shape[2:]) for name, g in zip(names, gathered)}


class _Reduction:
    def __init__(self, grads, wire, labels, c, tag):
        self.labels, self.wire, self.c, self.tag = labels, wire, c, tag
        self.core = c.astype(jnp.int32).reshape(1)
        g42 = [g.reshape(4, 2, g.shape[1] // 2, g.shape[2]) for g in grads]
        self.tiles = [_pick_tile(g.shape[2], 256, SUBLANES if dt == F32 else 2 * SUBLANES) for g, dt in zip(g42, wire)]
        got = _pair_exchange(g42, "grad_pair_exchange_" + tag)
        self.pair = [_pair_sum(g, o, self.core, dt, tm, "pair_sum_" + lb)
                     for g, o, dt, tm, lb in zip(g42, got, wire, self.tiles, labels)]

    def finish(self, by_src, gathered):
        red = [_sum4(b, self.core, tm, "chip_sum_" + lb) for b, tm, lb in zip(by_src, self.tiles, self.labels)]
        both = _pair_gather(red[:gathered], "grad_pair_gather_" + self.tag)
        pieces = [lax.dynamic_index_in_dim(r, self.c, 0, keepdims=False) for r in red[gathered:]]
        return [b.reshape(2 * b.shape[1], b.shape[2]) for b in both], pieces


def _small_as_shards(small_flat):
    unit = 8 * SUBLANES * PACK_COLS
    k = -(-small_flat.shape[0] // unit) * unit
    return jnp.pad(small_flat, (0, k - small_flat.shape[0])).reshape(4, k // (4 * PACK_COLS), PACK_COLS)


def _to_chunk_order(a):
    lp = a.shape[0]
    rest = a.shape[1:]
    a = a.reshape((lp // CHUNK, SUBLANES, KSTEPS) + rest)
    return a.swapaxes(1, 2).reshape((lp,) + rest)


def _from_chunk_order(a):
    lp = a.shape[0]
    rest = a.shape[1:]
    a = a.reshape((lp // CHUNK, KSTEPS, SUBLANES) + rest)
    return a.swapaxes(1, 2).reshape((lp,) + rest)


def _rope_tables(l_total, lp):
    n_real = l_total - N_META
    pos = np.arange(n_real)
    row_id = (pos // GRID_W).astype(np.float32)
    col_id = (pos % GRID_W).astype(np.float32)
    ppa = HEAD_DIM // 4
    inv_freq = (ROPE_THETA ** (-np.arange(ppa, dtype=np.float64) / ppa)).astype(np.float32)
    ang = np.concatenate([row_id[:, None] * inv_freq, col_id[:, None] * inv_freq], axis=-1)
    ang = np.concatenate([np.zeros((N_META, HEAD_DIM // 2), np.float32), ang,
                          np.zeros((lp - l_total, HEAD_DIM // 2), np.float32)], axis=0).astype(np.float64)
    cos = np.repeat(np.cos(ang), 2, axis=1)
    sin = np.repeat(np.sin(ang), 2, axis=1) * np.tile(np.asarray([-1.0, 1.0]), HEAD_DIM // 2)
    reps = (1, LANES // HEAD_DIM)
    return np.tile(cos, reps).astype(np.float32), np.tile(sin, reps).astype(np.float32)


def kernel(x, meta_tokens, norm_mix_g, w_in, ssm_a_re, ssm_a_im, ssm_log_dt, ssm_b_re, ssm_b_im, ssm_c_re, ssm_c_im, ssm_d, w_glu, b_glu, q_norm_g, k_norm_g, w_ssm_proj, w_attn_proj, w_out, norm_mlp_g, w_mlp_in, w_mlp_out, norm_final_g, loss_target, m_meta_tokens, m_norm_mix_g, m_w_in, m_ssm_a_re, m_ssm_a_im, m_ssm_log_dt, m_ssm_b_re, m_ssm_b_im, m_ssm_c_re, m_ssm_c_im, m_ssm_d, m_w_glu, m_b_glu, m_q_norm_g, m_k_norm_g, m_w_ssm_proj, m_w_attn_proj, m_w_out, m_norm_mlp_g, m_w_mlp_in, m_w_mlp_out, m_norm_final_g, v_meta_tokens, v_norm_mix_g, v_w_in, v_ssm_a_re, v_ssm_a_im, v_ssm_log_dt, v_ssm_b_re, v_ssm_b_im, v_ssm_c_re, v_ssm_c_im, v_ssm_d, v_w_glu, v_b_glu, v_q_norm_g, v_k_norm_g, v_w_ssm_proj, v_w_attn_proj, v_w_out, v_norm_mlp_g, v_w_mlp_in, v_w_mlp_out, v_norm_final_g):
    args = dict(locals())
    names = list(dict.fromkeys([n for n, _ in BIG] + list(SMALL)))
    order = ['meta_tokens', 'norm_mix_g', 'w_in', 'ssm_a_re', 'ssm_a_im', 'ssm_log_dt', 'ssm_b_re', 'ssm_b_im',
             'ssm_c_re', 'ssm_c_im', 'ssm_d', 'w_glu', 'b_glu', 'q_norm_g', 'k_norm_g', 'w_ssm_proj', 'w_attn_proj',
             'w_out', 'norm_mlp_g', 'w_mlp_in', 'w_mlp_out', 'norm_final_g']
    assert sorted(names) == sorted(order)
    c_idx = lax.axis_index("c")

    seq, d = x.shape[1], x.shape[2]
    l_total = seq + N_META
    lp = -(-l_total // SEQ_ALIGN) * SEQ_ALIGN
    hd = d // 2
    n_groups = hd // SSM_GROUP
    n_state = n_groups * SSM_STATE
    nj = n_state // SCAN_LANES
    kvh = d // HEAD_DIM // GQA_REP

    shard2d = {}
    for name, _ in BIG:
        t = args[name]
        shard2d[name] = t.reshape(t.shape[-2], t.shape[-1])
    full = _shard_major(EARLY_WEIGHTS, _all_gather8(_weight_blocks(shard2d, c_idx, EARLY_WEIGHTS), "weight_all_gather"))
    meta_full = jnp.transpose(full["meta_tokens"], (1, 0, 2)).reshape(N_META, d)
    w_in4 = full["w_in"]
    w_glu_f = full["w_glu"].reshape(hd, hd)

    xin = jnp.concatenate([meta_full, x[0], jnp.zeros((lp - l_total, d), F32)], axis=0)
    xin = _to_chunk_order(xin)
    tgt = _to_chunk_order(jnp.pad(loss_target[0], ((N_META, lp - l_total), (0, 0))))
    pos = np.arange(lp)
    rowmask = jnp.asarray(_to_chunk_order(((pos >= N_META) & (pos < l_total)).astype(np.float32)[:, None]))
    kbias = jnp.asarray(_to_chunk_order(np.where(pos < l_total, 0.0, MASK_VALUE).astype(np.float32)[:, None])
                        .reshape(1, lp))
    cos_t, sin_t = (jnp.asarray(_to_chunk_order(t)) for t in _rope_tables(l_total, lp))
    mean_m, sel = _head_tables(d)

    tm = _pick_tile(lp, 320)
    tm_mid = _pick_tile(lp, 384)
    tm_big = _pick_tile(lp, 640)
    tq = _pick_tile(lp, ATTN_Q_TILE, LANES)
    tk = _pick_tile(lp, ATTN_K_TILE, MXU_DIM)
    assert lp - CHUNK <= (l_total // CHUNK) * CHUNK and tk >= CHUNK
    g_mix = norm_mix_g.reshape(1, d)
    g_mlp = norm_mlp_g.reshape(1, d)
    g_fin = norm_final_g.reshape(1, d)
    qg = jnp.tile(q_norm_g.reshape(1, HEAD_DIM), (1, LANES // HEAD_DIM))
    kg = jnp.tile(k_norm_g.reshape(1, HEAD_DIM), (1, LANES // HEAD_DIM))
    dskip = ssm_d.reshape(1, hd)
    bglu = b_glu.reshape(1, hd)

    a_re, a_im = ssm_a_re[0], ssm_a_im[0]
    log_dt = ssm_log_dt[0][..., None]
    bt_re = jnp.swapaxes(ssm_b_re[0], 2, 3)
    bt_im = jnp.swapaxes(ssm_b_im[0], 2, 3)
    bb_re, bb_im, pw_re, pw_im, hi_re, hi_im = _ssm_discretize(a_re, a_im, log_dt, bt_re, bt_im)
    wb = [_embed_blocks(bb_re[i], bb_im[i]).astype(BF16) for i in range(2)]
    wct = [_embed_blocks(ssm_c_re[0, i], -ssm_c_im[0, i]).astype(BF16) for i in range(2)]
    tabs = [_scan_tables(pw_re[i], pw_im[i], hi_re[i], hi_im[i], rev=(i == 1)) for i in range(2)]
    tabs_adj = [_scan_tables(pw_re[i], pw_im[i], hi_re[i], hi_im[i], rev=(i == 0)) for i in range(2)]

    u, qkv, gates, hb = _in_proj(xin, g_mix, w_in4, tm_mid)
    y0, ck0 = _ssm_fwd(u, wb[0], wct[0], tabs[0], False, "ssm_fwd_0")
    y1, ck1 = _ssm_fwd(u, wb[1], wct[1], tabs[1], True, "ssm_fwd_1")
    yssm = _glu_fwd(u, y0, y1, dskip, w_glu_f, bglu, tm_big)
    q, k, v = _qk_prep(qkv, cos_t, sin_t, qg, kg, mean_m, tm)
    o, lse, pt, mblk, late = _attn_fwd(q, k, v, kbias, tq, tk, gather=_weight_blocks(shard2d, c_idx, LATE_WEIGHTS))
    full = _shard_major(LATE_WEIGHTS, late)
    w_mlp_in4 = full["w_mlp_in"]
    w_ssm_proj4 = full["w_ssm_proj"]
    w_attn_proj_f = full["w_attn_proj"].reshape(d, d)
    w_out_f = full["w_out"].reshape(d, d)
    w_mlp_out_f = full["w_mlp_out"].reshape(4 * d, d)
    h1, merged = _merge_fwd(yssm, o, gates, xin, w_ssm_proj4, w_attn_proj_f, w_out_f, tm_mid)
    r, h2b = _mlp_in(h1, g_mlp, w_mlp_in4, tm_mid)
    loss_tile, dh3, d_gfin, dz, dh3b = _mlp_out_loss(h1, r, w_mlp_out_f, g_fin, tgt, rowmask, tm_mid)

    dh1, d_gmlp = _mlp_bwd_b(dz, dh3, h1, g_mlp, w_mlp_in4, tm_mid)
    dgates, dms, dma, dyssm, do, delta, dh1b = _merge_bwd(dh1, yssm, o, gates, w_ssm_proj4, w_attn_proj_f, w_out_f,
                                                          sel, tm_mid)
    dyv, d_wglu, d_bglu, d_dskip = _glu_bwd(dyssm, u, y0, y1, dskip, w_glu_f, bglu, tm_big)

    tn = min(d, 1024)
    tm_w = _pick_tile(lp, 3 * MXU_DIM, MXU_DIM)
    grads4 = {
        "w_mlp_in": _wgrad(h2b, dz, 4, tm_w, tn, "wgrad_mlp_in"),
        "w_mlp_out": _wgrad(r, dh3b, 1, tm_w, min(d, 512), "wgrad_mlp_out", square=True).reshape(4, d, d),
        "w_out": _wgrad(merged, dh1b, 1, tm_w, tn, "wgrad_out").reshape(4, d // 4, d),
        "w_attn_proj": _wgrad(o, dma, 1, tm_w, tn, "wgrad_attn_proj").reshape(4, d // 4, d),
        "w_ssm_proj": _wgrad(yssm, dms, 4, tm_w, d // 4, "wgrad_ssm_proj"),
        "w_glu": d_wglu.reshape(4, hd // 4, hd),
    }
    first_names = list(grads4)
    first = _Reduction([grads4[n] for n in first_names], [BF16] * len(first_names), first_names, c_idx, "first")

    du0, dbb0, dcc0, dlb0, first_by_src = _ssm_bwd(u, dyv, ck0, wb[0], wct[0], _both(tabs[0], tabs_adj[0]), False,
                                                   "ssm_bwd_0", scatter=first.pair)
    du1, dbb1, dcc1, dlb1, _ = _ssm_bwd(u, dyv, ck1, wb[1], wct[1], _both(tabs[1], tabs_adj[1]), True, "ssm_bwd_1")
    dq, dk, dv = _attn_bwd(q, k, v, pt, mblk, do, lse, delta, _pick_tile(lp, MXU_DIM, LANES), tk)
    dqkv, d_qg, d_kg = _qk_bwd(qkv, dq, dk, dv, cos_t, sin_t, qg, kg, mean_m, tm)
    dxin, d_gmix, dproj = _in_proj_bwd(dyv, du0, du1, dskip, dqkv, dgates, dh1, xin, g_mix, w_in4, tm)
    red_big = dict(zip(first_names, first.finish(first_by_src, len(first_names))[0]))

    grads4["w_in"] = _wgrad(hb, dproj, 4, tm_w, tn, "wgrad_in")
    dx_nat = _from_chunk_order(dxin)
    grads4["meta_tokens"] = jnp.swapaxes(dx_nat[:N_META].reshape(N_META, 4, d // 4), 0, 1)
    grad_x = dx_nat[N_META:l_total][None]

    dlb = jnp.stack([dlb0, dlb1])[:, :, 0, :]
    dlb_re = dlb[:, :nj].reshape(2, n_groups, SSM_STATE)
    dlb_im = dlb[:, nj:].reshape(2, n_groups, SSM_STATE)
    gpn = (2, 2, n_groups, SSM_GROUP, SSM_STATE)
    dbb = jnp.stack([dbb0, dbb1]).reshape(gpn)
    dcc = jnp.stack([dcc0, dcc1]).reshape(gpn)
    d_are, d_aim, d_logdt, d_btre, d_btim = _ssm_param_bwd(a_re, a_im, log_dt, bt_re, bt_im, dlb_re, dlb_im,
                                                           dbb[:, 0], dbb[:, 1])
    small_grads = {
        "norm_mix_g": d_gmix, "ssm_a_re": d_are, "ssm_a_im": d_aim, "ssm_log_dt": d_logdt,
        "ssm_b_re": jnp.swapaxes(d_btre, 2, 3), "ssm_b_im": jnp.swapaxes(d_btim, 2, 3),
        "ssm_c_re": dcc[:, 0], "ssm_c_im": -dcc[:, 1],
        "ssm_d": d_dskip, "b_glu": d_bglu, "q_norm_g": d_qg[:, :HEAD_DIM], "k_norm_g": d_kg[:, :HEAD_DIM],
        "norm_mlp_g": d_gmlp, "norm_final_g": d_gfin,
    }
    small_flat = jnp.concatenate([small_grads[n].reshape(-1) for n in SMALL] + [loss_tile[0, :1]])

    last = _Reduction([grads4["meta_tokens"], grads4["w_in"], _small_as_shards(small_flat)], [F32, BF16, F32],
                      ["meta_tokens", "w_in", "small"], c_idx, "last")
    (red_big["meta_tokens"], red_big["w_in"]), (small_piece,) = last.finish(
        _chip_scatter(last.pair, "grad_chip_scatter"), 2)
    red_small = _all_gather8([small_piece], "small_grad_all_gather")[0].reshape(-1)[:small_flat.shape[0]]
    loss, red_small = red_small[-1], red_small[:-1]
    grad, delta_w, new_m, new_v = {}, {}, {}, {}
    for name, _ in BIG:
        w2 = shard2d[name]
        shp = args[name].shape
        g2 = red_big[name]
        t = _pick_tile(w2.shape[0], 256, 8)
        dl, nm, nv = _adamw(w2, g2, args["m_" + name].reshape(w2.shape), args["v_" + name].reshape(w2.shape), t)
        grad[name], delta_w[name], new_m[name], new_v[name] = (a.reshape(shp) for a in (g2, dl, nm, nv))

    def pack_small(prefix):
        flat = jnp.concatenate([args[prefix + n].reshape(-1) for n in SMALL])
        return _pad_rows(flat, SUBLANES)

    n_small = red_small.shape[0]
    gs = _pad_rows(red_small, SUBLANES)
    dl, nm, nv = _adamw(pack_small(""), gs, pack_small("m_"), pack_small("v_"), _pick_tile(gs.shape[0], 256, 8))
    off = 0
    for name in SMALL:
        shp = args[name].shape
        k = int(np.prod(shp))
        for dst, src in ((grad, gs), (delta_w, dl), (new_m, nm), (new_v, nv)):
            dst[name] = src.reshape(-1)[off:off + k].reshape(shp)
        off += k
    assert off == n_small

    return (loss, grad_x, *[grad[n] for n in order], *[delta_w[n] for n in order],
            *[new_m[n] for n in order], *[new_v[n] for n in order])


def _both(tab, tab_adj):
    return jnp.concatenate([tab, tab_adj], axis=0)
```
